```python
import math
import jax, jax.numpy as jnp
from jax import lax
import numpy as np

D_MODEL = 1024
BATCH = 8
SEQ = 8192
DEPTH = 2

SSM_WIDTH = 512
SSM_GROUP = 16
SSM_GROUPS = SSM_WIDTH // SSM_GROUP
SSM_STATE = 64
LOG_DT_MIN = math.log(1e-3)
LOG_DT_MAX = math.log(1e-1)
ATTN_HEADS = 8
HEAD_DIM = 64
ATTN_WIDTH = ATTN_HEADS * HEAD_DIM
Q_BLOCK = 128
D_FF = 2816
FFN_RES = 0.5
N_SUB = 3
RMS_EPS = 1e-6
IN_WIDTH = SSM_WIDTH + 3 * ATTN_WIDTH + ATTN_HEADS + 2 * D_MODEL

kernel_name = "hybrid_s5_fox_macaron_adaln"


def rms_norm(x, g):
    xf = x.astype(jnp.float32)
    xf = xf * lax.rsqrt(jnp.mean(xf * xf, axis=-1, keepdims=True) + RMS_EPS)
    return (xf * g.astype(jnp.float32)).astype(x.dtype)


def swiglu(h, w_in, w_out):
    gate, up = jnp.split(h @ w_in, 2, axis=-1)
    return (jax.nn.silu(gate) * up) @ w_out


def s5_ssm(u, a_re, a_im, log_dt, b_re, b_im, c_re, c_im, d_skip):
    f32 = jnp.float32
    bsz, L, _ = u.shape
    uf = u.astype(f32).reshape(bsz, L, SSM_GROUPS, SSM_GROUP)
    lam = lax.complex(jnp.minimum(a_re.astype(f32), -1e-4), a_im.astype(f32))
    dt = jnp.exp(log_dt.astype(f32))[:, None]
    lam_bar = jnp.exp(lam * dt)
    b = lax.complex(b_re.astype(f32), b_im.astype(f32))
    b_bar = ((lam_bar - 1.0) / lam)[..., None] * b
    bu = jnp.einsum('blgn,gpn->blgp', uf.astype(jnp.complex64), b_bar)
    a_all = jnp.broadcast_to(lam_bar, bu.shape)

    def combine(e1, e2):
        a1, s1 = e1
        a2, s2 = e2
        return a2 * a1, a2 * s1 + s2

    _, states = lax.associative_scan(combine, (a_all, bu), axis=1)
    c = lax.complex(c_re.astype(f32), c_im.astype(f32))
    y = jnp.real(jnp.einsum('gnp,blgp->blgn', c, states))
    y = y + d_skip.astype(f32).reshape(SSM_GROUPS, SSM_GROUP) * uf
    return y.reshape(bsz, L, SSM_WIDTH).astype(u.dtype)


def forgetting_attention(q, k, v, f_logit):
    f32 = jnp.float32
    bsz, L, H, Dh = q.shape
    nb = L // Q_BLOCK
    log_f = jax.nn.log_sigmoid(f_logit.astype(f32))
    cum = jnp.cumsum(log_f, axis=1).transpose(0, 2, 1)
    kt = k.transpose(0, 2, 1, 3)
    vt = v.transpose(0, 2, 1, 3)
    qb = q.transpose(0, 2, 1, 3).reshape(bsz, H, nb, Q_BLOCK, Dh).transpose(2, 0, 1, 3, 4)
    cqb = cum.reshape(bsz, H, nb, Q_BLOCK).transpose(2, 0, 1, 3)
    starts = jnp.arange(nb, dtype=jnp.int32) * Q_BLOCK
    k_pos = jnp.arange(L, dtype=jnp.int32)
    scale = Dh ** -0.5

    def one_block(args):
        q_blk, cq_blk, start = args
        s = jnp.einsum('bhqd,bhkd->bhqk', q_blk, kt).astype(f32) * scale
        s = s + (cq_blk[..., None] - cum[:, :, None, :])
        q_pos = start + jnp.arange(Q_BLOCK, dtype=jnp.int32)
        s = jnp.where(k_pos[None, :] <= q_pos[:, None], s, -jnp.inf)
        p = jax.nn.softmax(s, axis=-1).astype(vt.dtype)
        return jnp.einsum('bhqk,bhkd->bhqd', p, vt)

    out = lax.map(one_block, (qb, cqb, starts))
    return out.transpose(1, 0, 3, 2, 4).reshape(bsz, L, H * Dh)


def token_mixer(h, w_in, forget_b, a_re, a_im, log_dt, b_re, b_im, c_re, c_im,
                d_skip, glu_w, attn_w_out, w_out):
    bsz, L, _ = h.shape
    proj = h @ w_in
    cuts = np.cumsum([SSM_WIDTH, ATTN_WIDTH, ATTN_WIDTH, ATTN_WIDTH, ATTN_HEADS, D_MODEL]).tolist()
    u, q, k, v, f, g_a, g_b = jnp.split(proj, cuts, axis=-1)
    y_ssm = s5_ssm(u, a_re, a_im, log_dt, b_re, b_im, c_re, c_im, d_skip)
    z_val, z_gate = jnp.split(jax.nn.gelu(y_ssm) @ glu_w, 2, axis=-1)
    y_a = z_val * jax.nn.sigmoid(z_gate)
    shp = (bsz, L, ATTN_HEADS, HEAD_DIM)
    attn = forgetting_attention(q.reshape(shp), k.reshape(shp), v.reshape(shp), f + forget_b)
    y_b = attn @ attn_w_out
    merged = jax.nn.sigmoid(g_a) * y_a + jax.nn.sigmoid(g_b) * y_b
    return merged @ w_out


def _fwd_setup_inputs(seed: int = 0) -> dict:
    key = jax.random.key(seed)
    ks = jax.random.split(key, 24)
    f32 = jnp.float32
    nrm = lambda k, shape, s: jax.random.normal(k, shape, f32) * s
    D, G, P, N = D_MODEL, SSM_GROUPS, SSM_STATE, SSM_GROUP
    a_im_init = jnp.pi * jnp.arange(P, dtype=f32)
    return {
        "x": nrm(ks[0], (BATCH, SEQ, D), 1.0),
        "c": nrm(ks[1], (BATCH, D), 1.0),
        "mod_w": nrm(ks[2], (DEPTH, D, N_SUB * 3 * D), 0.5 * D ** -0.5),
        "mod_b": nrm(ks[3], (DEPTH, N_SUB * 3 * D), 0.01),
        "norm_pre": 1.0 + nrm(ks[4], (DEPTH, N_SUB, D), 0.02),
        "norm_post": 1.0 + nrm(ks[5], (DEPTH, N_SUB, D), 0.02),
        "ffn_w_in": nrm(ks[6], (DEPTH, 2, D, 2 * D_FF), D ** -0.5),
        "ffn_w_out": nrm(ks[7], (DEPTH, 2, D_FF, D), D_FF ** -0.5),
        "mix_w_in": nrm(ks[8], (DEPTH, D, IN_WIDTH), D ** -0.5),
        "forget_b": 3.0 + nrm(ks[9], (DEPTH, ATTN_HEADS), 0.5),
        "ssm_a_re": -0.5 + nrm(ks[10], (DEPTH, G, P), 0.01),
        "ssm_a_im": a_im_init + nrm(ks[11], (DEPTH, G, P), 0.01),
        "ssm_log_dt": jax.random.uniform(ks[12], (DEPTH, G), f32, LOG_DT_MIN, LOG_DT_MAX),
        "ssm_b_re": nrm(ks[13], (DEPTH, G, P, N), (2 * N) ** -0.5),
        "ssm_b_im": nrm(ks[14], (DEPTH, G, P, N), (2 * N) ** -0.5),
        "ssm_c_re": nrm(ks[15], (DEPTH, G, N, P), (2 * P) ** -0.5),
        "ssm_c_im": nrm(ks[16], (DEPTH, G, N, P), (2 * P) ** -0.5),
        "ssm_d": nrm(ks[17], (DEPTH, SSM_WIDTH), 1.0),
        "glu_w": nrm(ks[18], (DEPTH, SSM_WIDTH, 2 * D), SSM_WIDTH ** -0.5),
        "attn_w_out": nrm(ks[19], (DEPTH, ATTN_WIDTH, D), ATTN_WIDTH ** -0.5),
        "mix_w_out": nrm(ks[20], (DEPTH, D, D), D ** -0.5),
    }


def _fwd_reference(x, c, mod_w, mod_b, norm_pre, norm_post, ffn_w_in, ffn_w_out,
              mix_w_in, forget_b, ssm_a_re, ssm_a_im, ssm_log_dt, ssm_b_re,
              ssm_b_im, ssm_c_re, ssm_c_im, ssm_d, glu_w, attn_w_out, mix_w_out):
    bsz = x.shape[0]
    for l in range(DEPTH):
        mod = (jax.nn.silu(c) @ mod_w[l] + mod_b[l]).reshape(bsz, N_SUB, 3, D_MODEL)
        mod = mod[:, :, :, None, :]

        def pre(x_in, i):
            return rms_norm(x_in, norm_pre[l, i]) * (1.0 + mod[:, i, 1]) + mod[:, i, 0]

        def post_add(x_in, y, i, res_w):
            return x_in + res_w * mod[:, i, 2] * rms_norm(y, norm_post[l, i])

        x = post_add(x, swiglu(pre(x, 0), ffn_w_in[l, 0], ffn_w_out[l, 0]), 0, FFN_RES)
        y = token_mixer(pre(x, 1), mix_w_in[l], forget_b[l], ssm_a_re[l], ssm_a_im[l],
                        ssm_log_dt[l], ssm_b_re[l], ssm_b_im[l], ssm_c_re[l], ssm_c_im[l],
                        ssm_d[l], glu_w[l], attn_w_out[l], mix_w_out[l])
        x = post_add(x, y, 1, 1.0)
        x = post_add(x, swiglu(pre(x, 2), ffn_w_in[l, 1], ffn_w_out[l, 1]), 2, FFN_RES)
    return x


import jax as _jax
import jax.numpy as _jnp

TWIN_FORMAT = 'train_step'
FWD_PARAMS = ['x', 'c', 'mod_w', 'mod_b', 'norm_pre', 'norm_post', 'ffn_w_in', 'ffn_w_out', 'mix_w_in', 'forget_b', 'ssm_a_re', 'ssm_a_im', 'ssm_log_dt', 'ssm_b_re', 'ssm_b_im', 'ssm_c_re', 'ssm_c_im', 'ssm_d', 'glu_w', 'attn_w_out', 'mix_w_out']
TWIN_WEIGHTS = ['mod_w', 'mod_b', 'norm_pre', 'norm_post', 'ffn_w_in', 'ffn_w_out', 'mix_w_in', 'forget_b', 'ssm_a_re', 'ssm_a_im', 'ssm_log_dt', 'ssm_b_re', 'ssm_b_im', 'ssm_c_re', 'ssm_c_im', 'ssm_d', 'glu_w', 'attn_w_out', 'mix_w_out']
TWIN_DIFF_INPUT = 'x'
TWIN_INPUTS = ['x', 'c', 'mod_w', 'mod_b', 'norm_pre', 'norm_post', 'ffn_w_in', 'ffn_w_out', 'mix_w_in', 'forget_b', 'ssm_a_re', 'ssm_a_im', 'ssm_log_dt', 'ssm_b_re', 'ssm_b_im', 'ssm_c_re', 'ssm_c_im', 'ssm_d', 'glu_w', 'attn_w_out', 'mix_w_out', 'loss_target', 'm_mod_w', 'm_mod_b', 'm_norm_pre', 'm_norm_post', 'm_ffn_w_in', 'm_ffn_w_out', 'm_mix_w_in', 'm_forget_b', 'm_ssm_a_re', 'm_ssm_a_im', 'm_ssm_log_dt', 'm_ssm_b_re', 'm_ssm_b_im', 'm_ssm_c_re', 'm_ssm_c_im', 'm_ssm_d', 'm_glu_w', 'm_attn_w_out', 'm_mix_w_out', 'v_mod_w', 'v_mod_b', 'v_norm_pre', 'v_norm_post', 'v_ffn_w_in', 'v_ffn_w_out', 'v_mix_w_in', 'v_forget_b', 'v_ssm_a_re', 'v_ssm_a_im', 'v_ssm_log_dt', 'v_ssm_b_re', 'v_ssm_b_im', 'v_ssm_c_re', 'v_ssm_c_im', 'v_ssm_d', 'v_glu_w', 'v_attn_w_out', 'v_mix_w_out']
TWIN_OUTPUTS = ['loss', 'grad_x', 'grad_mod_w', 'grad_mod_b', 'grad_norm_pre', 'grad_norm_post', 'grad_ffn_w_in', 'grad_ffn_w_out', 'grad_mix_w_in', 'grad_forget_b', 'grad_ssm_a_re', 'grad_ssm_a_im', 'grad_ssm_log_dt', 'grad_ssm_b_re', 'grad_ssm_b_im', 'grad_ssm_c_re', 'grad_ssm_c_im', 'grad_ssm_d', 'grad_glu_w', 'grad_attn_w_out', 'grad_mix_w_out', 'delta_mod_w', 'delta_mod_b', 'delta_norm_pre', 'delta_norm_post', 'delta_ffn_w_in', 'delta_ffn_w_out', 'delta_mix_w_in', 'delta_forget_b', 'delta_ssm_a_re', 'delta_ssm_a_im', 'delta_ssm_log_dt', 'delta_ssm_b_re', 'delta_ssm_b_im', 'delta_ssm_c_re', 'delta_ssm_c_im', 'delta_ssm_d', 'delta_glu_w', 'delta_attn_w_out', 'delta_mix_w_out', 'new_m_mod_w', 'new_m_mod_b', 'new_m_norm_pre', 'new_m_norm_post', 'new_m_ffn_w_in', 'new_m_ffn_w_out', 'new_m_mix_w_in', 'new_m_forget_b', 'new_m_ssm_a_re', 'new_m_ssm_a_im', 'new_m_ssm_log_dt', 'new_m_ssm_b_re', 'new_m_ssm_b_im', 'new_m_ssm_c_re', 'new_m_ssm_c_im', 'new_m_ssm_d', 'new_m_glu_w', 'new_m_attn_w_out', 'new_m_mix_w_out', 'new_v_mod_w', 'new_v_mod_b', 'new_v_norm_pre', 'new_v_norm_post', 'new_v_ffn_w_in', 'new_v_ffn_w_out', 'new_v_mix_w_in', 'new_v_forget_b', 'new_v_ssm_a_re', 'new_v_ssm_a_im', 'new_v_ssm_log_dt', 'new_v_ssm_b_re', 'new_v_ssm_b_im', 'new_v_ssm_c_re', 'new_v_ssm_c_im', 'new_v_ssm_d', 'new_v_glu_w', 'new_v_attn_w_out', 'new_v_mix_w_out']
TWIN_LEAF_KINDS = {'loss': 'loss', 'grad_x': 'grad_x', 'grad_mod_w': 'grad_w', 'grad_mod_b': 'grad_w', 'grad_norm_pre': 'grad_w', 'grad_norm_post': 'grad_w', 'grad_ffn_w_in': 'grad_w', 'grad_ffn_w_out': 'grad_w', 'grad_mix_w_in': 'grad_w', 'grad_forget_b': 'grad_w', 'grad_ssm_a_re': 'grad_w', 'grad_ssm_a_im': 'grad_w', 'grad_ssm_log_dt': 'grad_w', 'grad_ssm_b_re': 'grad_w', 'grad_ssm_b_im': 'grad_w', 'grad_ssm_c_re': 'grad_w', 'grad_ssm_c_im': 'grad_w', 'grad_ssm_d': 'grad_w', 'grad_glu_w': 'grad_w', 'grad_attn_w_out': 'grad_w', 'grad_mix_w_out': 'grad_w', 'delta_mod_w': 'delta_w', 'delta_mod_b': 'delta_w', 'delta_norm_pre': 'delta_w', 'delta_norm_post': 'delta_w', 'delta_ffn_w_in': 'delta_w', 'delta_ffn_w_out': 'delta_w', 'delta_mix_w_in': 'delta_w', 'delta_forget_b': 'delta_w', 'delta_ssm_a_re': 'delta_w', 'delta_ssm_a_im': 'delta_w', 'delta_ssm_log_dt': 'delta_w', 'delta_ssm_b_re': 'delta_w', 'delta_ssm_b_im': 'delta_w', 'delta_ssm_c_re': 'delta_w', 'delta_ssm_c_im': 'delta_w', 'delta_ssm_d': 'delta_w', 'delta_glu_w': 'delta_w', 'delta_attn_w_out': 'delta_w', 'delta_mix_w_out': 'delta_w', 'new_m_mod_w': 'new_m', 'new_m_mod_b': 'new_m', 'new_m_norm_pre': 'new_m', 'new_m_norm_post': 'new_m', 'new_m_ffn_w_in': 'new_m', 'new_m_ffn_w_out': 'new_m', 'new_m_mix_w_in': 'new_m', 'new_m_forget_b': 'new_m', 'new_m_ssm_a_re': 'new_m', 'new_m_ssm_a_im': 'new_m', 'new_m_ssm_log_dt': 'new_m', 'new_m_ssm_b_re': 'new_m', 'new_m_ssm_b_im': 'new_m', 'new_m_ssm_c_re': 'new_m', 'new_m_ssm_c_im': 'new_m', 'new_m_ssm_d': 'new_m', 'new_m_glu_w': 'new_m', 'new_m_attn_w_out': 'new_m', 'new_m_mix_w_out': 'new_m', 'new_v_mod_w': 'new_v', 'new_v_mod_b': 'new_v', 'new_v_norm_pre': 'new_v', 'new_v_norm_post': 'new_v', 'new_v_ffn_w_in': 'new_v', 'new_v_ffn_w_out': 'new_v', 'new_v_mix_w_in': 'new_v', 'new_v_forget_b': 'new_v', 'new_v_ssm_a_re': 'new_v', 'new_v_ssm_a_im': 'new_v', 'new_v_ssm_log_dt': 'new_v', 'new_v_ssm_b_re': 'new_v', 'new_v_ssm_b_im': 'new_v', 'new_v_ssm_c_re': 'new_v', 'new_v_ssm_c_im': 'new_v', 'new_v_ssm_d': 'new_v', 'new_v_glu_w': 'new_v', 'new_v_attn_w_out': 'new_v', 'new_v_mix_w_out': 'new_v'}


def _forward(args):
    return _fwd_reference(*[args[k] for k in FWD_PARAMS])


def _output_shape():
    def fwd():
        inp = _fwd_setup_inputs(0)
        return _fwd_reference(*[inp[k] for k in FWD_PARAMS])
    out = _jax.eval_shape(fwd)
    return out.shape, out.dtype

N_MICROBATCH = 1
ADAM_LR = 0.001
ADAM_B1 = 0.9
ADAM_B2 = 0.999
ADAM_EPS = 1e-08
ADAM_WD = 0.01
ADAM_STEP = 10
PER_EXAMPLE_BATCH_AXIS = {'x': 0, 'c': 0, 'loss_target': 0}
SHARED_INPUTS = []
_WEIGHT_DTYPES = {'mod_w': _jnp.float32, 'mod_b': _jnp.float32, 'norm_pre': _jnp.float32, 'norm_post': _jnp.float32, 'ffn_w_in': _jnp.float32, 'ffn_w_out': _jnp.float32, 'mix_w_in': _jnp.float32, 'forget_b': _jnp.float32, 'ssm_a_re': _jnp.float32, 'ssm_a_im': _jnp.float32, 'ssm_log_dt': _jnp.float32, 'ssm_b_re': _jnp.float32, 'ssm_b_im': _jnp.float32, 'ssm_c_re': _jnp.float32, 'ssm_c_im': _jnp.float32, 'ssm_d': _jnp.float32, 'glu_w': _jnp.float32, 'attn_w_out': _jnp.float32, 'mix_w_out': _jnp.float32}
MOMENT_SCALE = {'mod_w': 1.884464e+00, 'mod_b': 3.718096e+00, 'norm_pre': 2.133605e-01, 'norm_post': 4.253548e+00, 'ffn_w_in': 6.279926e-02, 'ffn_w_out': 1.218948e-01, 'mix_w_in': 5.126410e-01, 'forget_b': 1.287084e+00, 'ssm_a_re': 3.674953e-02, 'ssm_a_im': 3.384959e-02, 'ssm_log_dt': 5.545699e+00, 'ssm_b_re': 3.122312e-02, 'ssm_b_im': 3.015803e-02, 'ssm_c_re': 6.120356e-02, 'ssm_c_im': 6.256095e-02, 'ssm_d': 9.976239e-01, 'glu_w': 4.818666e-01, 'attn_w_out': 1.117442e+00, 'mix_w_out': 1.300036e+00}


def _to_microbatches(a, axis):
    t = _jnp.moveaxis(a, axis, 0)
    t = t.reshape((N_MICROBATCH, t.shape[0] // N_MICROBATCH) + t.shape[1:])
    return _jnp.moveaxis(t, 1, axis + 1)


def setup_inputs(seed: int = 0) -> dict:
    inp = _fwd_setup_inputs(seed)
    key = _jax.random.fold_in(_jax.random.key(seed), 7919)
    shape, _ = _output_shape()
    out = dict(inp)
    out["loss_target"] = _jax.random.normal(_jax.random.fold_in(key, 0), shape, _jnp.float32)
    for i, name in enumerate(TWIN_WEIGHTS):
        w = inp[name].astype(_jnp.float32)
        if MOMENT_SCALE is None:
            s = _jnp.sqrt(_jnp.mean(_jnp.square(w)) + 1e-30)
        else:
            s = MOMENT_SCALE[name]
        km, kv = _jax.random.split(_jax.random.fold_in(key, i + 1))
        out[name] = w
        out["m_" + name] = s * _jax.random.normal(km, w.shape, _jnp.float32)
        out["v_" + name] = (s * s) * _jax.random.uniform(kv, w.shape, _jnp.float32, 0.5, 1.5)
    if N_MICROBATCH > 1:
        for name, axis in PER_EXAMPLE_BATCH_AXIS.items():
            out[name] = _to_microbatches(out[name], axis)
    return {'x': out['x'], 'c': out['c'], 'mod_w': out['mod_w'], 'mod_b': out['mod_b'], 'norm_pre': out['norm_pre'], 'norm_post': out['norm_post'], 'ffn_w_in': out['ffn_w_in'], 'ffn_w_out': out['ffn_w_out'], 'mix_w_in': out['mix_w_in'], 'forget_b': out['forget_b'], 'ssm_a_re': out['ssm_a_re'], 'ssm_a_im': out['ssm_a_im'], 'ssm_log_dt': out['ssm_log_dt'], 'ssm_b_re': out['ssm_b_re'], 'ssm_b_im': out['ssm_b_im'], 'ssm_c_re': out['ssm_c_re'], 'ssm_c_im': out['ssm_c_im'], 'ssm_d': out['ssm_d'], 'glu_w': out['glu_w'], 'attn_w_out': out['attn_w_out'], 'mix_w_out': out['mix_w_out'], 'loss_target': out['loss_target'], 'm_mod_w': out['m_mod_w'], 'm_mod_b': out['m_mod_b'], 'm_norm_pre': out['m_norm_pre'], 'm_norm_post': out['m_norm_post'], 'm_ffn_w_in': out['m_ffn_w_in'], 'm_ffn_w_out': out['m_ffn_w_out'], 'm_mix_w_in': out['m_mix_w_in'], 'm_forget_b': out['m_forget_b'], 'm_ssm_a_re': out['m_ssm_a_re'], 'm_ssm_a_im': out['m_ssm_a_im'], 'm_ssm_log_dt': out['m_ssm_log_dt'], 'm_ssm_b_re': out['m_ssm_b_re'], 'm_ssm_b_im': out['m_ssm_b_im'], 'm_ssm_c_re': out['m_ssm_c_re'], 'm_ssm_c_im': out['m_ssm_c_im'], 'm_ssm_d': out['m_ssm_d'], 'm_glu_w': out['m_glu_w'], 'm_attn_w_out': out['m_attn_w_out'], 'm_mix_w_out': out['m_mix_w_out'], 'v_mod_w': out['v_mod_w'], 'v_mod_b': out['v_mod_b'], 'v_norm_pre': out['v_norm_pre'], 'v_norm_post': out['v_norm_post'], 'v_ffn_w_in': out['v_ffn_w_in'], 'v_ffn_w_out': out['v_ffn_w_out'], 'v_mix_w_in': out['v_mix_w_in'], 'v_forget_b': out['v_forget_b'], 'v_ssm_a_re': out['v_ssm_a_re'], 'v_ssm_a_im': out['v_ssm_a_im'], 'v_ssm_log_dt': out['v_ssm_log_dt'], 'v_ssm_b_re': out['v_ssm_b_re'], 'v_ssm_b_im': out['v_ssm_b_im'], 'v_ssm_c_re': out['v_ssm_c_re'], 'v_ssm_c_im': out['v_ssm_c_im'], 'v_ssm_d': out['v_ssm_d'], 'v_glu_w': out['v_glu_w'], 'v_attn_w_out': out['v_attn_w_out'], 'v_mix_w_out': out['v_mix_w_out']}


def _loss(weights, diff, rest, loss_target):
    with _jax.named_scope("forward"):
        args = {**rest, TWIN_DIFF_INPUT: diff, **{k: w.astype(_WEIGHT_DTYPES[k]) for k, w in weights.items()}}
        y = _forward(args)
    with _jax.named_scope("loss_head"):
        err = _jnp.square(y.astype(_jnp.float32) - loss_target)
        return 0.5 * _jnp.sum(_jnp.mean(err, axis=-1)) if err.ndim else 0.5 * err


def _adamw(w, g, m, v):
    m = ADAM_B1 * m + (1.0 - ADAM_B1) * g
    v = ADAM_B2 * v + (1.0 - ADAM_B2) * _jnp.square(g)
    m_hat = m / (1.0 - ADAM_B1 ** ADAM_STEP)
    v_hat = v / (1.0 - ADAM_B2 ** ADAM_STEP)
    delta = -ADAM_LR * (m_hat / (_jnp.sqrt(v_hat) + ADAM_EPS) + ADAM_WD * w)
    return delta, m, v


def reference(x, c, mod_w, mod_b, norm_pre, norm_post, ffn_w_in, ffn_w_out, mix_w_in, forget_b, ssm_a_re, ssm_a_im, ssm_log_dt, ssm_b_re, ssm_b_im, ssm_c_re, ssm_c_im, ssm_d, glu_w, attn_w_out, mix_w_out, loss_target, m_mod_w, m_mod_b, m_norm_pre, m_norm_post, m_ffn_w_in, m_ffn_w_out, m_mix_w_in, m_forget_b, m_ssm_a_re, m_ssm_a_im, m_ssm_log_dt, m_ssm_b_re, m_ssm_b_im, m_ssm_c_re, m_ssm_c_im, m_ssm_d, m_glu_w, m_attn_w_out, m_mix_w_out, v_mod_w, v_mod_b, v_norm_pre, v_norm_post, v_ffn_w_in, v_ffn_w_out, v_mix_w_in, v_forget_b, v_ssm_a_re, v_ssm_a_im, v_ssm_log_dt, v_ssm_b_re, v_ssm_b_im, v_ssm_c_re, v_ssm_c_im, v_ssm_d, v_glu_w, v_attn_w_out, v_mix_w_out):
    given = dict(x=x, c=c, mod_w=mod_w, mod_b=mod_b, norm_pre=norm_pre, norm_post=norm_post, ffn_w_in=ffn_w_in, ffn_w_out=ffn_w_out, mix_w_in=mix_w_in, forget_b=forget_b, ssm_a_re=ssm_a_re, ssm_a_im=ssm_a_im, ssm_log_dt=ssm_log_dt, ssm_b_re=ssm_b_re, ssm_b_im=ssm_b_im, ssm_c_re=ssm_c_re, ssm_c_im=ssm_c_im, ssm_d=ssm_d, glu_w=glu_w, attn_w_out=attn_w_out, mix_w_out=mix_w_out, loss_target=loss_target, m_mod_w=m_mod_w, m_mod_b=m_mod_b, m_norm_pre=m_norm_pre, m_norm_post=m_norm_post, m_ffn_w_in=m_ffn_w_in, m_ffn_w_out=m_ffn_w_out, m_mix_w_in=m_mix_w_in, m_forget_b=m_forget_b, m_ssm_a_re=m_ssm_a_re, m_ssm_a_im=m_ssm_a_im, m_ssm_log_dt=m_ssm_log_dt, m_ssm_b_re=m_ssm_b_re, m_ssm_b_im=m_ssm_b_im, m_ssm_c_re=m_ssm_c_re, m_ssm_c_im=m_ssm_c_im, m_ssm_d=m_ssm_d, m_glu_w=m_glu_w, m_attn_w_out=m_attn_w_out, m_mix_w_out=m_mix_w_out, v_mod_w=v_mod_w, v_mod_b=v_mod_b, v_norm_pre=v_norm_pre, v_norm_post=v_norm_post, v_ffn_w_in=v_ffn_w_in, v_ffn_w_out=v_ffn_w_out, v_mix_w_in=v_mix_w_in, v_forget_b=v_forget_b, v_ssm_a_re=v_ssm_a_re, v_ssm_a_im=v_ssm_a_im, v_ssm_log_dt=v_ssm_log_dt, v_ssm_b_re=v_ssm_b_re, v_ssm_b_im=v_ssm_b_im, v_ssm_c_re=v_ssm_c_re, v_ssm_c_im=v_ssm_c_im, v_ssm_d=v_ssm_d, v_glu_w=v_glu_w, v_attn_w_out=v_attn_w_out, v_mix_w_out=v_mix_w_out)
    weights = {n: given[n] for n in TWIN_WEIGHTS}
    shared = {n: given[n] for n in SHARED_INPUTS}
    per_example = {n: given[n] for n in ['x', 'c']}
    grad_fn = _jax.value_and_grad(_loss, argnums=(0, 1))

    def one_microbatch(ex, loss_target):
        ex = dict(ex)
        diff = ex.pop(TWIN_DIFF_INPUT)
        return grad_fn(weights, diff, {**shared, **ex}, loss_target)

    if N_MICROBATCH == 1:
        loss, (grad_w, grad_x) = one_microbatch(per_example, given["loss_target"])
    else:
        def body(carry, xs):
            loss_sum, grad_sum = carry
            l_k, (gw_k, gx_k) = one_microbatch(xs[0], xs[1])
            with _jax.named_scope("update"):
                return (loss_sum + l_k, _jax.tree.map(_jnp.add, grad_sum, gw_k)), gx_k

        init = (_jnp.zeros((), _jnp.float32), _jax.tree.map(_jnp.zeros_like, weights))
        (loss, grad_w), grad_x = _jax.lax.scan(body, init, (per_example, given["loss_target"]))
    with _jax.named_scope("update"):
        delta_w, new_m, new_v = {}, {}, {}
        for n in TWIN_WEIGHTS:
            delta_w[n], new_m[n], new_v[n] = _adamw(weights[n], grad_w[n], given["m_" + n], given["v_" + n])
    return (loss, grad_x, *[grad_w[n] for n in TWIN_WEIGHTS], *[delta_w[n] for n in TWIN_WEIGHTS],
            *[new_m[n] for n in TWIN_WEIGHTS], *[new_v[n] for n in TWIN_WEIGHTS])
```

```python
import functools
import math

import jax
import jax.numpy as jnp
import numpy as np
from jax import lax
from jax.experimental import pallas as pl
from jax.experimental.pallas import tpu as pltpu

F32 = jnp.float32
BF16 = jnp.bfloat16

D_MODEL = 1024
DEPTH = 2
SSM_WIDTH = 512
SSM_GROUP = 16
SSM_GROUPS = 32
SSM_STATE = 64
SSM_FLAT = SSM_GROUPS * SSM_STATE
ATTN_HEADS = 8
HEAD_DIM = 64
ATTN_WIDTH = 512
D_FF = 2816
FFN_RES = 0.5
N_SUB = 3
RMS_EPS = 1e-6
N_CHIPS = 4
N_DEV = 8

ADAM_LR = 0.001
ADAM_B1 = 0.9
ADAM_B2 = 0.999
ADAM_EPS = 1e-08
ADAM_WD = 0.01
ADAM_STEP = 10

LANES = 128
SUBLANES = 8
VMEM_LIMIT = 52 * 1024 * 1024
MESH = pl.DeviceIdType.MESH

NN = (((1,), (0,)), ((), ()))
NT = (((1,), (1,)), ((), ()))
TN = (((0,), (0,)), ((), ()))


def _tile(dim, target, align=LANES):
    best = None
    t = align
    while t <= min(dim, target):
        if dim % t == 0:
            best = t
        t += align
    return dim if best is None else best


def _params(*sem):
    return pltpu.CompilerParams(dimension_semantics=sem, vmem_limit_bytes=VMEM_LIMIT)


def _mm(a, b, *, name, ta=False, tb=False, out_dtype=F32, bias=None, bscale=None,
        tm=512, tn=1024, tk=1024):
    M, K = (a.shape[1], a.shape[0]) if ta else a.shape
    N = b.shape[0] if tb else b.shape[1]
    assert K == (b.shape[1] if tb else b.shape[0]), (a.shape, b.shape, ta, tb)
    tm, tn, tk = _tile(M, tm), _tile(N, tn), _tile(K, tk)
    nk = K // tk
    dn = (((0 if ta else 1,), (1 if tb else 0,)), ((), ()))
    has_bias, has_scale = bias is not None, bscale is not None

    def body(*refs):
        a_ref, b_ref = refs[0], refs[1]
        pos = 2
        bias_ref = scale_ref = None
        if has_bias:
            bias_ref = refs[pos]
            pos += 1
        if has_scale:
            scale_ref = refs[pos]
            pos += 1
        o_ref = refs[pos]
        acc_ref = refs[pos + 1] if nk > 1 else None

        def finish(r):
            if has_bias:
                extra = bias_ref[...].astype(F32)
                if has_scale:
                    extra = extra * scale_ref[...]
                r = r + extra
            o_ref[...] = r.astype(out_dtype)

        part = lax.dot_general(a_ref[...].astype(BF16), b_ref[...].astype(BF16), dn,
                               preferred_element_type=F32)
        if nk == 1:
            finish(part)
        else:
            k = pl.program_id(2)

            @pl.when(k == 0)
            def _():
                acc_ref[...] = part

            @pl.when(k > 0)
            def _():
                acc_ref[...] += part

            @pl.when(k == nk - 1)
            def _():
                finish(acc_ref[...])

    a_spec = pl.BlockSpec((tk, tm), lambda j, i, k: (k, i)) if ta else pl.BlockSpec((tm, tk), lambda j, i, k: (i, k))
    b_spec = pl.BlockSpec((tn, tk), lambda j, i, k: (j, k)) if tb else pl.BlockSpec((tk, tn), lambda j, i, k: (k, j))
    in_specs = [a_spec, b_spec]
    args = [a, b]
    if has_bias:
        in_specs.append(pl.BlockSpec((tm, tn), lambda j, i, k: (i, j)))
        args.append(bias)
    if has_scale:
        in_specs.append(pl.BlockSpec((1, tn), lambda j, i, k: (0, j)))
        args.append(bscale)
    return pl.pallas_call(
        body, name=name,
        grid=(N // tn, M // tm, nk),
        in_specs=in_specs,
        out_specs=pl.BlockSpec((tm, tn), lambda j, i, k: (i, j)),
        out_shape=jax.ShapeDtypeStruct((M, N), out_dtype),
        scratch_shapes=[pltpu.VMEM((tm, tn), F32)] if nk > 1 else [],
        compiler_params=_params("parallel", "parallel", "arbitrary"),
    )(*args)


def _sigmoid(x):
    return 1.0 / (1.0 + jnp.exp(-x))


def _mm_swiglu(h, wg, wu, *, name, tm=512, tn=1408):
    M, K = h.shape
    N = wg.shape[1]
    tm, tn = _tile(M, tm), _tile(N, tn)

    def body(h_ref, wg_ref, wu_ref, g_ref, u_ref, a_ref):
        hv = h_ref[...]
        g = jnp.dot(hv, wg_ref[...], preferred_element_type=F32)
        u = jnp.dot(hv, wu_ref[...], preferred_element_type=F32)
        g_ref[...] = g.astype(BF16)
        u_ref[...] = u.astype(BF16)
        a_ref[...] = (g * _sigmoid(g) * u).astype(BF16)

    w_spec = pl.BlockSpec((K, tn), lambda j, i: (0, j))
    o_spec = pl.BlockSpec((tm, tn), lambda j, i: (i, j))
    sds = jax.ShapeDtypeStruct((M, N), BF16)
    return pl.pallas_call(
        body, name=name, grid=(N // tn, M // tm),
        in_specs=[pl.BlockSpec((tm, K), lambda j, i: (i, 0)), w_spec, w_spec],
        out_specs=[o_spec, o_spec, o_spec], out_shape=[sds, sds, sds],
        compiler_params=_params("parallel", "parallel"),
    )(h, wg, wu)


def _mm_swiglu_bwd(dy, w_out, gate, up, *, name, tm=512, tn=1408):
    M, K = dy.shape
    N = w_out.shape[0]
    tm, tn = _tile(M, tm), _tile(N, tn)

    def body(dy_ref, w_ref, g_ref, u_ref, dg_ref, du_ref):
        dact = lax.dot_general(dy_ref[...], w_ref[...], NT, preferred_element_type=F32)
        g = g_ref[...].astype(F32)
        u = u_ref[...].astype(F32)
        sig = _sigmoid(g)
        dg_ref[...] = (dact * u * (sig * (1.0 + g * (1.0 - sig)))).astype(BF16)
        du_ref[...] = (dact * (g * sig)).astype(BF16)

    t_spec = pl.BlockSpec((tm, tn), lambda j, i: (i, j))
    sds = jax.ShapeDtypeStruct((M, N), BF16)
    return pl.pallas_call(
        body, name=name, grid=(N // tn, M // tm),
        in_specs=[pl.BlockSpec((tm, K), lambda j, i: (i, 0)), pl.BlockSpec((tn, K), lambda j, i: (j, 0)),
                  t_spec, t_spec],
        out_specs=[t_spec, t_spec], out_shape=[sds, sds],
        compiler_params=_params("parallel", "parallel"),
    )(dy, w_out, gate, up)


ROW_TILE = 256


def _colsum8(v):
    return jnp.sum(v.reshape(v.shape[0] // SUBLANES, SUBLANES, v.shape[1]), axis=0)


def _finish_colsums(step, last, refs):
    @pl.when(step == last)
    def _():
        for r in refs:
            r[...] = jnp.broadcast_to(jnp.sum(r[...], axis=0, keepdims=True), r.shape)


def _row_spec(t, d):
    return pl.BlockSpec((t, d), lambda i: (i, 0))


def _vec_spec(d, rows=1):
    return pl.BlockSpec((rows, d), lambda i: (0, 0))


def _prenorm(x, g, sc, sh, *, name):
    L, D = x.shape
    t = _tile(L, ROW_TILE, SUBLANES)

    def body(x_ref, g_ref, sc_ref, sh_ref, h_ref):
        xv = x_ref[...]
        r = lax.rsqrt(jnp.mean(xv * xv, axis=-1, keepdims=True) + RMS_EPS)
        h_ref[...] = (((xv * r) * g_ref[...]) * (1.0 + sc_ref[...]) + sh_ref[...]).astype(BF16)

    return pl.pallas_call(
        body, name=name, grid=(L // t,),
        in_specs=[_row_spec(t, D), _vec_spec(D), _vec_spec(D), _vec_spec(D)],
        out_specs=_row_spec(t, D), out_shape=jax.ShapeDtypeStruct((L, D), BF16),
        compiler_params=_params("parallel"),
    )(x, g, sc, sh)


def _postnorm_res(x, y, g, gate, res_w, *, name):
    L, D = x.shape
    t = _tile(L, ROW_TILE, SUBLANES)

    def body(x_ref, y_ref, g_ref, gate_ref, o_ref):
        yv = y_ref[...]
        r = lax.rsqrt(jnp.mean(yv * yv, axis=-1, keepdims=True) + RMS_EPS)
        o_ref[...] = x_ref[...] + (res_w * gate_ref[...]) * ((yv * r) * g_ref[...])

    return pl.pallas_call(
        body, name=name, grid=(L // t,),
        in_specs=[_row_spec(t, D), _row_spec(t, D), _vec_spec(D), _vec_spec(D)],
        out_specs=_row_spec(t, D), out_shape=jax.ShapeDtypeStruct((L, D), F32),
        compiler_params=_params("parallel"),
    )(x, y, g, gate)


def _postnorm_bwd(dxo, y, g, gate, res_w, *, name):
    L, D = y.shape
    t = _tile(L, ROW_TILE, SUBLANES)
    n = L // t

    def body(dxo_ref, y_ref, g_ref, gate_ref, dy_ref, dgate_ref, dg_ref):
        i = pl.program_id(0)

        @pl.when(i == 0)
        def _():
            dgate_ref[...] = jnp.zeros_like(dgate_ref)
            dg_ref[...] = jnp.zeros_like(dg_ref)

        yv = y_ref[...]
        dv = dxo_ref[...]
        gv = g_ref[...]
        r = lax.rsqrt(jnp.mean(yv * yv, axis=-1, keepdims=True) + RMS_EPS)
        yn = yv * r
        dgate_ref[...] += _colsum8(dv * (res_w * (yn * gv)))
        do = dv * (res_w * gate_ref[...])
        dg_ref[...] += _colsum8(do * yn)
        dyn = do * gv
        dy_ref[...] = (r * (dyn - yn * jnp.mean(dyn * yn, axis=-1, keepdims=True))).astype(BF16)
        _finish_colsums(i, n - 1, (dgate_ref, dg_ref))

    sum_sds = jax.ShapeDtypeStruct((SUBLANES, D), F32)
    return pl.pallas_call(
        body, name=name, grid=(n,),
        in_specs=[_row_spec(t, D), _row_spec(t, D), _vec_spec(D), _vec_spec(D)],
        out_specs=[_row_spec(t, D), _vec_spec(D, SUBLANES), _vec_spec(D, SUBLANES)],
        out_shape=[jax.ShapeDtypeStruct((L, D), BF16), sum_sds, sum_sds],
        compiler_params=_params("arbitrary"),
    )(dxo, y, g, gate)


def _prenorm_bwd(x, dh, dxres, g, sc, *, name):
    L, D = x.shape
    t = _tile(L, ROW_TILE, SUBLANES)
    n = L // t

    def body(x_ref, dh_ref, dxr_ref, g_ref, sc_ref, dx_ref, dsh_ref, dsc_ref, dg_ref):
        i = pl.program_id(0)

        @pl.when(i == 0)
        def _():
            dsh_ref[...] = jnp.zeros_like(dsh_ref)
            dsc_ref[...] = jnp.zeros_like(dsc_ref)
            dg_ref[...] = jnp.zeros_like(dg_ref)

        xv = x_ref[...]
        dhv = dh_ref[...].astype(F32)
        gv = g_ref[...]
        one_sc = 1.0 + sc_ref[...]
        r = lax.rsqrt(jnp.mean(xv * xv, axis=-1, keepdims=True) + RMS_EPS)
        xn = xv * r
        tt = dhv * xn
        dsh_ref[...] += _colsum8(dhv)
        dsc_ref[...] += _colsum8(tt * gv)
        dg_ref[...] += _colsum8(tt * one_sc)
        dxn = dhv * (gv * one_sc)
        dx_ref[...] = dxr_ref[...] + r * (dxn - xn * jnp.mean(dxn * xn, axis=-1, keepdims=True))
        _finish_colsums(i, n - 1, (dsh_ref, dsc_ref, dg_ref))

    sum_sds = jax.ShapeDtypeStruct((SUBLANES, D), F32)
    sum_spec = _vec_spec(D, SUBLANES)
    return pl.pallas_call(
        body, name=name, grid=(n,),
        in_specs=[_row_spec(t, D), _row_spec(t, D), _row_spec(t, D), _vec_spec(D), _vec_spec(D)],
        out_specs=[_row_spec(t, D), sum_spec, sum_spec, sum_spec],
        out_shape=[jax.ShapeDtypeStruct((L, D), F32), sum_sds, sum_sds, sum_sds],
        compiler_params=_params("arbitrary"),
    )(x, dh, dxres, g, sc)


def _loss_head(y, target, *, name):
    L, D = y.shape
    t = _tile(L, ROW_TILE, SUBLANES)
    n = L // t

    def body(y_ref, t_ref, dy_ref, loss_ref):
        i = pl.program_id(0)

        @pl.when(i == 0)
        def _():
            loss_ref[...] = jnp.zeros_like(loss_ref)

        e = y_ref[...] - t_ref[...]
        dy_ref[...] = e * (1.0 / D)
        part = jnp.sum(jnp.mean(e * e, axis=-1, keepdims=True), axis=0, keepdims=True)
        loss_ref[...] += jnp.broadcast_to(0.5 * part, loss_ref.shape)

    return pl.pallas_call(
        body, name=name, grid=(n,),
        in_specs=[_row_spec(t, D), _row_spec(t, D)],
        out_specs=[_row_spec(t, D), pl.BlockSpec((SUBLANES, LANES), lambda i: (0, 0))],
        out_shape=[jax.ShapeDtypeStruct((L, D), F32), jax.ShapeDtypeStruct((SUBLANES, LANES), F32)],
        compiler_params=_params("arbitrary"),
    )(y, target)


GELU_C = math.sqrt(2.0 / math.pi)


def _gelu_fwd(y, *, name):
    L, W = y.shape
    t = _tile(L, 512, SUBLANES)

    def body(y_ref, o_ref):
        v = y_ref[...]
        o_ref[...] = (0.5 * v * (1.0 + jnp.tanh(GELU_C * (v + 0.044715 * (v * v * v))))).astype(BF16)

    return pl.pallas_call(
        body, name=name, grid=(L // t,), in_specs=[_row_spec(t, W)], out_specs=_row_spec(t, W),
        out_shape=jax.ShapeDtypeStruct((L, W), BF16), compiler_params=_params("parallel"),
    )(y)


def _gelu_bwd(dgl, y, u, dskip, *, name):
    L, W = y.shape
    t = _tile(L, 512, SUBLANES)
    n = L // t

    def body(dgl_ref, y_ref, u_ref, d_ref, dy_ref, sk_ref, dd_ref):
        i = pl.program_id(0)

        @pl.when(i == 0)
        def _():
            dd_ref[...] = jnp.zeros_like(dd_ref)

        v = y_ref[...]
        inner = GELU_C * (v + 0.044715 * (v * v * v))
        th = jnp.tanh(inner)
        dgelu = 0.5 * (1.0 + th) + 0.5 * v * (1.0 - th * th) * (GELU_C * (1.0 + 3.0 * 0.044715 * (v * v)))
        dy = dgl_ref[...] * dgelu
        dy_ref[...] = dy.astype(BF16)
        sk_ref[...] = dy * d_ref[...]
        dd_ref[...] += _colsum8(dy * u_ref[...])
        _finish_colsums(i, n - 1, (dd_ref,))

    return pl.pallas_call(
        body, name=name, grid=(n,),
        in_specs=[_row_spec(t, W), _row_spec(t, W), _row_spec(t, W), _vec_spec(W)],
        out_specs=[_row_spec(t, W), _row_spec(t, W), _vec_spec(W, SUBLANES)],
        out_shape=[jax.ShapeDtypeStruct((L, W), BF16), jax.ShapeDtypeStruct((L, W), F32),
                   jax.ShapeDtypeStruct((SUBLANES, W), F32)],
        compiler_params=_params("arbitrary"),
    )(dgl, y, u, dskip)


def _merge_fwd(z, yb, gab, *, name):
    L, D = yb.shape
    t = _tile(L, ROW_TILE, SUBLANES)

    def body(z_ref, yb_ref, gab_ref, o_ref):
        ya = z_ref[:, :D] * _sigmoid(z_ref[:, D:])
        o_ref[...] = (_sigmoid(gab_ref[:, :D]) * ya + _sigmoid(gab_ref[:, D:]) * yb_ref[...]).astype(BF16)

    return pl.pallas_call(
        body, name=name, grid=(L // t,),
        in_specs=[_row_spec(t, 2 * D), _row_spec(t, D), _row_spec(t, 2 * D)],
        out_specs=_row_spec(t, D), out_shape=jax.ShapeDtypeStruct((L, D), BF16),
        compiler_params=_params("parallel"),
    )(z, yb, gab)


def _merge_bwd(dm, z, yb, gab, *, name):
    L, D = yb.shape
    t = _tile(L, ROW_TILE, SUBLANES)

    def body(dm_ref, z_ref, yb_ref, gab_ref, dz_ref, dyb_ref, dgab_ref):
        dmv = dm_ref[...]
        zv = z_ref[:, :D]
        sz = _sigmoid(z_ref[:, D:])
        sa = _sigmoid(gab_ref[:, :D])
        sb = _sigmoid(gab_ref[:, D:])
        ybv = yb_ref[...]
        dya = dmv * sa
        dz_ref[:, :D] = (dya * sz).astype(BF16)
        dz_ref[:, D:] = (dya * zv * (sz * (1.0 - sz))).astype(BF16)
        dyb_ref[...] = (dmv * sb).astype(BF16)
        dgab_ref[:, :D] = (dmv * (zv * sz) * (sa * (1.0 - sa))).astype(BF16)
        dgab_ref[:, D:] = (dmv * ybv * (sb * (1.0 - sb))).astype(BF16)

    return pl.pallas_call(
        body, name=name, grid=(L // t,),
        in_specs=[_row_spec(t, D), _row_spec(t, 2 * D), _row_spec(t, D), _row_spec(t, 2 * D)],
        out_specs=[_row_spec(t, 2 * D), _row_spec(t, D), _row_spec(t, 2 * D)],
        out_shape=[jax.ShapeDtypeStruct((L, 2 * D), BF16), jax.ShapeDtypeStruct((L, D), BF16),
                   jax.ShapeDtypeStruct((L, 2 * D), BF16)],
        compiler_params=_params("parallel"),
    )(dm, z, yb, gab)


SCAN_W = 512
SCAN_T = 512


def _rows_to_tile(rows):
    w = rows[0].shape[1]
    sub = lax.broadcasted_iota(jnp.int32, (SUBLANES, w), 0)
    tile = jnp.broadcast_to(rows[0], (SUBLANES, w))
    for j in range(1, SUBLANES):
        tile = jnp.where(sub == j, jnp.broadcast_to(rows[j], (SUBLANES, w)), tile)
    return tile


def _ssm_scan_fwd(bu_re, bu_im, lam_re, lam_im, *, name):
    L, S = bu_re.shape
    w, t = _tile(S, SCAN_W), _tile(L, SCAN_T, SUBLANES)

    def body(br_ref, bi_ref, lr_ref, li_ref, sr_ref, si_ref, cr_ref, ci_ref):
        @pl.when(pl.program_id(1) == 0)
        def _():
            cr_ref[...] = jnp.zeros_like(cr_ref)
            ci_ref[...] = jnp.zeros_like(ci_ref)

        a = lr_ref[...]
        b = li_ref[...]

        def tile_body(i, carry):
            cr, ci = carry
            base = pl.multiple_of(i * SUBLANES, SUBLANES)
            xr = br_ref[pl.ds(base, SUBLANES), :]
            xi = bi_ref[pl.ds(base, SUBLANES), :]
            rows_r, rows_i = [], []
            for j in range(SUBLANES):
                nr = a * cr - b * ci + xr[j:j + 1, :]
                ni = a * ci + b * cr + xi[j:j + 1, :]
                cr, ci = nr, ni
                rows_r.append(nr)
                rows_i.append(ni)
            sr_ref[pl.ds(base, SUBLANES), :] = _rows_to_tile(rows_r)
            si_ref[pl.ds(base, SUBLANES), :] = _rows_to_tile(rows_i)
            return cr, ci

        cr, ci = lax.fori_loop(0, t // SUBLANES, tile_body, (cr_ref[...], ci_ref[...]))
        cr_ref[...] = cr
        ci_ref[...] = ci

    blk = pl.BlockSpec((t, w), lambda j, i: (i, j))
    vec = pl.BlockSpec((1, w), lambda j, i: (0, j))
    sds = jax.ShapeDtypeStruct((L, S), F32)
    return pl.pallas_call(
        body, name=name, grid=(S // w, L // t),
        in_specs=[blk, blk, vec, vec], out_specs=[blk, blk], out_shape=[sds, sds],
        scratch_shapes=[pltpu.VMEM((1, w), F32), pltpu.VMEM((1, w), F32)],
        compiler_params=_params("parallel", "arbitrary"),
    )(bu_re, bu_im, lam_re, lam_im)


def _ssm_scan_bwd(d_re, d_im, s_re, s_im, lam_re, lam_im, *, name):
    L, S = d_re.shape
    w, t = _tile(S, SCAN_W), _tile(L, SCAN_T, SUBLANES)
    nt = L // t
    ntile = t // SUBLANES

    def body(dr_ref, di_ref, sr_ref, si_ref, lr_ref, li_ref, gr_ref, gi_ref, ar_ref, ai_ref,
             cgr, cgi, car, cai):
        step = pl.program_id(1)

        @pl.when(step == 0)
        def _():
            for r in (cgr, cgi, car, cai):
                r[...] = jnp.zeros_like(r)

        a = lr_ref[...]
        b = li_ref[...]

        def tile_body(ii, carry):
            gr, gi, acr, aci = carry
            base = pl.multiple_of((ntile - 1 - ii) * SUBLANES, SUBLANES)
            dr = dr_ref[pl.ds(base, SUBLANES), :]
            di = di_ref[pl.ds(base, SUBLANES), :]
            sr = sr_ref[pl.ds(base, SUBLANES), :]
            si = si_ref[pl.ds(base, SUBLANES), :]
            rows_r, rows_i = [None] * SUBLANES, [None] * SUBLANES
            for j in reversed(range(SUBLANES)):
                srj, sij = sr[j:j + 1, :], si[j:j + 1, :]
                acr = acr + (gr * srj + gi * sij)
                aci = aci + (gi * srj - gr * sij)
                ngr = dr[j:j + 1, :] + (a * gr + b * gi)
                ngi = di[j:j + 1, :] + (a * gi - b * gr)
                gr, gi = ngr, ngi
                rows_r[j], rows_i[j] = gr, gi
            gr_ref[pl.ds(base, SUBLANES), :] = _rows_to_tile(rows_r)
            gi_ref[pl.ds(base, SUBLANES), :] = _rows_to_tile(rows_i)
            return gr, gi, acr, aci

        gr, gi, acr, aci = lax.fori_loop(0, ntile, tile_body, (cgr[...], cgi[...], car[...], cai[...]))
        cgr[...] = gr
        cgi[...] = gi
        car[...] = acr
        cai[...] = aci

        @pl.when(step == nt - 1)
        def _():
            ar_ref[...] = acr
            ai_ref[...] = aci

    blk = pl.BlockSpec((t, w), lambda j, i: (nt - 1 - i, j))
    vec = pl.BlockSpec((1, w), lambda j, i: (0, j))
    sds = jax.ShapeDtypeStruct((L, S), F32)
    vsds = jax.ShapeDtypeStruct((1, S), F32)
    return pl.pallas_call(
        body, name=name, grid=(S // w, nt),
        in_specs=[blk, blk, blk, blk, vec, vec], out_specs=[blk, blk, vec, vec],
        out_shape=[sds, sds, vsds, vsds],
        scratch_shapes=[pltpu.VMEM((1, w), F32)] * 4,
        compiler_params=_params("parallel", "arbitrary"),
    )(d_re, d_im, s_re, s_im, lam_re, lam_im)


def _ssm_discretize(a_re, a_im, log_dt, b_re, b_im, c_re, c_im, d_skip):
    G, P, N = SSM_GROUPS, SSM_STATE, SSM_GROUP
    a = jnp.minimum(a_re, -1e-4)
    dt = jnp.exp(log_dt)[:, None]
    mag = jnp.exp(a * dt)
    lr = mag * jnp.cos(a_im * dt)
    li = mag * jnp.sin(a_im * dt)
    den = a * a + a_im * a_im
    cr = ((lr - 1.0) * a + li * a_im) / den
    ci = (li * a - (lr - 1.0) * a_im) / den
    bbr = cr[..., None] * b_re - ci[..., None] * b_im
    bbi = cr[..., None] * b_im + ci[..., None] * b_re
    eye = jnp.eye(G, dtype=F32)

    def in_map(bb):
        return (eye[:, None, :, None] * bb.transpose(0, 2, 1)[:, :, None, :]).reshape(G * N, G * P)

    def out_map(c):
        return (eye[:, None, :, None] * c.transpose(0, 2, 1)[:, :, None, :]).reshape(G * P, G * N)

    return (lr.reshape(1, G * P), li.reshape(1, G * P), in_map(bbr), in_map(bbi),
            out_map(c_re), out_map(-c_im), d_skip.reshape(1, SSM_WIDTH))


ATT_T = 512
CUM_T = 256


def _split3(x):
    hi = x.astype(BF16)
    r1 = x - hi.astype(F32)
    mid = r1.astype(BF16)
    lo = (r1 - mid.astype(F32)).astype(BF16)
    return hi, mid, lo


def _tri_dot(tri, x):
    hi, mid, lo = _split3(x)
    dot = lambda p: jnp.dot(tri, p, preferred_element_type=F32)
    return dot(hi) + dot(mid) + dot(lo)


def _log_sigmoid(x):
    return jnp.minimum(x, 0.0) - jnp.log(1.0 + jnp.exp(-jnp.abs(x)))


def _fox_cum(f, fb, *, name):
    L, W = f.shape
    t = _tile(L, CUM_T, SUBLANES)

    def body(f_ref, b_ref, o_ref, carry):
        @pl.when(pl.program_id(0) == 0)
        def _():
            carry[...] = jnp.zeros_like(carry)

        row = lax.broadcasted_iota(jnp.int32, (t, t), 0)
        col = lax.broadcasted_iota(jnp.int32, (t, t), 1)
        tri = jnp.where(col <= row, 1.0, 0.0).astype(BF16)
        c = _tri_dot(tri, _log_sigmoid(f_ref[...] + b_ref[...])) + carry[...]
        o_ref[...] = c
        carry[...] = c[t - 1:t, :]

    return pl.pallas_call(
        body, name=name, grid=(L // t,),
        in_specs=[_row_spec(t, W), _vec_spec(W)], out_specs=_row_spec(t, W),
        out_shape=jax.ShapeDtypeStruct((L, W), F32),
        scratch_shapes=[pltpu.VMEM((1, W), F32)], compiler_params=_params("arbitrary"),
    )(f, fb)


def _fox_cum_bwd(dcum, f, fb, *, name):
    L, W = f.shape
    t = _tile(L, CUM_T, SUBLANES)
    n = L // t

    def body(d_ref, f_ref, b_ref, o_ref, db_ref, carry):
        i = pl.program_id(0)

        @pl.when(i == 0)
        def _():
            carry[...] = jnp.zeros_like(carry)
            db_ref[...] = jnp.zeros_like(db_ref)

        row = lax.broadcasted_iota(jnp.int32, (t, t), 0)
        col = lax.broadcasted_iota(jnp.int32, (t, t), 1)
        tri = jnp.where(col >= row, 1.0, 0.0).astype(BF16)
        dlog = _tri_dot(tri, d_ref[...]) + carry[...]
        carry[...] = dlog[0:1, :]
        df = dlog * _sigmoid(-(f_ref[...] + b_ref[...]))
        o_ref[...] = df.astype(BF16)
        db_ref[...] += _colsum8(df)
        _finish_colsums(i, n - 1, (db_ref,))

    rev = pl.BlockSpec((t, W), lambda i: (n - 1 - i, 0))
    return pl.pallas_call(
        body, name=name, grid=(n,),
        in_specs=[rev, rev, _vec_spec(W)], out_specs=[rev, _vec_spec(W, SUBLANES)],
        out_shape=[jax.ShapeDtypeStruct((L, W), BF16), jax.ShapeDtypeStruct((SUBLANES, W), F32)],
        scratch_shapes=[pltpu.VMEM((1, W), F32)], compiler_params=_params("arbitrary"),
    )(dcum, f, fb)


def _head_col(blk, h):
    lane = lax.broadcasted_iota(jnp.int32, blk.shape, 1)
    return jnp.sum(jnp.where(lane == h * HEAD_DIM, blk, 0.0), axis=1, keepdims=True)


def _lo_mask(rows):
    return lax.broadcasted_iota(jnp.int32, (rows, LANES), 1) < HEAD_DIM


def _causal(t):
    row = lax.broadcasted_iota(jnp.int32, (t, t), 0)
    col = lax.broadcasted_iota(jnp.int32, (t, t), 1)
    return col <= row


def _fox_fwd(qkv, cum_cols, cum_rows, *, name):
    L = qkv.shape[0]
    t = _tile(L, ATT_T)
    nq = L // t
    npair = ATTN_HEADS // 2

    def body(q_ref, k_ref, v_ref, cc_ref, cr_ref, o_ref, o32_ref, lse_ref):
        iq = pl.program_id(1)
        lo = _lo_mask(t)
        qv = q_ref[...] * 0.125
        zq = jnp.zeros_like(qv)
        qh = (jnp.where(lo, qv, zq), jnp.where(lo, zq, qv))
        ccv = cc_ref[...]
        cq = (_head_col(ccv, 0), _head_col(ccv, 1))

        def step(ik, carry, masked):
            start = pl.multiple_of(ik * t, t)
            kb = k_ref[pl.ds(start, t), :]
            vb = v_ref[pl.ds(start, t), :]
            out = []
            for h in range(2):
                m, l, acc = carry[h]
                s = lax.dot_general(qh[h], kb, NT, preferred_element_type=F32)
                s = s + (cq[h] - cr_ref[h:h + 1, pl.ds(start, t)])
                if masked:
                    s = jnp.where(_causal(t), s, -jnp.inf)
                m_new = jnp.maximum(m, jnp.max(s, axis=1, keepdims=True))
                alpha = jnp.exp(m - m_new)
                p = jnp.exp(s - m_new)
                l = alpha * l + jnp.sum(p, axis=1, keepdims=True)
                acc = alpha * acc + jnp.dot(p.astype(BF16), vb, preferred_element_type=F32)
                out.append((m_new, l, acc))
            return tuple(out)

        init1 = (jnp.full((t, 1), -jnp.inf, F32), jnp.zeros((t, 1), F32), jnp.zeros((t, LANES), F32))
        carry = lax.fori_loop(0, iq, lambda ik, c: step(ik, c, False), (init1, init1))
        (m0, l0, a0), (m1, l1, a1) = step(iq, carry, True)
        out = jnp.where(lo, a0 / l0, a1 / l1)
        o_ref[...] = out.astype(BF16)
        o32_ref[...] = out
        lse_ref[...] = jnp.where(lo, m0 + jnp.log(l0), m1 + jnp.log(l1))

    blk = lambda off: pl.BlockSpec((t, LANES), lambda hp, iq: (iq, off + hp))
    whole = lambda off: pl.BlockSpec((L, LANES), lambda hp, iq: (0, off + hp))
    return pl.pallas_call(
        body, name=name, grid=(npair, nq),
        in_specs=[blk(0), whole(npair), whole(2 * npair), blk(0),
                  pl.BlockSpec((None, SUBLANES, L), lambda hp, iq: (hp, 0, 0))],
        out_specs=[blk(0), blk(0), blk(0)],
        out_shape=[jax.ShapeDtypeStruct((L, ATTN_WIDTH), BF16), jax.ShapeDtypeStruct((L, ATTN_WIDTH), F32),
                   jax.ShapeDtypeStruct((L, ATTN_WIDTH), F32)],
        compiler_params=_params("parallel", "arbitrary"),
    )(qkv, qkv, qkv, cum_cols, cum_rows)


def _fox_bwd_kv(qkv, do, o, lse, cum_cols, cum_rows, *, name):
    L = qkv.shape[0]
    t = _tile(L, ATT_T)
    nq = L // t
    npair = ATTN_HEADS // 2

    def body(q_ref, do_ref, o_ref, lse_ref, cc_ref, k_ref, v_ref, cr_ref, dk_ref, dv_ref, dc_ref):
        ik = pl.program_id(1)
        lo = _lo_mask(t)
        kb = k_ref[...]
        vb = v_ref[...]
        zk = jnp.zeros_like(kb)
        kh = (jnp.where(lo, kb, zk), jnp.where(lo, zk, kb))
        vh = (jnp.where(lo, vb, zk), jnp.where(lo, zk, vb))
        ck = (cr_ref[0:1, :], cr_ref[1:2, :])

        def step(iq, carry, masked):
            dk, dv, dc0, dc1 = carry
            start = pl.multiple_of(iq * t, t)
            qb = q_ref[pl.ds(start, t), :] * 0.125
            dob = do_ref[pl.ds(start, t), :]
            dd = dob.astype(F32) * o_ref[pl.ds(start, t), :]
            lseb = lse_ref[pl.ds(start, t), :]
            ccb = cc_ref[pl.ds(start, t), :]
            dks, dvs, dcs = [], [], []
            for h in range(2):
                sel = lo if h == 0 else jnp.logical_not(lo)
                delta = jnp.sum(jnp.where(sel, dd, 0.0), axis=1, keepdims=True)
                s = lax.dot_general(qb, kh[h], NT, preferred_element_type=F32)
                s = s + (_head_col(ccb, h) - ck[h])
                if masked:
                    s = jnp.where(_causal(t), s, -jnp.inf)
                p = jnp.exp(s - _head_col(lseb, h))
                dp = lax.dot_general(dob, vh[h], NT, preferred_element_type=F32)
                ds = p * (dp - delta)
                dvs.append(lax.dot_general(p.astype(BF16), dob, TN, preferred_element_type=F32))
                dks.append(lax.dot_general(ds.astype(BF16), qb, TN, preferred_element_type=F32))
                dcs.append(jnp.sum(ds, axis=0, keepdims=True))
            return (dk + jnp.where(lo, dks[0], dks[1]), dv + jnp.where(lo, dvs[0], dvs[1]),
                    dc0 - dcs[0], dc1 - dcs[1])

        zero = jnp.zeros((t, LANES), F32)
        zrow = jnp.zeros((1, t), F32)
        carry = step(ik, (zero, zero, zrow, zrow), True)
        dk, dv, dc0, dc1 = lax.fori_loop(ik + 1, nq, lambda iq, c: step(iq, c, False), carry)
        dk_ref[...] = dk.astype(BF16)
        dv_ref[...] = dv.astype(BF16)
        dc_ref[...] = jnp.zeros_like(dc_ref)
        dc_ref[0:1, :] = dc0
        dc_ref[1:2, :] = dc1

    whole = lambda off: pl.BlockSpec((L, LANES), lambda hp, ik: (0, off + hp))
    blk = lambda off: pl.BlockSpec((t, LANES), lambda hp, ik: (ik, off + hp))
    rows = pl.BlockSpec((None, SUBLANES, t), lambda hp, ik: (hp, 0, ik))
    return pl.pallas_call(
        body, name=name, grid=(npair, nq),
        in_specs=[whole(0), whole(0), whole(0), whole(0), whole(0), blk(npair), blk(2 * npair), rows],
        out_specs=[blk(0), blk(0), rows],
        out_shape=[jax.ShapeDtypeStruct((L, ATTN_WIDTH), BF16), jax.ShapeDtypeStruct((L, ATTN_WIDTH), BF16),
                   jax.ShapeDtypeStruct((npair, SUBLANES, L), F32)],
        compiler_params=_params("parallel", "arbitrary"),
    )(qkv, do, o, lse, cum_cols, qkv, qkv, cum_rows)


def _fox_bwd_q(qkv, do, o, lse, cum_cols, cum_rows, *, name):
    L = qkv.shape[0]
    t = _tile(L, ATT_T)
    nq = L // t
    npair = ATTN_HEADS // 2

    def body(q_ref, do_ref, o_ref, lse_ref, cc_ref, k_ref, v_ref, cr_ref, dq_ref, drow_ref):
        iq = pl.program_id(1)
        lo = _lo_mask(t)
        qv = q_ref[...] * 0.125
        zq = jnp.zeros_like(qv)
        qh = (jnp.where(lo, qv, zq), jnp.where(lo, zq, qv))
        dob = do_ref[...]
        doh = (jnp.where(lo, dob, zq), jnp.where(lo, zq, dob))
        dd = dob.astype(F32) * o_ref[...]
        delta = (jnp.sum(jnp.where(lo, dd, 0.0), axis=1, keepdims=True),
                 jnp.sum(jnp.where(lo, 0.0, dd), axis=1, keepdims=True))
        lsev, ccv = lse_ref[...], cc_ref[...]
        lse = (_head_col(lsev, 0), _head_col(lsev, 1))
        cq = (_head_col(ccv, 0), _head_col(ccv, 1))

        def step(ik, carry, masked):
            start = pl.multiple_of(ik * t, t)
            kb = k_ref[pl.ds(start, t), :]
            vb = v_ref[pl.ds(start, t), :]
            out = []
            for h in range(2):
                s = lax.dot_general(qh[h], kb, NT, preferred_element_type=F32)
                s = s + (cq[h] - cr_ref[h:h + 1, pl.ds(start, t)])
                if masked:
                    s = jnp.where(_causal(t), s, -jnp.inf)
                p = jnp.exp(s - lse[h])
                dp = lax.dot_general(doh[h], vb, NT, preferred_element_type=F32)
                ds = p * (dp - delta[h])
                dq_h, rs_h = carry[h]
                out.append((dq_h + jnp.dot(ds.astype(BF16), kb, preferred_element_type=F32),
                            rs_h + jnp.sum(ds, axis=1, keepdims=True)))
            return tuple(out)

        init1 = (jnp.zeros((t, LANES), F32), jnp.zeros((t, 1), F32))
        carry = lax.fori_loop(0, iq, lambda ik, c: step(ik, c, False), (init1, init1))
        (dq0, rs0), (dq1, rs1) = step(iq, carry, True)
        dq_ref[...] = (jnp.where(lo, dq0, dq1) * 0.125).astype(BF16)
        drow_ref[...] = jnp.where(lo, rs0, rs1)

    blk = lambda off: pl.BlockSpec((t, LANES), lambda hp, iq: (iq, off + hp))
    whole = lambda off: pl.BlockSpec((L, LANES), lambda hp, iq: (0, off + hp))
    return pl.pallas_call(
        body, name=name, grid=(npair, nq),
        in_specs=[blk(0), blk(0), blk(0), blk(0), blk(0), whole(npair), whole(2 * npair),
                  pl.BlockSpec((None, SUBLANES, L), lambda hp, iq: (hp, 0, 0))],
        out_specs=[blk(0), blk(0)],
        out_shape=[jax.ShapeDtypeStruct((L, ATTN_WIDTH), BF16), jax.ShapeDtypeStruct((L, ATTN_WIDTH), F32)],
        compiler_params=_params("parallel", "arbitrary"),
    )(qkv, do, o, lse, cum_cols, qkv, qkv, cum_rows)


def _mod_partial(c_all, mod_w, mod_b_cols, *, name):
    depth, K, cols = mod_w.shape
    tn = _tile(cols, 768)

    def body(c_ref, w_ref, b_ref, o_ref):
        cv = c_ref[...]
        sc = (cv * _sigmoid(cv)).astype(BF16)
        o_ref[...] = jnp.dot(sc, w_ref[...].astype(BF16), preferred_element_type=F32) + b_ref[...]

    return pl.pallas_call(
        body, name=name, grid=(depth, cols // tn),
        in_specs=[pl.BlockSpec((N_DEV, K), lambda l, j: (0, 0)),
                  pl.BlockSpec((None, K, tn), lambda l, j: (l, 0, j)),
                  pl.BlockSpec((None, 1, tn), lambda l, j: (l, 0, j))],
        out_specs=pl.BlockSpec((None, N_DEV, tn), lambda l, j: (l, 0, j)),
        out_shape=jax.ShapeDtypeStruct((depth, N_DEV, cols), F32),
        compiler_params=_params("parallel", "parallel"),
    )(c_all, mod_w, mod_b_cols)


def _mod_wgrad(c_all_t, dmod, *, name):
    depth, nb, cols = dmod.shape
    K = c_all_t.shape[0]
    tn = _tile(cols, 768)
    tk = _tile(K, 256, SUBLANES)

    def body(c_ref, d_ref, o_ref):
        cv = c_ref[...]
        sc = cv * _sigmoid(cv)
        dv = d_ref[...]
        acc = sc[:, 0:1] * dv[0:1, :]
        for b in range(1, nb):
            acc = acc + sc[:, b:b + 1] * dv[b:b + 1, :]
        o_ref[...] = acc

    return pl.pallas_call(
        body, name=name, grid=(depth, K // tk, cols // tn),
        in_specs=[pl.BlockSpec((tk, nb), lambda l, i, j: (i, 0)),
                  pl.BlockSpec((None, nb, tn), lambda l, i, j: (l, 0, j))],
        out_specs=pl.BlockSpec((None, tk, tn), lambda l, i, j: (l, i, j)),
        out_shape=jax.ShapeDtypeStruct((depth, K, cols), F32),
        compiler_params=_params("parallel", "parallel", "parallel"),
    )(c_all_t, dmod)


def _adamw(w, g, m, v, *, name):
    shape = w.shape
    cols = shape[-1]
    rows = int(np.prod(shape[:-1]))
    t = _tile(rows, 256, SUBLANES) if rows % SUBLANES == 0 else rows
    r2 = lambda a: a.reshape(rows, cols)

    def body(w_ref, g_ref, m_ref, v_ref, d_ref, nm_ref, nv_ref):
        gv = g_ref[...]
        nm = ADAM_B1 * m_ref[...] + (1.0 - ADAM_B1) * gv
        nv = ADAM_B2 * v_ref[...] + (1.0 - ADAM_B2) * (gv * gv)
        m_hat = nm / (1.0 - ADAM_B1 ** ADAM_STEP)
        v_hat = nv / (1.0 - ADAM_B2 ** ADAM_STEP)
        d_ref[...] = -ADAM_LR * (m_hat / (jnp.sqrt(v_hat) + ADAM_EPS) + ADAM_WD * w_ref[...])
        nm_ref[...] = nm
        nv_ref[...] = nv

    spec = pl.BlockSpec((t, cols), lambda i: (i, 0))
    sds = jax.ShapeDtypeStruct((rows, cols), F32)
    d, nm, nv = pl.pallas_call(
        body, name=name, grid=(rows // t,),
        in_specs=[spec] * 4, out_specs=[spec] * 3, out_shape=[sds] * 3,
        compiler_params=_params("parallel"),
    )(r2(w), r2(g), r2(m), r2(v))
    return d.reshape(shape), nm.reshape(shape), nv.reshape(shape)


def _my_place():
    return lax.axis_index("x"), lax.axis_index("y"), lax.axis_index("c")


def _other_chips(x, y):
    return [(1 - x, y), (x, 1 - y), (1 - x, 1 - y)]


def _all_gather8(v, *, name, with_sum=False):
    m, n = v.shape

    def body(x_ref, out_ref, *rest):
        if with_sum:
            sum_ref, send_sems, recv_sems, local_sem = rest
        else:
            send_sems, recv_sems, local_sem = rest
        x, y, c = _my_place()
        me, sibling = (x, y, c), (x, y, 1 - c)
        chips = _other_chips(x, y)

        def rows(px, py, pc):
            return out_ref.at[pl.ds((4 * px + 2 * py + pc) * m, m), :]

        def copy(k, block, to, src=None):
            return pltpu.make_async_remote_copy(
                src_ref=rows(*block) if src is None else src, dst_ref=rows(*block),
                send_sem=send_sems.at[k], recv_sem=recv_sems.at[k], device_id=to, device_id_type=MESH)

        mine = pltpu.make_async_copy(x_ref, rows(*me), local_sem)
        mine.start()
        first = [copy(0, me, sibling, src=x_ref)]
        first += [copy(1 + j, me, (*chip, c), src=x_ref) for j, chip in enumerate(chips)]
        for cp in first:
            cp.start()
        passed = [copy(4 + j, (*chip, c), sibling) for j, chip in enumerate(chips)]
        for j, chip in enumerate(chips):
            copy(1 + j, (*chip, c), me).wait_recv()
            passed[j].start()
        copy(0, sibling, me).wait_recv()
        for j, chip in enumerate(chips):
            copy(4 + j, (*chip, 1 - c), me).wait_recv()
        for cp in first + passed:
            cp.wait_send()
        mine.wait()
        if with_sum:
            acc = out_ref[pl.ds(0, m), :]
            for d in range(1, N_DEV):
                acc = acc + out_ref[pl.ds(d * m, m), :]
            sum_ref[...] = acc

    vm = pl.BlockSpec(memory_space=pltpu.VMEM)
    out_shape = [jax.ShapeDtypeStruct((N_DEV * m, n), F32)]
    if with_sum:
        out_shape.append(jax.ShapeDtypeStruct((m, n), F32))
    res = pl.pallas_call(
        body, name=name, out_shape=out_shape, in_specs=[vm], out_specs=[vm] * len(out_shape),
        scratch_shapes=[pltpu.SemaphoreType.DMA((7,)), pltpu.SemaphoreType.DMA((7,)), pltpu.SemaphoreType.DMA],
        compiler_params=pltpu.CompilerParams(vmem_limit_bytes=VMEM_LIMIT),
    )(v)
    return res if with_sum else res[0]


def _gather_shards(shard, *, name):
    R, n = shard.shape
    half = R // 2

    def body(s_ref, out_ref, send_sems, recv_sems, local_sem):
        x, y, c = _my_place()
        sibling = (x, y, 1 - c)
        chips = _other_chips(x, y)

        def part(px, py, pc):
            return out_ref.at[2 * px + py, pl.ds(pc * half, half), :]

        def copy(k, block, to, src=None):
            return pltpu.make_async_remote_copy(
                src_ref=part(*block) if src is None else src, dst_ref=part(*block),
                send_sem=send_sems.at[k], recv_sem=recv_sems.at[k], device_id=to, device_id_type=MESH)

        mine = pltpu.make_async_copy(s_ref, out_ref.at[2 * x + y], local_sem)
        mine.start()
        my_half = s_ref.at[pl.ds(c * half, half), :]
        first = [copy(j, (x, y, c), (*chip, c), src=my_half) for j, chip in enumerate(chips)]
        for cp in first:
            cp.start()
        passed = [copy(3 + j, (*chip, c), sibling) for j, chip in enumerate(chips)]
        for j, chip in enumerate(chips):
            copy(j, (*chip, c), (x, y, c)).wait_recv()
            passed[j].start()
        for j, chip in enumerate(chips):
            copy(3 + j, (*chip, 1 - c), (x, y, c)).wait_recv()
        for cp in first + passed:
            cp.wait_send()
        mine.wait()

    hbm = pl.BlockSpec(memory_space=pltpu.HBM)
    return pl.pallas_call(
        body, name=name, out_shape=jax.ShapeDtypeStruct((N_CHIPS, R, n), shard.dtype),
        in_specs=[hbm], out_specs=hbm,
        scratch_shapes=[pltpu.SemaphoreType.DMA((6,)), pltpu.SemaphoreType.DMA((6,)), pltpu.SemaphoreType.DMA],
    )(shard)


def _swap_halves(g, *, name):
    _, R, n = g.shape
    half = R // 2

    def body(g_ref, out_ref, send_sem, recv_sem):
        x, y, c = _my_place()
        cp = pltpu.make_async_remote_copy(
            src_ref=g_ref.at[:, pl.ds((1 - c) * half, half), :], dst_ref=out_ref,
            send_sem=send_sem, recv_sem=recv_sem, device_id=(x, y, 1 - c), device_id_type=MESH)
        cp.start()
        cp.wait()

    hbm = pl.BlockSpec(memory_space=pltpu.HBM)
    return pl.pallas_call(
        body, name=name, out_shape=jax.ShapeDtypeStruct((N_CHIPS, half, n), g.dtype),
        in_specs=[hbm], out_specs=hbm,
        scratch_shapes=[pltpu.SemaphoreType.DMA, pltpu.SemaphoreType.DMA],
    )(g)


def _send_partials(s, *, name):
    _, H, n = s.shape

    def body(s_ref, out_ref, send_sems, recv_sems):
        x, y, c = _my_place()
        chips = _other_chips(x, y)
        cps = [pltpu.make_async_remote_copy(
            src_ref=s_ref.at[2 * chip[0] + chip[1]], dst_ref=out_ref.at[j],
            send_sem=send_sems.at[j], recv_sem=recv_sems.at[j], device_id=(*chip, c), device_id_type=MESH)
            for j, chip in enumerate(chips)]
        for cp in cps:
            cp.start()
        for cp in cps:
            cp.wait()

    hbm = pl.BlockSpec(memory_space=pltpu.HBM)
    return pl.pallas_call(
        body, name=name, out_shape=jax.ShapeDtypeStruct((3, H, n), s.dtype),
        in_specs=[hbm], out_specs=hbm,
        scratch_shapes=[pltpu.SemaphoreType.DMA((3,)), pltpu.SemaphoreType.DMA((3,))],
    )(s)


def _share_halves(r, *, name):
    H, n = r.shape

    def body(r_ref, out_ref, send_sem, recv_sem, local_sem):
        x, y, c = _my_place()
        mine = pltpu.make_async_copy(r_ref, out_ref.at[pl.ds(c * H, H), :], local_sem)
        mine.start()
        cp = pltpu.make_async_remote_copy(
            src_ref=r_ref, dst_ref=out_ref.at[pl.ds(c * H, H), :],
            send_sem=send_sem, recv_sem=recv_sem, device_id=(x, y, 1 - c), device_id_type=MESH)
        cp.start()
        cp.wait_send()
        pltpu.make_async_remote_copy(
            src_ref=r_ref, dst_ref=out_ref.at[pl.ds((1 - c) * H, H), :],
            send_sem=send_sem, recv_sem=recv_sem, device_id=(x, y, 1 - c), device_id_type=MESH).wait_recv()
        mine.wait()

    hbm = pl.BlockSpec(memory_space=pltpu.HBM)
    return pl.pallas_call(
        body, name=name, out_shape=jax.ShapeDtypeStruct((2 * H, n), r.dtype),
        in_specs=[hbm], out_specs=hbm,
        scratch_shapes=[pltpu.SemaphoreType.DMA, pltpu.SemaphoreType.DMA, pltpu.SemaphoreType.DMA],
    )(r)


def _add_halves(g, r1, c_idx, *, name):
    _, R, n = g.shape
    half = R // 2
    t = _tile(half, 192, 16)
    nb = half // t

    def body(c_ref, g_ref, r_ref, of_ref, ob_ref):
        s = g_ref[...] + r_ref[...]
        of_ref[...] = s
        ob_ref[...] = s.astype(BF16)

    grid_spec = pltpu.PrefetchScalarGridSpec(
        num_scalar_prefetch=1, grid=(N_CHIPS, nb),
        in_specs=[pl.BlockSpec((None, t, n), lambda k, i, c: (k, c[0] * nb + i, 0)),
                  pl.BlockSpec((None, t, n), lambda k, i, c: (k, i, 0))],
        out_specs=[pl.BlockSpec((None, t, n), lambda k, i, c: (k, i, 0))] * 2)
    return pl.pallas_call(
        body, name=name, grid_spec=grid_spec,
        out_shape=[jax.ShapeDtypeStruct((N_CHIPS, half, n), F32), jax.ShapeDtypeStruct((N_CHIPS, half, n), BF16)],
        compiler_params=_params("parallel", "parallel"),
    )(c_idx, g, r1)


def _sum_partials(own, parts, *, name):
    H, n = own.shape
    t = _tile(H, 192, 16)

    def body(o_ref, p_ref, out_ref):
        acc = o_ref[...]
        for j in range(3):
            acc = acc + p_ref[j].astype(F32)
        out_ref[...] = acc

    return pl.pallas_call(
        body, name=name, grid=(H // t,),
        in_specs=[pl.BlockSpec((t, n), lambda i: (i, 0)), pl.BlockSpec((3, t, n), lambda i: (0, i, 0))],
        out_specs=pl.BlockSpec((t, n), lambda i: (i, 0)), out_shape=jax.ShapeDtypeStruct((H, n), F32),
        compiler_params=_params("parallel"),
    )(own, parts)


PACK_COLS = 1024
PACK_ROW_ALIGN = 64
PACKED = (
    ("ffn_w_in", (DEPTH, 2, D_MODEL, 2 * D_FF // N_CHIPS), 3),
    ("ffn_w_out", (DEPTH, 2, D_FF // N_CHIPS, D_MODEL), 2),
    ("mix_w_in", (DEPTH, D_MODEL, (SSM_WIDTH + 3 * ATTN_WIDTH + ATTN_HEADS + 2 * D_MODEL) // N_CHIPS), 2),
    ("glu_w", (DEPTH, SSM_WIDTH, 2 * D_MODEL // N_CHIPS), 2),
    ("attn_w_out", (DEPTH, ATTN_WIDTH, D_MODEL // N_CHIPS), 2),
    ("mix_w_out", (DEPTH, D_MODEL // N_CHIPS, D_MODEL), 1),
)
PACK_SIZE = sum(int(np.prod(s)) for _, s, _ in PACKED)
PACK_ROWS = -(-PACK_SIZE // (PACK_COLS * PACK_ROW_ALIGN)) * PACK_ROW_ALIGN


def _pack(pieces, dtype):
    flat = jnp.concatenate([p.reshape(-1).astype(dtype) for p in pieces])
    flat = jnp.pad(flat, (0, PACK_ROWS * PACK_COLS - PACK_SIZE))
    return flat.reshape(PACK_ROWS, PACK_COLS)


def _unpack(flat2d):
    flat = flat2d.reshape(-1)
    out, off = {}, 0
    for name, shape, _ in PACKED:
        size = int(np.prod(shape))
        out[name] = flat[off:off + size].reshape(shape)
        off += size
    return out


def _unpack_full(gathered):
    per_chip = [_unpack(gathered[k]) for k in range(N_CHIPS)]
    return {name: jnp.concatenate([per_chip[k][name] for k in range(N_CHIPS)], axis=axis)
            for name, _, axis in PACKED}


def _pack_by_chip(full, dtype):
    slots = []
    for k in range(N_CHIPS):
        pieces = []
        for name, shape, axis in PACKED:
            w = shape[axis]
            pieces.append(lax.slice_in_dim(full[name], k * w, (k + 1) * w, axis=axis))
        slots.append(_pack(pieces, dtype))
    return jnp.stack(slots)


def _pad_rows(flat, cols=PACK_COLS, align=SUBLANES):
    n = flat.shape[0]
    rows = -(-n // (cols * align)) * align
    return jnp.pad(flat, (0, rows * cols - n)).reshape(rows, cols)


def _row(v):
    return v.reshape(1, -1)


def _ffn_fwd(x, mod, g_pre, g_post, wg, wu, w_out, tag):
    sh, sc, gate = _row(mod[0]), _row(mod[1]), _row(mod[2])
    h = _prenorm(x, _row(g_pre), sc, sh, name=f"prenorm_{tag}")
    gt, up, act = _mm_swiglu(h, wg, wu, name=f"swiglu_{tag}")
    y = _mm(act, w_out, name=f"ffn_out_{tag}")
    x_out = _postnorm_res(x, y, _row(g_post), gate, FFN_RES, name=f"postnorm_{tag}")
    return x_out, (x, h, gt, up, act, y)


def _ffn_bwd(dxo, saved, mod, g_pre, g_post, wg, wu, w_out, tag):
    x, h, gt, up, act, y = saved
    sc, gate = _row(mod[1]), _row(mod[2])
    dy, dgate, dgpost = _postnorm_bwd(dxo, y, _row(g_post), gate, FFN_RES, name=f"postnorm_bwd_{tag}")
    dgt, dup = _mm_swiglu_bwd(dy, w_out, gt, up, name=f"swiglu_bwd_{tag}")
    dw_out = _mm(act, dy, ta=True, name=f"dw_out_{tag}", tm=1408, tn=1024, tk=1024)
    dh = _mm(dgt, wg, tb=True, name=f"dh_gate_{tag}", tk=1408)
    dh = _mm(dup, wu, tb=True, bias=dh, name=f"dh_up_{tag}", tk=1408)
    dwg = _mm(h, dgt, ta=True, name=f"dw_gate_{tag}", tm=1024, tn=1408, tk=1024)
    dwu = _mm(h, dup, ta=True, name=f"dw_up_{tag}", tm=1024, tn=1408, tk=1024)
    dx, dsh, dsc, dgpre = _prenorm_bwd(x, dh, dxo, _row(g_pre), sc, name=f"prenorm_bwd_{tag}")
    dmod = jnp.stack([dsh[0], dsc[0], dgate[0]])
    return dx, dmod, dgpre[0], dgpost[0], jnp.concatenate([dwg, dwu], axis=1), dw_out


def _split_mix_w_in(w):
    u0, q0, f0, g0 = 0, SSM_WIDTH, SSM_WIDTH + 3 * ATTN_WIDTH, SSM_WIDTH + 3 * ATTN_WIDTH + ATTN_HEADS
    w_f = jnp.pad(w[:, f0:g0], ((0, 0), (0, LANES - ATTN_HEADS)))
    return w[:, u0:q0], w[:, q0:f0], w_f, w[:, g0:]


def _mixer_fwd(x, mod, g_pre, g_post, w, ssm, forget_b, tag):
    L = x.shape[0]
    sh, sc, gate = _row(mod[0]), _row(mod[1]), _row(mod[2])
    lam_re, lam_im, bin_re, bin_im, cout_re, cout_im, dskip = ssm
    h = _prenorm(x, _row(g_pre), sc, sh, name=f"prenorm_{tag}")
    u = _mm(h, w["u"], name=f"proj_u_{tag}")
    qkv = _mm(h, w["qkv"], out_dtype=BF16, name=f"proj_qkv_{tag}")
    f = _mm(h, w["f"], name=f"proj_f_{tag}")
    gab = _mm(h, w["gab"], name=f"proj_gab_{tag}")
    bu_re = _mm(u, bin_re, name=f"ssm_bu_re_{tag}")
    bu_im = _mm(u, bin_im, name=f"ssm_bu_im_{tag}")
    s_re, s_im = _ssm_scan_fwd(bu_re, bu_im, lam_re, lam_im, name=f"ssm_scan_{tag}")
    y_ssm = _mm(s_re, cout_re, bias=u, bscale=dskip, name=f"ssm_y_re_{tag}", tk=2048)
    y_ssm = _mm(s_im, cout_im, bias=y_ssm, name=f"ssm_y_im_{tag}", tk=2048)
    gl = _gelu_fwd(y_ssm, name=f"gelu_{tag}")
    z = _mm(gl, w["glu"], name=f"glu_{tag}")
    fb = jnp.pad(forget_b, (0, LANES - ATTN_HEADS)).reshape(1, LANES)
    cum = _fox_cum(f, fb, name=f"fox_cum_{tag}")
    cum8 = cum[:, :ATTN_HEADS]
    cum_cols = jnp.repeat(cum8, HEAD_DIM, axis=1)
    cum_rows = jnp.pad(cum8.T.reshape(ATTN_HEADS // 2, 2, L), ((0, 0), (0, SUBLANES - 2), (0, 0)))
    attn, attn32, lse = _fox_fwd(qkv, cum_cols, cum_rows, name=f"fox_fwd_{tag}")
    yb = _mm(attn, w["attn_out"], name=f"attn_out_{tag}")
    merged = _merge_fwd(z, yb, gab, name=f"merge_{tag}")
    y = _mm(merged, w["out"], name=f"mix_out_{tag}")
    x_out = _postnorm_res(x, y, _row(g_post), gate, 1.0, name=f"postnorm_{tag}")
    saved = (x, h, u, qkv, f, gab, s_re, s_im, y_ssm, gl, z, fb, cum_cols, cum_rows, attn, attn32, lse, yb, merged, y)
    return x_out, saved


def _mixer_bwd(dxo, saved, mod, g_pre, g_post, w, ssm, tag):
    x, h, u, qkv, f, gab, s_re, s_im, y_ssm, gl, z, fb, cum_cols, cum_rows, attn, attn32, lse, yb, merged, y = saved
    L = x.shape[0]
    sc, gate = _row(mod[1]), _row(mod[2])
    lam_re, lam_im, bin_re, bin_im, cout_re, cout_im, dskip = ssm
    dy, dgate, dgpost = _postnorm_bwd(dxo, y, _row(g_post), gate, 1.0, name=f"postnorm_bwd_{tag}")
    dmerged = _mm(dy, w["out"], tb=True, name=f"dmerged_{tag}")
    dw_out = _mm(merged, dy, ta=True, name=f"dw_mix_out_{tag}", tm=1024, tn=1024)
    dz, dyb, dgab = _merge_bwd(dmerged, z, yb, gab, name=f"merge_bwd_{tag}")
    dgl = _mm(dz, w["glu"], tb=True, name=f"dgl_{tag}", tk=2048)
    dw_glu = _mm(gl, dz, ta=True, name=f"dw_glu_{tag}", tm=512, tn=2048)
    dys, dsk, dd = _gelu_bwd(dgl, y_ssm, u, dskip, name=f"gelu_bwd_{tag}")
    d_re = _mm(dys, cout_re, tb=True, name=f"ssm_ds_re_{tag}")
    d_im = _mm(dys, cout_im, tb=True, name=f"ssm_ds_im_{tag}")
    dcout_re = _mm(s_re, dys, ta=True, name=f"ssm_dc_re_{tag}", tm=1024, tn=512)
    dcout_im = _mm(s_im, dys, ta=True, name=f"ssm_dc_im_{tag}", tm=1024, tn=512)
    g_re, g_im, dlam_re, dlam_im = _ssm_scan_bwd(d_re, d_im, s_re, s_im, lam_re, lam_im, name=f"ssm_scan_bwd_{tag}")
    du = _mm(g_re, bin_re, tb=True, bias=dsk, name=f"ssm_du_re_{tag}", tk=2048)
    du = _mm(g_im, bin_im, tb=True, bias=du, out_dtype=BF16, name=f"ssm_du_im_{tag}", tk=2048)
    dbin_re = _mm(u, g_re, ta=True, name=f"ssm_db_re_{tag}", tm=512, tn=1024)
    dbin_im = _mm(u, g_im, ta=True, name=f"ssm_db_im_{tag}", tm=512, tn=1024)
    dssm = (dlam_re, dlam_im, dbin_re, dbin_im, dcout_re, dcout_im, _row(dd[0]))
    dattn = _mm(dyb, w["attn_out"], tb=True, out_dtype=BF16, name=f"dattn_{tag}")
    dw_attn = _mm(attn, dyb, ta=True, name=f"dw_attn_out_{tag}", tm=512, tn=1024)
    dk, dv, dcum_rows = _fox_bwd_kv(qkv, dattn, attn32, lse, cum_cols, cum_rows, name=f"fox_bwd_kv_{tag}")
    dq, drow = _fox_bwd_q(qkv, dattn, attn32, lse, cum_cols, cum_rows, name=f"fox_bwd_q_{tag}")
    dcum = drow[:, ::HEAD_DIM] + dcum_rows[:, :2, :].reshape(ATTN_HEADS, L).T
    dcum = jnp.pad(dcum, ((0, 0), (0, LANES - ATTN_HEADS)))
    df, dfb = _fox_cum_bwd(dcum, f, fb, name=f"fox_cum_bwd_{tag}")
    dqkv = jnp.concatenate([dq, dk, dv], axis=1)
    dh = _mm(dqkv, w["qkv"], tb=True, name=f"dh_qkv_{tag}", tk=1536)
    dh = _mm(du, w["u"], tb=True, bias=dh, name=f"dh_u_{tag}")
    dh = _mm(dgab, w["gab"], tb=True, bias=dh, name=f"dh_gab_{tag}", tk=2048)
    dh = _mm(df, w["f"], tb=True, bias=dh, name=f"dh_f_{tag}")
    dw_u = _mm(h, du, ta=True, name=f"dw_u_{tag}", tm=1024, tn=512)
    dw_qkv = _mm(h, dqkv, ta=True, name=f"dw_qkv_{tag}", tm=1024, tn=1536)
    dw_f = _mm(h, df, ta=True, name=f"dw_f_{tag}", tm=1024)
    dw_gab = _mm(h, dgab, ta=True, name=f"dw_gab_{tag}", tm=1024, tn=1024)
    dw_in = jnp.concatenate([dw_u, dw_qkv, dw_f[:, :ATTN_HEADS], dw_gab], axis=1)
    dx, dsh, dsc, dgpre = _prenorm_bwd(x, dh, dxo, _row(g_pre), sc, name=f"prenorm_bwd_{tag}")
    dmod = jnp.stack([dsh[0], dsc[0], dgate[0]])
    grads = dict(mix_w_in=dw_in, glu_w=dw_glu, attn_w_out=dw_attn, mix_w_out=dw_out,
                 forget_b=dfb[0, :ATTN_HEADS])
    return dx, dmod, dgpre[0], dgpost[0], grads, dssm


SSM_NAMES = ("ssm_a_re", "ssm_a_im", "ssm_log_dt", "ssm_b_re", "ssm_b_im", "ssm_c_re", "ssm_c_im", "ssm_d")
SMALL_NAMES = ("forget_b",) + SSM_NAMES
WEIGHT_NAMES = ("mod_w", "mod_b", "norm_pre", "norm_post", "ffn_w_in", "ffn_w_out", "mix_w_in", "forget_b") \
    + SSM_NAMES + ("glu_w", "attn_w_out", "mix_w_out")


def _train_step(x, c, target, w, m, v):
    xi, yi, ci = _my_place()
    chip = 2 * xi + yi
    dev = 4 * xi + 2 * yi + ci
    mod_cols = N_SUB * 3 * D_MODEL // N_CHIPS
    norm_cols = D_MODEL // N_CHIPS

    gathered = _gather_shards(_pack([w[n] for n, _, _ in PACKED], BF16), name="gather_weights")
    c_all = _all_gather8(jnp.pad(c, ((0, SUBLANES - 1), (0, 0))), name="gather_c")[::SUBLANES]
    mod_b_cols = lax.dynamic_slice_in_dim(w["mod_b"], chip * mod_cols, mod_cols, axis=1)[:, None, :]
    mod_part = _mod_partial(c_all, w["mod_w"], mod_b_cols, name="mod_partial")
    small_fwd = jnp.concatenate([mod_part.reshape(-1), w["norm_pre"].reshape(-1), w["norm_post"].reshape(-1)])
    n_mod, n_norm = mod_part.size, w["norm_pre"].size
    sf_all = _all_gather8(_pad_rows(small_fwd), name="gather_mod").reshape(N_DEV, -1)
    sf_chips = sf_all[::2]
    mod_all = jnp.concatenate(
        [sf_chips[k, :n_mod].reshape(DEPTH, N_DEV, mod_cols) for k in range(N_CHIPS)], axis=2)
    mod_mine = lax.dynamic_index_in_dim(mod_all, dev, axis=1, keepdims=False).reshape(DEPTH, N_SUB, 3, D_MODEL)
    norm_pre = jnp.concatenate(
        [sf_chips[k, n_mod:n_mod + n_norm].reshape(DEPTH, N_SUB, norm_cols) for k in range(N_CHIPS)], axis=2)
    norm_post = jnp.concatenate(
        [sf_chips[k, n_mod + n_norm:n_mod + 2 * n_norm].reshape(DEPTH, N_SUB, norm_cols) for k in range(N_CHIPS)],
        axis=2)
    full = _unpack_full(gathered)

    saved, layer_w, ssm_prep, ssm_vjp = [], [], [], []
    h = x
    for l in range(DEPTH):
        w_u, w_qkv, w_f, w_gab = _split_mix_w_in(full["mix_w_in"][l])
        lw = dict(
            ffn=[(full["ffn_w_in"][l, j][:, :D_FF], full["ffn_w_in"][l, j][:, D_FF:], full["ffn_w_out"][l, j])
                 for j in range(2)],
            mix=dict(u=w_u, qkv=w_qkv, f=w_f, gab=w_gab, glu=full["glu_w"][l], attn_out=full["attn_w_out"][l],
                     out=full["mix_w_out"][l]))
        prep, vjp = jax.vjp(_ssm_discretize, *[w[n][l] for n in SSM_NAMES])
        layer_w.append(lw)
        ssm_prep.append(prep)
        ssm_vjp.append(vjp)
        h, s0 = _ffn_fwd(h, mod_mine[l, 0], norm_pre[l, 0], norm_post[l, 0], *lw["ffn"][0], tag=f"l{l}a")
        h, s1 = _mixer_fwd(h, mod_mine[l, 1], norm_pre[l, 1], norm_post[l, 1], lw["mix"], prep,
                           w["forget_b"][l], tag=f"l{l}m")
        h, s2 = _ffn_fwd(h, mod_mine[l, 2], norm_pre[l, 2], norm_post[l, 2], *lw["ffn"][1], tag=f"l{l}b")
        saved.append((s0, s1, s2))
    dh, loss8 = _loss_head(h, target, name="loss_head")

    g_full = {n: [None] * DEPTH for n, _, _ in PACKED}
    g_small = {n: [None] * DEPTH for n in SMALL_NAMES}
    dmod, dnpre, dnpost = [], [], []
    for l in reversed(range(DEPTH)):
        lw = layer_w[l]
        dh, dm2, dp2, dq2, dwin2, dwout2 = _ffn_bwd(dh, saved[l][2], mod_mine[l, 2], norm_pre[l, 2],
                                                    norm_post[l, 2], *lw["ffn"][1], tag=f"l{l}b")
        dh, dm1, dp1, dq1, gmix, dssm = _mixer_bwd(dh, saved[l][1], mod_mine[l, 1], norm_pre[l, 1],
                                                   norm_post[l, 1], lw["mix"], ssm_prep[l], tag=f"l{l}m")
        dh, dm0, dp0, dq0, dwin0, dwout0 = _ffn_bwd(dh, saved[l][0], mod_mine[l, 0], norm_pre[l, 0],
                                                    norm_post[l, 0], *lw["ffn"][0], tag=f"l{l}a")
        dmod.insert(0, jnp.stack([dm0, dm1, dm2]))
        dnpre.insert(0, jnp.stack([dp0, dp1, dp2]))
        dnpost.insert(0, jnp.stack([dq0, dq1, dq2]))
        g_full["ffn_w_in"][l] = jnp.stack([dwin0, dwin2])
        g_full["ffn_w_out"][l] = jnp.stack([dwout0, dwout2])
        for n in ("mix_w_in", "glu_w", "attn_w_out", "mix_w_out"):
            g_full[n][l] = gmix[n]
        g_small["forget_b"][l] = gmix["forget_b"]
        for n, g in zip(SSM_NAMES, ssm_vjp[l](dssm)):
            g_small[n][l] = g
    grad_x = dh
    g_full = {n: jnp.stack(g) for n, g in g_full.items()}
    g_small = {n: jnp.stack(g) for n, g in g_small.items()}

    small = [loss8[0, :1], jnp.stack(dmod).reshape(-1), jnp.stack(dnpre).reshape(-1), jnp.stack(dnpost).reshape(-1)]
    small += [g_small[n].reshape(-1) for n in SMALL_NAMES]
    sizes = [int(s.size) for s in small]
    offs = np.concatenate([[0], np.cumsum(sizes)])
    sb_all, sb_sum = _all_gather8(_pad_rows(jnp.concatenate(small)), name="gather_small_grads", with_sum=True)
    sb_sum = sb_sum.reshape(-1)
    take = lambda i: sb_sum[int(offs[i]):int(offs[i + 1])]
    loss = take(0)[0]
    grads = {"mod_b": take(1).reshape(DEPTH, N_SUB * 3 * D_MODEL)}
    dnorm_pre_full = take(2).reshape(DEPTH, N_SUB, D_MODEL)
    dnorm_post_full = take(3).reshape(DEPTH, N_SUB, D_MODEL)
    grads["norm_pre"] = lax.dynamic_slice_in_dim(dnorm_pre_full, chip * norm_cols, norm_cols, axis=2)
    grads["norm_post"] = lax.dynamic_slice_in_dim(dnorm_post_full, chip * norm_cols, norm_cols, axis=2)
    for i, n in enumerate(SMALL_NAMES):
        grads[n] = take(4 + i).reshape(w[n].shape)
    dmod_all = sb_all.reshape(N_DEV, -1)[:, int(offs[1]):int(offs[2])].reshape(N_DEV, DEPTH, N_SUB * 3 * D_MODEL)
    dmod_cols = lax.dynamic_slice_in_dim(dmod_all, chip * mod_cols, mod_cols, axis=2).transpose(1, 0, 2)
    grads["mod_w"] = _mod_wgrad(c_all.T, dmod_cols, name="mod_wgrad")

    g_slots = _pack_by_chip(g_full, F32)
    from_sibling = _swap_halves(g_slots, name="grad_swap_halves")
    sum_f32, sum_bf16 = _add_halves(g_slots, from_sibling, ci.reshape(1).astype(jnp.int32), name="grad_add_halves")
    parts = _send_partials(sum_bf16, name="grad_send_partials")
    own = lax.dynamic_index_in_dim(sum_f32, chip, axis=0, keepdims=False)
    reduced_half = _sum_partials(own, parts, name="grad_sum_partials")
    reduced = _share_halves(reduced_half, name="grad_share_halves")
    grads.update(_unpack(reduced))

    delta, new_m, new_v = {}, {}, {}
    for n in WEIGHT_NAMES:
        delta[n], new_m[n], new_v[n] = _adamw(w[n], grads[n], m[n], v[n], name=f"adamw_{n}")
    outs = [loss, grad_x[None]]
    for group in (grads, delta, new_m, new_v):
        outs += [group[n] for n in WEIGHT_NAMES]
    return tuple(outs)


def kernel(x, c, mod_w, mod_b, norm_pre, norm_post, ffn_w_in, ffn_w_out, mix_w_in, forget_b, ssm_a_re, ssm_a_im, ssm_log_dt, ssm_b_re, ssm_b_im, ssm_c_re, ssm_c_im, ssm_d, glu_w, attn_w_out, mix_w_out, loss_target, m_mod_w, m_mod_b, m_norm_pre, m_norm_post, m_ffn_w_in, m_ffn_w_out, m_mix_w_in, m_forget_b, m_ssm_a_re, m_ssm_a_im, m_ssm_log_dt, m_ssm_b_re, m_ssm_b_im, m_ssm_c_re, m_ssm_c_im, m_ssm_d, m_glu_w, m_attn_w_out, m_mix_w_out, v_mod_w, v_mod_b, v_norm_pre, v_norm_post, v_ffn_w_in, v_ffn_w_out, v_mix_w_in, v_forget_b, v_ssm_a_re, v_ssm_a_im, v_ssm_log_dt, v_ssm_b_re, v_ssm_b_im, v_ssm_c_re, v_ssm_c_im, v_ssm_d, v_glu_w, v_attn_w_out, v_mix_w_out):
    w = dict(mod_w=mod_w, mod_b=mod_b, norm_pre=norm_pre, norm_post=norm_post, ffn_w_in=ffn_w_in,
             ffn_w_out=ffn_w_out, mix_w_in=mix_w_in, forget_b=forget_b, ssm_a_re=ssm_a_re, ssm_a_im=ssm_a_im,
             ssm_log_dt=ssm_log_dt, ssm_b_re=ssm_b_re, ssm_b_im=ssm_b_im, ssm_c_re=ssm_c_re, ssm_c_im=ssm_c_im,
             ssm_d=ssm_d, glu_w=glu_w, attn_w_out=attn_w_out, mix_w_out=mix_w_out)
    m = dict(mod_w=m_mod_w, mod_b=m_mod_b, norm_pre=m_norm_pre, norm_post=m_norm_post, ffn_w_in=m_ffn_w_in,
             ffn_w_out=m_ffn_w_out, mix_w_in=m_mix_w_in, forget_b=m_forget_b, ssm_a_re=m_ssm_a_re,
             ssm_a_im=m_ssm_a_im, ssm_log_dt=m_ssm_log_dt, ssm_b_re=m_ssm_b_re, ssm_b_im=m_ssm_b_im,
             ssm_c_re=m_ssm_c_re, ssm_c_im=m_ssm_c_im, ssm_d=m_ssm_d, glu_w=m_glu_w, attn_w_out=m_attn_w_out,
             mix_w_out=m_mix_w_out)
    v = dict(mod_w=v_mod_w, mod_b=v_mod_b, norm_pre=v_norm_pre, norm_post=v_norm_post, ffn_w_in=v_ffn_w_in,
             ffn_w_out=v_ffn_w_out, mix_w_in=v_mix_w_in, forget_b=v_forget_b, ssm_a_re=v_ssm_a_re,
             ssm_a_im=v_ssm_a_im, ssm_log_dt=v_ssm_log_dt, ssm_b_re=v_ssm_b_re, ssm_b_im=v_ssm_b_im,
             ssm_c_re=v_ssm_c_re, ssm_c_im=v_ssm_c_im, ssm_d=v_ssm_d, glu_w=v_glu_w, attn_w_out=v_attn_w_out,
             mix_w_out=v_mix_w_out)
    return _train_step(x[0], c, loss_target[0], w, m, v)
```

```python
import functools
import math

import jax
import jax.numpy as jnp
import numpy as np
from jax import lax
from jax.experimental import pallas as pl
from jax.experimental.pallas import tpu as pltpu

F32 = jnp.float32
BF16 = jnp.bfloat16

D_MODEL = 1024
DEPTH = 2
SSM_WIDTH = 512
SSM_GROUP = 16
SSM_GROUPS = 32
SSM_STATE = 64
SSM_FLAT = SSM_GROUPS * SSM_STATE
ATTN_HEADS = 8
HEAD_DIM = 64
ATTN_WIDTH = 512
D_FF = 2816
FFN_RES = 0.5
N_SUB = 3
RMS_EPS = 1e-6
N_CHIPS = 4
N_DEV = 8

ADAM_LR = 0.001
ADAM_B1 = 0.9
ADAM_B2 = 0.999
ADAM_EPS = 1e-08
ADAM_WD = 0.01
ADAM_STEP = 10

LANES = 128
SUBLANES = 8
VMEM_LIMIT = 52 * 1024 * 1024
MESH = pl.DeviceIdType.MESH

NN = (((1,), (0,)), ((), ()))
NT = (((1,), (1,)), ((), ()))
TN = (((0,), (0,)), ((), ()))


def _tile(dim, target, align=LANES):
    best = None
    t = align
    while t <= min(dim, target):
        if dim % t == 0:
            best = t
        t += align
    return dim if best is None else best


def _params(*sem):
    return pltpu.CompilerParams(dimension_semantics=sem, vmem_limit_bytes=VMEM_LIMIT)


def _mm(a, b, *, name, ta=False, tb=False, out_dtype=F32, bias=None, bscale=None,
        tm=512, tn=1024, tk=1024):
    M, K = (a.shape[1], a.shape[0]) if ta else a.shape
    N = b.shape[0] if tb else b.shape[1]
    assert K == (b.shape[1] if tb else b.shape[0]), (a.shape, b.shape, ta, tb)
    tm, tn, tk = _tile(M, tm), _tile(N, tn), _tile(K, tk)
    nk = K // tk
    dn = (((0 if ta else 1,), (1 if tb else 0,)), ((), ()))
    has_bias, has_scale = bias is not None, bscale is not None

    def body(*refs):
        a_ref, b_ref = refs[0], refs[1]
        pos = 2
        bias_ref = scale_ref = None
        if has_bias:
            bias_ref = refs[pos]
            pos += 1
        if has_scale:
            scale_ref = refs[pos]
            pos += 1
        o_ref = refs[pos]
        acc_ref = refs[pos + 1] if nk > 1 else None

        def finish(r):
            if has_bias:
                extra = bias_ref[...].astype(F32)
                if has_scale:
                    extra = extra * scale_ref[...]
                r = r + extra
            o_ref[...] = r.astype(out_dtype)

        part = lax.dot_general(a_ref[...].astype(BF16), b_ref[...].astype(BF16), dn,
                               preferred_element_type=F32)
        if nk == 1:
            finish(part)
        else:
            k = pl.program_id(2)

            @pl.when(k == 0)
            def _():
                acc_ref[...] = part

            @pl.when(k > 0)
            def _():
                acc_ref[...] += part

            @pl.when(k == nk - 1)
            def _():
                finish(acc_ref[...])

    a_spec = pl.BlockSpec((tk, tm), lambda j, i, k: (k, i)) if ta else pl.BlockSpec((tm, tk), lambda j, i, k: (i, k))
    b_spec = pl.BlockSpec((tn, tk), lambda j, i, k: (j, k)) if tb else pl.BlockSpec((tk, tn), lambda j, i, k: (k, j))
    in_specs = [a_spec, b_spec]
    args = [a, b]
    if has_bias:
        in_specs.append(pl.BlockSpec((tm, tn), lambda j, i, k: (i, j)))
        args.append(bias)
    if has_scale:
        in_specs.append(pl.BlockSpec((1, tn), lambda j, i, k: (0, j)))
        args.append(bscale)
    return pl.pallas_call(
        body, name=name,
        grid=(N // tn, M // tm, nk),
        in_specs=in_specs,
        out_specs=pl.BlockSpec((tm, tn), lambda j, i, k: (i, j)),
        out_shape=jax.ShapeDtypeStruct((M, N), out_dtype),
        scratch_shapes=[pltpu.VMEM((tm, tn), F32)] if nk > 1 else [],
        compiler_params=_params("parallel", "parallel", "arbitrary"),
    )(*args)


def _sigmoid(x):
    return 1.0 / (1.0 + jnp.exp(-x))


def _mm_swiglu(h, wg, wu, *, name, tm=512, tn=1408):
    M, K = h.shape
    N = wg.shape[1]
    tm, tn = _tile(M, tm), _tile(N, tn)

    def body(h_ref, wg_ref, wu_ref, g_ref, u_ref, a_ref):
        hv = h_ref[...]
        g = jnp.dot(hv, wg_ref[...], preferred_element_type=F32)
        u = jnp.dot(hv, wu_ref[...], preferred_element_type=F32)
        g_ref[...] = g.astype(BF16)
        u_ref[...] = u.astype(BF16)
        a_ref[...] = (g * _sigmoid(g) * u).astype(BF16)

    w_spec = pl.BlockSpec((K, tn), lambda j, i: (0, j))
    o_spec = pl.BlockSpec((tm, tn), lambda j, i: (i, j))
    sds = jax.ShapeDtypeStruct((M, N), BF16)
    return pl.pallas_call(
        body, name=name, grid=(N // tn, M // tm),
        in_specs=[pl.BlockSpec((tm, K), lambda j, i: (i, 0)), w_spec, w_spec],
        out_specs=[o_spec, o_spec, o_spec], out_shape=[sds, sds, sds],
        compiler_params=_params("parallel", "parallel"),
    )(h, wg, wu)


def _mm_swiglu_bwd(dy, w_out, gate, up, *, name, tm=512, tn=1408):
    M, K = dy.shape
    N = w_out.shape[0]
    tm, tn = _tile(M, tm), _tile(N, tn)

    def body(dy_ref, w_ref, g_ref, u_ref, dg_ref, du_ref):
        dact = lax.dot_general(dy_ref[...], w_ref[...], NT, preferred_element_type=F32)
        g = g_ref[...].astype(F32)
        u = u_ref[...].astype(F32)
        sig = _sigmoid(g)
        dg_ref[...] = (dact * u * (sig * (1.0 + g * (1.0 - sig)))).astype(BF16)
        du_ref[...] = (dact * (g * sig)).astype(BF16)

    t_spec = pl.BlockSpec((tm, tn), lambda j, i: (i, j))
    sds = jax.ShapeDtypeStruct((M, N), BF16)
    return pl.pallas_call(
        body, name=name, grid=(N // tn, M // tm),
        in_specs=[pl.BlockSpec((tm, K), lambda j, i: (i, 0)), pl.BlockSpec((tn, K), lambda j, i: (j, 0)),
                  t_spec, t_spec],
        out_specs=[t_spec, t_spec], out_shape=[sds, sds],
        compiler_params=_params("parallel", "parallel"),
    )(dy, w_out, gate, up)


ROW_TILE = 256


def _colsum8(v):
    return jnp.sum(v.reshape(v.shape[0] // SUBLANES, SUBLANES, v.shape[1]), axis=0)


def _finish_colsums(step, last, refs):
    @pl.when(step == last)
    def _():
        for r in refs:
            r[...] = jnp.broadcast_to(jnp.sum(r[...], axis=0, keepdims=True), r.shape)


def _row_spec(t, d):
    return pl.BlockSpec((t, d), lambda i: (i, 0))


def _vec_spec(d, rows=1):
    return pl.BlockSpec((rows, d), lambda i: (0, 0))


def _prenorm(x, g, sc, sh, *, name):
    L, D = x.shape
    t = _tile(L, ROW_TILE, SUBLANES)

    def body(x_ref, g_ref, sc_ref, sh_ref, h_ref):
        xv = x_ref[...]
        r = lax.rsqrt(jnp.mean(xv * xv, axis=-1, keepdims=True) + RMS_EPS)
        h_ref[...] = (((xv * r) * g_ref[...]) * (1.0 + sc_ref[...]) + sh_ref[...]).astype(BF16)

    return pl.pallas_call(
        body, name=name, grid=(L // t,),
        in_specs=[_row_spec(t, D), _vec_spec(D), _vec_spec(D), _vec_spec(D)],
        out_specs=_row_spec(t, D), out_shape=jax.ShapeDtypeStruct((L, D), BF16),
        compiler_params=_params("parallel"),
    )(x, g, sc, sh)


def _postnorm_res(x, y, g, gate, res_w, *, name):
    L, D = x.shape
    t = _tile(L, ROW_TILE, SUBLANES)

    def body(x_ref, y_ref, g_ref, gate_ref, o_ref):
        yv = y_ref[...]
        r = lax.rsqrt(jnp.mean(yv * yv, axis=-1, keepdims=True) + RMS_EPS)
        o_ref[...] = x_ref[...] + (res_w * gate_ref[...]) * ((yv * r) * g_ref[...])

    return pl.pallas_call(
        body, name=name, grid=(L // t,),
        in_specs=[_row_spec(t, D), _row_spec(t, D), _vec_spec(D), _vec_spec(D)],
        out_specs=_row_spec(t, D), out_shape=jax.ShapeDtypeStruct((L, D), F32),
        compiler_params=_params("parallel"),
    )(x, y, g, gate)


def _postnorm_bwd(dxo, y, g, gate, res_w, *, name):
    L, D = y.shape
    t = _tile(L, ROW_TILE, SUBLANES)
    n = L // t

    def body(dxo_ref, y_ref, g_ref, gate_ref, dy_ref, dgate_ref, dg_ref):
        i = pl.program_id(0)

        @pl.when(i == 0)
        def _():
            dgate_ref[...] = jnp.zeros_like(dgate_ref)
            dg_ref[...] = jnp.zeros_like(dg_ref)

        yv = y_ref[...]
        dv = dxo_ref[...]
        gv = g_ref[...]
        r = lax.rsqrt(jnp.mean(yv * yv, axis=-1, keepdims=True) + RMS_EPS)
        yn = yv * r
        dgate_ref[...] += _colsum8(dv * (res_w * (yn * gv)))
        do = dv * (res_w * gate_ref[...])
        dg_ref[...] += _colsum8(do * yn)
        dyn = do * gv
        dy_ref[...] = (r * (dyn - yn * jnp.mean(dyn * yn, axis=-1, keepdims=True))).astype(BF16)
        _finish_colsums(i, n - 1, (dgate_ref, dg_ref))

    sum_sds = jax.ShapeDtypeStruct((SUBLANES, D), F32)
    return pl.pallas_call(
        body, name=name, grid=(n,),
        in_specs=[_row_spec(t, D), _row_spec(t, D), _vec_spec(D), _vec_spec(D)],
        out_specs=[_row_spec(t, D), _vec_spec(D, SUBLANES), _vec_spec(D, SUBLANES)],
        out_shape=[jax.ShapeDtypeStruct((L, D), BF16), sum_sds, sum_sds],
        compiler_params=_params("arbitrary"),
    )(dxo, y, g, gate)


def _prenorm_bwd(x, dh, dxres, g, sc, *, name):
    L, D = x.shape
    t = _tile(L, ROW_TILE, SUBLANES)
    n = L // t

    def body(x_ref, dh_ref, dxr_ref, g_ref, sc_ref, dx_ref, dsh_ref, dsc_ref, dg_ref):
        i = pl.program_id(0)

        @pl.when(i == 0)
        def _():
            dsh_ref[...] = jnp.zeros_like(dsh_ref)
            dsc_ref[...] = jnp.zeros_like(dsc_ref)
            dg_ref[...] = jnp.zeros_like(dg_ref)

        xv = x_ref[...]
        dhv = dh_ref[...].astype(F32)
        gv = g_ref[...]
        one_sc = 1.0 + sc_ref[...]
        r = lax.rsqrt(jnp.mean(xv * xv, axis=-1, keepdims=True) + RMS_EPS)
        xn = xv * r
        tt = dhv * xn
        dsh_ref[...] += _colsum8(dhv)
        dsc_ref[...] += _colsum8(tt * gv)
        dg_ref[...] += _colsum8(tt * one_sc)
        dxn = dhv * (gv * one_sc)
        dx_ref[...] = dxr_ref[...] + r * (dxn - xn * jnp.mean(dxn * xn, axis=-1, keepdims=True))
        _finish_colsums(i, n - 1, (dsh_ref, dsc_ref, dg_ref))

    sum_sds = jax.ShapeDtypeStruct((SUBLANES, D), F32)
    sum_spec = _vec_spec(D, SUBLANES)
    return pl.pallas_call(
        body, name=name, grid=(n,),
        in_specs=[_row_spec(t, D), _row_spec(t, D), _row_spec(t, D), _vec_spec(D), _vec_spec(D)],
        out_specs=[_row_spec(t, D), sum_spec, sum_spec, sum_spec],
        out_shape=[jax.ShapeDtypeStruct((L, D), F32), sum_sds, sum_sds, sum_sds],
        compiler_params=_params("arbitrary"),
    )(x, dh, dxres, g, sc)


def _loss_head(y, target, *, name):
    L, D = y.shape
    t = _tile(L, ROW_TILE, SUBLANES)
    n = L // t

    def body(y_ref, t_ref, dy_ref, loss_ref):
        i = pl.program_id(0)

        @pl.when(i == 0)
        def _():
            loss_ref[...] = jnp.zeros_like(loss_ref)

        e = y_ref[...] - t_ref[...]
        dy_ref[...] = e * (1.0 / D)
        part = jnp.sum(jnp.mean(e * e, axis=-1, keepdims=True), axis=0, keepdims=True)
        loss_ref[...] += jnp.broadcast_to(0.5 * part, loss_ref.shape)

    return pl.pallas_call(
        body, name=name, grid=(n,),
        in_specs=[_row_spec(t, D), _row_spec(t, D)],
        out_specs=[_row_spec(t, D), pl.BlockSpec((SUBLANES, LANES), lambda i: (0, 0))],
        out_shape=[jax.ShapeDtypeStruct((L, D), F32), jax.ShapeDtypeStruct((SUBLANES, LANES), F32)],
        compiler_params=_params("arbitrary"),
    )(y, target)


GELU_C = math.sqrt(2.0 / math.pi)


def _gelu_fwd(y, *, name):
    L, W = y.shape
    t = _tile(L, 512, SUBLANES)

    def body(y_ref, o_ref):
        v = y_ref[...]
        o_ref[...] = (0.5 * v * (1.0 + jnp.tanh(GELU_C * (v + 0.044715 * (v * v * v))))).astype(BF16)

    return pl.pallas_call(
        body, name=name, grid=(L // t,), in_specs=[_row_spec(t, W)], out_specs=_row_spec(t, W),
        out_shape=jax.ShapeDtypeStruct((L, W), BF16), compiler_params=_params("parallel"),
    )(y)


def _gelu_bwd(dgl, y, u, dskip, *, name):
    L, W = y.shape
    t = _tile(L, 512, SUBLANES)
    n = L // t

    def body(dgl_ref, y_ref, u_ref, d_ref, dy_ref, sk_ref, dd_ref):
        i = pl.program_id(0)

        @pl.when(i == 0)
        def _():
            dd_ref[...] = jnp.zeros_like(dd_ref)

        v = y_ref[...]
        inner = GELU_C * (v + 0.044715 * (v * v * v))
        th = jnp.tanh(inner)
        dgelu = 0.5 * (1.0 + th) + 0.5 * v * (1.0 - th * th) * (GELU_C * (1.0 + 3.0 * 0.044715 * (v * v)))
        dy = dgl_ref[...] * dgelu
        dy_ref[...] = dy.astype(BF16)
        sk_ref[...] = dy * d_ref[...]
        dd_ref[...] += _colsum8(dy * u_ref[...])
        _finish_colsums(i, n - 1, (dd_ref,))

    return pl.pallas_call(
        body, name=name, grid=(n,),
        in_specs=[_row_spec(t, W), _row_spec(t, W), _row_spec(t, W), _vec_spec(W)],
        out_specs=[_row_spec(t, W), _row_spec(t, W), _vec_spec(W, SUBLANES)],
        out_shape=[jax.ShapeDtypeStruct((L, W), BF16), jax.ShapeDtypeStruct((L, W), F32),
                   jax.ShapeDtypeStruct((SUBLANES, W), F32)],
        compiler_params=_params("arbitrary"),
    )(dgl, y, u, dskip)


def _merge_fwd(z, yb, gab, *, name):
    L, D = yb.shape
    t = _tile(L, ROW_TILE, SUBLANES)

    def body(z_ref, yb_ref, gab_ref, o_ref):
        ya = z_ref[:, :D] * _sigmoid(z_ref[:, D:])
        o_ref[...] = (_sigmoid(gab_ref[:, :D]) * ya + _sigmoid(gab_ref[:, D:]) * yb_ref[...]).astype(BF16)

    return pl.pallas_call(
        body, name=name, grid=(L // t,),
        in_specs=[_row_spec(t, 2 * D), _row_spec(t, D), _row_spec(t, 2 * D)],
        out_specs=_row_spec(t, D), out_shape=jax.ShapeDtypeStruct((L, D), BF16),
        compiler_params=_params("parallel"),
    )(z, yb, gab)


def _merge_bwd(dm, z, yb, gab, *, name):
    L, D = yb.shape
    t = _tile(L, ROW_TILE, SUBLANES)

    def body(dm_ref, z_ref, yb_ref, gab_ref, dz_ref, dyb_ref, dgab_ref):
        dmv = dm_ref[...]
        zv = z_ref[:, :D]
        sz = _sigmoid(z_ref[:, D:])
        sa = _sigmoid(gab_ref[:, :D])
        sb = _sigmoid(gab_ref[:, D:])
        ybv = yb_ref[...]
        dya = dmv * sa
        dz_ref[:, :D] = (dya * sz).astype(BF16)
        dz_ref[:, D:] = (dya * zv * (sz * (1.0 - sz))).astype(BF16)
        dyb_ref[...] = (dmv * sb).astype(BF16)
        dgab_ref[:, :D] = (dmv * (zv * sz) * (sa * (1.0 - sa))).astype(BF16)
        dgab_ref[:, D:] = (dmv * ybv * (sb * (1.0 - sb))).astype(BF16)

    return pl.pallas_call(
        body, name=name, grid=(L // t,),
        in_specs=[_row_spec(t, D), _row_spec(t, 2 * D), _row_spec(t, D), _row_spec(t, 2 * D)],
        out_specs=[_row_spec(t, 2 * D), _row_spec(t, D), _row_spec(t, 2 * D)],
        out_shape=[jax.ShapeDtypeStruct((L, 2 * D), BF16), jax.ShapeDtypeStruct((L, D), BF16),
                   jax.ShapeDtypeStruct((L, 2 * D), BF16)],
        compiler_params=_params("parallel"),
    )(dm, z, yb, gab)


SCAN_W = 1024
SCAN_T = 512


def _interleave(x):
    L, W = x.shape
    seg = SCAN_T // SUBLANES
    return x.reshape(L // SCAN_T, SUBLANES, seg, W).transpose(0, 2, 1, 3).reshape(L, W)


def _deinterleave(x):
    L, W = x.shape
    seg = SCAN_T // SUBLANES
    return x.reshape(L // SCAN_T, seg, SUBLANES, W).transpose(0, 2, 1, 3).reshape(L, W)


def _power_table(a, b, pr_tab, pi_tab, n):
    def fill(k, carry):
        pr, pi = carry
        pr_tab[k] = pr
        pi_tab[k] = pi
        return a * pr - b * pi, a * pi + b * pr

    lax.fori_loop(0, n, fill, (a, b))


def _rows_to_tile(rows):
    w = rows[0].shape[1]
    sub = lax.broadcasted_iota(jnp.int32, (SUBLANES, w), 0)
    tile = jnp.broadcast_to(rows[0], (SUBLANES, w))
    for j in range(1, SUBLANES):
        tile = jnp.where(sub == j, jnp.broadcast_to(rows[j], (SUBLANES, w)), tile)
    return tile


def _ssm_scan_fwd(bu_re, bu_im, lam_re, lam_im, *, name):
    L, S = bu_re.shape
    w, t = _tile(S, SCAN_W), SCAN_T
    seg = t // SUBLANES

    def body(br_ref, bi_ref, lr_ref, li_ref, sr_ref, si_ref, pr_tab, pi_tab, cr_ref, ci_ref):
        a = jnp.broadcast_to(lr_ref[...], (SUBLANES, w))
        b = jnp.broadcast_to(li_ref[...], (SUBLANES, w))

        @pl.when(pl.program_id(1) == 0)
        def _():
            cr_ref[...] = jnp.zeros_like(cr_ref)
            ci_ref[...] = jnp.zeros_like(ci_ref)
            _power_table(a, b, pr_tab, pi_tab, seg)

        def local_scan(i, carry):
            sr, si = carry
            base = pl.multiple_of(i * SUBLANES, SUBLANES)
            nr = a * sr - b * si + br_ref[pl.ds(base, SUBLANES), :]
            ni = a * si + b * sr + bi_ref[pl.ds(base, SUBLANES), :]
            sr_ref[pl.ds(base, SUBLANES), :] = nr
            si_ref[pl.ds(base, SUBLANES), :] = ni
            return nr, ni

        zero = jnp.zeros((SUBLANES, w), F32)
        fr, fi = lax.fori_loop(0, seg, local_scan, (zero, zero), unroll=2)
        lsr, lsi = pr_tab[seg - 1][0:1, :], pi_tab[seg - 1][0:1, :]
        cr, ci = cr_ref[...], ci_ref[...]
        rows_r, rows_i = [], []
        for j in range(SUBLANES):
            rows_r.append(cr)
            rows_i.append(ci)
            cr, ci = fr[j:j + 1, :] + (lsr * cr - lsi * ci), fi[j:j + 1, :] + (lsr * ci + lsi * cr)
        cr_ref[...] = cr
        ci_ref[...] = ci
        in_r, in_i = _rows_to_tile(rows_r), _rows_to_tile(rows_i)

        def add_entry(i, _):
            base = pl.multiple_of(i * SUBLANES, SUBLANES)
            pr, pi = pr_tab[i], pi_tab[i]
            sr_ref[pl.ds(base, SUBLANES), :] += pr * in_r - pi * in_i
            si_ref[pl.ds(base, SUBLANES), :] += pr * in_i + pi * in_r
            return 0

        lax.fori_loop(0, seg, add_entry, 0, unroll=2)

    blk = pl.BlockSpec((t, w), lambda j, i: (i, j))
    vec = pl.BlockSpec((1, w), lambda j, i: (0, j))
    sds = jax.ShapeDtypeStruct((L, S), F32)
    tab = pltpu.VMEM((seg, SUBLANES, w), F32)
    return pl.pallas_call(
        body, name=name, grid=(S // w, L // t),
        in_specs=[blk, blk, vec, vec], out_specs=[blk, blk], out_shape=[sds, sds],
        scratch_shapes=[tab, tab, pltpu.VMEM((1, w), F32), pltpu.VMEM((1, w), F32)],
        compiler_params=_params("parallel", "arbitrary"),
    )(bu_re, bu_im, lam_re, lam_im)


def _ssm_scan_bwd(d_re, d_im, s_re, s_im, lam_re, lam_im, *, name):
    L, S = d_re.shape
    w, t = _tile(S, SCAN_W), SCAN_T
    nt = L // t
    seg = t // SUBLANES

    def body(dr_ref, di_ref, sr_ref, si_ref, lr_ref, li_ref, gr_ref, gi_ref, ar_ref, ai_ref,
             pr_tab, pi_tab, cgr, cgi, acc_r, acc_i):
        step = pl.program_id(1)
        a = jnp.broadcast_to(lr_ref[...], (SUBLANES, w))
        b = jnp.broadcast_to(-li_ref[...], (SUBLANES, w))

        @pl.when(step == 0)
        def _():
            for r in (cgr, cgi, acc_r, acc_i):
                r[...] = jnp.zeros_like(r)
            _power_table(a, b, pr_tab, pi_tab, seg)

        def local_scan(ii, carry):
            gr, gi = carry
            base = pl.multiple_of((seg - 1 - ii) * SUBLANES, SUBLANES)
            ngr = a * gr - b * gi + dr_ref[pl.ds(base, SUBLANES), :]
            ngi = a * gi + b * gr + di_ref[pl.ds(base, SUBLANES), :]
            gr_ref[pl.ds(base, SUBLANES), :] = ngr
            gi_ref[pl.ds(base, SUBLANES), :] = ngi
            return ngr, ngi

        zero = jnp.zeros((SUBLANES, w), F32)
        fr, fi = lax.fori_loop(0, seg, local_scan, (zero, zero), unroll=2)
        lsr, lsi = pr_tab[seg - 1][0:1, :], pi_tab[seg - 1][0:1, :]
        cr, ci = cgr[...], cgi[...]
        rows_r, rows_i = [None] * SUBLANES, [None] * SUBLANES
        for j in reversed(range(SUBLANES)):
            rows_r[j], rows_i[j] = cr, ci
            cr, ci = fr[j:j + 1, :] + (lsr * cr - lsi * ci), fi[j:j + 1, :] + (lsr * ci + lsi * cr)
        cgr[...] = cr
        cgi[...] = ci
        in_r, in_i = _rows_to_tile(rows_r), _rows_to_tile(rows_i)

        def add_entry(ii, carry):
            nr, ni, xr, xi = carry
            base = pl.multiple_of((seg - 1 - ii) * SUBLANES, SUBLANES)
            sr = sr_ref[pl.ds(base, SUBLANES), :]
            si = si_ref[pl.ds(base, SUBLANES), :]
            xr = xr + (nr * sr + ni * si)
            xi = xi + (ni * sr - nr * si)
            pr, pi = pr_tab[ii], pi_tab[ii]
            gr = gr_ref[pl.ds(base, SUBLANES), :] + (pr * in_r - pi * in_i)
            gi = gi_ref[pl.ds(base, SUBLANES), :] + (pr * in_i + pi * in_r)
            gr_ref[pl.ds(base, SUBLANES), :] = gr
            gi_ref[pl.ds(base, SUBLANES), :] = gi
            return gr, gi, xr, xi

        _, _, xr, xi = lax.fori_loop(0, seg, add_entry, (in_r, in_i, acc_r[...], acc_i[...]), unroll=2)
        acc_r[...] = xr
        acc_i[...] = xi

        @pl.when(step == nt - 1)
        def _():
            ar_ref[...] = jnp.sum(xr, axis=0, keepdims=True)
            ai_ref[...] = jnp.sum(xi, axis=0, keepdims=True)

    blk = pl.BlockSpec((t, w), lambda j, i: (nt - 1 - i, j))
    vec = pl.BlockSpec((1, w), lambda j, i: (0, j))
    sds = jax.ShapeDtypeStruct((L, S), F32)
    vsds = jax.ShapeDtypeStruct((1, S), F32)
    tab = pltpu.VMEM((seg, SUBLANES, w), F32)
    tile = pltpu.VMEM((SUBLANES, w), F32)
    return pl.pallas_call(
        body, name=name, grid=(S // w, nt),
        in_specs=[blk, blk, blk, blk, vec, vec], out_specs=[blk, blk, vec, vec],
        out_shape=[sds, sds, vsds, vsds],
        scratch_shapes=[tab, tab, pltpu.VMEM((1, w), F32), pltpu.VMEM((1, w), F32), tile, tile],
        compiler_params=_params("parallel", "arbitrary"),
    )(d_re, d_im, s_re, s_im, lam_re, lam_im)


def _ssm_discretize(a_re, a_im, log_dt, b_re, b_im, c_re, c_im, d_skip):
    G, P, N = SSM_GROUPS, SSM_STATE, SSM_GROUP
    a = jnp.minimum(a_re, -1e-4)
    dt = jnp.exp(log_dt)[:, None]
    mag = jnp.exp(a * dt)
    lr = mag * jnp.cos(a_im * dt)
    li = mag * jnp.sin(a_im * dt)
    den = a * a + a_im * a_im
    cr = ((lr - 1.0) * a + li * a_im) / den
    ci = (li * a - (lr - 1.0) * a_im) / den
    bbr = cr[..., None] * b_re - ci[..., None] * b_im
    bbi = cr[..., None] * b_im + ci[..., None] * b_re
    eye = jnp.eye(G, dtype=F32)

    def in_map(bb):
        return (eye[:, None, :, None] * bb.transpose(0, 2, 1)[:, :, None, :]).reshape(G * N, G * P)

    def out_map(c):
        return (eye[:, None, :, None] * c.transpose(0, 2, 1)[:, :, None, :]).reshape(G * P, G * N)

    return (lr.reshape(1, G * P), li.reshape(1, G * P), in_map(bbr), in_map(bbi),
            out_map(c_re), out_map(-c_im), d_skip.reshape(1, SSM_WIDTH))


ATT_T = 512
CUM_T = 256


def _split3(x):
    hi = x.astype(BF16)
    r1 = x - hi.astype(F32)
    mid = r1.astype(BF16)
    lo = (r1 - mid.astype(F32)).astype(BF16)
    return hi, mid, lo


def _tri_dot(tri, x):
    hi, mid, lo = _split3(x)
    dot = lambda p: jnp.dot(tri, p, preferred_element_type=F32)
    return dot(hi) + dot(mid) + dot(lo)


def _log_sigmoid(x):
    return jnp.minimum(x, 0.0) - jnp.log(1.0 + jnp.exp(-jnp.abs(x)))


def _fox_cum(f, fb, *, name):
    L, W = f.shape
    t = _tile(L, CUM_T, SUBLANES)

    def body(f_ref, b_ref, o_ref, carry):
        @pl.when(pl.program_id(0) == 0)
        def _():
            carry[...] = jnp.zeros_like(carry)

        row = lax.broadcasted_iota(jnp.int32, (t, t), 0)
        col = lax.broadcasted_iota(jnp.int32, (t, t), 1)
        tri = jnp.where(col <= row, 1.0, 0.0).astype(BF16)
        c = _tri_dot(tri, _log_sigmoid(f_ref[...] + b_ref[...])) + carry[...]
        o_ref[...] = c
        carry[...] = c[t - 1:t, :]

    return pl.pallas_call(
        body, name=name, grid=(L // t,),
        in_specs=[_row_spec(t, W), _vec_spec(W)], out_specs=_row_spec(t, W),
        out_shape=jax.ShapeDtypeStruct((L, W), F32),
        scratch_shapes=[pltpu.VMEM((1, W), F32)], compiler_params=_params("arbitrary"),
    )(f, fb)


def _fox_cum_bwd(dcum, f, fb, *, name):
    L, W = f.shape
    t = _tile(L, CUM_T, SUBLANES)
    n = L // t

    def body(d_ref, f_ref, b_ref, o_ref, db_ref, carry):
        i = pl.program_id(0)

        @pl.when(i == 0)
        def _():
            carry[...] = jnp.zeros_like(carry)
            db_ref[...] = jnp.zeros_like(db_ref)

        row = lax.broadcasted_iota(jnp.int32, (t, t), 0)
        col = lax.broadcasted_iota(jnp.int32, (t, t), 1)
        tri = jnp.where(col >= row, 1.0, 0.0).astype(BF16)
        dlog = _tri_dot(tri, d_ref[...]) + carry[...]
        carry[...] = dlog[0:1, :]
        df = dlog * _sigmoid(-(f_ref[...] + b_ref[...]))
        o_ref[...] = df.astype(BF16)
        db_ref[...] += _colsum8(df)
        _finish_colsums(i, n - 1, (db_ref,))

    rev = pl.BlockSpec((t, W), lambda i: (n - 1 - i, 0))
    return pl.pallas_call(
        body, name=name, grid=(n,),
        in_specs=[rev, rev, _vec_spec(W)], out_specs=[rev, _vec_spec(W, SUBLANES)],
        out_shape=[jax.ShapeDtypeStruct((L, W), BF16), jax.ShapeDtypeStruct((SUBLANES, W), F32)],
        scratch_shapes=[pltpu.VMEM((1, W), F32)], compiler_params=_params("arbitrary"),
    )(dcum, f, fb)


def _head_col(blk, h):
    lane = lax.broadcasted_iota(jnp.int32, blk.shape, 1)
    return jnp.sum(jnp.where(lane == h * HEAD_DIM, blk, 0.0), axis=1, keepdims=True)


def _lo_mask(rows):
    return lax.broadcasted_iota(jnp.int32, (rows, LANES), 1) < HEAD_DIM


def _causal(t):
    row = lax.broadcasted_iota(jnp.int32, (t, t), 0)
    col = lax.broadcasted_iota(jnp.int32, (t, t), 1)
    return col <= row


def _fox_fwd(qkv, cum_cols, cum_rows, *, name):
    L = qkv.shape[0]
    t = _tile(L, ATT_T)
    nq = L // t
    npair = ATTN_HEADS // 2

    def body(q_ref, k_ref, v_ref, cc_ref, cr_ref, o_ref, o32_ref, lse_ref):
        iq = pl.program_id(1)
        lo = _lo_mask(t)
        qv = q_ref[...] * 0.125
        zq = jnp.zeros_like(qv)
        qh = (jnp.where(lo, qv, zq), jnp.where(lo, zq, qv))
        ccv = cc_ref[...]
        cq = (_head_col(ccv, 0), _head_col(ccv, 1))

        def step(ik, carry, masked):
            start = pl.multiple_of(ik * t, t)
            kb = k_ref[pl.ds(start, t), :]
            vb = v_ref[pl.ds(start, t), :]
            out = []
            for h in range(2):
                m, l, acc = carry[h]
                s = lax.dot_general(qh[h], kb, NT, preferred_element_type=F32)
                s = s + (cq[h] - cr_ref[h:h + 1, pl.ds(start, t)])
                if masked:
                    s = jnp.where(_causal(t), s, -jnp.inf)
                m_new = jnp.maximum(m, jnp.max(s, axis=1, keepdims=True))
                alpha = jnp.exp(m - m_new)
                p = jnp.exp(s - m_new)
                l = alpha * l + jnp.sum(p, axis=1, keepdims=True)
                acc = alpha * acc + jnp.dot(p.astype(BF16), vb, preferred_element_type=F32)
                out.append((m_new, l, acc))
            return tuple(out)

        init1 = (jnp.full((t, 1), -jnp.inf, F32), jnp.zeros((t, 1), F32), jnp.zeros((t, LANES), F32))
        carry = lax.fori_loop(0, iq, lambda ik, c: step(ik, c, False), (init1, init1))
        (m0, l0, a0), (m1, l1, a1) = step(iq, carry, True)
        out = jnp.where(lo, a0 / l0, a1 / l1)
        o_ref[...] = out.astype(BF16)
        o32_ref[...] = out
        lse_ref[...] = jnp.where(lo, m0 + jnp.log(l0), m1 + jnp.log(l1))

    blk = lambda off: pl.BlockSpec((t, LANES), lambda hp, iq: (iq, off + hp))
    whole = lambda off: pl.BlockSpec((L, LANES), lambda hp, iq: (0, off + hp))
    return pl.pallas_call(
        body, name=name, grid=(npair, nq),
        in_specs=[blk(0), whole(npair), whole(2 * npair), blk(0),
                  pl.BlockSpec((None, SUBLANES, L), lambda hp, iq: (hp, 0, 0))],
        out_specs=[blk(0), blk(0), blk(0)],
        out_shape=[jax.ShapeDtypeStruct((L, ATTN_WIDTH), BF16), jax.ShapeDtypeStruct((L, ATTN_WIDTH), F32),
                   jax.ShapeDtypeStruct((L, ATTN_WIDTH), F32)],
        compiler_params=_params("parallel", "arbitrary"),
    )(qkv, qkv, qkv, cum_cols, cum_rows)


STAT_LSE, STAT_CUM, STAT_DELTA = 0, 2, 4


def _lane_col(blk, idx):
    lane = lax.broadcasted_iota(jnp.int32, blk.shape, 1)
    return jnp.sum(jnp.where(lane == idx, blk, 0.0), axis=1, keepdims=True)


def _fox_rowstats(do, o, lse, cum_cols, *, name):
    L = do.shape[0]
    t = _tile(L, ATT_T)

    def body(do_ref, o_ref, lse_ref, cc_ref, st_ref):
        lo = _lo_mask(t)
        dd = do_ref[...].astype(F32) * o_ref[...]
        lsev, ccv = lse_ref[...], cc_ref[...]
        cols = (_head_col(lsev, 0), _head_col(lsev, 1), _head_col(ccv, 0), _head_col(ccv, 1),
                jnp.sum(jnp.where(lo, dd, 0.0), axis=1, keepdims=True),
                jnp.sum(jnp.where(lo, 0.0, dd), axis=1, keepdims=True))
        lane = lax.broadcasted_iota(jnp.int32, (t, LANES), 1)
        out = jnp.zeros((t, LANES), F32)
        for i, col in enumerate(cols):
            out = jnp.where(lane == i, col, out)
        st_ref[...] = out

    blk = pl.BlockSpec((t, LANES), lambda hp, i: (i, hp))
    return pl.pallas_call(
        body, name=name, grid=(ATTN_HEADS // 2, L // t),
        in_specs=[blk, blk, blk, blk], out_specs=blk,
        out_shape=jax.ShapeDtypeStruct((L, ATTN_WIDTH), F32),
        compiler_params=_params("parallel", "parallel"),
    )(do, o, lse, cum_cols)


def _fox_bwd(qkv, do, stats, cum_rows, *, name):
    L = qkv.shape[0]
    t = _tile(L, ATT_T)
    nq = L // t
    npair = ATTN_HEADS // 2

    def body(q_ref, do_ref, st_ref, k_ref, v_ref, cr_ref, dk_ref, dv_ref, dc_ref, dq_ref, drow_ref):
        ik = pl.program_id(1)

        @pl.when(ik == 0)
        def _():
            dq_ref[...] = jnp.zeros_like(dq_ref)
            drow_ref[...] = jnp.zeros_like(drow_ref)

        lo = _lo_mask(t)
        lane = lax.broadcasted_iota(jnp.int32, (t, LANES), 1)
        kb = k_ref[...]
        vb = v_ref[...]
        zk = jnp.zeros_like(kb)
        kh = (jnp.where(lo, kb, zk), jnp.where(lo, zk, kb))
        vh = (jnp.where(lo, vb, zk), jnp.where(lo, zk, vb))
        ck = (cr_ref[0:1, :], cr_ref[1:2, :])

        def step(iq, carry, masked):
            dk, dv, dc0, dc1 = carry
            start = pl.multiple_of(iq * t, t)
            qb = q_ref[pl.ds(start, t), :] * 0.125
            dob = do_ref[pl.ds(start, t), :]
            stb = st_ref[pl.ds(start, t), :]
            dks, dvs, dcs, dqs, rss = [], [], [], [], []
            for h in range(2):
                s = lax.dot_general(qb, kh[h], NT, preferred_element_type=F32)
                s = s + (_lane_col(stb, STAT_CUM + h) - ck[h])
                if masked:
                    s = jnp.where(_causal(t), s, -jnp.inf)
                p = jnp.exp(s - _lane_col(stb, STAT_LSE + h))
                dp = lax.dot_general(dob, vh[h], NT, preferred_element_type=F32)
                ds = p * (dp - _lane_col(stb, STAT_DELTA + h))
                dsb = ds.astype(BF16)
                dvs.append(lax.dot_general(p.astype(BF16), dob, TN, preferred_element_type=F32))
                dks.append(lax.dot_general(dsb, qb, TN, preferred_element_type=F32))
                dqs.append(jnp.dot(dsb, kh[h], preferred_element_type=F32))
                dcs.append(jnp.sum(ds, axis=0, keepdims=True))
                rss.append(jnp.sum(ds, axis=1, keepdims=True))
            dq_ref[pl.ds(start, t), :] += 0.125 * (dqs[0] + dqs[1])
            drow_ref[pl.ds(start, t), :] += jnp.where(lane == 0, rss[0], jnp.where(lane == 1, rss[1], 0.0))
            return (dk + jnp.where(lo, dks[0], dks[1]), dv + jnp.where(lo, dvs[0], dvs[1]),
                    dc0 - dcs[0], dc1 - dcs[1])

        zero = jnp.zeros((t, LANES), F32)
        zrow = jnp.zeros((1, t), F32)
        carry = step(ik, (zero, zero, zrow, zrow), True)
        dk, dv, dc0, dc1 = lax.fori_loop(ik + 1, nq, lambda iq, c: step(iq, c, False), carry)
        dk_ref[...] = dk.astype(BF16)
        dv_ref[...] = dv.astype(BF16)
        dc_ref[...] = jnp.zeros_like(dc_ref)
        dc_ref[0:1, :] = dc0
        dc_ref[1:2, :] = dc1

    whole = lambda off: pl.BlockSpec((L, LANES), lambda hp, ik: (0, off + hp))
    blk = lambda off: pl.BlockSpec((t, LANES), lambda hp, ik: (ik, off + hp))
    rows = pl.BlockSpec((None, SUBLANES, t), lambda hp, ik: (hp, 0, ik))
    return pl.pallas_call(
        body, name=name, grid=(npair, nq),
        in_specs=[whole(0), whole(0), whole(0), blk(npair), blk(2 * npair), rows],
        out_specs=[blk(0), blk(0), rows, whole(0), whole(0)],
        out_shape=[jax.ShapeDtypeStruct((L, ATTN_WIDTH), BF16), jax.ShapeDtypeStruct((L, ATTN_WIDTH), BF16),
                   jax.ShapeDtypeStruct((npair, SUBLANES, L), F32),
                   jax.ShapeDtypeStruct((L, ATTN_WIDTH), F32), jax.ShapeDtypeStruct((L, ATTN_WIDTH), F32)],
        compiler_params=_params("parallel", "arbitrary"),
    )(qkv, do, stats, qkv, qkv, cum_rows)


def _mod_partial(c_all, mod_w, mod_b_cols, *, name):
    depth, K, cols = mod_w.shape
    tn = _tile(cols, 768)

    def body(c_ref, w_ref, b_ref, o_ref):
        cv = c_ref[...]
        sc = (cv * _sigmoid(cv)).astype(BF16)
        o_ref[...] = jnp.dot(sc, w_ref[...].astype(BF16), preferred_element_type=F32) + b_ref[...]

    return pl.pallas_call(
        body, name=name, grid=(depth, cols // tn),
        in_specs=[pl.BlockSpec((N_DEV, K), lambda l, j: (0, 0)),
                  pl.BlockSpec((None, K, tn), lambda l, j: (l, 0, j)),
                  pl.BlockSpec((None, 1, tn), lambda l, j: (l, 0, j))],
        out_specs=pl.BlockSpec((None, N_DEV, tn), lambda l, j: (l, 0, j)),
        out_shape=jax.ShapeDtypeStruct((depth, N_DEV, cols), F32),
        compiler_params=_params("parallel", "parallel"),
    )(c_all, mod_w, mod_b_cols)


def _mod_wgrad(c_all_t, dmod, *, name):
    depth, nb, cols = dmod.shape
    K = c_all_t.shape[0]
    tn = _tile(cols, 768)
    tk = _tile(K, 256, SUBLANES)

    def body(c_ref, d_ref, o_ref):
        cv = c_ref[...]
        sc = cv * _sigmoid(cv)
        dv = d_ref[...]
        acc = sc[:, 0:1] * dv[0:1, :]
        for b in range(1, nb):
            acc = acc + sc[:, b:b + 1] * dv[b:b + 1, :]
        o_ref[...] = acc

    return pl.pallas_call(
        body, name=name, grid=(depth, K // tk, cols // tn),
        in_specs=[pl.BlockSpec((tk, nb), lambda l, i, j: (i, 0)),
                  pl.BlockSpec((None, nb, tn), lambda l, i, j: (l, 0, j))],
        out_specs=pl.BlockSpec((None, tk, tn), lambda l, i, j: (l, i, j)),
        out_shape=jax.ShapeDtypeStruct((depth, K, cols), F32),
        compiler_params=_params("parallel", "parallel", "parallel"),
    )(c_all_t, dmod)


def _adamw(w, g, m, v, *, name):
    shape = w.shape
    cols = shape[-1]
    rows = int(np.prod(shape[:-1]))
    t = _tile(rows, 256, SUBLANES) if rows % SUBLANES == 0 else rows
    r2 = lambda a: a.reshape(rows, cols)

    def body(w_ref, g_ref, m_ref, v_ref, d_ref, nm_ref, nv_ref):
        gv = g_ref[...]
        nm = ADAM_B1 * m_ref[...] + (1.0 - ADAM_B1) * gv
        nv = ADAM_B2 * v_ref[...] + (1.0 - ADAM_B2) * (gv * gv)
        m_hat = nm / (1.0 - ADAM_B1 ** ADAM_STEP)
        v_hat = nv / (1.0 - ADAM_B2 ** ADAM_STEP)
        d_ref[...] = -ADAM_LR * (m_hat / (jnp.sqrt(v_hat) + ADAM_EPS) + ADAM_WD * w_ref[...])
        nm_ref[...] = nm
        nv_ref[...] = nv

    spec = pl.BlockSpec((t, cols), lambda i: (i, 0))
    sds = jax.ShapeDtypeStruct((rows, cols), F32)
    d, nm, nv = pl.pallas_call(
        body, name=name, grid=(rows // t,),
        in_specs=[spec] * 4, out_specs=[spec] * 3, out_shape=[sds] * 3,
        compiler_params=_params("parallel"),
    )(r2(w), r2(g), r2(m), r2(v))
    return d.reshape(shape), nm.reshape(shape), nv.reshape(shape)


def _my_place():
    return lax.axis_index("x"), lax.axis_index("y"), lax.axis_index("c")


def _other_chips(x, y):
    return [(1 - x, y), (x, 1 - y), (1 - x, 1 - y)]


def _all_gather8(v, *, name, with_sum=False):
    m, n = v.shape

    def body(x_ref, out_ref, *rest):
        if with_sum:
            sum_ref, send_sems, recv_sems, local_sem = rest
        else:
            send_sems, recv_sems, local_sem = rest
        x, y, c = _my_place()
        me, sibling = (x, y, c), (x, y, 1 - c)
        chips = _other_chips(x, y)

        def rows(px, py, pc):
            return out_ref.at[pl.ds((4 * px + 2 * py + pc) * m, m), :]

        def copy(k, block, to, src=None):
            return pltpu.make_async_remote_copy(
                src_ref=rows(*block) if src is None else src, dst_ref=rows(*block),
                send_sem=send_sems.at[k], recv_sem=recv_sems.at[k], device_id=to, device_id_type=MESH)

        mine = pltpu.make_async_copy(x_ref, rows(*me), local_sem)
        mine.start()
        first = [copy(0, me, sibling, src=x_ref)]
        first += [copy(1 + j, me, (*chip, c), src=x_ref) for j, chip in enumerate(chips)]
        for cp in first:
            cp.start()
        passed = [copy(4 + j, (*chip, c), sibling) for j, chip in enumerate(chips)]
        for j, chip in enumerate(chips):
            copy(1 + j, (*chip, c), me).wait_recv()
            passed[j].start()
        copy(0, sibling, me).wait_recv()
        for j, chip in enumerate(chips):
            copy(4 + j, (*chip, 1 - c), me).wait_recv()
        for cp in first + passed:
            cp.wait_send()
        mine.wait()
        if with_sum:
            acc = out_ref[pl.ds(0, m), :]
            for d in range(1, N_DEV):
                acc = acc + out_ref[pl.ds(d * m, m), :]
            sum_ref[...] = acc

    vm = pl.BlockSpec(memory_space=pltpu.VMEM)
    out_shape = [jax.ShapeDtypeStruct((N_DEV * m, n), F32)]
    if with_sum:
        out_shape.append(jax.ShapeDtypeStruct((m, n), F32))
    res = pl.pallas_call(
        body, name=name, out_shape=out_shape, in_specs=[vm], out_specs=[vm] * len(out_shape),
        scratch_shapes=[pltpu.SemaphoreType.DMA((7,)), pltpu.SemaphoreType.DMA((7,)), pltpu.SemaphoreType.DMA],
        compiler_params=pltpu.CompilerParams(vmem_limit_bytes=VMEM_LIMIT),
    )(v)
    return res if with_sum else res[0]


def _gather_shards(shard, *, name):
    R, n = shard.shape
    half = R // 2

    def body(s_ref, out_ref, send_sems, recv_sems, local_sem):
        x, y, c = _my_place()
        sibling = (x, y, 1 - c)
        chips = _other_chips(x, y)

        def part(px, py, pc):
            return out_ref.at[2 * px + py, pl.ds(pc * half, half), :]

        def copy(k, block, to, src=None):
            return pltpu.make_async_remote_copy(
                src_ref=part(*block) if src is None else src, dst_ref=part(*block),
                send_sem=send_sems.at[k], recv_sem=recv_sems.at[k], device_id=to, device_id_type=MESH)

        mine = pltpu.make_async_copy(s_ref, out_ref.at[2 * x + y], local_sem)
        mine.start()
        my_half = s_ref.at[pl.ds(c * half, half), :]
        first = [copy(j, (x, y, c), (*chip, c), src=my_half) for j, chip in enumerate(chips)]
        for cp in first:
            cp.start()
        passed = [copy(3 + j, (*chip, c), sibling) for j, chip in enumerate(chips)]
        for j, chip in enumerate(chips):
            copy(j, (*chip, c), (x, y, c)).wait_recv()
            passed[j].start()
        for j, chip in enumerate(chips):
            copy(3 + j, (*chip, 1 - c), (x, y, c)).wait_recv()
        for cp in first + passed:
            cp.wait_send()
        mine.wait()

    hbm = pl.BlockSpec(memory_space=pltpu.HBM)
    return pl.pallas_call(
        body, name=name, out_shape=jax.ShapeDtypeStruct((N_CHIPS, R, n), shard.dtype),
        in_specs=[hbm], out_specs=hbm,
        scratch_shapes=[pltpu.SemaphoreType.DMA((6,)), pltpu.SemaphoreType.DMA((6,)), pltpu.SemaphoreType.DMA],
    )(shard)


def _swap_halves(g, *, name):
    _, R, n = g.shape
    half = R // 2

    def body(g_ref, out_ref, send_sem, recv_sem):
        x, y, c = _my_place()
        cp = pltpu.make_async_remote_copy(
            src_ref=g_ref.at[:, pl.ds((1 - c) * half, half), :], dst_ref=out_ref,
            send_sem=send_sem, recv_sem=recv_sem, device_id=(x, y, 1 - c), device_id_type=MESH)
        cp.start()
        cp.wait()

    hbm = pl.BlockSpec(memory_space=pltpu.HBM)
    return pl.pallas_call(
        body, name=name, out_shape=jax.ShapeDtypeStruct((N_CHIPS, half, n), g.dtype),
        in_specs=[hbm], out_specs=hbm,
        scratch_shapes=[pltpu.SemaphoreType.DMA, pltpu.SemaphoreType.DMA],
    )(g)


def _send_partials(s, *, name):
    _, H, n = s.shape

    def body(s_ref, out_ref, send_sems, recv_sems):
        x, y, c = _my_place()
        chips = _other_chips(x, y)
        cps = [pltpu.make_async_remote_copy(
            src_ref=s_ref.at[2 * chip[0] + chip[1]], dst_ref=out_ref.at[j],
            send_sem=send_sems.at[j], recv_sem=recv_sems.at[j], device_id=(*chip, c), device_id_type=MESH)
            for j, chip in enumerate(chips)]
        for cp in cps:
            cp.start()
        for cp in cps:
            cp.wait()

    hbm = pl.BlockSpec(memory_space=pltpu.HBM)
    return pl.pallas_call(
        body, name=name, out_shape=jax.ShapeDtypeStruct((3, H, n), s.dtype),
        in_specs=[hbm], out_specs=hbm,
        scratch_shapes=[pltpu.SemaphoreType.DMA((3,)), pltpu.SemaphoreType.DMA((3,))],
    )(s)


def _share_halves(r, *, name):
    H, n = r.shape

    def body(r_ref, out_ref, send_sem, recv_sem, local_sem):
        x, y, c = _my_place()
        mine = pltpu.make_async_copy(r_ref, out_ref.at[pl.ds(c * H, H), :], local_sem)
        mine.start()
        cp = pltpu.make_async_remote_copy(
            src_ref=r_ref, dst_ref=out_ref.at[pl.ds(c * H, H), :],
            send_sem=send_sem, recv_sem=recv_sem, device_id=(x, y, 1 - c), device_id_type=MESH)
        cp.start()
        cp.wait_send()
        pltpu.make_async_remote_copy(
            src_ref=r_ref, dst_ref=out_ref.at[pl.ds((1 - c) * H, H), :],
            send_sem=send_sem, recv_sem=recv_sem, device_id=(x, y, 1 - c), device_id_type=MESH).wait_recv()
        mine.wait()

    hbm = pl.BlockSpec(memory_space=pltpu.HBM)
    return pl.pallas_call(
        body, name=name, out_shape=jax.ShapeDtypeStruct((2 * H, n), r.dtype),
        in_specs=[hbm], out_specs=hbm,
        scratch_shapes=[pltpu.SemaphoreType.DMA, pltpu.SemaphoreType.DMA, pltpu.SemaphoreType.DMA],
    )(r)


def _add_halves(g, r1, c_idx, *, name):
    _, R, n = g.shape
    half = R // 2
    t = _tile(half, 192, 16)
    nb = half // t

    def body(c_ref, g_ref, r_ref, of_ref, ob_ref):
        s = g_ref[...] + r_ref[...]
        of_ref[...] = s
        ob_ref[...] = s.astype(BF16)

    grid_spec = pltpu.PrefetchScalarGridSpec(
        num_scalar_prefetch=1, grid=(N_CHIPS, nb),
        in_specs=[pl.BlockSpec((None, t, n), lambda k, i, c: (k, c[0] * nb + i, 0)),
                  pl.BlockSpec((None, t, n), lambda k, i, c: (k, i, 0))],
        out_specs=[pl.BlockSpec((None, t, n), lambda k, i, c: (k, i, 0))] * 2)
    return pl.pallas_call(
        body, name=name, grid_spec=grid_spec,
        out_shape=[jax.ShapeDtypeStruct((N_CHIPS, half, n), F32), jax.ShapeDtypeStruct((N_CHIPS, half, n), BF16)],
        compiler_params=_params("parallel", "parallel"),
    )(c_idx, g, r1)


def _sum_partials(own, parts, *, name):
    H, n = own.shape
    t = _tile(H, 192, 16)

    def body(o_ref, p_ref, out_ref):
        acc = o_ref[...]
        for j in range(3):
            acc = acc + p_ref[j].astype(F32)
        out_ref[...] = acc

    return pl.pallas_call(
        body, name=name, grid=(H // t,),
        in_specs=[pl.BlockSpec((t, n), lambda i: (i, 0)), pl.BlockSpec((3, t, n), lambda i: (0, i, 0))],
        out_specs=pl.BlockSpec((t, n), lambda i: (i, 0)), out_shape=jax.ShapeDtypeStruct((H, n), F32),
        compiler_params=_params("parallel"),
    )(own, parts)


PACK_COLS = 1024
PACK_ROW_ALIGN = 64
PACKED = (
    ("ffn_w_in", (DEPTH, 2, D_MODEL, 2 * D_FF // N_CHIPS), 3),
    ("ffn_w_out", (DEPTH, 2, D_FF // N_CHIPS, D_MODEL), 2),
    ("mix_w_in", (DEPTH, D_MODEL, (SSM_WIDTH + 3 * ATTN_WIDTH + ATTN_HEADS + 2 * D_MODEL) // N_CHIPS), 2),
    ("glu_w", (DEPTH, SSM_WIDTH, 2 * D_MODEL // N_CHIPS), 2),
    ("attn_w_out", (DEPTH, ATTN_WIDTH, D_MODEL // N_CHIPS), 2),
    ("mix_w_out", (DEPTH, D_MODEL // N_CHIPS, D_MODEL), 1),
)
PACK_SIZE = sum(int(np.prod(s)) for _, s, _ in PACKED)
PACK_ROWS = -(-PACK_SIZE // (PACK_COLS * PACK_ROW_ALIGN)) * PACK_ROW_ALIGN


def _pack(pieces, dtype):
    flat = jnp.concatenate([p.reshape(-1).astype(dtype) for p in pieces])
    flat = jnp.pad(flat, (0, PACK_ROWS * PACK_COLS - PACK_SIZE))
    return flat.reshape(PACK_ROWS, PACK_COLS)


def _unpack(flat2d):
    flat = flat2d.reshape(-1)
    out, off = {}, 0
    for name, shape, _ in PACKED:
        size = int(np.prod(shape))
        out[name] = flat[off:off + size].reshape(shape)
        off += size
    return out


def _unpack_full(gathered):
    per_chip = [_unpack(gathered[k]) for k in range(N_CHIPS)]
    return {name: jnp.concatenate([per_chip[k][name] for k in range(N_CHIPS)], axis=axis)
            for name, _, axis in PACKED}


def _pack_by_chip(full, dtype):
    slots = []
    for k in range(N_CHIPS):
        pieces = []
        for name, shape, axis in PACKED:
            w = shape[axis]
            pieces.append(lax.slice_in_dim(full[name], k * w, (k + 1) * w, axis=axis))
        slots.append(_pack(pieces, dtype))
    return jnp.stack(slots)


def _pad_rows(flat, cols=PACK_COLS, align=SUBLANES):
    n = flat.shape[0]
    rows = -(-n // (cols * align)) * align
    return jnp.pad(flat, (0, rows * cols - n)).reshape(rows, cols)


def _row(v):
    return v.reshape(1, -1)


def _ffn_fwd(x, mod, g_pre, g_post, wg, wu, w_out, tag):
    sh, sc, gate = _row(mod[0]), _row(mod[1]), _row(mod[2])
    h = _prenorm(x, _row(g_pre), sc, sh, name=f"prenorm_{tag}")
    gt, up, act = _mm_swiglu(h, wg, wu, name=f"swiglu_{tag}")
    y = _mm(act, w_out, name=f"ffn_out_{tag}", tk=1408)
    x_out = _postnorm_res(x, y, _row(g_post), gate, FFN_RES, name=f"postnorm_{tag}")
    return x_out, (x, h, gt, up, act, y)


def _ffn_bwd(dxo, saved, mod, g_pre, g_post, wg, wu, w_out, tag):
    x, h, gt, up, act, y = saved
    sc, gate = _row(mod[1]), _row(mod[2])
    dy, dgate, dgpost = _postnorm_bwd(dxo, y, _row(g_post), gate, FFN_RES, name=f"postnorm_bwd_{tag}")
    dgt, dup = _mm_swiglu_bwd(dy, w_out, gt, up, name=f"swiglu_bwd_{tag}")
    dw_out = _mm(act, dy, ta=True, name=f"dw_out_{tag}", tm=1408, tn=1024, tk=1024)
    dh = _mm(dgt, wg, tb=True, name=f"dh_gate_{tag}", tk=1408)
    dh = _mm(dup, wu, tb=True, bias=dh, name=f"dh_up_{tag}", tk=1408)
    dwg = _mm(h, dgt, ta=True, name=f"dw_gate_{tag}", tm=1024, tn=1408, tk=1024)
    dwu = _mm(h, dup, ta=True, name=f"dw_up_{tag}", tm=1024, tn=1408, tk=1024)
    dx, dsh, dsc, dgpre = _prenorm_bwd(x, dh, dxo, _row(g_pre), sc, name=f"prenorm_bwd_{tag}")
    dmod = jnp.stack([dsh[0], dsc[0], dgate[0]])
    return dx, dmod, dgpre[0], dgpost[0], jnp.concatenate([dwg, dwu], axis=1), dw_out


def _split_mix_w_in(w):
    u0, q0, f0, g0 = 0, SSM_WIDTH, SSM_WIDTH + 3 * ATTN_WIDTH, SSM_WIDTH + 3 * ATTN_WIDTH + ATTN_HEADS
    w_f = jnp.pad(w[:, f0:g0], ((0, 0), (0, LANES - ATTN_HEADS)))
    return w[:, u0:q0], w[:, q0:f0], w_f, w[:, g0:]


def _mixer_fwd(x, mod, g_pre, g_post, w, ssm, forget_b, tag):
    L = x.shape[0]
    sh, sc, gate = _row(mod[0]), _row(mod[1]), _row(mod[2])
    lam_re, lam_im, bin_re, bin_im, cout_re, cout_im, dskip = ssm
    h = _prenorm(x, _row(g_pre), sc, sh, name=f"prenorm_{tag}")
    u = _mm(h, w["u"], name=f"proj_u_{tag}")
    qkv = _mm(h, w["qkv"], out_dtype=BF16, name=f"proj_qkv_{tag}")
    f = _mm(h, w["f"], name=f"proj_f_{tag}")
    gab = _mm(h, w["gab"], name=f"proj_gab_{tag}")
    u_i = _interleave(u)
    bu_re = _mm(u_i, bin_re, name=f"ssm_bu_re_{tag}")
    bu_im = _mm(u_i, bin_im, name=f"ssm_bu_im_{tag}")
    s_re, s_im = _ssm_scan_fwd(bu_re, bu_im, lam_re, lam_im, name=f"ssm_scan_{tag}")
    y_ssm = _mm(s_re, cout_re, bias=u_i, bscale=dskip, name=f"ssm_y_re_{tag}", tk=2048)
    y_ssm = _deinterleave(_mm(s_im, cout_im, bias=y_ssm, name=f"ssm_y_im_{tag}", tk=2048))
    gl = _gelu_fwd(y_ssm, name=f"gelu_{tag}")
    z = _mm(gl, w["glu"], name=f"glu_{tag}")
    fb = jnp.pad(forget_b, (0, LANES - ATTN_HEADS)).reshape(1, LANES)
    cum = _fox_cum(f, fb, name=f"fox_cum_{tag}")
    cum8 = cum[:, :ATTN_HEADS]
    cum_cols = jnp.repeat(cum8, HEAD_DIM, axis=1)
    cum_rows = jnp.pad(cum8.T.reshape(ATTN_HEADS // 2, 2, L), ((0, 0), (0, SUBLANES - 2), (0, 0)))
    attn, attn32, lse = _fox_fwd(qkv, cum_cols, cum_rows, name=f"fox_fwd_{tag}")
    yb = _mm(attn, w["attn_out"], name=f"attn_out_{tag}")
    merged = _merge_fwd(z, yb, gab, name=f"merge_{tag}")
    y = _mm(merged, w["out"], name=f"mix_out_{tag}")
    x_out = _postnorm_res(x, y, _row(g_post), gate, 1.0, name=f"postnorm_{tag}")
    saved = (x, h, u, u_i, qkv, f, gab, s_re, s_im, y_ssm, gl, z, fb, cum_cols, cum_rows, attn, attn32, lse, yb,
             merged, y)
    return x_out, saved


def _mixer_bwd(dxo, saved, mod, g_pre, g_post, w, ssm, tag):
    (x, h, u, u_i, qkv, f, gab, s_re, s_im, y_ssm, gl, z, fb, cum_cols, cum_rows, attn, attn32, lse, yb,
     merged, y) = saved
    L = x.shape[0]
    sc, gate = _row(mod[1]), _row(mod[2])
    lam_re, lam_im, bin_re, bin_im, cout_re, cout_im, dskip = ssm
    dy, dgate, dgpost = _postnorm_bwd(dxo, y, _row(g_post), gate, 1.0, name=f"postnorm_bwd_{tag}")
    dmerged = _mm(dy, w["out"], tb=True, name=f"dmerged_{tag}")
    dw_out = _mm(merged, dy, ta=True, name=f"dw_mix_out_{tag}", tm=1024, tn=1024)
    dz, dyb, dgab = _merge_bwd(dmerged, z, yb, gab, name=f"merge_bwd_{tag}")
    dgl = _mm(dz, w["glu"], tb=True, name=f"dgl_{tag}", tk=2048)
    dw_glu = _mm(gl, dz, ta=True, name=f"dw_glu_{tag}", tm=512, tn=2048)
    dys, dsk, dd = _gelu_bwd(dgl, y_ssm, u, dskip, name=f"gelu_bwd_{tag}")
    dys, dsk = _interleave(dys), _interleave(dsk)
    d_re = _mm(dys, cout_re, tb=True, name=f"ssm_ds_re_{tag}")
    d_im = _mm(dys, cout_im, tb=True, name=f"ssm_ds_im_{tag}")
    dcout_re = _mm(s_re, dys, ta=True, name=f"ssm_dc_re_{tag}", tm=1024, tn=512)
    dcout_im = _mm(s_im, dys, ta=True, name=f"ssm_dc_im_{tag}", tm=1024, tn=512)
    g_re, g_im, dlam_re, dlam_im = _ssm_scan_bwd(d_re, d_im, s_re, s_im, lam_re, lam_im, name=f"ssm_scan_bwd_{tag}")
    du = _mm(g_re, bin_re, tb=True, bias=dsk, name=f"ssm_du_re_{tag}", tk=2048)
    du = _mm(g_im, bin_im, tb=True, bias=du, out_dtype=BF16, name=f"ssm_du_im_{tag}", tk=2048)
    du = _deinterleave(du)
    dbin_re = _mm(u_i, g_re, ta=True, name=f"ssm_db_re_{tag}", tm=512, tn=1024)
    dbin_im = _mm(u_i, g_im, ta=True, name=f"ssm_db_im_{tag}", tm=512, tn=1024)
    dssm = (dlam_re, dlam_im, dbin_re, dbin_im, dcout_re, dcout_im, _row(dd[0]))
    dattn = _mm(dyb, w["attn_out"], tb=True, out_dtype=BF16, name=f"dattn_{tag}")
    dw_attn = _mm(attn, dyb, ta=True, name=f"dw_attn_out_{tag}", tm=512, tn=1024)
    stats = _fox_rowstats(dattn, attn32, lse, cum_cols, name=f"fox_rowstats_{tag}")
    dk, dv, dcum_rows, dq, drow = _fox_bwd(qkv, dattn, stats, cum_rows, name=f"fox_bwd_{tag}")
    drow8 = drow.reshape(L, ATTN_HEADS // 2, LANES)[:, :, :2].reshape(L, ATTN_HEADS)
    dcum = drow8 + dcum_rows[:, :2, :].reshape(ATTN_HEADS, L).T
    dcum = jnp.pad(dcum, ((0, 0), (0, LANES - ATTN_HEADS)))
    df, dfb = _fox_cum_bwd(dcum, f, fb, name=f"fox_cum_bwd_{tag}")
    dqkv = jnp.concatenate([dq.astype(BF16), dk, dv], axis=1)
    dh = _mm(dqkv, w["qkv"], tb=True, name=f"dh_qkv_{tag}", tk=1536)
    dh = _mm(du, w["u"], tb=True, bias=dh, name=f"dh_u_{tag}")
    dh = _mm(dgab, w["gab"], tb=True, bias=dh, name=f"dh_gab_{tag}", tk=2048)
    dh = _mm(df, w["f"], tb=True, bias=dh, name=f"dh_f_{tag}")
    dw_u = _mm(h, du, ta=True, name=f"dw_u_{tag}", tm=1024, tn=512)
    dw_qkv = _mm(h, dqkv, ta=True, name=f"dw_qkv_{tag}", tm=1024, tn=1536)
    dw_f = _mm(h, df, ta=True, name=f"dw_f_{tag}", tm=1024)
    dw_gab = _mm(h, dgab, ta=True, name=f"dw_gab_{tag}", tm=1024, tn=1024)
    dw_in = jnp.concatenate([dw_u, dw_qkv, dw_f[:, :ATTN_HEADS], dw_gab], axis=1)
    dx, dsh, dsc, dgpre = _prenorm_bwd(x, dh, dxo, _row(g_pre), sc, name=f"prenorm_bwd_{tag}")
    dmod = jnp.stack([dsh[0], dsc[0], dgate[0]])
    grads = dict(mix_w_in=dw_in, glu_w=dw_glu, attn_w_out=dw_attn, mix_w_out=dw_out,
                 forget_b=dfb[0, :ATTN_HEADS])
    return dx, dmod, dgpre[0], dgpost[0], grads, dssm


SSM_NAMES = ("ssm_a_re", "ssm_a_im", "ssm_log_dt", "ssm_b_re", "ssm_b_im", "ssm_c_re", "ssm_c_im", "ssm_d")
SMALL_NAMES = ("forget_b",) + SSM_NAMES
WEIGHT_NAMES = ("mod_w", "mod_b", "norm_pre", "norm_post", "ffn_w_in", "ffn_w_out", "mix_w_in", "forget_b") \
    + SSM_NAMES + ("glu_w", "attn_w_out", "mix_w_out")


def _train_step(x, c, target, w, m, v):
    xi, yi, ci = _my_place()
    chip = 2 * xi + yi
    dev = 4 * xi + 2 * yi + ci
    mod_cols = N_SUB * 3 * D_MODEL // N_CHIPS
    norm_cols = D_MODEL // N_CHIPS

    gathered = _gather_shards(_pack([w[n] for n, _, _ in PACKED], BF16), name="gather_weights")
    c_all = _all_gather8(jnp.pad(c, ((0, SUBLANES - 1), (0, 0))), name="gather_c")[::SUBLANES]
    mod_b_cols = lax.dynamic_slice_in_dim(w["mod_b"], chip * mod_cols, mod_cols, axis=1)[:, None, :]
    mod_part = _mod_partial(c_all, w["mod_w"], mod_b_cols, name="mod_partial")
    small_fwd = jnp.concatenate([mod_part.reshape(-1), w["norm_pre"].reshape(-1), w["norm_post"].reshape(-1)])
    n_mod, n_norm = mod_part.size, w["norm_pre"].size
    sf_all = _all_gather8(_pad_rows(small_fwd), name="gather_mod").reshape(N_DEV, -1)
    sf_chips = sf_all[::2]
    mod_all = jnp.concatenate(
        [sf_chips[k, :n_mod].reshape(DEPTH, N_DEV, mod_cols) for k in range(N_CHIPS)], axis=2)
    mod_mine = lax.dynamic_index_in_dim(mod_all, dev, axis=1, keepdims=False).reshape(DEPTH, N_SUB, 3, D_MODEL)
    norm_pre = jnp.concatenate(
        [sf_chips[k, n_mod:n_mod + n_norm].reshape(DEPTH, N_SUB, norm_cols) for k in range(N_CHIPS)], axis=2)
    norm_post = jnp.concatenate(
        [sf_chips[k, n_mod + n_norm:n_mod + 2 * n_norm].reshape(DEPTH, N_SUB, norm_cols) for k in range(N_CHIPS)],
        axis=2)
    full = _unpack_full(gathered)

    saved, layer_w, ssm_prep, ssm_vjp = [], [], [], []
    h = x
    for l in range(DEPTH):
        w_u, w_qkv, w_f, w_gab = _split_mix_w_in(full["mix_w_in"][l])
        lw = dict(
            ffn=[(full["ffn_w_in"][l, j][:, :D_FF], full["ffn_w_in"][l, j][:, D_FF:], full["ffn_w_out"][l, j])
                 for j in range(2)],
            mix=dict(u=w_u, qkv=w_qkv, f=w_f, gab=w_gab, glu=full["glu_w"][l], attn_out=full["attn_w_out"][l],
                     out=full["mix_w_out"][l]))
        prep, vjp = jax.vjp(_ssm_discretize, *[w[n][l] for n in SSM_NAMES])
        layer_w.append(lw)
        ssm_prep.append(prep)
        ssm_vjp.append(vjp)
        h, s0 = _ffn_fwd(h, mod_mine[l, 0], norm_pre[l, 0], norm_post[l, 0], *lw["ffn"][0], tag=f"l{l}a")
        h, s1 = _mixer_fwd(h, mod_mine[l, 1], norm_pre[l, 1], norm_post[l, 1], lw["mix"], prep,
                           w["forget_b"][l], tag=f"l{l}m")
        h, s2 = _ffn_fwd(h, mod_mine[l, 2], norm_pre[l, 2], norm_post[l, 2], *lw["ffn"][1], tag=f"l{l}b")
        saved.append((s0, s1, s2))
    dh, loss8 = _loss_head(h, target, name="loss_head")

    g_full = {n: [None] * DEPTH for n, _, _ in PACKED}
    g_small = {n: [None] * DEPTH for n in SMALL_NAMES}
    dmod, dnpre, dnpost = [], [], []
    for l in reversed(range(DEPTH)):
        lw = layer_w[l]
        dh, dm2, dp2, dq2, dwin2, dwout2 = _ffn_bwd(dh, saved[l][2], mod_mine[l, 2], norm_pre[l, 2],
                                                    norm_post[l, 2], *lw["ffn"][1], tag=f"l{l}b")
        dh, dm1, dp1, dq1, gmix, dssm = _mixer_bwd(dh, saved[l][1], mod_mine[l, 1], norm_pre[l, 1],
                                                   norm_post[l, 1], lw["mix"], ssm_prep[l], tag=f"l{l}m")
        dh, dm0, dp0, dq0, dwin0, dwout0 = _ffn_bwd(dh, saved[l][0], mod_mine[l, 0], norm_pre[l, 0],
                                                    norm_post[l, 0], *lw["ffn"][0], tag=f"l{l}a")
        dmod.insert(0, jnp.stack([dm0, dm1, dm2]))
        dnpre.insert(0, jnp.stack([dp0, dp1, dp2]))
        dnpost.insert(0, jnp.stack([dq0, dq1, dq2]))
        g_full["ffn_w_in"][l] = jnp.stack([dwin0, dwin2])
        g_full["ffn_w_out"][l] = jnp.stack([dwout0, dwout2])
        for n in ("mix_w_in", "glu_w", "attn_w_out", "mix_w_out"):
            g_full[n][l] = gmix[n]
        g_small["forget_b"][l] = gmix["forget_b"]
        for n, g in zip(SSM_NAMES, ssm_vjp[l](dssm)):
            g_small[n][l] = g
    grad_x = dh
    g_full = {n: jnp.stack(g) for n, g in g_full.items()}
    g_small = {n: jnp.stack(g) for n, g in g_small.items()}

    small = [loss8[0, :1], jnp.stack(dmod).reshape(-1), jnp.stack(dnpre).reshape(-1), jnp.stack(dnpost).reshape(-1)]
    small += [g_small[n].reshape(-1) for n in SMALL_NAMES]
    sizes = [int(s.size) for s in small]
    offs = np.concatenate([[0], np.cumsum(sizes)])
    sb_all, sb_sum = _all_gather8(_pad_rows(jnp.concatenate(small)), name="gather_small_grads", with_sum=True)
    sb_sum = sb_sum.reshape(-1)
    take = lambda i: sb_sum[int(offs[i]):int(offs[i + 1])]
    loss = take(0)[0]
    grads = {"mod_b": take(1).reshape(DEPTH, N_SUB * 3 * D_MODEL)}
    dnorm_pre_full = take(2).reshape(DEPTH, N_SUB, D_MODEL)
    dnorm_post_full = take(3).reshape(DEPTH, N_SUB, D_MODEL)
    grads["norm_pre"] = lax.dynamic_slice_in_dim(dnorm_pre_full, chip * norm_cols, norm_cols, axis=2)
    grads["norm_post"] = lax.dynamic_slice_in_dim(dnorm_post_full, chip * norm_cols, norm_cols, axis=2)
    for i, n in enumerate(SMALL_NAMES):
        grads[n] = take(4 + i).reshape(w[n].shape)
    dmod_all = sb_all.reshape(N_DEV, -1)[:, int(offs[1]):int(offs[2])].reshape(N_DEV, DEPTH, N_SUB * 3 * D_MODEL)
    dmod_cols = lax.dynamic_slice_in_dim(dmod_all, chip * mod_cols, mod_cols, axis=2).transpose(1, 0, 2)
    grads["mod_w"] = _mod_wgrad(c_all.T, dmod_cols, name="mod_wgrad")

    g_slots = _pack_by_chip(g_full, F32)
    from_sibling = _swap_halves(g_slots, name="grad_swap_halves")
    sum_f32, sum_bf16 = _add_halves(g_slots, from_sibling, ci.reshape(1).astype(jnp.int32), name="grad_add_halves")
    parts = _send_partials(sum_bf16, name="grad_send_partials")
    own = lax.dynamic_index_in_dim(sum_f32, chip, axis=0, keepdims=False)
    reduced_half = _sum_partials(own, parts, name="grad_sum_partials")
    reduced = _share_halves(reduced_half, name="grad_share_halves")
    grads.update(_unpack(reduced))

    delta, new_m, new_v = {}, {}, {}
    for n in WEIGHT_NAMES:
        delta[n], new_m[n], new_v[n] = _adamw(w[n], grads[n], m[n], v[n], name=f"adamw_{n}")
    outs = [loss, grad_x[None]]
    for group in (grads, delta, new_m, new_v):
        outs += [group[n] for n in WEIGHT_NAMES]
    return tuple(outs)


def kernel(x, c, mod_w, mod_b, norm_pre, norm_post, ffn_w_in, ffn_w_out, mix_w_in, forget_b, ssm_a_re, ssm_a_im, ssm_log_dt, ssm_b_re, ssm_b_im, ssm_c_re, ssm_c_im, ssm_d, glu_w, attn_w_out, mix_w_out, loss_target, m_mod_w, m_mod_b, m_norm_pre, m_norm_post, m_ffn_w_in, m_ffn_w_out, m_mix_w_in, m_forget_b, m_ssm_a_re, m_ssm_a_im, m_ssm_log_dt, m_ssm_b_re, m_ssm_b_im, m_ssm_c_re, m_ssm_c_im, m_ssm_d, m_glu_w, m_attn_w_out, m_mix_w_out, v_mod_w, v_mod_b, v_norm_pre, v_norm_post, v_ffn_w_in, v_ffn_w_out, v_mix_w_in, v_forget_b, v_ssm_a_re, v_ssm_a_im, v_ssm_log_dt, v_ssm_b_re, v_ssm_b_im, v_ssm_c_re, v_ssm_c_im, v_ssm_d, v_glu_w, v_attn_w_out, v_mix_w_out):
    w = dict(mod_w=mod_w, mod_b=mod_b, norm_pre=norm_pre, norm_post=norm_post, ffn_w_in=ffn_w_in,
             ffn_w_out=ffn_w_out, mix_w_in=mix_w_in, forget_b=forget_b, ssm_a_re=ssm_a_re, ssm_a_im=ssm_a_im,
             ssm_log_dt=ssm_log_dt, ssm_b_re=ssm_b_re, ssm_b_im=ssm_b_im, ssm_c_re=ssm_c_re, ssm_c_im=ssm_c_im,
             ssm_d=ssm_d, glu_w=glu_w, attn_w_out=attn_w_out, mix_w_out=mix_w_out)
    m = dict(mod_w=m_mod_w, mod_b=m_mod_b, norm_pre=m_norm_pre, norm_post=m_norm_post, ffn_w_in=m_ffn_w_in,
             ffn_w_out=m_ffn_w_out, mix_w_in=m_mix_w_in, forget_b=m_forget_b, ssm_a_re=m_ssm_a_re,
             ssm_a_im=m_ssm_a_im, ssm_log_dt=m_ssm_log_dt, ssm_b_re=m_ssm_b_re, ssm_b_im=m_ssm_b_im,
             ssm_c_re=m_ssm_c_re, ssm_c_im=m_ssm_c_im, ssm_d=m_ssm_d, glu_w=m_glu_w, attn_w_out=m_attn_w_out,
             mix_w_out=m_mix_w_out)
    v = dict(mod_w=v_mod_w, mod_b=v_mod_b, norm_pre=v_norm_pre, norm_post=v_norm_post, ffn_w_in=v_ffn_w_in,
             ffn_w_out=v_ffn_w_out, mix_w_in=v_mix_w_in, forget_b=v_forget_b, ssm_a_re=v_ssm_a_re,
             ssm_a_im=v_ssm_a_im, ssm_log_dt=v_ssm_log_dt, ssm_b_re=v_ssm_b_re, ssm_b_im=v_ssm_b_im,
             ssm_c_re=v_ssm_c_re, ssm_c_im=v_ssm_c_im, ssm_d=v_ssm_d, glu_w=v_glu_w, attn_w_out=v_attn_w_out,
             mix_w_out=v_mix_w_out)
    return _train_step(x[0], c, loss_target[0], w, m, v)
```

```python
import math
from typing import NamedTuple

import jax
import jax.numpy as jnp
import numpy as np
from jax import lax
from jax.experimental import pallas as pl
from jax.experimental.pallas import tpu as pltpu

F32 = jnp.float32
BF16 = jnp.bfloat16

D_MODEL = 1024
DEPTH = 2
SSM_WIDTH = 512
SSM_GROUP = 16
SSM_GROUPS = 32
SSM_STATE = 64
SSM_FLAT = SSM_GROUPS * SSM_STATE
ATTN_HEADS = 8
HEAD_DIM = 64
ATTN_WIDTH = 512
D_FF = 2816
FFN_RES = 0.5
N_SUB = 3
RMS_EPS = 1e-6
N_CHIPS = 4
N_DEV = 8

ADAM_LR = 0.001
ADAM_B1 = 0.9
ADAM_B2 = 0.999
ADAM_EPS = 1e-08
ADAM_WD = 0.01
ADAM_STEP = 10

LANES = 128
SUBLANES = 8
VMEM_LIMIT = 52 * 1024 * 1024
MESH = pl.DeviceIdType.MESH

NN = (((1,), (0,)), ((), ()))
NT = (((1,), (1,)), ((), ()))
TN = (((0,), (0,)), ((), ()))


def _tile(dim, target, align=LANES):
    best = None
    t = align
    while t <= min(dim, target):
        if dim % t == 0:
            best = t
        t += align
    return dim if best is None else best


def _params(*sem):
    return pltpu.CompilerParams(dimension_semantics=sem, vmem_limit_bytes=VMEM_LIMIT)


def _mm(a, b, *, name, ta=False, tb=False, out_dtype=F32, bias=None, bscale=None, b_k0=0,
        tm=512, tn=1024, tk=1024):
    M, K = (a.shape[1], a.shape[0]) if ta else a.shape
    N = b.shape[0] if tb else b.shape[1]
    assert b_k0 + K <= (b.shape[1] if tb else b.shape[0]), (a.shape, b.shape, ta, tb)
    tm, tn, tk = _tile(M, tm), _tile(N, tn), _tile(K, tk)
    nk = K // tk
    assert b_k0 % tk == 0
    kb0 = b_k0 // tk
    dn = (((0 if ta else 1,), (1 if tb else 0,)), ((), ()))
    has_bias, has_scale = bias is not None, bscale is not None

    def body(*refs):
        a_ref, b_ref = refs[0], refs[1]
        pos = 2
        bias_ref = scale_ref = None
        if has_bias:
            bias_ref = refs[pos]
            pos += 1
        if has_scale:
            scale_ref = refs[pos]
            pos += 1
        o_ref = refs[pos]
        acc_ref = refs[pos + 1] if nk > 1 else None

        def finish(r):
            if has_bias:
                extra = bias_ref[...].astype(F32)
                if has_scale:
                    extra = extra * scale_ref[...]
                r = r + extra
            o_ref[...] = r.astype(out_dtype)

        part = lax.dot_general(a_ref[...].astype(BF16), b_ref[...].astype(BF16), dn,
                               preferred_element_type=F32)
        if nk == 1:
            finish(part)
        else:
            k = pl.program_id(2)

            @pl.when(k == 0)
            def _():
                acc_ref[...] = part

            @pl.when(k > 0)
            def _():
                acc_ref[...] += part

            @pl.when(k == nk - 1)
            def _():
                finish(acc_ref[...])

    a_spec = pl.BlockSpec((tk, tm), lambda j, i, k: (k, i)) if ta else pl.BlockSpec((tm, tk), lambda j, i, k: (i, k))
    b_spec = (pl.BlockSpec((tn, tk), lambda j, i, k: (j, kb0 + k)) if tb
              else pl.BlockSpec((tk, tn), lambda j, i, k: (kb0 + k, j)))
    in_specs = [a_spec, b_spec]
    args = [a, b]
    if has_bias:
        in_specs.append(pl.BlockSpec((tm, tn), lambda j, i, k: (i, j)))
        args.append(bias)
    if has_scale:
        in_specs.append(pl.BlockSpec((1, tn), lambda j, i, k: (0, j)))
        args.append(bscale)
    return pl.pallas_call(
        body, name=name,
        grid=(N // tn, M // tm, nk),
        in_specs=in_specs,
        out_specs=pl.BlockSpec((tm, tn), lambda j, i, k: (i, j)),
        out_shape=jax.ShapeDtypeStruct((M, N), out_dtype),
        scratch_shapes=[pltpu.VMEM((tm, tn), F32)] if nk > 1 else [],
        compiler_params=_params("parallel", "parallel", "arbitrary"),
    )(*args)


def _sigmoid(x):
    return 1.0 / (1.0 + jnp.exp(-x))


def _mm_swiglu(h, w_in, *, name, tm=512, tn=1408):
    M, K = h.shape
    N = w_in.shape[1] // 2
    tm, tn = _tile(M, tm), _tile(N, tn)
    nj = N // tn

    def body(h_ref, wg_ref, wu_ref, g_ref, u_ref, a_ref):
        hv = h_ref[...]
        g = jnp.dot(hv, wg_ref[...], preferred_element_type=F32)
        u = jnp.dot(hv, wu_ref[...], preferred_element_type=F32)
        g_ref[...] = g.astype(BF16)
        u_ref[...] = u.astype(BF16)
        a_ref[...] = (g * _sigmoid(g) * u).astype(BF16)

    o_spec = pl.BlockSpec((tm, tn), lambda j, i: (i, j))
    sds = jax.ShapeDtypeStruct((M, N), BF16)
    return pl.pallas_call(
        body, name=name, grid=(nj, M // tm),
        in_specs=[pl.BlockSpec((tm, K), lambda j, i: (i, 0)), pl.BlockSpec((K, tn), lambda j, i: (0, j)),
                  pl.BlockSpec((K, tn), lambda j, i: (0, nj + j))],
        out_specs=[o_spec, o_spec, o_spec], out_shape=[sds, sds, sds],
        compiler_params=_params("parallel", "parallel"),
    )(h, w_in, w_in)


def _mm_swiglu_bwd(dy, w_out, gate, up, *, name, tm=512, tn=1408):
    M, K = dy.shape
    N = w_out.shape[0]
    tm, tn = _tile(M, tm), _tile(N, tn)

    def body(dy_ref, w_ref, g_ref, u_ref, dg_ref, du_ref):
        dact = lax.dot_general(dy_ref[...], w_ref[...], NT, preferred_element_type=F32)
        g = g_ref[...].astype(F32)
        u = u_ref[...].astype(F32)
        sig = _sigmoid(g)
        dg_ref[...] = (dact * u * (sig * (1.0 + g * (1.0 - sig)))).astype(BF16)
        du_ref[...] = (dact * (g * sig)).astype(BF16)

    t_spec = pl.BlockSpec((tm, tn), lambda j, i: (i, j))
    sds = jax.ShapeDtypeStruct((M, N), BF16)
    return pl.pallas_call(
        body, name=name, grid=(N // tn, M // tm),
        in_specs=[pl.BlockSpec((tm, K), lambda j, i: (i, 0)), pl.BlockSpec((tn, K), lambda j, i: (j, 0)),
                  t_spec, t_spec],
        out_specs=[t_spec, t_spec], out_shape=[sds, sds],
        compiler_params=_params("parallel", "parallel"),
    )(dy, w_out, gate, up)


ROW_TILE = 256


def _colsum8(v):
    return jnp.sum(v.reshape(v.shape[0] // SUBLANES, SUBLANES, v.shape[1]), axis=0)


def _finish_colsums(step, last, refs):
    @pl.when(step == last)
    def _():
        for r in refs:
            r[...] = jnp.broadcast_to(jnp.sum(r[...], axis=0, keepdims=True), r.shape)


def _row_spec(t, d):
    return pl.BlockSpec((t, d), lambda i: (i, 0))


def _vec_spec(d, rows=1):
    return pl.BlockSpec((rows, d), lambda i: (0, 0))


def _prenorm(x, g, sc, sh, *, name):
    L, D = x.shape
    t = _tile(L, ROW_TILE, SUBLANES)

    def body(x_ref, g_ref, sc_ref, sh_ref, h_ref):
        xv = x_ref[...]
        r = lax.rsqrt(jnp.mean(xv * xv, axis=-1, keepdims=True) + RMS_EPS)
        h_ref[...] = (((xv * r) * g_ref[...]) * (1.0 + sc_ref[...]) + sh_ref[...]).astype(BF16)

    return pl.pallas_call(
        body, name=name, grid=(L // t,),
        in_specs=[_row_spec(t, D), _vec_spec(D), _vec_spec(D), _vec_spec(D)],
        out_specs=_row_spec(t, D), out_shape=jax.ShapeDtypeStruct((L, D), BF16),
        compiler_params=_params("parallel"),
    )(x, g, sc, sh)


def _postnorm_res(x, y, g, gate, res_w, *, name):
    L, D = x.shape
    t = _tile(L, ROW_TILE, SUBLANES)

    def body(x_ref, y_ref, g_ref, gate_ref, o_ref):
        yv = y_ref[...]
        r = lax.rsqrt(jnp.mean(yv * yv, axis=-1, keepdims=True) + RMS_EPS)
        o_ref[...] = x_ref[...] + (res_w * gate_ref[...]) * ((yv * r) * g_ref[...])

    return pl.pallas_call(
        body, name=name, grid=(L // t,),
        in_specs=[_row_spec(t, D), _row_spec(t, D), _vec_spec(D), _vec_spec(D)],
        out_specs=_row_spec(t, D), out_shape=jax.ShapeDtypeStruct((L, D), F32),
        compiler_params=_params("parallel"),
    )(x, y, g, gate)


def _postnorm_bwd(dxo, y, g, gate, res_w, *, name):
    L, D = y.shape
    t = _tile(L, ROW_TILE, SUBLANES)
    n = L // t

    def body(dxo_ref, y_ref, g_ref, gate_ref, dy_ref, dgate_ref, dg_ref):
        i = pl.program_id(0)

        @pl.when(i == 0)
        def _():
            dgate_ref[...] = jnp.zeros_like(dgate_ref)
            dg_ref[...] = jnp.zeros_like(dg_ref)

        yv = y_ref[...]
        dv = dxo_ref[...]
        gv = g_ref[...]
        r = lax.rsqrt(jnp.mean(yv * yv, axis=-1, keepdims=True) + RMS_EPS)
        yn = yv * r
        dgate_ref[...] += _colsum8(dv * (res_w * (yn * gv)))
        do = dv * (res_w * gate_ref[...])
        dg_ref[...] += _colsum8(do * yn)
        dyn = do * gv
        dy_ref[...] = (r * (dyn - yn * jnp.mean(dyn * yn, axis=-1, keepdims=True))).astype(BF16)
        _finish_colsums(i, n - 1, (dgate_ref, dg_ref))

    sum_sds = jax.ShapeDtypeStruct((SUBLANES, D), F32)
    return pl.pallas_call(
        body, name=name, grid=(n,),
        in_specs=[_row_spec(t, D), _row_spec(t, D), _vec_spec(D), _vec_spec(D)],
        out_specs=[_row_spec(t, D), _vec_spec(D, SUBLANES), _vec_spec(D, SUBLANES)],
        out_shape=[jax.ShapeDtypeStruct((L, D), BF16), sum_sds, sum_sds],
        compiler_params=_params("arbitrary"),
    )(dxo, y, g, gate)


def _prenorm_bwd(x, dh, dxres, g, sc, *, name):
    L, D = x.shape
    t = _tile(L, ROW_TILE, SUBLANES)
    n = L // t

    def body(x_ref, dh_ref, dxr_ref, g_ref, sc_ref, dx_ref, dsh_ref, dsc_ref, dg_ref):
        i = pl.program_id(0)

        @pl.when(i == 0)
        def _():
            dsh_ref[...] = jnp.zeros_like(dsh_ref)
            dsc_ref[...] = jnp.zeros_like(dsc_ref)
            dg_ref[...] = jnp.zeros_like(dg_ref)

        xv = x_ref[...]
        dhv = dh_ref[...].astype(F32)
        gv = g_ref[...]
        one_sc = 1.0 + sc_ref[...]
        r = lax.rsqrt(jnp.mean(xv * xv, axis=-1, keepdims=True) + RMS_EPS)
        xn = xv * r
        tt = dhv * xn
        dsh_ref[...] += _colsum8(dhv)
        dsc_ref[...] += _colsum8(tt * gv)
        dg_ref[...] += _colsum8(tt * one_sc)
        dxn = dhv * (gv * one_sc)
        dx_ref[...] = dxr_ref[...] + r * (dxn - xn * jnp.mean(dxn * xn, axis=-1, keepdims=True))
        _finish_colsums(i, n - 1, (dsh_ref, dsc_ref, dg_ref))

    sum_sds = jax.ShapeDtypeStruct((SUBLANES, D), F32)
    sum_spec = _vec_spec(D, SUBLANES)
    return pl.pallas_call(
        body, name=name, grid=(n,),
        in_specs=[_row_spec(t, D), _row_spec(t, D), _row_spec(t, D), _vec_spec(D), _vec_spec(D)],
        out_specs=[_row_spec(t, D), sum_spec, sum_spec, sum_spec],
        out_shape=[jax.ShapeDtypeStruct((L, D), F32), sum_sds, sum_sds, sum_sds],
        compiler_params=_params("arbitrary"),
    )(x, dh, dxres, g, sc)


def _loss_head(y, target, *, name):
    L, D = y.shape
    t = _tile(L, ROW_TILE, SUBLANES)
    n = L // t

    def body(y_ref, t_ref, dy_ref, loss_ref):
        i = pl.program_id(0)

        @pl.when(i == 0)
        def _():
            loss_ref[...] = jnp.zeros_like(loss_ref)

        e = y_ref[...] - t_ref[...]
        dy_ref[...] = e * (1.0 / D)
        part = jnp.sum(jnp.mean(e * e, axis=-1, keepdims=True), axis=0, keepdims=True)
        loss_ref[...] += jnp.broadcast_to(0.5 * part, loss_ref.shape)

    return pl.pallas_call(
        body, name=name, grid=(n,),
        in_specs=[_row_spec(t, D), _row_spec(t, D)],
        out_specs=[_row_spec(t, D), pl.BlockSpec((SUBLANES, LANES), lambda i: (0, 0))],
        out_shape=[jax.ShapeDtypeStruct((L, D), F32), jax.ShapeDtypeStruct((SUBLANES, LANES), F32)],
        compiler_params=_params("arbitrary"),
    )(y, target)


GELU_C = math.sqrt(2.0 / math.pi)


def _gelu_fwd(y, *, name):
    L, W = y.shape
    t = _tile(L, 512, SUBLANES)

    def body(y_ref, o_ref):
        v = y_ref[...]
        o_ref[...] = (0.5 * v * (1.0 + jnp.tanh(GELU_C * (v + 0.044715 * (v * v * v))))).astype(BF16)

    return pl.pallas_call(
        body, name=name, grid=(L // t,), in_specs=[_row_spec(t, W)], out_specs=_row_spec(t, W),
        out_shape=jax.ShapeDtypeStruct((L, W), BF16), compiler_params=_params("parallel"),
    )(y)


def _gelu_bwd(dgl, y, u, dskip, *, name):
    L, W = y.shape
    t = _tile(L, 512, SUBLANES)
    n = L // t

    def body(dgl_ref, y_ref, u_ref, d_ref, dy_ref, sk_ref, dd_ref):
        i = pl.program_id(0)

        @pl.when(i == 0)
        def _():
            dd_ref[...] = jnp.zeros_like(dd_ref)

        v = y_ref[...]
        inner = GELU_C * (v + 0.044715 * (v * v * v))
        th = jnp.tanh(inner)
        dgelu = 0.5 * (1.0 + th) + 0.5 * v * (1.0 - th * th) * (GELU_C * (1.0 + 3.0 * 0.044715 * (v * v)))
        dy = dgl_ref[...] * dgelu
        dy_ref[...] = dy.astype(BF16)
        sk_ref[...] = dy * d_ref[...]
        dd_ref[...] += _colsum8(dy * u_ref[...])
        _finish_colsums(i, n - 1, (dd_ref,))

    return pl.pallas_call(
        body, name=name, grid=(n,),
        in_specs=[_row_spec(t, W), _row_spec(t, W), _row_spec(t, W), _vec_spec(W)],
        out_specs=[_row_spec(t, W), _row_spec(t, W), _vec_spec(W, SUBLANES)],
        out_shape=[jax.ShapeDtypeStruct((L, W), BF16), jax.ShapeDtypeStruct((L, W), F32),
                   jax.ShapeDtypeStruct((SUBLANES, W), F32)],
        compiler_params=_params("arbitrary"),
    )(dgl, y, u, dskip)


def _merge_fwd(z, yb, gab, *, name):
    L, D = yb.shape
    t = _tile(L, ROW_TILE, SUBLANES)

    def body(z_ref, yb_ref, gab_ref, o_ref):
        ya = z_ref[:, :D] * _sigmoid(z_ref[:, D:])
        o_ref[...] = (_sigmoid(gab_ref[:, :D]) * ya + _sigmoid(gab_ref[:, D:]) * yb_ref[...]).astype(BF16)

    return pl.pallas_call(
        body, name=name, grid=(L // t,),
        in_specs=[_row_spec(t, 2 * D), _row_spec(t, D), _row_spec(t, 2 * D)],
        out_specs=_row_spec(t, D), out_shape=jax.ShapeDtypeStruct((L, D), BF16),
        compiler_params=_params("parallel"),
    )(z, yb, gab)


def _merge_bwd(dm, z, yb, gab, *, name):
    L, D = yb.shape
    t = _tile(L, ROW_TILE, SUBLANES)

    def body(dm_ref, z_ref, yb_ref, gab_ref, dz_ref, dyb_ref, dgab_ref):
        dmv = dm_ref[...]
        zv = z_ref[:, :D]
        sz = _sigmoid(z_ref[:, D:])
        sa = _sigmoid(gab_ref[:, :D])
        sb = _sigmoid(gab_ref[:, D:])
        ybv = yb_ref[...]
        dya = dmv * sa
        dz_ref[:, :D] = (dya * sz).astype(BF16)
        dz_ref[:, D:] = (dya * zv * (sz * (1.0 - sz))).astype(BF16)
        dyb_ref[...] = (dmv * sb).astype(BF16)
        dgab_ref[:, :D] = (dmv * (zv * sz) * (sa * (1.0 - sa))).astype(BF16)
        dgab_ref[:, D:] = (dmv * ybv * (sb * (1.0 - sb))).astype(BF16)

    return pl.pallas_call(
        body, name=name, grid=(L // t,),
        in_specs=[_row_spec(t, D), _row_spec(t, 2 * D), _row_spec(t, D), _row_spec(t, 2 * D)],
        out_specs=[_row_spec(t, 2 * D), _row_spec(t, D), _row_spec(t, 2 * D)],
        out_shape=[jax.ShapeDtypeStruct((L, 2 * D), BF16), jax.ShapeDtypeStruct((L, D), BF16),
                   jax.ShapeDtypeStruct((L, 2 * D), BF16)],
        compiler_params=_params("parallel"),
    )(dm, z, yb, gab)


SCAN_W = 1024
SCAN_T = 512


def _interleave(x):
    L, W = x.shape
    seg = SCAN_T // SUBLANES
    return x.reshape(L // SCAN_T, SUBLANES, seg, W).transpose(0, 2, 1, 3).reshape(L, W)


def _deinterleave(x):
    L, W = x.shape
    seg = SCAN_T // SUBLANES
    return x.reshape(L // SCAN_T, seg, SUBLANES, W).transpose(0, 2, 1, 3).reshape(L, W)


def _power_table(a, b, pr_tab, pi_tab, n):
    def fill(k, carry):
        pr, pi = carry
        pr_tab[k] = pr
        pi_tab[k] = pi
        return a * pr - b * pi, a * pi + b * pr

    lax.fori_loop(0, n, fill, (a, b))


def _rows_to_tile(rows):
    w = rows[0].shape[1]
    sub = lax.broadcasted_iota(jnp.int32, (SUBLANES, w), 0)
    tile = jnp.broadcast_to(rows[0], (SUBLANES, w))
    for j in range(1, SUBLANES):
        tile = jnp.where(sub == j, jnp.broadcast_to(rows[j], (SUBLANES, w)), tile)
    return tile


def _ssm_scan_fwd(bu_re, bu_im, lam_re, lam_im, *, name):
    L, S = bu_re.shape
    w, t = _tile(S, SCAN_W), SCAN_T
    seg = t // SUBLANES

    def body(br_ref, bi_ref, lr_ref, li_ref, sr_ref, si_ref, pr_tab, pi_tab, cr_ref, ci_ref):
        a = jnp.broadcast_to(lr_ref[...], (SUBLANES, w))
        b = jnp.broadcast_to(li_ref[...], (SUBLANES, w))

        @pl.when(pl.program_id(1) == 0)
        def _():
            cr_ref[...] = jnp.zeros_like(cr_ref)
            ci_ref[...] = jnp.zeros_like(ci_ref)
            _power_table(a, b, pr_tab, pi_tab, seg)

        def local_scan(i, carry):
            sr, si = carry
            base = pl.multiple_of(i * SUBLANES, SUBLANES)
            nr = a * sr - b * si + br_ref[pl.ds(base, SUBLANES), :]
            ni = a * si + b * sr + bi_ref[pl.ds(base, SUBLANES), :]
            sr_ref[pl.ds(base, SUBLANES), :] = nr
            si_ref[pl.ds(base, SUBLANES), :] = ni
            return nr, ni

        zero = jnp.zeros((SUBLANES, w), F32)
        fr, fi = lax.fori_loop(0, seg, local_scan, (zero, zero), unroll=2)
        lsr, lsi = pr_tab[seg - 1][0:1, :], pi_tab[seg - 1][0:1, :]
        cr, ci = cr_ref[...], ci_ref[...]
        rows_r, rows_i = [], []
        for j in range(SUBLANES):
            rows_r.append(cr)
            rows_i.append(ci)
            cr, ci = fr[j:j + 1, :] + (lsr * cr - lsi * ci), fi[j:j + 1, :] + (lsr * ci + lsi * cr)
        cr_ref[...] = cr
        ci_ref[...] = ci
        in_r, in_i = _rows_to_tile(rows_r), _rows_to_tile(rows_i)

        def add_entry(i, _):
            base = pl.multiple_of(i * SUBLANES, SUBLANES)
            pr, pi = pr_tab[i], pi_tab[i]
            sr_ref[pl.ds(base, SUBLANES), :] += pr * in_r - pi * in_i
            si_ref[pl.ds(base, SUBLANES), :] += pr * in_i + pi * in_r
            return 0

        lax.fori_loop(0, seg, add_entry, 0, unroll=2)

    blk = pl.BlockSpec((t, w), lambda j, i: (i, j))
    vec = pl.BlockSpec((1, w), lambda j, i: (0, j))
    sds = jax.ShapeDtypeStruct((L, S), F32)
    tab = pltpu.VMEM((seg, SUBLANES, w), F32)
    return pl.pallas_call(
        body, name=name, grid=(S // w, L // t),
        in_specs=[blk, blk, vec, vec], out_specs=[blk, blk], out_shape=[sds, sds],
        scratch_shapes=[tab, tab, pltpu.VMEM((1, w), F32), pltpu.VMEM((1, w), F32)],
        compiler_params=_params("parallel", "arbitrary"),
    )(bu_re, bu_im, lam_re, lam_im)


def _ssm_scan_bwd(d_re, d_im, s_re, s_im, lam_re, lam_im, *, name):
    L, S = d_re.shape
    w, t = _tile(S, SCAN_W), SCAN_T
    nt = L // t
    seg = t // SUBLANES

    def body(dr_ref, di_ref, sr_ref, si_ref, lr_ref, li_ref, gr_ref, gi_ref, ar_ref, ai_ref,
             pr_tab, pi_tab, cgr, cgi, acc_r, acc_i):
        step = pl.program_id(1)
        a = jnp.broadcast_to(lr_ref[...], (SUBLANES, w))
        b = jnp.broadcast_to(-li_ref[...], (SUBLANES, w))

        @pl.when(step == 0)
        def _():
            for r in (cgr, cgi, acc_r, acc_i):
                r[...] = jnp.zeros_like(r)
            _power_table(a, b, pr_tab, pi_tab, seg)

        def local_scan(ii, carry):
            gr, gi = carry
            base = pl.multiple_of((seg - 1 - ii) * SUBLANES, SUBLANES)
            ngr = a * gr - b * gi + dr_ref[pl.ds(base, SUBLANES), :]
            ngi = a * gi + b * gr + di_ref[pl.ds(base, SUBLANES), :]
            gr_ref[pl.ds(base, SUBLANES), :] = ngr
            gi_ref[pl.ds(base, SUBLANES), :] = ngi
            return ngr, ngi

        zero = jnp.zeros((SUBLANES, w), F32)
        fr, fi = lax.fori_loop(0, seg, local_scan, (zero, zero), unroll=2)
        lsr, lsi = pr_tab[seg - 1][0:1, :], pi_tab[seg - 1][0:1, :]
        cr, ci = cgr[...], cgi[...]
        rows_r, rows_i = [None] * SUBLANES, [None] * SUBLANES
        for j in reversed(range(SUBLANES)):
            rows_r[j], rows_i[j] = cr, ci
            cr, ci = fr[j:j + 1, :] + (lsr * cr - lsi * ci), fi[j:j + 1, :] + (lsr * ci + lsi * cr)
        cgr[...] = cr
        cgi[...] = ci
        in_r, in_i = _rows_to_tile(rows_r), _rows_to_tile(rows_i)

        def add_entry(ii, carry):
            nr, ni, xr, xi = carry
            base = pl.multiple_of((seg - 1 - ii) * SUBLANES, SUBLANES)
            sr = sr_ref[pl.ds(base, SUBLANES), :]
            si = si_ref[pl.ds(base, SUBLANES), :]
            xr = xr + (nr * sr + ni * si)
            xi = xi + (ni * sr - nr * si)
            pr, pi = pr_tab[ii], pi_tab[ii]
            gr = gr_ref[pl.ds(base, SUBLANES), :] + (pr * in_r - pi * in_i)
            gi = gi_ref[pl.ds(base, SUBLANES), :] + (pr * in_i + pi * in_r)
            gr_ref[pl.ds(base, SUBLANES), :] = gr
            gi_ref[pl.ds(base, SUBLANES), :] = gi
            return gr, gi, xr, xi

        _, _, xr, xi = lax.fori_loop(0, seg, add_entry, (in_r, in_i, acc_r[...], acc_i[...]), unroll=2)
        acc_r[...] = xr
        acc_i[...] = xi

        @pl.when(step == nt - 1)
        def _():
            ar_ref[...] = jnp.sum(xr, axis=0, keepdims=True)
            ai_ref[...] = jnp.sum(xi, axis=0, keepdims=True)

    blk = pl.BlockSpec((t, w), lambda j, i: (nt - 1 - i, j))
    vec = pl.BlockSpec((1, w), lambda j, i: (0, j))
    sds = jax.ShapeDtypeStruct((L, S), F32)
    vsds = jax.ShapeDtypeStruct((1, S), F32)
    tab = pltpu.VMEM((seg, SUBLANES, w), F32)
    tile = pltpu.VMEM((SUBLANES, w), F32)
    return pl.pallas_call(
        body, name=name, grid=(S // w, nt),
        in_specs=[blk, blk, blk, blk, vec, vec], out_specs=[blk, blk, vec, vec],
        out_shape=[sds, sds, vsds, vsds],
        scratch_shapes=[tab, tab, pltpu.VMEM((1, w), F32), pltpu.VMEM((1, w), F32), tile, tile],
        compiler_params=_params("parallel", "arbitrary"),
    )(d_re, d_im, s_re, s_im, lam_re, lam_im)


def _ssm_discretize(a_re, a_im, log_dt, b_re, b_im, c_re, c_im, d_skip):
    G, P, N = SSM_GROUPS, SSM_STATE, SSM_GROUP
    a = jnp.minimum(a_re, -1e-4)
    dt = jnp.exp(log_dt)[:, None]
    mag = jnp.exp(a * dt)
    lr = mag * jnp.cos(a_im * dt)
    li = mag * jnp.sin(a_im * dt)
    den = a * a + a_im * a_im
    cr = ((lr - 1.0) * a + li * a_im) / den
    ci = (li * a - (lr - 1.0) * a_im) / den
    bbr = cr[..., None] * b_re - ci[..., None] * b_im
    bbi = cr[..., None] * b_im + ci[..., None] * b_re
    eye = jnp.eye(G, dtype=F32)

    def in_map(bb):
        return (eye[:, None, :, None] * bb.transpose(0, 2, 1)[:, :, None, :]).reshape(G * N, G * P)

    def out_map(c):
        return (eye[:, None, :, None] * c.transpose(0, 2, 1)[:, :, None, :]).reshape(G * P, G * N)

    return (lr.reshape(1, G * P), li.reshape(1, G * P), in_map(bbr), in_map(bbi),
            out_map(c_re), out_map(-c_im), d_skip.reshape(1, SSM_WIDTH))


ATT_T = 512
CUM_T = 256


def _split3(x):
    hi = x.astype(BF16)
    r1 = x - hi.astype(F32)
    mid = r1.astype(BF16)
    lo = (r1 - mid.astype(F32)).astype(BF16)
    return hi, mid, lo


def _tri_dot(tri, x):
    hi, mid, lo = _split3(x)
    dot = lambda p: jnp.dot(tri, p, preferred_element_type=F32)
    return dot(hi) + dot(mid) + dot(lo)


def _log_sigmoid(x):
    return jnp.minimum(x, 0.0) - jnp.log(1.0 + jnp.exp(-jnp.abs(x)))


def _fox_cum(f, fb, *, name):
    L, W = f.shape
    t = _tile(L, CUM_T, SUBLANES)

    def body(f_ref, b_ref, o_ref, carry):
        @pl.when(pl.program_id(0) == 0)
        def _():
            carry[...] = jnp.zeros_like(carry)

        row = lax.broadcasted_iota(jnp.int32, (t, t), 0)
        col = lax.broadcasted_iota(jnp.int32, (t, t), 1)
        tri = jnp.where(col <= row, 1.0, 0.0).astype(BF16)
        c = _tri_dot(tri, _log_sigmoid(f_ref[...] + b_ref[...])) + carry[...]
        o_ref[...] = c
        carry[...] = c[t - 1:t, :]

    return pl.pallas_call(
        body, name=name, grid=(L // t,),
        in_specs=[_row_spec(t, W), _vec_spec(W)], out_specs=_row_spec(t, W),
        out_shape=jax.ShapeDtypeStruct((L, W), F32),
        scratch_shapes=[pltpu.VMEM((1, W), F32)], compiler_params=_params("arbitrary"),
    )(f, fb)


def _fox_cum_bwd(dcum, f, fb, *, name):
    L, W = f.shape
    t = _tile(L, CUM_T, SUBLANES)
    n = L // t

    def body(d_ref, f_ref, b_ref, o_ref, db_ref, carry):
        i = pl.program_id(0)

        @pl.when(i == 0)
        def _():
            carry[...] = jnp.zeros_like(carry)
            db_ref[...] = jnp.zeros_like(db_ref)

        row = lax.broadcasted_iota(jnp.int32, (t, t), 0)
        col = lax.broadcasted_iota(jnp.int32, (t, t), 1)
        tri = jnp.where(col >= row, 1.0, 0.0).astype(BF16)
        dlog = _tri_dot(tri, d_ref[...]) + carry[...]
        carry[...] = dlog[0:1, :]
        df = dlog * _sigmoid(-(f_ref[...] + b_ref[...]))
        o_ref[...] = df.astype(BF16)
        db_ref[...] += _colsum8(df)
        _finish_colsums(i, n - 1, (db_ref,))

    rev = pl.BlockSpec((t, W), lambda i: (n - 1 - i, 0))
    return pl.pallas_call(
        body, name=name, grid=(n,),
        in_specs=[rev, rev, _vec_spec(W)], out_specs=[rev, _vec_spec(W, SUBLANES)],
        out_shape=[jax.ShapeDtypeStruct((L, W), BF16), jax.ShapeDtypeStruct((SUBLANES, W), F32)],
        scratch_shapes=[pltpu.VMEM((1, W), F32)], compiler_params=_params("arbitrary"),
    )(dcum, f, fb)


def _head_col(blk, h):
    lane = lax.broadcasted_iota(jnp.int32, blk.shape, 1)
    return jnp.sum(jnp.where(lane == h * HEAD_DIM, blk, 0.0), axis=1, keepdims=True)


def _lo_mask(rows):
    return lax.broadcasted_iota(jnp.int32, (rows, LANES), 1) < HEAD_DIM


def _causal(t):
    row = lax.broadcasted_iota(jnp.int32, (t, t), 0)
    col = lax.broadcasted_iota(jnp.int32, (t, t), 1)
    return col <= row


def _fox_fwd(qkv, cum_cols, cum_rows, *, name):
    L = qkv.shape[0]
    t = _tile(L, ATT_T)
    nq = L // t
    npair = ATTN_HEADS // 2

    def body(q_ref, k_ref, v_ref, cc_ref, cr_ref, o_ref, o32_ref, lse_ref):
        iq = pl.program_id(1)
        lo = _lo_mask(t)
        qv = q_ref[...] * 0.125
        zq = jnp.zeros_like(qv)
        qh = (jnp.where(lo, qv, zq), jnp.where(lo, zq, qv))
        ccv = cc_ref[...]
        cq = (_head_col(ccv, 0), _head_col(ccv, 1))

        def step(ik, carry, masked):
            start = pl.multiple_of(ik * t, t)
            kb = k_ref[pl.ds(start, t), :]
            vb = v_ref[pl.ds(start, t), :]
            out = []
            for h in range(2):
                m, l, acc = carry[h]
                s = lax.dot_general(qh[h], kb, NT, preferred_element_type=F32)
                s = s + (cq[h] - cr_ref[h:h + 1, pl.ds(start, t)])
                if masked:
                    s = jnp.where(_causal(t), s, -jnp.inf)
                m_new = jnp.maximum(m, jnp.max(s, axis=1, keepdims=True))
                alpha = jnp.exp(m - m_new)
                p = jnp.exp(s - m_new)
                l = alpha * l + jnp.sum(p, axis=1, keepdims=True)
                acc = alpha * acc + jnp.dot(p.astype(BF16), vb, preferred_element_type=F32)
                out.append((m_new, l, acc))
            return tuple(out)

        init1 = (jnp.full((t, 1), -jnp.inf, F32), jnp.zeros((t, 1), F32), jnp.zeros((t, LANES), F32))
        carry = lax.fori_loop(0, iq, lambda ik, c: step(ik, c, False), (init1, init1))
        (m0, l0, a0), (m1, l1, a1) = step(iq, carry, True)
        out = jnp.where(lo, a0 / l0, a1 / l1)
        o_ref[...] = out.astype(BF16)
        o32_ref[...] = out
        lse_ref[...] = jnp.where(lo, m0 + jnp.log(l0), m1 + jnp.log(l1))

    blk = lambda off: pl.BlockSpec((t, LANES), lambda hp, iq: (iq, off + hp))
    whole = lambda off: pl.BlockSpec((L, LANES), lambda hp, iq: (0, off + hp))
    return pl.pallas_call(
        body, name=name, grid=(npair, nq),
        in_specs=[blk(0), whole(npair), whole(2 * npair), blk(0),
                  pl.BlockSpec((None, SUBLANES, L), lambda hp, iq: (hp, 0, 0))],
        out_specs=[blk(0), blk(0), blk(0)],
        out_shape=[jax.ShapeDtypeStruct((L, ATTN_WIDTH), BF16), jax.ShapeDtypeStruct((L, ATTN_WIDTH), F32),
                   jax.ShapeDtypeStruct((L, ATTN_WIDTH), F32)],
        compiler_params=_params("parallel", "arbitrary"),
    )(qkv, qkv, qkv, cum_cols, cum_rows)


STAT_LSE, STAT_CUM, STAT_DELTA = 0, 2, 4


def _lane_col(blk, idx):
    lane = lax.broadcasted_iota(jnp.int32, blk.shape, 1)
    return jnp.sum(jnp.where(lane == idx, blk, 0.0), axis=1, keepdims=True)


def _fox_rowstats(do, o, lse, cum_cols, *, name):
    L = do.shape[0]
    t = _tile(L, ATT_T)

    def body(do_ref, o_ref, lse_ref, cc_ref, st_ref):
        lo = _lo_mask(t)
        dd = do_ref[...].astype(F32) * o_ref[...]
        lsev, ccv = lse_ref[...], cc_ref[...]
        cols = (_head_col(lsev, 0), _head_col(lsev, 1), _head_col(ccv, 0), _head_col(ccv, 1),
                jnp.sum(jnp.where(lo, dd, 0.0), axis=1, keepdims=True),
                jnp.sum(jnp.where(lo, 0.0, dd), axis=1, keepdims=True))
        lane = lax.broadcasted_iota(jnp.int32, (t, LANES), 1)
        out = jnp.zeros((t, LANES), F32)
        for i, col in enumerate(cols):
            out = jnp.where(lane == i, col, out)
        st_ref[...] = out

    blk = pl.BlockSpec((t, LANES), lambda hp, i: (i, hp))
    return pl.pallas_call(
        body, name=name, grid=(ATTN_HEADS // 2, L // t),
        in_specs=[blk, blk, blk, blk], out_specs=blk,
        out_shape=jax.ShapeDtypeStruct((L, ATTN_WIDTH), F32),
        compiler_params=_params("parallel", "parallel"),
    )(do, o, lse, cum_cols)


def _fox_bwd(qkv, do, stats, cum_rows, *, name):
    L = qkv.shape[0]
    t = _tile(L, ATT_T)
    nq = L // t
    npair = ATTN_HEADS // 2

    def body(q_ref, do_ref, st_ref, k_ref, v_ref, cr_ref, dk_ref, dv_ref, dc_ref, dq_ref, drow_ref):
        ik = pl.program_id(1)

        @pl.when(ik == 0)
        def _():
            dq_ref[...] = jnp.zeros_like(dq_ref)
            drow_ref[...] = jnp.zeros_like(drow_ref)

        lo = _lo_mask(t)
        lane = lax.broadcasted_iota(jnp.int32, (t, LANES), 1)
        kb = k_ref[...]
        vb = v_ref[...]
        zk = jnp.zeros_like(kb)
        kh = (jnp.where(lo, kb, zk), jnp.where(lo, zk, kb))
        vh = (jnp.where(lo, vb, zk), jnp.where(lo, zk, vb))
        ck = (cr_ref[0:1, :], cr_ref[1:2, :])

        def step(iq, carry, masked):
            dk, dv, dc0, dc1 = carry
            start = pl.multiple_of(iq * t, t)
            qb = q_ref[pl.ds(start, t), :] * 0.125
            dob = do_ref[pl.ds(start, t), :]
            stb = st_ref[pl.ds(start, t), :]
            dks, dvs, dcs, dqs, rss = [], [], [], [], []
            for h in range(2):
                s = lax.dot_general(qb, kh[h], NT, preferred_element_type=F32)
                s = s + (_lane_col(stb, STAT_CUM + h) - ck[h])
                if masked:
                    s = jnp.where(_causal(t), s, -jnp.inf)
                p = jnp.exp(s - _lane_col(stb, STAT_LSE + h))
                dp = lax.dot_general(dob, vh[h], NT, preferred_element_type=F32)
                ds = p * (dp - _lane_col(stb, STAT_DELTA + h))
                dsb = ds.astype(BF16)
                dvs.append(lax.dot_general(p.astype(BF16), dob, TN, preferred_element_type=F32))
                dks.append(lax.dot_general(dsb, qb, TN, preferred_element_type=F32))
                dqs.append(jnp.dot(dsb, kh[h], preferred_element_type=F32))
                dcs.append(jnp.sum(ds, axis=0, keepdims=True))
                rss.append(jnp.sum(ds, axis=1, keepdims=True))
            dq_ref[pl.ds(start, t), :] += 0.125 * (dqs[0] + dqs[1])
            drow_ref[pl.ds(start, t), :] += jnp.where(lane == 0, rss[0], jnp.where(lane == 1, rss[1], 0.0))
            return (dk + jnp.where(lo, dks[0], dks[1]), dv + jnp.where(lo, dvs[0], dvs[1]),
                    dc0 - dcs[0], dc1 - dcs[1])

        zero = jnp.zeros((t, LANES), F32)
        zrow = jnp.zeros((1, t), F32)
        carry = step(ik, (zero, zero, zrow, zrow), True)
        dk, dv, dc0, dc1 = lax.fori_loop(ik + 1, nq, lambda iq, c: step(iq, c, False), carry)
        dk_ref[...] = dk.astype(BF16)
        dv_ref[...] = dv.astype(BF16)
        dc_ref[...] = jnp.zeros_like(dc_ref)
        dc_ref[0:1, :] = dc0
        dc_ref[1:2, :] = dc1

    whole = lambda off: pl.BlockSpec((L, LANES), lambda hp, ik: (0, off + hp))
    blk = lambda off: pl.BlockSpec((t, LANES), lambda hp, ik: (ik, off + hp))
    rows = pl.BlockSpec((None, SUBLANES, t), lambda hp, ik: (hp, 0, ik))
    return pl.pallas_call(
        body, name=name, grid=(npair, nq),
        in_specs=[whole(0), whole(0), whole(0), blk(npair), blk(2 * npair), rows],
        out_specs=[blk(0), blk(0), rows, whole(0), whole(0)],
        out_shape=[jax.ShapeDtypeStruct((L, ATTN_WIDTH), BF16), jax.ShapeDtypeStruct((L, ATTN_WIDTH), BF16),
                   jax.ShapeDtypeStruct((npair, SUBLANES, L), F32),
                   jax.ShapeDtypeStruct((L, ATTN_WIDTH), F32), jax.ShapeDtypeStruct((L, ATTN_WIDTH), F32)],
        compiler_params=_params("parallel", "arbitrary"),
    )(qkv, do, stats, qkv, qkv, cum_rows)


def _mod_partial(c_all, mod_w, mod_b_cols, *, name):
    depth, K, cols = mod_w.shape
    tn = _tile(cols, 768)

    def body(c_ref, w_ref, b_ref, o_ref):
        cv = c_ref[...]
        sc = (cv * _sigmoid(cv)).astype(BF16)
        o_ref[...] = jnp.dot(sc, w_ref[...].astype(BF16), preferred_element_type=F32) + b_ref[...]

    return pl.pallas_call(
        body, name=name, grid=(depth, cols // tn),
        in_specs=[pl.BlockSpec((N_DEV, K), lambda l, j: (0, 0)),
                  pl.BlockSpec((None, K, tn), lambda l, j: (l, 0, j)),
                  pl.BlockSpec((None, 1, tn), lambda l, j: (l, 0, j))],
        out_specs=pl.BlockSpec((None, N_DEV, tn), lambda l, j: (l, 0, j)),
        out_shape=jax.ShapeDtypeStruct((depth, N_DEV, cols), F32),
        compiler_params=_params("parallel", "parallel"),
    )(c_all, mod_w, mod_b_cols)


def _mod_wgrad(c_all_t, dmod, *, name):
    depth, nb, cols = dmod.shape
    K = c_all_t.shape[0]
    tn = _tile(cols, 768)
    tk = _tile(K, 256, SUBLANES)

    def body(c_ref, d_ref, o_ref):
        cv = c_ref[...]
        sc = cv * _sigmoid(cv)
        dv = d_ref[...]
        acc = sc[:, 0:1] * dv[0:1, :]
        for b in range(1, nb):
            acc = acc + sc[:, b:b + 1] * dv[b:b + 1, :]
        o_ref[...] = acc

    return pl.pallas_call(
        body, name=name, grid=(depth, K // tk, cols // tn),
        in_specs=[pl.BlockSpec((tk, nb), lambda l, i, j: (i, 0)),
                  pl.BlockSpec((None, nb, tn), lambda l, i, j: (l, 0, j))],
        out_specs=pl.BlockSpec((None, tk, tn), lambda l, i, j: (l, i, j)),
        out_shape=jax.ShapeDtypeStruct((depth, K, cols), F32),
        compiler_params=_params("parallel", "parallel", "parallel"),
    )(c_all_t, dmod)


def _adamw(w, g, m, v, *, name):
    shape = w.shape
    cols = shape[-1]
    rows = int(np.prod(shape[:-1]))
    t = _tile(rows, 256, SUBLANES) if rows % SUBLANES == 0 else rows
    r2 = lambda a: a.reshape(rows, cols)

    def body(w_ref, g_ref, m_ref, v_ref, d_ref, nm_ref, nv_ref):
        gv = g_ref[...]
        nm = ADAM_B1 * m_ref[...] + (1.0 - ADAM_B1) * gv
        nv = ADAM_B2 * v_ref[...] + (1.0 - ADAM_B2) * (gv * gv)
        m_hat = nm / (1.0 - ADAM_B1 ** ADAM_STEP)
        v_hat = nv / (1.0 - ADAM_B2 ** ADAM_STEP)
        d_ref[...] = -ADAM_LR * (m_hat / (jnp.sqrt(v_hat) + ADAM_EPS) + ADAM_WD * w_ref[...])
        nm_ref[...] = nm
        nv_ref[...] = nv

    spec = pl.BlockSpec((t, cols), lambda i: (i, 0))
    sds = jax.ShapeDtypeStruct((rows, cols), F32)
    d, nm, nv = pl.pallas_call(
        body, name=name, grid=(rows // t,),
        in_specs=[spec] * 4, out_specs=[spec] * 3, out_shape=[sds] * 3,
        compiler_params=_params("parallel"),
    )(r2(w), r2(g), r2(m), r2(v))
    return d.reshape(shape), nm.reshape(shape), nv.reshape(shape)


def _my_place():
    return lax.axis_index("x"), lax.axis_index("y"), lax.axis_index("c")


def _other_chips(x, y):
    return [(1 - x, y), (x, 1 - y), (1 - x, 1 - y)]


def _all_gather8(v, *, name, with_sum=False):
    m, n = v.shape

    def body(x_ref, out_ref, *rest):
        if with_sum:
            sum_ref, send_sems, recv_sems, local_sem = rest
        else:
            send_sems, recv_sems, local_sem = rest
        x, y, c = _my_place()
        me, sibling = (x, y, c), (x, y, 1 - c)
        chips = _other_chips(x, y)

        def rows(px, py, pc):
            return out_ref.at[pl.ds((4 * px + 2 * py + pc) * m, m), :]

        def copy(k, block, to, src=None):
            return pltpu.make_async_remote_copy(
                src_ref=rows(*block) if src is None else src, dst_ref=rows(*block),
                send_sem=send_sems.at[k], recv_sem=recv_sems.at[k], device_id=to, device_id_type=MESH)

        mine = pltpu.make_async_copy(x_ref, rows(*me), local_sem)
        mine.start()
        first = [copy(0, me, sibling, src=x_ref)]
        first += [copy(1 + j, me, (*chip, c), src=x_ref) for j, chip in enumerate(chips)]
        for cp in first:
            cp.start()
        passed = [copy(4 + j, (*chip, c), sibling) for j, chip in enumerate(chips)]
        for j, chip in enumerate(chips):
            copy(1 + j, (*chip, c), me).wait_recv()
            passed[j].start()
        copy(0, sibling, me).wait_recv()
        for j, chip in enumerate(chips):
            copy(4 + j, (*chip, 1 - c), me).wait_recv()
        for cp in first + passed:
            cp.wait_send()
        mine.wait()
        if with_sum:
            acc = out_ref[pl.ds(0, m), :]
            for d in range(1, N_DEV):
                acc = acc + out_ref[pl.ds(d * m, m), :]
            sum_ref[...] = acc

    vm = pl.BlockSpec(memory_space=pltpu.VMEM)
    out_shape = [jax.ShapeDtypeStruct((N_DEV * m, n), F32)]
    if with_sum:
        out_shape.append(jax.ShapeDtypeStruct((m, n), F32))
    res = pl.pallas_call(
        body, name=name, out_shape=out_shape, in_specs=[vm], out_specs=[vm] * len(out_shape),
        scratch_shapes=[pltpu.SemaphoreType.DMA((7,)), pltpu.SemaphoreType.DMA((7,)), pltpu.SemaphoreType.DMA],
        compiler_params=pltpu.CompilerParams(vmem_limit_bytes=VMEM_LIMIT),
    )(v)
    return res if with_sum else res[0]


class _Cut(NamedTuple):
    shape: tuple
    slab: int
    half: int


IN_WIDTH = SSM_WIDTH + 3 * ATTN_WIDTH + ATTN_HEADS + 2 * D_MODEL
CUTS = dict(
    ffn_w_in=_Cut((1, D_MODEL, 2 * D_FF), 2, 1),
    ffn_w_out=_Cut((1, D_FF, D_MODEL), 1, 2),
    mix_w_in=_Cut((N_CHIPS, D_MODEL, IN_WIDTH // N_CHIPS), 0, 1),
    glu_w=_Cut((1, SSM_WIDTH, 2 * D_MODEL), 2, 1),
    attn_w_out=_Cut((1, ATTN_WIDTH, D_MODEL), 2, 1),
    mix_w_out=_Cut((1, D_MODEL, D_MODEL), 1, 2),
)
LAYER_MATS = (("ffn_w_in", 0), ("ffn_w_in", 1), ("ffn_w_out", 0), ("ffn_w_out", 1), ("mix_w_in", None),
              ("glu_w", None), ("attn_w_out", None), ("mix_w_out", None))


def _part_shape(cut, slab=False, half=False):
    s = list(cut.shape)
    if slab:
        s[cut.slab] //= N_CHIPS
    if half:
        s[cut.half] //= 2
    return tuple(s)


def _window(ref, cut, slab=None, half=None):
    idx = [slice(None)] * len(cut.shape)
    for axis, parts, which in ((cut.slab, N_CHIPS, slab), (cut.half, 2, half)):
        if which is not None:
            width = cut.shape[axis] // parts
            idx[axis] = pl.ds(pl.multiple_of(which * width, width), width)
    return ref.at[tuple(idx)]


HBM_SPEC = pl.BlockSpec(memory_space=pltpu.HBM)


def _gather_layer(shards, cuts, *, name):
    n = len(shards)

    def body(*refs):
        s_refs, f_refs = refs[:n], refs[n:2 * n]
        send_sems, recv_sems, local_sems = refs[2 * n:]
        x, y, c = _my_place()
        me, sibling, mine = (x, y, c), (x, y, 1 - c), 2 * x + y
        chips = _other_chips(x, y)

        def copy(i, k, src, dst, to):
            return pltpu.make_async_remote_copy(
                src_ref=src, dst_ref=dst, send_sem=send_sems.at[6 * i + k], recv_sem=recv_sems.at[6 * i + k],
                device_id=to, device_id_type=MESH)

        local = [pltpu.make_async_copy(s_refs[i], _window(f_refs[i], cuts[i], slab=mine), local_sems.at[i])
                 for i in range(n)]
        for cp in local:
            cp.start()
        started = []
        for i in range(n):
            for j, chip in enumerate(chips):
                cp = copy(i, j, _window(s_refs[i], cuts[i], half=c), _window(f_refs[i], cuts[i], slab=mine, half=c),
                          (*chip, c))
                cp.start()
                started.append(cp)
        for i in range(n):
            for j, chip in enumerate(chips):
                part = _window(f_refs[i], cuts[i], slab=2 * chip[0] + chip[1], half=c)
                copy(i, j, part, part, me).wait_recv()
                cp = copy(i, 3 + j, part, part, sibling)
                cp.start()
                started.append(cp)
        for i in range(n):
            for j, chip in enumerate(chips):
                part = _window(f_refs[i], cuts[i], slab=2 * chip[0] + chip[1], half=1 - c)
                copy(i, 3 + j, part, part, me).wait_recv()
        for cp in started:
            cp.wait_send()
        for cp in local:
            cp.wait()

    return pl.pallas_call(
        body, name=name, out_shape=[jax.ShapeDtypeStruct(cut.shape, s.dtype) for s, cut in zip(shards, cuts)],
        in_specs=[HBM_SPEC] * n, out_specs=[HBM_SPEC] * n,
        scratch_shapes=[pltpu.SemaphoreType.DMA((6 * n,)), pltpu.SemaphoreType.DMA((6 * n,)),
                        pltpu.SemaphoreType.DMA((n,))],
    )(*shards)


def _swap_layer(mats, cuts, *, name):
    n = len(mats)

    def body(*refs):
        m_refs, r_refs = refs[:n], refs[n:2 * n]
        send_sems, recv_sems = refs[2 * n:]
        x, y, c = _my_place()
        cps = [pltpu.make_async_remote_copy(
            src_ref=_window(m_refs[i], cuts[i], half=1 - c), dst_ref=r_refs[i], send_sem=send_sems.at[i],
            recv_sem=recv_sems.at[i], device_id=(x, y, 1 - c), device_id_type=MESH) for i in range(n)]
        for cp in cps:
            cp.start()
        for cp in cps:
            cp.wait()

    return pl.pallas_call(
        body, name=name, out_shape=[jax.ShapeDtypeStruct(_part_shape(cut, half=True), m.dtype)
                                    for m, cut in zip(mats, cuts)],
        in_specs=[HBM_SPEC] * n, out_specs=[HBM_SPEC] * n,
        scratch_shapes=[pltpu.SemaphoreType.DMA((n,)), pltpu.SemaphoreType.DMA((n,))],
    )(*mats)


def _partials_layer(sums, cuts, *, name):
    n = len(sums)

    def body(*refs):
        s_refs, p_refs = refs[:n], refs[n:2 * n]
        send_sems, recv_sems = refs[2 * n:]
        x, y, c = _my_place()
        cps = []
        for i in range(n):
            for j, chip in enumerate(_other_chips(x, y)):
                cps.append(pltpu.make_async_remote_copy(
                    src_ref=_window(s_refs[i], cuts[i], slab=2 * chip[0] + chip[1]), dst_ref=p_refs[i].at[j],
                    send_sem=send_sems.at[3 * i + j], recv_sem=recv_sems.at[3 * i + j],
                    device_id=(*chip, c), device_id_type=MESH))
        for cp in cps:
            cp.start()
        for cp in cps:
            cp.wait()

    return pl.pallas_call(
        body, name=name,
        out_shape=[jax.ShapeDtypeStruct((3,) + _part_shape(cut, slab=True, half=True), s.dtype)
                   for s, cut in zip(sums, cuts)],
        in_specs=[HBM_SPEC] * n, out_specs=[HBM_SPEC] * n,
        scratch_shapes=[pltpu.SemaphoreType.DMA((3 * n,)), pltpu.SemaphoreType.DMA((3 * n,))],
    )(*sums)


def _share_all(reduced, cuts, places, out_shapes, *, name):
    n = len(reduced)
    names = list(out_shapes)

    def body(*refs):
        r_refs, o_refs = refs[:n], dict(zip(names, refs[n:n + len(names)]))
        send_sems, recv_sems, local_sems = refs[n + len(names):]
        x, y, c = _my_place()

        def dst(i, half):
            slab_cut = _Cut(_part_shape(cuts[i], slab=True), cuts[i].slab, cuts[i].half)
            return _window(o_refs[places[i][0]].at[places[i][1]], slab_cut, half=half)

        local = [pltpu.make_async_copy(r_refs[i], dst(i, c), local_sems.at[i]) for i in range(n)]
        cps = [pltpu.make_async_remote_copy(
            src_ref=r_refs[i], dst_ref=dst(i, c), send_sem=send_sems.at[i], recv_sem=recv_sems.at[i],
            device_id=(x, y, 1 - c), device_id_type=MESH) for i in range(n)]
        for cp in local + cps:
            cp.start()
        for i, cp in enumerate(cps):
            cp.wait_send()
            pltpu.make_async_remote_copy(
                src_ref=r_refs[i], dst_ref=dst(i, 1 - c), send_sem=send_sems.at[i], recv_sem=recv_sems.at[i],
                device_id=(x, y, 1 - c), device_id_type=MESH).wait_recv()
        for cp in local:
            cp.wait()

    return pl.pallas_call(
        body, name=name, out_shape=[jax.ShapeDtypeStruct(out_shapes[k], F32) for k in names],
        in_specs=[HBM_SPEC] * n, out_specs=[HBM_SPEC] * len(names),
        scratch_shapes=[pltpu.SemaphoreType.DMA((n,)), pltpu.SemaphoreType.DMA((n,)), pltpu.SemaphoreType.DMA((n,))],
    )(*reduced)


def _cut_blocks(shape):
    _, R, C = shape
    tr = _tile(R, 256, 16)
    tc = _tile(C, 2048) if C % LANES == 0 else C
    return (None, tr, tc), (shape[0], R // tr, C // tc)


def _offset_map(axis, blocks):
    def index_map(b, i, j, which):
        idx = [b, i, j]
        idx[axis] = which[0] * blocks[axis] + idx[axis]
        return tuple(idx)
    return index_map


def _add_half(mat, other, cut, c_idx, *, name):
    shape = _part_shape(cut, half=True)
    block, grid = _cut_blocks(shape)

    def body(c_ref, m_ref, o_ref, f_ref, b_ref):
        s = m_ref[...] + o_ref[...]
        f_ref[...] = s
        b_ref[...] = s.astype(BF16)

    plain = pl.BlockSpec(block, lambda b, i, j, which: (b, i, j))
    grid_spec = pltpu.PrefetchScalarGridSpec(
        num_scalar_prefetch=1, grid=grid,
        in_specs=[pl.BlockSpec(block, _offset_map(cut.half, grid)), plain], out_specs=[plain, plain])
    return pl.pallas_call(
        body, name=name, grid_spec=grid_spec,
        out_shape=[jax.ShapeDtypeStruct(shape, F32), jax.ShapeDtypeStruct(shape, BF16)],
        compiler_params=_params("parallel", "parallel", "parallel"),
    )(c_idx, mat, other)


def _sum_slab(own, parts, cut, chip_idx, *, name):
    shape = _part_shape(cut, slab=True, half=True)
    block, grid = _cut_blocks(shape)
    assert shape[0] == 1

    def body(k_ref, o_ref, p_ref, out_ref):
        acc = o_ref[...]
        for j in range(3):
            acc = acc + p_ref[j].astype(F32)
        out_ref[...] = acc

    grid_spec = pltpu.PrefetchScalarGridSpec(
        num_scalar_prefetch=1, grid=grid,
        in_specs=[pl.BlockSpec(block, _offset_map(cut.slab, grid)),
                  pl.BlockSpec((3,) + block[1:], lambda b, i, j, which: (0, i, j))],
        out_specs=pl.BlockSpec(block, lambda b, i, j, which: (b, i, j)))
    return pl.pallas_call(
        body, name=name, grid_spec=grid_spec, out_shape=jax.ShapeDtypeStruct(shape, F32),
        compiler_params=_params("parallel", "parallel", "parallel"),
    )(chip_idx, own, parts.reshape((3,) + shape[1:]))


def _pad_rows(flat, cols=8 * LANES, align=SUBLANES):
    n = flat.shape[0]
    rows = -(-n // (cols * align)) * align
    return jnp.pad(flat, (0, rows * cols - n)).reshape(rows, cols)


def _row(v):
    return v.reshape(1, -1)


def _ffn_fwd(x, mod, g_pre, g_post, w_in, w_out, tag):
    sh, sc, gate = _row(mod[0]), _row(mod[1]), _row(mod[2])
    h = _prenorm(x, _row(g_pre), sc, sh, name=f"prenorm_{tag}")
    gt, up, act = _mm_swiglu(h, w_in, name=f"swiglu_{tag}")
    y = _mm(act, w_out, name=f"ffn_out_{tag}", tk=1408)
    x_out = _postnorm_res(x, y, _row(g_post), gate, FFN_RES, name=f"postnorm_{tag}")
    return x_out, (x, h, gt, up, act, y)


def _ffn_bwd(dxo, saved, mod, g_pre, g_post, w_in, w_out, tag):
    x, h, gt, up, act, y = saved
    sc, gate = _row(mod[1]), _row(mod[2])
    dy, dgate, dgpost = _postnorm_bwd(dxo, y, _row(g_post), gate, FFN_RES, name=f"postnorm_bwd_{tag}")
    dgt, dup = _mm_swiglu_bwd(dy, w_out, gt, up, name=f"swiglu_bwd_{tag}")
    dw_out = _mm(act, dy, ta=True, name=f"dw_out_{tag}", tm=1408, tn=1024, tk=1024)
    dh = _mm(dgt, w_in, tb=True, name=f"dh_gate_{tag}", tk=1408)
    dh = _mm(dup, w_in, tb=True, b_k0=D_FF, bias=dh, name=f"dh_up_{tag}", tk=1408)
    dwg = _mm(h, dgt, ta=True, name=f"dw_gate_{tag}", tm=1024, tn=1408, tk=1024)
    dwu = _mm(h, dup, ta=True, name=f"dw_up_{tag}", tm=1024, tn=1408, tk=1024)
    dx, dsh, dsc, dgpre = _prenorm_bwd(x, dh, dxo, _row(g_pre), sc, name=f"prenorm_bwd_{tag}")
    dmod = jnp.stack([dsh[0], dsc[0], dgate[0]])
    return dx, dmod, dgpre[0], dgpost[0], jnp.concatenate([dwg, dwu], axis=1), dw_out


def _split_mix_w_in(w):
    u0, q0, f0, g0 = 0, SSM_WIDTH, SSM_WIDTH + 3 * ATTN_WIDTH, SSM_WIDTH + 3 * ATTN_WIDTH + ATTN_HEADS
    w_f = jnp.pad(w[:, f0:g0], ((0, 0), (0, LANES - ATTN_HEADS)))
    return w[:, u0:q0], w[:, q0:f0], w_f, w[:, g0:]


def _mixer_fwd(x, mod, g_pre, g_post, w, ssm, forget_b, tag):
    L = x.shape[0]
    sh, sc, gate = _row(mod[0]), _row(mod[1]), _row(mod[2])
    lam_re, lam_im, bin_re, bin_im, cout_re, cout_im, dskip = ssm
    h = _prenorm(x, _row(g_pre), sc, sh, name=f"prenorm_{tag}")
    u = _mm(h, w["u"], name=f"proj_u_{tag}")
    qkv = _mm(h, w["qkv"], out_dtype=BF16, name=f"proj_qkv_{tag}")
    f = _mm(h, w["f"], name=f"proj_f_{tag}")
    gab = _mm(h, w["gab"], name=f"proj_gab_{tag}")
    u_i = _interleave(u)
    bu_re = _mm(u_i, bin_re, name=f"ssm_bu_re_{tag}")
    bu_im = _mm(u_i, bin_im, name=f"ssm_bu_im_{tag}")
    s_re, s_im = _ssm_scan_fwd(bu_re, bu_im, lam_re, lam_im, name=f"ssm_scan_{tag}")
    y_ssm = _mm(s_re, cout_re, bias=u_i, bscale=dskip, name=f"ssm_y_re_{tag}", tk=2048)
    y_ssm = _deinterleave(_mm(s_im, cout_im, bias=y_ssm, name=f"ssm_y_im_{tag}", tk=2048))
    gl = _gelu_fwd(y_ssm, name=f"gelu_{tag}")
    z = _mm(gl, w["glu"], name=f"glu_{tag}")
    fb = jnp.pad(forget_b, (0, LANES - ATTN_HEADS)).reshape(1, LANES)
    cum = _fox_cum(f, fb, name=f"fox_cum_{tag}")
    cum8 = cum[:, :ATTN_HEADS]
    cum_cols = jnp.repeat(cum8, HEAD_DIM, axis=1)
    cum_rows = jnp.pad(cum8.T.reshape(ATTN_HEADS // 2, 2, L), ((0, 0), (0, SUBLANES - 2), (0, 0)))
    attn, attn32, lse = _fox_fwd(qkv, cum_cols, cum_rows, name=f"fox_fwd_{tag}")
    yb = _mm(attn, w["attn_out"], name=f"attn_out_{tag}")
    merged = _merge_fwd(z, yb, gab, name=f"merge_{tag}")
    y = _mm(merged, w["out"], name=f"mix_out_{tag}")
    x_out = _postnorm_res(x, y, _row(g_post), gate, 1.0, name=f"postnorm_{tag}")
    saved = (x, h, u, u_i, qkv, f, gab, s_re, s_im, y_ssm, gl, z, fb, cum_cols, cum_rows, attn, attn32, lse, yb,
             merged, y)
    return x_out, saved


def _mixer_bwd(dxo, saved, mod, g_pre, g_post, w, ssm, tag):
    (x, h, u, u_i, qkv, f, gab, s_re, s_im, y_ssm, gl, z, fb, cum_cols, cum_rows, attn, attn32, lse, yb,
     merged, y) = saved
    L = x.shape[0]
    sc, gate = _row(mod[1]), _row(mod[2])
    lam_re, lam_im, bin_re, bin_im, cout_re, cout_im, dskip = ssm
    dy, dgate, dgpost = _postnorm_bwd(dxo, y, _row(g_post), gate, 1.0, name=f"postnorm_bwd_{tag}")
    dmerged = _mm(dy, w["out"], tb=True, name=f"dmerged_{tag}")
    dw_out = _mm(merged, dy, ta=True, name=f"dw_mix_out_{tag}", tm=1024, tn=1024)
    dz, dyb, dgab = _merge_bwd(dmerged, z, yb, gab, name=f"merge_bwd_{tag}")
    dgl = _mm(dz, w["glu"], tb=True, name=f"dgl_{tag}", tk=2048)
    dw_glu = _mm(gl, dz, ta=True, name=f"dw_glu_{tag}", tm=512, tn=2048)
    dys, dsk, dd = _gelu_bwd(dgl, y_ssm, u, dskip, name=f"gelu_bwd_{tag}")
    dys, dsk = _interleave(dys), _interleave(dsk)
    d_re = _mm(dys, cout_re, tb=True, name=f"ssm_ds_re_{tag}")
    d_im = _mm(dys, cout_im, tb=True, name=f"ssm_ds_im_{tag}")
    dcout_re = _mm(s_re, dys, ta=True, name=f"ssm_dc_re_{tag}", tm=1024, tn=512)
    dcout_im = _mm(s_im, dys, ta=True, name=f"ssm_dc_im_{tag}", tm=1024, tn=512)
    g_re, g_im, dlam_re, dlam_im = _ssm_scan_bwd(d_re, d_im, s_re, s_im, lam_re, lam_im, name=f"ssm_scan_bwd_{tag}")
    du = _mm(g_re, bin_re, tb=True, bias=dsk, name=f"ssm_du_re_{tag}", tk=2048)
    du = _mm(g_im, bin_im, tb=True, bias=du, out_dtype=BF16, name=f"ssm_du_im_{tag}", tk=2048)
    du = _deinterleave(du)
    dbin_re = _mm(u_i, g_re, ta=True, name=f"ssm_db_re_{tag}", tm=512, tn=1024)
    dbin_im = _mm(u_i, g_im, ta=True, name=f"ssm_db_im_{tag}", tm=512, tn=1024)
    dssm = (dlam_re, dlam_im, dbin_re, dbin_im, dcout_re, dcout_im, _row(dd[0]))
    dattn = _mm(dyb, w["attn_out"], tb=True, out_dtype=BF16, name=f"dattn_{tag}")
    dw_attn = _mm(attn, dyb, ta=True, name=f"dw_attn_out_{tag}", tm=512, tn=1024)
    stats = _fox_rowstats(dattn, attn32, lse, cum_cols, name=f"fox_rowstats_{tag}")
    dk, dv, dcum_rows, dq, drow = _fox_bwd(qkv, dattn, stats, cum_rows, name=f"fox_bwd_{tag}")
    drow8 = drow.reshape(L, ATTN_HEADS // 2, LANES)[:, :, :2].reshape(L, ATTN_HEADS)
    dcum = drow8 + dcum_rows[:, :2, :].reshape(ATTN_HEADS, L).T
    dcum = jnp.pad(dcum, ((0, 0), (0, LANES - ATTN_HEADS)))
    df, dfb = _fox_cum_bwd(dcum, f, fb, name=f"fox_cum_bwd_{tag}")
    dqkv = jnp.concatenate([dq.astype(BF16), dk, dv], axis=1)
    dh = _mm(dqkv, w["qkv"], tb=True, name=f"dh_qkv_{tag}", tk=1536)
    dh = _mm(du, w["u"], tb=True, bias=dh, name=f"dh_u_{tag}")
    dh = _mm(dgab, w["gab"], tb=True, bias=dh, name=f"dh_gab_{tag}", tk=2048)
    dh = _mm(df, w["f"], tb=True, bias=dh, name=f"dh_f_{tag}")
    dw_u = _mm(h, du, ta=True, name=f"dw_u_{tag}", tm=1024, tn=512)
    dw_qkv = _mm(h, dqkv, ta=True, name=f"dw_qkv_{tag}", tm=1024, tn=1536)
    dw_f = _mm(h, df, ta=True, name=f"dw_f_{tag}", tm=1024)
    dw_gab = _mm(h, dgab, ta=True, name=f"dw_gab_{tag}", tm=1024, tn=1024)
    dw_in = jnp.concatenate([dw_u, dw_qkv, dw_f[:, :ATTN_HEADS], dw_gab], axis=1)
    dx, dsh, dsc, dgpre = _prenorm_bwd(x, dh, dxo, _row(g_pre), sc, name=f"prenorm_bwd_{tag}")
    dmod = jnp.stack([dsh[0], dsc[0], dgate[0]])
    grads = dict(mix_w_in=dw_in, glu_w=dw_glu, attn_w_out=dw_attn, mix_w_out=dw_out,
                 forget_b=dfb[0, :ATTN_HEADS])
    return dx, dmod, dgpre[0], dgpost[0], grads, dssm


SSM_NAMES = ("ssm_a_re", "ssm_a_im", "ssm_log_dt", "ssm_b_re", "ssm_b_im", "ssm_c_re", "ssm_c_im", "ssm_d")
SMALL_NAMES = ("forget_b",) + SSM_NAMES
WEIGHT_NAMES = ("mod_w", "mod_b", "norm_pre", "norm_post", "ffn_w_in", "ffn_w_out", "mix_w_in", "forget_b") \
    + SSM_NAMES + ("glu_w", "attn_w_out", "mix_w_out")


def _layer_shards(w, l):
    return [(w[n][l] if j is None else w[n][l, j]).astype(BF16).reshape(_part_shape(CUTS[n], slab=True))
            for n, j in LAYER_MATS]


def _train_step(x, c, target, w, m, v):
    xi, yi, ci = _my_place()
    chip = 2 * xi + yi
    dev = 4 * xi + 2 * yi + ci
    mod_cols = N_SUB * 3 * D_MODEL // N_CHIPS
    norm_cols = D_MODEL // N_CHIPS

    cuts = [CUTS[n] for n, _ in LAYER_MATS]
    full = [_gather_layer(_layer_shards(w, l), cuts, name=f"gather_weights_l{l}") for l in range(DEPTH)]
    c_all = _all_gather8(jnp.pad(c, ((0, SUBLANES - 1), (0, 0))), name="gather_c")[::SUBLANES]
    mod_b_cols = lax.dynamic_slice_in_dim(w["mod_b"], chip * mod_cols, mod_cols, axis=1)[:, None, :]
    mod_part = _mod_partial(c_all, w["mod_w"], mod_b_cols, name="mod_partial")
    small_fwd = jnp.concatenate([mod_part.reshape(-1), w["norm_pre"].reshape(-1), w["norm_post"].reshape(-1)])
    n_mod, n_norm = mod_part.size, w["norm_pre"].size
    sf_all = _all_gather8(_pad_rows(small_fwd), name="gather_mod").reshape(N_DEV, -1)
    sf_chips = sf_all[::2]
    mod_all = jnp.concatenate(
        [sf_chips[k, :n_mod].reshape(DEPTH, N_DEV, mod_cols) for k in range(N_CHIPS)], axis=2)
    mod_mine = lax.dynamic_index_in_dim(mod_all, dev, axis=1, keepdims=False).reshape(DEPTH, N_SUB, 3, D_MODEL)
    norm_pre = jnp.concatenate(
        [sf_chips[k, n_mod:n_mod + n_norm].reshape(DEPTH, N_SUB, norm_cols) for k in range(N_CHIPS)], axis=2)
    norm_post = jnp.concatenate(
        [sf_chips[k, n_mod + n_norm:n_mod + 2 * n_norm].reshape(DEPTH, N_SUB, norm_cols) for k in range(N_CHIPS)],
        axis=2)

    saved, layer_w, ssm_prep, ssm_vjp = [], [], [], []
    h = x
    for l in range(DEPTH):
        win0, win1, wout0, wout1, mix_in, glu, attn_out, mix_out = full[l]
        w_u, w_qkv, w_f, w_gab = _split_mix_w_in(mix_in.transpose(1, 0, 2).reshape(D_MODEL, IN_WIDTH))
        lw = dict(
            ffn=[(win0[0], wout0[0]), (win1[0], wout1[0])],
            mix=dict(u=w_u, qkv=w_qkv, f=w_f, gab=w_gab, glu=glu[0], attn_out=attn_out[0], out=mix_out[0]))
        prep, vjp = jax.vjp(_ssm_discretize, *[w[n][l] for n in SSM_NAMES])
        layer_w.append(lw)
        ssm_prep.append(prep)
        ssm_vjp.append(vjp)
        h, s0 = _ffn_fwd(h, mod_mine[l, 0], norm_pre[l, 0], norm_post[l, 0], *lw["ffn"][0], tag=f"l{l}a")
        h, s1 = _mixer_fwd(h, mod_mine[l, 1], norm_pre[l, 1], norm_post[l, 1], lw["mix"], prep,
                           w["forget_b"][l], tag=f"l{l}m")
        h, s2 = _ffn_fwd(h, mod_mine[l, 2], norm_pre[l, 2], norm_post[l, 2], *lw["ffn"][1], tag=f"l{l}b")
        saved.append((s0, s1, s2))
    dh, loss8 = _loss_head(h, target, name="loss_head")

    c_idx = ci.reshape(1).astype(jnp.int32)
    chip_idx = chip.reshape(1).astype(jnp.int32)
    reduced, places = [], []
    g_small = {n: [None] * DEPTH for n in SMALL_NAMES}
    dmod, dnpre, dnpost = [], [], []
    for l in reversed(range(DEPTH)):
        lw = layer_w[l]
        dh, dm2, dp2, dq2, dwin2, dwout2 = _ffn_bwd(dh, saved[l][2], mod_mine[l, 2], norm_pre[l, 2],
                                                    norm_post[l, 2], *lw["ffn"][1], tag=f"l{l}b")
        dh, dm1, dp1, dq1, gmix, dssm = _mixer_bwd(dh, saved[l][1], mod_mine[l, 1], norm_pre[l, 1],
                                                   norm_post[l, 1], lw["mix"], ssm_prep[l], tag=f"l{l}m")
        dh, dm0, dp0, dq0, dwin0, dwout0 = _ffn_bwd(dh, saved[l][0], mod_mine[l, 0], norm_pre[l, 0],
                                                    norm_post[l, 0], *lw["ffn"][0], tag=f"l{l}a")
        dmod.insert(0, jnp.stack([dm0, dm1, dm2]))
        dnpre.insert(0, jnp.stack([dp0, dp1, dp2]))
        dnpost.insert(0, jnp.stack([dq0, dq1, dq2]))
        g_small["forget_b"][l] = gmix["forget_b"]
        for n, g in zip(SSM_NAMES, ssm_vjp[l](dssm)):
            g_small[n][l] = g
        dmix_in = gmix["mix_w_in"].reshape(D_MODEL, N_CHIPS, IN_WIDTH // N_CHIPS).transpose(1, 0, 2)
        mats = [dwin0, dwin2, dwout0, dwout2, dmix_in, gmix["glu_w"], gmix["attn_w_out"], gmix["mix_w_out"]]
        mats = [a.reshape(cut.shape) for a, cut in zip(mats, cuts)]
        from_sibling = _swap_layer(mats, cuts, name=f"grad_swap_l{l}")
        sums = [_add_half(mats[i], from_sibling[i], cuts[i], c_idx, name=f"grad_add_l{l}_{i}")
                for i in range(len(mats))]
        parts = _partials_layer([s[1] for s in sums], cuts, name=f"grad_partials_l{l}")
        for i, (n, j) in enumerate(LAYER_MATS):
            reduced.append(_sum_slab(sums[i][0], parts[i], cuts[i], chip_idx, name=f"grad_sum_l{l}_{i}"))
            places.append((n, (l,) if j is None else (l, j)))
    grad_x = dh
    g_small = {n: jnp.stack(g) for n, g in g_small.items()}

    small = [loss8[0, :1], jnp.stack(dmod).reshape(-1), jnp.stack(dnpre).reshape(-1), jnp.stack(dnpost).reshape(-1)]
    small += [g_small[n].reshape(-1) for n in SMALL_NAMES]
    sizes = [int(s.size) for s in small]
    offs = np.concatenate([[0], np.cumsum(sizes)])
    sb_all, sb_sum = _all_gather8(_pad_rows(jnp.concatenate(small)), name="gather_small_grads", with_sum=True)
    sb_sum = sb_sum.reshape(-1)
    take = lambda i: sb_sum[int(offs[i]):int(offs[i + 1])]
    loss = take(0)[0]
    grads = {"mod_b": take(1).reshape(DEPTH, N_SUB * 3 * D_MODEL)}
    dnorm_pre_full = take(2).reshape(DEPTH, N_SUB, D_MODEL)
    dnorm_post_full = take(3).reshape(DEPTH, N_SUB, D_MODEL)
    grads["norm_pre"] = lax.dynamic_slice_in_dim(dnorm_pre_full, chip * norm_cols, norm_cols, axis=2)
    grads["norm_post"] = lax.dynamic_slice_in_dim(dnorm_post_full, chip * norm_cols, norm_cols, axis=2)
    for i, n in enumerate(SMALL_NAMES):
        grads[n] = take(4 + i).reshape(w[n].shape)
    dmod_all = sb_all.reshape(N_DEV, -1)[:, int(offs[1]):int(offs[2])].reshape(N_DEV, DEPTH, N_SUB * 3 * D_MODEL)
    dmod_cols = lax.dynamic_slice_in_dim(dmod_all, chip * mod_cols, mod_cols, axis=2).transpose(1, 0, 2)
    grads["mod_w"] = _mod_wgrad(c_all.T, dmod_cols, name="mod_wgrad")

    share_cuts = [CUTS[n] for n, _ in places]
    out_shapes = {n: ((DEPTH,) if j is None else (DEPTH, 2)) + _part_shape(CUTS[n], slab=True)
                  for n, j in LAYER_MATS}
    shared = _share_all(reduced, share_cuts, places, out_shapes, name="grad_share")
    for n, g in zip(out_shapes, shared):
        grads[n] = g.reshape(w[n].shape)

    delta, new_m, new_v = {}, {}, {}
    for n in WEIGHT_NAMES:
        delta[n], new_m[n], new_v[n] = _adamw(w[n], grads[n], m[n], v[n], name=f"adamw_{n}")
    outs = [loss, grad_x[None]]
    for group in (grads, delta, new_m, new_v):
        outs += [group[n] for n in WEIGHT_NAMES]
    return tuple(outs)


def kernel(x, c, mod_w, mod_b, norm_pre, norm_post, ffn_w_in, ffn_w_out, mix_w_in, forget_b, ssm_a_re, ssm_a_im, ssm_log_dt, ssm_b_re, ssm_b_im, ssm_c_re, ssm_c_im, ssm_d, glu_w, attn_w_out, mix_w_out, loss_target, m_mod_w, m_mod_b, m_norm_pre, m_norm_post, m_ffn_w_in, m_ffn_w_out, m_mix_w_in, m_forget_b, m_ssm_a_re, m_ssm_a_im, m_ssm_log_dt, m_ssm_b_re, m_ssm_b_im, m_ssm_c_re, m_ssm_c_im, m_ssm_d, m_glu_w, m_attn_w_out, m_mix_w_out, v_mod_w, v_mod_b, v_norm_pre, v_norm_post, v_ffn_w_in, v_ffn_w_out, v_mix_w_in, v_forget_b, v_ssm_a_re, v_ssm_a_im, v_ssm_log_dt, v_ssm_b_re, v_ssm_b_im, v_ssm_c_re, v_ssm_c_im, v_ssm_d, v_glu_w, v_attn_w_out, v_mix_w_out):
    w = dict(mod_w=mod_w, mod_b=mod_b, norm_pre=norm_pre, norm_post=norm_post, ffn_w_in=ffn_w_in,
             ffn_w_out=ffn_w_out, mix_w_in=mix_w_in, forget_b=forget_b, ssm_a_re=ssm_a_re, ssm_a_im=ssm_a_im,
             ssm_log_dt=ssm_log_dt, ssm_b_re=ssm_b_re, ssm_b_im=ssm_b_im, ssm_c_re=ssm_c_re, ssm_c_im=ssm_c_im,
             ssm_d=ssm_d, glu_w=glu_w, attn_w_out=attn_w_out, mix_w_out=mix_w_out)
    m = dict(mod_w=m_mod_w, mod_b=m_mod_b, norm_pre=m_norm_pre, norm_post=m_norm_post, ffn_w_in=m_ffn_w_in,
             ffn_w_out=m_ffn_w_out, mix_w_in=m_mix_w_in, forget_b=m_forget_b, ssm_a_re=m_ssm_a_re,
             ssm_a_im=m_ssm_a_im, ssm_log_dt=m_ssm_log_dt, ssm_b_re=m_ssm_b_re, ssm_b_im=m_ssm_b_im,
             ssm_c_re=m_ssm_c_re, ssm_c_im=m_ssm_c_im, ssm_d=m_ssm_d, glu_w=m_glu_w, attn_w_out=m_attn_w_out,
             mix_w_out=m_mix_w_out)
    v = dict(mod_w=v_mod_w, mod_b=v_mod_b, norm_pre=v_norm_pre, norm_post=v_norm_post, ffn_w_in=v_ffn_w_in,
             ffn_w_out=v_ffn_w_out, mix_w_in=v_mix_w_in, forget_b=v_forget_b, ssm_a_re=v_ssm_a_re,
             ssm_a_im=v_ssm_a_im, ssm_log_dt=v_ssm_log_dt, ssm_b_re=v_ssm_b_re, ssm_b_im=v_ssm_b_im,
             ssm_c_re=v_ssm_c_re, ssm_c_im=v_ssm_c_im, ssm_d=v_ssm_d, glu_w=v_glu_w, attn_w_out=v_attn_w_out,
             mix_w_out=v_mix_w_out)
    return _train_step(x[0], c, loss_target[0], w, m, v)
```

```python
import math
from typing import NamedTuple

import jax
import jax.numpy as jnp
import numpy as np
from jax import lax
from jax.experimental import pallas as pl
from jax.experimental.pallas import tpu as pltpu

F32 = jnp.float32
BF16 = jnp.bfloat16

D_MODEL = 1024
DEPTH = 2
SSM_WIDTH = 512
SSM_GROUP = 16
SSM_GROUPS = 32
SSM_STATE = 64
SSM_FLAT = SSM_GROUPS * SSM_STATE
ATTN_HEADS = 8
HEAD_DIM = 64
ATTN_WIDTH = 512
D_FF = 2816
FFN_RES = 0.5
N_SUB = 3
RMS_EPS = 1e-6
N_CHIPS = 4
N_DEV = 8

ADAM_LR = 0.001
ADAM_B1 = 0.9
ADAM_B2 = 0.999
ADAM_EPS = 1e-08
ADAM_WD = 0.01
ADAM_STEP = 10

LANES = 128
SUBLANES = 8
VMEM_LIMIT = 52 * 1024 * 1024
MESH = pl.DeviceIdType.MESH

NN = (((1,), (0,)), ((), ()))
NT = (((1,), (1,)), ((), ()))
TN = (((0,), (0,)), ((), ()))


def _tile(dim, target, align=LANES):
    best = None
    t = align
    while t <= min(dim, target):
        if dim % t == 0:
            best = t
        t += align
    return dim if best is None else best


def _params(*sem):
    return pltpu.CompilerParams(dimension_semantics=sem, vmem_limit_bytes=VMEM_LIMIT)


def _mm(a, b, *, name, ta=False, tb=False, out_dtype=F32, bias=None, bscale=None, b_k0=0,
        tm=512, tn=1024, tk=1024):
    M, K = (a.shape[1], a.shape[0]) if ta else a.shape
    N = b.shape[0] if tb else b.shape[1]
    assert b_k0 + K <= (b.shape[1] if tb else b.shape[0]), (a.shape, b.shape, ta, tb)
    tm, tn, tk = _tile(M, tm), _tile(N, tn), _tile(K, tk)
    nk = K // tk
    assert b_k0 % tk == 0
    kb0 = b_k0 // tk
    dn = (((0 if ta else 1,), (1 if tb else 0,)), ((), ()))
    has_bias, has_scale = bias is not None, bscale is not None

    def body(*refs):
        a_ref, b_ref = refs[0], refs[1]
        pos = 2
        bias_ref = scale_ref = None
        if has_bias:
            bias_ref = refs[pos]
            pos += 1
        if has_scale:
            scale_ref = refs[pos]
            pos += 1
        o_ref = refs[pos]
        acc_ref = refs[pos + 1] if nk > 1 else None

        def finish(r):
            if has_bias:
                extra = bias_ref[...].astype(F32)
                if has_scale:
                    extra = extra * scale_ref[...]
                r = r + extra
            o_ref[...] = r.astype(out_dtype)

        part = lax.dot_general(a_ref[...].astype(BF16), b_ref[...].astype(BF16), dn,
                               preferred_element_type=F32)
        if nk == 1:
            finish(part)
        else:
            k = pl.program_id(2)

            @pl.when(k == 0)
            def _():
                acc_ref[...] = part

            @pl.when(k > 0)
            def _():
                acc_ref[...] += part

            @pl.when(k == nk - 1)
            def _():
                finish(acc_ref[...])

    a_spec = pl.BlockSpec((tk, tm), lambda j, i, k: (k, i)) if ta else pl.BlockSpec((tm, tk), lambda j, i, k: (i, k))
    b_spec = (pl.BlockSpec((tn, tk), lambda j, i, k: (j, kb0 + k)) if tb
              else pl.BlockSpec((tk, tn), lambda j, i, k: (kb0 + k, j)))
    in_specs = [a_spec, b_spec]
    args = [a, b]
    if has_bias:
        in_specs.append(pl.BlockSpec((tm, tn), lambda j, i, k: (i, j)))
        args.append(bias)
    if has_scale:
        in_specs.append(pl.BlockSpec((1, tn), lambda j, i, k: (0, j)))
        args.append(bscale)
    return pl.pallas_call(
        body, name=name,
        grid=(N // tn, M // tm, nk),
        in_specs=in_specs,
        out_specs=pl.BlockSpec((tm, tn), lambda j, i, k: (i, j)),
        out_shape=jax.ShapeDtypeStruct((M, N), out_dtype),
        scratch_shapes=[pltpu.VMEM((tm, tn), F32)] if nk > 1 else [],
        compiler_params=_params("parallel", "parallel", "arbitrary"),
    )(*args)


def _sigmoid(x):
    return 1.0 / (1.0 + jnp.exp(-x))


def _mm_swiglu(h, w_in, *, name, tm=512, tn=1408):
    M, K = h.shape
    N = w_in.shape[1] // 2
    tm, tn = _tile(M, tm), _tile(N, tn)
    nj = N // tn

    def body(h_ref, wg_ref, wu_ref, g_ref, u_ref, a_ref):
        hv = h_ref[...]
        g = jnp.dot(hv, wg_ref[...], preferred_element_type=F32)
        u = jnp.dot(hv, wu_ref[...], preferred_element_type=F32)
        g_ref[...] = g.astype(BF16)
        u_ref[...] = u.astype(BF16)
        a_ref[...] = (g * _sigmoid(g) * u).astype(BF16)

    o_spec = pl.BlockSpec((tm, tn), lambda j, i: (i, j))
    sds = jax.ShapeDtypeStruct((M, N), BF16)
    return pl.pallas_call(
        body, name=name, grid=(nj, M // tm),
        in_specs=[pl.BlockSpec((tm, K), lambda j, i: (i, 0)), pl.BlockSpec((K, tn), lambda j, i: (0, j)),
                  pl.BlockSpec((K, tn), lambda j, i: (0, nj + j))],
        out_specs=[o_spec, o_spec, o_spec], out_shape=[sds, sds, sds],
        compiler_params=_params("parallel", "parallel"),
    )(h, w_in, w_in)


def _mm_swiglu_bwd(dy, w_out, gate, up, *, name, tm=512, tn=1408):
    M, K = dy.shape
    N = w_out.shape[0]
    tm, tn = _tile(M, tm), _tile(N, tn)

    def body(dy_ref, w_ref, g_ref, u_ref, dg_ref, du_ref):
        dact = lax.dot_general(dy_ref[...], w_ref[...], NT, preferred_element_type=F32)
        g = g_ref[...].astype(F32)
        u = u_ref[...].astype(F32)
        sig = _sigmoid(g)
        dg_ref[...] = (dact * u * (sig * (1.0 + g * (1.0 - sig)))).astype(BF16)
        du_ref[...] = (dact * (g * sig)).astype(BF16)

    t_spec = pl.BlockSpec((tm, tn), lambda j, i: (i, j))
    sds = jax.ShapeDtypeStruct((M, N), BF16)
    return pl.pallas_call(
        body, name=name, grid=(N // tn, M // tm),
        in_specs=[pl.BlockSpec((tm, K), lambda j, i: (i, 0)), pl.BlockSpec((tn, K), lambda j, i: (j, 0)),
                  t_spec, t_spec],
        out_specs=[t_spec, t_spec], out_shape=[sds, sds],
        compiler_params=_params("parallel", "parallel"),
    )(dy, w_out, gate, up)


ROW_TILE = 256


def _colsum8(v):
    return jnp.sum(v.reshape(v.shape[0] // SUBLANES, SUBLANES, v.shape[1]), axis=0)


def _finish_colsums(step, last, refs):
    @pl.when(step == last)
    def _():
        for r in refs:
            r[...] = jnp.broadcast_to(jnp.sum(r[...], axis=0, keepdims=True), r.shape)


def _row_spec(t, d):
    return pl.BlockSpec((t, d), lambda i: (i, 0))


def _vec_spec(d, rows=1):
    return pl.BlockSpec((rows, d), lambda i: (0, 0))


def _prenorm(x, g, sc, sh, *, name):
    L, D = x.shape
    t = _tile(L, ROW_TILE, SUBLANES)

    def body(x_ref, g_ref, sc_ref, sh_ref, h_ref):
        xv = x_ref[...]
        r = lax.rsqrt(jnp.mean(xv * xv, axis=-1, keepdims=True) + RMS_EPS)
        h_ref[...] = (((xv * r) * g_ref[...]) * (1.0 + sc_ref[...]) + sh_ref[...]).astype(BF16)

    return pl.pallas_call(
        body, name=name, grid=(L // t,),
        in_specs=[_row_spec(t, D), _vec_spec(D), _vec_spec(D), _vec_spec(D)],
        out_specs=_row_spec(t, D), out_shape=jax.ShapeDtypeStruct((L, D), BF16),
        compiler_params=_params("parallel"),
    )(x, g, sc, sh)


def _postnorm_res(x, y, g, gate, res_w, *, name):
    L, D = x.shape
    t = _tile(L, ROW_TILE, SUBLANES)

    def body(x_ref, y_ref, g_ref, gate_ref, o_ref):
        yv = y_ref[...]
        r = lax.rsqrt(jnp.mean(yv * yv, axis=-1, keepdims=True) + RMS_EPS)
        o_ref[...] = x_ref[...] + (res_w * gate_ref[...]) * ((yv * r) * g_ref[...])

    return pl.pallas_call(
        body, name=name, grid=(L // t,),
        in_specs=[_row_spec(t, D), _row_spec(t, D), _vec_spec(D), _vec_spec(D)],
        out_specs=_row_spec(t, D), out_shape=jax.ShapeDtypeStruct((L, D), F32),
        compiler_params=_params("parallel"),
    )(x, y, g, gate)


def _postnorm_bwd(dxo, y, g, gate, res_w, *, name):
    L, D = y.shape
    t = _tile(L, ROW_TILE, SUBLANES)
    n = L // t

    def body(dxo_ref, y_ref, g_ref, gate_ref, dy_ref, dgate_ref, dg_ref):
        i = pl.program_id(0)

        @pl.when(i == 0)
        def _():
            dgate_ref[...] = jnp.zeros_like(dgate_ref)
            dg_ref[...] = jnp.zeros_like(dg_ref)

        yv = y_ref[...]
        dv = dxo_ref[...]
        gv = g_ref[...]
        r = lax.rsqrt(jnp.mean(yv * yv, axis=-1, keepdims=True) + RMS_EPS)
        yn = yv * r
        dgate_ref[...] += _colsum8(dv * (res_w * (yn * gv)))
        do = dv * (res_w * gate_ref[...])
        dg_ref[...] += _colsum8(do * yn)
        dyn = do * gv
        dy_ref[...] = (r * (dyn - yn * jnp.mean(dyn * yn, axis=-1, keepdims=True))).astype(BF16)
        _finish_colsums(i, n - 1, (dgate_ref, dg_ref))

    sum_sds = jax.ShapeDtypeStruct((SUBLANES, D), F32)
    return pl.pallas_call(
        body, name=name, grid=(n,),
        in_specs=[_row_spec(t, D), _row_spec(t, D), _vec_spec(D), _vec_spec(D)],
        out_specs=[_row_spec(t, D), _vec_spec(D, SUBLANES), _vec_spec(D, SUBLANES)],
        out_shape=[jax.ShapeDtypeStruct((L, D), BF16), sum_sds, sum_sds],
        compiler_params=_params("arbitrary"),
    )(dxo, y, g, gate)


def _prenorm_bwd(x, dh, dxres, g, sc, *, name):
    L, D = x.shape
    t = _tile(L, ROW_TILE, SUBLANES)
    n = L // t

    def body(x_ref, dh_ref, dxr_ref, g_ref, sc_ref, dx_ref, dsh_ref, dsc_ref, dg_ref):
        i = pl.program_id(0)

        @pl.when(i == 0)
        def _():
            dsh_ref[...] = jnp.zeros_like(dsh_ref)
            dsc_ref[...] = jnp.zeros_like(dsc_ref)
            dg_ref[...] = jnp.zeros_like(dg_ref)

        xv = x_ref[...]
        dhv = dh_ref[...].astype(F32)
        gv = g_ref[...]
        one_sc = 1.0 + sc_ref[...]
        r = lax.rsqrt(jnp.mean(xv * xv, axis=-1, keepdims=True) + RMS_EPS)
        xn = xv * r
        tt = dhv * xn
        dsh_ref[...] += _colsum8(dhv)
        dsc_ref[...] += _colsum8(tt * gv)
        dg_ref[...] += _colsum8(tt * one_sc)
        dxn = dhv * (gv * one_sc)
        dx_ref[...] = dxr_ref[...] + r * (dxn - xn * jnp.mean(dxn * xn, axis=-1, keepdims=True))
        _finish_colsums(i, n - 1, (dsh_ref, dsc_ref, dg_ref))

    sum_sds = jax.ShapeDtypeStruct((SUBLANES, D), F32)
    sum_spec = _vec_spec(D, SUBLANES)
    return pl.pallas_call(
        body, name=name, grid=(n,),
        in_specs=[_row_spec(t, D), _row_spec(t, D), _row_spec(t, D), _vec_spec(D), _vec_spec(D)],
        out_specs=[_row_spec(t, D), sum_spec, sum_spec, sum_spec],
        out_shape=[jax.ShapeDtypeStruct((L, D), F32), sum_sds, sum_sds, sum_sds],
        compiler_params=_params("arbitrary"),
    )(x, dh, dxres, g, sc)


def _loss_head(y, target, *, name):
    L, D = y.shape
    t = _tile(L, ROW_TILE, SUBLANES)
    n = L // t

    def body(y_ref, t_ref, dy_ref, loss_ref):
        i = pl.program_id(0)

        @pl.when(i == 0)
        def _():
            loss_ref[...] = jnp.zeros_like(loss_ref)

        e = y_ref[...] - t_ref[...]
        dy_ref[...] = e * (1.0 / D)
        part = jnp.sum(jnp.mean(e * e, axis=-1, keepdims=True), axis=0, keepdims=True)
        loss_ref[...] += jnp.broadcast_to(0.5 * part, loss_ref.shape)

    return pl.pallas_call(
        body, name=name, grid=(n,),
        in_specs=[_row_spec(t, D), _row_spec(t, D)],
        out_specs=[_row_spec(t, D), pl.BlockSpec((SUBLANES, LANES), lambda i: (0, 0))],
        out_shape=[jax.ShapeDtypeStruct((L, D), F32), jax.ShapeDtypeStruct((SUBLANES, LANES), F32)],
        compiler_params=_params("arbitrary"),
    )(y, target)


GELU_C = math.sqrt(2.0 / math.pi)


def _gelu_fwd(y, *, name):
    L, W = y.shape
    t = _tile(L, 512, SUBLANES)

    def body(y_ref, o_ref):
        v = y_ref[...]
        o_ref[...] = (0.5 * v * (1.0 + jnp.tanh(GELU_C * (v + 0.044715 * (v * v * v))))).astype(BF16)

    return pl.pallas_call(
        body, name=name, grid=(L // t,), in_specs=[_row_spec(t, W)], out_specs=_row_spec(t, W),
        out_shape=jax.ShapeDtypeStruct((L, W), BF16), compiler_params=_params("parallel"),
    )(y)


def _gelu_bwd(dgl, y, u, dskip, *, name):
    L, W = y.shape
    t = _tile(L, 512, SUBLANES)
    n = L // t

    def body(dgl_ref, y_ref, u_ref, d_ref, dy_ref, sk_ref, dd_ref):
        i = pl.program_id(0)

        @pl.when(i == 0)
        def _():
            dd_ref[...] = jnp.zeros_like(dd_ref)

        v = y_ref[...]
        inner = GELU_C * (v + 0.044715 * (v * v * v))
        th = jnp.tanh(inner)
        dgelu = 0.5 * (1.0 + th) + 0.5 * v * (1.0 - th * th) * (GELU_C * (1.0 + 3.0 * 0.044715 * (v * v)))
        dy = dgl_ref[...] * dgelu
        dy_ref[...] = dy.astype(BF16)
        sk_ref[...] = dy * d_ref[...]
        dd_ref[...] += _colsum8(dy * u_ref[...])
        _finish_colsums(i, n - 1, (dd_ref,))

    return pl.pallas_call(
        body, name=name, grid=(n,),
        in_specs=[_row_spec(t, W), _row_spec(t, W), _row_spec(t, W), _vec_spec(W)],
        out_specs=[_row_spec(t, W), _row_spec(t, W), _vec_spec(W, SUBLANES)],
        out_shape=[jax.ShapeDtypeStruct((L, W), BF16), jax.ShapeDtypeStruct((L, W), F32),
                   jax.ShapeDtypeStruct((SUBLANES, W), F32)],
        compiler_params=_params("arbitrary"),
    )(dgl, y, u, dskip)


def _merge_fwd(z, yb, gab, *, name):
    L, D = yb.shape
    t = _tile(L, ROW_TILE, SUBLANES)

    def body(z_ref, yb_ref, gab_ref, o_ref):
        ya = z_ref[:, :D] * _sigmoid(z_ref[:, D:])
        o_ref[...] = (_sigmoid(gab_ref[:, :D]) * ya + _sigmoid(gab_ref[:, D:]) * yb_ref[...]).astype(BF16)

    return pl.pallas_call(
        body, name=name, grid=(L // t,),
        in_specs=[_row_spec(t, 2 * D), _row_spec(t, D), _row_spec(t, 2 * D)],
        out_specs=_row_spec(t, D), out_shape=jax.ShapeDtypeStruct((L, D), BF16),
        compiler_params=_params("parallel"),
    )(z, yb, gab)


def _merge_bwd(dm, z, yb, gab, *, name):
    L, D = yb.shape
    t = _tile(L, ROW_TILE, SUBLANES)

    def body(dm_ref, z_ref, yb_ref, gab_ref, dz_ref, dyb_ref, dgab_ref):
        dmv = dm_ref[...]
        zv = z_ref[:, :D]
        sz = _sigmoid(z_ref[:, D:])
        sa = _sigmoid(gab_ref[:, :D])
        sb = _sigmoid(gab_ref[:, D:])
        ybv = yb_ref[...]
        dya = dmv * sa
        dz_ref[:, :D] = (dya * sz).astype(BF16)
        dz_ref[:, D:] = (dya * zv * (sz * (1.0 - sz))).astype(BF16)
        dyb_ref[...] = (dmv * sb).astype(BF16)
        dgab_ref[:, :D] = (dmv * (zv * sz) * (sa * (1.0 - sa))).astype(BF16)
        dgab_ref[:, D:] = (dmv * ybv * (sb * (1.0 - sb))).astype(BF16)

    return pl.pallas_call(
        body, name=name, grid=(L // t,),
        in_specs=[_row_spec(t, D), _row_spec(t, 2 * D), _row_spec(t, D), _row_spec(t, 2 * D)],
        out_specs=[_row_spec(t, 2 * D), _row_spec(t, D), _row_spec(t, 2 * D)],
        out_shape=[jax.ShapeDtypeStruct((L, 2 * D), BF16), jax.ShapeDtypeStruct((L, D), BF16),
                   jax.ShapeDtypeStruct((L, 2 * D), BF16)],
        compiler_params=_params("parallel"),
    )(dm, z, yb, gab)


SCAN_W = 1024
SCAN_T = 512


def _interleave(x):
    L, W = x.shape
    seg = SCAN_T // SUBLANES
    return x.reshape(L // SCAN_T, SUBLANES, seg, W).transpose(0, 2, 1, 3).reshape(L, W)


def _deinterleave(x):
    L, W = x.shape
    seg = SCAN_T // SUBLANES
    return x.reshape(L // SCAN_T, seg, SUBLANES, W).transpose(0, 2, 1, 3).reshape(L, W)


def _power_table(a, b, pr_tab, pi_tab, n):
    def fill(k, carry):
        pr, pi = carry
        pr_tab[k] = pr
        pi_tab[k] = pi
        return a * pr - b * pi, a * pi + b * pr

    lax.fori_loop(0, n, fill, (a, b))


def _rows_to_tile(rows):
    w = rows[0].shape[1]
    sub = lax.broadcasted_iota(jnp.int32, (SUBLANES, w), 0)
    tile = jnp.broadcast_to(rows[0], (SUBLANES, w))
    for j in range(1, SUBLANES):
        tile = jnp.where(sub == j, jnp.broadcast_to(rows[j], (SUBLANES, w)), tile)
    return tile


def _ssm_scan_fwd(bu_re, bu_im, lam_re, lam_im, *, name):
    L, S = bu_re.shape
    w, t = _tile(S, SCAN_W), SCAN_T
    seg = t // SUBLANES

    def body(br_ref, bi_ref, lr_ref, li_ref, sr_ref, si_ref, pr_tab, pi_tab, cr_ref, ci_ref):
        a = jnp.broadcast_to(lr_ref[...], (SUBLANES, w))
        b = jnp.broadcast_to(li_ref[...], (SUBLANES, w))

        @pl.when(pl.program_id(1) == 0)
        def _():
            cr_ref[...] = jnp.zeros_like(cr_ref)
            ci_ref[...] = jnp.zeros_like(ci_ref)
            _power_table(a, b, pr_tab, pi_tab, seg)

        def local_scan(i, carry):
            sr, si = carry
            base = pl.multiple_of(i * SUBLANES, SUBLANES)
            nr = a * sr - b * si + br_ref[pl.ds(base, SUBLANES), :]
            ni = a * si + b * sr + bi_ref[pl.ds(base, SUBLANES), :]
            sr_ref[pl.ds(base, SUBLANES), :] = nr
            si_ref[pl.ds(base, SUBLANES), :] = ni
            return nr, ni

        zero = jnp.zeros((SUBLANES, w), F32)
        fr, fi = lax.fori_loop(0, seg, local_scan, (zero, zero), unroll=2)
        lsr, lsi = pr_tab[seg - 1][0:1, :], pi_tab[seg - 1][0:1, :]
        cr, ci = cr_ref[...], ci_ref[...]
        rows_r, rows_i = [], []
        for j in range(SUBLANES):
            rows_r.append(cr)
            rows_i.append(ci)
            cr, ci = fr[j:j + 1, :] + (lsr * cr - lsi * ci), fi[j:j + 1, :] + (lsr * ci + lsi * cr)
        cr_ref[...] = cr
        ci_ref[...] = ci
        in_r, in_i = _rows_to_tile(rows_r), _rows_to_tile(rows_i)

        def add_entry(i, _):
            base = pl.multiple_of(i * SUBLANES, SUBLANES)
            pr, pi = pr_tab[i], pi_tab[i]
            sr_ref[pl.ds(base, SUBLANES), :] += pr * in_r - pi * in_i
            si_ref[pl.ds(base, SUBLANES), :] += pr * in_i + pi * in_r
            return 0

        lax.fori_loop(0, seg, add_entry, 0, unroll=2)

    blk = pl.BlockSpec((t, w), lambda j, i: (i, j))
    vec = pl.BlockSpec((1, w), lambda j, i: (0, j))
    sds = jax.ShapeDtypeStruct((L, S), F32)
    tab = pltpu.VMEM((seg, SUBLANES, w), F32)
    return pl.pallas_call(
        body, name=name, grid=(S // w, L // t),
        in_specs=[blk, blk, vec, vec], out_specs=[blk, blk], out_shape=[sds, sds],
        scratch_shapes=[tab, tab, pltpu.VMEM((1, w), F32), pltpu.VMEM((1, w), F32)],
        compiler_params=_params("parallel", "arbitrary"),
    )(bu_re, bu_im, lam_re, lam_im)


def _ssm_scan_bwd(d_re, d_im, s_re, s_im, lam_re, lam_im, *, name):
    L, S = d_re.shape
    w, t = _tile(S, SCAN_W), SCAN_T
    nt = L // t
    seg = t // SUBLANES

    def body(dr_ref, di_ref, sr_ref, si_ref, lr_ref, li_ref, gr_ref, gi_ref, ar_ref, ai_ref,
             pr_tab, pi_tab, cgr, cgi, acc_r, acc_i):
        step = pl.program_id(1)
        a = jnp.broadcast_to(lr_ref[...], (SUBLANES, w))
        b = jnp.broadcast_to(-li_ref[...], (SUBLANES, w))

        @pl.when(step == 0)
        def _():
            for r in (cgr, cgi, acc_r, acc_i):
                r[...] = jnp.zeros_like(r)
            _power_table(a, b, pr_tab, pi_tab, seg)

        def local_scan(ii, carry):
            gr, gi = carry
            base = pl.multiple_of((seg - 1 - ii) * SUBLANES, SUBLANES)
            ngr = a * gr - b * gi + dr_ref[pl.ds(base, SUBLANES), :]
            ngi = a * gi + b * gr + di_ref[pl.ds(base, SUBLANES), :]
            gr_ref[pl.ds(base, SUBLANES), :] = ngr
            gi_ref[pl.ds(base, SUBLANES), :] = ngi
            return ngr, ngi

        zero = jnp.zeros((SUBLANES, w), F32)
        fr, fi = lax.fori_loop(0, seg, local_scan, (zero, zero), unroll=2)
        lsr, lsi = pr_tab[seg - 1][0:1, :], pi_tab[seg - 1][0:1, :]
        cr, ci = cgr[...], cgi[...]
        rows_r, rows_i = [None] * SUBLANES, [None] * SUBLANES
        for j in reversed(range(SUBLANES)):
            rows_r[j], rows_i[j] = cr, ci
            cr, ci = fr[j:j + 1, :] + (lsr * cr - lsi * ci), fi[j:j + 1, :] + (lsr * ci + lsi * cr)
        cgr[...] = cr
        cgi[...] = ci
        in_r, in_i = _rows_to_tile(rows_r), _rows_to_tile(rows_i)

        def add_entry(ii, carry):
            nr, ni, xr, xi = carry
            base = pl.multiple_of((seg - 1 - ii) * SUBLANES, SUBLANES)
            sr = sr_ref[pl.ds(base, SUBLANES), :]
            si = si_ref[pl.ds(base, SUBLANES), :]
            xr = xr + (nr * sr + ni * si)
            xi = xi + (ni * sr - nr * si)
            pr, pi = pr_tab[ii], pi_tab[ii]
            gr = gr_ref[pl.ds(base, SUBLANES), :] + (pr * in_r - pi * in_i)
            gi = gi_ref[pl.ds(base, SUBLANES), :] + (pr * in_i + pi * in_r)
            gr_ref[pl.ds(base, SUBLANES), :] = gr
            gi_ref[pl.ds(base, SUBLANES), :] = gi
            return gr, gi, xr, xi

        _, _, xr, xi = lax.fori_loop(0, seg, add_entry, (in_r, in_i, acc_r[...], acc_i[...]), unroll=2)
        acc_r[...] = xr
        acc_i[...] = xi

        @pl.when(step == nt - 1)
        def _():
            ar_ref[...] = jnp.sum(xr, axis=0, keepdims=True)
            ai_ref[...] = jnp.sum(xi, axis=0, keepdims=True)

    blk = pl.BlockSpec((t, w), lambda j, i: (nt - 1 - i, j))
    vec = pl.BlockSpec((1, w), lambda j, i: (0, j))
    sds = jax.ShapeDtypeStruct((L, S), F32)
    vsds = jax.ShapeDtypeStruct((1, S), F32)
    tab = pltpu.VMEM((seg, SUBLANES, w), F32)
    tile = pltpu.VMEM((SUBLANES, w), F32)
    return pl.pallas_call(
        body, name=name, grid=(S // w, nt),
        in_specs=[blk, blk, blk, blk, vec, vec], out_specs=[blk, blk, vec, vec],
        out_shape=[sds, sds, vsds, vsds],
        scratch_shapes=[tab, tab, pltpu.VMEM((1, w), F32), pltpu.VMEM((1, w), F32), tile, tile],
        compiler_params=_params("parallel", "arbitrary"),
    )(d_re, d_im, s_re, s_im, lam_re, lam_im)


def _ssm_discretize(a_re, a_im, log_dt, b_re, b_im, c_re, c_im, d_skip):
    G, P, N = SSM_GROUPS, SSM_STATE, SSM_GROUP
    a = jnp.minimum(a_re, -1e-4)
    dt = jnp.exp(log_dt)[:, None]
    mag = jnp.exp(a * dt)
    lr = mag * jnp.cos(a_im * dt)
    li = mag * jnp.sin(a_im * dt)
    den = a * a + a_im * a_im
    cr = ((lr - 1.0) * a + li * a_im) / den
    ci = (li * a - (lr - 1.0) * a_im) / den
    bbr = cr[..., None] * b_re - ci[..., None] * b_im
    bbi = cr[..., None] * b_im + ci[..., None] * b_re
    eye = jnp.eye(G, dtype=F32)

    def in_map(bb):
        return (eye[:, None, :, None] * bb.transpose(0, 2, 1)[:, :, None, :]).reshape(G * N, G * P)

    def out_map(c):
        return (eye[:, None, :, None] * c.transpose(0, 2, 1)[:, :, None, :]).reshape(G * P, G * N)

    return (lr.reshape(1, G * P), li.reshape(1, G * P), in_map(bbr), in_map(bbi),
            out_map(c_re), out_map(-c_im), d_skip.reshape(1, SSM_WIDTH))


ATT_T = 512
CUM_T = 256


def _split3(x):
    hi = x.astype(BF16)
    r1 = x - hi.astype(F32)
    mid = r1.astype(BF16)
    lo = (r1 - mid.astype(F32)).astype(BF16)
    return hi, mid, lo


def _tri_dot(tri, x):
    hi, mid, lo = _split3(x)
    dot = lambda p: jnp.dot(tri, p, preferred_element_type=F32)
    return dot(hi) + dot(mid) + dot(lo)


def _log_sigmoid(x):
    return jnp.minimum(x, 0.0) - jnp.log(1.0 + jnp.exp(-jnp.abs(x)))


def _fox_cum(f, fb, *, name):
    L, W = f.shape
    t = _tile(L, CUM_T, SUBLANES)

    def body(f_ref, b_ref, o_ref, carry):
        @pl.when(pl.program_id(0) == 0)
        def _():
            carry[...] = jnp.zeros_like(carry)

        row = lax.broadcasted_iota(jnp.int32, (t, t), 0)
        col = lax.broadcasted_iota(jnp.int32, (t, t), 1)
        tri = jnp.where(col <= row, 1.0, 0.0).astype(BF16)
        c = _tri_dot(tri, _log_sigmoid(f_ref[...] + b_ref[...])) + carry[...]
        o_ref[...] = c
        carry[...] = c[t - 1:t, :]

    return pl.pallas_call(
        body, name=name, grid=(L // t,),
        in_specs=[_row_spec(t, W), _vec_spec(W)], out_specs=_row_spec(t, W),
        out_shape=jax.ShapeDtypeStruct((L, W), F32),
        scratch_shapes=[pltpu.VMEM((1, W), F32)], compiler_params=_params("arbitrary"),
    )(f, fb)


def _fox_cum_bwd(dcum, f, fb, *, name):
    L, W = f.shape
    t = _tile(L, CUM_T, SUBLANES)
    n = L // t

    def body(d_ref, f_ref, b_ref, o_ref, db_ref, carry):
        i = pl.program_id(0)

        @pl.when(i == 0)
        def _():
            carry[...] = jnp.zeros_like(carry)
            db_ref[...] = jnp.zeros_like(db_ref)

        row = lax.broadcasted_iota(jnp.int32, (t, t), 0)
        col = lax.broadcasted_iota(jnp.int32, (t, t), 1)
        tri = jnp.where(col >= row, 1.0, 0.0).astype(BF16)
        dlog = _tri_dot(tri, d_ref[...]) + carry[...]
        carry[...] = dlog[0:1, :]
        df = dlog * _sigmoid(-(f_ref[...] + b_ref[...]))
        o_ref[...] = df.astype(BF16)
        db_ref[...] += _colsum8(df)
        _finish_colsums(i, n - 1, (db_ref,))

    rev = pl.BlockSpec((t, W), lambda i: (n - 1 - i, 0))
    return pl.pallas_call(
        body, name=name, grid=(n,),
        in_specs=[rev, rev, _vec_spec(W)], out_specs=[rev, _vec_spec(W, SUBLANES)],
        out_shape=[jax.ShapeDtypeStruct((L, W), BF16), jax.ShapeDtypeStruct((SUBLANES, W), F32)],
        scratch_shapes=[pltpu.VMEM((1, W), F32)], compiler_params=_params("arbitrary"),
    )(dcum, f, fb)


def _head_col(blk, h):
    lane = lax.broadcasted_iota(jnp.int32, blk.shape, 1)
    return jnp.sum(jnp.where(lane == h * HEAD_DIM, blk, 0.0), axis=1, keepdims=True)


def _lo_mask(rows):
    return lax.broadcasted_iota(jnp.int32, (rows, LANES), 1) < HEAD_DIM


def _causal(t):
    row = lax.broadcasted_iota(jnp.int32, (t, t), 0)
    col = lax.broadcasted_iota(jnp.int32, (t, t), 1)
    return col <= row


def _fox_fwd(qkv, cum_cols, cum_rows, *, name):
    L = qkv.shape[0]
    t = _tile(L, ATT_T)
    nq = L // t
    npair = ATTN_HEADS // 2

    def body(q_ref, k_ref, v_ref, cc_ref, cr_ref, o_ref, o32_ref, lse_ref):
        iq = pl.program_id(1)
        lo = _lo_mask(t)
        qv = q_ref[...] * 0.125
        zq = jnp.zeros_like(qv)
        qh = (jnp.where(lo, qv, zq), jnp.where(lo, zq, qv))
        ccv = cc_ref[...]
        cq = (_head_col(ccv, 0), _head_col(ccv, 1))

        def step(ik, carry, masked):
            start = pl.multiple_of(ik * t, t)
            kb = k_ref[pl.ds(start, t), :]
            vb = v_ref[pl.ds(start, t), :]
            out = []
            for h in range(2):
                m, l, acc = carry[h]
                s = lax.dot_general(qh[h], kb, NT, preferred_element_type=F32)
                s = s + (cq[h] - cr_ref[h:h + 1, pl.ds(start, t)])
                if masked:
                    s = jnp.where(_causal(t), s, -jnp.inf)
                m_new = jnp.maximum(m, jnp.max(s, axis=1, keepdims=True))
                alpha = jnp.exp(m - m_new)
                p = jnp.exp(s - m_new)
                l = alpha * l + jnp.sum(p, axis=1, keepdims=True)
                acc = alpha * acc + jnp.dot(p.astype(BF16), vb, preferred_element_type=F32)
                out.append((m_new, l, acc))
            return tuple(out)

        init1 = (jnp.full((t, 1), -jnp.inf, F32), jnp.zeros((t, 1), F32), jnp.zeros((t, LANES), F32))
        carry = lax.fori_loop(0, iq, lambda ik, c: step(ik, c, False), (init1, init1))
        (m0, l0, a0), (m1, l1, a1) = step(iq, carry, True)
        out = jnp.where(lo, a0 / l0, a1 / l1)
        o_ref[...] = out.astype(BF16)
        o32_ref[...] = out
        lse_ref[...] = jnp.where(lo, m0 + jnp.log(l0), m1 + jnp.log(l1))

    blk = lambda off: pl.BlockSpec((t, LANES), lambda hp, iq: (iq, off + hp))
    whole = lambda off: pl.BlockSpec((L, LANES), lambda hp, iq: (0, off + hp))
    return pl.pallas_call(
        body, name=name, grid=(npair, nq),
        in_specs=[blk(0), whole(npair), whole(2 * npair), blk(0),
                  pl.BlockSpec((None, SUBLANES, L), lambda hp, iq: (hp, 0, 0))],
        out_specs=[blk(0), blk(0), blk(0)],
        out_shape=[jax.ShapeDtypeStruct((L, ATTN_WIDTH), BF16), jax.ShapeDtypeStruct((L, ATTN_WIDTH), F32),
                   jax.ShapeDtypeStruct((L, ATTN_WIDTH), F32)],
        compiler_params=_params("parallel", "arbitrary"),
    )(qkv, qkv, qkv, cum_cols, cum_rows)


STAT_LSE, STAT_CUM, STAT_DELTA = 0, 2, 4


def _lane_col(blk, idx):
    lane = lax.broadcasted_iota(jnp.int32, blk.shape, 1)
    return jnp.sum(jnp.where(lane == idx, blk, 0.0), axis=1, keepdims=True)


def _fox_rowstats(do, o, lse, cum_cols, *, name):
    L = do.shape[0]
    t = _tile(L, ATT_T)

    def body(do_ref, o_ref, lse_ref, cc_ref, st_ref):
        lo = _lo_mask(t)
        dd = do_ref[...].astype(F32) * o_ref[...]
        lsev, ccv = lse_ref[...], cc_ref[...]
        cols = (_head_col(lsev, 0), _head_col(lsev, 1), _head_col(ccv, 0), _head_col(ccv, 1),
                jnp.sum(jnp.where(lo, dd, 0.0), axis=1, keepdims=True),
                jnp.sum(jnp.where(lo, 0.0, dd), axis=1, keepdims=True))
        lane = lax.broadcasted_iota(jnp.int32, (t, LANES), 1)
        out = jnp.zeros((t, LANES), F32)
        for i, col in enumerate(cols):
            out = jnp.where(lane == i, col, out)
        st_ref[...] = out

    blk = pl.BlockSpec((t, LANES), lambda hp, i: (i, hp))
    return pl.pallas_call(
        body, name=name, grid=(ATTN_HEADS // 2, L // t),
        in_specs=[blk, blk, blk, blk], out_specs=blk,
        out_shape=jax.ShapeDtypeStruct((L, ATTN_WIDTH), F32),
        compiler_params=_params("parallel", "parallel"),
    )(do, o, lse, cum_cols)


def _fox_bwd(qkv, do, stats, cum_rows, *, name):
    L = qkv.shape[0]
    t = _tile(L, ATT_T)
    nq = L // t
    npair = ATTN_HEADS // 2

    def body(q_ref, do_ref, st_ref, k_ref, v_ref, cr_ref, dk_ref, dv_ref, dc_ref, dq_ref, drow_ref):
        ik = pl.program_id(1)

        @pl.when(ik == 0)
        def _():
            dq_ref[...] = jnp.zeros_like(dq_ref)
            drow_ref[...] = jnp.zeros_like(drow_ref)

        lo = _lo_mask(t)
        lane = lax.broadcasted_iota(jnp.int32, (t, LANES), 1)
        kb = k_ref[...]
        vb = v_ref[...]
        zk = jnp.zeros_like(kb)
        kh = (jnp.where(lo, kb, zk), jnp.where(lo, zk, kb))
        vh = (jnp.where(lo, vb, zk), jnp.where(lo, zk, vb))
        ck = (cr_ref[0:1, :], cr_ref[1:2, :])

        def step(iq, carry, masked):
            dk, dv, dc0, dc1 = carry
            start = pl.multiple_of(iq * t, t)
            qb = q_ref[pl.ds(start, t), :] * 0.125
            dob = do_ref[pl.ds(start, t), :]
            stb = st_ref[pl.ds(start, t), :]
            dks, dvs, dcs, dqs, rss = [], [], [], [], []
            for h in range(2):
                s = lax.dot_general(qb, kh[h], NT, preferred_element_type=F32)
                s = s + (_lane_col(stb, STAT_CUM + h) - ck[h])
                if masked:
                    s = jnp.where(_causal(t), s, -jnp.inf)
                p = jnp.exp(s - _lane_col(stb, STAT_LSE + h))
                dp = lax.dot_general(dob, vh[h], NT, preferred_element_type=F32)
                ds = p * (dp - _lane_col(stb, STAT_DELTA + h))
                dsb = ds.astype(BF16)
                dvs.append(lax.dot_general(p.astype(BF16), dob, TN, preferred_element_type=F32))
                dks.append(lax.dot_general(dsb, qb, TN, preferred_element_type=F32))
                dqs.append(jnp.dot(dsb, kh[h], preferred_element_type=F32))
                dcs.append(jnp.sum(ds, axis=0, keepdims=True))
                rss.append(jnp.sum(ds, axis=1, keepdims=True))
            dq_ref[pl.ds(start, t), :] += 0.125 * (dqs[0] + dqs[1])
            drow_ref[pl.ds(start, t), :] += jnp.where(lane == 0, rss[0], jnp.where(lane == 1, rss[1], 0.0))
            return (dk + jnp.where(lo, dks[0], dks[1]), dv + jnp.where(lo, dvs[0], dvs[1]),
                    dc0 - dcs[0], dc1 - dcs[1])

        zero = jnp.zeros((t, LANES), F32)
        zrow = jnp.zeros((1, t), F32)
        carry = step(ik, (zero, zero, zrow, zrow), True)
        dk, dv, dc0, dc1 = lax.fori_loop(ik + 1, nq, lambda iq, c: step(iq, c, False), carry)
        dk_ref[...] = dk.astype(BF16)
        dv_ref[...] = dv.astype(BF16)
        dc_ref[...] = jnp.zeros_like(dc_ref)
        dc_ref[0:1, :] = dc0
        dc_ref[1:2, :] = dc1

    whole = lambda off: pl.BlockSpec((L, LANES), lambda hp, ik: (0, off + hp))
    blk = lambda off: pl.BlockSpec((t, LANES), lambda hp, ik: (ik, off + hp))
    rows = pl.BlockSpec((None, SUBLANES, t), lambda hp, ik: (hp, 0, ik))
    return pl.pallas_call(
        body, name=name, grid=(npair, nq),
        in_specs=[whole(0), whole(0), whole(0), blk(npair), blk(2 * npair), rows],
        out_specs=[blk(0), blk(0), rows, whole(0), whole(0)],
        out_shape=[jax.ShapeDtypeStruct((L, ATTN_WIDTH), BF16), jax.ShapeDtypeStruct((L, ATTN_WIDTH), BF16),
                   jax.ShapeDtypeStruct((npair, SUBLANES, L), F32),
                   jax.ShapeDtypeStruct((L, ATTN_WIDTH), F32), jax.ShapeDtypeStruct((L, ATTN_WIDTH), F32)],
        compiler_params=_params("parallel", "arbitrary"),
    )(qkv, do, stats, qkv, qkv, cum_rows)


def _mod_partial(c_all, mod_w, mod_b_cols, *, name):
    depth, K, cols = mod_w.shape
    tn = _tile(cols, 768)

    def body(c_ref, w_ref, b_ref, o_ref):
        cv = c_ref[...]
        sc = (cv * _sigmoid(cv)).astype(BF16)
        o_ref[...] = jnp.dot(sc, w_ref[...].astype(BF16), preferred_element_type=F32) + b_ref[...]

    return pl.pallas_call(
        body, name=name, grid=(depth, cols // tn),
        in_specs=[pl.BlockSpec((N_DEV, K), lambda l, j: (0, 0)),
                  pl.BlockSpec((None, K, tn), lambda l, j: (l, 0, j)),
                  pl.BlockSpec((None, 1, tn), lambda l, j: (l, 0, j))],
        out_specs=pl.BlockSpec((None, N_DEV, tn), lambda l, j: (l, 0, j)),
        out_shape=jax.ShapeDtypeStruct((depth, N_DEV, cols), F32),
        compiler_params=_params("parallel", "parallel"),
    )(c_all, mod_w, mod_b_cols)


def _mod_wgrad(c_all_t, dmod, *, name):
    depth, nb, cols = dmod.shape
    K = c_all_t.shape[0]
    tn = _tile(cols, 768)
    tk = _tile(K, 256, SUBLANES)

    def body(c_ref, d_ref, o_ref):
        cv = c_ref[...]
        sc = cv * _sigmoid(cv)
        dv = d_ref[...]
        acc = sc[:, 0:1] * dv[0:1, :]
        for b in range(1, nb):
            acc = acc + sc[:, b:b + 1] * dv[b:b + 1, :]
        o_ref[...] = acc

    return pl.pallas_call(
        body, name=name, grid=(depth, K // tk, cols // tn),
        in_specs=[pl.BlockSpec((tk, nb), lambda l, i, j: (i, 0)),
                  pl.BlockSpec((None, nb, tn), lambda l, i, j: (l, 0, j))],
        out_specs=pl.BlockSpec((None, tk, tn), lambda l, i, j: (l, i, j)),
        out_shape=jax.ShapeDtypeStruct((depth, K, cols), F32),
        compiler_params=_params("parallel", "parallel", "parallel"),
    )(c_all_t, dmod)


def _adamw(w, g, m, v, *, name):
    shape = w.shape
    cols = shape[-1]
    rows = int(np.prod(shape[:-1]))
    t = _tile(rows, 256, SUBLANES) if rows % SUBLANES == 0 else rows
    r2 = lambda a: a.reshape(rows, cols)

    def body(w_ref, g_ref, m_ref, v_ref, d_ref, nm_ref, nv_ref):
        gv = g_ref[...]
        nm = ADAM_B1 * m_ref[...] + (1.0 - ADAM_B1) * gv
        nv = ADAM_B2 * v_ref[...] + (1.0 - ADAM_B2) * (gv * gv)
        m_hat = nm / (1.0 - ADAM_B1 ** ADAM_STEP)
        v_hat = nv / (1.0 - ADAM_B2 ** ADAM_STEP)
        d_ref[...] = -ADAM_LR * (m_hat / (jnp.sqrt(v_hat) + ADAM_EPS) + ADAM_WD * w_ref[...])
        nm_ref[...] = nm
        nv_ref[...] = nv

    spec = pl.BlockSpec((t, cols), lambda i: (i, 0))
    sds = jax.ShapeDtypeStruct((rows, cols), F32)
    d, nm, nv = pl.pallas_call(
        body, name=name, grid=(rows // t,),
        in_specs=[spec] * 4, out_specs=[spec] * 3, out_shape=[sds] * 3,
        compiler_params=_params("parallel"),
    )(r2(w), r2(g), r2(m), r2(v))
    return d.reshape(shape), nm.reshape(shape), nv.reshape(shape)


def _my_place():
    return lax.axis_index("x"), lax.axis_index("y"), lax.axis_index("c")


def _other_chips(x, y):
    return [(1 - x, y), (x, 1 - y), (1 - x, 1 - y)]


def _all_gather8(v, *, name, with_sum=False):
    m, n = v.shape

    def body(x_ref, out_ref, *rest):
        if with_sum:
            sum_ref, send_sems, recv_sems, local_sem = rest
        else:
            send_sems, recv_sems, local_sem = rest
        x, y, c = _my_place()
        me, sibling = (x, y, c), (x, y, 1 - c)
        chips = _other_chips(x, y)

        def rows(px, py, pc):
            return out_ref.at[pl.ds((4 * px + 2 * py + pc) * m, m), :]

        def copy(k, block, to, src=None):
            return pltpu.make_async_remote_copy(
                src_ref=rows(*block) if src is None else src, dst_ref=rows(*block),
                send_sem=send_sems.at[k], recv_sem=recv_sems.at[k], device_id=to, device_id_type=MESH)

        mine = pltpu.make_async_copy(x_ref, rows(*me), local_sem)
        mine.start()
        first = [copy(0, me, sibling, src=x_ref)]
        first += [copy(1 + j, me, (*chip, c), src=x_ref) for j, chip in enumerate(chips)]
        for cp in first:
            cp.start()
        passed = [copy(4 + j, (*chip, c), sibling) for j, chip in enumerate(chips)]
        for j, chip in enumerate(chips):
            copy(1 + j, (*chip, c), me).wait_recv()
            passed[j].start()
        copy(0, sibling, me).wait_recv()
        for j, chip in enumerate(chips):
            copy(4 + j, (*chip, 1 - c), me).wait_recv()
        for cp in first + passed:
            cp.wait_send()
        mine.wait()
        if with_sum:
            acc = out_ref[pl.ds(0, m), :]
            for d in range(1, N_DEV):
                acc = acc + out_ref[pl.ds(d * m, m), :]
            sum_ref[...] = acc

    vm = pl.BlockSpec(memory_space=pltpu.VMEM)
    out_shape = [jax.ShapeDtypeStruct((N_DEV * m, n), F32)]
    if with_sum:
        out_shape.append(jax.ShapeDtypeStruct((m, n), F32))
    res = pl.pallas_call(
        body, name=name, out_shape=out_shape, in_specs=[vm], out_specs=[vm] * len(out_shape),
        scratch_shapes=[pltpu.SemaphoreType.DMA((7,)), pltpu.SemaphoreType.DMA((7,)), pltpu.SemaphoreType.DMA],
        compiler_params=pltpu.CompilerParams(vmem_limit_bytes=VMEM_LIMIT),
    )(v)
    return res if with_sum else res[0]


class _Cut(NamedTuple):
    shape: tuple
    slab: int
    half: int


IN_WIDTH = SSM_WIDTH + 3 * ATTN_WIDTH + ATTN_HEADS + 2 * D_MODEL
CUTS = dict(
    ffn_w_in=_Cut((1, D_MODEL, 2 * D_FF), 2, 1),
    ffn_w_out=_Cut((1, D_FF, D_MODEL), 1, 2),
    mix_w_in=_Cut((N_CHIPS, D_MODEL, IN_WIDTH // N_CHIPS), 0, 1),
    glu_w=_Cut((1, SSM_WIDTH, 2 * D_MODEL), 2, 1),
    attn_w_out=_Cut((1, ATTN_WIDTH, D_MODEL), 2, 1),
    mix_w_out=_Cut((1, D_MODEL, D_MODEL), 1, 2),
)
LAYER_MATS = (("ffn_w_in", 0), ("ffn_w_in", 1), ("ffn_w_out", 0), ("ffn_w_out", 1), ("mix_w_in", None),
              ("glu_w", None), ("attn_w_out", None), ("mix_w_out", None))


def _part_shape(cut, slab=False, half=False):
    s = list(cut.shape)
    if slab:
        s[cut.slab] //= N_CHIPS
    if half:
        s[cut.half] //= 2
    return tuple(s)


def _window(ref, cut, slab=None, half=None):
    idx = [slice(None)] * len(cut.shape)
    for axis, parts, which in ((cut.slab, N_CHIPS, slab), (cut.half, 2, half)):
        if which is not None:
            width = cut.shape[axis] // parts
            idx[axis] = pl.ds(pl.multiple_of(which * width, width), width)
    return ref.at[tuple(idx)]


HBM_SPEC = pl.BlockSpec(memory_space=pltpu.HBM)


def _gather_layer(shards, cuts, *, name):
    n = len(shards)

    def body(*refs):
        s_refs, f_refs = refs[:n], refs[n:2 * n]
        send_sems, recv_sems = refs[2 * n:]
        x, y, c = _my_place()
        me, sibling, mine = (x, y, c), (x, y, 1 - c), 2 * x + y
        chips = _other_chips(x, y)

        def copy(i, k, src, dst, to):
            return pltpu.make_async_remote_copy(
                src_ref=src, dst_ref=dst, send_sem=send_sems.at[7 * i + k], recv_sem=recv_sems.at[7 * i + k],
                device_id=to, device_id_type=MESH)

        started = [copy(i, 6, s_refs[i], _window(f_refs[i], cuts[i], slab=mine), sibling) for i in range(n)]
        for cp in started:
            cp.start()
        for i in range(n):
            for j, chip in enumerate(chips):
                cp = copy(i, j, _window(s_refs[i], cuts[i], half=c), _window(f_refs[i], cuts[i], slab=mine, half=c),
                          (*chip, c))
                cp.start()
                started.append(cp)
        for i in range(n):
            for j, chip in enumerate(chips):
                part = _window(f_refs[i], cuts[i], slab=2 * chip[0] + chip[1], half=c)
                copy(i, j, part, part, me).wait_recv()
                cp = copy(i, 3 + j, part, part, sibling)
                cp.start()
                started.append(cp)
        for i in range(n):
            for j, chip in enumerate(chips):
                part = _window(f_refs[i], cuts[i], slab=2 * chip[0] + chip[1], half=1 - c)
                copy(i, 3 + j, part, part, me).wait_recv()
        for i in range(n):
            own = _window(f_refs[i], cuts[i], slab=mine)
            copy(i, 6, own, own, me).wait_recv()
        for cp in started:
            cp.wait_send()

    return pl.pallas_call(
        body, name=name, out_shape=[jax.ShapeDtypeStruct(cut.shape, s.dtype) for s, cut in zip(shards, cuts)],
        in_specs=[HBM_SPEC] * n, out_specs=[HBM_SPEC] * n,
        scratch_shapes=[pltpu.SemaphoreType.DMA((7 * n,)), pltpu.SemaphoreType.DMA((7 * n,))],
    )(*shards)


def _swap_layer(mats, cuts, *, name):
    n = len(mats)

    def body(*refs):
        m_refs, r_refs = refs[:n], refs[n:2 * n]
        send_sems, recv_sems = refs[2 * n:]
        x, y, c = _my_place()
        cps = [pltpu.make_async_remote_copy(
            src_ref=_window(m_refs[i], cuts[i], half=1 - c), dst_ref=r_refs[i], send_sem=send_sems.at[i],
            recv_sem=recv_sems.at[i], device_id=(x, y, 1 - c), device_id_type=MESH) for i in range(n)]
        for cp in cps:
            cp.start()
        for cp in cps:
            cp.wait()

    return pl.pallas_call(
        body, name=name, out_shape=[jax.ShapeDtypeStruct(_part_shape(cut, half=True), m.dtype)
                                    for m, cut in zip(mats, cuts)],
        in_specs=[HBM_SPEC] * n, out_specs=[HBM_SPEC] * n,
        scratch_shapes=[pltpu.SemaphoreType.DMA((n,)), pltpu.SemaphoreType.DMA((n,))],
    )(*mats)


def _partials_layer(sums, cuts, *, name):
    n = len(sums)

    def body(*refs):
        s_refs, p_refs = refs[:n], refs[n:2 * n]
        send_sems, recv_sems = refs[2 * n:]
        x, y, c = _my_place()
        cps = []
        for i in range(n):
            for j, chip in enumerate(_other_chips(x, y)):
                cps.append(pltpu.make_async_remote_copy(
                    src_ref=_window(s_refs[i], cuts[i], slab=2 * chip[0] + chip[1]), dst_ref=p_refs[i].at[j],
                    send_sem=send_sems.at[3 * i + j], recv_sem=recv_sems.at[3 * i + j],
                    device_id=(*chip, c), device_id_type=MESH))
        for cp in cps:
            cp.start()
        for cp in cps:
            cp.wait()

    return pl.pallas_call(
        body, name=name,
        out_shape=[jax.ShapeDtypeStruct((3,) + _part_shape(cut, slab=True, half=True), s.dtype)
                   for s, cut in zip(sums, cuts)],
        in_specs=[HBM_SPEC] * n, out_specs=[HBM_SPEC] * n,
        scratch_shapes=[pltpu.SemaphoreType.DMA((3 * n,)), pltpu.SemaphoreType.DMA((3 * n,))],
    )(*sums)


def _share_all(dests, cuts, places, *, name):
    names = list(dests)
    nn, n = len(names), len(places)

    def body(*refs):
        o_refs = dict(zip(names, refs[nn:2 * nn]))
        send_sems, recv_sems = refs[2 * nn:]
        x, y, c = _my_place()

        def win(i, half):
            slab_cut = _Cut(_part_shape(cuts[i], slab=True), cuts[i].slab, cuts[i].half)
            return _window(o_refs[places[i][0]].at[places[i][1]], slab_cut, half=half)

        def copy(i, half):
            return pltpu.make_async_remote_copy(
                src_ref=win(i, half), dst_ref=win(i, half), send_sem=send_sems.at[i], recv_sem=recv_sems.at[i],
                device_id=(x, y, 1 - c), device_id_type=MESH)

        for i in range(n):
            copy(i, c).start()
        for i in range(n):
            copy(i, c).wait_send()
            copy(i, 1 - c).wait_recv()

    return pl.pallas_call(
        body, name=name, out_shape=[jax.ShapeDtypeStruct(dests[k].shape, F32) for k in names],
        in_specs=[HBM_SPEC] * nn, out_specs=[HBM_SPEC] * nn,
        input_output_aliases={i: i for i in range(nn)},
        scratch_shapes=[pltpu.SemaphoreType.DMA((n,)), pltpu.SemaphoreType.DMA((n,))],
    )(*[dests[k] for k in names])


def _cut_blocks(shape):
    _, R, C = shape
    tr = _tile(R, 256, 16)
    tc = _tile(C, 2048) if C % LANES == 0 else C
    return (None, tr, tc), (shape[0], R // tr, C // tc)


def _offset_map(axis, blocks):
    def index_map(b, i, j, which):
        idx = [b, i, j]
        idx[axis] = which[0] * blocks[axis] + idx[axis]
        return tuple(idx)
    return index_map


def _add_half(mat, other, cut, c_idx, *, name):
    shape = _part_shape(cut, half=True)
    block, grid = _cut_blocks(shape)

    def body(c_ref, m_ref, o_ref, f_ref, b_ref):
        s = m_ref[...] + o_ref[...]
        f_ref[...] = s
        b_ref[...] = s.astype(BF16)

    plain = pl.BlockSpec(block, lambda b, i, j, which: (b, i, j))
    grid_spec = pltpu.PrefetchScalarGridSpec(
        num_scalar_prefetch=1, grid=grid,
        in_specs=[pl.BlockSpec(block, _offset_map(cut.half, grid)), plain], out_specs=[plain, plain])
    return pl.pallas_call(
        body, name=name, grid_spec=grid_spec,
        out_shape=[jax.ShapeDtypeStruct(shape, F32), jax.ShapeDtypeStruct(shape, BF16)],
        compiler_params=_params("parallel", "parallel", "parallel"),
    )(c_idx, mat, other)


def _sum_slab(own, parts, cut, dest, place, chip_idx, c_idx, *, name):
    shape = _part_shape(cut, slab=True, half=True)
    block, grid = _cut_blocks(shape)
    assert shape[0] == 1

    def body(k_ref, c_ref, o_ref, p_ref, dest_ref, out_ref):
        acc = o_ref[...]
        for j in range(3):
            acc = acc + p_ref[j].astype(F32)
        out_ref[...] = acc

    def own_map(b, i, j, chip, core):
        idx = [b, i, j]
        idx[cut.slab] = chip[0] * grid[cut.slab] + idx[cut.slab]
        return tuple(idx)

    def dest_map(b, i, j, chip, core):
        idx = [b, i, j]
        idx[cut.half] = core[0] * grid[cut.half] + idx[cut.half]
        return tuple(place) + tuple(idx)

    grid_spec = pltpu.PrefetchScalarGridSpec(
        num_scalar_prefetch=2, grid=grid,
        in_specs=[pl.BlockSpec(block, own_map),
                  pl.BlockSpec((3,) + block[1:], lambda b, i, j, chip, core: (0, i, j)),
                  pl.BlockSpec(memory_space=pl.ANY)],
        out_specs=pl.BlockSpec((None,) * len(place) + block, dest_map))
    return pl.pallas_call(
        body, name=name, grid_spec=grid_spec, out_shape=jax.ShapeDtypeStruct(dest.shape, F32),
        input_output_aliases={4: 0},
        compiler_params=_params("parallel", "parallel", "parallel"),
    )(chip_idx, c_idx, own, parts.reshape((3,) + shape[1:]), dest)


def _pad_rows(flat, cols=8 * LANES, align=SUBLANES):
    n = flat.shape[0]
    rows = -(-n // (cols * align)) * align
    return jnp.pad(flat, (0, rows * cols - n)).reshape(rows, cols)


def _row(v):
    return v.reshape(1, -1)


def _ffn_fwd(x, mod, g_pre, g_post, w_in, w_out, tag):
    sh, sc, gate = _row(mod[0]), _row(mod[1]), _row(mod[2])
    h = _prenorm(x, _row(g_pre), sc, sh, name=f"prenorm_{tag}")
    gt, up, act = _mm_swiglu(h, w_in, name=f"swiglu_{tag}")
    y = _mm(act, w_out, name=f"ffn_out_{tag}", tk=1408)
    x_out = _postnorm_res(x, y, _row(g_post), gate, FFN_RES, name=f"postnorm_{tag}")
    return x_out, (x, h, gt, up, act, y)


def _ffn_bwd(dxo, saved, mod, g_pre, g_post, w_in, w_out, tag):
    x, h, gt, up, act, y = saved
    sc, gate = _row(mod[1]), _row(mod[2])
    dy, dgate, dgpost = _postnorm_bwd(dxo, y, _row(g_post), gate, FFN_RES, name=f"postnorm_bwd_{tag}")
    dgt, dup = _mm_swiglu_bwd(dy, w_out, gt, up, name=f"swiglu_bwd_{tag}")
    dw_out = _mm(act, dy, ta=True, name=f"dw_out_{tag}", tm=1408, tn=1024, tk=1024)
    dh = _mm(dgt, w_in, tb=True, name=f"dh_gate_{tag}", tk=1408)
    dh = _mm(dup, w_in, tb=True, b_k0=D_FF, bias=dh, name=f"dh_up_{tag}", tk=1408)
    dwg = _mm(h, dgt, ta=True, name=f"dw_gate_{tag}", tm=1024, tn=1408, tk=1024)
    dwu = _mm(h, dup, ta=True, name=f"dw_up_{tag}", tm=1024, tn=1408, tk=1024)
    dx, dsh, dsc, dgpre = _prenorm_bwd(x, dh, dxo, _row(g_pre), sc, name=f"prenorm_bwd_{tag}")
    dmod = jnp.stack([dsh[0], dsc[0], dgate[0]])
    return dx, dmod, dgpre[0], dgpost[0], jnp.concatenate([dwg, dwu], axis=1), dw_out


def _split_mix_w_in(w):
    u0, q0, f0, g0 = 0, SSM_WIDTH, SSM_WIDTH + 3 * ATTN_WIDTH, SSM_WIDTH + 3 * ATTN_WIDTH + ATTN_HEADS
    w_f = jnp.pad(w[:, f0:g0], ((0, 0), (0, LANES - ATTN_HEADS)))
    return w[:, u0:q0], w[:, q0:f0], w_f, w[:, g0:]


def _mixer_fwd(x, mod, g_pre, g_post, w, ssm, forget_b, tag):
    L = x.shape[0]
    sh, sc, gate = _row(mod[0]), _row(mod[1]), _row(mod[2])
    lam_re, lam_im, bin_re, bin_im, cout_re, cout_im, dskip = ssm
    h = _prenorm(x, _row(g_pre), sc, sh, name=f"prenorm_{tag}")
    u = _mm(h, w["u"], name=f"proj_u_{tag}")
    qkv = _mm(h, w["qkv"], out_dtype=BF16, name=f"proj_qkv_{tag}")
    f = _mm(h, w["f"], name=f"proj_f_{tag}")
    gab = _mm(h, w["gab"], name=f"proj_gab_{tag}")
    u_i = _interleave(u)
    bu_re = _mm(u_i, bin_re, name=f"ssm_bu_re_{tag}")
    bu_im = _mm(u_i, bin_im, name=f"ssm_bu_im_{tag}")
    s_re, s_im = _ssm_scan_fwd(bu_re, bu_im, lam_re, lam_im, name=f"ssm_scan_{tag}")
    y_ssm = _mm(s_re, cout_re, bias=u_i, bscale=dskip, name=f"ssm_y_re_{tag}", tk=2048)
    y_ssm = _deinterleave(_mm(s_im, cout_im, bias=y_ssm, name=f"ssm_y_im_{tag}", tk=2048))
    gl = _gelu_fwd(y_ssm, name=f"gelu_{tag}")
    z = _mm(gl, w["glu"], name=f"glu_{tag}")
    fb = jnp.pad(forget_b, (0, LANES - ATTN_HEADS)).reshape(1, LANES)
    cum = _fox_cum(f, fb, name=f"fox_cum_{tag}")
    cum8 = cum[:, :ATTN_HEADS]
    cum_cols = jnp.repeat(cum8, HEAD_DIM, axis=1)
    cum_rows = jnp.pad(cum8.T.reshape(ATTN_HEADS // 2, 2, L), ((0, 0), (0, SUBLANES - 2), (0, 0)))
    attn, attn32, lse = _fox_fwd(qkv, cum_cols, cum_rows, name=f"fox_fwd_{tag}")
    yb = _mm(attn, w["attn_out"], name=f"attn_out_{tag}")
    merged = _merge_fwd(z, yb, gab, name=f"merge_{tag}")
    y = _mm(merged, w["out"], name=f"mix_out_{tag}")
    x_out = _postnorm_res(x, y, _row(g_post), gate, 1.0, name=f"postnorm_{tag}")
    saved = (x, h, u, u_i, qkv, f, gab, s_re, s_im, y_ssm, gl, z, fb, cum_cols, cum_rows, attn, attn32, lse, yb,
             merged, y)
    return x_out, saved


def _mixer_bwd(dxo, saved, mod, g_pre, g_post, w, ssm, tag):
    (x, h, u, u_i, qkv, f, gab, s_re, s_im, y_ssm, gl, z, fb, cum_cols, cum_rows, attn, attn32, lse, yb,
     merged, y) = saved
    L = x.shape[0]
    sc, gate = _row(mod[1]), _row(mod[2])
    lam_re, lam_im, bin_re, bin_im, cout_re, cout_im, dskip = ssm
    dy, dgate, dgpost = _postnorm_bwd(dxo, y, _row(g_post), gate, 1.0, name=f"postnorm_bwd_{tag}")
    dmerged = _mm(dy, w["out"], tb=True, name=f"dmerged_{tag}")
    dw_out = _mm(merged, dy, ta=True, name=f"dw_mix_out_{tag}", tm=1024, tn=1024)
    dz, dyb, dgab = _merge_bwd(dmerged, z, yb, gab, name=f"merge_bwd_{tag}")
    dgl = _mm(dz, w["glu"], tb=True, name=f"dgl_{tag}", tk=2048)
    dw_glu = _mm(gl, dz, ta=True, name=f"dw_glu_{tag}", tm=512, tn=2048)
    dys, dsk, dd = _gelu_bwd(dgl, y_ssm, u, dskip, name=f"gelu_bwd_{tag}")
    dys, dsk = _interleave(dys), _interleave(dsk)
    d_re = _mm(dys, cout_re, tb=True, name=f"ssm_ds_re_{tag}")
    d_im = _mm(dys, cout_im, tb=True, name=f"ssm_ds_im_{tag}")
    dcout_re = _mm(s_re, dys, ta=True, name=f"ssm_dc_re_{tag}", tm=1024, tn=512)
    dcout_im = _mm(s_im, dys, ta=True, name=f"ssm_dc_im_{tag}", tm=1024, tn=512)
    g_re, g_im, dlam_re, dlam_im = _ssm_scan_bwd(d_re, d_im, s_re, s_im, lam_re, lam_im, name=f"ssm_scan_bwd_{tag}")
    du = _mm(g_re, bin_re, tb=True, bias=dsk, name=f"ssm_du_re_{tag}", tk=2048)
    du = _mm(g_im, bin_im, tb=True, bias=du, out_dtype=BF16, name=f"ssm_du_im_{tag}", tk=2048)
    du = _deinterleave(du)
    dbin_re = _mm(u_i, g_re, ta=True, name=f"ssm_db_re_{tag}", tm=512, tn=1024)
    dbin_im = _mm(u_i, g_im, ta=True, name=f"ssm_db_im_{tag}", tm=512, tn=1024)
    dssm = (dlam_re, dlam_im, dbin_re, dbin_im, dcout_re, dcout_im, _row(dd[0]))
    dattn = _mm(dyb, w["attn_out"], tb=True, out_dtype=BF16, name=f"dattn_{tag}")
    dw_attn = _mm(attn, dyb, ta=True, name=f"dw_attn_out_{tag}", tm=512, tn=1024)
    stats = _fox_rowstats(dattn, attn32, lse, cum_cols, name=f"fox_rowstats_{tag}")
    dk, dv, dcum_rows, dq, drow = _fox_bwd(qkv, dattn, stats, cum_rows, name=f"fox_bwd_{tag}")
    drow8 = drow.reshape(L, ATTN_HEADS // 2, LANES)[:, :, :2].reshape(L, ATTN_HEADS)
    dcum = drow8 + dcum_rows[:, :2, :].reshape(ATTN_HEADS, L).T
    dcum = jnp.pad(dcum, ((0, 0), (0, LANES - ATTN_HEADS)))
    df, dfb = _fox_cum_bwd(dcum, f, fb, name=f"fox_cum_bwd_{tag}")
    dqkv = jnp.concatenate([dq.astype(BF16), dk, dv], axis=1)
    dh = _mm(dqkv, w["qkv"], tb=True, name=f"dh_qkv_{tag}", tk=1536)
    dh = _mm(du, w["u"], tb=True, bias=dh, name=f"dh_u_{tag}")
    dh = _mm(dgab, w["gab"], tb=True, bias=dh, name=f"dh_gab_{tag}", tk=2048)
    dh = _mm(df, w["f"], tb=True, bias=dh, name=f"dh_f_{tag}")
    dw_u = _mm(h, du, ta=True, name=f"dw_u_{tag}", tm=1024, tn=512)
    dw_qkv = _mm(h, dqkv, ta=True, name=f"dw_qkv_{tag}", tm=1024, tn=1536)
    dw_f = _mm(h, df, ta=True, name=f"dw_f_{tag}", tm=1024)
    dw_gab = _mm(h, dgab, ta=True, name=f"dw_gab_{tag}", tm=1024, tn=1024)
    dw_in = jnp.concatenate([dw_u, dw_qkv, dw_f[:, :ATTN_HEADS], dw_gab], axis=1)
    dx, dsh, dsc, dgpre = _prenorm_bwd(x, dh, dxo, _row(g_pre), sc, name=f"prenorm_bwd_{tag}")
    dmod = jnp.stack([dsh[0], dsc[0], dgate[0]])
    grads = dict(mix_w_in=dw_in, glu_w=dw_glu, attn_w_out=dw_attn, mix_w_out=dw_out,
                 forget_b=dfb[0, :ATTN_HEADS])
    return dx, dmod, dgpre[0], dgpost[0], grads, dssm


SSM_NAMES = ("ssm_a_re", "ssm_a_im", "ssm_log_dt", "ssm_b_re", "ssm_b_im", "ssm_c_re", "ssm_c_im", "ssm_d")
SMALL_NAMES = ("forget_b",) + SSM_NAMES
WEIGHT_NAMES = ("mod_w", "mod_b", "norm_pre", "norm_post", "ffn_w_in", "ffn_w_out", "mix_w_in", "forget_b") \
    + SSM_NAMES + ("glu_w", "attn_w_out", "mix_w_out")


def _layer_shards(w, l):
    return [(w[n][l] if j is None else w[n][l, j]).astype(BF16).reshape(_part_shape(CUTS[n], slab=True))
            for n, j in LAYER_MATS]


def _train_step(x, c, target, w, m, v):
    xi, yi, ci = _my_place()
    chip = 2 * xi + yi
    dev = 4 * xi + 2 * yi + ci
    mod_cols = N_SUB * 3 * D_MODEL // N_CHIPS
    norm_cols = D_MODEL // N_CHIPS

    cuts = [CUTS[n] for n, _ in LAYER_MATS]
    full = [_gather_layer(_layer_shards(w, l), cuts, name=f"gather_weights_l{l}") for l in range(DEPTH)]
    c_all = _all_gather8(jnp.pad(c, ((0, SUBLANES - 1), (0, 0))), name="gather_c")[::SUBLANES]
    mod_b_cols = lax.dynamic_slice_in_dim(w["mod_b"], chip * mod_cols, mod_cols, axis=1)[:, None, :]
    mod_part = _mod_partial(c_all, w["mod_w"], mod_b_cols, name="mod_partial")
    small_fwd = jnp.concatenate([mod_part.reshape(-1), w["norm_pre"].reshape(-1), w["norm_post"].reshape(-1)])
    n_mod, n_norm = mod_part.size, w["norm_pre"].size
    sf_all = _all_gather8(_pad_rows(small_fwd), name="gather_mod").reshape(N_DEV, -1)
    sf_chips = sf_all[::2]
    mod_all = jnp.concatenate(
        [sf_chips[k, :n_mod].reshape(DEPTH, N_DEV, mod_cols) for k in range(N_CHIPS)], axis=2)
    mod_mine = lax.dynamic_index_in_dim(mod_all, dev, axis=1, keepdims=False).reshape(DEPTH, N_SUB, 3, D_MODEL)
    norm_pre = jnp.concatenate(
        [sf_chips[k, n_mod:n_mod + n_norm].reshape(DEPTH, N_SUB, norm_cols) for k in range(N_CHIPS)], axis=2)
    norm_post = jnp.concatenate(
        [sf_chips[k, n_mod + n_norm:n_mod + 2 * n_norm].reshape(DEPTH, N_SUB, norm_cols) for k in range(N_CHIPS)],
        axis=2)

    saved, layer_w, ssm_prep, ssm_vjp = [], [], [], []
    h = x
    for l in range(DEPTH):
        win0, win1, wout0, wout1, mix_in, glu, attn_out, mix_out = full[l]
        w_u, w_qkv, w_f, w_gab = _split_mix_w_in(mix_in.transpose(1, 0, 2).reshape(D_MODEL, IN_WIDTH))
        lw = dict(
            ffn=[(win0[0], wout0[0]), (win1[0], wout1[0])],
            mix=dict(u=w_u, qkv=w_qkv, f=w_f, gab=w_gab, glu=glu[0], attn_out=attn_out[0], out=mix_out[0]))
        prep, vjp = jax.vjp(_ssm_discretize, *[w[n][l] for n in SSM_NAMES])
        layer_w.append(lw)
        ssm_prep.append(prep)
        ssm_vjp.append(vjp)
        h, s0 = _ffn_fwd(h, mod_mine[l, 0], norm_pre[l, 0], norm_post[l, 0], *lw["ffn"][0], tag=f"l{l}a")
        h, s1 = _mixer_fwd(h, mod_mine[l, 1], norm_pre[l, 1], norm_post[l, 1], lw["mix"], prep,
                           w["forget_b"][l], tag=f"l{l}m")
        h, s2 = _ffn_fwd(h, mod_mine[l, 2], norm_pre[l, 2], norm_post[l, 2], *lw["ffn"][1], tag=f"l{l}b")
        saved.append((s0, s1, s2))
    dh, loss8 = _loss_head(h, target, name="loss_head")

    c_idx = ci.reshape(1).astype(jnp.int32)
    chip_idx = chip.reshape(1).astype(jnp.int32)
    places = []
    dests = {n: lax.empty(((DEPTH,) if j is None else (DEPTH, 2)) + _part_shape(CUTS[n], slab=True), F32)
             for n, j in LAYER_MATS}
    g_small = {n: [None] * DEPTH for n in SMALL_NAMES}
    dmod, dnpre, dnpost = [], [], []
    for l in reversed(range(DEPTH)):
        lw = layer_w[l]
        dh, dm2, dp2, dq2, dwin2, dwout2 = _ffn_bwd(dh, saved[l][2], mod_mine[l, 2], norm_pre[l, 2],
                                                    norm_post[l, 2], *lw["ffn"][1], tag=f"l{l}b")
        dh, dm1, dp1, dq1, gmix, dssm = _mixer_bwd(dh, saved[l][1], mod_mine[l, 1], norm_pre[l, 1],
                                                   norm_post[l, 1], lw["mix"], ssm_prep[l], tag=f"l{l}m")
        dh, dm0, dp0, dq0, dwin0, dwout0 = _ffn_bwd(dh, saved[l][0], mod_mine[l, 0], norm_pre[l, 0],
                                                    norm_post[l, 0], *lw["ffn"][0], tag=f"l{l}a")
        dmod.insert(0, jnp.stack([dm0, dm1, dm2]))
        dnpre.insert(0, jnp.stack([dp0, dp1, dp2]))
        dnpost.insert(0, jnp.stack([dq0, dq1, dq2]))
        g_small["forget_b"][l] = gmix["forget_b"]
        for n, g in zip(SSM_NAMES, ssm_vjp[l](dssm)):
            g_small[n][l] = g
        dmix_in = gmix["mix_w_in"].reshape(D_MODEL, N_CHIPS, IN_WIDTH // N_CHIPS).transpose(1, 0, 2)
        mats = [dwin0, dwin2, dwout0, dwout2, dmix_in, gmix["glu_w"], gmix["attn_w_out"], gmix["mix_w_out"]]
        mats = [a.reshape(cut.shape) for a, cut in zip(mats, cuts)]
        from_sibling = _swap_layer(mats, cuts, name=f"grad_swap_l{l}")
        sums = [_add_half(mats[i], from_sibling[i], cuts[i], c_idx, name=f"grad_add_l{l}_{i}")
                for i in range(len(mats))]
        parts = _partials_layer([s[1] for s in sums], cuts, name=f"grad_partials_l{l}")
        for i, (n, j) in enumerate(LAYER_MATS):
            place = (l,) if j is None else (l, j)
            dests[n] = _sum_slab(sums[i][0], parts[i], cuts[i], dests[n], place, chip_idx, c_idx,
                                 name=f"grad_sum_l{l}_{i}")
            places.append((n, place))
    grad_x = dh
    g_small = {n: jnp.stack(g) for n, g in g_small.items()}

    small = [loss8[0, :1], jnp.stack(dmod).reshape(-1), jnp.stack(dnpre).reshape(-1), jnp.stack(dnpost).reshape(-1)]
    small += [g_small[n].reshape(-1) for n in SMALL_NAMES]
    sizes = [int(s.size) for s in small]
    offs = np.concatenate([[0], np.cumsum(sizes)])
    sb_all, sb_sum = _all_gather8(_pad_rows(jnp.concatenate(small)), name="gather_small_grads", with_sum=True)
    sb_sum = sb_sum.reshape(-1)
    take = lambda i: sb_sum[int(offs[i]):int(offs[i + 1])]
    loss = take(0)[0]
    grads = {"mod_b": take(1).reshape(DEPTH, N_SUB * 3 * D_MODEL)}
    dnorm_pre_full = take(2).reshape(DEPTH, N_SUB, D_MODEL)
    dnorm_post_full = take(3).reshape(DEPTH, N_SUB, D_MODEL)
    grads["norm_pre"] = lax.dynamic_slice_in_dim(dnorm_pre_full, chip * norm_cols, norm_cols, axis=2)
    grads["norm_post"] = lax.dynamic_slice_in_dim(dnorm_post_full, chip * norm_cols, norm_cols, axis=2)
    for i, n in enumerate(SMALL_NAMES):
        grads[n] = take(4 + i).reshape(w[n].shape)
    dmod_all = sb_all.reshape(N_DEV, -1)[:, int(offs[1]):int(offs[2])].reshape(N_DEV, DEPTH, N_SUB * 3 * D_MODEL)
    dmod_cols = lax.dynamic_slice_in_dim(dmod_all, chip * mod_cols, mod_cols, axis=2).transpose(1, 0, 2)
    grads["mod_w"] = _mod_wgrad(c_all.T, dmod_cols, name="mod_wgrad")

    shared = _share_all(dests, [CUTS[n] for n, _ in places], places, name="grad_share")
    for n, g in zip(dests, shared):
        grads[n] = g.reshape(w[n].shape)

    delta, new_m, new_v = {}, {}, {}
    for n in WEIGHT_NAMES:
        delta[n], new_m[n], new_v[n] = _adamw(w[n], grads[n], m[n], v[n], name=f"adamw_{n}")
    outs = [loss, grad_x[None]]
    for group in (grads, delta, new_m, new_v):
        outs += [group[n] for n in WEIGHT_NAMES]
    return tuple(outs)


def kernel(x, c, mod_w, mod_b, norm_pre, norm_post, ffn_w_in, ffn_w_out, mix_w_in, forget_b, ssm_a_re, ssm_a_im, ssm_log_dt, ssm_b_re, ssm_b_im, ssm_c_re, ssm_c_im, ssm_d, glu_w, attn_w_out, mix_w_out, loss_target, m_mod_w, m_mod_b, m_norm_pre, m_norm_post, m_ffn_w_in, m_ffn_w_out, m_mix_w_in, m_forget_b, m_ssm_a_re, m_ssm_a_im, m_ssm_log_dt, m_ssm_b_re, m_ssm_b_im, m_ssm_c_re, m_ssm_c_im, m_ssm_d, m_glu_w, m_attn_w_out, m_mix_w_out, v_mod_w, v_mod_b, v_norm_pre, v_norm_post, v_ffn_w_in, v_ffn_w_out, v_mix_w_in, v_forget_b, v_ssm_a_re, v_ssm_a_im, v_ssm_log_dt, v_ssm_b_re, v_ssm_b_im, v_ssm_c_re, v_ssm_c_im, v_ssm_d, v_glu_w, v_attn_w_out, v_mix_w_out):
    w = dict(mod_w=mod_w, mod_b=mod_b, norm_pre=norm_pre, norm_post=norm_post, ffn_w_in=ffn_w_in,
             ffn_w_out=ffn_w_out, mix_w_in=mix_w_in, forget_b=forget_b, ssm_a_re=ssm_a_re, ssm_a_im=ssm_a_im,
             ssm_log_dt=ssm_log_dt, ssm_b_re=ssm_b_re, ssm_b_im=ssm_b_im, ssm_c_re=ssm_c_re, ssm_c_im=ssm_c_im,
             ssm_d=ssm_d, glu_w=glu_w, attn_w_out=attn_w_out, mix_w_out=mix_w_out)
    m = dict(mod_w=m_mod_w, mod_b=m_mod_b, norm_pre=m_norm_pre, norm_post=m_norm_post, ffn_w_in=m_ffn_w_in,
             ffn_w_out=m_ffn_w_out, mix_w_in=m_mix_w_in, forget_b=m_forget_b, ssm_a_re=m_ssm_a_re,
             ssm_a_im=m_ssm_a_im, ssm_log_dt=m_ssm_log_dt, ssm_b_re=m_ssm_b_re, ssm_b_im=m_ssm_b_im,
             ssm_c_re=m_ssm_c_re, ssm_c_im=m_ssm_c_im, ssm_d=m_ssm_d, glu_w=m_glu_w, attn_w_out=m_attn_w_out,
             mix_w_out=m_mix_w_out)
    v = dict(mod_w=v_mod_w, mod_b=v_mod_b, norm_pre=v_norm_pre, norm_post=v_norm_post, ffn_w_in=v_ffn_w_in,
             ffn_w_out=v_ffn_w_out, mix_w_in=v_mix_w_in, forget_b=v_forget_b, ssm_a_re=v_ssm_a_re,
             ssm_a_im=v_ssm_a_im, ssm_log_dt=v_ssm_log_dt, ssm_b_re=v_ssm_b_re, ssm_b_im=v_ssm_b_im,
             ssm_c_re=v_ssm_c_re, ssm_c_im=v_ssm_c_im, ssm_d=v_ssm_d, glu_w=v_glu_w, attn_w_out=v_attn_w_out,
             mix_w_out=v_mix_w_out)
    return _train_step(x[0], c, loss_target[0], w, m, v)
```

```python
import functools
import math
from typing import NamedTuple

import jax
import jax.numpy as jnp
import numpy as np
from jax import lax
from jax.experimental import pallas as pl
from jax.experimental.pallas import tpu as pltpu

F32 = jnp.float32
BF16 = jnp.bfloat16

D_MODEL = 1024
DEPTH = 2
SSM_WIDTH = 512
SSM_GROUP = 16
SSM_GROUPS = 32
SSM_STATE = 64
SSM_FLAT = SSM_GROUPS * SSM_STATE
ATTN_HEADS = 8
HEAD_DIM = 64
ATTN_WIDTH = 512
D_FF = 2816
FFN_RES = 0.5
N_SUB = 3
RMS_EPS = 1e-6
N_CHIPS = 4
N_DEV = 8

ADAM_LR = 0.001
ADAM_B1 = 0.9
ADAM_B2 = 0.999
ADAM_EPS = 1e-08
ADAM_WD = 0.01
ADAM_STEP = 10

LANES = 128
SUBLANES = 8
VMEM_LIMIT = 52 * 1024 * 1024
MESH = pl.DeviceIdType.MESH

NN = (((1,), (0,)), ((), ()))
NT = (((1,), (1,)), ((), ()))
TN = (((0,), (0,)), ((), ()))


def _tile(dim, target, align=LANES):
    best = None
    t = align
    while t <= min(dim, target):
        if dim % t == 0:
            best = t
        t += align
    return dim if best is None else best


def _params(*sem):
    return pltpu.CompilerParams(dimension_semantics=sem, vmem_limit_bytes=VMEM_LIMIT)


def _mm(a, b, *, name, ta=False, tb=False, out_dtype=F32, bias=None, bscale=None, b_k0=0,
        tm=512, tn=1024, tk=1024):
    M, K = (a.shape[1], a.shape[0]) if ta else a.shape
    N = b.shape[0] if tb else b.shape[1]
    assert b_k0 + K <= (b.shape[1] if tb else b.shape[0]), (a.shape, b.shape, ta, tb)
    tm, tn, tk = _tile(M, tm), _tile(N, tn), _tile(K, tk)
    nk = K // tk
    assert b_k0 % tk == 0
    kb0 = b_k0 // tk
    dn = (((0 if ta else 1,), (1 if tb else 0,)), ((), ()))
    has_bias, has_scale = bias is not None, bscale is not None

    def body(*refs):
        a_ref, b_ref = refs[0], refs[1]
        pos = 2
        bias_ref = scale_ref = None
        if has_bias:
            bias_ref = refs[pos]
            pos += 1
        if has_scale:
            scale_ref = refs[pos]
            pos += 1
        o_ref = refs[pos]
        acc_ref = refs[pos + 1] if nk > 1 else None

        def finish(r):
            if has_bias:
                extra = bias_ref[...].astype(F32)
                if has_scale:
                    extra = extra * scale_ref[...]
                r = r + extra
            o_ref[...] = r.astype(out_dtype)

        part = lax.dot_general(a_ref[...].astype(BF16), b_ref[...].astype(BF16), dn,
                               preferred_element_type=F32)
        if nk == 1:
            finish(part)
        else:
            k = pl.program_id(2)

            @pl.when(k == 0)
            def _():
                acc_ref[...] = part

            @pl.when(k > 0)
            def _():
                acc_ref[...] += part

            @pl.when(k == nk - 1)
            def _():
                finish(acc_ref[...])

    a_spec = pl.BlockSpec((tk, tm), lambda j, i, k: (k, i)) if ta else pl.BlockSpec((tm, tk), lambda j, i, k: (i, k))
    b_spec = (pl.BlockSpec((tn, tk), lambda j, i, k: (j, kb0 + k)) if tb
              else pl.BlockSpec((tk, tn), lambda j, i, k: (kb0 + k, j)))
    in_specs = [a_spec, b_spec]
    args = [a, b]
    if has_bias:
        in_specs.append(pl.BlockSpec((tm, tn), lambda j, i, k: (i, j)))
        args.append(bias)
    if has_scale:
        in_specs.append(pl.BlockSpec((1, tn), lambda j, i, k: (0, j)))
        args.append(bscale)
    return pl.pallas_call(
        body, name=name,
        grid=(N // tn, M // tm, nk),
        in_specs=in_specs,
        out_specs=pl.BlockSpec((tm, tn), lambda j, i, k: (i, j)),
        out_shape=jax.ShapeDtypeStruct((M, N), out_dtype),
        scratch_shapes=[pltpu.VMEM((tm, tn), F32)] if nk > 1 else [],
        compiler_params=_params("parallel", "parallel", "arbitrary"),
    )(*args)


def _sigmoid(x):
    return 1.0 / (1.0 + jnp.exp(-x))


def _mm_swiglu(h, w_in, *, name, tm=512, tn=1408):
    M, K = h.shape
    N = w_in.shape[1] // 2
    tm, tn = _tile(M, tm), _tile(N, tn)
    nj = N // tn

    def body(h_ref, wg_ref, wu_ref, g_ref, u_ref, a_ref):
        hv = h_ref[...]
        g = jnp.dot(hv, wg_ref[...], preferred_element_type=F32)
        u = jnp.dot(hv, wu_ref[...], preferred_element_type=F32)
        g_ref[...] = g.astype(BF16)
        u_ref[...] = u.astype(BF16)
        a_ref[...] = (g * _sigmoid(g) * u).astype(BF16)

    o_spec = pl.BlockSpec((tm, tn), lambda j, i: (i, j))
    sds = jax.ShapeDtypeStruct((M, N), BF16)
    return pl.pallas_call(
        body, name=name, grid=(nj, M // tm),
        in_specs=[pl.BlockSpec((tm, K), lambda j, i: (i, 0)), pl.BlockSpec((K, tn), lambda j, i: (0, j)),
                  pl.BlockSpec((K, tn), lambda j, i: (0, nj + j))],
        out_specs=[o_spec, o_spec, o_spec], out_shape=[sds, sds, sds],
        compiler_params=_params("parallel", "parallel"),
    )(h, w_in, w_in)


def _mm_swiglu_bwd(dy, w_out, gate, up, *, name, tm=512, tn=1408):
    M, K = dy.shape
    N = w_out.shape[0]
    tm, tn = _tile(M, tm), _tile(N, tn)

    def body(dy_ref, w_ref, g_ref, u_ref, dg_ref, du_ref):
        dact = lax.dot_general(dy_ref[...], w_ref[...], NT, preferred_element_type=F32)
        g = g_ref[...].astype(F32)
        u = u_ref[...].astype(F32)
        sig = _sigmoid(g)
        dg_ref[...] = (dact * u * (sig * (1.0 + g * (1.0 - sig)))).astype(BF16)
        du_ref[...] = (dact * (g * sig)).astype(BF16)

    t_spec = pl.BlockSpec((tm, tn), lambda j, i: (i, j))
    sds = jax.ShapeDtypeStruct((M, N), BF16)
    return pl.pallas_call(
        body, name=name, grid=(N // tn, M // tm),
        in_specs=[pl.BlockSpec((tm, K), lambda j, i: (i, 0)), pl.BlockSpec((tn, K), lambda j, i: (j, 0)),
                  t_spec, t_spec],
        out_specs=[t_spec, t_spec], out_shape=[sds, sds],
        compiler_params=_params("parallel", "parallel"),
    )(dy, w_out, gate, up)


ROW_TILE = 256


def _colsum8(v):
    return jnp.sum(v.reshape(v.shape[0] // SUBLANES, SUBLANES, v.shape[1]), axis=0)


def _finish_colsums(step, last, refs):
    @pl.when(step == last)
    def _():
        for r in refs:
            r[...] = jnp.broadcast_to(jnp.sum(r[...], axis=0, keepdims=True), r.shape)


def _row_spec(t, d):
    return pl.BlockSpec((t, d), lambda i: (i, 0))


def _vec_spec(d, rows=1):
    return pl.BlockSpec((rows, d), lambda i: (0, 0))


def _prenorm(x, g, sc, sh, *, name):
    L, D = x.shape
    t = _tile(L, ROW_TILE, SUBLANES)

    def body(x_ref, g_ref, sc_ref, sh_ref, h_ref):
        xv = x_ref[...]
        r = lax.rsqrt(jnp.mean(xv * xv, axis=-1, keepdims=True) + RMS_EPS)
        h_ref[...] = (((xv * r) * g_ref[...]) * (1.0 + sc_ref[...]) + sh_ref[...]).astype(BF16)

    return pl.pallas_call(
        body, name=name, grid=(L // t,),
        in_specs=[_row_spec(t, D), _vec_spec(D), _vec_spec(D), _vec_spec(D)],
        out_specs=_row_spec(t, D), out_shape=jax.ShapeDtypeStruct((L, D), BF16),
        compiler_params=_params("parallel"),
    )(x, g, sc, sh)


def _postnorm_res(x, y, g, gate, res_w, *, name):
    L, D = x.shape
    t = _tile(L, ROW_TILE, SUBLANES)

    def body(x_ref, y_ref, g_ref, gate_ref, o_ref):
        yv = y_ref[...]
        r = lax.rsqrt(jnp.mean(yv * yv, axis=-1, keepdims=True) + RMS_EPS)
        o_ref[...] = x_ref[...] + (res_w * gate_ref[...]) * ((yv * r) * g_ref[...])

    return pl.pallas_call(
        body, name=name, grid=(L // t,),
        in_specs=[_row_spec(t, D), _row_spec(t, D), _vec_spec(D), _vec_spec(D)],
        out_specs=_row_spec(t, D), out_shape=jax.ShapeDtypeStruct((L, D), F32),
        compiler_params=_params("parallel"),
    )(x, y, g, gate)


def _postnorm_bwd(dxo, y, g, gate, res_w, *, name):
    L, D = y.shape
    t = _tile(L, ROW_TILE, SUBLANES)
    n = L // t

    def body(dxo_ref, y_ref, g_ref, gate_ref, dy_ref, dgate_ref, dg_ref):
        i = pl.program_id(0)

        @pl.when(i == 0)
        def _():
            dgate_ref[...] = jnp.zeros_like(dgate_ref)
            dg_ref[...] = jnp.zeros_like(dg_ref)

        yv = y_ref[...]
        dv = dxo_ref[...]
        gv = g_ref[...]
        r = lax.rsqrt(jnp.mean(yv * yv, axis=-1, keepdims=True) + RMS_EPS)
        yn = yv * r
        dgate_ref[...] += _colsum8(dv * (res_w * (yn * gv)))
        do = dv * (res_w * gate_ref[...])
        dg_ref[...] += _colsum8(do * yn)
        dyn = do * gv
        dy_ref[...] = (r * (dyn - yn * jnp.mean(dyn * yn, axis=-1, keepdims=True))).astype(BF16)
        _finish_colsums(i, n - 1, (dgate_ref, dg_ref))

    sum_sds = jax.ShapeDtypeStruct((SUBLANES, D), F32)
    return pl.pallas_call(
        body, name=name, grid=(n,),
        in_specs=[_row_spec(t, D), _row_spec(t, D), _vec_spec(D), _vec_spec(D)],
        out_specs=[_row_spec(t, D), _vec_spec(D, SUBLANES), _vec_spec(D, SUBLANES)],
        out_shape=[jax.ShapeDtypeStruct((L, D), BF16), sum_sds, sum_sds],
        compiler_params=_params("arbitrary"),
    )(dxo, y, g, gate)


def _prenorm_bwd(x, dh, dxres, g, sc, *, name):
    L, D = x.shape
    t = _tile(L, ROW_TILE, SUBLANES)
    n = L // t

    def body(x_ref, dh_ref, dxr_ref, g_ref, sc_ref, dx_ref, dsh_ref, dsc_ref, dg_ref):
        i = pl.program_id(0)

        @pl.when(i == 0)
        def _():
            dsh_ref[...] = jnp.zeros_like(dsh_ref)
            dsc_ref[...] = jnp.zeros_like(dsc_ref)
            dg_ref[...] = jnp.zeros_like(dg_ref)

        xv = x_ref[...]
        dhv = dh_ref[...].astype(F32)
        gv = g_ref[...]
        one_sc = 1.0 + sc_ref[...]
        r = lax.rsqrt(jnp.mean(xv * xv, axis=-1, keepdims=True) + RMS_EPS)
        xn = xv * r
        tt = dhv * xn
        dsh_ref[...] += _colsum8(dhv)
        dsc_ref[...] += _colsum8(tt * gv)
        dg_ref[...] += _colsum8(tt * one_sc)
        dxn = dhv * (gv * one_sc)
        dx_ref[...] = dxr_ref[...] + r * (dxn - xn * jnp.mean(dxn * xn, axis=-1, keepdims=True))
        _finish_colsums(i, n - 1, (dsh_ref, dsc_ref, dg_ref))

    sum_sds = jax.ShapeDtypeStruct((SUBLANES, D), F32)
    sum_spec = _vec_spec(D, SUBLANES)
    return pl.pallas_call(
        body, name=name, grid=(n,),
        in_specs=[_row_spec(t, D), _row_spec(t, D), _row_spec(t, D), _vec_spec(D), _vec_spec(D)],
        out_specs=[_row_spec(t, D), sum_spec, sum_spec, sum_spec],
        out_shape=[jax.ShapeDtypeStruct((L, D), F32), sum_sds, sum_sds, sum_sds],
        compiler_params=_params("arbitrary"),
    )(x, dh, dxres, g, sc)


def _loss_head(y, target, *, name):
    L, D = y.shape
    t = _tile(L, ROW_TILE, SUBLANES)
    n = L // t

    def body(y_ref, t_ref, dy_ref, loss_ref):
        i = pl.program_id(0)

        @pl.when(i == 0)
        def _():
            loss_ref[...] = jnp.zeros_like(loss_ref)

        e = y_ref[...] - t_ref[...]
        dy_ref[...] = e * (1.0 / D)
        part = jnp.sum(jnp.mean(e * e, axis=-1, keepdims=True), axis=0, keepdims=True)
        loss_ref[...] += jnp.broadcast_to(0.5 * part, loss_ref.shape)

    return pl.pallas_call(
        body, name=name, grid=(n,),
        in_specs=[_row_spec(t, D), _row_spec(t, D)],
        out_specs=[_row_spec(t, D), pl.BlockSpec((SUBLANES, LANES), lambda i: (0, 0))],
        out_shape=[jax.ShapeDtypeStruct((L, D), F32), jax.ShapeDtypeStruct((SUBLANES, LANES), F32)],
        compiler_params=_params("arbitrary"),
    )(y, target)


GELU_C = math.sqrt(2.0 / math.pi)


def _gelu_fwd(y, *, name):
    L, W = y.shape
    t = _tile(L, 512, SUBLANES)

    def body(y_ref, o_ref):
        v = y_ref[...]
        o_ref[...] = (0.5 * v * (1.0 + jnp.tanh(GELU_C * (v + 0.044715 * (v * v * v))))).astype(BF16)

    return pl.pallas_call(
        body, name=name, grid=(L // t,), in_specs=[_row_spec(t, W)], out_specs=_row_spec(t, W),
        out_shape=jax.ShapeDtypeStruct((L, W), BF16), compiler_params=_params("parallel"),
    )(y)


def _gelu_bwd(dgl, y, u, dskip, *, name):
    L, W = y.shape
    t = _tile(L, 512, SUBLANES)
    n = L // t

    def body(dgl_ref, y_ref, u_ref, d_ref, dy_ref, sk_ref, dd_ref):
        i = pl.program_id(0)

        @pl.when(i == 0)
        def _():
            dd_ref[...] = jnp.zeros_like(dd_ref)

        v = y_ref[...]
        inner = GELU_C * (v + 0.044715 * (v * v * v))
        th = jnp.tanh(inner)
        dgelu = 0.5 * (1.0 + th) + 0.5 * v * (1.0 - th * th) * (GELU_C * (1.0 + 3.0 * 0.044715 * (v * v)))
        dy = dgl_ref[...] * dgelu
        dy_ref[...] = dy.astype(BF16)
        sk_ref[...] = dy * d_ref[...]
        dd_ref[...] += _colsum8(dy * u_ref[...])
        _finish_colsums(i, n - 1, (dd_ref,))

    return pl.pallas_call(
        body, name=name, grid=(n,),
        in_specs=[_row_spec(t, W), _row_spec(t, W), _row_spec(t, W), _vec_spec(W)],
        out_specs=[_row_spec(t, W), _row_spec(t, W), _vec_spec(W, SUBLANES)],
        out_shape=[jax.ShapeDtypeStruct((L, W), BF16), jax.ShapeDtypeStruct((L, W), F32),
                   jax.ShapeDtypeStruct((SUBLANES, W), F32)],
        compiler_params=_params("arbitrary"),
    )(dgl, y, u, dskip)


def _merge_fwd(z, yb, gab, *, name):
    L, D = yb.shape
    t = _tile(L, ROW_TILE, SUBLANES)

    def body(z_ref, yb_ref, gab_ref, o_ref):
        ya = z_ref[:, :D] * _sigmoid(z_ref[:, D:])
        o_ref[...] = (_sigmoid(gab_ref[:, :D]) * ya + _sigmoid(gab_ref[:, D:]) * yb_ref[...]).astype(BF16)

    return pl.pallas_call(
        body, name=name, grid=(L // t,),
        in_specs=[_row_spec(t, 2 * D), _row_spec(t, D), _row_spec(t, 2 * D)],
        out_specs=_row_spec(t, D), out_shape=jax.ShapeDtypeStruct((L, D), BF16),
        compiler_params=_params("parallel"),
    )(z, yb, gab)


def _merge_bwd(dm, z, yb, gab, *, name):
    L, D = yb.shape
    t = _tile(L, ROW_TILE, SUBLANES)

    def body(dm_ref, z_ref, yb_ref, gab_ref, dz_ref, dyb_ref, dgab_ref):
        dmv = dm_ref[...]
        zv = z_ref[:, :D]
        sz = _sigmoid(z_ref[:, D:])
        sa = _sigmoid(gab_ref[:, :D])
        sb = _sigmoid(gab_ref[:, D:])
        ybv = yb_ref[...]
        dya = dmv * sa
        dz_ref[:, :D] = (dya * sz).astype(BF16)
        dz_ref[:, D:] = (dya * zv * (sz * (1.0 - sz))).astype(BF16)
        dyb_ref[...] = (dmv * sb).astype(BF16)
        dgab_ref[:, :D] = (dmv * (zv * sz) * (sa * (1.0 - sa))).astype(BF16)
        dgab_ref[:, D:] = (dmv * ybv * (sb * (1.0 - sb))).astype(BF16)

    return pl.pallas_call(
        body, name=name, grid=(L // t,),
        in_specs=[_row_spec(t, D), _row_spec(t, 2 * D), _row_spec(t, D), _row_spec(t, 2 * D)],
        out_specs=[_row_spec(t, 2 * D), _row_spec(t, D), _row_spec(t, 2 * D)],
        out_shape=[jax.ShapeDtypeStruct((L, 2 * D), BF16), jax.ShapeDtypeStruct((L, D), BF16),
                   jax.ShapeDtypeStruct((L, 2 * D), BF16)],
        compiler_params=_params("parallel"),
    )(dm, z, yb, gab)


SCAN_W = 1024
SCAN_T = 512


def _interleave(x):
    L, W = x.shape
    seg = SCAN_T // SUBLANES
    return x.reshape(L // SCAN_T, SUBLANES, seg, W).transpose(0, 2, 1, 3).reshape(L, W)


def _deinterleave(x):
    L, W = x.shape
    seg = SCAN_T // SUBLANES
    return x.reshape(L // SCAN_T, seg, SUBLANES, W).transpose(0, 2, 1, 3).reshape(L, W)


def _power_table(a, b, pr_tab, pi_tab, n):
    def fill(k, carry):
        pr, pi = carry
        pr_tab[k] = pr
        pi_tab[k] = pi
        return a * pr - b * pi, a * pi + b * pr

    lax.fori_loop(0, n, fill, (a, b))


def _rows_to_tile(rows):
    w = rows[0].shape[1]
    sub = lax.broadcasted_iota(jnp.int32, (SUBLANES, w), 0)
    tile = jnp.broadcast_to(rows[0], (SUBLANES, w))
    for j in range(1, SUBLANES):
        tile = jnp.where(sub == j, jnp.broadcast_to(rows[j], (SUBLANES, w)), tile)
    return tile


def _ssm_scan_fwd(bu_re, bu_im, lam_re, lam_im, *, name):
    L, S = bu_re.shape
    w, t = _tile(S, SCAN_W), SCAN_T
    seg = t // SUBLANES

    def body(br_ref, bi_ref, lr_ref, li_ref, sr_ref, si_ref, pr_tab, pi_tab, cr_ref, ci_ref):
        a = jnp.broadcast_to(lr_ref[...], (SUBLANES, w))
        b = jnp.broadcast_to(li_ref[...], (SUBLANES, w))

        @pl.when(pl.program_id(1) == 0)
        def _():
            cr_ref[...] = jnp.zeros_like(cr_ref)
            ci_ref[...] = jnp.zeros_like(ci_ref)
            _power_table(a, b, pr_tab, pi_tab, seg)

        def local_scan(i, carry):
            sr, si = carry
            base = pl.multiple_of(i * SUBLANES, SUBLANES)
            nr = a * sr - b * si + br_ref[pl.ds(base, SUBLANES), :]
            ni = a * si + b * sr + bi_ref[pl.ds(base, SUBLANES), :]
            sr_ref[pl.ds(base, SUBLANES), :] = nr
            si_ref[pl.ds(base, SUBLANES), :] = ni
            return nr, ni

        zero = jnp.zeros((SUBLANES, w), F32)
        fr, fi = lax.fori_loop(0, seg, local_scan, (zero, zero), unroll=2)
        lsr, lsi = pr_tab[seg - 1][0:1, :], pi_tab[seg - 1][0:1, :]
        cr, ci = cr_ref[...], ci_ref[...]
        rows_r, rows_i = [], []
        for j in range(SUBLANES):
            rows_r.append(cr)
            rows_i.append(ci)
            cr, ci = fr[j:j + 1, :] + (lsr * cr - lsi * ci), fi[j:j + 1, :] + (lsr * ci + lsi * cr)
        cr_ref[...] = cr
        ci_ref[...] = ci
        in_r, in_i = _rows_to_tile(rows_r), _rows_to_tile(rows_i)

        def add_entry(i, _):
            base = pl.multiple_of(i * SUBLANES, SUBLANES)
            pr, pi = pr_tab[i], pi_tab[i]
            sr_ref[pl.ds(base, SUBLANES), :] += pr * in_r - pi * in_i
            si_ref[pl.ds(base, SUBLANES), :] += pr * in_i + pi * in_r
            return 0

        lax.fori_loop(0, seg, add_entry, 0, unroll=2)

    blk = pl.BlockSpec((t, w), lambda j, i: (i, j))
    vec = pl.BlockSpec((1, w), lambda j, i: (0, j))
    sds = jax.ShapeDtypeStruct((L, S), F32)
    tab = pltpu.VMEM((seg, SUBLANES, w), F32)
    return pl.pallas_call(
        body, name=name, grid=(S // w, L // t),
        in_specs=[blk, blk, vec, vec], out_specs=[blk, blk], out_shape=[sds, sds],
        scratch_shapes=[tab, tab, pltpu.VMEM((1, w), F32), pltpu.VMEM((1, w), F32)],
        compiler_params=_params("parallel", "arbitrary"),
    )(bu_re, bu_im, lam_re, lam_im)


def _ssm_scan_bwd(d_re, d_im, s_re, s_im, lam_re, lam_im, *, name):
    L, S = d_re.shape
    w, t = _tile(S, SCAN_W), SCAN_T
    nt = L // t
    seg = t // SUBLANES

    def body(dr_ref, di_ref, sr_ref, si_ref, lr_ref, li_ref, gr_ref, gi_ref, ar_ref, ai_ref,
             pr_tab, pi_tab, cgr, cgi, acc_r, acc_i):
        step = pl.program_id(1)
        a = jnp.broadcast_to(lr_ref[...], (SUBLANES, w))
        b = jnp.broadcast_to(-li_ref[...], (SUBLANES, w))

        @pl.when(step == 0)
        def _():
            for r in (cgr, cgi, acc_r, acc_i):
                r[...] = jnp.zeros_like(r)
            _power_table(a, b, pr_tab, pi_tab, seg)

        def local_scan(ii, carry):
            gr, gi = carry
            base = pl.multiple_of((seg - 1 - ii) * SUBLANES, SUBLANES)
            ngr = a * gr - b * gi + dr_ref[pl.ds(base, SUBLANES), :]
            ngi = a * gi + b * gr + di_ref[pl.ds(base, SUBLANES), :]
            gr_ref[pl.ds(base, SUBLANES), :] = ngr
            gi_ref[pl.ds(base, SUBLANES), :] = ngi
            return ngr, ngi

        zero = jnp.zeros((SUBLANES, w), F32)
        fr, fi = lax.fori_loop(0, seg, local_scan, (zero, zero), unroll=2)
        lsr, lsi = pr_tab[seg - 1][0:1, :], pi_tab[seg - 1][0:1, :]
        cr, ci = cgr[...], cgi[...]
        rows_r, rows_i = [None] * SUBLANES, [None] * SUBLANES
        for j in reversed(range(SUBLANES)):
            rows_r[j], rows_i[j] = cr, ci
            cr, ci = fr[j:j + 1, :] + (lsr * cr - lsi * ci), fi[j:j + 1, :] + (lsr * ci + lsi * cr)
        cgr[...] = cr
        cgi[...] = ci
        in_r, in_i = _rows_to_tile(rows_r), _rows_to_tile(rows_i)

        def add_entry(ii, carry):
            nr, ni, xr, xi = carry
            base = pl.multiple_of((seg - 1 - ii) * SUBLANES, SUBLANES)
            sr = sr_ref[pl.ds(base, SUBLANES), :]
            si = si_ref[pl.ds(base, SUBLANES), :]
            xr = xr + (nr * sr + ni * si)
            xi = xi + (ni * sr - nr * si)
            pr, pi = pr_tab[ii], pi_tab[ii]
            gr = gr_ref[pl.ds(base, SUBLANES), :] + (pr * in_r - pi * in_i)
            gi = gi_ref[pl.ds(base, SUBLANES), :] + (pr * in_i + pi * in_r)
            gr_ref[pl.ds(base, SUBLANES), :] = gr
            gi_ref[pl.ds(base, SUBLANES), :] = gi
            return gr, gi, xr, xi

        _, _, xr, xi = lax.fori_loop(0, seg, add_entry, (in_r, in_i, acc_r[...], acc_i[...]), unroll=2)
        acc_r[...] = xr
        acc_i[...] = xi

        @pl.when(step == nt - 1)
        def _():
            ar_ref[...] = jnp.sum(xr, axis=0, keepdims=True)
            ai_ref[...] = jnp.sum(xi, axis=0, keepdims=True)

    blk = pl.BlockSpec((t, w), lambda j, i: (nt - 1 - i, j))
    vec = pl.BlockSpec((1, w), lambda j, i: (0, j))
    sds = jax.ShapeDtypeStruct((L, S), F32)
    vsds = jax.ShapeDtypeStruct((1, S), F32)
    tab = pltpu.VMEM((seg, SUBLANES, w), F32)
    tile = pltpu.VMEM((SUBLANES, w), F32)
    return pl.pallas_call(
        body, name=name, grid=(S // w, nt),
        in_specs=[blk, blk, blk, blk, vec, vec], out_specs=[blk, blk, vec, vec],
        out_shape=[sds, sds, vsds, vsds],
        scratch_shapes=[tab, tab, pltpu.VMEM((1, w), F32), pltpu.VMEM((1, w), F32), tile, tile],
        compiler_params=_params("parallel", "arbitrary"),
    )(d_re, d_im, s_re, s_im, lam_re, lam_im)


def _ssm_discretize(a_re, a_im, log_dt, b_re, b_im, c_re, c_im, d_skip):
    G, P, N = SSM_GROUPS, SSM_STATE, SSM_GROUP
    a = jnp.minimum(a_re, -1e-4)
    dt = jnp.exp(log_dt)[:, None]
    mag = jnp.exp(a * dt)
    lr = mag * jnp.cos(a_im * dt)
    li = mag * jnp.sin(a_im * dt)
    den = a * a + a_im * a_im
    cr = ((lr - 1.0) * a + li * a_im) / den
    ci = (li * a - (lr - 1.0) * a_im) / den
    bbr = cr[..., None] * b_re - ci[..., None] * b_im
    bbi = cr[..., None] * b_im + ci[..., None] * b_re
    eye = jnp.eye(G, dtype=F32)

    def in_map(bb):
        return (eye[:, None, :, None] * bb.transpose(0, 2, 1)[:, :, None, :]).reshape(G * N, G * P)

    def out_map(c):
        return (eye[:, None, :, None] * c.transpose(0, 2, 1)[:, :, None, :]).reshape(G * P, G * N)

    return (lr.reshape(1, G * P), li.reshape(1, G * P), in_map(bbr), in_map(bbi),
            out_map(c_re), out_map(-c_im), d_skip.reshape(1, SSM_WIDTH))


ATT_T = 512
CUM_T = 256


def _split3(x):
    hi = x.astype(BF16)
    r1 = x - hi.astype(F32)
    mid = r1.astype(BF16)
    lo = (r1 - mid.astype(F32)).astype(BF16)
    return hi, mid, lo


def _tri_dot(tri, x):
    hi, mid, lo = _split3(x)
    dot = lambda p: jnp.dot(tri, p, preferred_element_type=F32)
    return dot(hi) + dot(mid) + dot(lo)


def _log_sigmoid(x):
    return jnp.minimum(x, 0.0) - jnp.log(1.0 + jnp.exp(-jnp.abs(x)))


def _fox_cum(f, fb, *, name):
    L, W = f.shape
    t = _tile(L, CUM_T, SUBLANES)

    def body(f_ref, b_ref, o_ref, carry):
        @pl.when(pl.program_id(0) == 0)
        def _():
            carry[...] = jnp.zeros_like(carry)

        row = lax.broadcasted_iota(jnp.int32, (t, t), 0)
        col = lax.broadcasted_iota(jnp.int32, (t, t), 1)
        tri = jnp.where(col <= row, 1.0, 0.0).astype(BF16)
        c = _tri_dot(tri, _log_sigmoid(f_ref[...] + b_ref[...])) + carry[...]
        o_ref[...] = c
        carry[...] = c[t - 1:t, :]

    return pl.pallas_call(
        body, name=name, grid=(L // t,),
        in_specs=[_row_spec(t, W), _vec_spec(W)], out_specs=_row_spec(t, W),
        out_shape=jax.ShapeDtypeStruct((L, W), F32),
        scratch_shapes=[pltpu.VMEM((1, W), F32)], compiler_params=_params("arbitrary"),
    )(f, fb)


def _fox_cum_bwd(dcum, f, fb, *, name):
    L, W = f.shape
    t = _tile(L, CUM_T, SUBLANES)
    n = L // t

    def body(d_ref, f_ref, b_ref, o_ref, db_ref, carry):
        i = pl.program_id(0)

        @pl.when(i == 0)
        def _():
            carry[...] = jnp.zeros_like(carry)
            db_ref[...] = jnp.zeros_like(db_ref)

        row = lax.broadcasted_iota(jnp.int32, (t, t), 0)
        col = lax.broadcasted_iota(jnp.int32, (t, t), 1)
        tri = jnp.where(col >= row, 1.0, 0.0).astype(BF16)
        dlog = _tri_dot(tri, d_ref[...]) + carry[...]
        carry[...] = dlog[0:1, :]
        df = dlog * _sigmoid(-(f_ref[...] + b_ref[...]))
        o_ref[...] = df.astype(BF16)
        db_ref[...] += _colsum8(df)
        _finish_colsums(i, n - 1, (db_ref,))

    rev = pl.BlockSpec((t, W), lambda i: (n - 1 - i, 0))
    return pl.pallas_call(
        body, name=name, grid=(n,),
        in_specs=[rev, rev, _vec_spec(W)], out_specs=[rev, _vec_spec(W, SUBLANES)],
        out_shape=[jax.ShapeDtypeStruct((L, W), BF16), jax.ShapeDtypeStruct((SUBLANES, W), F32)],
        scratch_shapes=[pltpu.VMEM((1, W), F32)], compiler_params=_params("arbitrary"),
    )(dcum, f, fb)


def _head_col(blk, h):
    lane = lax.broadcasted_iota(jnp.int32, blk.shape, 1)
    return jnp.sum(jnp.where(lane == h * HEAD_DIM, blk, 0.0), axis=1, keepdims=True)


def _lo_mask(rows):
    return lax.broadcasted_iota(jnp.int32, (rows, LANES), 1) < HEAD_DIM


def _causal(t):
    row = lax.broadcasted_iota(jnp.int32, (t, t), 0)
    col = lax.broadcasted_iota(jnp.int32, (t, t), 1)
    return col <= row


def _call_with_exchange(body, exchange, *, name, grid, in_specs, out_specs, out_shape, operands):
    sem = ("parallel",) + ("arbitrary",) * (len(grid) - 1)
    if exchange is None:
        return pl.pallas_call(body, name=name, grid=grid, in_specs=in_specs, out_specs=out_specs,
                              out_shape=out_shape, compiler_params=_params(*sem))(*operands)
    n_in, n_out = len(in_specs), len(out_specs)
    ei, eo = len(exchange.operands), len(exchange.out_shapes)

    def wrapped(*refs):
        ins, ex_in = refs[:n_in], refs[n_in:n_in + ei]
        outs, ex_out = refs[n_in + ei:n_in + ei + n_out], refs[n_in + ei + n_out:n_in + ei + n_out + eo]
        sems = refs[n_in + ei + n_out + eo:]
        ids = [pl.program_id(d) for d in range(len(grid))]
        first = functools.reduce(jnp.logical_and, [i == 0 for i in ids])
        last = functools.reduce(jnp.logical_and, [i == g - 1 for i, g in zip(ids, grid)])

        @pl.when(first)
        def _():
            exchange.start(ex_in, ex_out, sems)

        body(*ins, *outs)

        @pl.when(last)
        def _():
            exchange.finish(ex_in, ex_out, sems)

    return pl.pallas_call(
        wrapped, name=name, grid=grid, in_specs=list(in_specs) + [HBM_SPEC] * ei,
        out_specs=list(out_specs) + [HBM_SPEC] * eo, out_shape=list(out_shape) + list(exchange.out_shapes),
        scratch_shapes=exchange.sems, compiler_params=_params(*(("arbitrary",) * len(grid))),
    )(*operands, *exchange.operands)


def _fox_fwd(qkv, cum_cols, cum_rows, *, name, exchange=None):
    L = qkv.shape[0]
    t = _tile(L, ATT_T)
    nq = L // t
    npair = ATTN_HEADS // 2

    def body(q_ref, k_ref, v_ref, cc_ref, cr_ref, o_ref, o32_ref, lse_ref):
        iq = pl.program_id(1)
        lo = _lo_mask(t)
        qv = q_ref[...] * 0.125
        zq = jnp.zeros_like(qv)
        qh = (jnp.where(lo, qv, zq), jnp.where(lo, zq, qv))
        ccv = cc_ref[...]
        cq = (_head_col(ccv, 0), _head_col(ccv, 1))

        def step(ik, carry, masked):
            start = pl.multiple_of(ik * t, t)
            kb = k_ref[pl.ds(start, t), :]
            vb = v_ref[pl.ds(start, t), :]
            out = []
            for h in range(2):
                m, l, acc = carry[h]
                s = lax.dot_general(qh[h], kb, NT, preferred_element_type=F32)
                s = s + (cq[h] - cr_ref[h:h + 1, pl.ds(start, t)])
                if masked:
                    s = jnp.where(_causal(t), s, -jnp.inf)
                m_new = jnp.maximum(m, jnp.max(s, axis=1, keepdims=True))
                alpha = jnp.exp(m - m_new)
                p = jnp.exp(s - m_new)
                l = alpha * l + jnp.sum(p, axis=1, keepdims=True)
                acc = alpha * acc + jnp.dot(p.astype(BF16), vb, preferred_element_type=F32)
                out.append((m_new, l, acc))
            return tuple(out)

        init1 = (jnp.full((t, 1), -jnp.inf, F32), jnp.zeros((t, 1), F32), jnp.zeros((t, LANES), F32))
        carry = lax.fori_loop(0, iq, lambda ik, c: step(ik, c, False), (init1, init1))
        (m0, l0, a0), (m1, l1, a1) = step(iq, carry, True)
        out = jnp.where(lo, a0 / l0, a1 / l1)
        o_ref[...] = out.astype(BF16)
        o32_ref[...] = out
        lse_ref[...] = jnp.where(lo, m0 + jnp.log(l0), m1 + jnp.log(l1))

    blk = lambda off: pl.BlockSpec((t, LANES), lambda hp, iq: (iq, off + hp))
    whole = lambda off: pl.BlockSpec((L, LANES), lambda hp, iq: (0, off + hp))
    return _call_with_exchange(
        body, exchange, name=name, grid=(npair, nq),
        in_specs=[blk(0), whole(npair), whole(2 * npair), blk(0),
                  pl.BlockSpec((None, SUBLANES, L), lambda hp, iq: (hp, 0, 0))],
        out_specs=[blk(0), blk(0), blk(0)],
        out_shape=[jax.ShapeDtypeStruct((L, ATTN_WIDTH), BF16), jax.ShapeDtypeStruct((L, ATTN_WIDTH), F32),
                   jax.ShapeDtypeStruct((L, ATTN_WIDTH), F32)],
        operands=(qkv, qkv, qkv, cum_cols, cum_rows))


STAT_LSE, STAT_CUM, STAT_DELTA = 0, 2, 4


def _lane_col(blk, idx):
    lane = lax.broadcasted_iota(jnp.int32, blk.shape, 1)
    return jnp.sum(jnp.where(lane == idx, blk, 0.0), axis=1, keepdims=True)


def _fox_rowstats(do, o, lse, cum_cols, *, name):
    L = do.shape[0]
    t = _tile(L, ATT_T)

    def body(do_ref, o_ref, lse_ref, cc_ref, st_ref):
        lo = _lo_mask(t)
        dd = do_ref[...].astype(F32) * o_ref[...]
        lsev, ccv = lse_ref[...], cc_ref[...]
        cols = (_head_col(lsev, 0), _head_col(lsev, 1), _head_col(ccv, 0), _head_col(ccv, 1),
                jnp.sum(jnp.where(lo, dd, 0.0), axis=1, keepdims=True),
                jnp.sum(jnp.where(lo, 0.0, dd), axis=1, keepdims=True))
        lane = lax.broadcasted_iota(jnp.int32, (t, LANES), 1)
        out = jnp.zeros((t, LANES), F32)
        for i, col in enumerate(cols):
            out = jnp.where(lane == i, col, out)
        st_ref[...] = out

    blk = pl.BlockSpec((t, LANES), lambda hp, i: (i, hp))
    return pl.pallas_call(
        body, name=name, grid=(ATTN_HEADS // 2, L // t),
        in_specs=[blk, blk, blk, blk], out_specs=blk,
        out_shape=jax.ShapeDtypeStruct((L, ATTN_WIDTH), F32),
        compiler_params=_params("parallel", "parallel"),
    )(do, o, lse, cum_cols)


def _fox_bwd(qkv, do, stats, cum_rows, *, name, exchange=None):
    L = qkv.shape[0]
    t = _tile(L, ATT_T)
    nq = L // t
    npair = ATTN_HEADS // 2

    def body(q_ref, do_ref, st_ref, k_ref, v_ref, cr_ref, dk_ref, dv_ref, dc_ref, dq_ref, drow_ref):
        ik = pl.program_id(1)

        @pl.when(ik == 0)
        def _():
            dq_ref[...] = jnp.zeros_like(dq_ref)
            drow_ref[...] = jnp.zeros_like(drow_ref)

        lo = _lo_mask(t)
        lane = lax.broadcasted_iota(jnp.int32, (t, LANES), 1)
        kb = k_ref[...]
        vb = v_ref[...]
        zk = jnp.zeros_like(kb)
        kh = (jnp.where(lo, kb, zk), jnp.where(lo, zk, kb))
        vh = (jnp.where(lo, vb, zk), jnp.where(lo, zk, vb))
        ck = (cr_ref[0:1, :], cr_ref[1:2, :])

        def step(iq, carry, masked):
            dk, dv, dc0, dc1 = carry
            start = pl.multiple_of(iq * t, t)
            qb = q_ref[pl.ds(start, t), :] * 0.125
            dob = do_ref[pl.ds(start, t), :]
            stb = st_ref[pl.ds(start, t), :]
            dks, dvs, dcs, dqs, rss = [], [], [], [], []
            for h in range(2):
                s = lax.dot_general(qb, kh[h], NT, preferred_element_type=F32)
                s = s + (_lane_col(stb, STAT_CUM + h) - ck[h])
                if masked:
                    s = jnp.where(_causal(t), s, -jnp.inf)
                p = jnp.exp(s - _lane_col(stb, STAT_LSE + h))
                dp = lax.dot_general(dob, vh[h], NT, preferred_element_type=F32)
                ds = p * (dp - _lane_col(stb, STAT_DELTA + h))
                dsb = ds.astype(BF16)
                dvs.append(lax.dot_general(p.astype(BF16), dob, TN, preferred_element_type=F32))
                dks.append(lax.dot_general(dsb, qb, TN, preferred_element_type=F32))
                dqs.append(jnp.dot(dsb, kh[h], preferred_element_type=F32))
                dcs.append(jnp.sum(ds, axis=0, keepdims=True))
                rss.append(jnp.sum(ds, axis=1, keepdims=True))
            dq_ref[pl.ds(start, t), :] += 0.125 * (dqs[0] + dqs[1])
            drow_ref[pl.ds(start, t), :] += jnp.where(lane == 0, rss[0], jnp.where(lane == 1, rss[1], 0.0))
            return (dk + jnp.where(lo, dks[0], dks[1]), dv + jnp.where(lo, dvs[0], dvs[1]),
                    dc0 - dcs[0], dc1 - dcs[1])

        zero = jnp.zeros((t, LANES), F32)
        zrow = jnp.zeros((1, t), F32)
        carry = step(ik, (zero, zero, zrow, zrow), True)
        dk, dv, dc0, dc1 = lax.fori_loop(ik + 1, nq, lambda iq, c: step(iq, c, False), carry)
        dk_ref[...] = dk.astype(BF16)
        dv_ref[...] = dv.astype(BF16)
        dc_ref[...] = jnp.zeros_like(dc_ref)
        dc_ref[0:1, :] = dc0
        dc_ref[1:2, :] = dc1

    whole = lambda off: pl.BlockSpec((L, LANES), lambda hp, ik: (0, off + hp))
    blk = lambda off: pl.BlockSpec((t, LANES), lambda hp, ik: (ik, off + hp))
    rows = pl.BlockSpec((None, SUBLANES, t), lambda hp, ik: (hp, 0, ik))
    return _call_with_exchange(
        body, exchange, name=name, grid=(npair, nq),
        in_specs=[whole(0), whole(0), whole(0), blk(npair), blk(2 * npair), rows],
        out_specs=[blk(0), blk(0), rows, whole(0), whole(0)],
        out_shape=[jax.ShapeDtypeStruct((L, ATTN_WIDTH), BF16), jax.ShapeDtypeStruct((L, ATTN_WIDTH), BF16),
                   jax.ShapeDtypeStruct((npair, SUBLANES, L), F32),
                   jax.ShapeDtypeStruct((L, ATTN_WIDTH), F32), jax.ShapeDtypeStruct((L, ATTN_WIDTH), F32)],
        operands=(qkv, do, stats, qkv, qkv, cum_rows))


def _mod_partial(c_all, mod_w, mod_b_cols, *, name):
    depth, K, cols = mod_w.shape
    tn = _tile(cols, 768)

    def body(c_ref, w_ref, b_ref, o_ref):
        cv = c_ref[...]
        sc = (cv * _sigmoid(cv)).astype(BF16)
        o_ref[...] = jnp.dot(sc, w_ref[...].astype(BF16), preferred_element_type=F32) + b_ref[...]

    return pl.pallas_call(
        body, name=name, grid=(depth, cols // tn),
        in_specs=[pl.BlockSpec((N_DEV, K), lambda l, j: (0, 0)),
                  pl.BlockSpec((None, K, tn), lambda l, j: (l, 0, j)),
                  pl.BlockSpec((None, 1, tn), lambda l, j: (l, 0, j))],
        out_specs=pl.BlockSpec((None, N_DEV, tn), lambda l, j: (l, 0, j)),
        out_shape=jax.ShapeDtypeStruct((depth, N_DEV, cols), F32),
        compiler_params=_params("parallel", "parallel"),
    )(c_all, mod_w, mod_b_cols)


def _mod_wgrad(c_all_t, dmod, *, name):
    depth, nb, cols = dmod.shape
    K = c_all_t.shape[0]
    tn = _tile(cols, 768)
    tk = _tile(K, 256, SUBLANES)

    def body(c_ref, d_ref, o_ref):
        cv = c_ref[...]
        sc = cv * _sigmoid(cv)
        dv = d_ref[...]
        acc = sc[:, 0:1] * dv[0:1, :]
        for b in range(1, nb):
            acc = acc + sc[:, b:b + 1] * dv[b:b + 1, :]
        o_ref[...] = acc

    return pl.pallas_call(
        body, name=name, grid=(depth, K // tk, cols // tn),
        in_specs=[pl.BlockSpec((tk, nb), lambda l, i, j: (i, 0)),
                  pl.BlockSpec((None, nb, tn), lambda l, i, j: (l, 0, j))],
        out_specs=pl.BlockSpec((None, tk, tn), lambda l, i, j: (l, i, j)),
        out_shape=jax.ShapeDtypeStruct((depth, K, cols), F32),
        compiler_params=_params("parallel", "parallel", "parallel"),
    )(c_all_t, dmod)


def _adamw(w, g, m, v, *, name):
    shape = w.shape
    cols = shape[-1]
    rows = int(np.prod(shape[:-1]))
    t = _tile(rows, 256, SUBLANES) if rows % SUBLANES == 0 else rows
    r2 = lambda a: a.reshape(rows, cols)

    def body(w_ref, g_ref, m_ref, v_ref, d_ref, nm_ref, nv_ref):
        gv = g_ref[...]
        nm = ADAM_B1 * m_ref[...] + (1.0 - ADAM_B1) * gv
        nv = ADAM_B2 * v_ref[...] + (1.0 - ADAM_B2) * (gv * gv)
        m_hat = nm / (1.0 - ADAM_B1 ** ADAM_STEP)
        v_hat = nv / (1.0 - ADAM_B2 ** ADAM_STEP)
        d_ref[...] = -ADAM_LR * (m_hat / (jnp.sqrt(v_hat) + ADAM_EPS) + ADAM_WD * w_ref[...])
        nm_ref[...] = nm
        nv_ref[...] = nv

    spec = pl.BlockSpec((t, cols), lambda i: (i, 0))
    sds = jax.ShapeDtypeStruct((rows, cols), F32)
    d, nm, nv = pl.pallas_call(
        body, name=name, grid=(rows // t,),
        in_specs=[spec] * 4, out_specs=[spec] * 3, out_shape=[sds] * 3,
        compiler_params=_params("parallel"),
    )(r2(w), r2(g), r2(m), r2(v))
    return d.reshape(shape), nm.reshape(shape), nv.reshape(shape)


def _my_place():
    return lax.axis_index("x"), lax.axis_index("y"), lax.axis_index("c")


def _other_chips(x, y):
    return [(1 - x, y), (x, 1 - y), (1 - x, 1 - y)]


def _all_gather8(v, *, name, with_sum=False):
    m, n = v.shape

    def body(x_ref, out_ref, *rest):
        if with_sum:
            sum_ref, send_sems, recv_sems, local_sem = rest
        else:
            send_sems, recv_sems, local_sem = rest
        x, y, c = _my_place()
        me, sibling = (x, y, c), (x, y, 1 - c)
        chips = _other_chips(x, y)

        def rows(px, py, pc):
            return out_ref.at[pl.ds((4 * px + 2 * py + pc) * m, m), :]

        def copy(k, block, to, src=None):
            return pltpu.make_async_remote_copy(
                src_ref=rows(*block) if src is None else src, dst_ref=rows(*block),
                send_sem=send_sems.at[k], recv_sem=recv_sems.at[k], device_id=to, device_id_type=MESH)

        mine = pltpu.make_async_copy(x_ref, rows(*me), local_sem)
        mine.start()
        first = [copy(0, me, sibling, src=x_ref)]
        first += [copy(1 + j, me, (*chip, c), src=x_ref) for j, chip in enumerate(chips)]
        for cp in first:
            cp.start()
        passed = [copy(4 + j, (*chip, c), sibling) for j, chip in enumerate(chips)]
        for j, chip in enumerate(chips):
            copy(1 + j, (*chip, c), me).wait_recv()
            passed[j].start()
        copy(0, sibling, me).wait_recv()
        for j, chip in enumerate(chips):
            copy(4 + j, (*chip, 1 - c), me).wait_recv()
        for cp in first + passed:
            cp.wait_send()
        mine.wait()
        if with_sum:
            acc = out_ref[pl.ds(0, m), :]
            for d in range(1, N_DEV):
                acc = acc + out_ref[pl.ds(d * m, m), :]
            sum_ref[...] = acc

    vm = pl.BlockSpec(memory_space=pltpu.VMEM)
    out_shape = [jax.ShapeDtypeStruct((N_DEV * m, n), F32)]
    if with_sum:
        out_shape.append(jax.ShapeDtypeStruct((m, n), F32))
    res = pl.pallas_call(
        body, name=name, out_shape=out_shape, in_specs=[vm], out_specs=[vm] * len(out_shape),
        scratch_shapes=[pltpu.SemaphoreType.DMA((7,)), pltpu.SemaphoreType.DMA((7,)), pltpu.SemaphoreType.DMA],
        compiler_params=pltpu.CompilerParams(vmem_limit_bytes=VMEM_LIMIT),
    )(v)
    return res if with_sum else res[0]


class _Cut(NamedTuple):
    shape: tuple
    slab: int
    half: int


IN_WIDTH = SSM_WIDTH + 3 * ATTN_WIDTH + ATTN_HEADS + 2 * D_MODEL
CUTS = dict(
    ffn_w_in=_Cut((1, D_MODEL, 2 * D_FF), 2, 1),
    ffn_w_out=_Cut((1, D_FF, D_MODEL), 1, 2),
    mix_w_in=_Cut((N_CHIPS, D_MODEL, IN_WIDTH // N_CHIPS), 0, 1),
    glu_w=_Cut((1, SSM_WIDTH, 2 * D_MODEL), 2, 1),
    attn_w_out=_Cut((1, ATTN_WIDTH, D_MODEL), 2, 1),
    mix_w_out=_Cut((1, D_MODEL, D_MODEL), 1, 2),
)
LAYER_MATS = (("ffn_w_in", 0), ("ffn_w_in", 1), ("ffn_w_out", 0), ("ffn_w_out", 1), ("mix_w_in", None),
              ("glu_w", None), ("attn_w_out", None), ("mix_w_out", None))
EARLY_MATS = (0, 2, 4, 5, 6, 7)
LATE_MATS = (1, 3)


def _part_shape(cut, slab=False, half=False):
    s = list(cut.shape)
    if slab:
        s[cut.slab] //= N_CHIPS
    if half:
        s[cut.half] //= 2
    return tuple(s)


def _window(ref, cut, slab=None, half=None):
    idx = [slice(None)] * len(cut.shape)
    for axis, parts, which in ((cut.slab, N_CHIPS, slab), (cut.half, 2, half)):
        if which is not None:
            width = cut.shape[axis] // parts
            idx[axis] = pl.ds(pl.multiple_of(which * width, width), width)
    return ref.at[tuple(idx)]


HBM_SPEC = pl.BlockSpec(memory_space=pltpu.HBM)


class _Exchange(NamedTuple):
    operands: list
    out_shapes: list
    sems: list
    start: object
    finish: object


def _run_exchange(ex, *, name):
    ni, no = len(ex.operands), len(ex.out_shapes)

    def body(*refs):
        parts = (refs[:ni], refs[ni:ni + no], refs[ni + no:])
        ex.start(*parts)
        ex.finish(*parts)

    return pl.pallas_call(
        body, name=name, out_shape=ex.out_shapes, in_specs=[HBM_SPEC] * ni, out_specs=[HBM_SPEC] * no,
        scratch_shapes=ex.sems,
    )(*ex.operands)


def _gather_exchange(shards, cuts):
    n = len(shards)

    def setup(s_refs, f_refs, sems):
        send_sems, recv_sems = sems
        x, y, c = _my_place()
        me, sibling, mine = (x, y, c), (x, y, 1 - c), 2 * x + y
        chips = _other_chips(x, y)

        def copy(i, k, src, dst, to):
            return pltpu.make_async_remote_copy(
                src_ref=src, dst_ref=dst, send_sem=send_sems.at[7 * i + k], recv_sem=recv_sems.at[7 * i + k],
                device_id=to, device_id_type=MESH)

        def landed(i, j, half):
            return _window(f_refs[i], cuts[i], slab=2 * chips[j][0] + chips[j][1], half=half)

        def sends():
            own = [copy(i, 6, s_refs[i], _window(f_refs[i], cuts[i], slab=mine), sibling) for i in range(n)]
            return own + [copy(i, j, _window(s_refs[i], cuts[i], half=c),
                               _window(f_refs[i], cuts[i], slab=mine, half=c), (*chips[j], c))
                          for i in range(n) for j in range(3)]

        return c, me, sibling, copy, landed, sends

    def start(s_refs, f_refs, sems):
        for cp in setup(s_refs, f_refs, sems)[-1]():
            cp.start()

    def finish(s_refs, f_refs, sems):
        c, me, sibling, copy, landed, sends = setup(s_refs, f_refs, sems)
        passed = []
        for i in range(n):
            for j in range(3):
                copy(i, j, landed(i, j, c), landed(i, j, c), me).wait_recv()
                passed.append(copy(i, 3 + j, landed(i, j, c), landed(i, j, c), sibling))
                passed[-1].start()
        for i in range(n):
            for j in range(3):
                copy(i, 3 + j, landed(i, j, 1 - c), landed(i, j, 1 - c), me).wait_recv()
        for i in range(n):
            mine_i = _window(f_refs[i], cuts[i], slab=2 * me[0] + me[1])
            copy(i, 6, mine_i, mine_i, me).wait_recv()
        for cp in sends() + passed:
            cp.wait_send()

    return _Exchange(
        list(shards), [jax.ShapeDtypeStruct(cut.shape, s.dtype) for s, cut in zip(shards, cuts)],
        [pltpu.SemaphoreType.DMA((7 * n,)), pltpu.SemaphoreType.DMA((7 * n,))], start, finish)


def _swap_layer(mats, cuts, *, name):
    n = len(mats)

    def body(*refs):
        m_refs, r_refs = refs[:n], refs[n:2 * n]
        send_sems, recv_sems = refs[2 * n:]
        x, y, c = _my_place()
        cps = [pltpu.make_async_remote_copy(
            src_ref=_window(m_refs[i], cuts[i], half=1 - c), dst_ref=r_refs[i], send_sem=send_sems.at[i],
            recv_sem=recv_sems.at[i], device_id=(x, y, 1 - c), device_id_type=MESH) for i in range(n)]
        for cp in cps:
            cp.start()
        for cp in cps:
            cp.wait()

    return pl.pallas_call(
        body, name=name, out_shape=[jax.ShapeDtypeStruct(_part_shape(cut, half=True), m.dtype)
                                    for m, cut in zip(mats, cuts)],
        in_specs=[HBM_SPEC] * n, out_specs=[HBM_SPEC] * n,
        scratch_shapes=[pltpu.SemaphoreType.DMA((n,)), pltpu.SemaphoreType.DMA((n,))],
    )(*mats)


def _partials_exchange(sums, cuts):
    n = len(sums)

    def copies(s_refs, p_refs, sems):
        send_sems, recv_sems = sems
        x, y, c = _my_place()
        return [pltpu.make_async_remote_copy(
            src_ref=_window(s_refs[i], cuts[i], slab=2 * chip[0] + chip[1]), dst_ref=p_refs[i].at[j],
            send_sem=send_sems.at[3 * i + j], recv_sem=recv_sems.at[3 * i + j],
            device_id=(*chip, c), device_id_type=MESH)
            for i in range(n) for j, chip in enumerate(_other_chips(x, y))]

    def start(s_refs, p_refs, sems):
        for cp in copies(s_refs, p_refs, sems):
            cp.start()

    def finish(s_refs, p_refs, sems):
        for cp in copies(s_refs, p_refs, sems):
            cp.wait()

    return _Exchange(
        list(sums), [jax.ShapeDtypeStruct((3,) + _part_shape(cut, slab=True, half=True), s.dtype)
                     for s, cut in zip(sums, cuts)],
        [pltpu.SemaphoreType.DMA((3 * n,)), pltpu.SemaphoreType.DMA((3 * n,))], start, finish)


def _share_all(dests, cuts, places, *, name):
    names = list(dests)
    nn, n = len(names), len(places)

    def body(*refs):
        o_refs = dict(zip(names, refs[nn:2 * nn]))
        send_sems, recv_sems = refs[2 * nn:]
        x, y, c = _my_place()

        def win(i, half):
            slab_cut = _Cut(_part_shape(cuts[i], slab=True), cuts[i].slab, cuts[i].half)
            return _window(o_refs[places[i][0]].at[places[i][1]], slab_cut, half=half)

        def copy(i, half):
            return pltpu.make_async_remote_copy(
                src_ref=win(i, half), dst_ref=win(i, half), send_sem=send_sems.at[i], recv_sem=recv_sems.at[i],
                device_id=(x, y, 1 - c), device_id_type=MESH)

        for i in range(n):
            copy(i, c).start()
        for i in range(n):
            copy(i, c).wait_send()
            copy(i, 1 - c).wait_recv()

    return pl.pallas_call(
        body, name=name, out_shape=[jax.ShapeDtypeStruct(dests[k].shape, F32) for k in names],
        in_specs=[HBM_SPEC] * nn, out_specs=[HBM_SPEC] * nn,
        input_output_aliases={i: i for i in range(nn)},
        scratch_shapes=[pltpu.SemaphoreType.DMA((n,)), pltpu.SemaphoreType.DMA((n,))],
    )(*[dests[k] for k in names])


def _cut_blocks(shape):
    _, R, C = shape
    tr = _tile(R, 256, 16)
    tc = _tile(C, 2048) if C % LANES == 0 else C
    return (None, tr, tc), (shape[0], R // tr, C // tc)


def _offset_map(axis, blocks):
    def index_map(b, i, j, which):
        idx = [b, i, j]
        idx[axis] = which[0] * blocks[axis] + idx[axis]
        return tuple(idx)
    return index_map


def _add_half(mat, other, cut, c_idx, *, name):
    shape = _part_shape(cut, half=True)
    block, grid = _cut_blocks(shape)

    def body(c_ref, m_ref, o_ref, f_ref, b_ref):
        s = m_ref[...] + o_ref[...]
        f_ref[...] = s
        b_ref[...] = s.astype(BF16)

    plain = pl.BlockSpec(block, lambda b, i, j, which: (b, i, j))
    grid_spec = pltpu.PrefetchScalarGridSpec(
        num_scalar_prefetch=1, grid=grid,
        in_specs=[pl.BlockSpec(block, _offset_map(cut.half, grid)), plain], out_specs=[plain, plain])
    return pl.pallas_call(
        body, name=name, grid_spec=grid_spec,
        out_shape=[jax.ShapeDtypeStruct(shape, F32), jax.ShapeDtypeStruct(shape, BF16)],
        compiler_params=_params("parallel", "parallel", "parallel"),
    )(c_idx, mat, other)


def _sum_slab(own, parts, cut, dest, place, chip_idx, c_idx, *, name):
    shape = _part_shape(cut, slab=True, half=True)
    block, grid = _cut_blocks(shape)
    assert shape[0] == 1

    def body(k_ref, c_ref, o_ref, p_ref, dest_ref, out_ref):
        acc = o_ref[...]
        for j in range(3):
            acc = acc + p_ref[j].astype(F32)
        out_ref[...] = acc

    def own_map(b, i, j, chip, core):
        idx = [b, i, j]
        idx[cut.slab] = chip[0] * grid[cut.slab] + idx[cut.slab]
        return tuple(idx)

    def dest_map(b, i, j, chip, core):
        idx = [b, i, j]
        idx[cut.half] = core[0] * grid[cut.half] + idx[cut.half]
        return tuple(place) + tuple(idx)

    grid_spec = pltpu.PrefetchScalarGridSpec(
        num_scalar_prefetch=2, grid=grid,
        in_specs=[pl.BlockSpec(block, own_map),
                  pl.BlockSpec((3,) + block[1:], lambda b, i, j, chip, core: (0, i, j)),
                  pl.BlockSpec(memory_space=pl.ANY)],
        out_specs=pl.BlockSpec((None,) * len(place) + block, dest_map))
    return pl.pallas_call(
        body, name=name, grid_spec=grid_spec, out_shape=jax.ShapeDtypeStruct(dest.shape, F32),
        input_output_aliases={4: 0},
        compiler_params=_params("parallel", "parallel", "parallel"),
    )(chip_idx, c_idx, own, parts.reshape((3,) + shape[1:]), dest)


def _pad_rows(flat, cols=8 * LANES, align=SUBLANES):
    n = flat.shape[0]
    rows = -(-n // (cols * align)) * align
    return jnp.pad(flat, (0, rows * cols - n)).reshape(rows, cols)


def _row(v):
    return v.reshape(1, -1)


def _ffn_fwd(x, mod, g_pre, g_post, w_in, w_out, tag):
    sh, sc, gate = _row(mod[0]), _row(mod[1]), _row(mod[2])
    h = _prenorm(x, _row(g_pre), sc, sh, name=f"prenorm_{tag}")
    gt, up, act = _mm_swiglu(h, w_in, name=f"swiglu_{tag}")
    y = _mm(act, w_out, name=f"ffn_out_{tag}", tk=D_FF)
    x_out = _postnorm_res(x, y, _row(g_post), gate, FFN_RES, name=f"postnorm_{tag}")
    return x_out, (x, h, gt, up, act, y)


def _ffn_bwd(dxo, saved, mod, g_pre, g_post, w_in, w_out, tag):
    x, h, gt, up, act, y = saved
    sc, gate = _row(mod[1]), _row(mod[2])
    dy, dgate, dgpost = _postnorm_bwd(dxo, y, _row(g_post), gate, FFN_RES, name=f"postnorm_bwd_{tag}")
    dgt, dup = _mm_swiglu_bwd(dy, w_out, gt, up, name=f"swiglu_bwd_{tag}")
    dw_out = _mm(act, dy, ta=True, name=f"dw_out_{tag}", tm=1408, tn=1024, tk=1024)
    dh = _mm(dgt, w_in, tb=True, name=f"dh_gate_{tag}", tk=D_FF)
    dh = _mm(dup, w_in, tb=True, b_k0=D_FF, bias=dh, name=f"dh_up_{tag}", tk=D_FF)
    dwg = _mm(h, dgt, ta=True, name=f"dw_gate_{tag}", tm=1024, tn=1408, tk=1024)
    dwu = _mm(h, dup, ta=True, name=f"dw_up_{tag}", tm=1024, tn=1408, tk=1024)
    dx, dsh, dsc, dgpre = _prenorm_bwd(x, dh, dxo, _row(g_pre), sc, name=f"prenorm_bwd_{tag}")
    dmod = jnp.stack([dsh[0], dsc[0], dgate[0]])
    return dx, dmod, dgpre[0], dgpost[0], jnp.concatenate([dwg, dwu], axis=1), dw_out


def _split_mix_w_in(w):
    u0, q0, f0, g0 = 0, SSM_WIDTH, SSM_WIDTH + 3 * ATTN_WIDTH, SSM_WIDTH + 3 * ATTN_WIDTH + ATTN_HEADS
    w_f = jnp.pad(w[:, f0:g0], ((0, 0), (0, LANES - ATTN_HEADS)))
    return w[:, u0:q0], w[:, q0:f0], w_f, w[:, g0:]


def _mixer_fwd(x, mod, g_pre, g_post, w, ssm, forget_b, tag, exchange=None):
    L = x.shape[0]
    sh, sc, gate = _row(mod[0]), _row(mod[1]), _row(mod[2])
    lam_re, lam_im, bin_re, bin_im, cout_re, cout_im, dskip = ssm
    h = _prenorm(x, _row(g_pre), sc, sh, name=f"prenorm_{tag}")
    u = _mm(h, w["u"], name=f"proj_u_{tag}")
    qkv = _mm(h, w["qkv"], out_dtype=BF16, name=f"proj_qkv_{tag}")
    f = _mm(h, w["f"], name=f"proj_f_{tag}")
    gab = _mm(h, w["gab"], name=f"proj_gab_{tag}")
    u_i = _interleave(u)
    bu_re = _mm(u_i, bin_re, name=f"ssm_bu_re_{tag}")
    bu_im = _mm(u_i, bin_im, name=f"ssm_bu_im_{tag}")
    s_re, s_im = _ssm_scan_fwd(bu_re, bu_im, lam_re, lam_im, name=f"ssm_scan_{tag}")
    y_ssm = _mm(s_re, cout_re, bias=u_i, bscale=dskip, name=f"ssm_y_re_{tag}", tk=2048)
    y_ssm = _deinterleave(_mm(s_im, cout_im, bias=y_ssm, name=f"ssm_y_im_{tag}", tk=2048))
    gl = _gelu_fwd(y_ssm, name=f"gelu_{tag}")
    z = _mm(gl, w["glu"], name=f"glu_{tag}")
    fb = jnp.pad(forget_b, (0, LANES - ATTN_HEADS)).reshape(1, LANES)
    cum = _fox_cum(f, fb, name=f"fox_cum_{tag}")
    cum8 = cum[:, :ATTN_HEADS]
    cum_cols = jnp.repeat(cum8, HEAD_DIM, axis=1)
    cum_rows = jnp.pad(cum8.T.reshape(ATTN_HEADS // 2, 2, L), ((0, 0), (0, SUBLANES - 2), (0, 0)))
    attn, attn32, lse, *exchanged = _fox_fwd(qkv, cum_cols, cum_rows, name=f"fox_fwd_{tag}", exchange=exchange)
    yb = _mm(attn, w["attn_out"], name=f"attn_out_{tag}")
    merged = _merge_fwd(z, yb, gab, name=f"merge_{tag}")
    y = _mm(merged, w["out"], name=f"mix_out_{tag}")
    x_out = _postnorm_res(x, y, _row(g_post), gate, 1.0, name=f"postnorm_{tag}")
    saved = (x, h, u, u_i, qkv, f, gab, s_re, s_im, y_ssm, gl, z, fb, cum_cols, cum_rows, attn, attn32, lse, yb,
             merged, y)
    return x_out, saved, exchanged


def _mixer_bwd(dxo, saved, mod, g_pre, g_post, w, ssm, tag, exchange=None):
    (x, h, u, u_i, qkv, f, gab, s_re, s_im, y_ssm, gl, z, fb, cum_cols, cum_rows, attn, attn32, lse, yb,
     merged, y) = saved
    L = x.shape[0]
    sc, gate = _row(mod[1]), _row(mod[2])
    lam_re, lam_im, bin_re, bin_im, cout_re, cout_im, dskip = ssm
    dy, dgate, dgpost = _postnorm_bwd(dxo, y, _row(g_post), gate, 1.0, name=f"postnorm_bwd_{tag}")
    dmerged = _mm(dy, w["out"], tb=True, name=f"dmerged_{tag}")
    dw_out = _mm(merged, dy, ta=True, name=f"dw_mix_out_{tag}", tm=1024, tn=1024)
    dz, dyb, dgab = _merge_bwd(dmerged, z, yb, gab, name=f"merge_bwd_{tag}")
    dgl = _mm(dz, w["glu"], tb=True, name=f"dgl_{tag}", tk=2048)
    dw_glu = _mm(gl, dz, ta=True, name=f"dw_glu_{tag}", tm=512, tn=2048)
    dys, dsk, dd = _gelu_bwd(dgl, y_ssm, u, dskip, name=f"gelu_bwd_{tag}")
    dys, dsk = _interleave(dys), _interleave(dsk)
    d_re = _mm(dys, cout_re, tb=True, name=f"ssm_ds_re_{tag}")
    d_im = _mm(dys, cout_im, tb=True, name=f"ssm_ds_im_{tag}")
    dcout_re = _mm(s_re, dys, ta=True, name=f"ssm_dc_re_{tag}", tm=1024, tn=512)
    dcout_im = _mm(s_im, dys, ta=True, name=f"ssm_dc_im_{tag}", tm=1024, tn=512)
    g_re, g_im, dlam_re, dlam_im = _ssm_scan_bwd(d_re, d_im, s_re, s_im, lam_re, lam_im, name=f"ssm_scan_bwd_{tag}")
    du = _mm(g_re, bin_re, tb=True, bias=dsk, name=f"ssm_du_re_{tag}", tk=2048)
    du = _mm(g_im, bin_im, tb=True, bias=du, out_dtype=BF16, name=f"ssm_du_im_{tag}", tk=2048)
    du = _deinterleave(du)
    dbin_re = _mm(u_i, g_re, ta=True, name=f"ssm_db_re_{tag}", tm=512, tn=1024)
    dbin_im = _mm(u_i, g_im, ta=True, name=f"ssm_db_im_{tag}", tm=512, tn=1024)
    dssm = (dlam_re, dlam_im, dbin_re, dbin_im, dcout_re, dcout_im, _row(dd[0]))
    dattn = _mm(dyb, w["attn_out"], tb=True, out_dtype=BF16, name=f"dattn_{tag}")
    dw_attn = _mm(attn, dyb, ta=True, name=f"dw_attn_out_{tag}", tm=512, tn=1024)
    stats = _fox_rowstats(dattn, attn32, lse, cum_cols, name=f"fox_rowstats_{tag}")
    dk, dv, dcum_rows, dq, drow, *exchanged = _fox_bwd(qkv, dattn, stats, cum_rows, name=f"fox_bwd_{tag}",
                                                       exchange=exchange)
    drow8 = drow.reshape(L, ATTN_HEADS // 2, LANES)[:, :, :2].reshape(L, ATTN_HEADS)
    dcum = drow8 + dcum_rows[:, :2, :].reshape(ATTN_HEADS, L).T
    dcum = jnp.pad(dcum, ((0, 0), (0, LANES - ATTN_HEADS)))
    df, dfb = _fox_cum_bwd(dcum, f, fb, name=f"fox_cum_bwd_{tag}")
    dqkv = jnp.concatenate([dq.astype(BF16), dk, dv], axis=1)
    dh = _mm(dqkv, w["qkv"], tb=True, name=f"dh_qkv_{tag}", tk=1536)
    dh = _mm(du, w["u"], tb=True, bias=dh, name=f"dh_u_{tag}")
    dh = _mm(dgab, w["gab"], tb=True, bias=dh, name=f"dh_gab_{tag}", tk=2048)
    dh = _mm(df, w["f"], tb=True, bias=dh, name=f"dh_f_{tag}")
    dw_u = _mm(h, du, ta=True, name=f"dw_u_{tag}", tm=1024, tn=512)
    dw_qkv = _mm(h, dqkv, ta=True, name=f"dw_qkv_{tag}", tm=1024, tn=1536)
    dw_f = _mm(h, df, ta=True, name=f"dw_f_{tag}", tm=1024)
    dw_gab = _mm(h, dgab, ta=True, name=f"dw_gab_{tag}", tm=1024, tn=1024)
    dw_in = jnp.concatenate([dw_u, dw_qkv, dw_f[:, :ATTN_HEADS], dw_gab], axis=1)
    dx, dsh, dsc, dgpre = _prenorm_bwd(x, dh, dxo, _row(g_pre), sc, name=f"prenorm_bwd_{tag}")
    dmod = jnp.stack([dsh[0], dsc[0], dgate[0]])
    grads = dict(mix_w_in=dw_in, glu_w=dw_glu, attn_w_out=dw_attn, mix_w_out=dw_out,
                 forget_b=dfb[0, :ATTN_HEADS])
    return dx, dmod, dgpre[0], dgpost[0], grads, dssm, exchanged


SSM_NAMES = ("ssm_a_re", "ssm_a_im", "ssm_log_dt", "ssm_b_re", "ssm_b_im", "ssm_c_re", "ssm_c_im", "ssm_d")
SMALL_NAMES = ("forget_b",) + SSM_NAMES
WEIGHT_NAMES = ("mod_w", "mod_b", "norm_pre", "norm_post", "ffn_w_in", "ffn_w_out", "mix_w_in", "forget_b") \
    + SSM_NAMES + ("glu_w", "attn_w_out", "mix_w_out")


def _layer_shards(w, l):
    return [(w[n][l] if j is None else w[n][l, j]).astype(BF16).reshape(_part_shape(CUTS[n], slab=True))
            for n, j in LAYER_MATS]


def _train_step(x, c, target, w, m, v):
    xi, yi, ci = _my_place()
    chip = 2 * xi + yi
    dev = 4 * xi + 2 * yi + ci
    mod_cols = N_SUB * 3 * D_MODEL // N_CHIPS
    norm_cols = D_MODEL // N_CHIPS

    cuts = [CUTS[n] for n, _ in LAYER_MATS]
    shards = [_layer_shards(w, l) for l in range(DEPTH)]
    full = [[None] * len(LAYER_MATS) for _ in range(DEPTH)]

    def gather(keys):
        return _gather_exchange([shards[l][i] for l, i in keys], [cuts[i] for _, i in keys])

    def store(keys, mats):
        for (l, i), a in zip(keys, mats):
            full[l][i] = a

    early = lambda l: [(l, i) for i in EARLY_MATS]
    late = lambda l: [(l, i) for i in LATE_MATS]
    store(early(0), _run_exchange(gather(early(0)), name="gather_weights_first"))
    c_all = _all_gather8(jnp.pad(c, ((0, SUBLANES - 1), (0, 0))), name="gather_c")[::SUBLANES]
    mod_b_cols = lax.dynamic_slice_in_dim(w["mod_b"], chip * mod_cols, mod_cols, axis=1)[:, None, :]
    mod_part = _mod_partial(c_all, w["mod_w"], mod_b_cols, name="mod_partial")
    small_fwd = jnp.concatenate([mod_part.reshape(-1), w["norm_pre"].reshape(-1), w["norm_post"].reshape(-1)])
    n_mod, n_norm = mod_part.size, w["norm_pre"].size
    sf_all = _all_gather8(_pad_rows(small_fwd), name="gather_mod").reshape(N_DEV, -1)
    sf_chips = sf_all[::2]
    mod_all = jnp.concatenate(
        [sf_chips[k, :n_mod].reshape(DEPTH, N_DEV, mod_cols) for k in range(N_CHIPS)], axis=2)
    mod_mine = lax.dynamic_index_in_dim(mod_all, dev, axis=1, keepdims=False).reshape(DEPTH, N_SUB, 3, D_MODEL)
    norm_pre = jnp.concatenate(
        [sf_chips[k, n_mod:n_mod + n_norm].reshape(DEPTH, N_SUB, norm_cols) for k in range(N_CHIPS)], axis=2)
    norm_post = jnp.concatenate(
        [sf_chips[k, n_mod + n_norm:n_mod + 2 * n_norm].reshape(DEPTH, N_SUB, norm_cols) for k in range(N_CHIPS)],
        axis=2)

    saved, layer_w, ssm_prep, ssm_vjp = [], [], [], []
    h = x
    for l in range(DEPTH):
        win0, _, wout0, _, mix_in, glu, attn_out, mix_out = full[l]
        w_u, w_qkv, w_f, w_gab = _split_mix_w_in(mix_in.transpose(1, 0, 2).reshape(D_MODEL, IN_WIDTH))
        lw = dict(
            ffn=[(win0[0], wout0[0]), None],
            mix=dict(u=w_u, qkv=w_qkv, f=w_f, gab=w_gab, glu=glu[0], attn_out=attn_out[0], out=mix_out[0]))
        prep, vjp = jax.vjp(_ssm_discretize, *[w[n][l] for n in SSM_NAMES])
        layer_w.append(lw)
        ssm_prep.append(prep)
        ssm_vjp.append(vjp)
        h, s0 = _ffn_fwd(h, mod_mine[l, 0], norm_pre[l, 0], norm_post[l, 0], *lw["ffn"][0], tag=f"l{l}a")
        coming = late(l) + (early(l + 1) if l + 1 < DEPTH else [])
        h, s1, arrived = _mixer_fwd(h, mod_mine[l, 1], norm_pre[l, 1], norm_post[l, 1], lw["mix"], prep,
                                    w["forget_b"][l], tag=f"l{l}m", exchange=gather(coming))
        store(coming, arrived)
        lw["ffn"][1] = (full[l][1][0], full[l][3][0])
        h, s2 = _ffn_fwd(h, mod_mine[l, 2], norm_pre[l, 2], norm_post[l, 2], *lw["ffn"][1], tag=f"l{l}b")
        saved.append((s0, s1, s2))
    dh, loss8 = _loss_head(h, target, name="loss_head")

    c_idx = ci.reshape(1).astype(jnp.int32)
    chip_idx = chip.reshape(1).astype(jnp.int32)
    places = []
    dests = {n: lax.empty(((DEPTH,) if j is None else (DEPTH, 2)) + _part_shape(CUTS[n], slab=True), F32)
             for n, j in LAYER_MATS}
    g_small = {n: [None] * DEPTH for n in SMALL_NAMES}
    dmod, dnpre, dnpost = [], [], []
    pending = []

    def begin_reduce(l, idx, mats, tag):
        cs = [cuts[i] for i in idx]
        mats = [a.reshape(cut.shape) for a, cut in zip(mats, cs)]
        from_sibling = _swap_layer(mats, cs, name=f"grad_swap_{tag}")
        for k, i in enumerate(idx):
            f, b = _add_half(mats[k], from_sibling[k], cs[k], c_idx, name=f"grad_add_l{l}_{i}")
            pending.append((l, i, f, b))

    def partials():
        return _partials_exchange([p[3] for p in pending], [cuts[p[1]] for p in pending])

    def end_reduce(parts):
        for (l, i, f, _), part in zip(pending, parts):
            n, j = LAYER_MATS[i]
            place = (l,) if j is None else (l, j)
            dests[n] = _sum_slab(f, part, cuts[i], dests[n], place, chip_idx, c_idx, name=f"grad_sum_l{l}_{i}")
            places.append((n, place))
        pending.clear()

    for l in reversed(range(DEPTH)):
        lw = layer_w[l]
        dh, dm2, dp2, dq2, dwin2, dwout2 = _ffn_bwd(dh, saved[l][2], mod_mine[l, 2], norm_pre[l, 2],
                                                    norm_post[l, 2], *lw["ffn"][1], tag=f"l{l}b")
        begin_reduce(l, LATE_MATS, [dwin2, dwout2], f"l{l}b")
        dh, dm1, dp1, dq1, gmix, dssm, parts = _mixer_bwd(dh, saved[l][1], mod_mine[l, 1], norm_pre[l, 1],
                                                          norm_post[l, 1], lw["mix"], ssm_prep[l], tag=f"l{l}m",
                                                          exchange=partials())
        end_reduce(parts)
        dh, dm0, dp0, dq0, dwin0, dwout0 = _ffn_bwd(dh, saved[l][0], mod_mine[l, 0], norm_pre[l, 0],
                                                    norm_post[l, 0], *lw["ffn"][0], tag=f"l{l}a")
        dmod.insert(0, jnp.stack([dm0, dm1, dm2]))
        dnpre.insert(0, jnp.stack([dp0, dp1, dp2]))
        dnpost.insert(0, jnp.stack([dq0, dq1, dq2]))
        g_small["forget_b"][l] = gmix["forget_b"]
        for n, g in zip(SSM_NAMES, ssm_vjp[l](dssm)):
            g_small[n][l] = g
        dmix_in = gmix["mix_w_in"].reshape(D_MODEL, N_CHIPS, IN_WIDTH // N_CHIPS).transpose(1, 0, 2)
        begin_reduce(l, EARLY_MATS, [dwin0, dwout0, dmix_in, gmix["glu_w"], gmix["attn_w_out"], gmix["mix_w_out"]],
                     f"l{l}a")
    end_reduce(_run_exchange(partials(), name="grad_partials_last"))
    grad_x = dh
    g_small = {n: jnp.stack(g) for n, g in g_small.items()}

    small = [loss8[0, :1], jnp.stack(dmod).reshape(-1), jnp.stack(dnpre).reshape(-1), jnp.stack(dnpost).reshape(-1)]
    small += [g_small[n].reshape(-1) for n in SMALL_NAMES]
    sizes = [int(s.size) for s in small]
    offs = np.concatenate([[0], np.cumsum(sizes)])
    sb_all, sb_sum = _all_gather8(_pad_rows(jnp.concatenate(small)), name="gather_small_grads", with_sum=True)
    sb_sum = sb_sum.reshape(-1)
    take = lambda i: sb_sum[int(offs[i]):int(offs[i + 1])]
    loss = take(0)[0]
    grads = {"mod_b": take(1).reshape(DEPTH, N_SUB * 3 * D_MODEL)}
    dnorm_pre_full = take(2).reshape(DEPTH, N_SUB, D_MODEL)
    dnorm_post_full = take(3).reshape(DEPTH, N_SUB, D_MODEL)
    grads["norm_pre"] = lax.dynamic_slice_in_dim(dnorm_pre_full, chip * norm_cols, norm_cols, axis=2)
    grads["norm_post"] = lax.dynamic_slice_in_dim(dnorm_post_full, chip * norm_cols, norm_cols, axis=2)
    for i, n in enumerate(SMALL_NAMES):
        grads[n] = take(4 + i).reshape(w[n].shape)
    dmod_all = sb_all.reshape(N_DEV, -1)[:, int(offs[1]):int(offs[2])].reshape(N_DEV, DEPTH, N_SUB * 3 * D_MODEL)
    dmod_cols = lax.dynamic_slice_in_dim(dmod_all, chip * mod_cols, mod_cols, axis=2).transpose(1, 0, 2)
    grads["mod_w"] = _mod_wgrad(c_all.T, dmod_cols, name="mod_wgrad")

    shared = _share_all(dests, [CUTS[n] for n, _ in places], places, name="grad_share")
    for n, g in zip(dests, shared):
        grads[n] = g.reshape(w[n].shape)

    delta, new_m, new_v = {}, {}, {}
    for n in WEIGHT_NAMES:
        delta[n], new_m[n], new_v[n] = _adamw(w[n], grads[n], m[n], v[n], name=f"adamw_{n}")
    outs = [loss, grad_x[None]]
    for group in (grads, delta, new_m, new_v):
        outs += [group[n] for n in WEIGHT_NAMES]
    return tuple(outs)


def kernel(x, c, mod_w, mod_b, norm_pre, norm_post, ffn_w_in, ffn_w_out, mix_w_in, forget_b, ssm_a_re, ssm_a_im, ssm_log_dt, ssm_b_re, ssm_b_im, ssm_c_re, ssm_c_im, ssm_d, glu_w, attn_w_out, mix_w_out, loss_target, m_mod_w, m_mod_b, m_norm_pre, m_norm_post, m_ffn_w_in, m_ffn_w_out, m_mix_w_in, m_forget_b, m_ssm_a_re, m_ssm_a_im, m_ssm_log_dt, m_ssm_b_re, m_ssm_b_im, m_ssm_c_re, m_ssm_c_im, m_ssm_d, m_glu_w, m_attn_w_out, m_mix_w_out, v_mod_w, v_mod_b, v_norm_pre, v_norm_post, v_ffn_w_in, v_ffn_w_out, v_mix_w_in, v_forget_b, v_ssm_a_re, v_ssm_a_im, v_ssm_log_dt, v_ssm_b_re, v_ssm_b_im, v_ssm_c_re, v_ssm_c_im, v_ssm_d, v_glu_w, v_attn_w_out, v_mix_w_out):
    w = dict(mod_w=mod_w, mod_b=mod_b, norm_pre=norm_pre, norm_post=norm_post, ffn_w_in=ffn_w_in,
             ffn_w_out=ffn_w_out, mix_w_in=mix_w_in, forget_b=forget_b, ssm_a_re=ssm_a_re, ssm_a_im=ssm_a_im,
             ssm_log_dt=ssm_log_dt, ssm_b_re=ssm_b_re, ssm_b_im=ssm_b_im, ssm_c_re=ssm_c_re, ssm_c_im=ssm_c_im,
             ssm_d=ssm_d, glu_w=glu_w, attn_w_out=attn_w_out, mix_w_out=mix_w_out)
    m = dict(mod_w=m_mod_w, mod_b=m_mod_b, norm_pre=m_norm_pre, norm_post=m_norm_post, ffn_w_in=m_ffn_w_in,
             ffn_w_out=m_ffn_w_out, mix_w_in=m_mix_w_in, forget_b=m_forget_b, ssm_a_re=m_ssm_a_re,
             ssm_a_im=m_ssm_a_im, ssm_log_dt=m_ssm_log_dt, ssm_b_re=m_ssm_b_re, ssm_b_im=m_ssm_b_im,
             ssm_c_re=m_ssm_c_re, ssm_c_im=m_ssm_c_im, ssm_d=m_ssm_d, glu_w=m_glu_w, attn_w_out=m_attn_w_out,
             mix_w_out=m_mix_w_out)
    v = dict(mod_w=v_mod_w, mod_b=v_mod_b, norm_pre=v_norm_pre, norm_post=v_norm_post, ffn_w_in=v_ffn_w_in,
             ffn_w_out=v_ffn_w_out, mix_w_in=v_mix_w_in, forget_b=v_forget_b, ssm_a_re=v_ssm_a_re,
             ssm_a_im=v_ssm_a_im, ssm_log_dt=v_ssm_log_dt, ssm_b_re=v_ssm_b_re, ssm_b_im=v_ssm_b_im,
             ssm_c_re=v_ssm_c_re, ssm_c_im=v_ssm_c_im, ssm_d=v_ssm_d, glu_w=v_glu_w, attn_w_out=v_attn_w_out,
             mix_w_out=v_mix_w_out)
    return _train_step(x[0], c, loss_target[0], w, m, v)
```

```python
import functools
import math
from typing import NamedTuple

import jax
import jax.numpy as jnp
import numpy as np
from jax import lax
from jax.experimental import pallas as pl
from jax.experimental.pallas import tpu as pltpu

F32 = jnp.float32
BF16 = jnp.bfloat16

D_MODEL = 1024
DEPTH = 2
SSM_WIDTH = 512
SSM_GROUP = 16
SSM_GROUPS = 32
SSM_STATE = 64
SSM_FLAT = SSM_GROUPS * SSM_STATE
SSM_BLOCKS = 4
ATTN_HEADS = 8
HEAD_DIM = 64
ATTN_WIDTH = 512
D_FF = 2816
FFN_RES = 0.5
N_SUB = 3
RMS_EPS = 1e-6
N_CHIPS = 4
N_DEV = 8

ADAM_LR = 0.001
ADAM_B1 = 0.9
ADAM_B2 = 0.999
ADAM_EPS = 1e-08
ADAM_WD = 0.01
ADAM_STEP = 10

LANES = 128
SUBLANES = 8
VMEM_LIMIT = 52 * 1024 * 1024
MESH = pl.DeviceIdType.MESH

NN = (((1,), (0,)), ((), ()))
NT = (((1,), (1,)), ((), ()))
TN = (((0,), (0,)), ((), ()))


def _tile(dim, target, align=LANES):
    best = None
    t = align
    while t <= min(dim, target):
        if dim % t == 0:
            best = t
        t += align
    return dim if best is None else best


def _params(*sem):
    return pltpu.CompilerParams(dimension_semantics=sem, vmem_limit_bytes=VMEM_LIMIT)


def _mm(a, b, *, name, ta=False, tb=False, out_dtype=F32, bias=None, bscale=None, b_k0=0,
        tm=512, tn=1024, tk=1024):
    M, K = (a.shape[1], a.shape[0]) if ta else a.shape
    N = b.shape[0] if tb else b.shape[1]
    assert b_k0 + K <= (b.shape[1] if tb else b.shape[0]), (a.shape, b.shape, ta, tb)
    tm, tn, tk = _tile(M, tm), _tile(N, tn), _tile(K, tk)
    nk = K // tk
    assert b_k0 % tk == 0
    kb0 = b_k0 // tk
    dn = (((0 if ta else 1,), (1 if tb else 0,)), ((), ()))
    has_bias, has_scale = bias is not None, bscale is not None

    def body(*refs):
        a_ref, b_ref = refs[0], refs[1]
        pos = 2
        bias_ref = scale_ref = None
        if has_bias:
            bias_ref = refs[pos]
            pos += 1
        if has_scale:
            scale_ref = refs[pos]
            pos += 1
        o_ref = refs[pos]
        acc_ref = refs[pos + 1] if nk > 1 else None

        def finish(r):
            if has_bias:
                extra = bias_ref[...].astype(F32)
                if has_scale:
                    extra = extra * scale_ref[...]
                r = r + extra
            o_ref[...] = r.astype(out_dtype)

        part = lax.dot_general(a_ref[...].astype(BF16), b_ref[...].astype(BF16), dn,
                               preferred_element_type=F32)
        if nk == 1:
            finish(part)
        else:
            k = pl.program_id(2)

            @pl.when(k == 0)
            def _():
                acc_ref[...] = part

            @pl.when(k > 0)
            def _():
                acc_ref[...] += part

            @pl.when(k == nk - 1)
            def _():
                finish(acc_ref[...])

    a_spec = pl.BlockSpec((tk, tm), lambda j, i, k: (k, i)) if ta else pl.BlockSpec((tm, tk), lambda j, i, k: (i, k))
    b_spec = (pl.BlockSpec((tn, tk), lambda j, i, k: (j, kb0 + k)) if tb
              else pl.BlockSpec((tk, tn), lambda j, i, k: (kb0 + k, j)))
    in_specs = [a_spec, b_spec]
    args = [a, b]
    if has_bias:
        in_specs.append(pl.BlockSpec((tm, tn), lambda j, i, k: (i, j)))
        args.append(bias)
    if has_scale:
        in_specs.append(pl.BlockSpec((1, tn), lambda j, i, k: (0, j)))
        args.append(bscale)
    return pl.pallas_call(
        body, name=name,
        grid=(N // tn, M // tm, nk),
        in_specs=in_specs,
        out_specs=pl.BlockSpec((tm, tn), lambda j, i, k: (i, j)),
        out_shape=jax.ShapeDtypeStruct((M, N), out_dtype),
        scratch_shapes=[pltpu.VMEM((tm, tn), F32)] if nk > 1 else [],
        compiler_params=_params("parallel", "parallel", "arbitrary"),
    )(*args)


def _mm_bd(a_list, b_list, *, name, tb=False, out_dtype=F32, bias=None, bscale=None, tm=1024):
    G = b_list[0].shape[0]
    Kb, Nb = (b_list[0].shape[2], b_list[0].shape[1]) if tb else b_list[0].shape[1:]
    M = a_list[0].shape[0]
    tm = _tile(M, tm)
    na, nb = len(a_list), len(b_list)
    n_out = nb if na == 1 else 1
    dn = NT if tb else NN
    has_bias, has_scale = bias is not None, bscale is not None

    def body(*refs):
        a_refs, b_refs = refs[:na], refs[na:na + nb]
        pos = na + nb
        bias_ref = scale_ref = None
        if has_bias:
            bias_ref = refs[pos]
            pos += 1
        if has_scale:
            scale_ref = refs[pos]
            pos += 1
        o_refs = refs[pos:]
        prods = [lax.dot_general(a_refs[min(i, na - 1)][...].astype(BF16), b_refs[i][...].astype(BF16), dn,
                                 preferred_element_type=F32) for i in range(nb)]
        outs = prods if n_out == nb else [functools.reduce(jnp.add, prods)]
        for o_ref, r in zip(o_refs, outs):
            if has_bias:
                extra = bias_ref[...].astype(F32)
                r = r + (extra * scale_ref[...] if has_scale else extra)
            o_ref[...] = r.astype(out_dtype)

    a_spec = pl.BlockSpec((tm, Kb), lambda g, i: (i, g))
    b_spec = pl.BlockSpec((None,) + b_list[0].shape[1:], lambda g, i: (g, 0, 0))
    o_spec = pl.BlockSpec((tm, Nb), lambda g, i: (i, g))
    in_specs = [a_spec] * na + [b_spec] * nb
    args = list(a_list) + list(b_list)
    if has_bias:
        in_specs.append(o_spec)
        args.append(bias)
    if has_scale:
        in_specs.append(pl.BlockSpec((1, Nb), lambda g, i: (0, g)))
        args.append(bscale)
    sds = jax.ShapeDtypeStruct((M, G * Nb), out_dtype)
    res = pl.pallas_call(
        body, name=name, grid=(G, M // tm), in_specs=in_specs, out_specs=[o_spec] * n_out,
        out_shape=[sds] * n_out, compiler_params=_params("parallel", "parallel"),
    )(*args)
    return res[0] if n_out == 1 else res


def _mm_bd_t(a, b, G, *, name, tk=1024):
    K, Mb, Nb = a.shape[0], a.shape[1] // G, b.shape[1] // G
    tk = _tile(K, tk)
    nk = K // tk

    def body(a_ref, b_ref, o_ref, acc_ref):
        k = pl.program_id(1)
        part = lax.dot_general(a_ref[...].astype(BF16), b_ref[...].astype(BF16), TN, preferred_element_type=F32)

        @pl.when(k == 0)
        def _():
            acc_ref[...] = part

        @pl.when(k > 0)
        def _():
            acc_ref[...] += part

        @pl.when(k == nk - 1)
        def _():
            o_ref[...] = acc_ref[...]

    return pl.pallas_call(
        body, name=name, grid=(G, nk),
        in_specs=[pl.BlockSpec((tk, Mb), lambda g, k: (k, g)), pl.BlockSpec((tk, Nb), lambda g, k: (k, g))],
        out_specs=pl.BlockSpec((None, Mb, Nb), lambda g, k: (g, 0, 0)),
        out_shape=jax.ShapeDtypeStruct((G, Mb, Nb), F32),
        scratch_shapes=[pltpu.VMEM((Mb, Nb), F32)], compiler_params=_params("parallel", "arbitrary"),
    )(a, b)


def _sigmoid(x):
    return 0.5 * jnp.tanh(0.5 * x) + 0.5


def _mm_swiglu(h, w_in, *, name, tm=512, tn=1408):
    M, K = h.shape
    N = w_in.shape[1] // 2
    tm, tn = _tile(M, tm), _tile(N, tn)
    nj = N // tn

    def body(h_ref, wg_ref, wu_ref, g_ref, u_ref, a_ref):
        hv = h_ref[...]
        g = jnp.dot(hv, wg_ref[...], preferred_element_type=F32)
        u = jnp.dot(hv, wu_ref[...], preferred_element_type=F32)
        g_ref[...] = g.astype(BF16)
        u_ref[...] = u.astype(BF16)
        a_ref[...] = (g * _sigmoid(g) * u).astype(BF16)

    o_spec = pl.BlockSpec((tm, tn), lambda j, i: (i, j))
    sds = jax.ShapeDtypeStruct((M, N), BF16)
    return pl.pallas_call(
        body, name=name, grid=(nj, M // tm),
        in_specs=[pl.BlockSpec((tm, K), lambda j, i: (i, 0)), pl.BlockSpec((K, tn), lambda j, i: (0, j)),
                  pl.BlockSpec((K, tn), lambda j, i: (0, nj + j))],
        out_specs=[o_spec, o_spec, o_spec], out_shape=[sds, sds, sds],
        compiler_params=_params("parallel", "parallel"),
    )(h, w_in, w_in)


def _mm_swiglu_bwd(dy, w_out, gate, up, *, name, tm=512, tn=1408):
    M, K = dy.shape
    N = w_out.shape[0]
    tm, tn = _tile(M, tm), _tile(N, tn)

    def body(dy_ref, w_ref, g_ref, u_ref, dg_ref, du_ref):
        dact = lax.dot_general(dy_ref[...], w_ref[...], NT, preferred_element_type=F32)
        g = g_ref[...].astype(F32)
        u = u_ref[...].astype(F32)
        sig = _sigmoid(g)
        dg_ref[...] = (dact * u * (sig * (1.0 + g * (1.0 - sig)))).astype(BF16)
        du_ref[...] = (dact * (g * sig)).astype(BF16)

    t_spec = pl.BlockSpec((tm, tn), lambda j, i: (i, j))
    sds = jax.ShapeDtypeStruct((M, N), BF16)
    return pl.pallas_call(
        body, name=name, grid=(N // tn, M // tm),
        in_specs=[pl.BlockSpec((tm, K), lambda j, i: (i, 0)), pl.BlockSpec((tn, K), lambda j, i: (j, 0)),
                  t_spec, t_spec],
        out_specs=[t_spec, t_spec], out_shape=[sds, sds],
        compiler_params=_params("parallel", "parallel"),
    )(dy, w_out, gate, up)


ROW_TILE = 256


def _colsum8(v):
    return jnp.sum(v.reshape(v.shape[0] // SUBLANES, SUBLANES, v.shape[1]), axis=0)


def _finish_colsums(step, last, refs):
    @pl.when(step == last)
    def _():
        for r in refs:
            r[...] = jnp.broadcast_to(jnp.sum(r[...], axis=0, keepdims=True), r.shape)


def _row_spec(t, d):
    return pl.BlockSpec((t, d), lambda i: (i, 0))


def _vec_spec(d, rows=1):
    return pl.BlockSpec((rows, d), lambda i: (0, 0))


def _prenorm(x, g, sc, sh, *, name):
    L, D = x.shape
    t = _tile(L, ROW_TILE, SUBLANES)

    def body(x_ref, g_ref, sc_ref, sh_ref, h_ref):
        xv = x_ref[...]
        r = lax.rsqrt(jnp.mean(xv * xv, axis=-1, keepdims=True) + RMS_EPS)
        h_ref[...] = (((xv * r) * g_ref[...]) * (1.0 + sc_ref[...]) + sh_ref[...]).astype(BF16)

    return pl.pallas_call(
        body, name=name, grid=(L // t,),
        in_specs=[_row_spec(t, D), _vec_spec(D), _vec_spec(D), _vec_spec(D)],
        out_specs=_row_spec(t, D), out_shape=jax.ShapeDtypeStruct((L, D), BF16),
        compiler_params=_params("parallel"),
    )(x, g, sc, sh)


def _postnorm_res(x, y, g, gate, res_w, *, name):
    L, D = x.shape
    t = _tile(L, ROW_TILE, SUBLANES)

    def body(x_ref, y_ref, g_ref, gate_ref, o_ref):
        yv = y_ref[...]
        r = lax.rsqrt(jnp.mean(yv * yv, axis=-1, keepdims=True) + RMS_EPS)
        o_ref[...] = x_ref[...] + (res_w * gate_ref[...]) * ((yv * r) * g_ref[...])

    return pl.pallas_call(
        body, name=name, grid=(L // t,),
        in_specs=[_row_spec(t, D), _row_spec(t, D), _vec_spec(D), _vec_spec(D)],
        out_specs=_row_spec(t, D), out_shape=jax.ShapeDtypeStruct((L, D), F32),
        compiler_params=_params("parallel"),
    )(x, y, g, gate)


def _postnorm_bwd(dxo, y, g, gate, res_w, *, name):
    L, D = y.shape
    t = _tile(L, ROW_TILE, SUBLANES)
    n = L // t

    def body(dxo_ref, y_ref, g_ref, gate_ref, dy_ref, dgate_ref, dg_ref):
        i = pl.program_id(0)

        @pl.when(i == 0)
        def _():
            dgate_ref[...] = jnp.zeros_like(dgate_ref)
            dg_ref[...] = jnp.zeros_like(dg_ref)

        yv = y_ref[...]
        dv = dxo_ref[...]
        gv = g_ref[...]
        r = lax.rsqrt(jnp.mean(yv * yv, axis=-1, keepdims=True) + RMS_EPS)
        yn = yv * r
        dgate_ref[...] += _colsum8(dv * (res_w * (yn * gv)))
        do = dv * (res_w * gate_ref[...])
        dg_ref[...] += _colsum8(do * yn)
        dyn = do * gv
        dy_ref[...] = (r * (dyn - yn * jnp.mean(dyn * yn, axis=-1, keepdims=True))).astype(BF16)
        _finish_colsums(i, n - 1, (dgate_ref, dg_ref))

    sum_sds = jax.ShapeDtypeStruct((SUBLANES, D), F32)
    return pl.pallas_call(
        body, name=name, grid=(n,),
        in_specs=[_row_spec(t, D), _row_spec(t, D), _vec_spec(D), _vec_spec(D)],
        out_specs=[_row_spec(t, D), _vec_spec(D, SUBLANES), _vec_spec(D, SUBLANES)],
        out_shape=[jax.ShapeDtypeStruct((L, D), BF16), sum_sds, sum_sds],
        compiler_params=_params("arbitrary"),
    )(dxo, y, g, gate)


def _prenorm_bwd(x, dh, dxres, g, sc, *, name):
    L, D = x.shape
    t = _tile(L, ROW_TILE, SUBLANES)
    n = L // t

    def body(x_ref, dh_ref, dxr_ref, g_ref, sc_ref, dx_ref, dsh_ref, dsc_ref, dg_ref):
        i = pl.program_id(0)

        @pl.when(i == 0)
        def _():
            dsh_ref[...] = jnp.zeros_like(dsh_ref)
            dsc_ref[...] = jnp.zeros_like(dsc_ref)
            dg_ref[...] = jnp.zeros_like(dg_ref)

        xv = x_ref[...]
        dhv = dh_ref[...].astype(F32)
        gv = g_ref[...]
        one_sc = 1.0 + sc_ref[...]
        r = lax.rsqrt(jnp.mean(xv * xv, axis=-1, keepdims=True) + RMS_EPS)
        xn = xv * r
        tt = dhv * xn
        dsh_ref[...] += _colsum8(dhv)
        dsc_ref[...] += _colsum8(tt * gv)
        dg_ref[...] += _colsum8(tt * one_sc)
        dxn = dhv * (gv * one_sc)
        dx_ref[...] = dxr_ref[...] + r * (dxn - xn * jnp.mean(dxn * xn, axis=-1, keepdims=True))
        _finish_colsums(i, n - 1, (dsh_ref, dsc_ref, dg_ref))

    sum_sds = jax.ShapeDtypeStruct((SUBLANES, D), F32)
    sum_spec = _vec_spec(D, SUBLANES)
    return pl.pallas_call(
        body, name=name, grid=(n,),
        in_specs=[_row_spec(t, D), _row_spec(t, D), _row_spec(t, D), _vec_spec(D), _vec_spec(D)],
        out_specs=[_row_spec(t, D), sum_spec, sum_spec, sum_spec],
        out_shape=[jax.ShapeDtypeStruct((L, D), F32), sum_sds, sum_sds, sum_sds],
        compiler_params=_params("arbitrary"),
    )(x, dh, dxres, g, sc)


def _loss_head(y, target, *, name):
    L, D = y.shape
    t = _tile(L, ROW_TILE, SUBLANES)
    n = L // t

    def body(y_ref, t_ref, dy_ref, loss_ref):
        i = pl.program_id(0)

        @pl.when(i == 0)
        def _():
            loss_ref[...] = jnp.zeros_like(loss_ref)

        e = y_ref[...] - t_ref[...]
        dy_ref[...] = e * (1.0 / D)
        part = jnp.sum(jnp.mean(e * e, axis=-1, keepdims=True), axis=0, keepdims=True)
        loss_ref[...] += jnp.broadcast_to(0.5 * part, loss_ref.shape)

    return pl.pallas_call(
        body, name=name, grid=(n,),
        in_specs=[_row_spec(t, D), _row_spec(t, D)],
        out_specs=[_row_spec(t, D), pl.BlockSpec((SUBLANES, LANES), lambda i: (0, 0))],
        out_shape=[jax.ShapeDtypeStruct((L, D), F32), jax.ShapeDtypeStruct((SUBLANES, LANES), F32)],
        compiler_params=_params("arbitrary"),
    )(y, target)


GELU_C = math.sqrt(2.0 / math.pi)


def _gelu_fwd(y, *, name):
    L, W = y.shape
    t = _tile(L, 512, SUBLANES)

    def body(y_ref, o_ref):
        v = y_ref[...]
        o_ref[...] = (0.5 * v * (1.0 + jnp.tanh(GELU_C * (v + 0.044715 * (v * v * v))))).astype(BF16)

    return pl.pallas_call(
        body, name=name, grid=(L // t,), in_specs=[_row_spec(t, W)], out_specs=_row_spec(t, W),
        out_shape=jax.ShapeDtypeStruct((L, W), BF16), compiler_params=_params("parallel"),
    )(y)


def _gelu_bwd(dgl, y, u, dskip, *, name):
    L, W = y.shape
    t = _tile(L, 512, SUBLANES)
    n = L // t

    def body(dgl_ref, y_ref, u_ref, d_ref, dy_ref, sk_ref, dd_ref):
        i = pl.program_id(0)

        @pl.when(i == 0)
        def _():
            dd_ref[...] = jnp.zeros_like(dd_ref)

        v = y_ref[...]
        inner = GELU_C * (v + 0.044715 * (v * v * v))
        th = jnp.tanh(inner)
        dgelu = 0.5 * (1.0 + th) + 0.5 * v * (1.0 - th * th) * (GELU_C * (1.0 + 3.0 * 0.044715 * (v * v)))
        dy = dgl_ref[...] * dgelu
        dy_ref[...] = dy.astype(BF16)
        sk_ref[...] = dy * d_ref[...]
        dd_ref[...] += _colsum8(dy * u_ref[...])
        _finish_colsums(i, n - 1, (dd_ref,))

    return pl.pallas_call(
        body, name=name, grid=(n,),
        in_specs=[_row_spec(t, W), _row_spec(t, W), _row_spec(t, W), _vec_spec(W)],
        out_specs=[_row_spec(t, W), _row_spec(t, W), _vec_spec(W, SUBLANES)],
        out_shape=[jax.ShapeDtypeStruct((L, W), BF16), jax.ShapeDtypeStruct((L, W), F32),
                   jax.ShapeDtypeStruct((SUBLANES, W), F32)],
        compiler_params=_params("arbitrary"),
    )(dgl, y, u, dskip)


def _merge_fwd(z, yb, gab, *, name):
    L, D = yb.shape
    t = _tile(L, ROW_TILE, SUBLANES)

    def body(z_ref, yb_ref, gab_ref, o_ref):
        ya = z_ref[:, :D] * _sigmoid(z_ref[:, D:])
        o_ref[...] = (_sigmoid(gab_ref[:, :D]) * ya + _sigmoid(gab_ref[:, D:]) * yb_ref[...]).astype(BF16)

    return pl.pallas_call(
        body, name=name, grid=(L // t,),
        in_specs=[_row_spec(t, 2 * D), _row_spec(t, D), _row_spec(t, 2 * D)],
        out_specs=_row_spec(t, D), out_shape=jax.ShapeDtypeStruct((L, D), BF16),
        compiler_params=_params("parallel"),
    )(z, yb, gab)


def _merge_bwd(dm, z, yb, gab, *, name):
    L, D = yb.shape
    t = _tile(L, ROW_TILE, SUBLANES)

    def body(dm_ref, z_ref, yb_ref, gab_ref, dz_ref, dyb_ref, dgab_ref):
        dmv = dm_ref[...]
        zv = z_ref[:, :D]
        sz = _sigmoid(z_ref[:, D:])
        sa = _sigmoid(gab_ref[:, :D])
        sb = _sigmoid(gab_ref[:, D:])
        ybv = yb_ref[...]
        dya = dmv * sa
        dz_ref[:, :D] = (dya * sz).astype(BF16)
        dz_ref[:, D:] = (dya * zv * (sz * (1.0 - sz))).astype(BF16)
        dyb_ref[...] = (dmv * sb).astype(BF16)
        dgab_ref[:, :D] = (dmv * (zv * sz) * (sa * (1.0 - sa))).astype(BF16)
        dgab_ref[:, D:] = (dmv * ybv * (sb * (1.0 - sb))).astype(BF16)

    return pl.pallas_call(
        body, name=name, grid=(L // t,),
        in_specs=[_row_spec(t, D), _row_spec(t, 2 * D), _row_spec(t, D), _row_spec(t, 2 * D)],
        out_specs=[_row_spec(t, 2 * D), _row_spec(t, D), _row_spec(t, 2 * D)],
        out_shape=[jax.ShapeDtypeStruct((L, 2 * D), BF16), jax.ShapeDtypeStruct((L, D), BF16),
                   jax.ShapeDtypeStruct((L, 2 * D), BF16)],
        compiler_params=_params("parallel"),
    )(dm, z, yb, gab)


SCAN_W = 1024
SCAN_T = 512


def _interleave(x):
    L, W = x.shape
    seg = SCAN_T // SUBLANES
    return x.reshape(L // SCAN_T, SUBLANES, seg, W).transpose(0, 2, 1, 3).reshape(L, W)


def _deinterleave(x):
    L, W = x.shape
    seg = SCAN_T // SUBLANES
    return x.reshape(L // SCAN_T, seg, SUBLANES, W).transpose(0, 2, 1, 3).reshape(L, W)


def _power_table(a, b, pr_tab, pi_tab, n):
    def fill(k, carry):
        pr, pi = carry
        pr_tab[k] = pr
        pi_tab[k] = pi
        return a * pr - b * pi, a * pi + b * pr

    lax.fori_loop(0, n, fill, (a, b))


def _rows_to_tile(rows):
    w = rows[0].shape[1]
    sub = lax.broadcasted_iota(jnp.int32, (SUBLANES, w), 0)
    tile = jnp.broadcast_to(rows[0], (SUBLANES, w))
    for j in range(1, SUBLANES):
        tile = jnp.where(sub == j, jnp.broadcast_to(rows[j], (SUBLANES, w)), tile)
    return tile


def _ssm_scan_fwd(bu_re, bu_im, lam_re, lam_im, *, name):
    L, S = bu_re.shape
    w, t = _tile(S, SCAN_W), SCAN_T
    seg = t // SUBLANES

    def body(br_ref, bi_ref, lr_ref, li_ref, sr_ref, si_ref, pr_tab, pi_tab, cr_ref, ci_ref):
        a = jnp.broadcast_to(lr_ref[...], (SUBLANES, w))
        b = jnp.broadcast_to(li_ref[...], (SUBLANES, w))

        @pl.when(pl.program_id(1) == 0)
        def _():
            cr_ref[...] = jnp.zeros_like(cr_ref)
            ci_ref[...] = jnp.zeros_like(ci_ref)
            _power_table(a, b, pr_tab, pi_tab, seg)

        def local_scan(i, carry):
            sr, si = carry
            base = pl.multiple_of(i * SUBLANES, SUBLANES)
            nr = a * sr - b * si + br_ref[pl.ds(base, SUBLANES), :]
            ni = a * si + b * sr + bi_ref[pl.ds(base, SUBLANES), :]
            sr_ref[pl.ds(base, SUBLANES), :] = nr
            si_ref[pl.ds(base, SUBLANES), :] = ni
            return nr, ni

        zero = jnp.zeros((SUBLANES, w), F32)
        fr, fi = lax.fori_loop(0, seg, local_scan, (zero, zero), unroll=2)
        lsr, lsi = pr_tab[seg - 1][0:1, :], pi_tab[seg - 1][0:1, :]
        cr, ci = cr_ref[...], ci_ref[...]
        rows_r, rows_i = [], []
        for j in range(SUBLANES):
            rows_r.append(cr)
            rows_i.append(ci)
            cr, ci = fr[j:j + 1, :] + (lsr * cr - lsi * ci), fi[j:j + 1, :] + (lsr * ci + lsi * cr)
        cr_ref[...] = cr
        ci_ref[...] = ci
        in_r, in_i = _rows_to_tile(rows_r), _rows_to_tile(rows_i)

        def add_entry(i, _):
            base = pl.multiple_of(i * SUBLANES, SUBLANES)
            pr, pi = pr_tab[i], pi_tab[i]
            sr_ref[pl.ds(base, SUBLANES), :] += pr * in_r - pi * in_i
            si_ref[pl.ds(base, SUBLANES), :] += pr * in_i + pi * in_r
            return 0

        lax.fori_loop(0, seg, add_entry, 0, unroll=2)

    blk = pl.BlockSpec((t, w), lambda j, i: (i, j))
    vec = pl.BlockSpec((1, w), lambda j, i: (0, j))
    sds = jax.ShapeDtypeStruct((L, S), F32)
    tab = pltpu.VMEM((seg, SUBLANES, w), F32)
    return pl.pallas_call(
        body, name=name, grid=(S // w, L // t),
        in_specs=[blk, blk, vec, vec], out_specs=[blk, blk], out_shape=[sds, sds],
        scratch_shapes=[tab, tab, pltpu.VMEM((1, w), F32), pltpu.VMEM((1, w), F32)],
        compiler_params=_params("parallel", "arbitrary"),
    )(bu_re, bu_im, lam_re, lam_im)


def _ssm_scan_bwd(d_re, d_im, s_re, s_im, lam_re, lam_im, *, name):
    L, S = d_re.shape
    w, t = _tile(S, SCAN_W), SCAN_T
    nt = L // t
    seg = t // SUBLANES

    def body(dr_ref, di_ref, sr_ref, si_ref, lr_ref, li_ref, gr_ref, gi_ref, ar_ref, ai_ref,
             pr_tab, pi_tab, cgr, cgi, acc_r, acc_i):
        step = pl.program_id(1)
        a = jnp.broadcast_to(lr_ref[...], (SUBLANES, w))
        b = jnp.broadcast_to(-li_ref[...], (SUBLANES, w))

        @pl.when(step == 0)
        def _():
            for r in (cgr, cgi, acc_r, acc_i):
                r[...] = jnp.zeros_like(r)
            _power_table(a, b, pr_tab, pi_tab, seg)

        def local_scan(ii, carry):
            gr, gi = carry
            base = pl.multiple_of((seg - 1 - ii) * SUBLANES, SUBLANES)
            ngr = a * gr - b * gi + dr_ref[pl.ds(base, SUBLANES), :]
            ngi = a * gi + b * gr + di_ref[pl.ds(base, SUBLANES), :]
            gr_ref[pl.ds(base, SUBLANES), :] = ngr
            gi_ref[pl.ds(base, SUBLANES), :] = ngi
            return ngr, ngi

        zero = jnp.zeros((SUBLANES, w), F32)
        fr, fi = lax.fori_loop(0, seg, local_scan, (zero, zero), unroll=2)
        lsr, lsi = pr_tab[seg - 1][0:1, :], pi_tab[seg - 1][0:1, :]
        cr, ci = cgr[...], cgi[...]
        rows_r, rows_i = [None] * SUBLANES, [None] * SUBLANES
        for j in reversed(range(SUBLANES)):
            rows_r[j], rows_i[j] = cr, ci
            cr, ci = fr[j:j + 1, :] + (lsr * cr - lsi * ci), fi[j:j + 1, :] + (lsr * ci + lsi * cr)
        cgr[...] = cr
        cgi[...] = ci
        in_r, in_i = _rows_to_tile(rows_r), _rows_to_tile(rows_i)

        def add_entry(ii, carry):
            nr, ni, xr, xi = carry
            base = pl.multiple_of((seg - 1 - ii) * SUBLANES, SUBLANES)
            sr = sr_ref[pl.ds(base, SUBLANES), :]
            si = si_ref[pl.ds(base, SUBLANES), :]
            xr = xr + (nr * sr + ni * si)
            xi = xi + (ni * sr - nr * si)
            pr, pi = pr_tab[ii], pi_tab[ii]
            gr = gr_ref[pl.ds(base, SUBLANES), :] + (pr * in_r - pi * in_i)
            gi = gi_ref[pl.ds(base, SUBLANES), :] + (pr * in_i + pi * in_r)
            gr_ref[pl.ds(base, SUBLANES), :] = gr
            gi_ref[pl.ds(base, SUBLANES), :] = gi
            return gr, gi, xr, xi

        _, _, xr, xi = lax.fori_loop(0, seg, add_entry, (in_r, in_i, acc_r[...], acc_i[...]), unroll=2)
        acc_r[...] = xr
        acc_i[...] = xi

        @pl.when(step == nt - 1)
        def _():
            ar_ref[...] = jnp.sum(xr, axis=0, keepdims=True)
            ai_ref[...] = jnp.sum(xi, axis=0, keepdims=True)

    blk = pl.BlockSpec((t, w), lambda j, i: (nt - 1 - i, j))
    vec = pl.BlockSpec((1, w), lambda j, i: (0, j))
    sds = jax.ShapeDtypeStruct((L, S), F32)
    vsds = jax.ShapeDtypeStruct((1, S), F32)
    tab = pltpu.VMEM((seg, SUBLANES, w), F32)
    tile = pltpu.VMEM((SUBLANES, w), F32)
    return pl.pallas_call(
        body, name=name, grid=(S // w, nt),
        in_specs=[blk, blk, blk, blk, vec, vec], out_specs=[blk, blk, vec, vec],
        out_shape=[sds, sds, vsds, vsds],
        scratch_shapes=[tab, tab, pltpu.VMEM((1, w), F32), pltpu.VMEM((1, w), F32), tile, tile],
        compiler_params=_params("parallel", "arbitrary"),
    )(d_re, d_im, s_re, s_im, lam_re, lam_im)


def _ssm_discretize(a_re, a_im, log_dt, b_re, b_im, c_re, c_im, d_skip):
    G, P, N = SSM_GROUPS, SSM_STATE, SSM_GROUP
    a = jnp.minimum(a_re, -1e-4)
    dt = jnp.exp(log_dt)[:, None]
    mag = jnp.exp(a * dt)
    lr = mag * jnp.cos(a_im * dt)
    li = mag * jnp.sin(a_im * dt)
    den = a * a + a_im * a_im
    cr = ((lr - 1.0) * a + li * a_im) / den
    ci = (li * a - (lr - 1.0) * a_im) / den
    bbr = cr[..., None] * b_re - ci[..., None] * b_im
    bbi = cr[..., None] * b_im + ci[..., None] * b_re
    gl = G // SSM_BLOCKS
    eye = jnp.eye(gl, dtype=F32)[None, :, None, :, None]

    def in_map(bb):
        t = bb.transpose(0, 2, 1).reshape(SSM_BLOCKS, gl, N, P)
        return (eye * t[:, :, :, None, :]).reshape(SSM_BLOCKS, gl * N, gl * P)

    def out_map(c):
        t = c.transpose(0, 2, 1).reshape(SSM_BLOCKS, gl, P, N)
        return (eye * t[:, :, :, None, :]).reshape(SSM_BLOCKS, gl * P, gl * N)

    return (lr.reshape(1, G * P), li.reshape(1, G * P), in_map(bbr), in_map(bbi),
            out_map(c_re), out_map(-c_im), d_skip.reshape(1, SSM_WIDTH))


ATT_T = 512
CUM_T = 256


def _split3(x):
    hi = x.astype(BF16)
    r1 = x - hi.astype(F32)
    mid = r1.astype(BF16)
    lo = (r1 - mid.astype(F32)).astype(BF16)
    return hi, mid, lo


def _tri_dot(tri, x):
    hi, mid, lo = _split3(x)
    dot = lambda p: jnp.dot(tri, p, preferred_element_type=F32)
    return dot(hi) + dot(mid) + dot(lo)


def _log_sigmoid(x):
    return jnp.minimum(x, 0.0) - jnp.log(1.0 + jnp.exp(-jnp.abs(x)))


def _fox_cum(f, fb, *, name):
    L, W = f.shape
    t = _tile(L, CUM_T, SUBLANES)

    def body(f_ref, b_ref, o_ref, carry):
        @pl.when(pl.program_id(0) == 0)
        def _():
            carry[...] = jnp.zeros_like(carry)

        row = lax.broadcasted_iota(jnp.int32, (t, t), 0)
        col = lax.broadcasted_iota(jnp.int32, (t, t), 1)
        tri = jnp.where(col <= row, 1.0, 0.0).astype(BF16)
        c = _tri_dot(tri, _log_sigmoid(f_ref[...] + b_ref[...])) + carry[...]
        o_ref[...] = c
        carry[...] = c[t - 1:t, :]

    return pl.pallas_call(
        body, name=name, grid=(L // t,),
        in_specs=[_row_spec(t, W), _vec_spec(W)], out_specs=_row_spec(t, W),
        out_shape=jax.ShapeDtypeStruct((L, W), F32),
        scratch_shapes=[pltpu.VMEM((1, W), F32)], compiler_params=_params("arbitrary"),
    )(f, fb)


def _fox_cum_bwd(dcum, f, fb, *, name):
    L, W = f.shape
    t = _tile(L, CUM_T, SUBLANES)
    n = L // t

    def body(d_ref, f_ref, b_ref, o_ref, db_ref, carry):
        i = pl.program_id(0)

        @pl.when(i == 0)
        def _():
            carry[...] = jnp.zeros_like(carry)
            db_ref[...] = jnp.zeros_like(db_ref)

        row = lax.broadcasted_iota(jnp.int32, (t, t), 0)
        col = lax.broadcasted_iota(jnp.int32, (t, t), 1)
        tri = jnp.where(col >= row, 1.0, 0.0).astype(BF16)
        dlog = _tri_dot(tri, d_ref[...]) + carry[...]
        carry[...] = dlog[0:1, :]
        df = dlog * _sigmoid(-(f_ref[...] + b_ref[...]))
        o_ref[...] = df.astype(BF16)
        db_ref[...] += _colsum8(df)
        _finish_colsums(i, n - 1, (db_ref,))

    rev = pl.BlockSpec((t, W), lambda i: (n - 1 - i, 0))
    return pl.pallas_call(
        body, name=name, grid=(n,),
        in_specs=[rev, rev, _vec_spec(W)], out_specs=[rev, _vec_spec(W, SUBLANES)],
        out_shape=[jax.ShapeDtypeStruct((L, W), BF16), jax.ShapeDtypeStruct((SUBLANES, W), F32)],
        scratch_shapes=[pltpu.VMEM((1, W), F32)], compiler_params=_params("arbitrary"),
    )(dcum, f, fb)


def _head_col(blk, h):
    lane = lax.broadcasted_iota(jnp.int32, blk.shape, 1)
    return jnp.sum(jnp.where(lane == h * HEAD_DIM, blk, 0.0), axis=1, keepdims=True)


def _lo_mask(rows):
    return lax.broadcasted_iota(jnp.int32, (rows, LANES), 1) < HEAD_DIM


def _causal(t):
    row = lax.broadcasted_iota(jnp.int32, (t, t), 0)
    col = lax.broadcasted_iota(jnp.int32, (t, t), 1)
    return col <= row


def _call_with_exchange(body, exchange, *, name, grid, in_specs, out_specs, out_shape, operands):
    sem = ("parallel",) + ("arbitrary",) * (len(grid) - 1)
    if exchange is None:
        return pl.pallas_call(body, name=name, grid=grid, in_specs=in_specs, out_specs=out_specs,
                              out_shape=out_shape, compiler_params=_params(*sem))(*operands)
    n_in, n_out = len(in_specs), len(out_specs)
    ei, eo = len(exchange.operands), len(exchange.out_shapes)

    def wrapped(*refs):
        ins, ex_in = refs[:n_in], refs[n_in:n_in + ei]
        outs, ex_out = refs[n_in + ei:n_in + ei + n_out], refs[n_in + ei + n_out:n_in + ei + n_out + eo]
        sems = refs[n_in + ei + n_out + eo:]
        ids = [pl.program_id(d) for d in range(len(grid))]
        first = functools.reduce(jnp.logical_and, [i == 0 for i in ids])
        last = functools.reduce(jnp.logical_and, [i == g - 1 for i, g in zip(ids, grid)])

        @pl.when(first)
        def _():
            exchange.start(ex_in, ex_out, sems)

        body(*ins, *outs)

        @pl.when(last)
        def _():
            exchange.finish(ex_in, ex_out, sems)

    return pl.pallas_call(
        wrapped, name=name, grid=grid, in_specs=list(in_specs) + [HBM_SPEC] * ei,
        out_specs=list(out_specs) + [HBM_SPEC] * eo, out_shape=list(out_shape) + list(exchange.out_shapes),
        scratch_shapes=exchange.sems, compiler_params=_params(*(("arbitrary",) * len(grid))),
    )(*operands, *exchange.operands)


def _fox_fwd(qkv, cum_cols, cum_rows, *, name, exchange=None):
    L = qkv.shape[0]
    t = _tile(L, ATT_T)
    nq = L // t
    npair = ATTN_HEADS // 2

    def body(q_ref, k_ref, v_ref, cc_ref, cr_ref, o_ref, o32_ref, lse_ref):
        iq = pl.program_id(1)
        lo = _lo_mask(t)
        qv = q_ref[...] * 0.125
        zq = jnp.zeros_like(qv)
        qh = (jnp.where(lo, qv, zq), jnp.where(lo, zq, qv))
        ccv = cc_ref[...]
        cq = (_head_col(ccv, 0), _head_col(ccv, 1))

        def step(ik, carry, masked):
            start = pl.multiple_of(ik * t, t)
            kb = k_ref[pl.ds(start, t), :]
            vb = v_ref[pl.ds(start, t), :]
            out = []
            for h in range(2):
                m, l, acc = carry[h]
                s = lax.dot_general(qh[h], kb, NT, preferred_element_type=F32)
                s = s + (cq[h] - cr_ref[h:h + 1, pl.ds(start, t)])
                if masked:
                    s = jnp.where(_causal(t), s, -jnp.inf)
                m_new = jnp.maximum(m, jnp.max(s, axis=1, keepdims=True))
                alpha = jnp.exp(m - m_new)
                p = jnp.exp(s - m_new)
                l = alpha * l + jnp.sum(p, axis=1, keepdims=True)
                acc = alpha * acc + jnp.dot(p.astype(BF16), vb, preferred_element_type=F32)
                out.append((m_new, l, acc))
            return tuple(out)

        init1 = (jnp.full((t, 1), -jnp.inf, F32), jnp.zeros((t, 1), F32), jnp.zeros((t, LANES), F32))
        carry = lax.fori_loop(0, iq, lambda ik, c: step(ik, c, False), (init1, init1))
        (m0, l0, a0), (m1, l1, a1) = step(iq, carry, True)
        out = jnp.where(lo, a0 / l0, a1 / l1)
        o_ref[...] = out.astype(BF16)
        o32_ref[...] = out
        lse_ref[...] = jnp.where(lo, m0 + jnp.log(l0), m1 + jnp.log(l1))

    blk = lambda off: pl.BlockSpec((t, LANES), lambda hp, iq: (iq, off + hp))
    whole = lambda off: pl.BlockSpec((L, LANES), lambda hp, iq: (0, off + hp))
    return _call_with_exchange(
        body, exchange, name=name, grid=(npair, nq),
        in_specs=[blk(0), whole(npair), whole(2 * npair), blk(0),
                  pl.BlockSpec((None, SUBLANES, L), lambda hp, iq: (hp, 0, 0))],
        out_specs=[blk(0), blk(0), blk(0)],
        out_shape=[jax.ShapeDtypeStruct((L, ATTN_WIDTH), BF16), jax.ShapeDtypeStruct((L, ATTN_WIDTH), F32),
                   jax.ShapeDtypeStruct((L, ATTN_WIDTH), F32)],
        operands=(qkv, qkv, qkv, cum_cols, cum_rows))


STAT_LSE, STAT_CUM, STAT_DELTA = 0, 2, 4


def _lane_col(blk, idx):
    lane = lax.broadcasted_iota(jnp.int32, blk.shape, 1)
    return jnp.sum(jnp.where(lane == idx, blk, 0.0), axis=1, keepdims=True)


def _fox_rowstats(do, o, lse, cum_cols, *, name):
    L = do.shape[0]
    t = _tile(L, ATT_T)

    def body(do_ref, o_ref, lse_ref, cc_ref, st_ref):
        lo = _lo_mask(t)
        dd = do_ref[...].astype(F32) * o_ref[...]
        lsev, ccv = lse_ref[...], cc_ref[...]
        cols = (_head_col(lsev, 0), _head_col(lsev, 1), _head_col(ccv, 0), _head_col(ccv, 1),
                jnp.sum(jnp.where(lo, dd, 0.0), axis=1, keepdims=True),
                jnp.sum(jnp.where(lo, 0.0, dd), axis=1, keepdims=True))
        lane = lax.broadcasted_iota(jnp.int32, (t, LANES), 1)
        out = jnp.zeros((t, LANES), F32)
        for i, col in enumerate(cols):
            out = jnp.where(lane == i, col, out)
        st_ref[...] = out

    blk = pl.BlockSpec((t, LANES), lambda hp, i: (i, hp))
    return pl.pallas_call(
        body, name=name, grid=(ATTN_HEADS // 2, L // t),
        in_specs=[blk, blk, blk, blk], out_specs=blk,
        out_shape=jax.ShapeDtypeStruct((L, ATTN_WIDTH), F32),
        compiler_params=_params("parallel", "parallel"),
    )(do, o, lse, cum_cols)


def _fox_bwd(qkv, do, stats, cum_rows, *, name, exchange=None):
    L = qkv.shape[0]
    t = _tile(L, ATT_T)
    nq = L // t
    npair = ATTN_HEADS // 2

    def body(q_ref, do_ref, st_ref, k_ref, v_ref, cr_ref, dk_ref, dv_ref, dc_ref, dq_ref, drow_ref):
        ik = pl.program_id(1)

        @pl.when(ik == 0)
        def _():
            dq_ref[...] = jnp.zeros_like(dq_ref)
            drow_ref[...] = jnp.zeros_like(drow_ref)

        lo = _lo_mask(t)
        lane = lax.broadcasted_iota(jnp.int32, (t, LANES), 1)
        kb = k_ref[...]
        vb = v_ref[...]
        zk = jnp.zeros_like(kb)
        kh = (jnp.where(lo, kb, zk), jnp.where(lo, zk, kb))
        vh = (jnp.where(lo, vb, zk), jnp.where(lo, zk, vb))
        ck = (cr_ref[0:1, :], cr_ref[1:2, :])

        def step(iq, carry, masked):
            dk, dv, dc0, dc1 = carry
            start = pl.multiple_of(iq * t, t)
            qb = q_ref[pl.ds(start, t), :] * 0.125
            dob = do_ref[pl.ds(start, t), :]
            stb = st_ref[pl.ds(start, t), :]
            dks, dvs, dcs, dqs, rss = [], [], [], [], []
            for h in range(2):
                s = lax.dot_general(qb, kh[h], NT, preferred_element_type=F32)
                s = s + (_lane_col(stb, STAT_CUM + h) - ck[h])
                if masked:
                    s = jnp.where(_causal(t), s, -jnp.inf)
                p = jnp.exp(s - _lane_col(stb, STAT_LSE + h))
                dp = lax.dot_general(dob, vh[h], NT, preferred_element_type=F32)
                ds = p * (dp - _lane_col(stb, STAT_DELTA + h))
                dsb = ds.astype(BF16)
                dvs.append(lax.dot_general(p.astype(BF16), dob, TN, preferred_element_type=F32))
                dks.append(lax.dot_general(dsb, qb, TN, preferred_element_type=F32))
                dqs.append(jnp.dot(dsb, kh[h], preferred_element_type=F32))
                dcs.append(jnp.sum(ds, axis=0, keepdims=True))
                rss.append(jnp.sum(ds, axis=1, keepdims=True))
            dq_ref[pl.ds(start, t), :] += 0.125 * (dqs[0] + dqs[1])
            drow_ref[pl.ds(start, t), :] += jnp.where(lane == 0, rss[0], jnp.where(lane == 1, rss[1], 0.0))
            return (dk + jnp.where(lo, dks[0], dks[1]), dv + jnp.where(lo, dvs[0], dvs[1]),
                    dc0 - dcs[0], dc1 - dcs[1])

        zero = jnp.zeros((t, LANES), F32)
        zrow = jnp.zeros((1, t), F32)
        carry = step(ik, (zero, zero, zrow, zrow), True)
        dk, dv, dc0, dc1 = lax.fori_loop(ik + 1, nq, lambda iq, c: step(iq, c, False), carry)
        dk_ref[...] = dk.astype(BF16)
        dv_ref[...] = dv.astype(BF16)
        dc_ref[...] = jnp.zeros_like(dc_ref)
        dc_ref[0:1, :] = dc0
        dc_ref[1:2, :] = dc1

    whole = lambda off: pl.BlockSpec((L, LANES), lambda hp, ik: (0, off + hp))
    blk = lambda off: pl.BlockSpec((t, LANES), lambda hp, ik: (ik, off + hp))
    rows = pl.BlockSpec((None, SUBLANES, t), lambda hp, ik: (hp, 0, ik))
    return _call_with_exchange(
        body, exchange, name=name, grid=(npair, nq),
        in_specs=[whole(0), whole(0), whole(0), blk(npair), blk(2 * npair), rows],
        out_specs=[blk(0), blk(0), rows, whole(0), whole(0)],
        out_shape=[jax.ShapeDtypeStruct((L, ATTN_WIDTH), BF16), jax.ShapeDtypeStruct((L, ATTN_WIDTH), BF16),
                   jax.ShapeDtypeStruct((npair, SUBLANES, L), F32),
                   jax.ShapeDtypeStruct((L, ATTN_WIDTH), F32), jax.ShapeDtypeStruct((L, ATTN_WIDTH), F32)],
        operands=(qkv, do, stats, qkv, qkv, cum_rows))


def _mod_partial(c_all, mod_w, mod_b_cols, *, name):
    depth, K, cols = mod_w.shape
    tn = _tile(cols, 768)

    def body(c_ref, w_ref, b_ref, o_ref):
        cv = c_ref[...]
        sc = (cv * _sigmoid(cv)).astype(BF16)
        o_ref[...] = jnp.dot(sc, w_ref[...].astype(BF16), preferred_element_type=F32) + b_ref[...]

    return pl.pallas_call(
        body, name=name, grid=(depth, cols // tn),
        in_specs=[pl.BlockSpec((N_DEV, K), lambda l, j: (0, 0)),
                  pl.BlockSpec((None, K, tn), lambda l, j: (l, 0, j)),
                  pl.BlockSpec((None, 1, tn), lambda l, j: (l, 0, j))],
        out_specs=pl.BlockSpec((None, N_DEV, tn), lambda l, j: (l, 0, j)),
        out_shape=jax.ShapeDtypeStruct((depth, N_DEV, cols), F32),
        compiler_params=_params("parallel", "parallel"),
    )(c_all, mod_w, mod_b_cols)


def _mod_wgrad(c_all_t, dmod, *, name):
    depth, nb, cols = dmod.shape
    K = c_all_t.shape[0]
    tn = _tile(cols, 768)
    tk = _tile(K, 256, SUBLANES)

    def body(c_ref, d_ref, o_ref):
        cv = c_ref[...]
        sc = cv * _sigmoid(cv)
        dv = d_ref[...]
        acc = sc[:, 0:1] * dv[0:1, :]
        for b in range(1, nb):
            acc = acc + sc[:, b:b + 1] * dv[b:b + 1, :]
        o_ref[...] = acc

    return pl.pallas_call(
        body, name=name, grid=(depth, K // tk, cols // tn),
        in_specs=[pl.BlockSpec((tk, nb), lambda l, i, j: (i, 0)),
                  pl.BlockSpec((None, nb, tn), lambda l, i, j: (l, 0, j))],
        out_specs=pl.BlockSpec((None, tk, tn), lambda l, i, j: (l, i, j)),
        out_shape=jax.ShapeDtypeStruct((depth, K, cols), F32),
        compiler_params=_params("parallel", "parallel", "parallel"),
    )(c_all_t, dmod)


def _adamw(w, g, m, v, *, name):
    shape = w.shape
    cols = shape[-1]
    rows = int(np.prod(shape[:-1]))
    t = _tile(rows, 256, SUBLANES) if rows % SUBLANES == 0 else rows
    r2 = lambda a: a.reshape(rows, cols)

    def body(w_ref, g_ref, m_ref, v_ref, d_ref, nm_ref, nv_ref):
        gv = g_ref[...]
        nm = ADAM_B1 * m_ref[...] + (1.0 - ADAM_B1) * gv
        nv = ADAM_B2 * v_ref[...] + (1.0 - ADAM_B2) * (gv * gv)
        m_hat = nm / (1.0 - ADAM_B1 ** ADAM_STEP)
        v_hat = nv / (1.0 - ADAM_B2 ** ADAM_STEP)
        d_ref[...] = -ADAM_LR * (m_hat / (jnp.sqrt(v_hat) + ADAM_EPS) + ADAM_WD * w_ref[...])
        nm_ref[...] = nm
        nv_ref[...] = nv

    spec = pl.BlockSpec((t, cols), lambda i: (i, 0))
    sds = jax.ShapeDtypeStruct((rows, cols), F32)
    d, nm, nv = pl.pallas_call(
        body, name=name, grid=(rows // t,),
        in_specs=[spec] * 4, out_specs=[spec] * 3, out_shape=[sds] * 3,
        compiler_params=_params("parallel"),
    )(r2(w), r2(g), r2(m), r2(v))
    return d.reshape(shape), nm.reshape(shape), nv.reshape(shape)


def _my_place():
    return lax.axis_index("x"), lax.axis_index("y"), lax.axis_index("c")


def _other_chips(x, y):
    return [(1 - x, y), (x, 1 - y), (1 - x, 1 - y)]


def _all_gather8(v, *, name, with_sum=False):
    m, n = v.shape

    def body(x_ref, out_ref, *rest):
        if with_sum:
            sum_ref, send_sems, recv_sems, local_sem = rest
        else:
            send_sems, recv_sems, local_sem = rest
        x, y, c = _my_place()
        me, sibling = (x, y, c), (x, y, 1 - c)
        chips = _other_chips(x, y)

        def rows(px, py, pc):
            return out_ref.at[pl.ds((4 * px + 2 * py + pc) * m, m), :]

        def copy(k, block, to, src=None):
            return pltpu.make_async_remote_copy(
                src_ref=rows(*block) if src is None else src, dst_ref=rows(*block),
                send_sem=send_sems.at[k], recv_sem=recv_sems.at[k], device_id=to, device_id_type=MESH)

        mine = pltpu.make_async_copy(x_ref, rows(*me), local_sem)
        mine.start()
        first = [copy(0, me, sibling, src=x_ref)]
        first += [copy(1 + j, me, (*chip, c), src=x_ref) for j, chip in enumerate(chips)]
        for cp in first:
            cp.start()
        passed = [copy(4 + j, (*chip, c), sibling) for j, chip in enumerate(chips)]
        for j, chip in enumerate(chips):
            copy(1 + j, (*chip, c), me).wait_recv()
            passed[j].start()
        copy(0, sibling, me).wait_recv()
        for j, chip in enumerate(chips):
            copy(4 + j, (*chip, 1 - c), me).wait_recv()
        for cp in first + passed:
            cp.wait_send()
        mine.wait()
        if with_sum:
            acc = out_ref[pl.ds(0, m), :]
            for d in range(1, N_DEV):
                acc = acc + out_ref[pl.ds(d * m, m), :]
            sum_ref[...] = acc

    vm = pl.BlockSpec(memory_space=pltpu.VMEM)
    out_shape = [jax.ShapeDtypeStruct((N_DEV * m, n), F32)]
    if with_sum:
        out_shape.append(jax.ShapeDtypeStruct((m, n), F32))
    res = pl.pallas_call(
        body, name=name, out_shape=out_shape, in_specs=[vm], out_specs=[vm] * len(out_shape),
        scratch_shapes=[pltpu.SemaphoreType.DMA((7,)), pltpu.SemaphoreType.DMA((7,)), pltpu.SemaphoreType.DMA],
        compiler_params=pltpu.CompilerParams(vmem_limit_bytes=VMEM_LIMIT),
    )(v)
    return res if with_sum else res[0]


class _Cut(NamedTuple):
    shape: tuple
    slab: int
    half: int


IN_WIDTH = SSM_WIDTH + 3 * ATTN_WIDTH + ATTN_HEADS + 2 * D_MODEL
CUTS = dict(
    ffn_w_in=_Cut((1, D_MODEL, 2 * D_FF), 2, 1),
    ffn_w_out=_Cut((1, D_FF, D_MODEL), 1, 2),
    mix_w_in=_Cut((N_CHIPS, D_MODEL, IN_WIDTH // N_CHIPS), 0, 1),
    glu_w=_Cut((1, SSM_WIDTH, 2 * D_MODEL), 2, 1),
    attn_w_out=_Cut((1, ATTN_WIDTH, D_MODEL), 2, 1),
    mix_w_out=_Cut((1, D_MODEL, D_MODEL), 1, 2),
)
LAYER_MATS = (("ffn_w_in", 0), ("ffn_w_in", 1), ("ffn_w_out", 0), ("ffn_w_out", 1), ("mix_w_in", None),
              ("glu_w", None), ("attn_w_out", None), ("mix_w_out", None))
EARLY_MATS = (0, 2, 4, 5, 6, 7)
LATE_MATS = (1, 3)


def _part_shape(cut, slab=False, half=False):
    s = list(cut.shape)
    if slab:
        s[cut.slab] //= N_CHIPS
    if half:
        s[cut.half] //= 2
    return tuple(s)


def _window(ref, cut, slab=None, half=None):
    idx = [slice(None)] * len(cut.shape)
    for axis, parts, which in ((cut.slab, N_CHIPS, slab), (cut.half, 2, half)):
        if which is not None:
            width = cut.shape[axis] // parts
            idx[axis] = pl.ds(pl.multiple_of(which * width, width), width)
    return ref.at[tuple(idx)]


HBM_SPEC = pl.BlockSpec(memory_space=pltpu.HBM)


class _Exchange(NamedTuple):
    operands: list
    out_shapes: list
    sems: list
    start: object
    finish: object


def _run_exchange(ex, *, name):
    ni, no = len(ex.operands), len(ex.out_shapes)

    def body(*refs):
        parts = (refs[:ni], refs[ni:ni + no], refs[ni + no:])
        ex.start(*parts)
        ex.finish(*parts)

    return pl.pallas_call(
        body, name=name, out_shape=ex.out_shapes, in_specs=[HBM_SPEC] * ni, out_specs=[HBM_SPEC] * no,
        scratch_shapes=ex.sems,
    )(*ex.operands)


def _gather_exchange(shards, cuts):
    n = len(shards)

    def setup(s_refs, f_refs, sems):
        send_sems, recv_sems = sems
        x, y, c = _my_place()
        me, sibling, mine = (x, y, c), (x, y, 1 - c), 2 * x + y
        chips = _other_chips(x, y)

        def copy(i, k, src, dst, to):
            return pltpu.make_async_remote_copy(
                src_ref=src, dst_ref=dst, send_sem=send_sems.at[7 * i + k], recv_sem=recv_sems.at[7 * i + k],
                device_id=to, device_id_type=MESH)

        def landed(i, j, half):
            return _window(f_refs[i], cuts[i], slab=2 * chips[j][0] + chips[j][1], half=half)

        def sends():
            own = [copy(i, 6, s_refs[i], _window(f_refs[i], cuts[i], slab=mine), sibling) for i in range(n)]
            return own + [copy(i, j, _window(s_refs[i], cuts[i], half=c),
                               _window(f_refs[i], cuts[i], slab=mine, half=c), (*chips[j], c))
                          for i in range(n) for j in range(3)]

        return c, me, sibling, copy, landed, sends

    def start(s_refs, f_refs, sems):
        for cp in setup(s_refs, f_refs, sems)[-1]():
            cp.start()

    def finish(s_refs, f_refs, sems):
        c, me, sibling, copy, landed, sends = setup(s_refs, f_refs, sems)
        passed = []
        for i in range(n):
            for j in range(3):
                copy(i, j, landed(i, j, c), landed(i, j, c), me).wait_recv()
                passed.append(copy(i, 3 + j, landed(i, j, c), landed(i, j, c), sibling))
                passed[-1].start()
        for i in range(n):
            for j in range(3):
                copy(i, 3 + j, landed(i, j, 1 - c), landed(i, j, 1 - c), me).wait_recv()
        for i in range(n):
            mine_i = _window(f_refs[i], cuts[i], slab=2 * me[0] + me[1])
            copy(i, 6, mine_i, mine_i, me).wait_recv()
        for cp in sends() + passed:
            cp.wait_send()

    return _Exchange(
        list(shards), [jax.ShapeDtypeStruct(cut.shape, s.dtype) for s, cut in zip(shards, cuts)],
        [pltpu.SemaphoreType.DMA((7 * n,)), pltpu.SemaphoreType.DMA((7 * n,))], start, finish)


def _swap_layer(mats, cuts, *, name):
    n = len(mats)

    def body(*refs):
        m_refs, r_refs = refs[:n], refs[n:2 * n]
        send_sems, recv_sems = refs[2 * n:]
        x, y, c = _my_place()
        cps = [pltpu.make_async_remote_copy(
            src_ref=_window(m_refs[i], cuts[i], half=1 - c), dst_ref=r_refs[i], send_sem=send_sems.at[i],
            recv_sem=recv_sems.at[i], device_id=(x, y, 1 - c), device_id_type=MESH) for i in range(n)]
        for cp in cps:
            cp.start()
        for cp in cps:
            cp.wait()

    return pl.pallas_call(
        body, name=name, out_shape=[jax.ShapeDtypeStruct(_part_shape(cut, half=True), m.dtype)
                                    for m, cut in zip(mats, cuts)],
        in_specs=[HBM_SPEC] * n, out_specs=[HBM_SPEC] * n,
        scratch_shapes=[pltpu.SemaphoreType.DMA((n,)), pltpu.SemaphoreType.DMA((n,))],
    )(*mats)


def _partials_exchange(sums, cuts):
    n = len(sums)

    def copies(s_refs, p_refs, sems):
        send_sems, recv_sems = sems
        x, y, c = _my_place()
        return [pltpu.make_async_remote_copy(
            src_ref=_window(s_refs[i], cuts[i], slab=2 * chip[0] + chip[1]), dst_ref=p_refs[i].at[j],
            send_sem=send_sems.at[3 * i + j], recv_sem=recv_sems.at[3 * i + j],
            device_id=(*chip, c), device_id_type=MESH)
            for i in range(n) for j, chip in enumerate(_other_chips(x, y))]

    def start(s_refs, p_refs, sems):
        for cp in copies(s_refs, p_refs, sems):
            cp.start()

    def finish(s_refs, p_refs, sems):
        for cp in copies(s_refs, p_refs, sems):
            cp.wait()

    return _Exchange(
        list(sums), [jax.ShapeDtypeStruct((3,) + _part_shape(cut, slab=True, half=True), s.dtype)
                     for s, cut in zip(sums, cuts)],
        [pltpu.SemaphoreType.DMA((3 * n,)), pltpu.SemaphoreType.DMA((3 * n,))], start, finish)


def _share_all(dests, cuts, places, *, name):
    names = list(dests)
    nn, n = len(names), len(places)

    def body(*refs):
        o_refs = dict(zip(names, refs[nn:2 * nn]))
        send_sems, recv_sems = refs[2 * nn:]
        x, y, c = _my_place()

        def win(i, half):
            slab_cut = _Cut(_part_shape(cuts[i], slab=True), cuts[i].slab, cuts[i].half)
            return _window(o_refs[places[i][0]].at[places[i][1]], slab_cut, half=half)

        def copy(i, half):
            return pltpu.make_async_remote_copy(
                src_ref=win(i, half), dst_ref=win(i, half), send_sem=send_sems.at[i], recv_sem=recv_sems.at[i],
                device_id=(x, y, 1 - c), device_id_type=MESH)

        for i in range(n):
            copy(i, c).start()
        for i in range(n):
            copy(i, c).wait_send()
            copy(i, 1 - c).wait_recv()

    return pl.pallas_call(
        body, name=name, out_shape=[jax.ShapeDtypeStruct(dests[k].shape, F32) for k in names],
        in_specs=[HBM_SPEC] * nn, out_specs=[HBM_SPEC] * nn,
        input_output_aliases={i: i for i in range(nn)},
        scratch_shapes=[pltpu.SemaphoreType.DMA((n,)), pltpu.SemaphoreType.DMA((n,))],
    )(*[dests[k] for k in names])


def _cut_blocks(shape):
    _, R, C = shape
    tr = _tile(R, 256, 16)
    tc = _tile(C, 2048) if C % LANES == 0 else C
    return (None, tr, tc), (shape[0], R // tr, C // tc)


def _offset_map(axis, blocks):
    def index_map(b, i, j, which):
        idx = [b, i, j]
        idx[axis] = which[0] * blocks[axis] + idx[axis]
        return tuple(idx)
    return index_map


def _add_half(mat, other, cut, c_idx, *, name):
    shape = _part_shape(cut, half=True)
    block, grid = _cut_blocks(shape)

    def body(c_ref, m_ref, o_ref, f_ref, b_ref):
        s = m_ref[...] + o_ref[...]
        f_ref[...] = s
        b_ref[...] = s.astype(BF16)

    plain = pl.BlockSpec(block, lambda b, i, j, which: (b, i, j))
    grid_spec = pltpu.PrefetchScalarGridSpec(
        num_scalar_prefetch=1, grid=grid,
        in_specs=[pl.BlockSpec(block, _offset_map(cut.half, grid)), plain], out_specs=[plain, plain])
    return pl.pallas_call(
        body, name=name, grid_spec=grid_spec,
        out_shape=[jax.ShapeDtypeStruct(shape, F32), jax.ShapeDtypeStruct(shape, BF16)],
        compiler_params=_params("parallel", "parallel", "parallel"),
    )(c_idx, mat, other)


def _sum_slab(own, parts, cut, dest, place, chip_idx, c_idx, *, name):
    shape = _part_shape(cut, slab=True, half=True)
    block, grid = _cut_blocks(shape)
    assert shape[0] == 1

    def body(k_ref, c_ref, o_ref, p_ref, dest_ref, out_ref):
        acc = o_ref[...]
        for j in range(3):
            acc = acc + p_ref[j].astype(F32)
        out_ref[...] = acc

    def own_map(b, i, j, chip, core):
        idx = [b, i, j]
        idx[cut.slab] = chip[0] * grid[cut.slab] + idx[cut.slab]
        return tuple(idx)

    def dest_map(b, i, j, chip, core):
        idx = [b, i, j]
        idx[cut.half] = core[0] * grid[cut.half] + idx[cut.half]
        return tuple(place) + tuple(idx)

    grid_spec = pltpu.PrefetchScalarGridSpec(
        num_scalar_prefetch=2, grid=grid,
        in_specs=[pl.BlockSpec(block, own_map),
                  pl.BlockSpec((3,) + block[1:], lambda b, i, j, chip, core: (0, i, j)),
                  pl.BlockSpec(memory_space=pl.ANY)],
        out_specs=pl.BlockSpec((None,) * len(place) + block, dest_map))
    return pl.pallas_call(
        body, name=name, grid_spec=grid_spec, out_shape=jax.ShapeDtypeStruct(dest.shape, F32),
        input_output_aliases={4: 0},
        compiler_params=_params("parallel", "parallel", "parallel"),
    )(chip_idx, c_idx, own, parts.reshape((3,) + shape[1:]), dest)


def _pad_rows(flat, cols=8 * LANES, align=SUBLANES):
    n = flat.shape[0]
    rows = -(-n // (cols * align)) * align
    return jnp.pad(flat, (0, rows * cols - n)).reshape(rows, cols)


def _row(v):
    return v.reshape(1, -1)


def _ffn_fwd(x, mod, g_pre, g_post, w_in, w_out, tag):
    sh, sc, gate = _row(mod[0]), _row(mod[1]), _row(mod[2])
    h = _prenorm(x, _row(g_pre), sc, sh, name=f"prenorm_{tag}")
    gt, up, act = _mm_swiglu(h, w_in, name=f"swiglu_{tag}")
    y = _mm(act, w_out, name=f"ffn_out_{tag}", tk=D_FF)
    x_out = _postnorm_res(x, y, _row(g_post), gate, FFN_RES, name=f"postnorm_{tag}")
    return x_out, (x, h, gt, up, act, y)


def _ffn_bwd(dxo, saved, mod, g_pre, g_post, w_in, w_out, tag):
    x, h, gt, up, act, y = saved
    sc, gate = _row(mod[1]), _row(mod[2])
    dy, dgate, dgpost = _postnorm_bwd(dxo, y, _row(g_post), gate, FFN_RES, name=f"postnorm_bwd_{tag}")
    dgt, dup = _mm_swiglu_bwd(dy, w_out, gt, up, name=f"swiglu_bwd_{tag}")
    dw_out = _mm(act, dy, ta=True, name=f"dw_out_{tag}", tm=1408, tn=1024, tk=1024)
    dh = _mm(dgt, w_in, tb=True, name=f"dh_gate_{tag}", tk=D_FF)
    dh = _mm(dup, w_in, tb=True, b_k0=D_FF, bias=dh, name=f"dh_up_{tag}", tk=D_FF)
    dwg = _mm(h, dgt, ta=True, name=f"dw_gate_{tag}", tm=1024, tn=1408, tk=1024)
    dwu = _mm(h, dup, ta=True, name=f"dw_up_{tag}", tm=1024, tn=1408, tk=1024)
    dx, dsh, dsc, dgpre = _prenorm_bwd(x, dh, dxo, _row(g_pre), sc, name=f"prenorm_bwd_{tag}")
    dmod = jnp.stack([dsh[0], dsc[0], dgate[0]])
    return dx, dmod, dgpre[0], dgpost[0], jnp.concatenate([dwg, dwu], axis=1), dw_out


def _split_mix_w_in(w):
    u0, q0, f0, g0 = 0, SSM_WIDTH, SSM_WIDTH + 3 * ATTN_WIDTH, SSM_WIDTH + 3 * ATTN_WIDTH + ATTN_HEADS
    w_f = jnp.pad(w[:, f0:g0], ((0, 0), (0, LANES - ATTN_HEADS)))
    return w[:, u0:q0], w[:, q0:f0], w_f, w[:, g0:]


def _mixer_fwd(x, mod, g_pre, g_post, w, ssm, forget_b, tag, exchange=None):
    L = x.shape[0]
    sh, sc, gate = _row(mod[0]), _row(mod[1]), _row(mod[2])
    lam_re, lam_im, bin_re, bin_im, cout_re, cout_im, dskip = ssm
    h = _prenorm(x, _row(g_pre), sc, sh, name=f"prenorm_{tag}")
    u = _mm(h, w["u"], name=f"proj_u_{tag}")
    qkv = _mm(h, w["qkv"], out_dtype=BF16, name=f"proj_qkv_{tag}")
    f = _mm(h, w["f"], name=f"proj_f_{tag}")
    gab = _mm(h, w["gab"], name=f"proj_gab_{tag}")
    u_i = _interleave(u)
    bu_re, bu_im = _mm_bd([u_i], [bin_re, bin_im], name=f"ssm_bu_{tag}")
    s_re, s_im = _ssm_scan_fwd(bu_re, bu_im, lam_re, lam_im, name=f"ssm_scan_{tag}")
    y_ssm = _deinterleave(_mm_bd([s_re, s_im], [cout_re, cout_im], bias=u_i, bscale=dskip, name=f"ssm_y_{tag}"))
    gl = _gelu_fwd(y_ssm, name=f"gelu_{tag}")
    z = _mm(gl, w["glu"], name=f"glu_{tag}")
    fb = jnp.pad(forget_b, (0, LANES - ATTN_HEADS)).reshape(1, LANES)
    cum = _fox_cum(f, fb, name=f"fox_cum_{tag}")
    cum8 = cum[:, :ATTN_HEADS]
    cum_cols = jnp.repeat(cum8, HEAD_DIM, axis=1)
    cum_rows = jnp.pad(cum8.T.reshape(ATTN_HEADS // 2, 2, L), ((0, 0), (0, SUBLANES - 2), (0, 0)))
    attn, attn32, lse, *exchanged = _fox_fwd(qkv, cum_cols, cum_rows, name=f"fox_fwd_{tag}", exchange=exchange)
    yb = _mm(attn, w["attn_out"], name=f"attn_out_{tag}")
    merged = _merge_fwd(z, yb, gab, name=f"merge_{tag}")
    y = _mm(merged, w["out"], name=f"mix_out_{tag}")
    x_out = _postnorm_res(x, y, _row(g_post), gate, 1.0, name=f"postnorm_{tag}")
    saved = (x, h, u, u_i, qkv, f, gab, s_re, s_im, y_ssm, gl, z, fb, cum_cols, cum_rows, attn, attn32, lse, yb,
             merged, y)
    return x_out, saved, exchanged


def _mixer_bwd(dxo, saved, mod, g_pre, g_post, w, ssm, tag, exchange=None):
    (x, h, u, u_i, qkv, f, gab, s_re, s_im, y_ssm, gl, z, fb, cum_cols, cum_rows, attn, attn32, lse, yb,
     merged, y) = saved
    L = x.shape[0]
    sc, gate = _row(mod[1]), _row(mod[2])
    lam_re, lam_im, bin_re, bin_im, cout_re, cout_im, dskip = ssm
    dy, dgate, dgpost = _postnorm_bwd(dxo, y, _row(g_post), gate, 1.0, name=f"postnorm_bwd_{tag}")
    dmerged = _mm(dy, w["out"], tb=True, name=f"dmerged_{tag}")
    dw_out = _mm(merged, dy, ta=True, name=f"dw_mix_out_{tag}", tm=1024, tn=1024)
    dz, dyb, dgab = _merge_bwd(dmerged, z, yb, gab, name=f"merge_bwd_{tag}")
    dgl = _mm(dz, w["glu"], tb=True, name=f"dgl_{tag}", tk=2048)
    dw_glu = _mm(gl, dz, ta=True, name=f"dw_glu_{tag}", tm=512, tn=2048)
    dys, dsk, dd = _gelu_bwd(dgl, y_ssm, u, dskip, name=f"gelu_bwd_{tag}")
    dys, dsk = _interleave(dys), _interleave(dsk)
    d_re, d_im = _mm_bd([dys], [cout_re, cout_im], tb=True, name=f"ssm_ds_{tag}")
    dcout_re = _mm_bd_t(s_re, dys, SSM_BLOCKS, name=f"ssm_dc_re_{tag}")
    dcout_im = _mm_bd_t(s_im, dys, SSM_BLOCKS, name=f"ssm_dc_im_{tag}")
    g_re, g_im, dlam_re, dlam_im = _ssm_scan_bwd(d_re, d_im, s_re, s_im, lam_re, lam_im, name=f"ssm_scan_bwd_{tag}")
    du = _mm_bd([g_re, g_im], [bin_re, bin_im], tb=True, bias=dsk, out_dtype=BF16, name=f"ssm_du_{tag}")
    du = _deinterleave(du)
    dbin_re = _mm_bd_t(u_i, g_re, SSM_BLOCKS, name=f"ssm_db_re_{tag}")
    dbin_im = _mm_bd_t(u_i, g_im, SSM_BLOCKS, name=f"ssm_db_im_{tag}")
    dssm = (dlam_re, dlam_im, dbin_re, dbin_im, dcout_re, dcout_im, _row(dd[0]))
    dattn = _mm(dyb, w["attn_out"], tb=True, out_dtype=BF16, name=f"dattn_{tag}")
    dw_attn = _mm(attn, dyb, ta=True, name=f"dw_attn_out_{tag}", tm=512, tn=1024)
    stats = _fox_rowstats(dattn, attn32, lse, cum_cols, name=f"fox_rowstats_{tag}")
    dk, dv, dcum_rows, dq, drow, *exchanged = _fox_bwd(qkv, dattn, stats, cum_rows, name=f"fox_bwd_{tag}",
                                                       exchange=exchange)
    drow8 = drow.reshape(L, ATTN_HEADS // 2, LANES)[:, :, :2].reshape(L, ATTN_HEADS)
    dcum = drow8 + dcum_rows[:, :2, :].reshape(ATTN_HEADS, L).T
    dcum = jnp.pad(dcum, ((0, 0), (0, LANES - ATTN_HEADS)))
    df, dfb = _fox_cum_bwd(dcum, f, fb, name=f"fox_cum_bwd_{tag}")
    dqkv = jnp.concatenate([dq.astype(BF16), dk, dv], axis=1)
    dh = _mm(dqkv, w["qkv"], tb=True, name=f"dh_qkv_{tag}", tk=1536)
    dh = _mm(du, w["u"], tb=True, bias=dh, name=f"dh_u_{tag}")
    dh = _mm(dgab, w["gab"], tb=True, bias=dh, name=f"dh_gab_{tag}", tk=2048)
    dh = _mm(df, w["f"], tb=True, bias=dh, name=f"dh_f_{tag}")
    dw_u = _mm(h, du, ta=True, name=f"dw_u_{tag}", tm=1024, tn=512)
    dw_qkv = _mm(h, dqkv, ta=True, name=f"dw_qkv_{tag}", tm=1024, tn=1536)
    dw_f = _mm(h, df, ta=True, name=f"dw_f_{tag}", tm=1024)
    dw_gab = _mm(h, dgab, ta=True, name=f"dw_gab_{tag}", tm=1024, tn=1024)
    dw_in = jnp.concatenate([dw_u, dw_qkv, dw_f[:, :ATTN_HEADS], dw_gab], axis=1)
    dx, dsh, dsc, dgpre = _prenorm_bwd(x, dh, dxo, _row(g_pre), sc, name=f"prenorm_bwd_{tag}")
    dmod = jnp.stack([dsh[0], dsc[0], dgate[0]])
    grads = dict(mix_w_in=dw_in, glu_w=dw_glu, attn_w_out=dw_attn, mix_w_out=dw_out,
                 forget_b=dfb[0, :ATTN_HEADS])
    return dx, dmod, dgpre[0], dgpost[0], grads, dssm, exchanged


SSM_NAMES = ("ssm_a_re", "ssm_a_im", "ssm_log_dt", "ssm_b_re", "ssm_b_im", "ssm_c_re", "ssm_c_im", "ssm_d")
SMALL_NAMES = ("forget_b",) + SSM_NAMES
WEIGHT_NAMES = ("mod_w", "mod_b", "norm_pre", "norm_post", "ffn_w_in", "ffn_w_out", "mix_w_in", "forget_b") \
    + SSM_NAMES + ("glu_w", "attn_w_out", "mix_w_out")


def _layer_shards(w, l):
    return [(w[n][l] if j is None else w[n][l, j]).astype(BF16).reshape(_part_shape(CUTS[n], slab=True))
            for n, j in LAYER_MATS]


def _train_step(x, c, target, w, m, v):
    xi, yi, ci = _my_place()
    chip = 2 * xi + yi
    dev = 4 * xi + 2 * yi + ci
    mod_cols = N_SUB * 3 * D_MODEL // N_CHIPS
    norm_cols = D_MODEL // N_CHIPS

    cuts = [CUTS[n] for n, _ in LAYER_MATS]
    shards = [_layer_shards(w, l) for l in range(DEPTH)]
    full = [[None] * len(LAYER_MATS) for _ in range(DEPTH)]

    def gather(keys):
        return _gather_exchange([shards[l][i] for l, i in keys], [cuts[i] for _, i in keys])

    def store(keys, mats):
        for (l, i), a in zip(keys, mats):
            full[l][i] = a

    early = lambda l: [(l, i) for i in EARLY_MATS]
    late = lambda l: [(l, i) for i in LATE_MATS]
    store(early(0), _run_exchange(gather(early(0)), name="gather_weights_first"))
    c_all = _all_gather8(jnp.pad(c, ((0, SUBLANES - 1), (0, 0))), name="gather_c")[::SUBLANES]
    mod_b_cols = lax.dynamic_slice_in_dim(w["mod_b"], chip * mod_cols, mod_cols, axis=1)[:, None, :]
    mod_part = _mod_partial(c_all, w["mod_w"], mod_b_cols, name="mod_partial")
    small_fwd = jnp.concatenate([mod_part.reshape(-1), w["norm_pre"].reshape(-1), w["norm_post"].reshape(-1)])
    n_mod, n_norm = mod_part.size, w["norm_pre"].size
    sf_all = _all_gather8(_pad_rows(small_fwd), name="gather_mod").reshape(N_DEV, -1)
    sf_chips = sf_all[::2]
    mod_all = jnp.concatenate(
        [sf_chips[k, :n_mod].reshape(DEPTH, N_DEV, mod_cols) for k in range(N_CHIPS)], axis=2)
    mod_mine = lax.dynamic_index_in_dim(mod_all, dev, axis=1, keepdims=False).reshape(DEPTH, N_SUB, 3, D_MODEL)
    norm_pre = jnp.concatenate(
        [sf_chips[k, n_mod:n_mod + n_norm].reshape(DEPTH, N_SUB, norm_cols) for k in range(N_CHIPS)], axis=2)
    norm_post = jnp.concatenate(
        [sf_chips[k, n_mod + n_norm:n_mod + 2 * n_norm].reshape(DEPTH, N_SUB, norm_cols) for k in range(N_CHIPS)],
        axis=2)

    saved, layer_w, ssm_prep, ssm_vjp = [], [], [], []
    h = x
    for l in range(DEPTH):
        win0, _, wout0, _, mix_in, glu, attn_out, mix_out = full[l]
        w_u, w_qkv, w_f, w_gab = _split_mix_w_in(mix_in.transpose(1, 0, 2).reshape(D_MODEL, IN_WIDTH))
        lw = dict(
            ffn=[(win0[0], wout0[0]), None],
            mix=dict(u=w_u, qkv=w_qkv, f=w_f, gab=w_gab, glu=glu[0], attn_out=attn_out[0], out=mix_out[0]))
        prep, vjp = jax.vjp(_ssm_discretize, *[w[n][l] for n in SSM_NAMES])
        layer_w.append(lw)
        ssm_prep.append(prep)
        ssm_vjp.append(vjp)
        h, s0 = _ffn_fwd(h, mod_mine[l, 0], norm_pre[l, 0], norm_post[l, 0], *lw["ffn"][0], tag=f"l{l}a")
        coming = late(l) + (early(l + 1) if l + 1 < DEPTH else [])
        h, s1, arrived = _mixer_fwd(h, mod_mine[l, 1], norm_pre[l, 1], norm_post[l, 1], lw["mix"], prep,
                                    w["forget_b"][l], tag=f"l{l}m", exchange=gather(coming))
        store(coming, arrived)
        lw["ffn"][1] = (full[l][1][0], full[l][3][0])
        h, s2 = _ffn_fwd(h, mod_mine[l, 2], norm_pre[l, 2], norm_post[l, 2], *lw["ffn"][1], tag=f"l{l}b")
        saved.append((s0, s1, s2))
    dh, loss8 = _loss_head(h, target, name="loss_head")

    c_idx = ci.reshape(1).astype(jnp.int32)
    chip_idx = chip.reshape(1).astype(jnp.int32)
    places = []
    dests = {n: lax.empty(((DEPTH,) if j is None else (DEPTH, 2)) + _part_shape(CUTS[n], slab=True), F32)
             for n, j in LAYER_MATS}
    g_small = {n: [None] * DEPTH for n in SMALL_NAMES}
    dmod, dnpre, dnpost = [], [], []
    pending = []

    def begin_reduce(l, idx, mats, tag):
        cs = [cuts[i] for i in idx]
        mats = [a.reshape(cut.shape) for a, cut in zip(mats, cs)]
        from_sibling = _swap_layer(mats, cs, name=f"grad_swap_{tag}")
        for k, i in enumerate(idx):
            f, b = _add_half(mats[k], from_sibling[k], cs[k], c_idx, name=f"grad_add_l{l}_{i}")
            pending.append((l, i, f, b))

    def partials():
        return _partials_exchange([p[3] for p in pending], [cuts[p[1]] for p in pending])

    def end_reduce(parts):
        for (l, i, f, _), part in zip(pending, parts):
            n, j = LAYER_MATS[i]
            place = (l,) if j is None else (l, j)
            dests[n] = _sum_slab(f, part, cuts[i], dests[n], place, chip_idx, c_idx, name=f"grad_sum_l{l}_{i}")
            places.append((n, place))
        pending.clear()

    for l in reversed(range(DEPTH)):
        lw = layer_w[l]
        dh, dm2, dp2, dq2, dwin2, dwout2 = _ffn_bwd(dh, saved[l][2], mod_mine[l, 2], norm_pre[l, 2],
                                                    norm_post[l, 2], *lw["ffn"][1], tag=f"l{l}b")
        begin_reduce(l, LATE_MATS, [dwin2, dwout2], f"l{l}b")
        dh, dm1, dp1, dq1, gmix, dssm, parts = _mixer_bwd(dh, saved[l][1], mod_mine[l, 1], norm_pre[l, 1],
                                                          norm_post[l, 1], lw["mix"], ssm_prep[l], tag=f"l{l}m",
                                                          exchange=partials())
        end_reduce(parts)
        dh, dm0, dp0, dq0, dwin0, dwout0 = _ffn_bwd(dh, saved[l][0], mod_mine[l, 0], norm_pre[l, 0],
                                                    norm_post[l, 0], *lw["ffn"][0], tag=f"l{l}a")
        dmod.insert(0, jnp.stack([dm0, dm1, dm2]))
        dnpre.insert(0, jnp.stack([dp0, dp1, dp2]))
        dnpost.insert(0, jnp.stack([dq0, dq1, dq2]))
        g_small["forget_b"][l] = gmix["forget_b"]
        for n, g in zip(SSM_NAMES, ssm_vjp[l](dssm)):
            g_small[n][l] = g
        dmix_in = gmix["mix_w_in"].reshape(D_MODEL, N_CHIPS, IN_WIDTH // N_CHIPS).transpose(1, 0, 2)
        begin_reduce(l, EARLY_MATS, [dwin0, dwout0, dmix_in, gmix["glu_w"], gmix["attn_w_out"], gmix["mix_w_out"]],
                     f"l{l}a")
    end_reduce(_run_exchange(partials(), name="grad_partials_last"))
    grad_x = dh
    g_small = {n: jnp.stack(g) for n, g in g_small.items()}

    small = [loss8[0, :1], jnp.stack(dmod).reshape(-1), jnp.stack(dnpre).reshape(-1), jnp.stack(dnpost).reshape(-1)]
    small += [g_small[n].reshape(-1) for n in SMALL_NAMES]
    sizes = [int(s.size) for s in small]
    offs = np.concatenate([[0], np.cumsum(sizes)])
    sb_all, sb_sum = _all_gather8(_pad_rows(jnp.concatenate(small)), name="gather_small_grads", with_sum=True)
    sb_sum = sb_sum.reshape(-1)
    take = lambda i: sb_sum[int(offs[i]):int(offs[i + 1])]
    loss = take(0)[0]
    grads = {"mod_b": take(1).reshape(DEPTH, N_SUB * 3 * D_MODEL)}
    dnorm_pre_full = take(2).reshape(DEPTH, N_SUB, D_MODEL)
    dnorm_post_full = take(3).reshape(DEPTH, N_SUB, D_MODEL)
    grads["norm_pre"] = lax.dynamic_slice_in_dim(dnorm_pre_full, chip * norm_cols, norm_cols, axis=2)
    grads["norm_post"] = lax.dynamic_slice_in_dim(dnorm_post_full, chip * norm_cols, norm_cols, axis=2)
    for i, n in enumerate(SMALL_NAMES):
        grads[n] = take(4 + i).reshape(w[n].shape)
    dmod_all = sb_all.reshape(N_DEV, -1)[:, int(offs[1]):int(offs[2])].reshape(N_DEV, DEPTH, N_SUB * 3 * D_MODEL)
    dmod_cols = lax.dynamic_slice_in_dim(dmod_all, chip * mod_cols, mod_cols, axis=2).transpose(1, 0, 2)
    grads["mod_w"] = _mod_wgrad(c_all.T, dmod_cols, name="mod_wgrad")

    shared = _share_all(dests, [CUTS[n] for n, _ in places], places, name="grad_share")
    for n, g in zip(dests, shared):
        grads[n] = g.reshape(w[n].shape)

    delta, new_m, new_v = {}, {}, {}
    for n in WEIGHT_NAMES:
        delta[n], new_m[n], new_v[n] = _adamw(w[n], grads[n], m[n], v[n], name=f"adamw_{n}")
    outs = [loss, grad_x[None]]
    for group in (grads, delta, new_m, new_v):
        outs += [group[n] for n in WEIGHT_NAMES]
    return tuple(outs)


def kernel(x, c, mod_w, mod_b, norm_pre, norm_post, ffn_w_in, ffn_w_out, mix_w_in, forget_b, ssm_a_re, ssm_a_im, ssm_log_dt, ssm_b_re, ssm_b_im, ssm_c_re, ssm_c_im, ssm_d, glu_w, attn_w_out, mix_w_out, loss_target, m_mod_w, m_mod_b, m_norm_pre, m_norm_post, m_ffn_w_in, m_ffn_w_out, m_mix_w_in, m_forget_b, m_ssm_a_re, m_ssm_a_im, m_ssm_log_dt, m_ssm_b_re, m_ssm_b_im, m_ssm_c_re, m_ssm_c_im, m_ssm_d, m_glu_w, m_attn_w_out, m_mix_w_out, v_mod_w, v_mod_b, v_norm_pre, v_norm_post, v_ffn_w_in, v_ffn_w_out, v_mix_w_in, v_forget_b, v_ssm_a_re, v_ssm_a_im, v_ssm_log_dt, v_ssm_b_re, v_ssm_b_im, v_ssm_c_re, v_ssm_c_im, v_ssm_d, v_glu_w, v_attn_w_out, v_mix_w_out):
    w = dict(mod_w=mod_w, mod_b=mod_b, norm_pre=norm_pre, norm_post=norm_post, ffn_w_in=ffn_w_in,
             ffn_w_out=ffn_w_out, mix_w_in=mix_w_in, forget_b=forget_b, ssm_a_re=ssm_a_re, ssm_a_im=ssm_a_im,
             ssm_log_dt=ssm_log_dt, ssm_b_re=ssm_b_re, ssm_b_im=ssm_b_im, ssm_c_re=ssm_c_re, ssm_c_im=ssm_c_im,
             ssm_d=ssm_d, glu_w=glu_w, attn_w_out=attn_w_out, mix_w_out=mix_w_out)
    m = dict(mod_w=m_mod_w, mod_b=m_mod_b, norm_pre=m_norm_pre, norm_post=m_norm_post, ffn_w_in=m_ffn_w_in,
             ffn_w_out=m_ffn_w_out, mix_w_in=m_mix_w_in, forget_b=m_forget_b, ssm_a_re=m_ssm_a_re,
             ssm_a_im=m_ssm_a_im, ssm_log_dt=m_ssm_log_dt, ssm_b_re=m_ssm_b_re, ssm_b_im=m_ssm_b_im,
             ssm_c_re=m_ssm_c_re, ssm_c_im=m_ssm_c_im, ssm_d=m_ssm_d, glu_w=m_glu_w, attn_w_out=m_attn_w_out,
             mix_w_out=m_mix_w_out)
    v = dict(mod_w=v_mod_w, mod_b=v_mod_b, norm_pre=v_norm_pre, norm_post=v_norm_post, ffn_w_in=v_ffn_w_in,
             ffn_w_out=v_ffn_w_out, mix_w_in=v_mix_w_in, forget_b=v_forget_b, ssm_a_re=v_ssm_a_re,
             ssm_a_im=v_ssm_a_im, ssm_log_dt=v_ssm_log_dt, ssm_b_re=v_ssm_b_re, ssm_b_im=v_ssm_b_im,
             ssm_c_re=v_ssm_c_re, ssm_c_im=v_ssm_c_im, ssm_d=v_ssm_d, glu_w=v_glu_w, attn_w_out=v_attn_w_out,
             mix_w_out=v_mix_w_out)
    return _train_step(x[0], c, loss_target[0], w, m, v)
```

```python
import functools
import math
from typing import NamedTuple

import jax
import jax.numpy as jnp
import numpy as np
from jax import lax
from jax.experimental import pallas as pl
from jax.experimental.pallas import tpu as pltpu

F32 = jnp.float32
BF16 = jnp.bfloat16

D_MODEL = 1024
DEPTH = 2
SSM_WIDTH = 512
SSM_GROUP = 16
SSM_GROUPS = 32
SSM_STATE = 64
SSM_FLAT = SSM_GROUPS * SSM_STATE
SSM_BLOCKS = 4
ATTN_HEADS = 8
HEAD_DIM = 64
ATTN_WIDTH = 512
D_FF = 2816
FFN_RES = 0.5
N_SUB = 3
RMS_EPS = 1e-6
N_CHIPS = 4
N_DEV = 8

ADAM_LR = 0.001
ADAM_B1 = 0.9
ADAM_B2 = 0.999
ADAM_EPS = 1e-08
ADAM_WD = 0.01
ADAM_STEP = 10

LANES = 128
SUBLANES = 8
VMEM_LIMIT = 52 * 1024 * 1024
MESH = pl.DeviceIdType.MESH

NN = (((1,), (0,)), ((), ()))
NT = (((1,), (1,)), ((), ()))
TN = (((0,), (0,)), ((), ()))


def _tile(dim, target, align=LANES):
    best = None
    t = align
    while t <= min(dim, target):
        if dim % t == 0:
            best = t
        t += align
    return dim if best is None else best


def _params(*sem):
    return pltpu.CompilerParams(dimension_semantics=sem, vmem_limit_bytes=VMEM_LIMIT)


def _mm(a, b, *, name, ta=False, tb=False, out_dtype=F32, bias=None, bscale=None, b_k0=0,
        tm=512, tn=1024, tk=1024):
    M, K = (a.shape[1], a.shape[0]) if ta else a.shape
    N = b.shape[0] if tb else b.shape[1]
    assert b_k0 + K <= (b.shape[1] if tb else b.shape[0]), (a.shape, b.shape, ta, tb)
    tm, tn, tk = _tile(M, tm), _tile(N, tn), _tile(K, tk)
    nk = K // tk
    assert b_k0 % tk == 0
    kb0 = b_k0 // tk
    dn = (((0 if ta else 1,), (1 if tb else 0,)), ((), ()))
    has_bias, has_scale = bias is not None, bscale is not None

    def body(*refs):
        a_ref, b_ref = refs[0], refs[1]
        pos = 2
        bias_ref = scale_ref = None
        if has_bias:
            bias_ref = refs[pos]
            pos += 1
        if has_scale:
            scale_ref = refs[pos]
            pos += 1
        o_ref = refs[pos]
        acc_ref = refs[pos + 1] if nk > 1 else None

        def finish(r):
            if has_bias:
                extra = bias_ref[...].astype(F32)
                if has_scale:
                    extra = extra * scale_ref[...]
                r = r + extra
            o_ref[...] = r.astype(out_dtype)

        part = lax.dot_general(a_ref[...].astype(BF16), b_ref[...].astype(BF16), dn,
                               preferred_element_type=F32)
        if nk == 1:
            finish(part)
        else:
            k = pl.program_id(2)

            @pl.when(k == 0)
            def _():
                acc_ref[...] = part

            @pl.when(k > 0)
            def _():
                acc_ref[...] += part

            @pl.when(k == nk - 1)
            def _():
                finish(acc_ref[...])

    a_spec = pl.BlockSpec((tk, tm), lambda j, i, k: (k, i)) if ta else pl.BlockSpec((tm, tk), lambda j, i, k: (i, k))
    b_spec = (pl.BlockSpec((tn, tk), lambda j, i, k: (j, kb0 + k)) if tb
              else pl.BlockSpec((tk, tn), lambda j, i, k: (kb0 + k, j)))
    in_specs = [a_spec, b_spec]
    args = [a, b]
    if has_bias:
        in_specs.append(pl.BlockSpec((tm, tn), lambda j, i, k: (i, j)))
        args.append(bias)
    if has_scale:
        in_specs.append(pl.BlockSpec((1, tn), lambda j, i, k: (0, j)))
        args.append(bscale)
    return pl.pallas_call(
        body, name=name,
        grid=(N // tn, M // tm, nk),
        in_specs=in_specs,
        out_specs=pl.BlockSpec((tm, tn), lambda j, i, k: (i, j)),
        out_shape=jax.ShapeDtypeStruct((M, N), out_dtype),
        scratch_shapes=[pltpu.VMEM((tm, tn), F32)] if nk > 1 else [],
        compiler_params=_params("parallel", "parallel", "arbitrary"),
    )(*args)


def _mm_bd(a_list, b_list, *, name, tb=False, out_dtype=F32, bias=None, bscale=None, tm=1024):
    G = b_list[0].shape[0]
    Kb, Nb = (b_list[0].shape[2], b_list[0].shape[1]) if tb else b_list[0].shape[1:]
    M = a_list[0].shape[0]
    tm = _tile(M, tm)
    na, nb = len(a_list), len(b_list)
    n_out = nb if na == 1 else 1
    dn = NT if tb else NN
    has_bias, has_scale = bias is not None, bscale is not None

    def body(*refs):
        a_refs, b_refs = refs[:na], refs[na:na + nb]
        pos = na + nb
        bias_ref = scale_ref = None
        if has_bias:
            bias_ref = refs[pos]
            pos += 1
        if has_scale:
            scale_ref = refs[pos]
            pos += 1
        o_refs = refs[pos:]
        prods = [lax.dot_general(a_refs[min(i, na - 1)][...].astype(BF16), b_refs[i][...].astype(BF16), dn,
                                 preferred_element_type=F32) for i in range(nb)]
        outs = prods if n_out == nb else [functools.reduce(jnp.add, prods)]
        for o_ref, r in zip(o_refs, outs):
            if has_bias:
                extra = bias_ref[...].astype(F32)
                r = r + (extra * scale_ref[...] if has_scale else extra)
            o_ref[...] = r.astype(out_dtype)

    a_spec = pl.BlockSpec((tm, Kb), lambda g, i: (i, g))
    b_spec = pl.BlockSpec((None,) + b_list[0].shape[1:], lambda g, i: (g, 0, 0))
    o_spec = pl.BlockSpec((tm, Nb), lambda g, i: (i, g))
    in_specs = [a_spec] * na + [b_spec] * nb
    args = list(a_list) + list(b_list)
    if has_bias:
        in_specs.append(o_spec)
        args.append(bias)
    if has_scale:
        in_specs.append(pl.BlockSpec((1, Nb), lambda g, i: (0, g)))
        args.append(bscale)
    sds = jax.ShapeDtypeStruct((M, G * Nb), out_dtype)
    res = pl.pallas_call(
        body, name=name, grid=(G, M // tm), in_specs=in_specs, out_specs=[o_spec] * n_out,
        out_shape=[sds] * n_out, compiler_params=_params("parallel", "parallel"),
    )(*args)
    return res[0] if n_out == 1 else res


def _mm_bd_t(a, b, G, *, name, tk=1024):
    K, Mb, Nb = a.shape[0], a.shape[1] // G, b.shape[1] // G
    tk = _tile(K, tk)
    nk = K // tk

    def body(a_ref, b_ref, o_ref, acc_ref):
        k = pl.program_id(1)
        part = lax.dot_general(a_ref[...].astype(BF16), b_ref[...].astype(BF16), TN, preferred_element_type=F32)

        @pl.when(k == 0)
        def _():
            acc_ref[...] = part

        @pl.when(k > 0)
        def _():
            acc_ref[...] += part

        @pl.when(k == nk - 1)
        def _():
            o_ref[...] = acc_ref[...]

    return pl.pallas_call(
        body, name=name, grid=(G, nk),
        in_specs=[pl.BlockSpec((tk, Mb), lambda g, k: (k, g)), pl.BlockSpec((tk, Nb), lambda g, k: (k, g))],
        out_specs=pl.BlockSpec((None, Mb, Nb), lambda g, k: (g, 0, 0)),
        out_shape=jax.ShapeDtypeStruct((G, Mb, Nb), F32),
        scratch_shapes=[pltpu.VMEM((Mb, Nb), F32)], compiler_params=_params("parallel", "arbitrary"),
    )(a, b)


def _sigmoid(x):
    return 0.5 * jnp.tanh(0.5 * x) + 0.5


def _mm_swiglu(h, w_in, *, name, tm=512, tn=1408):
    M, K = h.shape
    N = w_in.shape[1] // 2
    tm, tn = _tile(M, tm), _tile(N, tn)
    nj = N // tn

    def body(h_ref, wg_ref, wu_ref, g_ref, u_ref, a_ref):
        hv = h_ref[...]
        g = jnp.dot(hv, wg_ref[...], preferred_element_type=F32)
        u = jnp.dot(hv, wu_ref[...], preferred_element_type=F32)
        g_ref[...] = g.astype(BF16)
        u_ref[...] = u.astype(BF16)
        a_ref[...] = (g * _sigmoid(g) * u).astype(BF16)

    o_spec = pl.BlockSpec((tm, tn), lambda j, i: (i, j))
    sds = jax.ShapeDtypeStruct((M, N), BF16)
    return pl.pallas_call(
        body, name=name, grid=(nj, M // tm),
        in_specs=[pl.BlockSpec((tm, K), lambda j, i: (i, 0)), pl.BlockSpec((K, tn), lambda j, i: (0, j)),
                  pl.BlockSpec((K, tn), lambda j, i: (0, nj + j))],
        out_specs=[o_spec, o_spec, o_spec], out_shape=[sds, sds, sds],
        compiler_params=_params("parallel", "parallel"),
    )(h, w_in, w_in)


def _mm_swiglu_bwd(dy, w_out, gate, up, *, name, tm=512, tn=1408):
    M, K = dy.shape
    N = w_out.shape[0]
    tm, tn = _tile(M, tm), _tile(N, tn)

    def body(dy_ref, w_ref, g_ref, u_ref, dg_ref, du_ref):
        dact = lax.dot_general(dy_ref[...], w_ref[...], NT, preferred_element_type=F32)
        g = g_ref[...].astype(F32)
        u = u_ref[...].astype(F32)
        sig = _sigmoid(g)
        dg_ref[...] = (dact * u * (sig * (1.0 + g * (1.0 - sig)))).astype(BF16)
        du_ref[...] = (dact * (g * sig)).astype(BF16)

    t_spec = pl.BlockSpec((tm, tn), lambda j, i: (i, j))
    sds = jax.ShapeDtypeStruct((M, N), BF16)
    return pl.pallas_call(
        body, name=name, grid=(N // tn, M // tm),
        in_specs=[pl.BlockSpec((tm, K), lambda j, i: (i, 0)), pl.BlockSpec((tn, K), lambda j, i: (j, 0)),
                  t_spec, t_spec],
        out_specs=[t_spec, t_spec], out_shape=[sds, sds],
        compiler_params=_params("parallel", "parallel"),
    )(dy, w_out, gate, up)


ROW_TILE = 256


def _colsum8(v):
    return jnp.sum(v.reshape(v.shape[0] // SUBLANES, SUBLANES, v.shape[1]), axis=0)


def _finish_colsums(step, last, refs):
    @pl.when(step == last)
    def _():
        for r in refs:
            r[...] = jnp.broadcast_to(jnp.sum(r[...], axis=0, keepdims=True), r.shape)


def _row_spec(t, d):
    return pl.BlockSpec((t, d), lambda i: (i, 0))


def _vec_spec(d, rows=1):
    return pl.BlockSpec((rows, d), lambda i: (0, 0))


def _prenorm(x, g, sc, sh, *, name):
    L, D = x.shape
    t = _tile(L, ROW_TILE, SUBLANES)

    def body(x_ref, g_ref, sc_ref, sh_ref, h_ref):
        xv = x_ref[...]
        r = lax.rsqrt(jnp.mean(xv * xv, axis=-1, keepdims=True) + RMS_EPS)
        h_ref[...] = (((xv * r) * g_ref[...]) * (1.0 + sc_ref[...]) + sh_ref[...]).astype(BF16)

    return pl.pallas_call(
        body, name=name, grid=(L // t,),
        in_specs=[_row_spec(t, D), _vec_spec(D), _vec_spec(D), _vec_spec(D)],
        out_specs=_row_spec(t, D), out_shape=jax.ShapeDtypeStruct((L, D), BF16),
        compiler_params=_params("parallel"),
    )(x, g, sc, sh)


def _postnorm_res(x, y, g, gate, res_w, *, name):
    L, D = x.shape
    t = _tile(L, ROW_TILE, SUBLANES)

    def body(x_ref, y_ref, g_ref, gate_ref, o_ref):
        yv = y_ref[...]
        r = lax.rsqrt(jnp.mean(yv * yv, axis=-1, keepdims=True) + RMS_EPS)
        o_ref[...] = x_ref[...] + (res_w * gate_ref[...]) * ((yv * r) * g_ref[...])

    return pl.pallas_call(
        body, name=name, grid=(L // t,),
        in_specs=[_row_spec(t, D), _row_spec(t, D), _vec_spec(D), _vec_spec(D)],
        out_specs=_row_spec(t, D), out_shape=jax.ShapeDtypeStruct((L, D), F32),
        compiler_params=_params("parallel"),
    )(x, y, g, gate)


def _postnorm_bwd(dxo, y, g, gate, res_w, *, name):
    L, D = y.shape
    t = _tile(L, ROW_TILE, SUBLANES)
    n = L // t

    def body(dxo_ref, y_ref, g_ref, gate_ref, dy_ref, dgate_ref, dg_ref):
        i = pl.program_id(0)

        @pl.when(i == 0)
        def _():
            dgate_ref[...] = jnp.zeros_like(dgate_ref)
            dg_ref[...] = jnp.zeros_like(dg_ref)

        yv = y_ref[...]
        dv = dxo_ref[...]
        gv = g_ref[...]
        r = lax.rsqrt(jnp.mean(yv * yv, axis=-1, keepdims=True) + RMS_EPS)
        yn = yv * r
        dgate_ref[...] += _colsum8(dv * (res_w * (yn * gv)))
        do = dv * (res_w * gate_ref[...])
        dg_ref[...] += _colsum8(do * yn)
        dyn = do * gv
        dy_ref[...] = (r * (dyn - yn * jnp.mean(dyn * yn, axis=-1, keepdims=True))).astype(BF16)
        _finish_colsums(i, n - 1, (dgate_ref, dg_ref))

    sum_sds = jax.ShapeDtypeStruct((SUBLANES, D), F32)
    return pl.pallas_call(
        body, name=name, grid=(n,),
        in_specs=[_row_spec(t, D), _row_spec(t, D), _vec_spec(D), _vec_spec(D)],
        out_specs=[_row_spec(t, D), _vec_spec(D, SUBLANES), _vec_spec(D, SUBLANES)],
        out_shape=[jax.ShapeDtypeStruct((L, D), BF16), sum_sds, sum_sds],
        compiler_params=_params("arbitrary"),
    )(dxo, y, g, gate)


def _prenorm_bwd(x, dh, dxres, g, sc, *, name):
    L, D = x.shape
    t = _tile(L, ROW_TILE, SUBLANES)
    n = L // t

    def body(x_ref, dh_ref, dxr_ref, g_ref, sc_ref, dx_ref, dsh_ref, dsc_ref, dg_ref):
        i = pl.program_id(0)

        @pl.when(i == 0)
        def _():
            dsh_ref[...] = jnp.zeros_like(dsh_ref)
            dsc_ref[...] = jnp.zeros_like(dsc_ref)
            dg_ref[...] = jnp.zeros_like(dg_ref)

        xv = x_ref[...]
        dhv = dh_ref[...].astype(F32)
        gv = g_ref[...]
        one_sc = 1.0 + sc_ref[...]
        r = lax.rsqrt(jnp.mean(xv * xv, axis=-1, keepdims=True) + RMS_EPS)
        xn = xv * r
        tt = dhv * xn
        dsh_ref[...] += _colsum8(dhv)
        dsc_ref[...] += _colsum8(tt * gv)
        dg_ref[...] += _colsum8(tt * one_sc)
        dxn = dhv * (gv * one_sc)
        dx_ref[...] = dxr_ref[...] + r * (dxn - xn * jnp.mean(dxn * xn, axis=-1, keepdims=True))
        _finish_colsums(i, n - 1, (dsh_ref, dsc_ref, dg_ref))

    sum_sds = jax.ShapeDtypeStruct((SUBLANES, D), F32)
    sum_spec = _vec_spec(D, SUBLANES)
    return pl.pallas_call(
        body, name=name, grid=(n,),
        in_specs=[_row_spec(t, D), _row_spec(t, D), _row_spec(t, D), _vec_spec(D), _vec_spec(D)],
        out_specs=[_row_spec(t, D), sum_spec, sum_spec, sum_spec],
        out_shape=[jax.ShapeDtypeStruct((L, D), F32), sum_sds, sum_sds, sum_sds],
        compiler_params=_params("arbitrary"),
    )(x, dh, dxres, g, sc)


def _loss_head(y, target, *, name):
    L, D = y.shape
    t = _tile(L, ROW_TILE, SUBLANES)
    n = L // t

    def body(y_ref, t_ref, dy_ref, loss_ref):
        i = pl.program_id(0)

        @pl.when(i == 0)
        def _():
            loss_ref[...] = jnp.zeros_like(loss_ref)

        e = y_ref[...] - t_ref[...]
        dy_ref[...] = e * (1.0 / D)
        part = jnp.sum(jnp.mean(e * e, axis=-1, keepdims=True), axis=0, keepdims=True)
        loss_ref[...] += jnp.broadcast_to(0.5 * part, loss_ref.shape)

    return pl.pallas_call(
        body, name=name, grid=(n,),
        in_specs=[_row_spec(t, D), _row_spec(t, D)],
        out_specs=[_row_spec(t, D), pl.BlockSpec((SUBLANES, LANES), lambda i: (0, 0))],
        out_shape=[jax.ShapeDtypeStruct((L, D), F32), jax.ShapeDtypeStruct((SUBLANES, LANES), F32)],
        compiler_params=_params("arbitrary"),
    )(y, target)


GELU_C = math.sqrt(2.0 / math.pi)


def _gelu_fwd(y, *, name):
    L, W = y.shape
    t = _tile(L, 512, SUBLANES)

    def body(y_ref, o_ref):
        v = y_ref[...]
        o_ref[...] = (0.5 * v * (1.0 + jnp.tanh(GELU_C * (v + 0.044715 * (v * v * v))))).astype(BF16)

    return pl.pallas_call(
        body, name=name, grid=(L // t,), in_specs=[_row_spec(t, W)], out_specs=_row_spec(t, W),
        out_shape=jax.ShapeDtypeStruct((L, W), BF16), compiler_params=_params("parallel"),
    )(y)


def _gelu_bwd(dgl, y, u, dskip, *, name):
    L, W = y.shape
    t = _tile(L, 512, SUBLANES)
    n = L // t

    def body(dgl_ref, y_ref, u_ref, d_ref, dy_ref, sk_ref, dd_ref):
        i = pl.program_id(0)

        @pl.when(i == 0)
        def _():
            dd_ref[...] = jnp.zeros_like(dd_ref)

        v = y_ref[...]
        inner = GELU_C * (v + 0.044715 * (v * v * v))
        th = jnp.tanh(inner)
        dgelu = 0.5 * (1.0 + th) + 0.5 * v * (1.0 - th * th) * (GELU_C * (1.0 + 3.0 * 0.044715 * (v * v)))
        dy = dgl_ref[...] * dgelu
        dy_ref[...] = dy.astype(BF16)
        sk_ref[...] = dy * d_ref[...]
        dd_ref[...] += _colsum8(dy * u_ref[...])
        _finish_colsums(i, n - 1, (dd_ref,))

    return pl.pallas_call(
        body, name=name, grid=(n,),
        in_specs=[_row_spec(t, W), _row_spec(t, W), _row_spec(t, W), _vec_spec(W)],
        out_specs=[_row_spec(t, W), _row_spec(t, W), _vec_spec(W, SUBLANES)],
        out_shape=[jax.ShapeDtypeStruct((L, W), BF16), jax.ShapeDtypeStruct((L, W), F32),
                   jax.ShapeDtypeStruct((SUBLANES, W), F32)],
        compiler_params=_params("arbitrary"),
    )(dgl, y, u, dskip)


def _merge_fwd(z, yb, gab, *, name):
    L, D = yb.shape
    t = _tile(L, ROW_TILE, SUBLANES)

    def body(z_ref, yb_ref, gab_ref, o_ref):
        ya = z_ref[:, :D] * _sigmoid(z_ref[:, D:])
        o_ref[...] = (_sigmoid(gab_ref[:, :D]) * ya + _sigmoid(gab_ref[:, D:]) * yb_ref[...]).astype(BF16)

    return pl.pallas_call(
        body, name=name, grid=(L // t,),
        in_specs=[_row_spec(t, 2 * D), _row_spec(t, D), _row_spec(t, 2 * D)],
        out_specs=_row_spec(t, D), out_shape=jax.ShapeDtypeStruct((L, D), BF16),
        compiler_params=_params("parallel"),
    )(z, yb, gab)


def _merge_bwd(dm, z, yb, gab, *, name):
    L, D = yb.shape
    t = _tile(L, ROW_TILE, SUBLANES)

    def body(dm_ref, z_ref, yb_ref, gab_ref, dz_ref, dyb_ref, dgab_ref):
        dmv = dm_ref[...]
        zv = z_ref[:, :D]
        sz = _sigmoid(z_ref[:, D:])
        sa = _sigmoid(gab_ref[:, :D])
        sb = _sigmoid(gab_ref[:, D:])
        ybv = yb_ref[...]
        dya = dmv * sa
        dz_ref[:, :D] = (dya * sz).astype(BF16)
        dz_ref[:, D:] = (dya * zv * (sz * (1.0 - sz))).astype(BF16)
        dyb_ref[...] = (dmv * sb).astype(BF16)
        dgab_ref[:, :D] = (dmv * (zv * sz) * (sa * (1.0 - sa))).astype(BF16)
        dgab_ref[:, D:] = (dmv * ybv * (sb * (1.0 - sb))).astype(BF16)

    return pl.pallas_call(
        body, name=name, grid=(L // t,),
        in_specs=[_row_spec(t, D), _row_spec(t, 2 * D), _row_spec(t, D), _row_spec(t, 2 * D)],
        out_specs=[_row_spec(t, 2 * D), _row_spec(t, D), _row_spec(t, 2 * D)],
        out_shape=[jax.ShapeDtypeStruct((L, 2 * D), BF16), jax.ShapeDtypeStruct((L, D), BF16),
                   jax.ShapeDtypeStruct((L, 2 * D), BF16)],
        compiler_params=_params("parallel"),
    )(dm, z, yb, gab)


SCAN_W = 1024
SCAN_T = 512


def _interleave(x):
    L, W = x.shape
    seg = SCAN_T // SUBLANES
    return x.reshape(L // SCAN_T, SUBLANES, seg, W).transpose(0, 2, 1, 3).reshape(L, W)


def _deinterleave(x):
    L, W = x.shape
    seg = SCAN_T // SUBLANES
    return x.reshape(L // SCAN_T, seg, SUBLANES, W).transpose(0, 2, 1, 3).reshape(L, W)


def _power_table(a, b, pr_tab, pi_tab, n):
    def fill(k, carry):
        pr, pi = carry
        pr_tab[k] = pr
        pi_tab[k] = pi
        return a * pr - b * pi, a * pi + b * pr

    lax.fori_loop(0, n, fill, (a, b))


def _rows_to_tile(rows):
    w = rows[0].shape[1]
    sub = lax.broadcasted_iota(jnp.int32, (SUBLANES, w), 0)
    tile = jnp.broadcast_to(rows[0], (SUBLANES, w))
    for j in range(1, SUBLANES):
        tile = jnp.where(sub == j, jnp.broadcast_to(rows[j], (SUBLANES, w)), tile)
    return tile


def _ssm_scan_fwd(bu_re, bu_im, lam_re, lam_im, *, name):
    L, S = bu_re.shape
    w, t = _tile(S, SCAN_W), SCAN_T
    seg = t // SUBLANES

    def body(br_ref, bi_ref, lr_ref, li_ref, sr_ref, si_ref, pr_tab, pi_tab, cr_ref, ci_ref):
        a = jnp.broadcast_to(lr_ref[...], (SUBLANES, w))
        b = jnp.broadcast_to(li_ref[...], (SUBLANES, w))

        @pl.when(pl.program_id(1) == 0)
        def _():
            cr_ref[...] = jnp.zeros_like(cr_ref)
            ci_ref[...] = jnp.zeros_like(ci_ref)
            _power_table(a, b, pr_tab, pi_tab, seg)

        def local_scan(i, carry):
            sr, si = carry
            base = pl.multiple_of(i * SUBLANES, SUBLANES)
            nr = a * sr - b * si + br_ref[pl.ds(base, SUBLANES), :]
            ni = a * si + b * sr + bi_ref[pl.ds(base, SUBLANES), :]
            sr_ref[pl.ds(base, SUBLANES), :] = nr
            si_ref[pl.ds(base, SUBLANES), :] = ni
            return nr, ni

        zero = jnp.zeros((SUBLANES, w), F32)
        fr, fi = lax.fori_loop(0, seg, local_scan, (zero, zero), unroll=2)
        lsr, lsi = pr_tab[seg - 1][0:1, :], pi_tab[seg - 1][0:1, :]
        cr, ci = cr_ref[...], ci_ref[...]
        rows_r, rows_i = [], []
        for j in range(SUBLANES):
            rows_r.append(cr)
            rows_i.append(ci)
            cr, ci = fr[j:j + 1, :] + (lsr * cr - lsi * ci), fi[j:j + 1, :] + (lsr * ci + lsi * cr)
        cr_ref[...] = cr
        ci_ref[...] = ci
        in_r, in_i = _rows_to_tile(rows_r), _rows_to_tile(rows_i)

        def add_entry(i, _):
            base = pl.multiple_of(i * SUBLANES, SUBLANES)
            pr, pi = pr_tab[i], pi_tab[i]
            sr_ref[pl.ds(base, SUBLANES), :] += pr * in_r - pi * in_i
            si_ref[pl.ds(base, SUBLANES), :] += pr * in_i + pi * in_r
            return 0

        lax.fori_loop(0, seg, add_entry, 0, unroll=2)

    blk = pl.BlockSpec((t, w), lambda j, i: (i, j))
    vec = pl.BlockSpec((1, w), lambda j, i: (0, j))
    sds = jax.ShapeDtypeStruct((L, S), F32)
    tab = pltpu.VMEM((seg, SUBLANES, w), F32)
    return pl.pallas_call(
        body, name=name, grid=(S // w, L // t),
        in_specs=[blk, blk, vec, vec], out_specs=[blk, blk], out_shape=[sds, sds],
        scratch_shapes=[tab, tab, pltpu.VMEM((1, w), F32), pltpu.VMEM((1, w), F32)],
        compiler_params=_params("parallel", "arbitrary"),
    )(bu_re, bu_im, lam_re, lam_im)


def _ssm_scan_bwd(d_re, d_im, s_re, s_im, lam_re, lam_im, *, name):
    L, S = d_re.shape
    w, t = _tile(S, SCAN_W), SCAN_T
    nt = L // t
    seg = t // SUBLANES

    def body(dr_ref, di_ref, sr_ref, si_ref, lr_ref, li_ref, gr_ref, gi_ref, ar_ref, ai_ref,
             pr_tab, pi_tab, cgr, cgi, acc_r, acc_i):
        step = pl.program_id(1)
        a = jnp.broadcast_to(lr_ref[...], (SUBLANES, w))
        b = jnp.broadcast_to(-li_ref[...], (SUBLANES, w))

        @pl.when(step == 0)
        def _():
            for r in (cgr, cgi, acc_r, acc_i):
                r[...] = jnp.zeros_like(r)
            _power_table(a, b, pr_tab, pi_tab, seg)

        def local_scan(ii, carry):
            gr, gi = carry
            base = pl.multiple_of((seg - 1 - ii) * SUBLANES, SUBLANES)
            ngr = a * gr - b * gi + dr_ref[pl.ds(base, SUBLANES), :]
            ngi = a * gi + b * gr + di_ref[pl.ds(base, SUBLANES), :]
            gr_ref[pl.ds(base, SUBLANES), :] = ngr
            gi_ref[pl.ds(base, SUBLANES), :] = ngi
            return ngr, ngi

        zero = jnp.zeros((SUBLANES, w), F32)
        fr, fi = lax.fori_loop(0, seg, local_scan, (zero, zero), unroll=2)
        lsr, lsi = pr_tab[seg - 1][0:1, :], pi_tab[seg - 1][0:1, :]
        cr, ci = cgr[...], cgi[...]
        rows_r, rows_i = [None] * SUBLANES, [None] * SUBLANES
        for j in reversed(range(SUBLANES)):
            rows_r[j], rows_i[j] = cr, ci
            cr, ci = fr[j:j + 1, :] + (lsr * cr - lsi * ci), fi[j:j + 1, :] + (lsr * ci + lsi * cr)
        cgr[...] = cr
        cgi[...] = ci
        in_r, in_i = _rows_to_tile(rows_r), _rows_to_tile(rows_i)

        def add_entry(ii, carry):
            nr, ni, xr, xi = carry
            base = pl.multiple_of((seg - 1 - ii) * SUBLANES, SUBLANES)
            sr = sr_ref[pl.ds(base, SUBLANES), :]
            si = si_ref[pl.ds(base, SUBLANES), :]
            xr = xr + (nr * sr + ni * si)
            xi = xi + (ni * sr - nr * si)
            pr, pi = pr_tab[ii], pi_tab[ii]
            gr = gr_ref[pl.ds(base, SUBLANES), :] + (pr * in_r - pi * in_i)
            gi = gi_ref[pl.ds(base, SUBLANES), :] + (pr * in_i + pi * in_r)
            gr_ref[pl.ds(base, SUBLANES), :] = gr
            gi_ref[pl.ds(base, SUBLANES), :] = gi
            return gr, gi, xr, xi

        _, _, xr, xi = lax.fori_loop(0, seg, add_entry, (in_r, in_i, acc_r[...], acc_i[...]), unroll=2)
        acc_r[...] = xr
        acc_i[...] = xi

        @pl.when(step == nt - 1)
        def _():
            ar_ref[...] = jnp.sum(xr, axis=0, keepdims=True)
            ai_ref[...] = jnp.sum(xi, axis=0, keepdims=True)

    blk = pl.BlockSpec((t, w), lambda j, i: (nt - 1 - i, j))
    vec = pl.BlockSpec((1, w), lambda j, i: (0, j))
    sds = jax.ShapeDtypeStruct((L, S), F32)
    vsds = jax.ShapeDtypeStruct((1, S), F32)
    tab = pltpu.VMEM((seg, SUBLANES, w), F32)
    tile = pltpu.VMEM((SUBLANES, w), F32)
    return pl.pallas_call(
        body, name=name, grid=(S // w, nt),
        in_specs=[blk, blk, blk, blk, vec, vec], out_specs=[blk, blk, vec, vec],
        out_shape=[sds, sds, vsds, vsds],
        scratch_shapes=[tab, tab, pltpu.VMEM((1, w), F32), pltpu.VMEM((1, w), F32), tile, tile],
        compiler_params=_params("parallel", "arbitrary"),
    )(d_re, d_im, s_re, s_im, lam_re, lam_im)


def _ssm_discretize(a_re, a_im, log_dt, b_re, b_im, c_re, c_im, d_skip):
    G, P, N = SSM_GROUPS, SSM_STATE, SSM_GROUP
    a = jnp.minimum(a_re, -1e-4)
    dt = jnp.exp(log_dt)[:, None]
    mag = jnp.exp(a * dt)
    lr = mag * jnp.cos(a_im * dt)
    li = mag * jnp.sin(a_im * dt)
    den = a * a + a_im * a_im
    cr = ((lr - 1.0) * a + li * a_im) / den
    ci = (li * a - (lr - 1.0) * a_im) / den
    bbr = cr[..., None] * b_re - ci[..., None] * b_im
    bbi = cr[..., None] * b_im + ci[..., None] * b_re
    gl = G // SSM_BLOCKS
    eye = jnp.eye(gl, dtype=F32)[None, :, None, :, None]

    def in_map(bb):
        t = bb.transpose(0, 2, 1).reshape(SSM_BLOCKS, gl, N, P)
        return (eye * t[:, :, :, None, :]).reshape(SSM_BLOCKS, gl * N, gl * P)

    def out_map(c):
        t = c.transpose(0, 2, 1).reshape(SSM_BLOCKS, gl, P, N)
        return (eye * t[:, :, :, None, :]).reshape(SSM_BLOCKS, gl * P, gl * N)

    return (lr.reshape(1, G * P), li.reshape(1, G * P), in_map(bbr), in_map(bbi),
            out_map(c_re), out_map(-c_im), d_skip.reshape(1, SSM_WIDTH))


ATT_T = 512
CUM_T = 256


def _split3(x):
    hi = x.astype(BF16)
    r1 = x - hi.astype(F32)
    mid = r1.astype(BF16)
    lo = (r1 - mid.astype(F32)).astype(BF16)
    return hi, mid, lo


def _tri_dot(tri, x):
    hi, mid, lo = _split3(x)
    dot = lambda p: jnp.dot(tri, p, preferred_element_type=F32)
    return dot(hi) + dot(mid) + dot(lo)


def _log_sigmoid(x):
    return jnp.minimum(x, 0.0) - jnp.log(1.0 + jnp.exp(-jnp.abs(x)))


def _fox_cum(f, fb, *, name):
    L, W = f.shape
    t = _tile(L, CUM_T, SUBLANES)

    def body(f_ref, b_ref, o_ref, carry):
        @pl.when(pl.program_id(0) == 0)
        def _():
            carry[...] = jnp.zeros_like(carry)

        row = lax.broadcasted_iota(jnp.int32, (t, t), 0)
        col = lax.broadcasted_iota(jnp.int32, (t, t), 1)
        tri = jnp.where(col <= row, 1.0, 0.0).astype(BF16)
        c = _tri_dot(tri, _log_sigmoid(f_ref[...] + b_ref[...])) + carry[...]
        o_ref[...] = c
        carry[...] = c[t - 1:t, :]

    return pl.pallas_call(
        body, name=name, grid=(L // t,),
        in_specs=[_row_spec(t, W), _vec_spec(W)], out_specs=_row_spec(t, W),
        out_shape=jax.ShapeDtypeStruct((L, W), F32),
        scratch_shapes=[pltpu.VMEM((1, W), F32)], compiler_params=_params("arbitrary"),
    )(f, fb)


def _fox_cum_bwd(dcum, f, fb, *, name):
    L, W = f.shape
    t = _tile(L, CUM_T, SUBLANES)
    n = L // t

    def body(d_ref, f_ref, b_ref, o_ref, db_ref, carry):
        i = pl.program_id(0)

        @pl.when(i == 0)
        def _():
            carry[...] = jnp.zeros_like(carry)
            db_ref[...] = jnp.zeros_like(db_ref)

        row = lax.broadcasted_iota(jnp.int32, (t, t), 0)
        col = lax.broadcasted_iota(jnp.int32, (t, t), 1)
        tri = jnp.where(col >= row, 1.0, 0.0).astype(BF16)
        dlog = _tri_dot(tri, d_ref[...]) + carry[...]
        carry[...] = dlog[0:1, :]
        df = dlog * _sigmoid(-(f_ref[...] + b_ref[...]))
        o_ref[...] = df.astype(BF16)
        db_ref[...] += _colsum8(df)
        _finish_colsums(i, n - 1, (db_ref,))

    rev = pl.BlockSpec((t, W), lambda i: (n - 1 - i, 0))
    return pl.pallas_call(
        body, name=name, grid=(n,),
        in_specs=[rev, rev, _vec_spec(W)], out_specs=[rev, _vec_spec(W, SUBLANES)],
        out_shape=[jax.ShapeDtypeStruct((L, W), BF16), jax.ShapeDtypeStruct((SUBLANES, W), F32)],
        scratch_shapes=[pltpu.VMEM((1, W), F32)], compiler_params=_params("arbitrary"),
    )(dcum, f, fb)


def _head_col(blk, h):
    lane = lax.broadcasted_iota(jnp.int32, blk.shape, 1)
    return jnp.sum(jnp.where(lane == h * HEAD_DIM, blk, 0.0), axis=1, keepdims=True)


def _lo_mask(rows):
    return lax.broadcasted_iota(jnp.int32, (rows, LANES), 1) < HEAD_DIM


def _causal(t):
    row = lax.broadcasted_iota(jnp.int32, (t, t), 0)
    col = lax.broadcasted_iota(jnp.int32, (t, t), 1)
    return col <= row


def _call_with_exchange(body, exchange, *, name, grid, in_specs, out_specs, out_shape, operands):
    sem = ("parallel",) + ("arbitrary",) * (len(grid) - 1)
    if exchange is None:
        return pl.pallas_call(body, name=name, grid=grid, in_specs=in_specs, out_specs=out_specs,
                              out_shape=out_shape, compiler_params=_params(*sem))(*operands)
    n_in, n_out = len(in_specs), len(out_specs)
    ei, eo = len(exchange.operands), len(exchange.out_shapes)

    def wrapped(*refs):
        ins, ex_in = refs[:n_in], refs[n_in:n_in + ei]
        outs, ex_out = refs[n_in + ei:n_in + ei + n_out], refs[n_in + ei + n_out:n_in + ei + n_out + eo]
        sems = refs[n_in + ei + n_out + eo:]
        ids = [pl.program_id(d) for d in range(len(grid))]
        first = functools.reduce(jnp.logical_and, [i == 0 for i in ids])
        last = functools.reduce(jnp.logical_and, [i == g - 1 for i, g in zip(ids, grid)])

        @pl.when(first)
        def _():
            exchange.start(ex_in, ex_out, sems)

        body(*ins, *outs)

        @pl.when(last)
        def _():
            exchange.finish(ex_in, ex_out, sems)

    return pl.pallas_call(
        wrapped, name=name, grid=grid, in_specs=list(in_specs) + [HBM_SPEC] * ei,
        out_specs=list(out_specs) + [HBM_SPEC] * eo, out_shape=list(out_shape) + list(exchange.out_shapes),
        scratch_shapes=exchange.sems, compiler_params=_params(*(("arbitrary",) * len(grid))),
    )(*operands, *exchange.operands)


def _fox_fwd(qkv, cum_cols, cum_rows, *, name, exchange=None):
    L = qkv.shape[0]
    t = _tile(L, ATT_T)
    nq = L // t
    npair = ATTN_HEADS // 2

    def body(q_ref, k_ref, v_ref, cc_ref, cr_ref, o_ref, o32_ref, lse_ref):
        iq = pl.program_id(1)
        lo = _lo_mask(t)
        qv = q_ref[...] * 0.125
        zq = jnp.zeros_like(qv)
        qh = (jnp.where(lo, qv, zq), jnp.where(lo, zq, qv))
        ccv = cc_ref[...]
        cq = (_head_col(ccv, 0), _head_col(ccv, 1))

        def step(ik, carry, masked):
            start = pl.multiple_of(ik * t, t)
            kb = k_ref[pl.ds(start, t), :]
            vb = v_ref[pl.ds(start, t), :]
            out = []
            for h in range(2):
                m, l, acc = carry[h]
                s = lax.dot_general(qh[h], kb, NT, preferred_element_type=F32)
                s = s + (cq[h] - cr_ref[h:h + 1, pl.ds(start, t)])
                if masked:
                    s = jnp.where(_causal(t), s, -jnp.inf)
                m_new = jnp.maximum(m, jnp.max(s, axis=1, keepdims=True))
                alpha = jnp.exp(m - m_new)
                p = jnp.exp(s - m_new)
                l = alpha * l + jnp.sum(p, axis=1, keepdims=True)
                acc = alpha * acc + jnp.dot(p.astype(BF16), vb, preferred_element_type=F32)
                out.append((m_new, l, acc))
            return tuple(out)

        init1 = (jnp.full((t, 1), -jnp.inf, F32), jnp.zeros((t, 1), F32), jnp.zeros((t, LANES), F32))
        carry = lax.fori_loop(0, iq, lambda ik, c: step(ik, c, False), (init1, init1))
        (m0, l0, a0), (m1, l1, a1) = step(iq, carry, True)
        out = jnp.where(lo, a0 / l0, a1 / l1)
        o_ref[...] = out.astype(BF16)
        o32_ref[...] = out
        lse_ref[...] = jnp.where(lo, m0 + jnp.log(l0), m1 + jnp.log(l1))

    blk = lambda off: pl.BlockSpec((t, LANES), lambda hp, iq: (iq, off + hp))
    whole = lambda off: pl.BlockSpec((L, LANES), lambda hp, iq: (0, off + hp))
    return _call_with_exchange(
        body, exchange, name=name, grid=(npair, nq),
        in_specs=[blk(0), whole(npair), whole(2 * npair), blk(0),
                  pl.BlockSpec((None, SUBLANES, L), lambda hp, iq: (hp, 0, 0))],
        out_specs=[blk(0), blk(0), blk(0)],
        out_shape=[jax.ShapeDtypeStruct((L, ATTN_WIDTH), BF16), jax.ShapeDtypeStruct((L, ATTN_WIDTH), F32),
                   jax.ShapeDtypeStruct((L, ATTN_WIDTH), F32)],
        operands=(qkv, qkv, qkv, cum_cols, cum_rows))


STAT_LSE, STAT_CUM, STAT_DELTA = 0, 2, 4


def _lane_col(blk, idx):
    lane = lax.broadcasted_iota(jnp.int32, blk.shape, 1)
    return jnp.sum(jnp.where(lane == idx, blk, 0.0), axis=1, keepdims=True)


def _fox_rowstats(do, o, lse, cum_cols, *, name):
    L = do.shape[0]
    t = _tile(L, ATT_T)

    def body(do_ref, o_ref, lse_ref, cc_ref, st_ref):
        lo = _lo_mask(t)
        dd = do_ref[...].astype(F32) * o_ref[...]
        lsev, ccv = lse_ref[...], cc_ref[...]
        cols = (_head_col(lsev, 0), _head_col(lsev, 1), _head_col(ccv, 0), _head_col(ccv, 1),
                jnp.sum(jnp.where(lo, dd, 0.0), axis=1, keepdims=True),
                jnp.sum(jnp.where(lo, 0.0, dd), axis=1, keepdims=True))
        lane = lax.broadcasted_iota(jnp.int32, (t, LANES), 1)
        out = jnp.zeros((t, LANES), F32)
        for i, col in enumerate(cols):
            out = jnp.where(lane == i, col, out)
        st_ref[...] = out

    blk = pl.BlockSpec((t, LANES), lambda hp, i: (i, hp))
    return pl.pallas_call(
        body, name=name, grid=(ATTN_HEADS // 2, L // t),
        in_specs=[blk, blk, blk, blk], out_specs=blk,
        out_shape=jax.ShapeDtypeStruct((L, ATTN_WIDTH), F32),
        compiler_params=_params("parallel", "parallel"),
    )(do, o, lse, cum_cols)


def _fox_bwd(qkv, do, stats, cum_rows, *, name, exchange=None):
    L = qkv.shape[0]
    t = _tile(L, ATT_T)
    nq = L // t
    npair = ATTN_HEADS // 2

    def body(q_ref, do_ref, st_ref, qt_ref, dot_ref, k_ref, v_ref, cr_ref, dk_ref, dv_ref, dc_ref, dq_ref, drow_ref):
        ik = pl.program_id(1)

        @pl.when(ik == 0)
        def _():
            dq_ref[...] = jnp.zeros_like(dq_ref)
            drow_ref[...] = jnp.zeros_like(drow_ref)

        lo = _lo_mask(t)
        lo_rows = lax.broadcasted_iota(jnp.int32, (LANES, t), 0) < HEAD_DIM
        lane = lax.broadcasted_iota(jnp.int32, (t, LANES), 1)
        kb = k_ref[...]
        vb = v_ref[...]
        zk = jnp.zeros_like(kb)
        kh = (jnp.where(lo, kb, zk), jnp.where(lo, zk, kb))
        vh = (jnp.where(lo, vb, zk), jnp.where(lo, zk, vb))
        ck = (cr_ref[0:1, :], cr_ref[1:2, :])

        def step(iq, carry, masked):
            dk, dv, dc0, dc1 = carry
            start = pl.multiple_of(iq * t, t)
            qb = q_ref[pl.ds(start, t), :] * 0.125
            dob = do_ref[pl.ds(start, t), :]
            stb = st_ref[pl.ds(start, t), :]
            qtb = qt_ref[:, pl.ds(start, t)] * 0.125
            dotb = dot_ref[:, pl.ds(start, t)]
            dks, dvs, dcs, dqs, rss = [], [], [], [], []
            for h in range(2):
                s = lax.dot_general(qb, kh[h], NT, preferred_element_type=F32)
                s = s + (_lane_col(stb, STAT_CUM + h) - ck[h])
                if masked:
                    s = jnp.where(_causal(t), s, -jnp.inf)
                p = jnp.exp(s - _lane_col(stb, STAT_LSE + h))
                dp = lax.dot_general(dob, vh[h], NT, preferred_element_type=F32)
                ds = p * (dp - _lane_col(stb, STAT_DELTA + h))
                dsb = ds.astype(BF16)
                dvs.append(jnp.dot(dotb, p.astype(BF16), preferred_element_type=F32))
                dks.append(jnp.dot(qtb, dsb, preferred_element_type=F32))
                dqs.append(jnp.dot(dsb, kh[h], preferred_element_type=F32))
                dcs.append(jnp.sum(ds, axis=0, keepdims=True))
                rss.append(jnp.sum(ds, axis=1, keepdims=True))
            dq_ref[pl.ds(start, t), :] += 0.125 * (dqs[0] + dqs[1])
            drow_ref[pl.ds(start, t), :] += jnp.where(lane == 0, rss[0], jnp.where(lane == 1, rss[1], 0.0))
            return (dk + jnp.where(lo_rows, dks[0], dks[1]), dv + jnp.where(lo_rows, dvs[0], dvs[1]),
                    dc0 - dcs[0], dc1 - dcs[1])

        zero = jnp.zeros((LANES, t), F32)
        zrow = jnp.zeros((1, t), F32)
        carry = step(ik, (zero, zero, zrow, zrow), True)
        dk, dv, dc0, dc1 = lax.fori_loop(ik + 1, nq, lambda iq, c: step(iq, c, False), carry)
        dk_ref[...] = dk.T.astype(BF16)
        dv_ref[...] = dv.T.astype(BF16)
        dc_ref[...] = jnp.zeros_like(dc_ref)
        dc_ref[0:1, :] = dc0
        dc_ref[1:2, :] = dc1

    whole = lambda off: pl.BlockSpec((L, LANES), lambda hp, ik: (0, off + hp))
    blk = lambda off: pl.BlockSpec((t, LANES), lambda hp, ik: (ik, off + hp))
    rows = pl.BlockSpec((None, SUBLANES, t), lambda hp, ik: (hp, 0, ik))
    whole_t = pl.BlockSpec((LANES, L), lambda hp, ik: (hp, 0))
    return _call_with_exchange(
        body, exchange, name=name, grid=(npair, nq),
        in_specs=[whole(0), whole(0), whole(0), whole_t, whole_t, blk(npair), blk(2 * npair), rows],
        out_specs=[blk(0), blk(0), rows, whole(0), whole(0)],
        out_shape=[jax.ShapeDtypeStruct((L, ATTN_WIDTH), BF16), jax.ShapeDtypeStruct((L, ATTN_WIDTH), BF16),
                   jax.ShapeDtypeStruct((npair, SUBLANES, L), F32),
                   jax.ShapeDtypeStruct((L, ATTN_WIDTH), F32), jax.ShapeDtypeStruct((L, ATTN_WIDTH), F32)],
        operands=(qkv, do, stats, qkv[:, :ATTN_WIDTH].T, do.T, qkv, qkv, cum_rows))


def _mod_partial(c_all, mod_w, mod_b_cols, *, name):
    depth, K, cols = mod_w.shape
    tn = _tile(cols, 768)

    def body(c_ref, w_ref, b_ref, o_ref):
        cv = c_ref[...]
        sc = (cv * _sigmoid(cv)).astype(BF16)
        o_ref[...] = jnp.dot(sc, w_ref[...].astype(BF16), preferred_element_type=F32) + b_ref[...]

    return pl.pallas_call(
        body, name=name, grid=(depth, cols // tn),
        in_specs=[pl.BlockSpec((N_DEV, K), lambda l, j: (0, 0)),
                  pl.BlockSpec((None, K, tn), lambda l, j: (l, 0, j)),
                  pl.BlockSpec((None, 1, tn), lambda l, j: (l, 0, j))],
        out_specs=pl.BlockSpec((None, N_DEV, tn), lambda l, j: (l, 0, j)),
        out_shape=jax.ShapeDtypeStruct((depth, N_DEV, cols), F32),
        compiler_params=_params("parallel", "parallel"),
    )(c_all, mod_w, mod_b_cols)


def _mod_wgrad(c_all_t, dmod, *, name):
    depth, nb, cols = dmod.shape
    K = c_all_t.shape[0]
    tn = _tile(cols, 768)
    tk = _tile(K, 256, SUBLANES)

    def body(c_ref, d_ref, o_ref):
        cv = c_ref[...]
        sc = cv * _sigmoid(cv)
        dv = d_ref[...]
        acc = sc[:, 0:1] * dv[0:1, :]
        for b in range(1, nb):
            acc = acc + sc[:, b:b + 1] * dv[b:b + 1, :]
        o_ref[...] = acc

    return pl.pallas_call(
        body, name=name, grid=(depth, K // tk, cols // tn),
        in_specs=[pl.BlockSpec((tk, nb), lambda l, i, j: (i, 0)),
                  pl.BlockSpec((None, nb, tn), lambda l, i, j: (l, 0, j))],
        out_specs=pl.BlockSpec((None, tk, tn), lambda l, i, j: (l, i, j)),
        out_shape=jax.ShapeDtypeStruct((depth, K, cols), F32),
        compiler_params=_params("parallel", "parallel", "parallel"),
    )(c_all_t, dmod)


def _adamw(w, g, m, v, *, name):
    shape = w.shape
    cols = shape[-1]
    rows = int(np.prod(shape[:-1]))
    t = _tile(rows, 256, SUBLANES) if rows % SUBLANES == 0 else rows
    r2 = lambda a: a.reshape(rows, cols)

    def body(w_ref, g_ref, m_ref, v_ref, d_ref, nm_ref, nv_ref):
        gv = g_ref[...]
        nm = ADAM_B1 * m_ref[...] + (1.0 - ADAM_B1) * gv
        nv = ADAM_B2 * v_ref[...] + (1.0 - ADAM_B2) * (gv * gv)
        m_hat = nm / (1.0 - ADAM_B1 ** ADAM_STEP)
        v_hat = nv / (1.0 - ADAM_B2 ** ADAM_STEP)
        d_ref[...] = -ADAM_LR * (m_hat / (jnp.sqrt(v_hat) + ADAM_EPS) + ADAM_WD * w_ref[...])
        nm_ref[...] = nm
        nv_ref[...] = nv

    spec = pl.BlockSpec((t, cols), lambda i: (i, 0))
    sds = jax.ShapeDtypeStruct((rows, cols), F32)
    d, nm, nv = pl.pallas_call(
        body, name=name, grid=(rows // t,),
        in_specs=[spec] * 4, out_specs=[spec] * 3, out_shape=[sds] * 3,
        compiler_params=_params("parallel"),
    )(r2(w), r2(g), r2(m), r2(v))
    return d.reshape(shape), nm.reshape(shape), nv.reshape(shape)


def _my_place():
    return lax.axis_index("x"), lax.axis_index("y"), lax.axis_index("c")


def _other_chips(x, y):
    return [(1 - x, y), (x, 1 - y), (1 - x, 1 - y)]


def _all_gather8(v, *, name, with_sum=False):
    m, n = v.shape

    def body(x_ref, out_ref, *rest):
        if with_sum:
            sum_ref, send_sems, recv_sems, local_sem = rest
        else:
            send_sems, recv_sems, local_sem = rest
        x, y, c = _my_place()
        me, sibling = (x, y, c), (x, y, 1 - c)
        chips = _other_chips(x, y)

        def rows(px, py, pc):
            return out_ref.at[pl.ds((4 * px + 2 * py + pc) * m, m), :]

        def copy(k, block, to, src=None):
            return pltpu.make_async_remote_copy(
                src_ref=rows(*block) if src is None else src, dst_ref=rows(*block),
                send_sem=send_sems.at[k], recv_sem=recv_sems.at[k], device_id=to, device_id_type=MESH)

        mine = pltpu.make_async_copy(x_ref, rows(*me), local_sem)
        mine.start()
        first = [copy(0, me, sibling, src=x_ref)]
        first += [copy(1 + j, me, (*chip, c), src=x_ref) for j, chip in enumerate(chips)]
        for cp in first:
            cp.start()
        passed = [copy(4 + j, (*chip, c), sibling) for j, chip in enumerate(chips)]
        for j, chip in enumerate(chips):
            copy(1 + j, (*chip, c), me).wait_recv()
            passed[j].start()
        copy(0, sibling, me).wait_recv()
        for j, chip in enumerate(chips):
            copy(4 + j, (*chip, 1 - c), me).wait_recv()
        for cp in first + passed:
            cp.wait_send()
        mine.wait()
        if with_sum:
            acc = out_ref[pl.ds(0, m), :]
            for d in range(1, N_DEV):
                acc = acc + out_ref[pl.ds(d * m, m), :]
            sum_ref[...] = acc

    vm = pl.BlockSpec(memory_space=pltpu.VMEM)
    out_shape = [jax.ShapeDtypeStruct((N_DEV * m, n), F32)]
    if with_sum:
        out_shape.append(jax.ShapeDtypeStruct((m, n), F32))
    res = pl.pallas_call(
        body, name=name, out_shape=out_shape, in_specs=[vm], out_specs=[vm] * len(out_shape),
        scratch_shapes=[pltpu.SemaphoreType.DMA((7,)), pltpu.SemaphoreType.DMA((7,)), pltpu.SemaphoreType.DMA],
        compiler_params=pltpu.CompilerParams(vmem_limit_bytes=VMEM_LIMIT),
    )(v)
    return res if with_sum else res[0]


class _Cut(NamedTuple):
    shape: tuple
    slab: int
    half: int


IN_WIDTH = SSM_WIDTH + 3 * ATTN_WIDTH + ATTN_HEADS + 2 * D_MODEL
CUTS = dict(
    ffn_w_in=_Cut((1, D_MODEL, 2 * D_FF), 2, 1),
    ffn_w_out=_Cut((1, D_FF, D_MODEL), 1, 2),
    mix_w_in=_Cut((N_CHIPS, D_MODEL, IN_WIDTH // N_CHIPS), 0, 1),
    glu_w=_Cut((1, SSM_WIDTH, 2 * D_MODEL), 2, 1),
    attn_w_out=_Cut((1, ATTN_WIDTH, D_MODEL), 2, 1),
    mix_w_out=_Cut((1, D_MODEL, D_MODEL), 1, 2),
)
LAYER_MATS = (("ffn_w_in", 0), ("ffn_w_in", 1), ("ffn_w_out", 0), ("ffn_w_out", 1), ("mix_w_in", None),
              ("glu_w", None), ("attn_w_out", None), ("mix_w_out", None))
EARLY_MATS = (0, 2, 4, 5, 6, 7)
LATE_MATS = (1, 3)


def _part_shape(cut, slab=False, half=False):
    s = list(cut.shape)
    if slab:
        s[cut.slab] //= N_CHIPS
    if half:
        s[cut.half] //= 2
    return tuple(s)


def _window(ref, cut, slab=None, half=None):
    idx = [slice(None)] * len(cut.shape)
    for axis, parts, which in ((cut.slab, N_CHIPS, slab), (cut.half, 2, half)):
        if which is not None:
            width = cut.shape[axis] // parts
            idx[axis] = pl.ds(pl.multiple_of(which * width, width), width)
    return ref.at[tuple(idx)]


HBM_SPEC = pl.BlockSpec(memory_space=pltpu.HBM)


class _Exchange(NamedTuple):
    operands: list
    out_shapes: list
    sems: list
    start: object
    finish: object


def _run_exchange(ex, *, name):
    ni, no = len(ex.operands), len(ex.out_shapes)

    def body(*refs):
        parts = (refs[:ni], refs[ni:ni + no], refs[ni + no:])
        ex.start(*parts)
        ex.finish(*parts)

    return pl.pallas_call(
        body, name=name, out_shape=ex.out_shapes, in_specs=[HBM_SPEC] * ni, out_specs=[HBM_SPEC] * no,
        scratch_shapes=ex.sems,
    )(*ex.operands)


def _gather_exchange(shards, cuts):
    n = len(shards)

    def setup(s_refs, f_refs, sems):
        send_sems, recv_sems = sems
        x, y, c = _my_place()
        me, sibling, mine = (x, y, c), (x, y, 1 - c), 2 * x + y
        chips = _other_chips(x, y)

        def copy(i, k, src, dst, to):
            return pltpu.make_async_remote_copy(
                src_ref=src, dst_ref=dst, send_sem=send_sems.at[7 * i + k], recv_sem=recv_sems.at[7 * i + k],
                device_id=to, device_id_type=MESH)

        def landed(i, j, half):
            return _window(f_refs[i], cuts[i], slab=2 * chips[j][0] + chips[j][1], half=half)

        def sends():
            own = [copy(i, 6, s_refs[i], _window(f_refs[i], cuts[i], slab=mine), sibling) for i in range(n)]
            return own + [copy(i, j, _window(s_refs[i], cuts[i], half=c),
                               _window(f_refs[i], cuts[i], slab=mine, half=c), (*chips[j], c))
                          for i in range(n) for j in range(3)]

        return c, me, sibling, copy, landed, sends

    def start(s_refs, f_refs, sems):
        for cp in setup(s_refs, f_refs, sems)[-1]():
            cp.start()

    def finish(s_refs, f_refs, sems):
        c, me, sibling, copy, landed, sends = setup(s_refs, f_refs, sems)
        passed = []
        for i in range(n):
            for j in range(3):
                copy(i, j, landed(i, j, c), landed(i, j, c), me).wait_recv()
                passed.append(copy(i, 3 + j, landed(i, j, c), landed(i, j, c), sibling))
                passed[-1].start()
        for i in range(n):
            for j in range(3):
                copy(i, 3 + j, landed(i, j, 1 - c), landed(i, j, 1 - c), me).wait_recv()
        for i in range(n):
            mine_i = _window(f_refs[i], cuts[i], slab=2 * me[0] + me[1])
            copy(i, 6, mine_i, mine_i, me).wait_recv()
        for cp in sends() + passed:
            cp.wait_send()

    return _Exchange(
        list(shards), [jax.ShapeDtypeStruct(cut.shape, s.dtype) for s, cut in zip(shards, cuts)],
        [pltpu.SemaphoreType.DMA((7 * n,)), pltpu.SemaphoreType.DMA((7 * n,))], start, finish)


def _swap_layer(mats, cuts, *, name):
    n = len(mats)

    def body(*refs):
        m_refs, r_refs = refs[:n], refs[n:2 * n]
        send_sems, recv_sems = refs[2 * n:]
        x, y, c = _my_place()
        cps = [pltpu.make_async_remote_copy(
            src_ref=_window(m_refs[i], cuts[i], half=1 - c), dst_ref=r_refs[i], send_sem=send_sems.at[i],
            recv_sem=recv_sems.at[i], device_id=(x, y, 1 - c), device_id_type=MESH) for i in range(n)]
        for cp in cps:
            cp.start()
        for cp in cps:
            cp.wait()

    return pl.pallas_call(
        body, name=name, out_shape=[jax.ShapeDtypeStruct(_part_shape(cut, half=True), m.dtype)
                                    for m, cut in zip(mats, cuts)],
        in_specs=[HBM_SPEC] * n, out_specs=[HBM_SPEC] * n,
        scratch_shapes=[pltpu.SemaphoreType.DMA((n,)), pltpu.SemaphoreType.DMA((n,))],
    )(*mats)


def _partials_exchange(sums, cuts):
    n = len(sums)

    def copies(s_refs, p_refs, sems):
        send_sems, recv_sems = sems
        x, y, c = _my_place()
        return [pltpu.make_async_remote_copy(
            src_ref=_window(s_refs[i], cuts[i], slab=2 * chip[0] + chip[1]), dst_ref=p_refs[i].at[j],
            send_sem=send_sems.at[3 * i + j], recv_sem=recv_sems.at[3 * i + j],
            device_id=(*chip, c), device_id_type=MESH)
            for i in range(n) for j, chip in enumerate(_other_chips(x, y))]

    def start(s_refs, p_refs, sems):
        for cp in copies(s_refs, p_refs, sems):
            cp.start()

    def finish(s_refs, p_refs, sems):
        for cp in copies(s_refs, p_refs, sems):
            cp.wait()

    return _Exchange(
        list(sums), [jax.ShapeDtypeStruct((3,) + _part_shape(cut, slab=True, half=True), s.dtype)
                     for s, cut in zip(sums, cuts)],
        [pltpu.SemaphoreType.DMA((3 * n,)), pltpu.SemaphoreType.DMA((3 * n,))], start, finish)


def _share_all(dests, cuts, places, *, name):
    names = list(dests)
    nn, n = len(names), len(places)

    def body(*refs):
        o_refs = dict(zip(names, refs[nn:2 * nn]))
        send_sems, recv_sems = refs[2 * nn:]
        x, y, c = _my_place()

        def win(i, half):
            slab_cut = _Cut(_part_shape(cuts[i], slab=True), cuts[i].slab, cuts[i].half)
            return _window(o_refs[places[i][0]].at[places[i][1]], slab_cut, half=half)

        def copy(i, half):
            return pltpu.make_async_remote_copy(
                src_ref=win(i, half), dst_ref=win(i, half), send_sem=send_sems.at[i], recv_sem=recv_sems.at[i],
                device_id=(x, y, 1 - c), device_id_type=MESH)

        for i in range(n):
            copy(i, c).start()
        for i in range(n):
            copy(i, c).wait_send()
            copy(i, 1 - c).wait_recv()

    return pl.pallas_call(
        body, name=name, out_shape=[jax.ShapeDtypeStruct(dests[k].shape, F32) for k in names],
        in_specs=[HBM_SPEC] * nn, out_specs=[HBM_SPEC] * nn,
        input_output_aliases={i: i for i in range(nn)},
        scratch_shapes=[pltpu.SemaphoreType.DMA((n,)), pltpu.SemaphoreType.DMA((n,))],
    )(*[dests[k] for k in names])


def _cut_blocks(shape):
    _, R, C = shape
    tr = _tile(R, 256, 16)
    tc = _tile(C, 2048) if C % LANES == 0 else C
    return (None, tr, tc), (shape[0], R // tr, C // tc)


def _offset_map(axis, blocks):
    def index_map(b, i, j, which):
        idx = [b, i, j]
        idx[axis] = which[0] * blocks[axis] + idx[axis]
        return tuple(idx)
    return index_map


def _add_half(mat, other, cut, c_idx, *, name):
    shape = _part_shape(cut, half=True)
    block, grid = _cut_blocks(shape)

    def body(c_ref, m_ref, o_ref, f_ref, b_ref):
        s = m_ref[...] + o_ref[...]
        f_ref[...] = s
        b_ref[...] = s.astype(BF16)

    plain = pl.BlockSpec(block, lambda b, i, j, which: (b, i, j))
    grid_spec = pltpu.PrefetchScalarGridSpec(
        num_scalar_prefetch=1, grid=grid,
        in_specs=[pl.BlockSpec(block, _offset_map(cut.half, grid)), plain], out_specs=[plain, plain])
    return pl.pallas_call(
        body, name=name, grid_spec=grid_spec,
        out_shape=[jax.ShapeDtypeStruct(shape, F32), jax.ShapeDtypeStruct(shape, BF16)],
        compiler_params=_params("parallel", "parallel", "parallel"),
    )(c_idx, mat, other)


def _sum_slab(own, parts, cut, dest, place, chip_idx, c_idx, *, name):
    shape = _part_shape(cut, slab=True, half=True)
    block, grid = _cut_blocks(shape)
    assert shape[0] == 1

    def body(k_ref, c_ref, o_ref, p_ref, dest_ref, out_ref):
        acc = o_ref[...]
        for j in range(3):
            acc = acc + p_ref[j].astype(F32)
        out_ref[...] = acc

    def own_map(b, i, j, chip, core):
        idx = [b, i, j]
        idx[cut.slab] = chip[0] * grid[cut.slab] + idx[cut.slab]
        return tuple(idx)

    def dest_map(b, i, j, chip, core):
        idx = [b, i, j]
        idx[cut.half] = core[0] * grid[cut.half] + idx[cut.half]
        return tuple(place) + tuple(idx)

    grid_spec = pltpu.PrefetchScalarGridSpec(
        num_scalar_prefetch=2, grid=grid,
        in_specs=[pl.BlockSpec(block, own_map),
                  pl.BlockSpec((3,) + block[1:], lambda b, i, j, chip, core: (0, i, j)),
                  pl.BlockSpec(memory_space=pl.ANY)],
        out_specs=pl.BlockSpec((None,) * len(place) + block, dest_map))
    return pl.pallas_call(
        body, name=name, grid_spec=grid_spec, out_shape=jax.ShapeDtypeStruct(dest.shape, F32),
        input_output_aliases={4: 0},
        compiler_params=_params("parallel", "parallel", "parallel"),
    )(chip_idx, c_idx, own, parts.reshape((3,) + shape[1:]), dest)


def _pad_rows(flat, cols=8 * LANES, align=SUBLANES):
    n = flat.shape[0]
    rows = -(-n // (cols * align)) * align
    return jnp.pad(flat, (0, rows * cols - n)).reshape(rows, cols)


def _row(v):
    return v.reshape(1, -1)


def _ffn_fwd(x, mod, g_pre, g_post, w_in, w_out, tag):
    sh, sc, gate = _row(mod[0]), _row(mod[1]), _row(mod[2])
    h = _prenorm(x, _row(g_pre), sc, sh, name=f"prenorm_{tag}")
    gt, up, act = _mm_swiglu(h, w_in, name=f"swiglu_{tag}")
    y = _mm(act, w_out, name=f"ffn_out_{tag}", tk=D_FF)
    x_out = _postnorm_res(x, y, _row(g_post), gate, FFN_RES, name=f"postnorm_{tag}")
    return x_out, (x, h, gt, up, act, y)


def _ffn_bwd(dxo, saved, mod, g_pre, g_post, w_in, w_out, tag):
    x, h, gt, up, act, y = saved
    sc, gate = _row(mod[1]), _row(mod[2])
    dy, dgate, dgpost = _postnorm_bwd(dxo, y, _row(g_post), gate, FFN_RES, name=f"postnorm_bwd_{tag}")
    dgt, dup = _mm_swiglu_bwd(dy, w_out, gt, up, name=f"swiglu_bwd_{tag}")
    dw_out = _mm(act, dy, ta=True, name=f"dw_out_{tag}", tm=1408, tn=1024, tk=1024)
    dh = _mm(dgt, w_in, tb=True, name=f"dh_gate_{tag}", tk=D_FF)
    dh = _mm(dup, w_in, tb=True, b_k0=D_FF, bias=dh, name=f"dh_up_{tag}", tk=D_FF)
    dwg = _mm(h, dgt, ta=True, name=f"dw_gate_{tag}", tm=1024, tn=1408, tk=1024)
    dwu = _mm(h, dup, ta=True, name=f"dw_up_{tag}", tm=1024, tn=1408, tk=1024)
    dx, dsh, dsc, dgpre = _prenorm_bwd(x, dh, dxo, _row(g_pre), sc, name=f"prenorm_bwd_{tag}")
    dmod = jnp.stack([dsh[0], dsc[0], dgate[0]])
    return dx, dmod, dgpre[0], dgpost[0], jnp.concatenate([dwg, dwu], axis=1), dw_out


def _split_mix_w_in(w):
    u0, q0, f0, g0 = 0, SSM_WIDTH, SSM_WIDTH + 3 * ATTN_WIDTH, SSM_WIDTH + 3 * ATTN_WIDTH + ATTN_HEADS
    w_f = jnp.pad(w[:, f0:g0], ((0, 0), (0, LANES - ATTN_HEADS)))
    return w[:, u0:q0], w[:, q0:f0], w_f, w[:, g0:]


def _mixer_fwd(x, mod, g_pre, g_post, w, ssm, forget_b, tag, exchange=None):
    L = x.shape[0]
    sh, sc, gate = _row(mod[0]), _row(mod[1]), _row(mod[2])
    lam_re, lam_im, bin_re, bin_im, cout_re, cout_im, dskip = ssm
    h = _prenorm(x, _row(g_pre), sc, sh, name=f"prenorm_{tag}")
    u = _mm(h, w["u"], name=f"proj_u_{tag}")
    qkv = _mm(h, w["qkv"], out_dtype=BF16, name=f"proj_qkv_{tag}")
    f = _mm(h, w["f"], name=f"proj_f_{tag}")
    gab = _mm(h, w["gab"], name=f"proj_gab_{tag}")
    u_i = _interleave(u)
    bu_re, bu_im = _mm_bd([u_i], [bin_re, bin_im], name=f"ssm_bu_{tag}")
    s_re, s_im = _ssm_scan_fwd(bu_re, bu_im, lam_re, lam_im, name=f"ssm_scan_{tag}")
    y_ssm = _deinterleave(_mm_bd([s_re, s_im], [cout_re, cout_im], bias=u_i, bscale=dskip, name=f"ssm_y_{tag}"))
    gl = _gelu_fwd(y_ssm, name=f"gelu_{tag}")
    z = _mm(gl, w["glu"], name=f"glu_{tag}")
    fb = jnp.pad(forget_b, (0, LANES - ATTN_HEADS)).reshape(1, LANES)
    cum = _fox_cum(f, fb, name=f"fox_cum_{tag}")
    cum8 = cum[:, :ATTN_HEADS]
    cum_cols = jnp.repeat(cum8, HEAD_DIM, axis=1)
    cum_rows = jnp.pad(cum8.T.reshape(ATTN_HEADS // 2, 2, L), ((0, 0), (0, SUBLANES - 2), (0, 0)))
    attn, attn32, lse, *exchanged = _fox_fwd(qkv, cum_cols, cum_rows, name=f"fox_fwd_{tag}", exchange=exchange)
    yb = _mm(attn, w["attn_out"], name=f"attn_out_{tag}")
    merged = _merge_fwd(z, yb, gab, name=f"merge_{tag}")
    y = _mm(merged, w["out"], name=f"mix_out_{tag}")
    x_out = _postnorm_res(x, y, _row(g_post), gate, 1.0, name=f"postnorm_{tag}")
    saved = (x, h, u, u_i, qkv, f, gab, s_re, s_im, y_ssm, gl, z, fb, cum_cols, cum_rows, attn, attn32, lse, yb,
             merged, y)
    return x_out, saved, exchanged


def _mixer_bwd(dxo, saved, mod, g_pre, g_post, w, ssm, tag, exchange=None):
    (x, h, u, u_i, qkv, f, gab, s_re, s_im, y_ssm, gl, z, fb, cum_cols, cum_rows, attn, attn32, lse, yb,
     merged, y) = saved
    L = x.shape[0]
    sc, gate = _row(mod[1]), _row(mod[2])
    lam_re, lam_im, bin_re, bin_im, cout_re, cout_im, dskip = ssm
    dy, dgate, dgpost = _postnorm_bwd(dxo, y, _row(g_post), gate, 1.0, name=f"postnorm_bwd_{tag}")
    dmerged = _mm(dy, w["out"], tb=True, name=f"dmerged_{tag}")
    dw_out = _mm(merged, dy, ta=True, name=f"dw_mix_out_{tag}", tm=1024, tn=1024)
    dz, dyb, dgab = _merge_bwd(dmerged, z, yb, gab, name=f"merge_bwd_{tag}")
    dgl = _mm(dz, w["glu"], tb=True, name=f"dgl_{tag}", tk=2048)
    dw_glu = _mm(gl, dz, ta=True, name=f"dw_glu_{tag}", tm=512, tn=2048)
    dys, dsk, dd = _gelu_bwd(dgl, y_ssm, u, dskip, name=f"gelu_bwd_{tag}")
    dys, dsk = _interleave(dys), _interleave(dsk)
    d_re, d_im = _mm_bd([dys], [cout_re, cout_im], tb=True, name=f"ssm_ds_{tag}")
    dcout_re = _mm_bd_t(s_re, dys, SSM_BLOCKS, name=f"ssm_dc_re_{tag}")
    dcout_im = _mm_bd_t(s_im, dys, SSM_BLOCKS, name=f"ssm_dc_im_{tag}")
    g_re, g_im, dlam_re, dlam_im = _ssm_scan_bwd(d_re, d_im, s_re, s_im, lam_re, lam_im, name=f"ssm_scan_bwd_{tag}")
    du = _mm_bd([g_re, g_im], [bin_re, bin_im], tb=True, bias=dsk, out_dtype=BF16, name=f"ssm_du_{tag}")
    du = _deinterleave(du)
    dbin_re = _mm_bd_t(u_i, g_re, SSM_BLOCKS, name=f"ssm_db_re_{tag}")
    dbin_im = _mm_bd_t(u_i, g_im, SSM_BLOCKS, name=f"ssm_db_im_{tag}")
    dssm = (dlam_re, dlam_im, dbin_re, dbin_im, dcout_re, dcout_im, _row(dd[0]))
    dattn = _mm(dyb, w["attn_out"], tb=True, out_dtype=BF16, name=f"dattn_{tag}")
    dw_attn = _mm(attn, dyb, ta=True, name=f"dw_attn_out_{tag}", tm=512, tn=1024)
    stats = _fox_rowstats(dattn, attn32, lse, cum_cols, name=f"fox_rowstats_{tag}")
    dk, dv, dcum_rows, dq, drow, *exchanged = _fox_bwd(qkv, dattn, stats, cum_rows, name=f"fox_bwd_{tag}",
                                                       exchange=exchange)
    drow8 = drow.reshape(L, ATTN_HEADS // 2, LANES)[:, :, :2].reshape(L, ATTN_HEADS)
    dcum = drow8 + dcum_rows[:, :2, :].reshape(ATTN_HEADS, L).T
    dcum = jnp.pad(dcum, ((0, 0), (0, LANES - ATTN_HEADS)))
    df, dfb = _fox_cum_bwd(dcum, f, fb, name=f"fox_cum_bwd_{tag}")
    dqkv = jnp.concatenate([dq.astype(BF16), dk, dv], axis=1)
    dh = _mm(dqkv, w["qkv"], tb=True, name=f"dh_qkv_{tag}", tk=1536)
    dh = _mm(du, w["u"], tb=True, bias=dh, name=f"dh_u_{tag}")
    dh = _mm(dgab, w["gab"], tb=True, bias=dh, name=f"dh_gab_{tag}", tk=2048)
    dh = _mm(df, w["f"], tb=True, bias=dh, name=f"dh_f_{tag}")
    dw_u = _mm(h, du, ta=True, name=f"dw_u_{tag}", tm=1024, tn=512)
    dw_qkv = _mm(h, dqkv, ta=True, name=f"dw_qkv_{tag}", tm=1024, tn=1536)
    dw_f = _mm(h, df, ta=True, name=f"dw_f_{tag}", tm=1024)
    dw_gab = _mm(h, dgab, ta=True, name=f"dw_gab_{tag}", tm=1024, tn=1024)
    dw_in = jnp.concatenate([dw_u, dw_qkv, dw_f[:, :ATTN_HEADS], dw_gab], axis=1)
    dx, dsh, dsc, dgpre = _prenorm_bwd(x, dh, dxo, _row(g_pre), sc, name=f"prenorm_bwd_{tag}")
    dmod = jnp.stack([dsh[0], dsc[0], dgate[0]])
    grads = dict(mix_w_in=dw_in, glu_w=dw_glu, attn_w_out=dw_attn, mix_w_out=dw_out,
                 forget_b=dfb[0, :ATTN_HEADS])
    return dx, dmod, dgpre[0], dgpost[0], grads, dssm, exchanged


SSM_NAMES = ("ssm_a_re", "ssm_a_im", "ssm_log_dt", "ssm_b_re", "ssm_b_im", "ssm_c_re", "ssm_c_im", "ssm_d")
SMALL_NAMES = ("forget_b",) + SSM_NAMES
WEIGHT_NAMES = ("mod_w", "mod_b", "norm_pre", "norm_post", "ffn_w_in", "ffn_w_out", "mix_w_in", "forget_b") \
    + SSM_NAMES + ("glu_w", "attn_w_out", "mix_w_out")


def _layer_shards(w, l):
    return [(w[n][l] if j is None else w[n][l, j]).astype(BF16).reshape(_part_shape(CUTS[n], slab=True))
            for n, j in LAYER_MATS]


def _train_step(x, c, target, w, m, v):
    xi, yi, ci = _my_place()
    chip = 2 * xi + yi
    dev = 4 * xi + 2 * yi + ci
    mod_cols = N_SUB * 3 * D_MODEL // N_CHIPS
    norm_cols = D_MODEL // N_CHIPS

    cuts = [CUTS[n] for n, _ in LAYER_MATS]
    shards = [_layer_shards(w, l) for l in range(DEPTH)]
    full = [[None] * len(LAYER_MATS) for _ in range(DEPTH)]

    def gather(keys):
        return _gather_exchange([shards[l][i] for l, i in keys], [cuts[i] for _, i in keys])

    def store(keys, mats):
        for (l, i), a in zip(keys, mats):
            full[l][i] = a

    early = lambda l: [(l, i) for i in EARLY_MATS]
    late = lambda l: [(l, i) for i in LATE_MATS]
    store(early(0), _run_exchange(gather(early(0)), name="gather_weights_first"))
    c_all = _all_gather8(jnp.pad(c, ((0, SUBLANES - 1), (0, 0))), name="gather_c")[::SUBLANES]
    mod_b_cols = lax.dynamic_slice_in_dim(w["mod_b"], chip * mod_cols, mod_cols, axis=1)[:, None, :]
    mod_part = _mod_partial(c_all, w["mod_w"], mod_b_cols, name="mod_partial")
    small_fwd = jnp.concatenate([mod_part.reshape(-1), w["norm_pre"].reshape(-1), w["norm_post"].reshape(-1)])
    n_mod, n_norm = mod_part.size, w["norm_pre"].size
    sf_all = _all_gather8(_pad_rows(small_fwd), name="gather_mod").reshape(N_DEV, -1)
    sf_chips = sf_all[::2]
    mod_all = jnp.concatenate(
        [sf_chips[k, :n_mod].reshape(DEPTH, N_DEV, mod_cols) for k in range(N_CHIPS)], axis=2)
    mod_mine = lax.dynamic_index_in_dim(mod_all, dev, axis=1, keepdims=False).reshape(DEPTH, N_SUB, 3, D_MODEL)
    norm_pre = jnp.concatenate(
        [sf_chips[k, n_mod:n_mod + n_norm].reshape(DEPTH, N_SUB, norm_cols) for k in range(N_CHIPS)], axis=2)
    norm_post = jnp.concatenate(
        [sf_chips[k, n_mod + n_norm:n_mod + 2 * n_norm].reshape(DEPTH, N_SUB, norm_cols) for k in range(N_CHIPS)],
        axis=2)

    saved, layer_w, ssm_prep, ssm_vjp = [], [], [], []
    h = x
    for l in range(DEPTH):
        win0, _, wout0, _, mix_in, glu, attn_out, mix_out = full[l]
        w_u, w_qkv, w_f, w_gab = _split_mix_w_in(mix_in.transpose(1, 0, 2).reshape(D_MODEL, IN_WIDTH))
        lw = dict(
            ffn=[(win0[0], wout0[0]), None],
            mix=dict(u=w_u, qkv=w_qkv, f=w_f, gab=w_gab, glu=glu[0], attn_out=attn_out[0], out=mix_out[0]))
        prep, vjp = jax.vjp(_ssm_discretize, *[w[n][l] for n in SSM_NAMES])
        layer_w.append(lw)
        ssm_prep.append(prep)
        ssm_vjp.append(vjp)
        h, s0 = _ffn_fwd(h, mod_mine[l, 0], norm_pre[l, 0], norm_post[l, 0], *lw["ffn"][0], tag=f"l{l}a")
        coming = late(l) + (early(l + 1) if l + 1 < DEPTH else [])
        h, s1, arrived = _mixer_fwd(h, mod_mine[l, 1], norm_pre[l, 1], norm_post[l, 1], lw["mix"], prep,
                                    w["forget_b"][l], tag=f"l{l}m", exchange=gather(coming))
        store(coming, arrived)
        lw["ffn"][1] = (full[l][1][0], full[l][3][0])
        h, s2 = _ffn_fwd(h, mod_mine[l, 2], norm_pre[l, 2], norm_post[l, 2], *lw["ffn"][1], tag=f"l{l}b")
        saved.append((s0, s1, s2))
    dh, loss8 = _loss_head(h, target, name="loss_head")

    c_idx = ci.reshape(1).astype(jnp.int32)
    chip_idx = chip.reshape(1).astype(jnp.int32)
    places = []
    dests = {n: lax.empty(((DEPTH,) if j is None else (DEPTH, 2)) + _part_shape(CUTS[n], slab=True), F32)
             for n, j in LAYER_MATS}
    g_small = {n: [None] * DEPTH for n in SMALL_NAMES}
    dmod, dnpre, dnpost = [], [], []
    pending = []

    def begin_reduce(l, idx, mats, tag):
        cs = [cuts[i] for i in idx]
        mats = [a.reshape(cut.shape) for a, cut in zip(mats, cs)]
        from_sibling = _swap_layer(mats, cs, name=f"grad_swap_{tag}")
        for k, i in enumerate(idx):
            f, b = _add_half(mats[k], from_sibling[k], cs[k], c_idx, name=f"grad_add_l{l}_{i}")
            pending.append((l, i, f, b))

    def partials():
        return _partials_exchange([p[3] for p in pending], [cuts[p[1]] for p in pending])

    def end_reduce(parts):
        for (l, i, f, _), part in zip(pending, parts):
            n, j = LAYER_MATS[i]
            place = (l,) if j is None else (l, j)
            dests[n] = _sum_slab(f, part, cuts[i], dests[n], place, chip_idx, c_idx, name=f"grad_sum_l{l}_{i}")
            places.append((n, place))
        pending.clear()

    for l in reversed(range(DEPTH)):
        lw = layer_w[l]
        dh, dm2, dp2, dq2, dwin2, dwout2 = _ffn_bwd(dh, saved[l][2], mod_mine[l, 2], norm_pre[l, 2],
                                                    norm_post[l, 2], *lw["ffn"][1], tag=f"l{l}b")
        begin_reduce(l, LATE_MATS, [dwin2, dwout2], f"l{l}b")
        dh, dm1, dp1, dq1, gmix, dssm, parts = _mixer_bwd(dh, saved[l][1], mod_mine[l, 1], norm_pre[l, 1],
                                                          norm_post[l, 1], lw["mix"], ssm_prep[l], tag=f"l{l}m",
                                                          exchange=partials())
        end_reduce(parts)
        dh, dm0, dp0, dq0, dwin0, dwout0 = _ffn_bwd(dh, saved[l][0], mod_mine[l, 0], norm_pre[l, 0],
                                                    norm_post[l, 0], *lw["ffn"][0], tag=f"l{l}a")
        dmod.insert(0, jnp.stack([dm0, dm1, dm2]))
        dnpre.insert(0, jnp.stack([dp0, dp1, dp2]))
        dnpost.insert(0, jnp.stack([dq0, dq1, dq2]))
        g_small["forget_b"][l] = gmix["forget_b"]
        for n, g in zip(SSM_NAMES, ssm_vjp[l](dssm)):
            g_small[n][l] = g
        dmix_in = gmix["mix_w_in"].reshape(D_MODEL, N_CHIPS, IN_WIDTH // N_CHIPS).transpose(1, 0, 2)
        begin_reduce(l, EARLY_MATS, [dwin0, dwout0, dmix_in, gmix["glu_w"], gmix["attn_w_out"], gmix["mix_w_out"]],
                     f"l{l}a")
    end_reduce(_run_exchange(partials(), name="grad_partials_last"))
    grad_x = dh
    g_small = {n: jnp.stack(g) for n, g in g_small.items()}

    small = [loss8[0, :1], jnp.stack(dmod).reshape(-1), jnp.stack(dnpre).reshape(-1), jnp.stack(dnpost).reshape(-1)]
    small += [g_small[n].reshape(-1) for n in SMALL_NAMES]
    sizes = [int(s.size) for s in small]
    offs = np.concatenate([[0], np.cumsum(sizes)])
    sb_all, sb_sum = _all_gather8(_pad_rows(jnp.concatenate(small)), name="gather_small_grads", with_sum=True)
    sb_sum = sb_sum.reshape(-1)
    take = lambda i: sb_sum[int(offs[i]):int(offs[i + 1])]
    loss = take(0)[0]
    grads = {"mod_b": take(1).reshape(DEPTH, N_SUB * 3 * D_MODEL)}
    dnorm_pre_full = take(2).reshape(DEPTH, N_SUB, D_MODEL)
    dnorm_post_full = take(3).reshape(DEPTH, N_SUB, D_MODEL)
    grads["norm_pre"] = lax.dynamic_slice_in_dim(dnorm_pre_full, chip * norm_cols, norm_cols, axis=2)
    grads["norm_post"] = lax.dynamic_slice_in_dim(dnorm_post_full, chip * norm_cols, norm_cols, axis=2)
    for i, n in enumerate(SMALL_NAMES):
        grads[n] = take(4 + i).reshape(w[n].shape)
    dmod_all = sb_all.reshape(N_DEV, -1)[:, int(offs[1]):int(offs[2])].reshape(N_DEV, DEPTH, N_SUB * 3 * D_MODEL)
    dmod_cols = lax.dynamic_slice_in_dim(dmod_all, chip * mod_cols, mod_cols, axis=2).transpose(1, 0, 2)
    grads["mod_w"] = _mod_wgrad(c_all.T, dmod_cols, name="mod_wgrad")

    shared = _share_all(dests, [CUTS[n] for n, _ in places], places, name="grad_share")
    for n, g in zip(dests, shared):
        grads[n] = g.reshape(w[n].shape)

    delta, new_m, new_v = {}, {}, {}
    for n in WEIGHT_NAMES:
        delta[n], new_m[n], new_v[n] = _adamw(w[n], grads[n], m[n], v[n], name=f"adamw_{n}")
    outs = [loss, grad_x[None]]
    for group in (grads, delta, new_m, new_v):
        outs += [group[n] for n in WEIGHT_NAMES]
    return tuple(outs)


def kernel(x, c, mod_w, mod_b, norm_pre, norm_post, ffn_w_in, ffn_w_out, mix_w_in, forget_b, ssm_a_re, ssm_a_im, ssm_log_dt, ssm_b_re, ssm_b_im, ssm_c_re, ssm_c_im, ssm_d, glu_w, attn_w_out, mix_w_out, loss_target, m_mod_w, m_mod_b, m_norm_pre, m_norm_post, m_ffn_w_in, m_ffn_w_out, m_mix_w_in, m_forget_b, m_ssm_a_re, m_ssm_a_im, m_ssm_log_dt, m_ssm_b_re, m_ssm_b_im, m_ssm_c_re, m_ssm_c_im, m_ssm_d, m_glu_w, m_attn_w_out, m_mix_w_out, v_mod_w, v_mod_b, v_norm_pre, v_norm_post, v_ffn_w_in, v_ffn_w_out, v_mix_w_in, v_forget_b, v_ssm_a_re, v_ssm_a_im, v_ssm_log_dt, v_ssm_b_re, v_ssm_b_im, v_ssm_c_re, v_ssm_c_im, v_ssm_d, v_glu_w, v_attn_w_out, v_mix_w_out):
    w = dict(mod_w=mod_w, mod_b=mod_b, norm_pre=norm_pre, norm_post=norm_post, ffn_w_in=ffn_w_in,
             ffn_w_out=ffn_w_out, mix_w_in=mix_w_in, forget_b=forget_b, ssm_a_re=ssm_a_re, ssm_a_im=ssm_a_im,
             ssm_log_dt=ssm_log_dt, ssm_b_re=ssm_b_re, ssm_b_im=ssm_b_im, ssm_c_re=ssm_c_re, ssm_c_im=ssm_c_im,
             ssm_d=ssm_d, glu_w=glu_w, attn_w_out=attn_w_out, mix_w_out=mix_w_out)
    m = dict(mod_w=m_mod_w, mod_b=m_mod_b, norm_pre=m_norm_pre, norm_post=m_norm_post, ffn_w_in=m_ffn_w_in,
             ffn_w_out=m_ffn_w_out, mix_w_in=m_mix_w_in, forget_b=m_forget_b, ssm_a_re=m_ssm_a_re,
             ssm_a_im=m_ssm_a_im, ssm_log_dt=m_ssm_log_dt, ssm_b_re=m_ssm_b_re, ssm_b_im=m_ssm_b_im,
             ssm_c_re=m_ssm_c_re, ssm_c_im=m_ssm_c_im, ssm_d=m_ssm_d, glu_w=m_glu_w, attn_w_out=m_attn_w_out,
             mix_w_out=m_mix_w_out)
    v = dict(mod_w=v_mod_w, mod_b=v_mod_b, norm_pre=v_norm_pre, norm_post=v_norm_post, ffn_w_in=v_ffn_w_in,
             ffn_w_out=v_ffn_w_out, mix_w_in=v_mix_w_in, forget_b=v_forget_b, ssm_a_re=v_ssm_a_re,
             ssm_a_im=v_ssm_a_im, ssm_log_dt=v_ssm_log_dt, ssm_b_re=v_ssm_b_re, ssm_b_im=v_ssm_b_im,
             ssm_c_re=v_ssm_c_re, ssm_c_im=v_ssm_c_im, ssm_d=v_ssm_d, glu_w=v_glu_w, attn_w_out=v_attn_w_out,
             mix_w_out=v_mix_w_out)
    return _train_step(x[0], c, loss_target[0], w, m, v)
```

```python
import functools
import math
from typing import NamedTuple

import jax
import jax.numpy as jnp
import numpy as np
from jax import lax
from jax.experimental import pallas as pl
from jax.experimental.pallas import tpu as pltpu

F32 = jnp.float32
BF16 = jnp.bfloat16

D_MODEL = 1024
DEPTH = 2
SSM_WIDTH = 512
SSM_GROUP = 16
SSM_GROUPS = 32
SSM_STATE = 64
SSM_FLAT = SSM_GROUPS * SSM_STATE
SSM_BLOCKS = 4
ATTN_HEADS = 8
HEAD_DIM = 64
ATTN_WIDTH = 512
D_FF = 2816
FFN_RES = 0.5
N_SUB = 3
RMS_EPS = 1e-6
N_CHIPS = 4
N_DEV = 8

ADAM_LR = 0.001
ADAM_B1 = 0.9
ADAM_B2 = 0.999
ADAM_EPS = 1e-08
ADAM_WD = 0.01
ADAM_STEP = 10

LANES = 128
SUBLANES = 8
VMEM_LIMIT = 52 * 1024 * 1024
MESH = pl.DeviceIdType.MESH

NN = (((1,), (0,)), ((), ()))
NT = (((1,), (1,)), ((), ()))
TN = (((0,), (0,)), ((), ()))


def _tile(dim, target, align=LANES):
    best = None
    t = align
    while t <= min(dim, target):
        if dim % t == 0:
            best = t
        t += align
    return dim if best is None else best


def _params(*sem):
    return pltpu.CompilerParams(dimension_semantics=sem, vmem_limit_bytes=VMEM_LIMIT)


def _mm(a, b, *, name, ta=False, tb=False, out_dtype=F32, bias=None, bscale=None, b_k0=0,
        out_n=None, out_j0=0, into=None, tm=512, tn=1024, tk=1024):
    M, K = (a.shape[1], a.shape[0]) if ta else a.shape
    N = b.shape[0] if tb else b.shape[1]
    assert b_k0 + K <= (b.shape[1] if tb else b.shape[0]), (a.shape, b.shape, ta, tb)
    tm, tn, tk = _tile(M, tm), _tile(N, tn), _tile(K, tk)
    nk = K // tk
    assert b_k0 % tk == 0
    kb0 = b_k0 // tk
    dn = (((0 if ta else 1,), (1 if tb else 0,)), ((), ()))
    has_bias, has_scale = bias is not None, bscale is not None

    def body(*refs):
        a_ref, b_ref = refs[0], refs[1]
        pos = 2
        bias_ref = scale_ref = None
        if has_bias:
            bias_ref = refs[pos]
            pos += 1
        if has_scale:
            scale_ref = refs[pos]
            pos += 1
        if into is not None:
            pos += 1
        o_ref = refs[pos]
        acc_ref = refs[pos + 1] if nk > 1 else None

        def finish(r):
            if has_bias:
                extra = bias_ref[...].astype(F32)
                if has_scale:
                    extra = extra * scale_ref[...]
                r = r + extra
            o_ref[...] = r.astype(out_dtype)

        part = lax.dot_general(a_ref[...].astype(BF16), b_ref[...].astype(BF16), dn,
                               preferred_element_type=F32)
        if nk == 1:
            finish(part)
        else:
            k = pl.program_id(2)

            @pl.when(k == 0)
            def _():
                acc_ref[...] = part

            @pl.when(k > 0)
            def _():
                acc_ref[...] += part

            @pl.when(k == nk - 1)
            def _():
                finish(acc_ref[...])

    a_spec = pl.BlockSpec((tk, tm), lambda j, i, k: (k, i)) if ta else pl.BlockSpec((tm, tk), lambda j, i, k: (i, k))
    b_spec = (pl.BlockSpec((tn, tk), lambda j, i, k: (j, kb0 + k)) if tb
              else pl.BlockSpec((tk, tn), lambda j, i, k: (kb0 + k, j)))
    in_specs = [a_spec, b_spec]
    args = [a, b]
    if has_bias:
        in_specs.append(pl.BlockSpec((tm, tn), lambda j, i, k: (i, j)))
        args.append(bias)
    if has_scale:
        in_specs.append(pl.BlockSpec((1, tn), lambda j, i, k: (0, j)))
        args.append(bscale)
    aliases = {}
    if into is not None:
        in_specs.append(pl.BlockSpec(memory_space=pl.ANY))
        args.append(into)
        aliases = {len(args) - 1: 0}
    out_n = N if out_n is None else out_n
    assert out_j0 % tn == 0
    jb0 = out_j0 // tn
    return pl.pallas_call(
        body, name=name,
        grid=(N // tn, M // tm, nk),
        in_specs=in_specs,
        out_specs=pl.BlockSpec((tm, tn), lambda j, i, k: (i, jb0 + j)),
        out_shape=jax.ShapeDtypeStruct((M, out_n), out_dtype),
        scratch_shapes=[pltpu.VMEM((tm, tn), F32)] if nk > 1 else [],
        input_output_aliases=aliases,
        compiler_params=_params("parallel", "parallel", "arbitrary"),
    )(*args)


def _mm_bd(a_list, b_list, *, name, tb=False, out_dtype=F32, bias=None, bscale=None, tm=1024):
    G = b_list[0].shape[0]
    Kb, Nb = (b_list[0].shape[2], b_list[0].shape[1]) if tb else b_list[0].shape[1:]
    M = a_list[0].shape[0]
    tm = _tile(M, tm)
    na, nb = len(a_list), len(b_list)
    n_out = nb if na == 1 else 1
    dn = NT if tb else NN
    has_bias, has_scale = bias is not None, bscale is not None

    def body(*refs):
        a_refs, b_refs = refs[:na], refs[na:na + nb]
        pos = na + nb
        bias_ref = scale_ref = None
        if has_bias:
            bias_ref = refs[pos]
            pos += 1
        if has_scale:
            scale_ref = refs[pos]
            pos += 1
        o_refs = refs[pos:]
        prods = [lax.dot_general(a_refs[min(i, na - 1)][...].astype(BF16), b_refs[i][...].astype(BF16), dn,
                                 preferred_element_type=F32) for i in range(nb)]
        outs = prods if n_out == nb else [functools.reduce(jnp.add, prods)]
        for o_ref, r in zip(o_refs, outs):
            if has_bias:
                extra = bias_ref[...].astype(F32)
                r = r + (extra * scale_ref[...] if has_scale else extra)
            o_ref[...] = r.astype(out_dtype)

    a_spec = pl.BlockSpec((tm, Kb), lambda g, i: (i, g))
    b_spec = pl.BlockSpec((None,) + b_list[0].shape[1:], lambda g, i: (g, 0, 0))
    o_spec = pl.BlockSpec((tm, Nb), lambda g, i: (i, g))
    in_specs = [a_spec] * na + [b_spec] * nb
    args = list(a_list) + list(b_list)
    if has_bias:
        in_specs.append(o_spec)
        args.append(bias)
    if has_scale:
        in_specs.append(pl.BlockSpec((1, Nb), lambda g, i: (0, g)))
        args.append(bscale)
    sds = jax.ShapeDtypeStruct((M, G * Nb), out_dtype)
    res = pl.pallas_call(
        body, name=name, grid=(G, M // tm), in_specs=in_specs, out_specs=[o_spec] * n_out,
        out_shape=[sds] * n_out, compiler_params=_params("parallel", "parallel"),
    )(*args)
    return res[0] if n_out == 1 else res


def _mm_bd_t(a, b, G, *, name, tk=1024):
    K, Mb, Nb = a.shape[0], a.shape[1] // G, b.shape[1] // G
    tk = _tile(K, tk)
    nk = K // tk

    def body(a_ref, b_ref, o_ref, acc_ref):
        k = pl.program_id(1)
        part = lax.dot_general(a_ref[...].astype(BF16), b_ref[...].astype(BF16), TN, preferred_element_type=F32)

        @pl.when(k == 0)
        def _():
            acc_ref[...] = part

        @pl.when(k > 0)
        def _():
            acc_ref[...] += part

        @pl.when(k == nk - 1)
        def _():
            o_ref[...] = acc_ref[...]

    return pl.pallas_call(
        body, name=name, grid=(G, nk),
        in_specs=[pl.BlockSpec((tk, Mb), lambda g, k: (k, g)), pl.BlockSpec((tk, Nb), lambda g, k: (k, g))],
        out_specs=pl.BlockSpec((None, Mb, Nb), lambda g, k: (g, 0, 0)),
        out_shape=jax.ShapeDtypeStruct((G, Mb, Nb), F32),
        scratch_shapes=[pltpu.VMEM((Mb, Nb), F32)], compiler_params=_params("parallel", "arbitrary"),
    )(a, b)


def _sigmoid(x):
    return 0.5 * jnp.tanh(0.5 * x) + 0.5


def _mm_swiglu(h, w_in, *, name, tm=512, tn=1408, exchange=None):
    M, K = h.shape
    N = w_in.shape[1] // 2
    tm, tn = _tile(M, tm), _tile(N, tn)
    nj = N // tn

    def body(h_ref, wg_ref, wu_ref, g_ref, u_ref, a_ref):
        hv = h_ref[...]
        g = jnp.dot(hv, wg_ref[...], preferred_element_type=F32)
        u = jnp.dot(hv, wu_ref[...], preferred_element_type=F32)
        g_ref[...] = g.astype(BF16)
        u_ref[...] = u.astype(BF16)
        a_ref[...] = (g * _sigmoid(g) * u).astype(BF16)

    o_spec = pl.BlockSpec((tm, tn), lambda j, i: (i, j))
    sds = jax.ShapeDtypeStruct((M, N), BF16)
    return _call_with_exchange(
        body, exchange, name=name, grid=(nj, M // tm),
        in_specs=[pl.BlockSpec((tm, K), lambda j, i: (i, 0)), pl.BlockSpec((K, tn), lambda j, i: (0, j)),
                  pl.BlockSpec((K, tn), lambda j, i: (0, nj + j))],
        out_specs=[o_spec, o_spec, o_spec], out_shape=[sds, sds, sds], operands=(h, w_in, w_in))


def _mm_swiglu_bwd(dy, w_out, gate, up, *, name, tm=512, tn=1408, exchange=None):
    M, K = dy.shape
    N = w_out.shape[0]
    tm, tn = _tile(M, tm), _tile(N, tn)

    def body(dy_ref, w_ref, g_ref, u_ref, dg_ref, du_ref):
        dact = lax.dot_general(dy_ref[...], w_ref[...], NT, preferred_element_type=F32)
        g = g_ref[...].astype(F32)
        u = u_ref[...].astype(F32)
        sig = _sigmoid(g)
        dg_ref[...] = (dact * u * (sig * (1.0 + g * (1.0 - sig)))).astype(BF16)
        du_ref[...] = (dact * (g * sig)).astype(BF16)

    t_spec = pl.BlockSpec((tm, tn), lambda j, i: (i, j))
    sds = jax.ShapeDtypeStruct((M, N), BF16)
    return _call_with_exchange(
        body, exchange, name=name, grid=(N // tn, M // tm),
        in_specs=[pl.BlockSpec((tm, K), lambda j, i: (i, 0)), pl.BlockSpec((tn, K), lambda j, i: (j, 0)),
                  t_spec, t_spec],
        out_specs=[t_spec, t_spec], out_shape=[sds, sds], operands=(dy, w_out, gate, up))


ROW_TILE = 256


def _colsum8(v):
    return jnp.sum(v.reshape(v.shape[0] // SUBLANES, SUBLANES, v.shape[1]), axis=0)


def _finish_colsums(step, last, refs):
    @pl.when(step == last)
    def _():
        for r in refs:
            r[...] = jnp.broadcast_to(jnp.sum(r[...], axis=0, keepdims=True), r.shape)


def _row_spec(t, d):
    return pl.BlockSpec((t, d), lambda i: (i, 0))


def _vec_spec(d, rows=1):
    return pl.BlockSpec((rows, d), lambda i: (0, 0))


def _prenorm(x, g, sc, sh, *, name):
    L, D = x.shape
    t = _tile(L, ROW_TILE, SUBLANES)

    def body(x_ref, g_ref, sc_ref, sh_ref, h_ref):
        xv = x_ref[...]
        r = lax.rsqrt(jnp.mean(xv * xv, axis=-1, keepdims=True) + RMS_EPS)
        h_ref[...] = (((xv * r) * g_ref[...]) * (1.0 + sc_ref[...]) + sh_ref[...]).astype(BF16)

    return pl.pallas_call(
        body, name=name, grid=(L // t,),
        in_specs=[_row_spec(t, D), _vec_spec(D), _vec_spec(D), _vec_spec(D)],
        out_specs=_row_spec(t, D), out_shape=jax.ShapeDtypeStruct((L, D), BF16),
        compiler_params=_params("parallel"),
    )(x, g, sc, sh)


def _postnorm_res(x, y, g, gate, res_w, *, name):
    L, D = x.shape
    t = _tile(L, ROW_TILE, SUBLANES)

    def body(x_ref, y_ref, g_ref, gate_ref, o_ref):
        yv = y_ref[...]
        r = lax.rsqrt(jnp.mean(yv * yv, axis=-1, keepdims=True) + RMS_EPS)
        o_ref[...] = x_ref[...] + (res_w * gate_ref[...]) * ((yv * r) * g_ref[...])

    return pl.pallas_call(
        body, name=name, grid=(L // t,),
        in_specs=[_row_spec(t, D), _row_spec(t, D), _vec_spec(D), _vec_spec(D)],
        out_specs=_row_spec(t, D), out_shape=jax.ShapeDtypeStruct((L, D), F32),
        compiler_params=_params("parallel"),
    )(x, y, g, gate)


def _postnorm_bwd(dxo, y, g, gate, res_w, *, name):
    L, D = y.shape
    t = _tile(L, ROW_TILE, SUBLANES)
    n = L // t

    def body(dxo_ref, y_ref, g_ref, gate_ref, dy_ref, dgate_ref, dg_ref):
        i = pl.program_id(0)

        @pl.when(i == 0)
        def _():
            dgate_ref[...] = jnp.zeros_like(dgate_ref)
            dg_ref[...] = jnp.zeros_like(dg_ref)

        yv = y_ref[...]
        dv = dxo_ref[...]
        gv = g_ref[...]
        r = lax.rsqrt(jnp.mean(yv * yv, axis=-1, keepdims=True) + RMS_EPS)
        yn = yv * r
        dgate_ref[...] += _colsum8(dv * (res_w * (yn * gv)))
        do = dv * (res_w * gate_ref[...])
        dg_ref[...] += _colsum8(do * yn)
        dyn = do * gv
        dy_ref[...] = (r * (dyn - yn * jnp.mean(dyn * yn, axis=-1, keepdims=True))).astype(BF16)
        _finish_colsums(i, n - 1, (dgate_ref, dg_ref))

    sum_sds = jax.ShapeDtypeStruct((SUBLANES, D), F32)
    return pl.pallas_call(
        body, name=name, grid=(n,),
        in_specs=[_row_spec(t, D), _row_spec(t, D), _vec_spec(D), _vec_spec(D)],
        out_specs=[_row_spec(t, D), _vec_spec(D, SUBLANES), _vec_spec(D, SUBLANES)],
        out_shape=[jax.ShapeDtypeStruct((L, D), BF16), sum_sds, sum_sds],
        compiler_params=_params("arbitrary"),
    )(dxo, y, g, gate)


def _prenorm_bwd(x, dh, dxres, g, sc, *, name):
    L, D = x.shape
    t = _tile(L, ROW_TILE, SUBLANES)
    n = L // t

    def body(x_ref, dh_ref, dxr_ref, g_ref, sc_ref, dx_ref, dsh_ref, dsc_ref, dg_ref):
        i = pl.program_id(0)

        @pl.when(i == 0)
        def _():
            dsh_ref[...] = jnp.zeros_like(dsh_ref)
            dsc_ref[...] = jnp.zeros_like(dsc_ref)
            dg_ref[...] = jnp.zeros_like(dg_ref)

        xv = x_ref[...]
        dhv = dh_ref[...].astype(F32)
        gv = g_ref[...]
        one_sc = 1.0 + sc_ref[...]
        r = lax.rsqrt(jnp.mean(xv * xv, axis=-1, keepdims=True) + RMS_EPS)
        xn = xv * r
        tt = dhv * xn
        dsh_ref[...] += _colsum8(dhv)
        dsc_ref[...] += _colsum8(tt * gv)
        dg_ref[...] += _colsum8(tt * one_sc)
        dxn = dhv * (gv * one_sc)
        dx_ref[...] = dxr_ref[...] + r * (dxn - xn * jnp.mean(dxn * xn, axis=-1, keepdims=True))
        _finish_colsums(i, n - 1, (dsh_ref, dsc_ref, dg_ref))

    sum_sds = jax.ShapeDtypeStruct((SUBLANES, D), F32)
    sum_spec = _vec_spec(D, SUBLANES)
    return pl.pallas_call(
        body, name=name, grid=(n,),
        in_specs=[_row_spec(t, D), _row_spec(t, D), _row_spec(t, D), _vec_spec(D), _vec_spec(D)],
        out_specs=[_row_spec(t, D), sum_spec, sum_spec, sum_spec],
        out_shape=[jax.ShapeDtypeStruct((L, D), F32), sum_sds, sum_sds, sum_sds],
        compiler_params=_params("arbitrary"),
    )(x, dh, dxres, g, sc)


def _loss_head(y, target, *, name):
    L, D = y.shape
    t = _tile(L, ROW_TILE, SUBLANES)
    n = L // t

    def body(y_ref, t_ref, dy_ref, loss_ref):
        i = pl.program_id(0)

        @pl.when(i == 0)
        def _():
            loss_ref[...] = jnp.zeros_like(loss_ref)

        e = y_ref[...] - t_ref[...]
        dy_ref[...] = e * (1.0 / D)
        part = jnp.sum(jnp.mean(e * e, axis=-1, keepdims=True), axis=0, keepdims=True)
        loss_ref[...] += jnp.broadcast_to(0.5 * part, loss_ref.shape)

    return pl.pallas_call(
        body, name=name, grid=(n,),
        in_specs=[_row_spec(t, D), _row_spec(t, D)],
        out_specs=[_row_spec(t, D), pl.BlockSpec((SUBLANES, LANES), lambda i: (0, 0))],
        out_shape=[jax.ShapeDtypeStruct((L, D), F32), jax.ShapeDtypeStruct((SUBLANES, LANES), F32)],
        compiler_params=_params("arbitrary"),
    )(y, target)


GELU_C = math.sqrt(2.0 / math.pi)


def _gelu_fwd(y, *, name):
    L, W = y.shape
    t = _tile(L, 512, SUBLANES)

    def body(y_ref, o_ref):
        v = y_ref[...]
        o_ref[...] = (0.5 * v * (1.0 + jnp.tanh(GELU_C * (v + 0.044715 * (v * v * v))))).astype(BF16)

    return pl.pallas_call(
        body, name=name, grid=(L // t,), in_specs=[_row_spec(t, W)], out_specs=_row_spec(t, W),
        out_shape=jax.ShapeDtypeStruct((L, W), BF16), compiler_params=_params("parallel"),
    )(y)


def _gelu_bwd(dgl, y, u, dskip, *, name):
    L, W = y.shape
    t = _tile(L, 512, SUBLANES)
    n = L // t

    def body(dgl_ref, y_ref, u_ref, d_ref, dy_ref, sk_ref, dd_ref):
        i = pl.program_id(0)

        @pl.when(i == 0)
        def _():
            dd_ref[...] = jnp.zeros_like(dd_ref)

        v = y_ref[...]
        inner = GELU_C * (v + 0.044715 * (v * v * v))
        th = jnp.tanh(inner)
        dgelu = 0.5 * (1.0 + th) + 0.5 * v * (1.0 - th * th) * (GELU_C * (1.0 + 3.0 * 0.044715 * (v * v)))
        dy = dgl_ref[...] * dgelu
        dy_ref[...] = dy.astype(BF16)
        sk_ref[...] = dy * d_ref[...]
        dd_ref[...] += _colsum8(dy * u_ref[...])
        _finish_colsums(i, n - 1, (dd_ref,))

    return pl.pallas_call(
        body, name=name, grid=(n,),
        in_specs=[_row_spec(t, W), _row_spec(t, W), _row_spec(t, W), _vec_spec(W)],
        out_specs=[_row_spec(t, W), _row_spec(t, W), _vec_spec(W, SUBLANES)],
        out_shape=[jax.ShapeDtypeStruct((L, W), BF16), jax.ShapeDtypeStruct((L, W), F32),
                   jax.ShapeDtypeStruct((SUBLANES, W), F32)],
        compiler_params=_params("arbitrary"),
    )(dgl, y, u, dskip)


def _merge_fwd(z, yb, gab, *, name):
    L, D = yb.shape
    t = _tile(L, ROW_TILE, SUBLANES)

    def body(z_ref, yb_ref, gab_ref, o_ref):
        ya = z_ref[:, :D] * _sigmoid(z_ref[:, D:])
        o_ref[...] = (_sigmoid(gab_ref[:, :D]) * ya + _sigmoid(gab_ref[:, D:]) * yb_ref[...]).astype(BF16)

    return pl.pallas_call(
        body, name=name, grid=(L // t,),
        in_specs=[_row_spec(t, 2 * D), _row_spec(t, D), _row_spec(t, 2 * D)],
        out_specs=_row_spec(t, D), out_shape=jax.ShapeDtypeStruct((L, D), BF16),
        compiler_params=_params("parallel"),
    )(z, yb, gab)


def _merge_bwd(dm, z, yb, gab, *, name):
    L, D = yb.shape
    t = _tile(L, ROW_TILE, SUBLANES)

    def body(dm_ref, z_ref, yb_ref, gab_ref, dz_ref, dyb_ref, dgab_ref):
        dmv = dm_ref[...]
        zv = z_ref[:, :D]
        sz = _sigmoid(z_ref[:, D:])
        sa = _sigmoid(gab_ref[:, :D])
        sb = _sigmoid(gab_ref[:, D:])
        ybv = yb_ref[...]
        dya = dmv * sa
        dz_ref[:, :D] = (dya * sz).astype(BF16)
        dz_ref[:, D:] = (dya * zv * (sz * (1.0 - sz))).astype(BF16)
        dyb_ref[...] = (dmv * sb).astype(BF16)
        dgab_ref[:, :D] = (dmv * (zv * sz) * (sa * (1.0 - sa))).astype(BF16)
        dgab_ref[:, D:] = (dmv * ybv * (sb * (1.0 - sb))).astype(BF16)

    return pl.pallas_call(
        body, name=name, grid=(L // t,),
        in_specs=[_row_spec(t, D), _row_spec(t, 2 * D), _row_spec(t, D), _row_spec(t, 2 * D)],
        out_specs=[_row_spec(t, 2 * D), _row_spec(t, D), _row_spec(t, 2 * D)],
        out_shape=[jax.ShapeDtypeStruct((L, 2 * D), BF16), jax.ShapeDtypeStruct((L, D), BF16),
                   jax.ShapeDtypeStruct((L, 2 * D), BF16)],
        compiler_params=_params("parallel"),
    )(dm, z, yb, gab)


SCAN_W = 1024
SCAN_T = 512


def _interleave(x):
    L, W = x.shape
    seg = SCAN_T // SUBLANES
    return x.reshape(L // SCAN_T, SUBLANES, seg, W).transpose(0, 2, 1, 3).reshape(L, W)


def _deinterleave(x):
    L, W = x.shape
    seg = SCAN_T // SUBLANES
    return x.reshape(L // SCAN_T, seg, SUBLANES, W).transpose(0, 2, 1, 3).reshape(L, W)


def _power_table(a, b, pr_tab, pi_tab, n):
    def fill(k, carry):
        pr, pi = carry
        pr_tab[k] = pr
        pi_tab[k] = pi
        return a * pr - b * pi, a * pi + b * pr

    lax.fori_loop(0, n, fill, (a, b))


def _rows_to_tile(rows):
    w = rows[0].shape[1]
    sub = lax.broadcasted_iota(jnp.int32, (SUBLANES, w), 0)
    tile = jnp.broadcast_to(rows[0], (SUBLANES, w))
    for j in range(1, SUBLANES):
        tile = jnp.where(sub == j, jnp.broadcast_to(rows[j], (SUBLANES, w)), tile)
    return tile


def _ssm_scan_fwd(bu_re, bu_im, lam_re, lam_im, *, name):
    L, S = bu_re.shape
    w, t = _tile(S, SCAN_W), SCAN_T
    seg = t // SUBLANES

    def body(br_ref, bi_ref, lr_ref, li_ref, sr_ref, si_ref, pr_tab, pi_tab, cr_ref, ci_ref):
        a = jnp.broadcast_to(lr_ref[...], (SUBLANES, w))
        b = jnp.broadcast_to(li_ref[...], (SUBLANES, w))

        @pl.when(pl.program_id(1) == 0)
        def _():
            cr_ref[...] = jnp.zeros_like(cr_ref)
            ci_ref[...] = jnp.zeros_like(ci_ref)
            _power_table(a, b, pr_tab, pi_tab, seg)

        def local_scan(i, carry):
            sr, si = carry
            base = pl.multiple_of(i * SUBLANES, SUBLANES)
            nr = a * sr - b * si + br_ref[pl.ds(base, SUBLANES), :]
            ni = a * si + b * sr + bi_ref[pl.ds(base, SUBLANES), :]
            sr_ref[pl.ds(base, SUBLANES), :] = nr
            si_ref[pl.ds(base, SUBLANES), :] = ni
            return nr, ni

        zero = jnp.zeros((SUBLANES, w), F32)
        fr, fi = lax.fori_loop(0, seg, local_scan, (zero, zero), unroll=2)
        lsr, lsi = pr_tab[seg - 1][0:1, :], pi_tab[seg - 1][0:1, :]
        cr, ci = cr_ref[...], ci_ref[...]
        rows_r, rows_i = [], []
        for j in range(SUBLANES):
            rows_r.append(cr)
            rows_i.append(ci)
            cr, ci = fr[j:j + 1, :] + (lsr * cr - lsi * ci), fi[j:j + 1, :] + (lsr * ci + lsi * cr)
        cr_ref[...] = cr
        ci_ref[...] = ci
        in_r, in_i = _rows_to_tile(rows_r), _rows_to_tile(rows_i)

        def add_entry(i, _):
            base = pl.multiple_of(i * SUBLANES, SUBLANES)
            pr, pi = pr_tab[i], pi_tab[i]
            sr_ref[pl.ds(base, SUBLANES), :] += pr * in_r - pi * in_i
            si_ref[pl.ds(base, SUBLANES), :] += pr * in_i + pi * in_r
            return 0

        lax.fori_loop(0, seg, add_entry, 0, unroll=2)

    blk = pl.BlockSpec((t, w), lambda j, i: (i, j))
    vec = pl.BlockSpec((1, w), lambda j, i: (0, j))
    sds = jax.ShapeDtypeStruct((L, S), F32)
    tab = pltpu.VMEM((seg, SUBLANES, w), F32)
    return pl.pallas_call(
        body, name=name, grid=(S // w, L // t),
        in_specs=[blk, blk, vec, vec], out_specs=[blk, blk], out_shape=[sds, sds],
        scratch_shapes=[tab, tab, pltpu.VMEM((1, w), F32), pltpu.VMEM((1, w), F32)],
        compiler_params=_params("parallel", "arbitrary"),
    )(bu_re, bu_im, lam_re, lam_im)


def _ssm_scan_bwd(d_re, d_im, s_re, s_im, lam_re, lam_im, *, name):
    L, S = d_re.shape
    w, t = _tile(S, SCAN_W), SCAN_T
    nt = L // t
    seg = t // SUBLANES

    def body(dr_ref, di_ref, sr_ref, si_ref, lr_ref, li_ref, gr_ref, gi_ref, ar_ref, ai_ref,
             pr_tab, pi_tab, cgr, cgi, acc_r, acc_i):
        step = pl.program_id(1)
        a = jnp.broadcast_to(lr_ref[...], (SUBLANES, w))
        b = jnp.broadcast_to(-li_ref[...], (SUBLANES, w))

        @pl.when(step == 0)
        def _():
            for r in (cgr, cgi, acc_r, acc_i):
                r[...] = jnp.zeros_like(r)
            _power_table(a, b, pr_tab, pi_tab, seg)

        def local_scan(ii, carry):
            gr, gi = carry
            base = pl.multiple_of((seg - 1 - ii) * SUBLANES, SUBLANES)
            ngr = a * gr - b * gi + dr_ref[pl.ds(base, SUBLANES), :]
            ngi = a * gi + b * gr + di_ref[pl.ds(base, SUBLANES), :]
            gr_ref[pl.ds(base, SUBLANES), :] = ngr
            gi_ref[pl.ds(base, SUBLANES), :] = ngi
            return ngr, ngi

        zero = jnp.zeros((SUBLANES, w), F32)
        fr, fi = lax.fori_loop(0, seg, local_scan, (zero, zero), unroll=2)
        lsr, lsi = pr_tab[seg - 1][0:1, :], pi_tab[seg - 1][0:1, :]
        cr, ci = cgr[...], cgi[...]
        rows_r, rows_i = [None] * SUBLANES, [None] * SUBLANES
        for j in reversed(range(SUBLANES)):
            rows_r[j], rows_i[j] = cr, ci
            cr, ci = fr[j:j + 1, :] + (lsr * cr - lsi * ci), fi[j:j + 1, :] + (lsr * ci + lsi * cr)
        cgr[...] = cr
        cgi[...] = ci
        in_r, in_i = _rows_to_tile(rows_r), _rows_to_tile(rows_i)

        def add_entry(ii, carry):
            nr, ni, xr, xi = carry
            base = pl.multiple_of((seg - 1 - ii) * SUBLANES, SUBLANES)
            sr = sr_ref[pl.ds(base, SUBLANES), :]
            si = si_ref[pl.ds(base, SUBLANES), :]
            xr = xr + (nr * sr + ni * si)
            xi = xi + (ni * sr - nr * si)
            pr, pi = pr_tab[ii], pi_tab[ii]
            gr = gr_ref[pl.ds(base, SUBLANES), :] + (pr * in_r - pi * in_i)
            gi = gi_ref[pl.ds(base, SUBLANES), :] + (pr * in_i + pi * in_r)
            gr_ref[pl.ds(base, SUBLANES), :] = gr
            gi_ref[pl.ds(base, SUBLANES), :] = gi
            return gr, gi, xr, xi

        _, _, xr, xi = lax.fori_loop(0, seg, add_entry, (in_r, in_i, acc_r[...], acc_i[...]), unroll=2)
        acc_r[...] = xr
        acc_i[...] = xi

        @pl.when(step == nt - 1)
        def _():
            ar_ref[...] = jnp.sum(xr, axis=0, keepdims=True)
            ai_ref[...] = jnp.sum(xi, axis=0, keepdims=True)

    blk = pl.BlockSpec((t, w), lambda j, i: (nt - 1 - i, j))
    vec = pl.BlockSpec((1, w), lambda j, i: (0, j))
    sds = jax.ShapeDtypeStruct((L, S), F32)
    vsds = jax.ShapeDtypeStruct((1, S), F32)
    tab = pltpu.VMEM((seg, SUBLANES, w), F32)
    tile = pltpu.VMEM((SUBLANES, w), F32)
    return pl.pallas_call(
        body, name=name, grid=(S // w, nt),
        in_specs=[blk, blk, blk, blk, vec, vec], out_specs=[blk, blk, vec, vec],
        out_shape=[sds, sds, vsds, vsds],
        scratch_shapes=[tab, tab, pltpu.VMEM((1, w), F32), pltpu.VMEM((1, w), F32), tile, tile],
        compiler_params=_params("parallel", "arbitrary"),
    )(d_re, d_im, s_re, s_im, lam_re, lam_im)


def _ssm_discretize(a_re, a_im, log_dt, b_re, b_im, c_re, c_im, d_skip):
    G, P, N = SSM_GROUPS, SSM_STATE, SSM_GROUP
    a = jnp.minimum(a_re, -1e-4)
    dt = jnp.exp(log_dt)[:, None]
    mag = jnp.exp(a * dt)
    lr = mag * jnp.cos(a_im * dt)
    li = mag * jnp.sin(a_im * dt)
    den = a * a + a_im * a_im
    cr = ((lr - 1.0) * a + li * a_im) / den
    ci = (li * a - (lr - 1.0) * a_im) / den
    bbr = cr[..., None] * b_re - ci[..., None] * b_im
    bbi = cr[..., None] * b_im + ci[..., None] * b_re
    gl = G // SSM_BLOCKS
    eye = jnp.eye(gl, dtype=F32)[None, :, None, :, None]

    def in_map(bb):
        t = bb.transpose(0, 2, 1).reshape(SSM_BLOCKS, gl, N, P)
        return (eye * t[:, :, :, None, :]).reshape(SSM_BLOCKS, gl * N, gl * P)

    def out_map(c):
        t = c.transpose(0, 2, 1).reshape(SSM_BLOCKS, gl, P, N)
        return (eye * t[:, :, :, None, :]).reshape(SSM_BLOCKS, gl * P, gl * N)

    return (lr.reshape(1, G * P), li.reshape(1, G * P), in_map(bbr), in_map(bbi),
            out_map(c_re), out_map(-c_im), d_skip.reshape(1, SSM_WIDTH))


ATT_T = 512
CUM_T = 256


def _split3(x):
    hi = x.astype(BF16)
    r1 = x - hi.astype(F32)
    mid = r1.astype(BF16)
    lo = (r1 - mid.astype(F32)).astype(BF16)
    return hi, mid, lo


def _tri_dot(tri, x):
    hi, mid, lo = _split3(x)
    dot = lambda p: jnp.dot(tri, p, preferred_element_type=F32)
    return dot(hi) + dot(mid) + dot(lo)


def _log_sigmoid(x):
    return jnp.minimum(x, 0.0) - jnp.log(1.0 + jnp.exp(-jnp.abs(x)))


def _fox_cum(f, fb, *, name):
    L, W = f.shape
    t = _tile(L, CUM_T, SUBLANES)

    def body(f_ref, b_ref, o_ref, carry):
        @pl.when(pl.program_id(0) == 0)
        def _():
            carry[...] = jnp.zeros_like(carry)

        row = lax.broadcasted_iota(jnp.int32, (t, t), 0)
        col = lax.broadcasted_iota(jnp.int32, (t, t), 1)
        tri = jnp.where(col <= row, 1.0, 0.0).astype(BF16)
        c = _tri_dot(tri, _log_sigmoid(f_ref[...] + b_ref[...])) + carry[...]
        o_ref[...] = c
        carry[...] = c[t - 1:t, :]

    return pl.pallas_call(
        body, name=name, grid=(L // t,),
        in_specs=[_row_spec(t, W), _vec_spec(W)], out_specs=_row_spec(t, W),
        out_shape=jax.ShapeDtypeStruct((L, W), F32),
        scratch_shapes=[pltpu.VMEM((1, W), F32)], compiler_params=_params("arbitrary"),
    )(f, fb)


def _fox_cum_bwd(dcum, f, fb, *, name):
    L, W = f.shape
    t = _tile(L, CUM_T, SUBLANES)
    n = L // t

    def body(d_ref, f_ref, b_ref, o_ref, db_ref, carry):
        i = pl.program_id(0)

        @pl.when(i == 0)
        def _():
            carry[...] = jnp.zeros_like(carry)
            db_ref[...] = jnp.zeros_like(db_ref)

        row = lax.broadcasted_iota(jnp.int32, (t, t), 0)
        col = lax.broadcasted_iota(jnp.int32, (t, t), 1)
        tri = jnp.where(col >= row, 1.0, 0.0).astype(BF16)
        dlog = _tri_dot(tri, d_ref[...]) + carry[...]
        carry[...] = dlog[0:1, :]
        df = dlog * _sigmoid(-(f_ref[...] + b_ref[...]))
        o_ref[...] = df.astype(BF16)
        db_ref[...] += _colsum8(df)
        _finish_colsums(i, n - 1, (db_ref,))

    rev = pl.BlockSpec((t, W), lambda i: (n - 1 - i, 0))
    return pl.pallas_call(
        body, name=name, grid=(n,),
        in_specs=[rev, rev, _vec_spec(W)], out_specs=[rev, _vec_spec(W, SUBLANES)],
        out_shape=[jax.ShapeDtypeStruct((L, W), BF16), jax.ShapeDtypeStruct((SUBLANES, W), F32)],
        scratch_shapes=[pltpu.VMEM((1, W), F32)], compiler_params=_params("arbitrary"),
    )(dcum, f, fb)


def _head_col(blk, h):
    lane = lax.broadcasted_iota(jnp.int32, blk.shape, 1)
    return jnp.sum(jnp.where(lane == h * HEAD_DIM, blk, 0.0), axis=1, keepdims=True)


def _lo_mask(rows):
    return lax.broadcasted_iota(jnp.int32, (rows, LANES), 1) < HEAD_DIM


def _causal(t):
    row = lax.broadcasted_iota(jnp.int32, (t, t), 0)
    col = lax.broadcasted_iota(jnp.int32, (t, t), 1)
    return col <= row


def _call_with_exchange(body, exchange, *, name, grid, in_specs, out_specs, out_shape, operands):
    sem = ("parallel",) + ("arbitrary",) * (len(grid) - 1)
    if exchange is None:
        return pl.pallas_call(body, name=name, grid=grid, in_specs=in_specs, out_specs=out_specs,
                              out_shape=out_shape, compiler_params=_params(*sem))(*operands)
    n_in, n_out = len(in_specs), len(out_specs)
    ei, eo = len(exchange.operands), len(exchange.out_shapes)

    def wrapped(*refs):
        ins, ex_in = refs[:n_in], refs[n_in:n_in + ei]
        outs, ex_out = refs[n_in + ei:n_in + ei + n_out], refs[n_in + ei + n_out:n_in + ei + n_out + eo]
        sems = refs[n_in + ei + n_out + eo:]
        ids = [pl.program_id(d) for d in range(len(grid))]
        first = functools.reduce(jnp.logical_and, [i == 0 for i in ids])
        last = functools.reduce(jnp.logical_and, [i == g - 1 for i, g in zip(ids, grid)])

        @pl.when(first)
        def _():
            exchange.start(ex_in, ex_out, sems)

        body(*ins, *outs)

        @pl.when(last)
        def _():
            exchange.finish(ex_in, ex_out, sems)

    return pl.pallas_call(
        wrapped, name=name, grid=grid, in_specs=list(in_specs) + [HBM_SPEC] * ei,
        out_specs=list(out_specs) + [HBM_SPEC] * eo, out_shape=list(out_shape) + list(exchange.out_shapes),
        scratch_shapes=exchange.sems, compiler_params=_params(*(("arbitrary",) * len(grid))),
    )(*operands, *exchange.operands)


def _fox_fwd(qkv, cum_cols, cum_rows, *, name, exchange=None):
    L = qkv.shape[0]
    t = _tile(L, ATT_T)
    nq = L // t
    npair = ATTN_HEADS // 2

    def body(q_ref, k_ref, v_ref, cc_ref, cr_ref, o_ref, o32_ref, lse_ref):
        iq = pl.program_id(1)
        lo = _lo_mask(t)
        qv = q_ref[...] * 0.125
        zq = jnp.zeros_like(qv)
        qh = (jnp.where(lo, qv, zq), jnp.where(lo, zq, qv))
        ccv = cc_ref[...]
        cq = (_head_col(ccv, 0), _head_col(ccv, 1))

        def step(ik, carry, masked):
            start = pl.multiple_of(ik * t, t)
            kb = k_ref[pl.ds(start, t), :]
            vb = v_ref[pl.ds(start, t), :]
            out = []
            for h in range(2):
                m, l, acc = carry[h]
                s = lax.dot_general(qh[h], kb, NT, preferred_element_type=F32)
                s = s + (cq[h] - cr_ref[h:h + 1, pl.ds(start, t)])
                if masked:
                    s = jnp.where(_causal(t), s, -jnp.inf)
                m_new = jnp.maximum(m, jnp.max(s, axis=1, keepdims=True))
                alpha = jnp.exp(m - m_new)
                p = jnp.exp(s - m_new)
                l = alpha * l + jnp.sum(p, axis=1, keepdims=True)
                acc = alpha * acc + jnp.dot(p.astype(BF16), vb, preferred_element_type=F32)
                out.append((m_new, l, acc))
            return tuple(out)

        init1 = (jnp.full((t, 1), -jnp.inf, F32), jnp.zeros((t, 1), F32), jnp.zeros((t, LANES), F32))
        carry = lax.fori_loop(0, iq, lambda ik, c: step(ik, c, False), (init1, init1))
        (m0, l0, a0), (m1, l1, a1) = step(iq, carry, True)
        out = jnp.where(lo, a0 / l0, a1 / l1)
        o_ref[...] = out.astype(BF16)
        o32_ref[...] = out
        lse_ref[...] = jnp.where(lo, m0 + jnp.log(l0), m1 + jnp.log(l1))

    blk = lambda off: pl.BlockSpec((t, LANES), lambda hp, iq: (iq, off + hp))
    whole = lambda off: pl.BlockSpec((L, LANES), lambda hp, iq: (0, off + hp))
    return _call_with_exchange(
        body, exchange, name=name, grid=(npair, nq),
        in_specs=[blk(0), whole(npair), whole(2 * npair), blk(0),
                  pl.BlockSpec((None, SUBLANES, L), lambda hp, iq: (hp, 0, 0))],
        out_specs=[blk(0), blk(0), blk(0)],
        out_shape=[jax.ShapeDtypeStruct((L, ATTN_WIDTH), BF16), jax.ShapeDtypeStruct((L, ATTN_WIDTH), F32),
                   jax.ShapeDtypeStruct((L, ATTN_WIDTH), F32)],
        operands=(qkv, qkv, qkv, cum_cols, cum_rows))


STAT_LSE, STAT_CUM, STAT_DELTA = 0, 2, 4


def _lane_col(blk, idx):
    lane = lax.broadcasted_iota(jnp.int32, blk.shape, 1)
    return jnp.sum(jnp.where(lane == idx, blk, 0.0), axis=1, keepdims=True)


def _fox_rowstats(do, o, lse, cum_cols, *, name):
    L = do.shape[0]
    t = _tile(L, ATT_T)

    def body(do_ref, o_ref, lse_ref, cc_ref, st_ref):
        lo = _lo_mask(t)
        dd = do_ref[...].astype(F32) * o_ref[...]
        lsev, ccv = lse_ref[...], cc_ref[...]
        cols = (_head_col(lsev, 0), _head_col(lsev, 1), _head_col(ccv, 0), _head_col(ccv, 1),
                jnp.sum(jnp.where(lo, dd, 0.0), axis=1, keepdims=True),
                jnp.sum(jnp.where(lo, 0.0, dd), axis=1, keepdims=True))
        lane = lax.broadcasted_iota(jnp.int32, (t, LANES), 1)
        out = jnp.zeros((t, LANES), F32)
        for i, col in enumerate(cols):
            out = jnp.where(lane == i, col, out)
        st_ref[...] = out

    blk = pl.BlockSpec((t, LANES), lambda hp, i: (i, hp))
    return pl.pallas_call(
        body, name=name, grid=(ATTN_HEADS // 2, L // t),
        in_specs=[blk, blk, blk, blk], out_specs=blk,
        out_shape=jax.ShapeDtypeStruct((L, ATTN_WIDTH), F32),
        compiler_params=_params("parallel", "parallel"),
    )(do, o, lse, cum_cols)


def _fox_bwd(qkv, do, stats, cum_rows, *, name, exchange=None):
    L = qkv.shape[0]
    t = _tile(L, ATT_T)
    nq = L // t
    npair = ATTN_HEADS // 2

    def body(q_ref, do_ref, st_ref, qt_ref, dot_ref, k_ref, v_ref, cr_ref, dk_ref, dv_ref, dc_ref, dq_ref, drow_ref):
        ik = pl.program_id(1)

        @pl.when(ik == 0)
        def _():
            dq_ref[...] = jnp.zeros_like(dq_ref)
            drow_ref[...] = jnp.zeros_like(drow_ref)

        lo = _lo_mask(t)
        lo_rows = lax.broadcasted_iota(jnp.int32, (LANES, t), 0) < HEAD_DIM
        lane = lax.broadcasted_iota(jnp.int32, (t, LANES), 1)
        kb = k_ref[...]
        vb = v_ref[...]
        zk = jnp.zeros_like(kb)
        kh = (jnp.where(lo, kb, zk), jnp.where(lo, zk, kb))
        vh = (jnp.where(lo, vb, zk), jnp.where(lo, zk, vb))
        ck = (cr_ref[0:1, :], cr_ref[1:2, :])

        def step(iq, carry, masked):
            dk, dv, dc0, dc1 = carry
            start = pl.multiple_of(iq * t, t)
            qb = q_ref[pl.ds(start, t), :] * 0.125
            dob = do_ref[pl.ds(start, t), :]
            stb = st_ref[pl.ds(start, t), :]
            qtb = qt_ref[:, pl.ds(start, t)] * 0.125
            dotb = dot_ref[:, pl.ds(start, t)]
            dks, dvs, dcs, dqs, rss = [], [], [], [], []
            for h in range(2):
                s = lax.dot_general(qb, kh[h], NT, preferred_element_type=F32)
                s = s + (_lane_col(stb, STAT_CUM + h) - ck[h])
                if masked:
                    s = jnp.where(_causal(t), s, -jnp.inf)
                p = jnp.exp(s - _lane_col(stb, STAT_LSE + h))
                dp = lax.dot_general(dob, vh[h], NT, preferred_element_type=F32)
                ds = p * (dp - _lane_col(stb, STAT_DELTA + h))
                dsb = ds.astype(BF16)
                dvs.append(jnp.dot(dotb, p.astype(BF16), preferred_element_type=F32))
                dks.append(jnp.dot(qtb, dsb, preferred_element_type=F32))
                dqs.append(jnp.dot(dsb, kh[h], preferred_element_type=F32))
                dcs.append(jnp.sum(ds, axis=0, keepdims=True))
                rss.append(jnp.sum(ds, axis=1, keepdims=True))
            dq_ref[pl.ds(start, t), :] += 0.125 * (dqs[0] + dqs[1])
            drow_ref[pl.ds(start, t), :] += jnp.where(lane == 0, rss[0], jnp.where(lane == 1, rss[1], 0.0))
            return (dk + jnp.where(lo_rows, dks[0], dks[1]), dv + jnp.where(lo_rows, dvs[0], dvs[1]),
                    dc0 - dcs[0], dc1 - dcs[1])

        zero = jnp.zeros((LANES, t), F32)
        zrow = jnp.zeros((1, t), F32)
        carry = step(ik, (zero, zero, zrow, zrow), True)
        dk, dv, dc0, dc1 = lax.fori_loop(ik + 1, nq, lambda iq, c: step(iq, c, False), carry)
        dk_ref[...] = dk.T.astype(BF16)
        dv_ref[...] = dv.T.astype(BF16)
        dc_ref[...] = jnp.zeros_like(dc_ref)
        dc_ref[0:1, :] = dc0
        dc_ref[1:2, :] = dc1

    whole = lambda off: pl.BlockSpec((L, LANES), lambda hp, ik: (0, off + hp))
    blk = lambda off: pl.BlockSpec((t, LANES), lambda hp, ik: (ik, off + hp))
    rows = pl.BlockSpec((None, SUBLANES, t), lambda hp, ik: (hp, 0, ik))
    whole_t = pl.BlockSpec((LANES, L), lambda hp, ik: (hp, 0))
    return _call_with_exchange(
        body, exchange, name=name, grid=(npair, nq),
        in_specs=[whole(0), whole(0), whole(0), whole_t, whole_t, blk(npair), blk(2 * npair), rows],
        out_specs=[blk(0), blk(0), rows, whole(0), whole(0)],
        out_shape=[jax.ShapeDtypeStruct((L, ATTN_WIDTH), BF16), jax.ShapeDtypeStruct((L, ATTN_WIDTH), BF16),
                   jax.ShapeDtypeStruct((npair, SUBLANES, L), F32),
                   jax.ShapeDtypeStruct((L, ATTN_WIDTH), F32), jax.ShapeDtypeStruct((L, ATTN_WIDTH), F32)],
        operands=(qkv, do, stats, qkv[:, :ATTN_WIDTH].T, do.T, qkv, qkv, cum_rows))


def _mod_partial(c_all, mod_w, mod_b_cols, *, name):
    depth, K, cols = mod_w.shape
    tn = _tile(cols, 768)

    def body(c_ref, w_ref, b_ref, o_ref):
        cv = c_ref[...]
        sc = (cv * _sigmoid(cv)).astype(BF16)
        o_ref[...] = jnp.dot(sc, w_ref[...].astype(BF16), preferred_element_type=F32) + b_ref[...]

    return pl.pallas_call(
        body, name=name, grid=(depth, cols // tn),
        in_specs=[pl.BlockSpec((N_DEV, K), lambda l, j: (0, 0)),
                  pl.BlockSpec((None, K, tn), lambda l, j: (l, 0, j)),
                  pl.BlockSpec((None, 1, tn), lambda l, j: (l, 0, j))],
        out_specs=pl.BlockSpec((None, N_DEV, tn), lambda l, j: (l, 0, j)),
        out_shape=jax.ShapeDtypeStruct((depth, N_DEV, cols), F32),
        compiler_params=_params("parallel", "parallel"),
    )(c_all, mod_w, mod_b_cols)


def _mod_wgrad(c_all_t, dmod, *, name):
    depth, nb, cols = dmod.shape
    K = c_all_t.shape[0]
    tn = _tile(cols, 768)
    tk = _tile(K, 256, SUBLANES)

    def body(c_ref, d_ref, o_ref):
        cv = c_ref[...]
        sc = cv * _sigmoid(cv)
        dv = d_ref[...]
        acc = sc[:, 0:1] * dv[0:1, :]
        for b in range(1, nb):
            acc = acc + sc[:, b:b + 1] * dv[b:b + 1, :]
        o_ref[...] = acc

    return pl.pallas_call(
        body, name=name, grid=(depth, K // tk, cols // tn),
        in_specs=[pl.BlockSpec((tk, nb), lambda l, i, j: (i, 0)),
                  pl.BlockSpec((None, nb, tn), lambda l, i, j: (l, 0, j))],
        out_specs=pl.BlockSpec((None, tk, tn), lambda l, i, j: (l, i, j)),
        out_shape=jax.ShapeDtypeStruct((depth, K, cols), F32),
        compiler_params=_params("parallel", "parallel", "parallel"),
    )(c_all_t, dmod)


def _adamw(w, g, m, v, *, name):
    shape = w.shape
    cols = shape[-1]
    rows = int(np.prod(shape[:-1]))
    t = _tile(rows, 256, SUBLANES) if rows % SUBLANES == 0 else rows
    r2 = lambda a: a.reshape(rows, cols)

    def body(w_ref, g_ref, m_ref, v_ref, d_ref, nm_ref, nv_ref):
        gv = g_ref[...]
        nm = ADAM_B1 * m_ref[...] + (1.0 - ADAM_B1) * gv
        nv = ADAM_B2 * v_ref[...] + (1.0 - ADAM_B2) * (gv * gv)
        m_hat = nm / (1.0 - ADAM_B1 ** ADAM_STEP)
        v_hat = nv / (1.0 - ADAM_B2 ** ADAM_STEP)
        d_ref[...] = -ADAM_LR * (m_hat / (jnp.sqrt(v_hat) + ADAM_EPS) + ADAM_WD * w_ref[...])
        nm_ref[...] = nm
        nv_ref[...] = nv

    spec = pl.BlockSpec((t, cols), lambda i: (i, 0))
    sds = jax.ShapeDtypeStruct((rows, cols), F32)
    d, nm, nv = pl.pallas_call(
        body, name=name, grid=(rows // t,),
        in_specs=[spec] * 4, out_specs=[spec] * 3, out_shape=[sds] * 3,
        compiler_params=_params("parallel"),
    )(r2(w), r2(g), r2(m), r2(v))
    return d.reshape(shape), nm.reshape(shape), nv.reshape(shape)


def _my_place():
    return lax.axis_index("x"), lax.axis_index("y"), lax.axis_index("c")


def _other_chips(x, y):
    return [(1 - x, y), (x, 1 - y), (1 - x, 1 - y)]


def _all_gather8(v, *, name, with_sum=False):
    m, n = v.shape

    def body(x_ref, out_ref, *rest):
        if with_sum:
            sum_ref, send_sems, recv_sems, local_sem = rest
        else:
            send_sems, recv_sems, local_sem = rest
        x, y, c = _my_place()
        me, sibling = (x, y, c), (x, y, 1 - c)
        chips = _other_chips(x, y)

        def rows(px, py, pc):
            return out_ref.at[pl.ds((4 * px + 2 * py + pc) * m, m), :]

        def copy(k, block, to, src=None):
            return pltpu.make_async_remote_copy(
                src_ref=rows(*block) if src is None else src, dst_ref=rows(*block),
                send_sem=send_sems.at[k], recv_sem=recv_sems.at[k], device_id=to, device_id_type=MESH)

        mine = pltpu.make_async_copy(x_ref, rows(*me), local_sem)
        mine.start()
        first = [copy(0, me, sibling, src=x_ref)]
        first += [copy(1 + j, me, (*chip, c), src=x_ref) for j, chip in enumerate(chips)]
        for cp in first:
            cp.start()
        passed = [copy(4 + j, (*chip, c), sibling) for j, chip in enumerate(chips)]
        for j, chip in enumerate(chips):
            copy(1 + j, (*chip, c), me).wait_recv()
            passed[j].start()
        copy(0, sibling, me).wait_recv()
        for j, chip in enumerate(chips):
            copy(4 + j, (*chip, 1 - c), me).wait_recv()
        for cp in first + passed:
            cp.wait_send()
        mine.wait()
        if with_sum:
            acc = out_ref[pl.ds(0, m), :]
            for d in range(1, N_DEV):
                acc = acc + out_ref[pl.ds(d * m, m), :]
            sum_ref[...] = acc

    vm = pl.BlockSpec(memory_space=pltpu.VMEM)
    out_shape = [jax.ShapeDtypeStruct((N_DEV * m, n), F32)]
    if with_sum:
        out_shape.append(jax.ShapeDtypeStruct((m, n), F32))
    res = pl.pallas_call(
        body, name=name, out_shape=out_shape, in_specs=[vm], out_specs=[vm] * len(out_shape),
        scratch_shapes=[pltpu.SemaphoreType.DMA((7,)), pltpu.SemaphoreType.DMA((7,)), pltpu.SemaphoreType.DMA],
        compiler_params=pltpu.CompilerParams(vmem_limit_bytes=VMEM_LIMIT),
    )(v)
    return res if with_sum else res[0]


class _Cut(NamedTuple):
    shape: tuple
    slab: int
    half: int


IN_WIDTH = SSM_WIDTH + 3 * ATTN_WIDTH + ATTN_HEADS + 2 * D_MODEL
CUTS = dict(
    ffn_w_in=_Cut((1, D_MODEL, 2 * D_FF), 2, 1),
    ffn_w_out=_Cut((1, D_FF, D_MODEL), 1, 2),
    mix_w_in=_Cut((N_CHIPS, D_MODEL, IN_WIDTH // N_CHIPS), 0, 1),
    glu_w=_Cut((1, SSM_WIDTH, 2 * D_MODEL), 2, 1),
    attn_w_out=_Cut((1, ATTN_WIDTH, D_MODEL), 2, 1),
    mix_w_out=_Cut((1, D_MODEL, D_MODEL), 1, 2),
)
LAYER_MATS = (("ffn_w_in", 0), ("ffn_w_in", 1), ("ffn_w_out", 0), ("ffn_w_out", 1), ("mix_w_in", None),
              ("glu_w", None), ("attn_w_out", None), ("mix_w_out", None))
FIRST_MATS = (0, 2)
MIXER_MATS = (4, 5, 6, 7)
LATE_MATS = (1, 3)


def _part_shape(cut, slab=False, half=False):
    s = list(cut.shape)
    if slab:
        s[cut.slab] //= N_CHIPS
    if half:
        s[cut.half] //= 2
    return tuple(s)


def _window(ref, cut, slab=None, half=None):
    idx = [slice(None)] * len(cut.shape)
    for axis, parts, which in ((cut.slab, N_CHIPS, slab), (cut.half, 2, half)):
        if which is not None:
            width = cut.shape[axis] // parts
            idx[axis] = pl.ds(pl.multiple_of(which * width, width), width)
    return ref.at[tuple(idx)]


HBM_SPEC = pl.BlockSpec(memory_space=pltpu.HBM)


class _Exchange(NamedTuple):
    operands: list
    out_shapes: list
    sems: list
    start: object
    finish: object


def _run_exchange(ex, *, name):
    ni, no = len(ex.operands), len(ex.out_shapes)

    def body(*refs):
        parts = (refs[:ni], refs[ni:ni + no], refs[ni + no:])
        ex.start(*parts)
        ex.finish(*parts)

    return pl.pallas_call(
        body, name=name, out_shape=ex.out_shapes, in_specs=[HBM_SPEC] * ni, out_specs=[HBM_SPEC] * no,
        scratch_shapes=ex.sems,
    )(*ex.operands)


def _gather_exchange(shards, cuts):
    n = len(shards)

    def setup(s_refs, f_refs, sems):
        send_sems, recv_sems = sems
        x, y, c = _my_place()
        me, sibling, mine = (x, y, c), (x, y, 1 - c), 2 * x + y
        chips = _other_chips(x, y)

        def copy(i, k, src, dst, to):
            return pltpu.make_async_remote_copy(
                src_ref=src, dst_ref=dst, send_sem=send_sems.at[7 * i + k], recv_sem=recv_sems.at[7 * i + k],
                device_id=to, device_id_type=MESH)

        def landed(i, j, half):
            return _window(f_refs[i], cuts[i], slab=2 * chips[j][0] + chips[j][1], half=half)

        def sends():
            own = [copy(i, 6, s_refs[i], _window(f_refs[i], cuts[i], slab=mine), sibling) for i in range(n)]
            return own + [copy(i, j, _window(s_refs[i], cuts[i], half=c),
                               _window(f_refs[i], cuts[i], slab=mine, half=c), (*chips[j], c))
                          for i in range(n) for j in range(3)]

        return c, me, sibling, copy, landed, sends

    def start(s_refs, f_refs, sems):
        for cp in setup(s_refs, f_refs, sems)[-1]():
            cp.start()

    def finish(s_refs, f_refs, sems):
        c, me, sibling, copy, landed, sends = setup(s_refs, f_refs, sems)
        passed = []
        for i in range(n):
            for j in range(3):
                copy(i, j, landed(i, j, c), landed(i, j, c), me).wait_recv()
                passed.append(copy(i, 3 + j, landed(i, j, c), landed(i, j, c), sibling))
                passed[-1].start()
        for i in range(n):
            for j in range(3):
                copy(i, 3 + j, landed(i, j, 1 - c), landed(i, j, 1 - c), me).wait_recv()
        for i in range(n):
            mine_i = _window(f_refs[i], cuts[i], slab=2 * me[0] + me[1])
            copy(i, 6, mine_i, mine_i, me).wait_recv()
        for cp in sends() + passed:
            cp.wait_send()

    return _Exchange(
        list(shards), [jax.ShapeDtypeStruct(cut.shape, s.dtype) for s, cut in zip(shards, cuts)],
        [pltpu.SemaphoreType.DMA((7 * n,)), pltpu.SemaphoreType.DMA((7 * n,))], start, finish)


def _swap_layer(mats, cuts, *, name):
    n = len(mats)

    def body(*refs):
        m_refs, r_refs = refs[:n], refs[n:2 * n]
        send_sems, recv_sems = refs[2 * n:]
        x, y, c = _my_place()
        cps = [pltpu.make_async_remote_copy(
            src_ref=_window(m_refs[i], cuts[i], half=1 - c), dst_ref=r_refs[i], send_sem=send_sems.at[i],
            recv_sem=recv_sems.at[i], device_id=(x, y, 1 - c), device_id_type=MESH) for i in range(n)]
        for cp in cps:
            cp.start()
        for cp in cps:
            cp.wait()

    return pl.pallas_call(
        body, name=name, out_shape=[jax.ShapeDtypeStruct(_part_shape(cut, half=True), m.dtype)
                                    for m, cut in zip(mats, cuts)],
        in_specs=[HBM_SPEC] * n, out_specs=[HBM_SPEC] * n,
        scratch_shapes=[pltpu.SemaphoreType.DMA((n,)), pltpu.SemaphoreType.DMA((n,))],
    )(*mats)


def _partials_exchange(sums, cuts):
    n = len(sums)

    def copies(s_refs, p_refs, sems):
        send_sems, recv_sems = sems
        x, y, c = _my_place()
        return [pltpu.make_async_remote_copy(
            src_ref=_window(s_refs[i], cuts[i], slab=2 * chip[0] + chip[1]), dst_ref=p_refs[i].at[j],
            send_sem=send_sems.at[3 * i + j], recv_sem=recv_sems.at[3 * i + j],
            device_id=(*chip, c), device_id_type=MESH)
            for i in range(n) for j, chip in enumerate(_other_chips(x, y))]

    def start(s_refs, p_refs, sems):
        for cp in copies(s_refs, p_refs, sems):
            cp.start()

    def finish(s_refs, p_refs, sems):
        for cp in copies(s_refs, p_refs, sems):
            cp.wait()

    return _Exchange(
        list(sums), [jax.ShapeDtypeStruct((3,) + _part_shape(cut, slab=True, half=True), s.dtype)
                     for s, cut in zip(sums, cuts)],
        [pltpu.SemaphoreType.DMA((3 * n,)), pltpu.SemaphoreType.DMA((3 * n,))], start, finish)


def _share_all(dests, cuts, places, *, name):
    names = list(dests)
    nn, n = len(names), len(places)

    def body(*refs):
        o_refs = dict(zip(names, refs[nn:2 * nn]))
        send_sems, recv_sems = refs[2 * nn:]
        x, y, c = _my_place()

        def win(i, half):
            slab_cut = _Cut(_part_shape(cuts[i], slab=True), cuts[i].slab, cuts[i].half)
            return _window(o_refs[places[i][0]].at[places[i][1]], slab_cut, half=half)

        def copy(i, half):
            return pltpu.make_async_remote_copy(
                src_ref=win(i, half), dst_ref=win(i, half), send_sem=send_sems.at[i], recv_sem=recv_sems.at[i],
                device_id=(x, y, 1 - c), device_id_type=MESH)

        for i in range(n):
            copy(i, c).start()
        for i in range(n):
            copy(i, c).wait_send()
            copy(i, 1 - c).wait_recv()

    return pl.pallas_call(
        body, name=name, out_shape=[jax.ShapeDtypeStruct(dests[k].shape, F32) for k in names],
        in_specs=[HBM_SPEC] * nn, out_specs=[HBM_SPEC] * nn,
        input_output_aliases={i: i for i in range(nn)},
        scratch_shapes=[pltpu.SemaphoreType.DMA((n,)), pltpu.SemaphoreType.DMA((n,))],
    )(*[dests[k] for k in names])


def _cut_blocks(shape):
    _, R, C = shape
    tr = _tile(R, 256, 16)
    tc = _tile(C, 2048) if C % LANES == 0 else C
    return (None, tr, tc), (shape[0], R // tr, C // tc)


def _offset_map(axis, blocks):
    def index_map(b, i, j, which):
        idx = [b, i, j]
        idx[axis] = which[0] * blocks[axis] + idx[axis]
        return tuple(idx)
    return index_map


def _add_half(mat, other, cut, c_idx, *, name):
    shape = _part_shape(cut, half=True)
    block, grid = _cut_blocks(shape)

    def body(c_ref, m_ref, o_ref, f_ref, b_ref):
        s = m_ref[...] + o_ref[...]
        f_ref[...] = s
        b_ref[...] = s.astype(BF16)

    plain = pl.BlockSpec(block, lambda b, i, j, which: (b, i, j))
    grid_spec = pltpu.PrefetchScalarGridSpec(
        num_scalar_prefetch=1, grid=grid,
        in_specs=[pl.BlockSpec(block, _offset_map(cut.half, grid)), plain], out_specs=[plain, plain])
    return pl.pallas_call(
        body, name=name, grid_spec=grid_spec,
        out_shape=[jax.ShapeDtypeStruct(shape, F32), jax.ShapeDtypeStruct(shape, BF16)],
        compiler_params=_params("parallel", "parallel", "parallel"),
    )(c_idx, mat, other)


def _sum_slab(own, parts, cut, dest, place, chip_idx, c_idx, *, name):
    shape = _part_shape(cut, slab=True, half=True)
    block, grid = _cut_blocks(shape)
    assert shape[0] == 1

    def body(k_ref, c_ref, o_ref, p_ref, dest_ref, out_ref):
        acc = o_ref[...]
        for j in range(3):
            acc = acc + p_ref[j].astype(F32)
        out_ref[...] = acc

    def own_map(b, i, j, chip, core):
        idx = [b, i, j]
        idx[cut.slab] = chip[0] * grid[cut.slab] + idx[cut.slab]
        return tuple(idx)

    def dest_map(b, i, j, chip, core):
        idx = [b, i, j]
        idx[cut.half] = core[0] * grid[cut.half] + idx[cut.half]
        return tuple(place) + tuple(idx)

    grid_spec = pltpu.PrefetchScalarGridSpec(
        num_scalar_prefetch=2, grid=grid,
        in_specs=[pl.BlockSpec(block, own_map),
                  pl.BlockSpec((3,) + block[1:], lambda b, i, j, chip, core: (0, i, j)),
                  pl.BlockSpec(memory_space=pl.ANY)],
        out_specs=pl.BlockSpec((None,) * len(place) + block, dest_map))
    return pl.pallas_call(
        body, name=name, grid_spec=grid_spec, out_shape=jax.ShapeDtypeStruct(dest.shape, F32),
        input_output_aliases={4: 0},
        compiler_params=_params("parallel", "parallel", "parallel"),
    )(chip_idx, c_idx, own, parts.reshape((3,) + shape[1:]), dest)


def _pad_rows(flat, cols=8 * LANES, align=SUBLANES):
    n = flat.shape[0]
    rows = -(-n // (cols * align)) * align
    return jnp.pad(flat, (0, rows * cols - n)).reshape(rows, cols)


def _row(v):
    return v.reshape(1, -1)


def _ffn_fwd(x, mod, g_pre, g_post, w_in, w_out, tag, exchange=None):
    sh, sc, gate = _row(mod[0]), _row(mod[1]), _row(mod[2])
    h = _prenorm(x, _row(g_pre), sc, sh, name=f"prenorm_{tag}")
    gt, up, act, *exchanged = _mm_swiglu(h, w_in, name=f"swiglu_{tag}", exchange=exchange)
    y = _mm(act, w_out, name=f"ffn_out_{tag}", tk=D_FF)
    x_out = _postnorm_res(x, y, _row(g_post), gate, FFN_RES, name=f"postnorm_{tag}")
    return x_out, (x, h, gt, up, act, y), exchanged


def _ffn_bwd(dxo, saved, mod, g_pre, g_post, w_in, w_out, tag, exchange=None):
    x, h, gt, up, act, y = saved
    sc, gate = _row(mod[1]), _row(mod[2])
    dy, dgate, dgpost = _postnorm_bwd(dxo, y, _row(g_post), gate, FFN_RES, name=f"postnorm_bwd_{tag}")
    dgt, dup, *exchanged = _mm_swiglu_bwd(dy, w_out, gt, up, name=f"swiglu_bwd_{tag}", exchange=exchange)
    dw_out = _mm(act, dy, ta=True, name=f"dw_out_{tag}", tm=1408, tn=1024, tk=1024)
    dh = _mm(dgt, w_in, tb=True, name=f"dh_gate_{tag}", tk=D_FF)
    dh = _mm(dup, w_in, tb=True, b_k0=D_FF, bias=dh, name=f"dh_up_{tag}", tk=D_FF)
    dw_in = _mm(h, dgt, ta=True, out_n=2 * D_FF, name=f"dw_gate_{tag}", tm=1024, tn=1408, tk=1024)
    dw_in = _mm(h, dup, ta=True, out_n=2 * D_FF, out_j0=D_FF, into=dw_in, name=f"dw_up_{tag}",
                tm=1024, tn=1408, tk=1024)
    dx, dsh, dsc, dgpre = _prenorm_bwd(x, dh, dxo, _row(g_pre), sc, name=f"prenorm_bwd_{tag}")
    dmod = jnp.stack([dsh[0], dsc[0], dgate[0]])
    return dx, dmod, dgpre[0], dgpost[0], dw_in, dw_out, exchanged


def _split_mix_w_in(w):
    u0, q0, f0, g0 = 0, SSM_WIDTH, SSM_WIDTH + 3 * ATTN_WIDTH, SSM_WIDTH + 3 * ATTN_WIDTH + ATTN_HEADS
    w_f = jnp.pad(w[:, f0:g0], ((0, 0), (0, LANES - ATTN_HEADS)))
    return w[:, u0:q0], w[:, q0:f0], w_f, w[:, g0:]


def _mixer_fwd(x, mod, g_pre, g_post, w, ssm, forget_b, tag, exchange=None):
    L = x.shape[0]
    sh, sc, gate = _row(mod[0]), _row(mod[1]), _row(mod[2])
    lam_re, lam_im, bin_re, bin_im, cout_re, cout_im, dskip = ssm
    h = _prenorm(x, _row(g_pre), sc, sh, name=f"prenorm_{tag}")
    u = _mm(h, w["u"], name=f"proj_u_{tag}")
    qkv = _mm(h, w["qkv"], out_dtype=BF16, name=f"proj_qkv_{tag}")
    f = _mm(h, w["f"], name=f"proj_f_{tag}")
    gab = _mm(h, w["gab"], name=f"proj_gab_{tag}")
    u_i = _interleave(u)
    bu_re, bu_im = _mm_bd([u_i], [bin_re, bin_im], name=f"ssm_bu_{tag}")
    s_re, s_im = _ssm_scan_fwd(bu_re, bu_im, lam_re, lam_im, name=f"ssm_scan_{tag}")
    y_ssm = _deinterleave(_mm_bd([s_re, s_im], [cout_re, cout_im], bias=u_i, bscale=dskip, name=f"ssm_y_{tag}"))
    gl = _gelu_fwd(y_ssm, name=f"gelu_{tag}")
    z = _mm(gl, w["glu"], name=f"glu_{tag}")
    fb = jnp.pad(forget_b, (0, LANES - ATTN_HEADS)).reshape(1, LANES)
    cum = _fox_cum(f, fb, name=f"fox_cum_{tag}")
    cum8 = cum[:, :ATTN_HEADS]
    cum_cols = jnp.repeat(cum8, HEAD_DIM, axis=1)
    cum_rows = jnp.pad(cum8.T.reshape(ATTN_HEADS // 2, 2, L), ((0, 0), (0, SUBLANES - 2), (0, 0)))
    attn, attn32, lse, *exchanged = _fox_fwd(qkv, cum_cols, cum_rows, name=f"fox_fwd_{tag}", exchange=exchange)
    yb = _mm(attn, w["attn_out"], name=f"attn_out_{tag}")
    merged = _merge_fwd(z, yb, gab, name=f"merge_{tag}")
    y = _mm(merged, w["out"], name=f"mix_out_{tag}")
    x_out = _postnorm_res(x, y, _row(g_post), gate, 1.0, name=f"postnorm_{tag}")
    saved = (x, h, u, u_i, qkv, f, gab, s_re, s_im, y_ssm, gl, z, fb, cum_cols, cum_rows, attn, attn32, lse, yb,
             merged, y)
    return x_out, saved, exchanged


def _mixer_bwd(dxo, saved, mod, g_pre, g_post, w, ssm, tag, exchange=None):
    (x, h, u, u_i, qkv, f, gab, s_re, s_im, y_ssm, gl, z, fb, cum_cols, cum_rows, attn, attn32, lse, yb,
     merged, y) = saved
    L = x.shape[0]
    sc, gate = _row(mod[1]), _row(mod[2])
    lam_re, lam_im, bin_re, bin_im, cout_re, cout_im, dskip = ssm
    dy, dgate, dgpost = _postnorm_bwd(dxo, y, _row(g_post), gate, 1.0, name=f"postnorm_bwd_{tag}")
    dmerged = _mm(dy, w["out"], tb=True, name=f"dmerged_{tag}")
    dw_out = _mm(merged, dy, ta=True, name=f"dw_mix_out_{tag}", tm=1024, tn=1024)
    dz, dyb, dgab = _merge_bwd(dmerged, z, yb, gab, name=f"merge_bwd_{tag}")
    dgl = _mm(dz, w["glu"], tb=True, name=f"dgl_{tag}", tk=2048)
    dw_glu = _mm(gl, dz, ta=True, name=f"dw_glu_{tag}", tm=512, tn=2048)
    dys, dsk, dd = _gelu_bwd(dgl, y_ssm, u, dskip, name=f"gelu_bwd_{tag}")
    dys, dsk = _interleave(dys), _interleave(dsk)
    d_re, d_im = _mm_bd([dys], [cout_re, cout_im], tb=True, name=f"ssm_ds_{tag}")
    dcout_re = _mm_bd_t(s_re, dys, SSM_BLOCKS, name=f"ssm_dc_re_{tag}")
    dcout_im = _mm_bd_t(s_im, dys, SSM_BLOCKS, name=f"ssm_dc_im_{tag}")
    g_re, g_im, dlam_re, dlam_im = _ssm_scan_bwd(d_re, d_im, s_re, s_im, lam_re, lam_im, name=f"ssm_scan_bwd_{tag}")
    du = _mm_bd([g_re, g_im], [bin_re, bin_im], tb=True, bias=dsk, out_dtype=BF16, name=f"ssm_du_{tag}")
    du = _deinterleave(du)
    dbin_re = _mm_bd_t(u_i, g_re, SSM_BLOCKS, name=f"ssm_db_re_{tag}")
    dbin_im = _mm_bd_t(u_i, g_im, SSM_BLOCKS, name=f"ssm_db_im_{tag}")
    dssm = (dlam_re, dlam_im, dbin_re, dbin_im, dcout_re, dcout_im, _row(dd[0]))
    dattn = _mm(dyb, w["attn_out"], tb=True, out_dtype=BF16, name=f"dattn_{tag}")
    dw_attn = _mm(attn, dyb, ta=True, name=f"dw_attn_out_{tag}", tm=512, tn=1024)
    stats = _fox_rowstats(dattn, attn32, lse, cum_cols, name=f"fox_rowstats_{tag}")
    dk, dv, dcum_rows, dq, drow, *exchanged = _fox_bwd(qkv, dattn, stats, cum_rows, name=f"fox_bwd_{tag}",
                                                       exchange=exchange)
    drow8 = drow.reshape(L, ATTN_HEADS // 2, LANES)[:, :, :2].reshape(L, ATTN_HEADS)
    dcum = drow8 + dcum_rows[:, :2, :].reshape(ATTN_HEADS, L).T
    dcum = jnp.pad(dcum, ((0, 0), (0, LANES - ATTN_HEADS)))
    df, dfb = _fox_cum_bwd(dcum, f, fb, name=f"fox_cum_bwd_{tag}")
    dqkv = jnp.concatenate([dq.astype(BF16), dk, dv], axis=1)
    dh = _mm(dqkv, w["qkv"], tb=True, name=f"dh_qkv_{tag}", tk=1536)
    dh = _mm(du, w["u"], tb=True, bias=dh, name=f"dh_u_{tag}")
    dh = _mm(dgab, w["gab"], tb=True, bias=dh, name=f"dh_gab_{tag}", tk=2048)
    dh = _mm(df, w["f"], tb=True, bias=dh, name=f"dh_f_{tag}")
    dw_u = _mm(h, du, ta=True, name=f"dw_u_{tag}", tm=1024, tn=512)
    dw_qkv = _mm(h, dqkv, ta=True, name=f"dw_qkv_{tag}", tm=1024, tn=1536)
    dw_f = _mm(h, df, ta=True, name=f"dw_f_{tag}", tm=1024)
    dw_gab = _mm(h, dgab, ta=True, name=f"dw_gab_{tag}", tm=1024, tn=1024)
    dw_in = jnp.concatenate([dw_u, dw_qkv, dw_f[:, :ATTN_HEADS], dw_gab], axis=1)
    dx, dsh, dsc, dgpre = _prenorm_bwd(x, dh, dxo, _row(g_pre), sc, name=f"prenorm_bwd_{tag}")
    dmod = jnp.stack([dsh[0], dsc[0], dgate[0]])
    grads = dict(mix_w_in=dw_in, glu_w=dw_glu, attn_w_out=dw_attn, mix_w_out=dw_out,
                 forget_b=dfb[0, :ATTN_HEADS])
    return dx, dmod, dgpre[0], dgpost[0], grads, dssm, exchanged


SSM_NAMES = ("ssm_a_re", "ssm_a_im", "ssm_log_dt", "ssm_b_re", "ssm_b_im", "ssm_c_re", "ssm_c_im", "ssm_d")
SMALL_NAMES = ("forget_b",) + SSM_NAMES
WEIGHT_NAMES = ("mod_w", "mod_b", "norm_pre", "norm_post", "ffn_w_in", "ffn_w_out", "mix_w_in", "forget_b") \
    + SSM_NAMES + ("glu_w", "attn_w_out", "mix_w_out")


def _layer_shards(w, l):
    return [(w[n][l] if j is None else w[n][l, j]).astype(BF16).reshape(_part_shape(CUTS[n], slab=True))
            for n, j in LAYER_MATS]


def _train_step(x, c, target, w, m, v):
    xi, yi, ci = _my_place()
    chip = 2 * xi + yi
    dev = 4 * xi + 2 * yi + ci
    mod_cols = N_SUB * 3 * D_MODEL // N_CHIPS
    norm_cols = D_MODEL // N_CHIPS

    cuts = [CUTS[n] for n, _ in LAYER_MATS]
    shards = [_layer_shards(w, l) for l in range(DEPTH)]
    full = [[None] * len(LAYER_MATS) for _ in range(DEPTH)]

    def gather(keys):
        return _gather_exchange([shards[l][i] for l, i in keys], [cuts[i] for _, i in keys])

    def store(keys, mats):
        for (l, i), a in zip(keys, mats):
            full[l][i] = a

    first = lambda l: [(l, i) for i in FIRST_MATS]
    mixer = lambda l: [(l, i) for i in MIXER_MATS]
    late = lambda l: [(l, i) for i in LATE_MATS]
    store(first(0), _run_exchange(gather(first(0)), name="gather_weights_first"))
    c_all = _all_gather8(jnp.pad(c, ((0, SUBLANES - 1), (0, 0))), name="gather_c")[::SUBLANES]
    mod_b_cols = lax.dynamic_slice_in_dim(w["mod_b"], chip * mod_cols, mod_cols, axis=1)[:, None, :]
    mod_part = _mod_partial(c_all, w["mod_w"], mod_b_cols, name="mod_partial")
    small_fwd = jnp.concatenate([mod_part.reshape(-1), w["norm_pre"].reshape(-1), w["norm_post"].reshape(-1)])
    n_mod, n_norm = mod_part.size, w["norm_pre"].size
    sf_all = _all_gather8(_pad_rows(small_fwd), name="gather_mod").reshape(N_DEV, -1)
    sf_chips = sf_all[::2]
    mod_all = jnp.concatenate(
        [sf_chips[k, :n_mod].reshape(DEPTH, N_DEV, mod_cols) for k in range(N_CHIPS)], axis=2)
    mod_mine = lax.dynamic_index_in_dim(mod_all, dev, axis=1, keepdims=False).reshape(DEPTH, N_SUB, 3, D_MODEL)
    norm_pre = jnp.concatenate(
        [sf_chips[k, n_mod:n_mod + n_norm].reshape(DEPTH, N_SUB, norm_cols) for k in range(N_CHIPS)], axis=2)
    norm_post = jnp.concatenate(
        [sf_chips[k, n_mod + n_norm:n_mod + 2 * n_norm].reshape(DEPTH, N_SUB, norm_cols) for k in range(N_CHIPS)],
        axis=2)

    saved, layer_w, ssm_prep, ssm_vjp = [], [], [], []
    h = x
    for l in range(DEPTH):
        lw = dict(ffn=[(full[l][0][0], full[l][2][0]), None])
        prep, vjp = jax.vjp(_ssm_discretize, *[w[n][l] for n in SSM_NAMES])
        layer_w.append(lw)
        ssm_prep.append(prep)
        ssm_vjp.append(vjp)
        h, s0, arrived = _ffn_fwd(h, mod_mine[l, 0], norm_pre[l, 0], norm_post[l, 0], *lw["ffn"][0], tag=f"l{l}a",
                                  exchange=gather(mixer(l)))
        store(mixer(l), arrived)
        mix_in, glu, attn_out, mix_out = full[l][4:]
        w_u, w_qkv, w_f, w_gab = _split_mix_w_in(mix_in.transpose(1, 0, 2).reshape(D_MODEL, IN_WIDTH))
        lw["mix"] = dict(u=w_u, qkv=w_qkv, f=w_f, gab=w_gab, glu=glu[0], attn_out=attn_out[0], out=mix_out[0])
        coming = late(l) + (first(l + 1) if l + 1 < DEPTH else [])
        h, s1, arrived = _mixer_fwd(h, mod_mine[l, 1], norm_pre[l, 1], norm_post[l, 1], lw["mix"], prep,
                                    w["forget_b"][l], tag=f"l{l}m", exchange=gather(coming))
        store(coming, arrived)
        lw["ffn"][1] = (full[l][1][0], full[l][3][0])
        h, s2, _ = _ffn_fwd(h, mod_mine[l, 2], norm_pre[l, 2], norm_post[l, 2], *lw["ffn"][1], tag=f"l{l}b")
        saved.append((s0, s1, s2))
    dh, loss8 = _loss_head(h, target, name="loss_head")

    c_idx = ci.reshape(1).astype(jnp.int32)
    chip_idx = chip.reshape(1).astype(jnp.int32)
    places = []
    dests = {n: lax.empty(((DEPTH,) if j is None else (DEPTH, 2)) + _part_shape(CUTS[n], slab=True), F32)
             for n, j in LAYER_MATS}
    g_small = {n: [None] * DEPTH for n in SMALL_NAMES}
    dmod, dnpre, dnpost = [], [], []
    pending = []

    def begin_reduce(l, idx, mats, tag):
        cs = [cuts[i] for i in idx]
        mats = [a.reshape(cut.shape) for a, cut in zip(mats, cs)]
        from_sibling = _swap_layer(mats, cs, name=f"grad_swap_{tag}")
        for k, i in enumerate(idx):
            f, b = _add_half(mats[k], from_sibling[k], cs[k], c_idx, name=f"grad_add_l{l}_{i}")
            pending.append((l, i, f, b))

    def partials():
        return _partials_exchange([p[3] for p in pending], [cuts[p[1]] for p in pending])

    def end_reduce(parts):
        for (l, i, f, _), part in zip(pending, parts):
            n, j = LAYER_MATS[i]
            place = (l,) if j is None else (l, j)
            dests[n] = _sum_slab(f, part, cuts[i], dests[n], place, chip_idx, c_idx, name=f"grad_sum_l{l}_{i}")
            places.append((n, place))
        pending.clear()

    for l in reversed(range(DEPTH)):
        lw = layer_w[l]
        dh, dm2, dp2, dq2, dwin2, dwout2, _ = _ffn_bwd(dh, saved[l][2], mod_mine[l, 2], norm_pre[l, 2],
                                                       norm_post[l, 2], *lw["ffn"][1], tag=f"l{l}b")
        begin_reduce(l, LATE_MATS, [dwin2, dwout2], f"l{l}b")
        dh, dm1, dp1, dq1, gmix, dssm, parts = _mixer_bwd(dh, saved[l][1], mod_mine[l, 1], norm_pre[l, 1],
                                                          norm_post[l, 1], lw["mix"], ssm_prep[l], tag=f"l{l}m",
                                                          exchange=partials())
        end_reduce(parts)
        dmix_in = gmix["mix_w_in"].reshape(D_MODEL, N_CHIPS, IN_WIDTH // N_CHIPS).transpose(1, 0, 2)
        begin_reduce(l, MIXER_MATS, [dmix_in, gmix["glu_w"], gmix["attn_w_out"], gmix["mix_w_out"]], f"l{l}m")
        dh, dm0, dp0, dq0, dwin0, dwout0, parts = _ffn_bwd(dh, saved[l][0], mod_mine[l, 0], norm_pre[l, 0],
                                                           norm_post[l, 0], *lw["ffn"][0], tag=f"l{l}a",
                                                           exchange=partials())
        end_reduce(parts)
        dmod.insert(0, jnp.stack([dm0, dm1, dm2]))
        dnpre.insert(0, jnp.stack([dp0, dp1, dp2]))
        dnpost.insert(0, jnp.stack([dq0, dq1, dq2]))
        g_small["forget_b"][l] = gmix["forget_b"]
        for n, g in zip(SSM_NAMES, ssm_vjp[l](dssm)):
            g_small[n][l] = g
        begin_reduce(l, FIRST_MATS, [dwin0, dwout0], f"l{l}a")
    end_reduce(_run_exchange(partials(), name="grad_partials_last"))
    grad_x = dh
    g_small = {n: jnp.stack(g) for n, g in g_small.items()}

    small = [loss8[0, :1], jnp.stack(dmod).reshape(-1), jnp.stack(dnpre).reshape(-1), jnp.stack(dnpost).reshape(-1)]
    small += [g_small[n].reshape(-1) for n in SMALL_NAMES]
    sizes = [int(s.size) for s in small]
    offs = np.concatenate([[0], np.cumsum(sizes)])
    sb_all, sb_sum = _all_gather8(_pad_rows(jnp.concatenate(small)), name="gather_small_grads", with_sum=True)
    sb_sum = sb_sum.reshape(-1)
    take = lambda i: sb_sum[int(offs[i]):int(offs[i + 1])]
    loss = take(0)[0]
    grads = {"mod_b": take(1).reshape(DEPTH, N_SUB * 3 * D_MODEL)}
    dnorm_pre_full = take(2).reshape(DEPTH, N_SUB, D_MODEL)
    dnorm_post_full = take(3).reshape(DEPTH, N_SUB, D_MODEL)
    grads["norm_pre"] = lax.dynamic_slice_in_dim(dnorm_pre_full, chip * norm_cols, norm_cols, axis=2)
    grads["norm_post"] = lax.dynamic_slice_in_dim(dnorm_post_full, chip * norm_cols, norm_cols, axis=2)
    for i, n in enumerate(SMALL_NAMES):
        grads[n] = take(4 + i).reshape(w[n].shape)
    dmod_all = sb_all.reshape(N_DEV, -1)[:, int(offs[1]):int(offs[2])].reshape(N_DEV, DEPTH, N_SUB * 3 * D_MODEL)
    dmod_cols = lax.dynamic_slice_in_dim(dmod_all, chip * mod_cols, mod_cols, axis=2).transpose(1, 0, 2)
    grads["mod_w"] = _mod_wgrad(c_all.T, dmod_cols, name="mod_wgrad")

    shared = _share_all(dests, [CUTS[n] for n, _ in places], places, name="grad_share")
    for n, g in zip(dests, shared):
        grads[n] = g.reshape(w[n].shape)

    delta, new_m, new_v = {}, {}, {}
    for n in WEIGHT_NAMES:
        delta[n], new_m[n], new_v[n] = _adamw(w[n], grads[n], m[n], v[n], name=f"adamw_{n}")
    outs = [loss, grad_x[None]]
    for group in (grads, delta, new_m, new_v):
        outs += [group[n] for n in WEIGHT_NAMES]
    return tuple(outs)


def kernel(x, c, mod_w, mod_b, norm_pre, norm_post, ffn_w_in, ffn_w_out, mix_w_in, forget_b, ssm_a_re, ssm_a_im, ssm_log_dt, ssm_b_re, ssm_b_im, ssm_c_re, ssm_c_im, ssm_d, glu_w, attn_w_out, mix_w_out, loss_target, m_mod_w, m_mod_b, m_norm_pre, m_norm_post, m_ffn_w_in, m_ffn_w_out, m_mix_w_in, m_forget_b, m_ssm_a_re, m_ssm_a_im, m_ssm_log_dt, m_ssm_b_re, m_ssm_b_im, m_ssm_c_re, m_ssm_c_im, m_ssm_d, m_glu_w, m_attn_w_out, m_mix_w_out, v_mod_w, v_mod_b, v_norm_pre, v_norm_post, v_ffn_w_in, v_ffn_w_out, v_mix_w_in, v_forget_b, v_ssm_a_re, v_ssm_a_im, v_ssm_log_dt, v_ssm_b_re, v_ssm_b_im, v_ssm_c_re, v_ssm_c_im, v_ssm_d, v_glu_w, v_attn_w_out, v_mix_w_out):
    w = dict(mod_w=mod_w, mod_b=mod_b, norm_pre=norm_pre, norm_post=norm_post, ffn_w_in=ffn_w_in,
             ffn_w_out=ffn_w_out, mix_w_in=mix_w_in, forget_b=forget_b, ssm_a_re=ssm_a_re, ssm_a_im=ssm_a_im,
             ssm_log_dt=ssm_log_dt, ssm_b_re=ssm_b_re, ssm_b_im=ssm_b_im, ssm_c_re=ssm_c_re, ssm_c_im=ssm_c_im,
             ssm_d=ssm_d, glu_w=glu_w, attn_w_out=attn_w_out, mix_w_out=mix_w_out)
    m = dict(mod_w=m_mod_w, mod_b=m_mod_b, norm_pre=m_norm_pre, norm_post=m_norm_post, ffn_w_in=m_ffn_w_in,
             ffn_w_out=m_ffn_w_out, mix_w_in=m_mix_w_in, forget_b=m_forget_b, ssm_a_re=m_ssm_a_re,
             ssm_a_im=m_ssm_a_im, ssm_log_dt=m_ssm_log_dt, ssm_b_re=m_ssm_b_re, ssm_b_im=m_ssm_b_im,
             ssm_c_re=m_ssm_c_re, ssm_c_im=m_ssm_c_im, ssm_d=m_ssm_d, glu_w=m_glu_w, attn_w_out=m_attn_w_out,
             mix_w_out=m_mix_w_out)
    v = dict(mod_w=v_mod_w, mod_b=v_mod_b, norm_pre=v_norm_pre, norm_post=v_norm_post, ffn_w_in=v_ffn_w_in,
             ffn_w_out=v_ffn_w_out, mix_w_in=v_mix_w_in, forget_b=v_forget_b, ssm_a_re=v_ssm_a_re,
             ssm_a_im=v_ssm_a_im, ssm_log_dt=v_ssm_log_dt, ssm_b_re=v_ssm_b_re, ssm_b_im=v_ssm_b_im,
             ssm_c_re=v_ssm_c_re, ssm_c_im=v_ssm_c_im, ssm_d=v_ssm_d, glu_w=v_glu_w, attn_w_out=v_attn_w_out,
             mix_w_out=v_mix_w_out)
    return _train_step(x[0], c, loss_target[0], w, m, v)
```

```python
import functools
import math
from typing import NamedTuple

import jax
import jax.numpy as jnp
import numpy as np
from jax import lax
from jax.experimental import pallas as pl
from jax.experimental.pallas import tpu as pltpu

F32 = jnp.float32
BF16 = jnp.bfloat16

D_MODEL = 1024
DEPTH = 2
SSM_WIDTH = 512
SSM_GROUP = 16
SSM_GROUPS = 32
SSM_STATE = 64
SSM_FLAT = SSM_GROUPS * SSM_STATE
SSM_BLOCKS = 4
ATTN_HEADS = 8
HEAD_DIM = 64
ATTN_WIDTH = 512
D_FF = 2816
FFN_RES = 0.5
N_SUB = 3
RMS_EPS = 1e-6
N_CHIPS = 4
N_DEV = 8

ADAM_LR = 0.001
ADAM_B1 = 0.9
ADAM_B2 = 0.999
ADAM_EPS = 1e-08
ADAM_WD = 0.01
ADAM_STEP = 10

LANES = 128
SUBLANES = 8
VMEM_LIMIT = 52 * 1024 * 1024
MESH = pl.DeviceIdType.MESH

NN = (((1,), (0,)), ((), ()))
NT = (((1,), (1,)), ((), ()))
TN = (((0,), (0,)), ((), ()))


def _tile(dim, target, align=LANES):
    best = None
    t = align
    while t <= min(dim, target):
        if dim % t == 0:
            best = t
        t += align
    return dim if best is None else best


def _params(*sem):
    return pltpu.CompilerParams(dimension_semantics=sem, vmem_limit_bytes=VMEM_LIMIT)


def _mm(a, b, *, name, ta=False, tb=False, out_dtype=F32, bias=None, bscale=None, b_k0=0,
        out_n=None, out_j0=0, into=None, tm=512, tn=1024, tk=1024):
    M, K = (a.shape[1], a.shape[0]) if ta else a.shape
    N = b.shape[0] if tb else b.shape[1]
    assert b_k0 + K <= (b.shape[1] if tb else b.shape[0]), (a.shape, b.shape, ta, tb)
    tm, tn, tk = _tile(M, tm), _tile(N, tn), _tile(K, tk)
    nk = K // tk
    assert b_k0 % tk == 0
    kb0 = b_k0 // tk
    dn = (((0 if ta else 1,), (1 if tb else 0,)), ((), ()))
    has_bias, has_scale = bias is not None, bscale is not None

    def body(*refs):
        a_ref, b_ref = refs[0], refs[1]
        pos = 2
        bias_ref = scale_ref = None
        if has_bias:
            bias_ref = refs[pos]
            pos += 1
        if has_scale:
            scale_ref = refs[pos]
            pos += 1
        if into is not None:
            pos += 1
        o_ref = refs[pos]
        acc_ref = refs[pos + 1] if nk > 1 else None

        def finish(r):
            if has_bias:
                extra = bias_ref[...].astype(F32)
                if has_scale:
                    extra = extra * scale_ref[...]
                r = r + extra
            o_ref[...] = r.astype(out_dtype)

        part = lax.dot_general(a_ref[...].astype(BF16), b_ref[...].astype(BF16), dn,
                               preferred_element_type=F32)
        if nk == 1:
            finish(part)
        else:
            k = pl.program_id(2)

            @pl.when(k == 0)
            def _():
                acc_ref[...] = part

            @pl.when(k > 0)
            def _():
                acc_ref[...] += part

            @pl.when(k == nk - 1)
            def _():
                finish(acc_ref[...])

    a_spec = pl.BlockSpec((tk, tm), lambda j, i, k: (k, i)) if ta else pl.BlockSpec((tm, tk), lambda j, i, k: (i, k))
    b_spec = (pl.BlockSpec((tn, tk), lambda j, i, k: (j, kb0 + k)) if tb
              else pl.BlockSpec((tk, tn), lambda j, i, k: (kb0 + k, j)))
    in_specs = [a_spec, b_spec]
    args = [a, b]
    if has_bias:
        in_specs.append(pl.BlockSpec((tm, tn), lambda j, i, k: (i, j)))
        args.append(bias)
    if has_scale:
        in_specs.append(pl.BlockSpec((1, tn), lambda j, i, k: (0, j)))
        args.append(bscale)
    aliases = {}
    if into is not None:
        in_specs.append(pl.BlockSpec(memory_space=pl.ANY))
        args.append(into)
        aliases = {len(args) - 1: 0}
    out_n = N if out_n is None else out_n
    assert out_j0 % tn == 0
    jb0 = out_j0 // tn
    return pl.pallas_call(
        body, name=name,
        grid=(N // tn, M // tm, nk),
        in_specs=in_specs,
        out_specs=pl.BlockSpec((tm, tn), lambda j, i, k: (i, jb0 + j)),
        out_shape=jax.ShapeDtypeStruct((M, out_n), out_dtype),
        scratch_shapes=[pltpu.VMEM((tm, tn), F32)] if nk > 1 else [],
        input_output_aliases=aliases,
        compiler_params=_params("parallel", "parallel", "arbitrary"),
    )(*args)


def _mm_bd(a_list, b_list, *, name, tb=False, out_dtype=F32, bias=None, bscale=None, tm=1024):
    G = b_list[0].shape[0]
    Kb, Nb = (b_list[0].shape[2], b_list[0].shape[1]) if tb else b_list[0].shape[1:]
    M = a_list[0].shape[0]
    tm = _tile(M, tm)
    na, nb = len(a_list), len(b_list)
    n_out = nb if na == 1 else 1
    dn = NT if tb else NN
    has_bias, has_scale = bias is not None, bscale is not None

    def body(*refs):
        a_refs, b_refs = refs[:na], refs[na:na + nb]
        pos = na + nb
        bias_ref = scale_ref = None
        if has_bias:
            bias_ref = refs[pos]
            pos += 1
        if has_scale:
            scale_ref = refs[pos]
            pos += 1
        o_refs = refs[pos:]
        prods = [lax.dot_general(a_refs[min(i, na - 1)][...].astype(BF16), b_refs[i][...].astype(BF16), dn,
                                 preferred_element_type=F32) for i in range(nb)]
        outs = prods if n_out == nb else [functools.reduce(jnp.add, prods)]
        for o_ref, r in zip(o_refs, outs):
            if has_bias:
                extra = bias_ref[...].astype(F32)
                r = r + (extra * scale_ref[...] if has_scale else extra)
            o_ref[...] = r.astype(out_dtype)

    a_spec = pl.BlockSpec((tm, Kb), lambda g, i: (i, g))
    b_spec = pl.BlockSpec((None,) + b_list[0].shape[1:], lambda g, i: (g, 0, 0))
    o_spec = pl.BlockSpec((tm, Nb), lambda g, i: (i, g))
    in_specs = [a_spec] * na + [b_spec] * nb
    args = list(a_list) + list(b_list)
    if has_bias:
        in_specs.append(o_spec)
        args.append(bias)
    if has_scale:
        in_specs.append(pl.BlockSpec((1, Nb), lambda g, i: (0, g)))
        args.append(bscale)
    sds = jax.ShapeDtypeStruct((M, G * Nb), out_dtype)
    res = pl.pallas_call(
        body, name=name, grid=(G, M // tm), in_specs=in_specs, out_specs=[o_spec] * n_out,
        out_shape=[sds] * n_out, compiler_params=_params("parallel", "parallel"),
    )(*args)
    return res[0] if n_out == 1 else res


def _mm_bd_t(a, b, G, *, name, tk=1024):
    K, Mb, Nb = a.shape[0], a.shape[1] // G, b.shape[1] // G
    tk = _tile(K, tk)
    nk = K // tk

    def body(a_ref, b_ref, o_ref, acc_ref):
        k = pl.program_id(1)
        part = lax.dot_general(a_ref[...].astype(BF16), b_ref[...].astype(BF16), TN, preferred_element_type=F32)

        @pl.when(k == 0)
        def _():
            acc_ref[...] = part

        @pl.when(k > 0)
        def _():
            acc_ref[...] += part

        @pl.when(k == nk - 1)
        def _():
            o_ref[...] = acc_ref[...]

    return pl.pallas_call(
        body, name=name, grid=(G, nk),
        in_specs=[pl.BlockSpec((tk, Mb), lambda g, k: (k, g)), pl.BlockSpec((tk, Nb), lambda g, k: (k, g))],
        out_specs=pl.BlockSpec((None, Mb, Nb), lambda g, k: (g, 0, 0)),
        out_shape=jax.ShapeDtypeStruct((G, Mb, Nb), F32),
        scratch_shapes=[pltpu.VMEM((Mb, Nb), F32)], compiler_params=_params("parallel", "arbitrary"),
    )(a, b)


def _sigmoid(x):
    return 0.5 * jnp.tanh(0.5 * x) + 0.5


def _mm_swiglu(h, w_in, *, name, tm=512, tn=1408, exchange=None):
    M, K = h.shape
    N = w_in.shape[1] // 2
    tm, tn = _tile(M, tm), _tile(N, tn)
    nj = N // tn

    def body(h_ref, wg_ref, wu_ref, g_ref, u_ref, a_ref):
        hv = h_ref[...]
        g = jnp.dot(hv, wg_ref[...], preferred_element_type=F32)
        u = jnp.dot(hv, wu_ref[...], preferred_element_type=F32)
        g_ref[...] = g.astype(BF16)
        u_ref[...] = u.astype(BF16)
        a_ref[...] = (g * _sigmoid(g) * u).astype(BF16)

    o_spec = pl.BlockSpec((tm, tn), lambda j, i: (i, j))
    sds = jax.ShapeDtypeStruct((M, N), BF16)
    return _call_with_exchange(
        body, exchange, name=name, grid=(nj, M // tm),
        in_specs=[pl.BlockSpec((tm, K), lambda j, i: (i, 0)), pl.BlockSpec((K, tn), lambda j, i: (0, j)),
                  pl.BlockSpec((K, tn), lambda j, i: (0, nj + j))],
        out_specs=[o_spec, o_spec, o_spec], out_shape=[sds, sds, sds], operands=(h, w_in, w_in))


def _mm_swiglu_bwd(dy, w_out, gate, up, *, name, tm=512, tn=1408, exchange=None):
    M, K = dy.shape
    N = w_out.shape[0]
    tm, tn = _tile(M, tm), _tile(N, tn)

    def body(dy_ref, w_ref, g_ref, u_ref, dg_ref, du_ref):
        dact = lax.dot_general(dy_ref[...], w_ref[...], NT, preferred_element_type=F32)
        g = g_ref[...].astype(F32)
        u = u_ref[...].astype(F32)
        sig = _sigmoid(g)
        dg_ref[...] = (dact * u * (sig * (1.0 + g * (1.0 - sig)))).astype(BF16)
        du_ref[...] = (dact * (g * sig)).astype(BF16)

    t_spec = pl.BlockSpec((tm, tn), lambda j, i: (i, j))
    sds = jax.ShapeDtypeStruct((M, N), BF16)
    return _call_with_exchange(
        body, exchange, name=name, grid=(N // tn, M // tm),
        in_specs=[pl.BlockSpec((tm, K), lambda j, i: (i, 0)), pl.BlockSpec((tn, K), lambda j, i: (j, 0)),
                  t_spec, t_spec],
        out_specs=[t_spec, t_spec], out_shape=[sds, sds], operands=(dy, w_out, gate, up))


ROW_TILE = 256


def _colsum8(v):
    return jnp.sum(v.reshape(v.shape[0] // SUBLANES, SUBLANES, v.shape[1]), axis=0)


def _finish_colsums(step, last, refs):
    @pl.when(step == last)
    def _():
        for r in refs:
            r[...] = jnp.broadcast_to(jnp.sum(r[...], axis=0, keepdims=True), r.shape)


def _row_spec(t, d):
    return pl.BlockSpec((t, d), lambda i: (i, 0))


def _vec_spec(d, rows=1):
    return pl.BlockSpec((rows, d), lambda i: (0, 0))


def _prenorm(x, g, sc, sh, *, name):
    L, D = x.shape
    t = _tile(L, ROW_TILE, SUBLANES)

    def body(x_ref, g_ref, sc_ref, sh_ref, h_ref):
        xv = x_ref[...]
        r = lax.rsqrt(jnp.mean(xv * xv, axis=-1, keepdims=True) + RMS_EPS)
        h_ref[...] = (((xv * r) * g_ref[...]) * (1.0 + sc_ref[...]) + sh_ref[...]).astype(BF16)

    return pl.pallas_call(
        body, name=name, grid=(L // t,),
        in_specs=[_row_spec(t, D), _vec_spec(D), _vec_spec(D), _vec_spec(D)],
        out_specs=_row_spec(t, D), out_shape=jax.ShapeDtypeStruct((L, D), BF16),
        compiler_params=_params("parallel"),
    )(x, g, sc, sh)


def _mm_postnorm(a, b, x, g, gate, res_w, *, name, tm=512):
    M, K = a.shape
    D = b.shape[1]
    tm = _tile(M, tm, SUBLANES)

    def body(a_ref, b_ref, x_ref, g_ref, gate_ref, y_ref, o_ref):
        yv = jnp.dot(a_ref[...], b_ref[...], preferred_element_type=F32)
        y_ref[...] = yv
        r = lax.rsqrt(jnp.mean(yv * yv, axis=-1, keepdims=True) + RMS_EPS)
        o_ref[...] = x_ref[...] + (res_w * gate_ref[...]) * ((yv * r) * g_ref[...])

    sds = jax.ShapeDtypeStruct((M, D), F32)
    return pl.pallas_call(
        body, name=name, grid=(M // tm,),
        in_specs=[_row_spec(tm, K), pl.BlockSpec((K, D), lambda i: (0, 0)), _row_spec(tm, D), _vec_spec(D),
                  _vec_spec(D)],
        out_specs=[_row_spec(tm, D), _row_spec(tm, D)], out_shape=[sds, sds],
        compiler_params=_params("parallel"),
    )(a, b, x, g, gate)


def _postnorm_bwd(dxo, y, g, gate, res_w, *, name, exchange=None):
    L, D = y.shape
    t = _tile(L, ROW_TILE, SUBLANES)
    n = L // t

    def body(dxo_ref, y_ref, g_ref, gate_ref, dy_ref, dgate_ref, dg_ref):
        i = pl.program_id(0)

        @pl.when(i == 0)
        def _():
            dgate_ref[...] = jnp.zeros_like(dgate_ref)
            dg_ref[...] = jnp.zeros_like(dg_ref)

        yv = y_ref[...]
        dv = dxo_ref[...]
        gv = g_ref[...]
        r = lax.rsqrt(jnp.mean(yv * yv, axis=-1, keepdims=True) + RMS_EPS)
        yn = yv * r
        dgate_ref[...] += _colsum8(dv * (res_w * (yn * gv)))
        do = dv * (res_w * gate_ref[...])
        dg_ref[...] += _colsum8(do * yn)
        dyn = do * gv
        dy_ref[...] = (r * (dyn - yn * jnp.mean(dyn * yn, axis=-1, keepdims=True))).astype(BF16)
        _finish_colsums(i, n - 1, (dgate_ref, dg_ref))

    sum_sds = jax.ShapeDtypeStruct((SUBLANES, D), F32)
    return _call_with_exchange(
        body, exchange, name=name, grid=(n,), sem=("arbitrary",),
        in_specs=[_row_spec(t, D), _row_spec(t, D), _vec_spec(D), _vec_spec(D)],
        out_specs=[_row_spec(t, D), _vec_spec(D, SUBLANES), _vec_spec(D, SUBLANES)],
        out_shape=[jax.ShapeDtypeStruct((L, D), BF16), sum_sds, sum_sds], operands=(dxo, y, g, gate))


def _prenorm_bwd(x, dh, dxres, g, sc, *, name):
    L, D = x.shape
    t = _tile(L, ROW_TILE, SUBLANES)
    n = L // t

    def body(x_ref, dh_ref, dxr_ref, g_ref, sc_ref, dx_ref, dsh_ref, dsc_ref, dg_ref):
        i = pl.program_id(0)

        @pl.when(i == 0)
        def _():
            dsh_ref[...] = jnp.zeros_like(dsh_ref)
            dsc_ref[...] = jnp.zeros_like(dsc_ref)
            dg_ref[...] = jnp.zeros_like(dg_ref)

        xv = x_ref[...]
        dhv = dh_ref[...].astype(F32)
        gv = g_ref[...]
        one_sc = 1.0 + sc_ref[...]
        r = lax.rsqrt(jnp.mean(xv * xv, axis=-1, keepdims=True) + RMS_EPS)
        xn = xv * r
        tt = dhv * xn
        dsh_ref[...] += _colsum8(dhv)
        dsc_ref[...] += _colsum8(tt * gv)
        dg_ref[...] += _colsum8(tt * one_sc)
        dxn = dhv * (gv * one_sc)
        dx_ref[...] = dxr_ref[...] + r * (dxn - xn * jnp.mean(dxn * xn, axis=-1, keepdims=True))
        _finish_colsums(i, n - 1, (dsh_ref, dsc_ref, dg_ref))

    sum_sds = jax.ShapeDtypeStruct((SUBLANES, D), F32)
    sum_spec = _vec_spec(D, SUBLANES)
    return pl.pallas_call(
        body, name=name, grid=(n,),
        in_specs=[_row_spec(t, D), _row_spec(t, D), _row_spec(t, D), _vec_spec(D), _vec_spec(D)],
        out_specs=[_row_spec(t, D), sum_spec, sum_spec, sum_spec],
        out_shape=[jax.ShapeDtypeStruct((L, D), F32), sum_sds, sum_sds, sum_sds],
        compiler_params=_params("arbitrary"),
    )(x, dh, dxres, g, sc)


def _loss_head(y, target, *, name):
    L, D = y.shape
    t = _tile(L, ROW_TILE, SUBLANES)
    n = L // t

    def body(y_ref, t_ref, dy_ref, loss_ref):
        i = pl.program_id(0)

        @pl.when(i == 0)
        def _():
            loss_ref[...] = jnp.zeros_like(loss_ref)

        e = y_ref[...] - t_ref[...]
        dy_ref[...] = e * (1.0 / D)
        part = jnp.sum(jnp.mean(e * e, axis=-1, keepdims=True), axis=0, keepdims=True)
        loss_ref[...] += jnp.broadcast_to(0.5 * part, loss_ref.shape)

    return pl.pallas_call(
        body, name=name, grid=(n,),
        in_specs=[_row_spec(t, D), _row_spec(t, D)],
        out_specs=[_row_spec(t, D), pl.BlockSpec((SUBLANES, LANES), lambda i: (0, 0))],
        out_shape=[jax.ShapeDtypeStruct((L, D), F32), jax.ShapeDtypeStruct((SUBLANES, LANES), F32)],
        compiler_params=_params("arbitrary"),
    )(y, target)


GELU_C = math.sqrt(2.0 / math.pi)


def _gelu_fwd(y, *, name):
    L, W = y.shape
    t = _tile(L, 512, SUBLANES)

    def body(y_ref, o_ref):
        v = y_ref[...]
        o_ref[...] = (0.5 * v * (1.0 + jnp.tanh(GELU_C * (v + 0.044715 * (v * v * v))))).astype(BF16)

    return pl.pallas_call(
        body, name=name, grid=(L // t,), in_specs=[_row_spec(t, W)], out_specs=_row_spec(t, W),
        out_shape=jax.ShapeDtypeStruct((L, W), BF16), compiler_params=_params("parallel"),
    )(y)


def _gelu_bwd(dgl, y, u, dskip, *, name):
    L, W = y.shape
    t = _tile(L, 512, SUBLANES)
    n = L // t

    def body(dgl_ref, y_ref, u_ref, d_ref, dy_ref, sk_ref, dd_ref):
        i = pl.program_id(0)

        @pl.when(i == 0)
        def _():
            dd_ref[...] = jnp.zeros_like(dd_ref)

        v = y_ref[...]
        inner = GELU_C * (v + 0.044715 * (v * v * v))
        th = jnp.tanh(inner)
        dgelu = 0.5 * (1.0 + th) + 0.5 * v * (1.0 - th * th) * (GELU_C * (1.0 + 3.0 * 0.044715 * (v * v)))
        dy = dgl_ref[...] * dgelu
        dy_ref[...] = dy.astype(BF16)
        sk_ref[...] = dy * d_ref[...]
        dd_ref[...] += _colsum8(dy * u_ref[...])
        _finish_colsums(i, n - 1, (dd_ref,))

    return pl.pallas_call(
        body, name=name, grid=(n,),
        in_specs=[_row_spec(t, W), _row_spec(t, W), _row_spec(t, W), _vec_spec(W)],
        out_specs=[_row_spec(t, W), _row_spec(t, W), _vec_spec(W, SUBLANES)],
        out_shape=[jax.ShapeDtypeStruct((L, W), BF16), jax.ShapeDtypeStruct((L, W), F32),
                   jax.ShapeDtypeStruct((SUBLANES, W), F32)],
        compiler_params=_params("arbitrary"),
    )(dgl, y, u, dskip)


def _merge_fwd(z, yb, gab, *, name):
    L, D = yb.shape
    t = _tile(L, ROW_TILE, SUBLANES)

    def body(z_ref, yb_ref, gab_ref, o_ref):
        ya = z_ref[:, :D] * _sigmoid(z_ref[:, D:])
        o_ref[...] = (_sigmoid(gab_ref[:, :D]) * ya + _sigmoid(gab_ref[:, D:]) * yb_ref[...]).astype(BF16)

    return pl.pallas_call(
        body, name=name, grid=(L // t,),
        in_specs=[_row_spec(t, 2 * D), _row_spec(t, D), _row_spec(t, 2 * D)],
        out_specs=_row_spec(t, D), out_shape=jax.ShapeDtypeStruct((L, D), BF16),
        compiler_params=_params("parallel"),
    )(z, yb, gab)


def _merge_bwd(dm, z, yb, gab, *, name, exchange=None):
    L, D = yb.shape
    t = _tile(L, ROW_TILE, SUBLANES)

    def body(dm_ref, z_ref, yb_ref, gab_ref, dz_ref, dyb_ref, dgab_ref):
        dmv = dm_ref[...]
        zv = z_ref[:, :D]
        sz = _sigmoid(z_ref[:, D:])
        sa = _sigmoid(gab_ref[:, :D])
        sb = _sigmoid(gab_ref[:, D:])
        ybv = yb_ref[...]
        dya = dmv * sa
        dz_ref[:, :D] = (dya * sz).astype(BF16)
        dz_ref[:, D:] = (dya * zv * (sz * (1.0 - sz))).astype(BF16)
        dyb_ref[...] = (dmv * sb).astype(BF16)
        dgab_ref[:, :D] = (dmv * (zv * sz) * (sa * (1.0 - sa))).astype(BF16)
        dgab_ref[:, D:] = (dmv * ybv * (sb * (1.0 - sb))).astype(BF16)

    return _call_with_exchange(
        body, exchange, name=name, grid=(L // t,),
        in_specs=[_row_spec(t, D), _row_spec(t, 2 * D), _row_spec(t, D), _row_spec(t, 2 * D)],
        out_specs=[_row_spec(t, 2 * D), _row_spec(t, D), _row_spec(t, 2 * D)],
        out_shape=[jax.ShapeDtypeStruct((L, 2 * D), BF16), jax.ShapeDtypeStruct((L, D), BF16),
                   jax.ShapeDtypeStruct((L, 2 * D), BF16)], operands=(dm, z, yb, gab))


SCAN_W = 1024
SCAN_T = 512


def _interleave(x):
    L, W = x.shape
    seg = SCAN_T // SUBLANES
    return x.reshape(L // SCAN_T, SUBLANES, seg, W).transpose(0, 2, 1, 3).reshape(L, W)


def _deinterleave(x):
    L, W = x.shape
    seg = SCAN_T // SUBLANES
    return x.reshape(L // SCAN_T, seg, SUBLANES, W).transpose(0, 2, 1, 3).reshape(L, W)


def _power_table(a, b, pr_tab, pi_tab, n):
    def fill(k, carry):
        pr, pi = carry
        pr_tab[k] = pr
        pi_tab[k] = pi
        return a * pr - b * pi, a * pi + b * pr

    lax.fori_loop(0, n, fill, (a, b))


def _rows_to_tile(rows):
    w = rows[0].shape[1]
    sub = lax.broadcasted_iota(jnp.int32, (SUBLANES, w), 0)
    tile = jnp.broadcast_to(rows[0], (SUBLANES, w))
    for j in range(1, SUBLANES):
        tile = jnp.where(sub == j, jnp.broadcast_to(rows[j], (SUBLANES, w)), tile)
    return tile


def _ssm_scan_fwd(bu_re, bu_im, lam_re, lam_im, *, name):
    L, S = bu_re.shape
    w, t = _tile(S, SCAN_W), SCAN_T
    seg = t // SUBLANES

    def body(br_ref, bi_ref, lr_ref, li_ref, sr_ref, si_ref, pr_tab, pi_tab, cr_ref, ci_ref):
        a = jnp.broadcast_to(lr_ref[...], (SUBLANES, w))
        b = jnp.broadcast_to(li_ref[...], (SUBLANES, w))

        @pl.when(pl.program_id(1) == 0)
        def _():
            cr_ref[...] = jnp.zeros_like(cr_ref)
            ci_ref[...] = jnp.zeros_like(ci_ref)
            _power_table(a, b, pr_tab, pi_tab, seg)

        def local_scan(i, carry):
            sr, si = carry
            rows = pl.ds(pl.multiple_of(i * SUBLANES, SUBLANES), SUBLANES)
            nr = a * sr - b * si + br_ref[rows, :]
            ni = a * si + b * sr + bi_ref[rows, :]
            sr_ref[rows, :] = nr
            si_ref[rows, :] = ni
            return nr, ni

        zero = jnp.zeros((SUBLANES, w), F32)
        fr, fi = lax.fori_loop(0, seg, local_scan, (zero, zero), unroll=2)
        lsr, lsi = pr_tab[seg - 1][0:1, :], pi_tab[seg - 1][0:1, :]
        cr, ci = cr_ref[...], ci_ref[...]
        rows_r, rows_i = [], []
        for j in range(SUBLANES):
            rows_r.append(cr)
            rows_i.append(ci)
            cr, ci = fr[j:j + 1, :] + (lsr * cr - lsi * ci), fi[j:j + 1, :] + (lsr * ci + lsi * cr)
        cr_ref[...] = cr
        ci_ref[...] = ci
        in_r, in_i = _rows_to_tile(rows_r), _rows_to_tile(rows_i)

        def add_entry(i, _):
            rows = pl.ds(pl.multiple_of(i * SUBLANES, SUBLANES), SUBLANES)
            pr, pi = pr_tab[i], pi_tab[i]
            sr_ref[rows, :] += pr * in_r - pi * in_i
            si_ref[rows, :] += pr * in_i + pi * in_r
            return 0

        lax.fori_loop(0, seg, add_entry, 0, unroll=2)

    blk = pl.BlockSpec((t, w), lambda j, i: (i, j))
    vec = pl.BlockSpec((1, w), lambda j, i: (0, j))
    sds = jax.ShapeDtypeStruct((L, S), F32)
    tab = pltpu.VMEM((seg, SUBLANES, w), F32)
    return pl.pallas_call(
        body, name=name, grid=(S // w, L // t),
        in_specs=[blk, blk, vec, vec], out_specs=[blk, blk], out_shape=[sds, sds],
        scratch_shapes=[tab, tab, pltpu.VMEM((1, w), F32), pltpu.VMEM((1, w), F32)],
        compiler_params=_params("parallel", "arbitrary"),
    )(bu_re, bu_im, lam_re, lam_im)


def _ssm_scan_bwd(d_re, d_im, s_re, s_im, lam_re, lam_im, *, name):
    L, S = d_re.shape
    w, t = _tile(S, SCAN_W), SCAN_T
    nt = L // t
    seg = t // SUBLANES

    def body(dr_ref, di_ref, sr_ref, si_ref, lr_ref, li_ref, gr_ref, gi_ref, ar_ref, ai_ref,
             pr_tab, pi_tab, cgr, cgi, acc_r, acc_i):
        step = pl.program_id(1)
        a = jnp.broadcast_to(lr_ref[...], (SUBLANES, w))
        b = jnp.broadcast_to(-li_ref[...], (SUBLANES, w))

        @pl.when(step == 0)
        def _():
            for r in (cgr, cgi, acc_r, acc_i):
                r[...] = jnp.zeros_like(r)
            _power_table(a, b, pr_tab, pi_tab, seg)

        def local_scan(ii, carry):
            gr, gi = carry
            rows = pl.ds(pl.multiple_of((seg - 1 - ii) * SUBLANES, SUBLANES), SUBLANES)
            ngr = a * gr - b * gi + dr_ref[rows, :]
            ngi = a * gi + b * gr + di_ref[rows, :]
            gr_ref[rows, :] = ngr
            gi_ref[rows, :] = ngi
            return ngr, ngi

        zero = jnp.zeros((SUBLANES, w), F32)
        fr, fi = lax.fori_loop(0, seg, local_scan, (zero, zero), unroll=2)
        lsr, lsi = pr_tab[seg - 1][0:1, :], pi_tab[seg - 1][0:1, :]
        cr, ci = cgr[...], cgi[...]
        rows_r, rows_i = [None] * SUBLANES, [None] * SUBLANES
        for j in reversed(range(SUBLANES)):
            rows_r[j], rows_i[j] = cr, ci
            cr, ci = fr[j:j + 1, :] + (lsr * cr - lsi * ci), fi[j:j + 1, :] + (lsr * ci + lsi * cr)
        cgr[...] = cr
        cgi[...] = ci
        in_r, in_i = _rows_to_tile(rows_r), _rows_to_tile(rows_i)

        def add_entry(ii, carry):
            nr, ni, xr, xi = carry
            rows = pl.ds(pl.multiple_of((seg - 1 - ii) * SUBLANES, SUBLANES), SUBLANES)
            sr = sr_ref[rows, :]
            si = si_ref[rows, :]
            xr = xr + (nr * sr + ni * si)
            xi = xi + (ni * sr - nr * si)
            pr, pi = pr_tab[ii], pi_tab[ii]
            gr = gr_ref[rows, :] + (pr * in_r - pi * in_i)
            gi = gi_ref[rows, :] + (pr * in_i + pi * in_r)
            gr_ref[rows, :] = gr
            gi_ref[rows, :] = gi
            return gr, gi, xr, xi

        _, _, xr, xi = lax.fori_loop(0, seg, add_entry, (in_r, in_i, acc_r[...], acc_i[...]), unroll=2)
        acc_r[...] = xr
        acc_i[...] = xi

        @pl.when(step == nt - 1)
        def _():
            ar_ref[...] = jnp.sum(xr, axis=0, keepdims=True)
            ai_ref[...] = jnp.sum(xi, axis=0, keepdims=True)

    blk = pl.BlockSpec((t, w), lambda j, i: (nt - 1 - i, j))
    vec = pl.BlockSpec((1, w), lambda j, i: (0, j))
    sds = jax.ShapeDtypeStruct((L, S), F32)
    vsds = jax.ShapeDtypeStruct((1, S), F32)
    tab = pltpu.VMEM((seg, SUBLANES, w), F32)
    tile = pltpu.VMEM((SUBLANES, w), F32)
    return pl.pallas_call(
        body, name=name, grid=(S // w, nt),
        in_specs=[blk, blk, blk, blk, vec, vec], out_specs=[blk, blk, vec, vec],
        out_shape=[sds, sds, vsds, vsds],
        scratch_shapes=[tab, tab, pltpu.VMEM((1, w), F32), pltpu.VMEM((1, w), F32), tile, tile],
        compiler_params=_params("parallel", "arbitrary"),
    )(d_re, d_im, s_re, s_im, lam_re, lam_im)


def _ssm_discretize(a_re, a_im, log_dt, b_re, b_im, c_re, c_im, d_skip):
    G, P, N = SSM_GROUPS, SSM_STATE, SSM_GROUP
    a = jnp.minimum(a_re, -1e-4)
    dt = jnp.exp(log_dt)[:, None]
    mag = jnp.exp(a * dt)
    lr = mag * jnp.cos(a_im * dt)
    li = mag * jnp.sin(a_im * dt)
    den = a * a + a_im * a_im
    cr = ((lr - 1.0) * a + li * a_im) / den
    ci = (li * a - (lr - 1.0) * a_im) / den
    bbr = cr[..., None] * b_re - ci[..., None] * b_im
    bbi = cr[..., None] * b_im + ci[..., None] * b_re
    gl = G // SSM_BLOCKS
    eye = jnp.eye(gl, dtype=F32)[None, :, None, :, None]

    def in_map(bb):
        t = bb.transpose(0, 2, 1).reshape(SSM_BLOCKS, gl, N, P)
        return (eye * t[:, :, :, None, :]).reshape(SSM_BLOCKS, gl * N, gl * P)

    def out_map(c):
        t = c.transpose(0, 2, 1).reshape(SSM_BLOCKS, gl, P, N)
        return (eye * t[:, :, :, None, :]).reshape(SSM_BLOCKS, gl * P, gl * N)

    return (lr.reshape(1, G * P), li.reshape(1, G * P), in_map(bbr), in_map(bbi),
            out_map(c_re), out_map(-c_im), d_skip.reshape(1, SSM_WIDTH))


ATT_T = 512
CUM_T = 256


def _split3(x):
    hi = x.astype(BF16)
    r1 = x - hi.astype(F32)
    mid = r1.astype(BF16)
    lo = (r1 - mid.astype(F32)).astype(BF16)
    return hi, mid, lo


def _tri_dot(tri, x):
    hi, mid, lo = _split3(x)
    dot = lambda p: jnp.dot(tri, p, preferred_element_type=F32)
    return dot(hi) + dot(mid) + dot(lo)


def _log_sigmoid(x):
    return jnp.minimum(x, 0.0) - jnp.log(1.0 + jnp.exp(-jnp.abs(x)))


def _fox_cum(f, fb, *, name):
    L, W = f.shape
    t = _tile(L, CUM_T, SUBLANES)

    def body(f_ref, b_ref, o_ref, carry):
        @pl.when(pl.program_id(0) == 0)
        def _():
            carry[...] = jnp.zeros_like(carry)

        row = lax.broadcasted_iota(jnp.int32, (t, t), 0)
        col = lax.broadcasted_iota(jnp.int32, (t, t), 1)
        tri = jnp.where(col <= row, 1.0, 0.0).astype(BF16)
        c = _tri_dot(tri, _log_sigmoid(f_ref[...] + b_ref[...])) + carry[...]
        o_ref[...] = c
        carry[...] = c[t - 1:t, :]

    return pl.pallas_call(
        body, name=name, grid=(L // t,),
        in_specs=[_row_spec(t, W), _vec_spec(W)], out_specs=_row_spec(t, W),
        out_shape=jax.ShapeDtypeStruct((L, W), F32),
        scratch_shapes=[pltpu.VMEM((1, W), F32)], compiler_params=_params("arbitrary"),
    )(f, fb)


def _fox_cum_bwd(dcum, f, fb, *, name):
    L, W = f.shape
    t = _tile(L, CUM_T, SUBLANES)
    n = L // t

    def body(d_ref, f_ref, b_ref, o_ref, db_ref, carry):
        i = pl.program_id(0)

        @pl.when(i == 0)
        def _():
            carry[...] = jnp.zeros_like(carry)
            db_ref[...] = jnp.zeros_like(db_ref)

        row = lax.broadcasted_iota(jnp.int32, (t, t), 0)
        col = lax.broadcasted_iota(jnp.int32, (t, t), 1)
        tri = jnp.where(col >= row, 1.0, 0.0).astype(BF16)
        dlog = _tri_dot(tri, d_ref[...]) + carry[...]
        carry[...] = dlog[0:1, :]
        df = dlog * _sigmoid(-(f_ref[...] + b_ref[...]))
        o_ref[...] = df.astype(BF16)
        db_ref[...] += _colsum8(df)
        _finish_colsums(i, n - 1, (db_ref,))

    rev = pl.BlockSpec((t, W), lambda i: (n - 1 - i, 0))
    return pl.pallas_call(
        body, name=name, grid=(n,),
        in_specs=[rev, rev, _vec_spec(W)], out_specs=[rev, _vec_spec(W, SUBLANES)],
        out_shape=[jax.ShapeDtypeStruct((L, W), BF16), jax.ShapeDtypeStruct((SUBLANES, W), F32)],
        scratch_shapes=[pltpu.VMEM((1, W), F32)], compiler_params=_params("arbitrary"),
    )(dcum, f, fb)


def _head_col(blk, h):
    lane = lax.broadcasted_iota(jnp.int32, blk.shape, 1)
    return jnp.sum(jnp.where(lane == h * HEAD_DIM, blk, 0.0), axis=1, keepdims=True)


def _lo_mask(rows):
    return lax.broadcasted_iota(jnp.int32, (rows, LANES), 1) < HEAD_DIM


def _causal(t):
    row = lax.broadcasted_iota(jnp.int32, (t, t), 0)
    col = lax.broadcasted_iota(jnp.int32, (t, t), 1)
    return col <= row


def _call_with_exchange(body, exchange, *, name, grid, in_specs, out_specs, out_shape, operands, sem=None):
    sem = sem or ("parallel",) + ("arbitrary",) * (len(grid) - 1)
    if exchange is None:
        return pl.pallas_call(body, name=name, grid=grid, in_specs=in_specs, out_specs=out_specs,
                              out_shape=out_shape, compiler_params=_params(*sem))(*operands)
    n_in, n_out = len(in_specs), len(out_specs)
    ei, eo = len(exchange.operands), len(exchange.out_shapes)

    def wrapped(*refs):
        ins, ex_in = refs[:n_in], refs[n_in:n_in + ei]
        outs, ex_out = refs[n_in + ei:n_in + ei + n_out], refs[n_in + ei + n_out:n_in + ei + n_out + eo]
        sems = refs[n_in + ei + n_out + eo:]
        ids = [pl.program_id(d) for d in range(len(grid))]
        first = functools.reduce(jnp.logical_and, [i == 0 for i in ids])
        last = functools.reduce(jnp.logical_and, [i == g - 1 for i, g in zip(ids, grid)])

        @pl.when(first)
        def _():
            exchange.start(ex_in, ex_out, sems)

        body(*ins, *outs)

        @pl.when(last)
        def _():
            exchange.finish(ex_in, ex_out, sems)

    return pl.pallas_call(
        wrapped, name=name, grid=grid, in_specs=list(in_specs) + [HBM_SPEC] * ei,
        out_specs=list(out_specs) + [HBM_SPEC] * eo, out_shape=list(out_shape) + list(exchange.out_shapes),
        scratch_shapes=exchange.sems, compiler_params=_params(*(("arbitrary",) * len(grid))),
    )(*operands, *exchange.operands)


def _fox_fwd(qkv, cum_cols, cum_rows, *, name, exchange=None):
    L = qkv.shape[0]
    t = _tile(L, ATT_T)
    nq = L // t
    npair = ATTN_HEADS // 2

    def body(q_ref, k_ref, v_ref, cc_ref, cr_ref, o_ref, o32_ref, lse_ref):
        iq = pl.program_id(1)
        lo = _lo_mask(t)
        qv = q_ref[...] * 0.125
        zq = jnp.zeros_like(qv)
        qh = (jnp.where(lo, qv, zq), jnp.where(lo, zq, qv))
        ccv = cc_ref[...]
        cq = (_head_col(ccv, 0), _head_col(ccv, 1))

        def step(ik, carry, masked):
            start = pl.multiple_of(ik * t, t)
            kb = k_ref[pl.ds(start, t), :]
            vb = v_ref[pl.ds(start, t), :]
            out = []
            for h in range(2):
                m, l, acc = carry[h]
                s = lax.dot_general(qh[h], kb, NT, preferred_element_type=F32)
                s = s + (cq[h] - cr_ref[h:h + 1, pl.ds(start, t)])
                if masked:
                    s = jnp.where(_causal(t), s, -jnp.inf)
                m_new = jnp.maximum(m, jnp.max(s, axis=1, keepdims=True))
                alpha = jnp.exp(m - m_new)
                p = jnp.exp(s - m_new)
                l = alpha * l + jnp.sum(p, axis=1, keepdims=True)
                acc = alpha * acc + jnp.dot(p.astype(BF16), vb, preferred_element_type=F32)
                out.append((m_new, l, acc))
            return tuple(out)

        init1 = (jnp.full((t, 1), -jnp.inf, F32), jnp.zeros((t, 1), F32), jnp.zeros((t, LANES), F32))
        carry = lax.fori_loop(0, iq, lambda ik, c: step(ik, c, False), (init1, init1))
        (m0, l0, a0), (m1, l1, a1) = step(iq, carry, True)
        out = jnp.where(lo, a0 / l0, a1 / l1)
        o_ref[...] = out.astype(BF16)
        o32_ref[...] = out
        lse_ref[...] = jnp.where(lo, m0 + jnp.log(l0), m1 + jnp.log(l1))

    blk = lambda off: pl.BlockSpec((t, LANES), lambda hp, iq: (iq, off + hp))
    whole = lambda off: pl.BlockSpec((L, LANES), lambda hp, iq: (0, off + hp))
    return _call_with_exchange(
        body, exchange, name=name, grid=(npair, nq),
        in_specs=[blk(0), whole(npair), whole(2 * npair), blk(0),
                  pl.BlockSpec((None, SUBLANES, L), lambda hp, iq: (hp, 0, 0))],
        out_specs=[blk(0), blk(0), blk(0)],
        out_shape=[jax.ShapeDtypeStruct((L, ATTN_WIDTH), BF16), jax.ShapeDtypeStruct((L, ATTN_WIDTH), F32),
                   jax.ShapeDtypeStruct((L, ATTN_WIDTH), F32)],
        operands=(qkv, qkv, qkv, cum_cols, cum_rows))


STAT_LSE, STAT_CUM, STAT_DELTA = 0, 2, 4


def _lane_col(blk, idx):
    lane = lax.broadcasted_iota(jnp.int32, blk.shape, 1)
    return jnp.sum(jnp.where(lane == idx, blk, 0.0), axis=1, keepdims=True)


def _fox_rowstats(do, o, lse, cum_cols, *, name):
    L = do.shape[0]
    t = _tile(L, ATT_T)

    def body(do_ref, o_ref, lse_ref, cc_ref, st_ref):
        lo = _lo_mask(t)
        dd = do_ref[...].astype(F32) * o_ref[...]
        lsev, ccv = lse_ref[...], cc_ref[...]
        cols = (_head_col(lsev, 0), _head_col(lsev, 1), _head_col(ccv, 0), _head_col(ccv, 1),
                jnp.sum(jnp.where(lo, dd, 0.0), axis=1, keepdims=True),
                jnp.sum(jnp.where(lo, 0.0, dd), axis=1, keepdims=True))
        lane = lax.broadcasted_iota(jnp.int32, (t, LANES), 1)
        out = jnp.zeros((t, LANES), F32)
        for i, col in enumerate(cols):
            out = jnp.where(lane == i, col, out)
        st_ref[...] = out

    blk = pl.BlockSpec((t, LANES), lambda hp, i: (i, hp))
    return pl.pallas_call(
        body, name=name, grid=(ATTN_HEADS // 2, L // t),
        in_specs=[blk, blk, blk, blk], out_specs=blk,
        out_shape=jax.ShapeDtypeStruct((L, ATTN_WIDTH), F32),
        compiler_params=_params("parallel", "parallel"),
    )(do, o, lse, cum_cols)


def _fox_bwd(qkv, do, stats, cum_rows, *, name, exchange=None):
    L = qkv.shape[0]
    t = _tile(L, ATT_T)
    nq = L // t
    npair = ATTN_HEADS // 2

    def body(q_ref, do_ref, st_ref, qt_ref, dot_ref, k_ref, v_ref, cr_ref, dk_ref, dv_ref, dc_ref, dq_ref, drow_ref):
        ik = pl.program_id(1)

        @pl.when(ik == 0)
        def _():
            dq_ref[...] = jnp.zeros_like(dq_ref)
            drow_ref[...] = jnp.zeros_like(drow_ref)

        lo = _lo_mask(t)
        lo_rows = lax.broadcasted_iota(jnp.int32, (LANES, t), 0) < HEAD_DIM
        lane = lax.broadcasted_iota(jnp.int32, (t, LANES), 1)
        kb = k_ref[...]
        vb = v_ref[...]
        zk = jnp.zeros_like(kb)
        kh = (jnp.where(lo, kb, zk), jnp.where(lo, zk, kb))
        vh = (jnp.where(lo, vb, zk), jnp.where(lo, zk, vb))
        ck = (cr_ref[0:1, :], cr_ref[1:2, :])

        def step(iq, carry, masked):
            dk, dv, dc0, dc1 = carry
            start = pl.multiple_of(iq * t, t)
            qb = q_ref[pl.ds(start, t), :] * 0.125
            dob = do_ref[pl.ds(start, t), :]
            stb = st_ref[pl.ds(start, t), :]
            qtb = qt_ref[:, pl.ds(start, t)] * 0.125
            dotb = dot_ref[:, pl.ds(start, t)]
            dks, dvs, dcs, dqs, rss = [], [], [], [], []
            for h in range(2):
                s = lax.dot_general(qb, kh[h], NT, preferred_element_type=F32)
                s = s + (_lane_col(stb, STAT_CUM + h) - ck[h])
                if masked:
                    s = jnp.where(_causal(t), s, -jnp.inf)
                p = jnp.exp(s - _lane_col(stb, STAT_LSE + h))
                dp = lax.dot_general(dob, vh[h], NT, preferred_element_type=F32)
                ds = p * (dp - _lane_col(stb, STAT_DELTA + h))
                dsb = ds.astype(BF16)
                dvs.append(jnp.dot(dotb, p.astype(BF16), preferred_element_type=F32))
                dks.append(jnp.dot(qtb, dsb, preferred_element_type=F32))
                dqs.append(jnp.dot(dsb, kh[h], preferred_element_type=F32))
                dcs.append(jnp.sum(ds, axis=0, keepdims=True))
                rss.append(jnp.sum(ds, axis=1, keepdims=True))
            dq_ref[pl.ds(start, t), :] += 0.125 * (dqs[0] + dqs[1])
            drow_ref[pl.ds(start, t), :] += jnp.where(lane == 0, rss[0], jnp.where(lane == 1, rss[1], 0.0))
            return (dk + jnp.where(lo_rows, dks[0], dks[1]), dv + jnp.where(lo_rows, dvs[0], dvs[1]),
                    dc0 - dcs[0], dc1 - dcs[1])

        zero = jnp.zeros((LANES, t), F32)
        zrow = jnp.zeros((1, t), F32)
        carry = step(ik, (zero, zero, zrow, zrow), True)
        dk, dv, dc0, dc1 = lax.fori_loop(ik + 1, nq, lambda iq, c: step(iq, c, False), carry)
        dk_ref[...] = dk.T.astype(BF16)
        dv_ref[...] = dv.T.astype(BF16)
        dc_ref[...] = jnp.zeros_like(dc_ref)
        dc_ref[0:1, :] = dc0
        dc_ref[1:2, :] = dc1

    whole = lambda off: pl.BlockSpec((L, LANES), lambda hp, ik: (0, off + hp))
    blk = lambda off: pl.BlockSpec((t, LANES), lambda hp, ik: (ik, off + hp))
    rows = pl.BlockSpec((None, SUBLANES, t), lambda hp, ik: (hp, 0, ik))
    whole_t = pl.BlockSpec((LANES, L), lambda hp, ik: (hp, 0))
    return _call_with_exchange(
        body, exchange, name=name, grid=(npair, nq),
        in_specs=[whole(0), whole(0), whole(0), whole_t, whole_t, blk(npair), blk(2 * npair), rows],
        out_specs=[blk(0), blk(0), rows, whole(0), whole(0)],
        out_shape=[jax.ShapeDtypeStruct((L, ATTN_WIDTH), BF16), jax.ShapeDtypeStruct((L, ATTN_WIDTH), BF16),
                   jax.ShapeDtypeStruct((npair, SUBLANES, L), F32),
                   jax.ShapeDtypeStruct((L, ATTN_WIDTH), F32), jax.ShapeDtypeStruct((L, ATTN_WIDTH), F32)],
        operands=(qkv, do, stats, qkv[:, :ATTN_WIDTH].T, do.T, qkv, qkv, cum_rows))


def _mod_partial(c_all, mod_w, mod_b_cols, *, name):
    depth, K, cols = mod_w.shape
    tn = _tile(cols, 768)

    def body(c_ref, w_ref, b_ref, o_ref):
        cv = c_ref[...]
        sc = (cv * _sigmoid(cv)).astype(BF16)
        o_ref[...] = jnp.dot(sc, w_ref[...].astype(BF16), preferred_element_type=F32) + b_ref[...]

    return pl.pallas_call(
        body, name=name, grid=(depth, cols // tn),
        in_specs=[pl.BlockSpec((N_DEV, K), lambda l, j: (0, 0)),
                  pl.BlockSpec((None, K, tn), lambda l, j: (l, 0, j)),
                  pl.BlockSpec((None, 1, tn), lambda l, j: (l, 0, j))],
        out_specs=pl.BlockSpec((None, N_DEV, tn), lambda l, j: (l, 0, j)),
        out_shape=jax.ShapeDtypeStruct((depth, N_DEV, cols), F32),
        compiler_params=_params("parallel", "parallel"),
    )(c_all, mod_w, mod_b_cols)


def _mod_wgrad(c_all_t, dmod, *, name):
    depth, nb, cols = dmod.shape
    K = c_all_t.shape[0]
    tn = _tile(cols, 768)
    tk = _tile(K, 256, SUBLANES)

    def body(c_ref, d_ref, o_ref):
        cv = c_ref[...]
        sc = cv * _sigmoid(cv)
        dv = d_ref[...]
        acc = sc[:, 0:1] * dv[0:1, :]
        for b in range(1, nb):
            acc = acc + sc[:, b:b + 1] * dv[b:b + 1, :]
        o_ref[...] = acc

    return pl.pallas_call(
        body, name=name, grid=(depth, K // tk, cols // tn),
        in_specs=[pl.BlockSpec((tk, nb), lambda l, i, j: (i, 0)),
                  pl.BlockSpec((None, nb, tn), lambda l, i, j: (l, 0, j))],
        out_specs=pl.BlockSpec((None, tk, tn), lambda l, i, j: (l, i, j)),
        out_shape=jax.ShapeDtypeStruct((depth, K, cols), F32),
        compiler_params=_params("parallel", "parallel", "parallel"),
    )(c_all_t, dmod)


def _adamw(w, g, m, v, *, name):
    shape = w.shape
    cols = shape[-1]
    rows = int(np.prod(shape[:-1]))
    t = _tile(rows, 256, SUBLANES) if rows % SUBLANES == 0 else rows
    r2 = lambda a: a.reshape(rows, cols)

    def body(w_ref, g_ref, m_ref, v_ref, d_ref, nm_ref, nv_ref):
        gv = g_ref[...]
        nm = ADAM_B1 * m_ref[...] + (1.0 - ADAM_B1) * gv
        nv = ADAM_B2 * v_ref[...] + (1.0 - ADAM_B2) * (gv * gv)
        m_hat = nm / (1.0 - ADAM_B1 ** ADAM_STEP)
        v_hat = nv / (1.0 - ADAM_B2 ** ADAM_STEP)
        d_ref[...] = -ADAM_LR * (m_hat / (jnp.sqrt(v_hat) + ADAM_EPS) + ADAM_WD * w_ref[...])
        nm_ref[...] = nm
        nv_ref[...] = nv

    spec = pl.BlockSpec((t, cols), lambda i: (i, 0))
    sds = jax.ShapeDtypeStruct((rows, cols), F32)
    d, nm, nv = pl.pallas_call(
        body, name=name, grid=(rows // t,),
        in_specs=[spec] * 4, out_specs=[spec] * 3, out_shape=[sds] * 3,
        compiler_params=_params("parallel"),
    )(r2(w), r2(g), r2(m), r2(v))
    return d.reshape(shape), nm.reshape(shape), nv.reshape(shape)


def _my_place():
    return lax.axis_index("x"), lax.axis_index("y"), lax.axis_index("c")


def _other_chips(x, y):
    return [(1 - x, y), (x, 1 - y), (1 - x, 1 - y)]


def _all_gather8(v, *, name, with_sum=False):
    m, n = v.shape

    def body(x_ref, out_ref, *rest):
        if with_sum:
            sum_ref, send_sems, recv_sems, local_sem = rest
        else:
            send_sems, recv_sems, local_sem = rest
        x, y, c = _my_place()
        me, sibling = (x, y, c), (x, y, 1 - c)
        chips = _other_chips(x, y)

        def rows(px, py, pc):
            return out_ref.at[pl.ds((4 * px + 2 * py + pc) * m, m), :]

        def copy(k, block, to, src=None):
            return pltpu.make_async_remote_copy(
                src_ref=rows(*block) if src is None else src, dst_ref=rows(*block),
                send_sem=send_sems.at[k], recv_sem=recv_sems.at[k], device_id=to, device_id_type=MESH)

        mine = pltpu.make_async_copy(x_ref, rows(*me), local_sem)
        mine.start()
        first = [copy(0, me, sibling, src=x_ref)]
        first += [copy(1 + j, me, (*chip, c), src=x_ref) for j, chip in enumerate(chips)]
        for cp in first:
            cp.start()
        passed = [copy(4 + j, (*chip, c), sibling) for j, chip in enumerate(chips)]
        for j, chip in enumerate(chips):
            copy(1 + j, (*chip, c), me).wait_recv()
            passed[j].start()
        copy(0, sibling, me).wait_recv()
        for j, chip in enumerate(chips):
            copy(4 + j, (*chip, 1 - c), me).wait_recv()
        for cp in first + passed:
            cp.wait_send()
        mine.wait()
        if with_sum:
            acc = out_ref[pl.ds(0, m), :]
            for d in range(1, N_DEV):
                acc = acc + out_ref[pl.ds(d * m, m), :]
            sum_ref[...] = acc

    vm = pl.BlockSpec(memory_space=pltpu.VMEM)
    out_shape = [jax.ShapeDtypeStruct((N_DEV * m, n), F32)]
    if with_sum:
        out_shape.append(jax.ShapeDtypeStruct((m, n), F32))
    res = pl.pallas_call(
        body, name=name, out_shape=out_shape, in_specs=[vm], out_specs=[vm] * len(out_shape),
        scratch_shapes=[pltpu.SemaphoreType.DMA((7,)), pltpu.SemaphoreType.DMA((7,)), pltpu.SemaphoreType.DMA],
        compiler_params=pltpu.CompilerParams(vmem_limit_bytes=VMEM_LIMIT),
    )(v)
    return res if with_sum else res[0]


class _Cut(NamedTuple):
    shape: tuple
    slab: int
    half: int


IN_WIDTH = SSM_WIDTH + 3 * ATTN_WIDTH + ATTN_HEADS + 2 * D_MODEL
CUTS = dict(
    ffn_w_in=_Cut((1, D_MODEL, 2 * D_FF), 2, 1),
    ffn_w_out=_Cut((1, D_FF, D_MODEL), 1, 2),
    mix_w_in=_Cut((N_CHIPS, D_MODEL, IN_WIDTH // N_CHIPS), 0, 1),
    glu_w=_Cut((1, SSM_WIDTH, 2 * D_MODEL), 2, 1),
    attn_w_out=_Cut((1, ATTN_WIDTH, D_MODEL), 2, 1),
    mix_w_out=_Cut((1, D_MODEL, D_MODEL), 1, 2),
)
LAYER_MATS = (("ffn_w_in", 0), ("ffn_w_in", 1), ("ffn_w_out", 0), ("ffn_w_out", 1), ("mix_w_in", None),
              ("glu_w", None), ("attn_w_out", None), ("mix_w_out", None))
FIRST_MATS = (0, 2)
MIXER_MATS = (4, 5, 6, 7)
LATE_MATS = (1, 3)


def _part_shape(cut, slab=False, half=False):
    s = list(cut.shape)
    if slab:
        s[cut.slab] //= N_CHIPS
    if half:
        s[cut.half] //= 2
    return tuple(s)


def _window(ref, cut, slab=None, half=None):
    idx = [slice(None)] * len(cut.shape)
    for axis, parts, which in ((cut.slab, N_CHIPS, slab), (cut.half, 2, half)):
        if which is not None:
            width = cut.shape[axis] // parts
            idx[axis] = pl.ds(pl.multiple_of(which * width, width), width)
    return ref.at[tuple(idx)]


HBM_SPEC = pl.BlockSpec(memory_space=pltpu.HBM)


class _Exchange(NamedTuple):
    operands: list
    out_shapes: list
    sems: list
    start: object
    finish: object


def _run_exchange(ex, *, name):
    ni, no = len(ex.operands), len(ex.out_shapes)

    def body(*refs):
        parts = (refs[:ni], refs[ni:ni + no], refs[ni + no:])
        ex.start(*parts)
        ex.finish(*parts)

    return pl.pallas_call(
        body, name=name, out_shape=ex.out_shapes, in_specs=[HBM_SPEC] * ni, out_specs=[HBM_SPEC] * no,
        scratch_shapes=ex.sems,
    )(*ex.operands)


def _gather_exchange(shards, cuts):
    n = len(shards)

    def setup(s_refs, f_refs, sems):
        send_sems, recv_sems = sems
        x, y, c = _my_place()
        me, sibling, mine = (x, y, c), (x, y, 1 - c), 2 * x + y
        chips = _other_chips(x, y)

        def copy(i, k, src, dst, to):
            return pltpu.make_async_remote_copy(
                src_ref=src, dst_ref=dst, send_sem=send_sems.at[7 * i + k], recv_sem=recv_sems.at[7 * i + k],
                device_id=to, device_id_type=MESH)

        def landed(i, j, half):
            return _window(f_refs[i], cuts[i], slab=2 * chips[j][0] + chips[j][1], half=half)

        def sends():
            own = [copy(i, 6, s_refs[i], _window(f_refs[i], cuts[i], slab=mine), sibling) for i in range(n)]
            return own + [copy(i, j, _window(s_refs[i], cuts[i], half=c),
                               _window(f_refs[i], cuts[i], slab=mine, half=c), (*chips[j], c))
                          for i in range(n) for j in range(3)]

        return c, me, sibling, copy, landed, sends

    def start(s_refs, f_refs, sems):
        for cp in setup(s_refs, f_refs, sems)[-1]():
            cp.start()

    def finish(s_refs, f_refs, sems):
        c, me, sibling, copy, landed, sends = setup(s_refs, f_refs, sems)
        passed = []
        for i in range(n):
            for j in range(3):
                copy(i, j, landed(i, j, c), landed(i, j, c), me).wait_recv()
                passed.append(copy(i, 3 + j, landed(i, j, c), landed(i, j, c), sibling))
                passed[-1].start()
        for i in range(n):
            for j in range(3):
                copy(i, 3 + j, landed(i, j, 1 - c), landed(i, j, 1 - c), me).wait_recv()
        for i in range(n):
            mine_i = _window(f_refs[i], cuts[i], slab=2 * me[0] + me[1])
            copy(i, 6, mine_i, mine_i, me).wait_recv()
        for cp in sends() + passed:
            cp.wait_send()

    return _Exchange(
        list(shards), [jax.ShapeDtypeStruct(cut.shape, s.dtype) for s, cut in zip(shards, cuts)],
        [pltpu.SemaphoreType.DMA((7 * n,)), pltpu.SemaphoreType.DMA((7 * n,))], start, finish)


def _swap_exchange(mats, cuts):
    n = len(mats)

    def copies(m_refs, r_refs, sems):
        send_sems, recv_sems = sems
        x, y, c = _my_place()
        return [pltpu.make_async_remote_copy(
            src_ref=_window(m_refs[i], cuts[i], half=1 - c), dst_ref=r_refs[i], send_sem=send_sems.at[i],
            recv_sem=recv_sems.at[i], device_id=(x, y, 1 - c), device_id_type=MESH) for i in range(n)]

    def start(m_refs, r_refs, sems):
        for cp in copies(m_refs, r_refs, sems):
            cp.start()

    def finish(m_refs, r_refs, sems):
        for cp in copies(m_refs, r_refs, sems):
            cp.wait()

    return _Exchange(
        list(mats), [jax.ShapeDtypeStruct(_part_shape(cut, half=True), m.dtype) for m, cut in zip(mats, cuts)],
        [pltpu.SemaphoreType.DMA((n,)), pltpu.SemaphoreType.DMA((n,))], start, finish)


def _partials_exchange(sums, cuts):
    n = len(sums)

    def copies(s_refs, p_refs, sems):
        send_sems, recv_sems = sems
        x, y, c = _my_place()
        return [pltpu.make_async_remote_copy(
            src_ref=_window(s_refs[i], cuts[i], slab=2 * chip[0] + chip[1]), dst_ref=p_refs[i].at[j],
            send_sem=send_sems.at[3 * i + j], recv_sem=recv_sems.at[3 * i + j],
            device_id=(*chip, c), device_id_type=MESH)
            for i in range(n) for j, chip in enumerate(_other_chips(x, y))]

    def start(s_refs, p_refs, sems):
        for cp in copies(s_refs, p_refs, sems):
            cp.start()

    def finish(s_refs, p_refs, sems):
        for cp in copies(s_refs, p_refs, sems):
            cp.wait()

    return _Exchange(
        list(sums), [jax.ShapeDtypeStruct((3,) + _part_shape(cut, slab=True, half=True), s.dtype)
                     for s, cut in zip(sums, cuts)],
        [pltpu.SemaphoreType.DMA((3 * n,)), pltpu.SemaphoreType.DMA((3 * n,))], start, finish)


def _share_all(dests, cuts, places, *, name):
    names = list(dests)
    nn, n = len(names), len(places)

    def body(*refs):
        o_refs = dict(zip(names, refs[nn:2 * nn]))
        send_sems, recv_sems = refs[2 * nn:]
        x, y, c = _my_place()

        def win(i, half):
            slab_cut = _Cut(_part_shape(cuts[i], slab=True), cuts[i].slab, cuts[i].half)
            return _window(o_refs[places[i][0]].at[places[i][1]], slab_cut, half=half)

        def copy(i, half):
            return pltpu.make_async_remote_copy(
                src_ref=win(i, half), dst_ref=win(i, half), send_sem=send_sems.at[i], recv_sem=recv_sems.at[i],
                device_id=(x, y, 1 - c), device_id_type=MESH)

        for i in range(n):
            copy(i, c).start()
        for i in range(n):
            copy(i, c).wait_send()
            copy(i, 1 - c).wait_recv()

    return pl.pallas_call(
        body, name=name, out_shape=[jax.ShapeDtypeStruct(dests[k].shape, F32) for k in names],
        in_specs=[HBM_SPEC] * nn, out_specs=[HBM_SPEC] * nn,
        input_output_aliases={i: i for i in range(nn)},
        scratch_shapes=[pltpu.SemaphoreType.DMA((n,)), pltpu.SemaphoreType.DMA((n,))],
    )(*[dests[k] for k in names])


def _cut_blocks(shape):
    _, R, C = shape
    tr = _tile(R, 256, 16)
    tc = _tile(C, 2048) if C % LANES == 0 else C
    return (None, tr, tc), (shape[0], R // tr, C // tc)


def _offset_map(axis, blocks):
    def index_map(b, i, j, which):
        idx = [b, i, j]
        idx[axis] = which[0] * blocks[axis] + idx[axis]
        return tuple(idx)
    return index_map


def _add_half(mat, other, cut, c_idx, *, name):
    shape = _part_shape(cut, half=True)
    block, grid = _cut_blocks(shape)

    def body(c_ref, m_ref, o_ref, f_ref, b_ref):
        s = m_ref[...] + o_ref[...]
        f_ref[...] = s
        b_ref[...] = s.astype(BF16)

    plain = pl.BlockSpec(block, lambda b, i, j, which: (b, i, j))
    grid_spec = pltpu.PrefetchScalarGridSpec(
        num_scalar_prefetch=1, grid=grid,
        in_specs=[pl.BlockSpec(block, _offset_map(cut.half, grid)), plain], out_specs=[plain, plain])
    return pl.pallas_call(
        body, name=name, grid_spec=grid_spec,
        out_shape=[jax.ShapeDtypeStruct(shape, F32), jax.ShapeDtypeStruct(shape, BF16)],
        compiler_params=_params("parallel", "parallel", "parallel"),
    )(c_idx, mat, other)


def _sum_slab(own, parts, cut, dest, place, chip_idx, c_idx, *, name):
    shape = _part_shape(cut, slab=True, half=True)
    block, grid = _cut_blocks(shape)
    assert shape[0] == 1

    def body(k_ref, c_ref, o_ref, p_ref, dest_ref, out_ref):
        acc = o_ref[...]
        for j in range(3):
            acc = acc + p_ref[j].astype(F32)
        out_ref[...] = acc

    def own_map(b, i, j, chip, core):
        idx = [b, i, j]
        idx[cut.slab] = chip[0] * grid[cut.slab] + idx[cut.slab]
        return tuple(idx)

    def dest_map(b, i, j, chip, core):
        idx = [b, i, j]
        idx[cut.half] = core[0] * grid[cut.half] + idx[cut.half]
        return tuple(place) + tuple(idx)

    grid_spec = pltpu.PrefetchScalarGridSpec(
        num_scalar_prefetch=2, grid=grid,
        in_specs=[pl.BlockSpec(block, own_map),
                  pl.BlockSpec((3,) + block[1:], lambda b, i, j, chip, core: (0, i, j)),
                  pl.BlockSpec(memory_space=pl.ANY)],
        out_specs=pl.BlockSpec((None,) * len(place) + block, dest_map))
    return pl.pallas_call(
        body, name=name, grid_spec=grid_spec, out_shape=jax.ShapeDtypeStruct(dest.shape, F32),
        input_output_aliases={4: 0},
        compiler_params=_params("parallel", "parallel", "parallel"),
    )(chip_idx, c_idx, own, parts.reshape((3,) + shape[1:]), dest)


def _pad_rows(flat, cols=8 * LANES, align=SUBLANES):
    n = flat.shape[0]
    rows = -(-n // (cols * align)) * align
    return jnp.pad(flat, (0, rows * cols - n)).reshape(rows, cols)


def _row(v):
    return v.reshape(1, -1)


def _ffn_fwd(x, mod, g_pre, g_post, w_in, w_out, tag, exchange=None):
    sh, sc, gate = _row(mod[0]), _row(mod[1]), _row(mod[2])
    h = _prenorm(x, _row(g_pre), sc, sh, name=f"prenorm_{tag}")
    gt, up, act, *exchanged = _mm_swiglu(h, w_in, name=f"swiglu_{tag}", exchange=exchange)
    y, x_out = _mm_postnorm(act, w_out, x, _row(g_post), gate, FFN_RES, name=f"ffn_out_{tag}")
    return x_out, (x, h, gt, up, act, y), exchanged


def _ffn_bwd(dxo, saved, mod, g_pre, g_post, w_in, w_out, tag, swap=None, after_swap=None):
    x, h, gt, up, act, y = saved
    sc, gate = _row(mod[1]), _row(mod[2])
    dy, dgate, dgpost, *swapped = _postnorm_bwd(dxo, y, _row(g_post), gate, FFN_RES, name=f"postnorm_bwd_{tag}",
                                                exchange=swap)
    exchange = after_swap(swapped) if after_swap is not None else None
    dgt, dup, *exchanged = _mm_swiglu_bwd(dy, w_out, gt, up, name=f"swiglu_bwd_{tag}", exchange=exchange)
    dw_out = _mm(act, dy, ta=True, name=f"dw_out_{tag}", tm=1408, tn=1024, tk=1024)
    dh = _mm(dgt, w_in, tb=True, name=f"dh_gate_{tag}", tk=D_FF)
    dh = _mm(dup, w_in, tb=True, b_k0=D_FF, bias=dh, name=f"dh_up_{tag}", tk=D_FF)
    dw_in = _mm(h, dgt, ta=True, out_n=2 * D_FF, name=f"dw_gate_{tag}", tm=1024, tn=1408, tk=1024)
    dw_in = _mm(h, dup, ta=True, out_n=2 * D_FF, out_j0=D_FF, into=dw_in, name=f"dw_up_{tag}",
                tm=1024, tn=1408, tk=1024)
    dx, dsh, dsc, dgpre = _prenorm_bwd(x, dh, dxo, _row(g_pre), sc, name=f"prenorm_bwd_{tag}")
    dmod = jnp.stack([dsh[0], dsc[0], dgate[0]])
    return dx, dmod, dgpre[0], dgpost[0], dw_in, dw_out, exchanged


def _split_mix_w_in(w):
    u0, q0, f0, g0 = 0, SSM_WIDTH, SSM_WIDTH + 3 * ATTN_WIDTH, SSM_WIDTH + 3 * ATTN_WIDTH + ATTN_HEADS
    w_f = jnp.pad(w[:, f0:g0], ((0, 0), (0, LANES - ATTN_HEADS)))
    return w[:, u0:q0], w[:, q0:f0], w_f, w[:, g0:]


def _mixer_fwd(x, mod, g_pre, g_post, w, ssm, forget_b, tag, exchange=None):
    L = x.shape[0]
    sh, sc, gate = _row(mod[0]), _row(mod[1]), _row(mod[2])
    lam_re, lam_im, bin_re, bin_im, cout_re, cout_im, dskip = ssm
    h = _prenorm(x, _row(g_pre), sc, sh, name=f"prenorm_{tag}")
    u = _mm(h, w["u"], name=f"proj_u_{tag}")
    qkv = _mm(h, w["qkv"], out_dtype=BF16, name=f"proj_qkv_{tag}")
    f = _mm(h, w["f"], name=f"proj_f_{tag}")
    gab = _mm(h, w["gab"], name=f"proj_gab_{tag}")
    u_i = _interleave(u)
    bu_re, bu_im = _mm_bd([u_i], [bin_re, bin_im], name=f"ssm_bu_{tag}")
    s_re, s_im = _ssm_scan_fwd(bu_re, bu_im, lam_re, lam_im, name=f"ssm_scan_{tag}")
    y_ssm = _deinterleave(_mm_bd([s_re, s_im], [cout_re, cout_im], bias=u_i, bscale=dskip, name=f"ssm_y_{tag}"))
    gl = _gelu_fwd(y_ssm, name=f"gelu_{tag}")
    z = _mm(gl, w["glu"], name=f"glu_{tag}")
    fb = jnp.pad(forget_b, (0, LANES - ATTN_HEADS)).reshape(1, LANES)
    cum = _fox_cum(f, fb, name=f"fox_cum_{tag}")
    cum8 = cum[:, :ATTN_HEADS]
    cum_cols = jnp.repeat(cum8, HEAD_DIM, axis=1)
    cum_rows = jnp.pad(cum8.T.reshape(ATTN_HEADS // 2, 2, L), ((0, 0), (0, SUBLANES - 2), (0, 0)))
    attn, attn32, lse, *exchanged = _fox_fwd(qkv, cum_cols, cum_rows, name=f"fox_fwd_{tag}", exchange=exchange)
    yb = _mm(attn, w["attn_out"], name=f"attn_out_{tag}")
    merged = _merge_fwd(z, yb, gab, name=f"merge_{tag}")
    y, x_out = _mm_postnorm(merged, w["out"], x, _row(g_post), gate, 1.0, name=f"mix_out_{tag}")
    saved = (x, h, u, u_i, qkv, f, gab, s_re, s_im, y_ssm, gl, z, fb, cum_cols, cum_rows, attn, attn32, lse, yb,
             merged, y)
    return x_out, saved, exchanged


def _mixer_bwd(dxo, saved, mod, g_pre, g_post, w, ssm, tag, swap=None, after_swap=None):
    (x, h, u, u_i, qkv, f, gab, s_re, s_im, y_ssm, gl, z, fb, cum_cols, cum_rows, attn, attn32, lse, yb,
     merged, y) = saved
    L = x.shape[0]
    sc, gate = _row(mod[1]), _row(mod[2])
    lam_re, lam_im, bin_re, bin_im, cout_re, cout_im, dskip = ssm
    dy, dgate, dgpost = _postnorm_bwd(dxo, y, _row(g_post), gate, 1.0, name=f"postnorm_bwd_{tag}")
    dmerged = _mm(dy, w["out"], tb=True, name=f"dmerged_{tag}")
    dw_out = _mm(merged, dy, ta=True, name=f"dw_mix_out_{tag}", tm=1024, tn=1024)
    dz, dyb, dgab, *swapped = _merge_bwd(dmerged, z, yb, gab, name=f"merge_bwd_{tag}", exchange=swap)
    exchange = after_swap(swapped) if after_swap is not None else None
    dgl = _mm(dz, w["glu"], tb=True, name=f"dgl_{tag}", tk=2048)
    dw_glu = _mm(gl, dz, ta=True, name=f"dw_glu_{tag}", tm=512, tn=2048)
    dys, dsk, dd = _gelu_bwd(dgl, y_ssm, u, dskip, name=f"gelu_bwd_{tag}")
    dys, dsk = _interleave(dys), _interleave(dsk)
    d_re, d_im = _mm_bd([dys], [cout_re, cout_im], tb=True, name=f"ssm_ds_{tag}")
    dcout_re = _mm_bd_t(s_re, dys, SSM_BLOCKS, name=f"ssm_dc_re_{tag}")
    dcout_im = _mm_bd_t(s_im, dys, SSM_BLOCKS, name=f"ssm_dc_im_{tag}")
    g_re, g_im, dlam_re, dlam_im = _ssm_scan_bwd(d_re, d_im, s_re, s_im, lam_re, lam_im, name=f"ssm_scan_bwd_{tag}")
    du = _mm_bd([g_re, g_im], [bin_re, bin_im], tb=True, bias=dsk, out_dtype=BF16, name=f"ssm_du_{tag}")
    du = _deinterleave(du)
    dbin_re = _mm_bd_t(u_i, g_re, SSM_BLOCKS, name=f"ssm_db_re_{tag}")
    dbin_im = _mm_bd_t(u_i, g_im, SSM_BLOCKS, name=f"ssm_db_im_{tag}")
    dssm = (dlam_re, dlam_im, dbin_re, dbin_im, dcout_re, dcout_im, _row(dd[0]))
    dattn = _mm(dyb, w["attn_out"], tb=True, out_dtype=BF16, name=f"dattn_{tag}")
    dw_attn = _mm(attn, dyb, ta=True, name=f"dw_attn_out_{tag}", tm=512, tn=1024)
    stats = _fox_rowstats(dattn, attn32, lse, cum_cols, name=f"fox_rowstats_{tag}")
    dk, dv, dcum_rows, dq, drow, *exchanged = _fox_bwd(qkv, dattn, stats, cum_rows, name=f"fox_bwd_{tag}",
                                                       exchange=exchange)
    drow8 = drow.reshape(L, ATTN_HEADS // 2, LANES)[:, :, :2].reshape(L, ATTN_HEADS)
    dcum = drow8 + dcum_rows[:, :2, :].reshape(ATTN_HEADS, L).T
    dcum = jnp.pad(dcum, ((0, 0), (0, LANES - ATTN_HEADS)))
    df, dfb = _fox_cum_bwd(dcum, f, fb, name=f"fox_cum_bwd_{tag}")
    dqkv = jnp.concatenate([dq.astype(BF16), dk, dv], axis=1)
    dh = _mm(dqkv, w["qkv"], tb=True, name=f"dh_qkv_{tag}", tk=1536)
    dh = _mm(du, w["u"], tb=True, bias=dh, name=f"dh_u_{tag}")
    dh = _mm(dgab, w["gab"], tb=True, bias=dh, name=f"dh_gab_{tag}", tk=2048)
    dh = _mm(df, w["f"], tb=True, bias=dh, name=f"dh_f_{tag}")
    dw_u = _mm(h, du, ta=True, name=f"dw_u_{tag}", tm=1024, tn=512)
    dw_qkv = _mm(h, dqkv, ta=True, name=f"dw_qkv_{tag}", tm=1024, tn=1536)
    dw_f = _mm(h, df, ta=True, name=f"dw_f_{tag}", tm=1024)
    dw_gab = _mm(h, dgab, ta=True, name=f"dw_gab_{tag}", tm=1024, tn=1024)
    dw_in = jnp.concatenate([dw_u, dw_qkv, dw_f[:, :ATTN_HEADS], dw_gab], axis=1)
    dx, dsh, dsc, dgpre = _prenorm_bwd(x, dh, dxo, _row(g_pre), sc, name=f"prenorm_bwd_{tag}")
    dmod = jnp.stack([dsh[0], dsc[0], dgate[0]])
    grads = dict(mix_w_in=dw_in, glu_w=dw_glu, attn_w_out=dw_attn, mix_w_out=dw_out,
                 forget_b=dfb[0, :ATTN_HEADS])
    return dx, dmod, dgpre[0], dgpost[0], grads, dssm, exchanged


SSM_NAMES = ("ssm_a_re", "ssm_a_im", "ssm_log_dt", "ssm_b_re", "ssm_b_im", "ssm_c_re", "ssm_c_im", "ssm_d")
SMALL_NAMES = ("forget_b",) + SSM_NAMES
WEIGHT_NAMES = ("mod_w", "mod_b", "norm_pre", "norm_post", "ffn_w_in", "ffn_w_out", "mix_w_in", "forget_b") \
    + SSM_NAMES + ("glu_w", "attn_w_out", "mix_w_out")


def _layer_shards(w, l):
    return [(w[n][l] if j is None else w[n][l, j]).astype(BF16).reshape(_part_shape(CUTS[n], slab=True))
            for n, j in LAYER_MATS]


def _train_step(x, c, target, w, m, v):
    xi, yi, ci = _my_place()
    chip = 2 * xi + yi
    dev = 4 * xi + 2 * yi + ci
    mod_cols = N_SUB * 3 * D_MODEL // N_CHIPS
    norm_cols = D_MODEL // N_CHIPS

    cuts = [CUTS[n] for n, _ in LAYER_MATS]
    shards = [_layer_shards(w, l) for l in range(DEPTH)]
    full = [[None] * len(LAYER_MATS) for _ in range(DEPTH)]

    def gather(keys):
        return _gather_exchange([shards[l][i] for l, i in keys], [cuts[i] for _, i in keys])

    def store(keys, mats):
        for (l, i), a in zip(keys, mats):
            full[l][i] = a

    first = lambda l: [(l, i) for i in FIRST_MATS]
    mixer = lambda l: [(l, i) for i in MIXER_MATS]
    late = lambda l: [(l, i) for i in LATE_MATS]
    store(first(0), _run_exchange(gather(first(0)), name="gather_weights_first"))
    c_all = _all_gather8(jnp.pad(c, ((0, SUBLANES - 1), (0, 0))), name="gather_c")[::SUBLANES]
    mod_b_cols = lax.dynamic_slice_in_dim(w["mod_b"], chip * mod_cols, mod_cols, axis=1)[:, None, :]
    mod_part = _mod_partial(c_all, w["mod_w"], mod_b_cols, name="mod_partial")
    small_fwd = jnp.concatenate([mod_part.reshape(-1), w["norm_pre"].reshape(-1), w["norm_post"].reshape(-1)])
    n_mod, n_norm = mod_part.size, w["norm_pre"].size
    sf_all = _all_gather8(_pad_rows(small_fwd), name="gather_mod").reshape(N_DEV, -1)
    sf_chips = sf_all[::2]
    mod_all = jnp.concatenate(
        [sf_chips[k, :n_mod].reshape(DEPTH, N_DEV, mod_cols) for k in range(N_CHIPS)], axis=2)
    mod_mine = lax.dynamic_index_in_dim(mod_all, dev, axis=1, keepdims=False).reshape(DEPTH, N_SUB, 3, D_MODEL)
    norm_pre = jnp.concatenate(
        [sf_chips[k, n_mod:n_mod + n_norm].reshape(DEPTH, N_SUB, norm_cols) for k in range(N_CHIPS)], axis=2)
    norm_post = jnp.concatenate(
        [sf_chips[k, n_mod + n_norm:n_mod + 2 * n_norm].reshape(DEPTH, N_SUB, norm_cols) for k in range(N_CHIPS)],
        axis=2)

    saved, layer_w, ssm_prep, ssm_vjp = [], [], [], []
    h = x
    for l in range(DEPTH):
        lw = dict(ffn=[(full[l][0][0], full[l][2][0]), None])
        prep, vjp = jax.vjp(_ssm_discretize, *[w[n][l] for n in SSM_NAMES])
        layer_w.append(lw)
        ssm_prep.append(prep)
        ssm_vjp.append(vjp)
        h, s0, arrived = _ffn_fwd(h, mod_mine[l, 0], norm_pre[l, 0], norm_post[l, 0], *lw["ffn"][0], tag=f"l{l}a",
                                  exchange=gather(mixer(l)))
        store(mixer(l), arrived)
        mix_in, glu, attn_out, mix_out = full[l][4:]
        w_u, w_qkv, w_f, w_gab = _split_mix_w_in(mix_in.transpose(1, 0, 2).reshape(D_MODEL, IN_WIDTH))
        lw["mix"] = dict(u=w_u, qkv=w_qkv, f=w_f, gab=w_gab, glu=glu[0], attn_out=attn_out[0], out=mix_out[0])
        coming = late(l) + (first(l + 1) if l + 1 < DEPTH else [])
        h, s1, arrived = _mixer_fwd(h, mod_mine[l, 1], norm_pre[l, 1], norm_post[l, 1], lw["mix"], prep,
                                    w["forget_b"][l], tag=f"l{l}m", exchange=gather(coming))
        store(coming, arrived)
        lw["ffn"][1] = (full[l][1][0], full[l][3][0])
        h, s2, _ = _ffn_fwd(h, mod_mine[l, 2], norm_pre[l, 2], norm_post[l, 2], *lw["ffn"][1], tag=f"l{l}b")
        saved.append((s0, s1, s2))
    dh, loss8 = _loss_head(h, target, name="loss_head")

    c_idx = ci.reshape(1).astype(jnp.int32)
    chip_idx = chip.reshape(1).astype(jnp.int32)
    places = []
    dests = {n: lax.empty(((DEPTH,) if j is None else (DEPTH, 2)) + _part_shape(CUTS[n], slab=True), F32)
             for n, j in LAYER_MATS}
    g_small = {n: [None] * DEPTH for n in SMALL_NAMES}
    dmod, dnpre, dnpost = [], [], []
    pending = []

    def swap_of(l, idx, mats):
        cs = [cuts[i] for i in idx]
        mats = [a.reshape(cut.shape) for a, cut in zip(mats, cs)]

        def added(from_sibling):
            for k, i in enumerate(idx):
                f, b = _add_half(mats[k], from_sibling[k], cs[k], c_idx, name=f"grad_add_l{l}_{i}")
                pending.append((l, i, f, b))

        return _swap_exchange(mats, cs), added

    def partials():
        return _partials_exchange([p[3] for p in pending], [cuts[p[1]] for p in pending])

    def then_partials(added):
        def after_swap(from_sibling):
            added(from_sibling)
            return partials()
        return after_swap

    def end_reduce(parts):
        for (l, i, f, _), part in zip(pending, parts):
            n, j = LAYER_MATS[i]
            place = (l,) if j is None else (l, j)
            dests[n] = _sum_slab(f, part, cuts[i], dests[n], place, chip_idx, c_idx, name=f"grad_sum_l{l}_{i}")
            places.append((n, place))
        pending.clear()

    swap, added = None, None
    for l in reversed(range(DEPTH)):
        lw = layer_w[l]
        dh, dm2, dp2, dq2, dwin2, dwout2, _ = _ffn_bwd(
            dh, saved[l][2], mod_mine[l, 2], norm_pre[l, 2], norm_post[l, 2], *lw["ffn"][1], tag=f"l{l}b",
            swap=swap, after_swap=added)
        swap, added = swap_of(l, LATE_MATS, [dwin2, dwout2])
        dh, dm1, dp1, dq1, gmix, dssm, parts = _mixer_bwd(
            dh, saved[l][1], mod_mine[l, 1], norm_pre[l, 1], norm_post[l, 1], lw["mix"], ssm_prep[l],
            tag=f"l{l}m", swap=swap, after_swap=then_partials(added))
        end_reduce(parts)
        dmix_in = gmix["mix_w_in"].reshape(D_MODEL, N_CHIPS, IN_WIDTH // N_CHIPS).transpose(1, 0, 2)
        swap, added = swap_of(l, MIXER_MATS, [dmix_in, gmix["glu_w"], gmix["attn_w_out"], gmix["mix_w_out"]])
        dh, dm0, dp0, dq0, dwin0, dwout0, parts = _ffn_bwd(
            dh, saved[l][0], mod_mine[l, 0], norm_pre[l, 0], norm_post[l, 0], *lw["ffn"][0], tag=f"l{l}a",
            swap=swap, after_swap=then_partials(added))
        end_reduce(parts)
        swap, added = swap_of(l, FIRST_MATS, [dwin0, dwout0])
        dmod.insert(0, jnp.stack([dm0, dm1, dm2]))
        dnpre.insert(0, jnp.stack([dp0, dp1, dp2]))
        dnpost.insert(0, jnp.stack([dq0, dq1, dq2]))
        g_small["forget_b"][l] = gmix["forget_b"]
        for n, g in zip(SSM_NAMES, ssm_vjp[l](dssm)):
            g_small[n][l] = g
    added(_run_exchange(swap, name="grad_swap_last"))
    end_reduce(_run_exchange(partials(), name="grad_partials_last"))
    grad_x = dh
    g_small = {n: jnp.stack(g) for n, g in g_small.items()}

    small = [loss8[0, :1], jnp.stack(dmod).reshape(-1), jnp.stack(dnpre).reshape(-1), jnp.stack(dnpost).reshape(-1)]
    small += [g_small[n].reshape(-1) for n in SMALL_NAMES]
    sizes = [int(s.size) for s in small]
    offs = np.concatenate([[0], np.cumsum(sizes)])
    sb_all, sb_sum = _all_gather8(_pad_rows(jnp.concatenate(small)), name="gather_small_grads", with_sum=True)
    sb_sum = sb_sum.reshape(-1)
    take = lambda i: sb_sum[int(offs[i]):int(offs[i + 1])]
    loss = take(0)[0]
    grads = {"mod_b": take(1).reshape(DEPTH, N_SUB * 3 * D_MODEL)}
    dnorm_pre_full = take(2).reshape(DEPTH, N_SUB, D_MODEL)
    dnorm_post_full = take(3).reshape(DEPTH, N_SUB, D_MODEL)
    grads["norm_pre"] = lax.dynamic_slice_in_dim(dnorm_pre_full, chip * norm_cols, norm_cols, axis=2)
    grads["norm_post"] = lax.dynamic_slice_in_dim(dnorm_post_full, chip * norm_cols, norm_cols, axis=2)
    for i, n in enumerate(SMALL_NAMES):
        grads[n] = take(4 + i).reshape(w[n].shape)
    dmod_all = sb_all.reshape(N_DEV, -1)[:, int(offs[1]):int(offs[2])].reshape(N_DEV, DEPTH, N_SUB * 3 * D_MODEL)
    dmod_cols = lax.dynamic_slice_in_dim(dmod_all, chip * mod_cols, mod_cols, axis=2).transpose(1, 0, 2)
    grads["mod_w"] = _mod_wgrad(c_all.T, dmod_cols, name="mod_wgrad")

    shared = _share_all(dests, [CUTS[n] for n, _ in places], places, name="grad_share")
    for n, g in zip(dests, shared):
        grads[n] = g.reshape(w[n].shape)

    delta, new_m, new_v = {}, {}, {}
    for n in WEIGHT_NAMES:
        delta[n], new_m[n], new_v[n] = _adamw(w[n], grads[n], m[n], v[n], name=f"adamw_{n}")
    outs = [loss, grad_x[None]]
    for group in (grads, delta, new_m, new_v):
        outs += [group[n] for n in WEIGHT_NAMES]
    return tuple(outs)


def kernel(x, c, mod_w, mod_b, norm_pre, norm_post, ffn_w_in, ffn_w_out, mix_w_in, forget_b, ssm_a_re, ssm_a_im, ssm_log_dt, ssm_b_re, ssm_b_im, ssm_c_re, ssm_c_im, ssm_d, glu_w, attn_w_out, mix_w_out, loss_target, m_mod_w, m_mod_b, m_norm_pre, m_norm_post, m_ffn_w_in, m_ffn_w_out, m_mix_w_in, m_forget_b, m_ssm_a_re, m_ssm_a_im, m_ssm_log_dt, m_ssm_b_re, m_ssm_b_im, m_ssm_c_re, m_ssm_c_im, m_ssm_d, m_glu_w, m_attn_w_out, m_mix_w_out, v_mod_w, v_mod_b, v_norm_pre, v_norm_post, v_ffn_w_in, v_ffn_w_out, v_mix_w_in, v_forget_b, v_ssm_a_re, v_ssm_a_im, v_ssm_log_dt, v_ssm_b_re, v_ssm_b_im, v_ssm_c_re, v_ssm_c_im, v_ssm_d, v_glu_w, v_attn_w_out, v_mix_w_out):
    w = dict(mod_w=mod_w, mod_b=mod_b, norm_pre=norm_pre, norm_post=norm_post, ffn_w_in=ffn_w_in,
             ffn_w_out=ffn_w_out, mix_w_in=mix_w_in, forget_b=forget_b, ssm_a_re=ssm_a_re, ssm_a_im=ssm_a_im,
             ssm_log_dt=ssm_log_dt, ssm_b_re=ssm_b_re, ssm_b_im=ssm_b_im, ssm_c_re=ssm_c_re, ssm_c_im=ssm_c_im,
             ssm_d=ssm_d, glu_w=glu_w, attn_w_out=attn_w_out, mix_w_out=mix_w_out)
    m = dict(mod_w=m_mod_w, mod_b=m_mod_b, norm_pre=m_norm_pre, norm_post=m_norm_post, ffn_w_in=m_ffn_w_in,
             ffn_w_out=m_ffn_w_out, mix_w_in=m_mix_w_in, forget_b=m_forget_b, ssm_a_re=m_ssm_a_re,
             ssm_a_im=m_ssm_a_im, ssm_log_dt=m_ssm_log_dt, ssm_b_re=m_ssm_b_re, ssm_b_im=m_ssm_b_im,
             ssm_c_re=m_ssm_c_re, ssm_c_im=m_ssm_c_im, ssm_d=m_ssm_d, glu_w=m_glu_w, attn_w_out=m_attn_w_out,
             mix_w_out=m_mix_w_out)
    v = dict(mod_w=v_mod_w, mod_b=v_mod_b, norm_pre=v_norm_pre, norm_post=v_norm_post, ffn_w_in=v_ffn_w_in,
             ffn_w_out=v_ffn_w_out, mix_w_in=v_mix_w_in, forget_b=v_forget_b, ssm_a_re=v_ssm_a_re,
             ssm_a_im=v_ssm_a_im, ssm_log_dt=v_ssm_log_dt, ssm_b_re=v_ssm_b_re, ssm_b_im=v_ssm_b_im,
             ssm_c_re=v_ssm_c_re, ssm_c_im=v_ssm_c_im, ssm_d=v_ssm_d, glu_w=v_glu_w, attn_w_out=v_attn_w_out,
             mix_w_out=v_mix_w_out)
    return _train_step(x[0], c, loss_target[0], w, m, v)
```

```python
import functools
import math
from typing import NamedTuple

import jax
import jax.numpy as jnp
import numpy as np
from jax import lax
from jax.experimental import pallas as pl
from jax.experimental.pallas import tpu as pltpu

F32 = jnp.float32
BF16 = jnp.bfloat16

D_MODEL = 1024
DEPTH = 2
SSM_WIDTH = 512
SSM_GROUP = 16
SSM_GROUPS = 32
SSM_STATE = 64
SSM_FLAT = SSM_GROUPS * SSM_STATE
SSM_BLOCKS = 4
ATTN_HEADS = 8
HEAD_DIM = 64
ATTN_WIDTH = 512
D_FF = 2816
FFN_RES = 0.5
N_SUB = 3
RMS_EPS = 1e-6
N_CHIPS = 4
N_DEV = 8

ADAM_LR = 0.001
ADAM_B1 = 0.9
ADAM_B2 = 0.999
ADAM_EPS = 1e-08
ADAM_WD = 0.01
ADAM_STEP = 10

LANES = 128
SUBLANES = 8
VMEM_LIMIT = 52 * 1024 * 1024
MESH = pl.DeviceIdType.MESH

NN = (((1,), (0,)), ((), ()))
NT = (((1,), (1,)), ((), ()))
TN = (((0,), (0,)), ((), ()))


def _tile(dim, target, align=LANES):
    best = None
    t = align
    while t <= min(dim, target):
        if dim % t == 0:
            best = t
        t += align
    return dim if best is None else best


def _params(*sem):
    return pltpu.CompilerParams(dimension_semantics=sem, vmem_limit_bytes=VMEM_LIMIT)


def _mm(a, b, *, name, ta=False, tb=False, out_dtype=F32, bias=None, bscale=None, b_k0=0,
        out_n=None, out_j0=0, into=None, tm=512, tn=1024, tk=1024):
    M, K = (a.shape[1], a.shape[0]) if ta else a.shape
    N = b.shape[0] if tb else b.shape[1]
    assert b_k0 + K <= (b.shape[1] if tb else b.shape[0]), (a.shape, b.shape, ta, tb)
    tm, tn, tk = _tile(M, tm), _tile(N, tn), _tile(K, tk)
    nk = K // tk
    assert b_k0 % tk == 0
    kb0 = b_k0 // tk
    dn = (((0 if ta else 1,), (1 if tb else 0,)), ((), ()))
    has_bias, has_scale = bias is not None, bscale is not None

    def body(*refs):
        a_ref, b_ref = refs[0], refs[1]
        pos = 2
        bias_ref = scale_ref = None
        if has_bias:
            bias_ref = refs[pos]
            pos += 1
        if has_scale:
            scale_ref = refs[pos]
            pos += 1
        if into is not None:
            pos += 1
        o_ref = refs[pos]
        acc_ref = refs[pos + 1] if nk > 1 else None

        def finish(r):
            if has_bias:
                extra = bias_ref[...].astype(F32)
                if has_scale:
                    extra = extra * scale_ref[...]
                r = r + extra
            o_ref[...] = r.astype(out_dtype)

        part = lax.dot_general(a_ref[...].astype(BF16), b_ref[...].astype(BF16), dn,
                               preferred_element_type=F32)
        if nk == 1:
            finish(part)
        else:
            k = pl.program_id(2)

            @pl.when(k == 0)
            def _():
                acc_ref[...] = part

            @pl.when(k > 0)
            def _():
                acc_ref[...] += part

            @pl.when(k == nk - 1)
            def _():
                finish(acc_ref[...])

    a_spec = pl.BlockSpec((tk, tm), lambda j, i, k: (k, i)) if ta else pl.BlockSpec((tm, tk), lambda j, i, k: (i, k))
    b_spec = (pl.BlockSpec((tn, tk), lambda j, i, k: (j, kb0 + k)) if tb
              else pl.BlockSpec((tk, tn), lambda j, i, k: (kb0 + k, j)))
    in_specs = [a_spec, b_spec]
    args = [a, b]
    if has_bias:
        in_specs.append(pl.BlockSpec((tm, tn), lambda j, i, k: (i, j)))
        args.append(bias)
    if has_scale:
        in_specs.append(pl.BlockSpec((1, tn), lambda j, i, k: (0, j)))
        args.append(bscale)
    aliases = {}
    if into is not None:
        in_specs.append(pl.BlockSpec(memory_space=pl.ANY))
        args.append(into)
        aliases = {len(args) - 1: 0}
    out_n = N if out_n is None else out_n
    assert out_j0 % tn == 0
    jb0 = out_j0 // tn
    return pl.pallas_call(
        body, name=name,
        grid=(N // tn, M // tm, nk),
        in_specs=in_specs,
        out_specs=pl.BlockSpec((tm, tn), lambda j, i, k: (i, jb0 + j)),
        out_shape=jax.ShapeDtypeStruct((M, out_n), out_dtype),
        scratch_shapes=[pltpu.VMEM((tm, tn), F32)] if nk > 1 else [],
        input_output_aliases=aliases,
        compiler_params=_params("parallel", "parallel", "arbitrary"),
    )(*args)


def _mm_bd(a_list, b_list, *, name, tb=False, out_dtype=F32, bias=None, bscale=None, tm=1024):
    G = b_list[0].shape[0]
    Kb, Nb = (b_list[0].shape[2], b_list[0].shape[1]) if tb else b_list[0].shape[1:]
    M = a_list[0].shape[0]
    tm = _tile(M, tm)
    na, nb = len(a_list), len(b_list)
    n_out = nb if na == 1 else 1
    dn = NT if tb else NN
    has_bias, has_scale = bias is not None, bscale is not None

    def body(*refs):
        a_refs, b_refs = refs[:na], refs[na:na + nb]
        pos = na + nb
        bias_ref = scale_ref = None
        if has_bias:
            bias_ref = refs[pos]
            pos += 1
        if has_scale:
            scale_ref = refs[pos]
            pos += 1
        o_refs = refs[pos:]
        prods = [lax.dot_general(a_refs[min(i, na - 1)][...].astype(BF16), b_refs[i][...].astype(BF16), dn,
                                 preferred_element_type=F32) for i in range(nb)]
        outs = prods if n_out == nb else [functools.reduce(jnp.add, prods)]
        for o_ref, r in zip(o_refs, outs):
            if has_bias:
                extra = bias_ref[...].astype(F32)
                r = r + (extra * scale_ref[...] if has_scale else extra)
            o_ref[...] = r.astype(out_dtype)

    a_spec = pl.BlockSpec((tm, Kb), lambda g, i: (i, g))
    b_spec = pl.BlockSpec((None,) + b_list[0].shape[1:], lambda g, i: (g, 0, 0))
    o_spec = pl.BlockSpec((tm, Nb), lambda g, i: (i, g))
    in_specs = [a_spec] * na + [b_spec] * nb
    args = list(a_list) + list(b_list)
    if has_bias:
        in_specs.append(o_spec)
        args.append(bias)
    if has_scale:
        in_specs.append(pl.BlockSpec((1, Nb), lambda g, i: (0, g)))
        args.append(bscale)
    sds = jax.ShapeDtypeStruct((M, G * Nb), out_dtype)
    res = pl.pallas_call(
        body, name=name, grid=(G, M // tm), in_specs=in_specs, out_specs=[o_spec] * n_out,
        out_shape=[sds] * n_out, compiler_params=_params("parallel", "parallel"),
    )(*args)
    return res[0] if n_out == 1 else res


def _mm_bd_t(a, b, G, *, name, tk=1024):
    K, Mb, Nb = a.shape[0], a.shape[1] // G, b.shape[1] // G
    tk = _tile(K, tk)
    nk = K // tk

    def body(a_ref, b_ref, o_ref, acc_ref):
        k = pl.program_id(1)
        part = lax.dot_general(a_ref[...].astype(BF16), b_ref[...].astype(BF16), TN, preferred_element_type=F32)

        @pl.when(k == 0)
        def _():
            acc_ref[...] = part

        @pl.when(k > 0)
        def _():
            acc_ref[...] += part

        @pl.when(k == nk - 1)
        def _():
            o_ref[...] = acc_ref[...]

    return pl.pallas_call(
        body, name=name, grid=(G, nk),
        in_specs=[pl.BlockSpec((tk, Mb), lambda g, k: (k, g)), pl.BlockSpec((tk, Nb), lambda g, k: (k, g))],
        out_specs=pl.BlockSpec((None, Mb, Nb), lambda g, k: (g, 0, 0)),
        out_shape=jax.ShapeDtypeStruct((G, Mb, Nb), F32),
        scratch_shapes=[pltpu.VMEM((Mb, Nb), F32)], compiler_params=_params("parallel", "arbitrary"),
    )(a, b)


def _sigmoid(x):
    return 0.5 * jnp.tanh(0.5 * x) + 0.5


def _mm_swiglu(h, w_in, *, name, tm=512, tn=1408, exchange=None):
    M, K = h.shape
    N = w_in.shape[1] // 2
    tm, tn = _tile(M, tm), _tile(N, tn)
    nj = N // tn

    def body(h_ref, wg_ref, wu_ref, g_ref, u_ref, a_ref):
        hv = h_ref[...]
        g = jnp.dot(hv, wg_ref[...], preferred_element_type=F32)
        u = jnp.dot(hv, wu_ref[...], preferred_element_type=F32)
        g_ref[...] = g.astype(BF16)
        u_ref[...] = u.astype(BF16)
        a_ref[...] = (g * _sigmoid(g) * u).astype(BF16)

    o_spec = pl.BlockSpec((tm, tn), lambda j, i: (i, j))
    sds = jax.ShapeDtypeStruct((M, N), BF16)
    return _call_with_exchange(
        body, exchange, name=name, grid=(nj, M // tm),
        in_specs=[pl.BlockSpec((tm, K), lambda j, i: (i, 0)), pl.BlockSpec((K, tn), lambda j, i: (0, j)),
                  pl.BlockSpec((K, tn), lambda j, i: (0, nj + j))],
        out_specs=[o_spec, o_spec, o_spec], out_shape=[sds, sds, sds], operands=(h, w_in, w_in))


def _mm_swiglu_bwd(dy, w_out, gate, up, *, name, tm=512, tn=1408, exchange=None):
    M, K = dy.shape
    N = w_out.shape[0]
    tm, tn = _tile(M, tm), _tile(N, tn)

    def body(dy_ref, w_ref, g_ref, u_ref, dg_ref, du_ref):
        dact = lax.dot_general(dy_ref[...], w_ref[...], NT, preferred_element_type=F32)
        g = g_ref[...].astype(F32)
        u = u_ref[...].astype(F32)
        sig = _sigmoid(g)
        dg_ref[...] = (dact * u * (sig * (1.0 + g * (1.0 - sig)))).astype(BF16)
        du_ref[...] = (dact * (g * sig)).astype(BF16)

    t_spec = pl.BlockSpec((tm, tn), lambda j, i: (i, j))
    sds = jax.ShapeDtypeStruct((M, N), BF16)
    return _call_with_exchange(
        body, exchange, name=name, grid=(N // tn, M // tm),
        in_specs=[pl.BlockSpec((tm, K), lambda j, i: (i, 0)), pl.BlockSpec((tn, K), lambda j, i: (j, 0)),
                  t_spec, t_spec],
        out_specs=[t_spec, t_spec], out_shape=[sds, sds], operands=(dy, w_out, gate, up))


ROW_TILE = 256


def _colsum8(v):
    return jnp.sum(v.reshape(v.shape[0] // SUBLANES, SUBLANES, v.shape[1]), axis=0)


def _finish_colsums(step, last, refs):
    @pl.when(step == last)
    def _():
        for r in refs:
            r[...] = jnp.broadcast_to(jnp.sum(r[...], axis=0, keepdims=True), r.shape)


def _row_spec(t, d):
    return pl.BlockSpec((t, d), lambda i: (i, 0))


def _vec_spec(d, rows=1):
    return pl.BlockSpec((rows, d), lambda i: (0, 0))


def _prenorm(x, g, sc, sh, *, name):
    L, D = x.shape
    t = _tile(L, ROW_TILE, SUBLANES)

    def body(x_ref, g_ref, sc_ref, sh_ref, h_ref):
        xv = x_ref[...]
        r = lax.rsqrt(jnp.mean(xv * xv, axis=-1, keepdims=True) + RMS_EPS)
        h_ref[...] = (((xv * r) * g_ref[...]) * (1.0 + sc_ref[...]) + sh_ref[...]).astype(BF16)

    return pl.pallas_call(
        body, name=name, grid=(L // t,),
        in_specs=[_row_spec(t, D), _vec_spec(D), _vec_spec(D), _vec_spec(D)],
        out_specs=_row_spec(t, D), out_shape=jax.ShapeDtypeStruct((L, D), BF16),
        compiler_params=_params("parallel"),
    )(x, g, sc, sh)


def _mm_postnorm(a, b, x, g, gate, res_w, *, name, tm=512):
    M, K = a.shape
    D = b.shape[1]
    tm = _tile(M, tm, SUBLANES)

    def body(a_ref, b_ref, x_ref, g_ref, gate_ref, y_ref, o_ref):
        yv = jnp.dot(a_ref[...], b_ref[...], preferred_element_type=F32)
        y_ref[...] = yv
        r = lax.rsqrt(jnp.mean(yv * yv, axis=-1, keepdims=True) + RMS_EPS)
        o_ref[...] = x_ref[...] + (res_w * gate_ref[...]) * ((yv * r) * g_ref[...])

    sds = jax.ShapeDtypeStruct((M, D), F32)
    return pl.pallas_call(
        body, name=name, grid=(M // tm,),
        in_specs=[_row_spec(tm, K), pl.BlockSpec((K, D), lambda i: (0, 0)), _row_spec(tm, D), _vec_spec(D),
                  _vec_spec(D)],
        out_specs=[_row_spec(tm, D), _row_spec(tm, D)], out_shape=[sds, sds],
        compiler_params=_params("parallel"),
    )(a, b, x, g, gate)


def _postnorm_bwd(dxo, y, g, gate, res_w, *, name, exchange=None):
    L, D = y.shape
    t = _tile(L, ROW_TILE, SUBLANES)
    n = L // t

    def body(dxo_ref, y_ref, g_ref, gate_ref, dy_ref, dgate_ref, dg_ref):
        i = pl.program_id(0)

        @pl.when(i == 0)
        def _():
            dgate_ref[...] = jnp.zeros_like(dgate_ref)
            dg_ref[...] = jnp.zeros_like(dg_ref)

        yv = y_ref[...]
        dv = dxo_ref[...]
        gv = g_ref[...]
        r = lax.rsqrt(jnp.mean(yv * yv, axis=-1, keepdims=True) + RMS_EPS)
        yn = yv * r
        dgate_ref[...] += _colsum8(dv * (res_w * (yn * gv)))
        do = dv * (res_w * gate_ref[...])
        dg_ref[...] += _colsum8(do * yn)
        dyn = do * gv
        dy_ref[...] = (r * (dyn - yn * jnp.mean(dyn * yn, axis=-1, keepdims=True))).astype(BF16)
        _finish_colsums(i, n - 1, (dgate_ref, dg_ref))

    sum_sds = jax.ShapeDtypeStruct((SUBLANES, D), F32)
    return _call_with_exchange(
        body, exchange, name=name, grid=(n,), sem=("arbitrary",),
        in_specs=[_row_spec(t, D), _row_spec(t, D), _vec_spec(D), _vec_spec(D)],
        out_specs=[_row_spec(t, D), _vec_spec(D, SUBLANES), _vec_spec(D, SUBLANES)],
        out_shape=[jax.ShapeDtypeStruct((L, D), BF16), sum_sds, sum_sds], operands=(dxo, y, g, gate))


def _prenorm_bwd(x, dh, dxres, g, sc, *, name):
    L, D = x.shape
    t = _tile(L, ROW_TILE, SUBLANES)
    n = L // t

    def body(x_ref, dh_ref, dxr_ref, g_ref, sc_ref, dx_ref, dsh_ref, dsc_ref, dg_ref):
        i = pl.program_id(0)

        @pl.when(i == 0)
        def _():
            dsh_ref[...] = jnp.zeros_like(dsh_ref)
            dsc_ref[...] = jnp.zeros_like(dsc_ref)
            dg_ref[...] = jnp.zeros_like(dg_ref)

        xv = x_ref[...]
        dhv = dh_ref[...].astype(F32)
        gv = g_ref[...]
        one_sc = 1.0 + sc_ref[...]
        r = lax.rsqrt(jnp.mean(xv * xv, axis=-1, keepdims=True) + RMS_EPS)
        xn = xv * r
        tt = dhv * xn
        dsh_ref[...] += _colsum8(dhv)
        dsc_ref[...] += _colsum8(tt * gv)
        dg_ref[...] += _colsum8(tt * one_sc)
        dxn = dhv * (gv * one_sc)
        dx_ref[...] = dxr_ref[...] + r * (dxn - xn * jnp.mean(dxn * xn, axis=-1, keepdims=True))
        _finish_colsums(i, n - 1, (dsh_ref, dsc_ref, dg_ref))

    sum_sds = jax.ShapeDtypeStruct((SUBLANES, D), F32)
    sum_spec = _vec_spec(D, SUBLANES)
    return pl.pallas_call(
        body, name=name, grid=(n,),
        in_specs=[_row_spec(t, D), _row_spec(t, D), _row_spec(t, D), _vec_spec(D), _vec_spec(D)],
        out_specs=[_row_spec(t, D), sum_spec, sum_spec, sum_spec],
        out_shape=[jax.ShapeDtypeStruct((L, D), F32), sum_sds, sum_sds, sum_sds],
        compiler_params=_params("arbitrary"),
    )(x, dh, dxres, g, sc)


def _loss_head(y, target, *, name):
    L, D = y.shape
    t = _tile(L, ROW_TILE, SUBLANES)
    n = L // t

    def body(y_ref, t_ref, dy_ref, loss_ref):
        i = pl.program_id(0)

        @pl.when(i == 0)
        def _():
            loss_ref[...] = jnp.zeros_like(loss_ref)

        e = y_ref[...] - t_ref[...]
        dy_ref[...] = e * (1.0 / D)
        part = jnp.sum(jnp.mean(e * e, axis=-1, keepdims=True), axis=0, keepdims=True)
        loss_ref[...] += jnp.broadcast_to(0.5 * part, loss_ref.shape)

    return pl.pallas_call(
        body, name=name, grid=(n,),
        in_specs=[_row_spec(t, D), _row_spec(t, D)],
        out_specs=[_row_spec(t, D), pl.BlockSpec((SUBLANES, LANES), lambda i: (0, 0))],
        out_shape=[jax.ShapeDtypeStruct((L, D), F32), jax.ShapeDtypeStruct((SUBLANES, LANES), F32)],
        compiler_params=_params("arbitrary"),
    )(y, target)


GELU_C = math.sqrt(2.0 / math.pi)


def _gelu_fwd(y, *, name):
    L, W = y.shape
    t = _tile(L, 512, SUBLANES)

    def body(y_ref, o_ref):
        v = y_ref[...]
        o_ref[...] = (0.5 * v * (1.0 + jnp.tanh(GELU_C * (v + 0.044715 * (v * v * v))))).astype(BF16)

    return pl.pallas_call(
        body, name=name, grid=(L // t,), in_specs=[_row_spec(t, W)], out_specs=_row_spec(t, W),
        out_shape=jax.ShapeDtypeStruct((L, W), BF16), compiler_params=_params("parallel"),
    )(y)


def _gelu_bwd(dgl, y, u, dskip, *, name):
    L, W = y.shape
    t = _tile(L, 512, SUBLANES)
    n = L // t

    def body(dgl_ref, y_ref, u_ref, d_ref, dy_ref, sk_ref, dd_ref):
        i = pl.program_id(0)

        @pl.when(i == 0)
        def _():
            dd_ref[...] = jnp.zeros_like(dd_ref)

        v = y_ref[...]
        inner = GELU_C * (v + 0.044715 * (v * v * v))
        th = jnp.tanh(inner)
        dgelu = 0.5 * (1.0 + th) + 0.5 * v * (1.0 - th * th) * (GELU_C * (1.0 + 3.0 * 0.044715 * (v * v)))
        dy = dgl_ref[...] * dgelu
        dy_ref[...] = dy.astype(BF16)
        sk_ref[...] = dy * d_ref[...]
        dd_ref[...] += _colsum8(dy * u_ref[...])
        _finish_colsums(i, n - 1, (dd_ref,))

    return pl.pallas_call(
        body, name=name, grid=(n,),
        in_specs=[_row_spec(t, W), _row_spec(t, W), _row_spec(t, W), _vec_spec(W)],
        out_specs=[_row_spec(t, W), _row_spec(t, W), _vec_spec(W, SUBLANES)],
        out_shape=[jax.ShapeDtypeStruct((L, W), BF16), jax.ShapeDtypeStruct((L, W), F32),
                   jax.ShapeDtypeStruct((SUBLANES, W), F32)],
        compiler_params=_params("arbitrary"),
    )(dgl, y, u, dskip)


def _merge_fwd(z, yb, gab, *, name):
    L, D = yb.shape
    t = _tile(L, ROW_TILE, SUBLANES)

    def body(z_ref, yb_ref, gab_ref, o_ref):
        ya = z_ref[:, :D] * _sigmoid(z_ref[:, D:])
        o_ref[...] = (_sigmoid(gab_ref[:, :D]) * ya + _sigmoid(gab_ref[:, D:]) * yb_ref[...]).astype(BF16)

    return pl.pallas_call(
        body, name=name, grid=(L // t,),
        in_specs=[_row_spec(t, 2 * D), _row_spec(t, D), _row_spec(t, 2 * D)],
        out_specs=_row_spec(t, D), out_shape=jax.ShapeDtypeStruct((L, D), BF16),
        compiler_params=_params("parallel"),
    )(z, yb, gab)


def _merge_bwd(dm, z, yb, gab, *, name, exchange=None):
    L, D = yb.shape
    t = _tile(L, ROW_TILE, SUBLANES)

    def body(dm_ref, z_ref, yb_ref, gab_ref, dz_ref, dyb_ref, dgab_ref):
        dmv = dm_ref[...]
        zv = z_ref[:, :D]
        sz = _sigmoid(z_ref[:, D:])
        sa = _sigmoid(gab_ref[:, :D])
        sb = _sigmoid(gab_ref[:, D:])
        ybv = yb_ref[...]
        dya = dmv * sa
        dz_ref[:, :D] = (dya * sz).astype(BF16)
        dz_ref[:, D:] = (dya * zv * (sz * (1.0 - sz))).astype(BF16)
        dyb_ref[...] = (dmv * sb).astype(BF16)
        dgab_ref[:, :D] = (dmv * (zv * sz) * (sa * (1.0 - sa))).astype(BF16)
        dgab_ref[:, D:] = (dmv * ybv * (sb * (1.0 - sb))).astype(BF16)

    return _call_with_exchange(
        body, exchange, name=name, grid=(L // t,),
        in_specs=[_row_spec(t, D), _row_spec(t, 2 * D), _row_spec(t, D), _row_spec(t, 2 * D)],
        out_specs=[_row_spec(t, 2 * D), _row_spec(t, D), _row_spec(t, 2 * D)],
        out_shape=[jax.ShapeDtypeStruct((L, 2 * D), BF16), jax.ShapeDtypeStruct((L, D), BF16),
                   jax.ShapeDtypeStruct((L, 2 * D), BF16)], operands=(dm, z, yb, gab))


SCAN_W = 1024
SCAN_T = 512


def _interleave(x):
    L, W = x.shape
    seg = SCAN_T // SUBLANES
    return x.reshape(L // SCAN_T, SUBLANES, seg, W).transpose(0, 2, 1, 3).reshape(L, W)


def _deinterleave(x):
    L, W = x.shape
    seg = SCAN_T // SUBLANES
    return x.reshape(L // SCAN_T, seg, SUBLANES, W).transpose(0, 2, 1, 3).reshape(L, W)


def _power_table(a, b, pr_tab, pi_tab, n):
    def fill(k, carry):
        pr, pi = carry
        pr_tab[k] = pr
        pi_tab[k] = pi
        return a * pr - b * pi, a * pi + b * pr

    lax.fori_loop(0, n, fill, (a, b))


def _rows_to_tile(rows):
    w = rows[0].shape[1]
    sub = lax.broadcasted_iota(jnp.int32, (SUBLANES, w), 0)
    tile = jnp.broadcast_to(rows[0], (SUBLANES, w))
    for j in range(1, SUBLANES):
        tile = jnp.where(sub == j, jnp.broadcast_to(rows[j], (SUBLANES, w)), tile)
    return tile


def _ssm_scan_fwd(bu_re, bu_im, lam_re, lam_im, *, name):
    L, S = bu_re.shape
    w, t = _tile(S, SCAN_W), SCAN_T
    seg = t // SUBLANES

    def body(br_ref, bi_ref, lr_ref, li_ref, sr_ref, si_ref, pr_tab, pi_tab, cr_ref, ci_ref):
        a = jnp.broadcast_to(lr_ref[...], (SUBLANES, w))
        b = jnp.broadcast_to(li_ref[...], (SUBLANES, w))

        @pl.when(pl.program_id(1) == 0)
        def _():
            cr_ref[...] = jnp.zeros_like(cr_ref)
            ci_ref[...] = jnp.zeros_like(ci_ref)
            _power_table(a, b, pr_tab, pi_tab, seg)

        def local_scan(i, carry):
            sr, si = carry
            rows = pl.ds(pl.multiple_of(i * SUBLANES, SUBLANES), SUBLANES)
            nr = a * sr - b * si + br_ref[rows, :]
            ni = a * si + b * sr + bi_ref[rows, :]
            sr_ref[rows, :] = nr
            si_ref[rows, :] = ni
            return nr, ni

        zero = jnp.zeros((SUBLANES, w), F32)
        fr, fi = lax.fori_loop(0, seg, local_scan, (zero, zero), unroll=2)
        lsr, lsi = pr_tab[seg - 1][0:1, :], pi_tab[seg - 1][0:1, :]
        cr, ci = cr_ref[...], ci_ref[...]
        rows_r, rows_i = [], []
        for j in range(SUBLANES):
            rows_r.append(cr)
            rows_i.append(ci)
            cr, ci = fr[j:j + 1, :] + (lsr * cr - lsi * ci), fi[j:j + 1, :] + (lsr * ci + lsi * cr)
        cr_ref[...] = cr
        ci_ref[...] = ci
        in_r, in_i = _rows_to_tile(rows_r), _rows_to_tile(rows_i)

        def add_entry(i, _):
            rows = pl.ds(pl.multiple_of(i * SUBLANES, SUBLANES), SUBLANES)
            pr, pi = pr_tab[i], pi_tab[i]
            sr_ref[rows, :] += pr * in_r - pi * in_i
            si_ref[rows, :] += pr * in_i + pi * in_r
            return 0

        lax.fori_loop(0, seg, add_entry, 0, unroll=2)

    blk = pl.BlockSpec((t, w), lambda j, i: (i, j))
    vec = pl.BlockSpec((1, w), lambda j, i: (0, j))
    sds = jax.ShapeDtypeStruct((L, S), F32)
    tab = pltpu.VMEM((seg, SUBLANES, w), F32)
    return pl.pallas_call(
        body, name=name, grid=(S // w, L // t),
        in_specs=[blk, blk, vec, vec], out_specs=[blk, blk], out_shape=[sds, sds],
        scratch_shapes=[tab, tab, pltpu.VMEM((1, w), F32), pltpu.VMEM((1, w), F32)],
        compiler_params=_params("parallel", "arbitrary"),
    )(bu_re, bu_im, lam_re, lam_im)


def _ssm_scan_bwd(d_re, d_im, s_re, s_im, lam_re, lam_im, *, name):
    L, S = d_re.shape
    w, t = _tile(S, SCAN_W), SCAN_T
    nt = L // t
    seg = t // SUBLANES

    def body(dr_ref, di_ref, sr_ref, si_ref, lr_ref, li_ref, gr_ref, gi_ref, ar_ref, ai_ref,
             pr_tab, pi_tab, cgr, cgi, acc_r, acc_i):
        step = pl.program_id(1)
        a = jnp.broadcast_to(lr_ref[...], (SUBLANES, w))
        b = jnp.broadcast_to(-li_ref[...], (SUBLANES, w))

        @pl.when(step == 0)
        def _():
            for r in (cgr, cgi, acc_r, acc_i):
                r[...] = jnp.zeros_like(r)
            _power_table(a, b, pr_tab, pi_tab, seg)

        def local_scan(ii, carry):
            gr, gi = carry
            rows = pl.ds(pl.multiple_of((seg - 1 - ii) * SUBLANES, SUBLANES), SUBLANES)
            ngr = a * gr - b * gi + dr_ref[rows, :]
            ngi = a * gi + b * gr + di_ref[rows, :]
            gr_ref[rows, :] = ngr
            gi_ref[rows, :] = ngi
            return ngr, ngi

        zero = jnp.zeros((SUBLANES, w), F32)
        fr, fi = lax.fori_loop(0, seg, local_scan, (zero, zero), unroll=2)
        lsr, lsi = pr_tab[seg - 1][0:1, :], pi_tab[seg - 1][0:1, :]
        cr, ci = cgr[...], cgi[...]
        rows_r, rows_i = [None] * SUBLANES, [None] * SUBLANES
        for j in reversed(range(SUBLANES)):
            rows_r[j], rows_i[j] = cr, ci
            cr, ci = fr[j:j + 1, :] + (lsr * cr - lsi * ci), fi[j:j + 1, :] + (lsr * ci + lsi * cr)
        cgr[...] = cr
        cgi[...] = ci
        in_r, in_i = _rows_to_tile(rows_r), _rows_to_tile(rows_i)

        def add_entry(ii, carry):
            nr, ni, xr, xi = carry
            rows = pl.ds(pl.multiple_of((seg - 1 - ii) * SUBLANES, SUBLANES), SUBLANES)
            sr = sr_ref[rows, :]
            si = si_ref[rows, :]
            xr = xr + (nr * sr + ni * si)
            xi = xi + (ni * sr - nr * si)
            pr, pi = pr_tab[ii], pi_tab[ii]
            gr = gr_ref[rows, :] + (pr * in_r - pi * in_i)
            gi = gi_ref[rows, :] + (pr * in_i + pi * in_r)
            gr_ref[rows, :] = gr
            gi_ref[rows, :] = gi
            return gr, gi, xr, xi

        _, _, xr, xi = lax.fori_loop(0, seg, add_entry, (in_r, in_i, acc_r[...], acc_i[...]), unroll=2)
        acc_r[...] = xr
        acc_i[...] = xi

        @pl.when(step == nt - 1)
        def _():
            ar_ref[...] = jnp.sum(xr, axis=0, keepdims=True)
            ai_ref[...] = jnp.sum(xi, axis=0, keepdims=True)

    blk = pl.BlockSpec((t, w), lambda j, i: (nt - 1 - i, j))
    vec = pl.BlockSpec((1, w), lambda j, i: (0, j))
    sds = jax.ShapeDtypeStruct((L, S), F32)
    vsds = jax.ShapeDtypeStruct((1, S), F32)
    tab = pltpu.VMEM((seg, SUBLANES, w), F32)
    tile = pltpu.VMEM((SUBLANES, w), F32)
    return pl.pallas_call(
        body, name=name, grid=(S // w, nt),
        in_specs=[blk, blk, blk, blk, vec, vec], out_specs=[blk, blk, vec, vec],
        out_shape=[sds, sds, vsds, vsds],
        scratch_shapes=[tab, tab, pltpu.VMEM((1, w), F32), pltpu.VMEM((1, w), F32), tile, tile],
        compiler_params=_params("parallel", "arbitrary"),
    )(d_re, d_im, s_re, s_im, lam_re, lam_im)


def _ssm_discretize(a_re, a_im, log_dt, b_re, b_im, c_re, c_im, d_skip):
    G, P, N = SSM_GROUPS, SSM_STATE, SSM_GROUP
    a = jnp.minimum(a_re, -1e-4)
    dt = jnp.exp(log_dt)[:, None]
    mag = jnp.exp(a * dt)
    lr = mag * jnp.cos(a_im * dt)
    li = mag * jnp.sin(a_im * dt)
    den = a * a + a_im * a_im
    cr = ((lr - 1.0) * a + li * a_im) / den
    ci = (li * a - (lr - 1.0) * a_im) / den
    bbr = cr[..., None] * b_re - ci[..., None] * b_im
    bbi = cr[..., None] * b_im + ci[..., None] * b_re
    gl = G // SSM_BLOCKS
    eye = jnp.eye(gl, dtype=F32)[None, :, None, :, None]

    def in_map(bb):
        t = bb.transpose(0, 2, 1).reshape(SSM_BLOCKS, gl, N, P)
        return (eye * t[:, :, :, None, :]).reshape(SSM_BLOCKS, gl * N, gl * P)

    def out_map(c):
        t = c.transpose(0, 2, 1).reshape(SSM_BLOCKS, gl, P, N)
        return (eye * t[:, :, :, None, :]).reshape(SSM_BLOCKS, gl * P, gl * N)

    return (lr.reshape(1, G * P), li.reshape(1, G * P), in_map(bbr), in_map(bbi),
            out_map(c_re), out_map(-c_im), d_skip.reshape(1, SSM_WIDTH))


ATT_T = 512
ATT_WIDE = 2
CUM_T = 256


def _split3(x):
    hi = x.astype(BF16)
    r1 = x - hi.astype(F32)
    mid = r1.astype(BF16)
    lo = (r1 - mid.astype(F32)).astype(BF16)
    return hi, mid, lo


def _tri_dot(tri, x):
    hi, mid, lo = _split3(x)
    dot = lambda p: jnp.dot(tri, p, preferred_element_type=F32)
    return dot(hi) + dot(mid) + dot(lo)


def _log_sigmoid(x):
    return jnp.minimum(x, 0.0) - jnp.log(1.0 + jnp.exp(-jnp.abs(x)))


def _fox_cum(f, fb, *, name):
    L, W = f.shape
    t = _tile(L, CUM_T, SUBLANES)

    def body(f_ref, b_ref, o_ref, carry):
        @pl.when(pl.program_id(0) == 0)
        def _():
            carry[...] = jnp.zeros_like(carry)

        row = lax.broadcasted_iota(jnp.int32, (t, t), 0)
        col = lax.broadcasted_iota(jnp.int32, (t, t), 1)
        tri = jnp.where(col <= row, 1.0, 0.0).astype(BF16)
        c = _tri_dot(tri, _log_sigmoid(f_ref[...] + b_ref[...])) + carry[...]
        o_ref[...] = c
        carry[...] = c[t - 1:t, :]

    return pl.pallas_call(
        body, name=name, grid=(L // t,),
        in_specs=[_row_spec(t, W), _vec_spec(W)], out_specs=_row_spec(t, W),
        out_shape=jax.ShapeDtypeStruct((L, W), F32),
        scratch_shapes=[pltpu.VMEM((1, W), F32)], compiler_params=_params("arbitrary"),
    )(f, fb)


def _fox_cum_bwd(dcum, f, fb, *, name):
    L, W = f.shape
    t = _tile(L, CUM_T, SUBLANES)
    n = L // t

    def body(d_ref, f_ref, b_ref, o_ref, db_ref, carry):
        i = pl.program_id(0)

        @pl.when(i == 0)
        def _():
            carry[...] = jnp.zeros_like(carry)
            db_ref[...] = jnp.zeros_like(db_ref)

        row = lax.broadcasted_iota(jnp.int32, (t, t), 0)
        col = lax.broadcasted_iota(jnp.int32, (t, t), 1)
        tri = jnp.where(col >= row, 1.0, 0.0).astype(BF16)
        dlog = _tri_dot(tri, d_ref[...]) + carry[...]
        carry[...] = dlog[0:1, :]
        df = dlog * _sigmoid(-(f_ref[...] + b_ref[...]))
        o_ref[...] = df.astype(BF16)
        db_ref[...] += _colsum8(df)
        _finish_colsums(i, n - 1, (db_ref,))

    rev = pl.BlockSpec((t, W), lambda i: (n - 1 - i, 0))
    return pl.pallas_call(
        body, name=name, grid=(n,),
        in_specs=[rev, rev, _vec_spec(W)], out_specs=[rev, _vec_spec(W, SUBLANES)],
        out_shape=[jax.ShapeDtypeStruct((L, W), BF16), jax.ShapeDtypeStruct((SUBLANES, W), F32)],
        scratch_shapes=[pltpu.VMEM((1, W), F32)], compiler_params=_params("arbitrary"),
    )(dcum, f, fb)


def _head_col(blk, h):
    lane = lax.broadcasted_iota(jnp.int32, blk.shape, 1)
    return jnp.sum(jnp.where(lane == h * HEAD_DIM, blk, 0.0), axis=1, keepdims=True)


def _lo_mask(rows):
    return lax.broadcasted_iota(jnp.int32, (rows, LANES), 1) < HEAD_DIM


def _causal(t):
    row = lax.broadcasted_iota(jnp.int32, (t, t), 0)
    col = lax.broadcasted_iota(jnp.int32, (t, t), 1)
    return col <= row


def _call_with_exchange(body, exchange, *, name, grid, in_specs, out_specs, out_shape, operands, sem=None):
    sem = sem or ("parallel",) + ("arbitrary",) * (len(grid) - 1)
    if exchange is None:
        return pl.pallas_call(body, name=name, grid=grid, in_specs=in_specs, out_specs=out_specs,
                              out_shape=out_shape, compiler_params=_params(*sem))(*operands)
    n_in, n_out = len(in_specs), len(out_specs)
    ei, eo = len(exchange.operands), len(exchange.out_shapes)

    def wrapped(*refs):
        ins, ex_in = refs[:n_in], refs[n_in:n_in + ei]
        outs, ex_out = refs[n_in + ei:n_in + ei + n_out], refs[n_in + ei + n_out:n_in + ei + n_out + eo]
        sems = refs[n_in + ei + n_out + eo:]
        ids = [pl.program_id(d) for d in range(len(grid))]
        first = functools.reduce(jnp.logical_and, [i == 0 for i in ids])
        last = functools.reduce(jnp.logical_and, [i == g - 1 for i, g in zip(ids, grid)])

        @pl.when(first)
        def _():
            exchange.start(ex_in, ex_out, sems)

        body(*ins, *outs)

        @pl.when(last)
        def _():
            exchange.finish(ex_in, ex_out, sems)

    return pl.pallas_call(
        wrapped, name=name, grid=grid, in_specs=list(in_specs) + [HBM_SPEC] * ei,
        out_specs=list(out_specs) + [HBM_SPEC] * eo, out_shape=list(out_shape) + list(exchange.out_shapes),
        scratch_shapes=exchange.sems, compiler_params=_params(*(("arbitrary",) * len(grid))),
    )(*operands, *exchange.operands)


def _fox_fwd(qkv, cum_cols, cum_rows, *, name, exchange=None):
    L = qkv.shape[0]
    t = _tile(L, ATT_T)
    nq = L // t
    npair = ATTN_HEADS // 2

    def body(q_ref, k_ref, v_ref, cc_ref, cr_ref, o_ref, o32_ref, lse_ref):
        iq = pl.program_id(1)
        lo = _lo_mask(t)
        qv = q_ref[...] * 0.125
        zq = jnp.zeros_like(qv)
        qh = (jnp.where(lo, qv, zq), jnp.where(lo, zq, qv))
        ccv = cc_ref[...]
        cq = (_head_col(ccv, 0), _head_col(ccv, 1))

        def step(block, width, carry, masked):
            start = pl.multiple_of(block * width, width)
            kb = k_ref[pl.ds(start, width), :]
            vb = v_ref[pl.ds(start, width), :]
            out = []
            for h in range(2):
                m, l, acc = carry[h]
                s = lax.dot_general(qh[h], kb, NT, preferred_element_type=F32)
                s = s + (cq[h] - cr_ref[h:h + 1, pl.ds(start, width)])
                if masked:
                    s = jnp.where(_causal(t), s, -jnp.inf)
                m_new = jnp.maximum(m, jnp.max(s, axis=1, keepdims=True))
                alpha = jnp.exp(m - m_new)
                p = jnp.exp(s - m_new)
                l = alpha * l + jnp.sum(p, axis=1, keepdims=True)
                acc = alpha * acc + jnp.dot(p.astype(BF16), vb, preferred_element_type=F32)
                out.append((m_new, l, acc))
            return tuple(out)

        wide = ATT_WIDE * t
        init1 = (jnp.full((t, 1), -jnp.inf, F32), jnp.zeros((t, 1), F32), jnp.zeros((t, LANES), F32))
        carry = lax.fori_loop(0, iq // ATT_WIDE, lambda j, c: step(j, wide, c, False), (init1, init1))
        carry = lax.fori_loop((iq // ATT_WIDE) * ATT_WIDE, iq, lambda ik, c: step(ik, t, c, False), carry)
        (m0, l0, a0), (m1, l1, a1) = step(iq, t, carry, True)
        out = jnp.where(lo, a0 / l0, a1 / l1)
        o_ref[...] = out.astype(BF16)
        o32_ref[...] = out
        lse_ref[...] = jnp.where(lo, m0 + jnp.log(l0), m1 + jnp.log(l1))

    blk = lambda off: pl.BlockSpec((t, LANES), lambda hp, iq: (iq, off + hp))
    whole = lambda off: pl.BlockSpec((L, LANES), lambda hp, iq: (0, off + hp))
    return _call_with_exchange(
        body, exchange, name=name, grid=(npair, nq),
        in_specs=[blk(0), whole(npair), whole(2 * npair), blk(0),
                  pl.BlockSpec((None, SUBLANES, L), lambda hp, iq: (hp, 0, 0))],
        out_specs=[blk(0), blk(0), blk(0)],
        out_shape=[jax.ShapeDtypeStruct((L, ATTN_WIDTH), BF16), jax.ShapeDtypeStruct((L, ATTN_WIDTH), F32),
                   jax.ShapeDtypeStruct((L, ATTN_WIDTH), F32)],
        operands=(qkv, qkv, qkv, cum_cols, cum_rows))


STAT_LSE, STAT_CUM, STAT_DELTA = 0, 2, 4


def _lane_col(blk, idx):
    lane = lax.broadcasted_iota(jnp.int32, blk.shape, 1)
    return jnp.sum(jnp.where(lane == idx, blk, 0.0), axis=1, keepdims=True)


def _fox_rowstats(do, o, lse, cum_cols, *, name):
    L = do.shape[0]
    t = _tile(L, ATT_T)

    def body(do_ref, o_ref, lse_ref, cc_ref, st_ref):
        lo = _lo_mask(t)
        dd = do_ref[...].astype(F32) * o_ref[...]
        lsev, ccv = lse_ref[...], cc_ref[...]
        cols = (_head_col(lsev, 0), _head_col(lsev, 1), _head_col(ccv, 0), _head_col(ccv, 1),
                jnp.sum(jnp.where(lo, dd, 0.0), axis=1, keepdims=True),
                jnp.sum(jnp.where(lo, 0.0, dd), axis=1, keepdims=True))
        lane = lax.broadcasted_iota(jnp.int32, (t, LANES), 1)
        out = jnp.zeros((t, LANES), F32)
        for i, col in enumerate(cols):
            out = jnp.where(lane == i, col, out)
        st_ref[...] = out

    blk = pl.BlockSpec((t, LANES), lambda hp, i: (i, hp))
    return pl.pallas_call(
        body, name=name, grid=(ATTN_HEADS // 2, L // t),
        in_specs=[blk, blk, blk, blk], out_specs=blk,
        out_shape=jax.ShapeDtypeStruct((L, ATTN_WIDTH), F32),
        compiler_params=_params("parallel", "parallel"),
    )(do, o, lse, cum_cols)


def _fox_bwd(qkv, do, stats, cum_rows, *, name, exchange=None):
    L = qkv.shape[0]
    t = _tile(L, ATT_T)
    nq = L // t
    npair = ATTN_HEADS // 2

    def body(q_ref, do_ref, st_ref, qt_ref, dot_ref, k_ref, v_ref, cr_ref, dk_ref, dv_ref, dc_ref, dq_ref, drow_ref):
        ik = pl.program_id(1)

        @pl.when(ik == 0)
        def _():
            dq_ref[...] = jnp.zeros_like(dq_ref)
            drow_ref[...] = jnp.zeros_like(drow_ref)

        lo = _lo_mask(t)
        lo_rows = lax.broadcasted_iota(jnp.int32, (LANES, t), 0) < HEAD_DIM
        lane = lax.broadcasted_iota(jnp.int32, (t, LANES), 1)
        kb = k_ref[...]
        vb = v_ref[...]
        zk = jnp.zeros_like(kb)
        kh = (jnp.where(lo, kb, zk), jnp.where(lo, zk, kb))
        vh = (jnp.where(lo, vb, zk), jnp.where(lo, zk, vb))
        ck = (cr_ref[0:1, :], cr_ref[1:2, :])

        def step(block, width, carry, masked):
            dk, dv, dc0, dc1 = carry
            start = pl.multiple_of(block * width, width)
            qb = q_ref[pl.ds(start, width), :] * 0.125
            dob = do_ref[pl.ds(start, width), :]
            stb = st_ref[pl.ds(start, width), :]
            lane = lax.broadcasted_iota(jnp.int32, (width, LANES), 1)
            qtb = qt_ref[:, pl.ds(start, width)] * 0.125
            dotb = dot_ref[:, pl.ds(start, width)]
            dks, dvs, dcs, dqs, rss = [], [], [], [], []
            for h in range(2):
                s = lax.dot_general(qb, kh[h], NT, preferred_element_type=F32)
                s = s + (_lane_col(stb, STAT_CUM + h) - ck[h])
                if masked:
                    s = jnp.where(_causal(t), s, -jnp.inf)
                p = jnp.exp(s - _lane_col(stb, STAT_LSE + h))
                dp = lax.dot_general(dob, vh[h], NT, preferred_element_type=F32)
                ds = p * (dp - _lane_col(stb, STAT_DELTA + h))
                dsb = ds.astype(BF16)
                dvs.append(jnp.dot(dotb, p.astype(BF16), preferred_element_type=F32))
                dks.append(jnp.dot(qtb, dsb, preferred_element_type=F32))
                dqs.append(jnp.dot(dsb, kh[h], preferred_element_type=F32))
                dcs.append(jnp.sum(ds, axis=0, keepdims=True))
                rss.append(jnp.sum(ds, axis=1, keepdims=True))
            dq_ref[pl.ds(start, width), :] += 0.125 * (dqs[0] + dqs[1])
            drow_ref[pl.ds(start, width), :] += jnp.where(lane == 0, rss[0], jnp.where(lane == 1, rss[1], 0.0))
            return (dk + jnp.where(lo_rows, dks[0], dks[1]), dv + jnp.where(lo_rows, dvs[0], dvs[1]),
                    dc0 - dcs[0], dc1 - dcs[1])

        zero = jnp.zeros((LANES, t), F32)
        zrow = jnp.zeros((1, t), F32)
        carry = step(ik, t, (zero, zero, zrow, zrow), True)
        first_wide = (ik + ATT_WIDE) // ATT_WIDE
        carry = lax.fori_loop(ik + 1, jnp.minimum(first_wide * ATT_WIDE, nq), lambda iq, c: step(iq, t, c, False),
                              carry)
        dk, dv, dc0, dc1 = lax.fori_loop(first_wide, nq // ATT_WIDE,
                                         lambda j, c: step(j, ATT_WIDE * t, c, False), carry)
        dk_ref[...] = dk.T.astype(BF16)
        dv_ref[...] = dv.T.astype(BF16)
        dc_ref[...] = jnp.zeros_like(dc_ref)
        dc_ref[0:1, :] = dc0
        dc_ref[1:2, :] = dc1

    whole = lambda off: pl.BlockSpec((L, LANES), lambda hp, ik: (0, off + hp))
    blk = lambda off: pl.BlockSpec((t, LANES), lambda hp, ik: (ik, off + hp))
    rows = pl.BlockSpec((None, SUBLANES, t), lambda hp, ik: (hp, 0, ik))
    whole_t = pl.BlockSpec((LANES, L), lambda hp, ik: (hp, 0))
    return _call_with_exchange(
        body, exchange, name=name, grid=(npair, nq),
        in_specs=[whole(0), whole(0), whole(0), whole_t, whole_t, blk(npair), blk(2 * npair), rows],
        out_specs=[blk(0), blk(0), rows, whole(0), whole(0)],
        out_shape=[jax.ShapeDtypeStruct((L, ATTN_WIDTH), BF16), jax.ShapeDtypeStruct((L, ATTN_WIDTH), BF16),
                   jax.ShapeDtypeStruct((npair, SUBLANES, L), F32),
                   jax.ShapeDtypeStruct((L, ATTN_WIDTH), F32), jax.ShapeDtypeStruct((L, ATTN_WIDTH), F32)],
        operands=(qkv, do, stats, qkv[:, :ATTN_WIDTH].T, do.T, qkv, qkv, cum_rows))


def _mod_partial(c_all, mod_w, mod_b_cols, *, name):
    depth, K, cols = mod_w.shape
    tn = _tile(cols, 768)

    def body(c_ref, w_ref, b_ref, o_ref):
        cv = c_ref[...]
        sc = (cv * _sigmoid(cv)).astype(BF16)
        o_ref[...] = jnp.dot(sc, w_ref[...].astype(BF16), preferred_element_type=F32) + b_ref[...]

    return pl.pallas_call(
        body, name=name, grid=(depth, cols // tn),
        in_specs=[pl.BlockSpec((N_DEV, K), lambda l, j: (0, 0)),
                  pl.BlockSpec((None, K, tn), lambda l, j: (l, 0, j)),
                  pl.BlockSpec((None, 1, tn), lambda l, j: (l, 0, j))],
        out_specs=pl.BlockSpec((None, N_DEV, tn), lambda l, j: (l, 0, j)),
        out_shape=jax.ShapeDtypeStruct((depth, N_DEV, cols), F32),
        compiler_params=_params("parallel", "parallel"),
    )(c_all, mod_w, mod_b_cols)


def _mod_wgrad(c_all_t, dmod, *, name):
    depth, nb, cols = dmod.shape
    K = c_all_t.shape[0]
    tn = _tile(cols, 768)
    tk = _tile(K, 256, SUBLANES)

    def body(c_ref, d_ref, o_ref):
        cv = c_ref[...]
        sc = cv * _sigmoid(cv)
        dv = d_ref[...]
        acc = sc[:, 0:1] * dv[0:1, :]
        for b in range(1, nb):
            acc = acc + sc[:, b:b + 1] * dv[b:b + 1, :]
        o_ref[...] = acc

    return pl.pallas_call(
        body, name=name, grid=(depth, K // tk, cols // tn),
        in_specs=[pl.BlockSpec((tk, nb), lambda l, i, j: (i, 0)),
                  pl.BlockSpec((None, nb, tn), lambda l, i, j: (l, 0, j))],
        out_specs=pl.BlockSpec((None, tk, tn), lambda l, i, j: (l, i, j)),
        out_shape=jax.ShapeDtypeStruct((depth, K, cols), F32),
        compiler_params=_params("parallel", "parallel", "parallel"),
    )(c_all_t, dmod)


def _adamw(w, g, m, v, *, name):
    shape = w.shape
    cols = shape[-1]
    rows = int(np.prod(shape[:-1]))
    t = _tile(rows, 256, SUBLANES) if rows % SUBLANES == 0 else rows
    r2 = lambda a: a.reshape(rows, cols)

    def body(w_ref, g_ref, m_ref, v_ref, d_ref, nm_ref, nv_ref):
        gv = g_ref[...]
        nm = ADAM_B1 * m_ref[...] + (1.0 - ADAM_B1) * gv
        nv = ADAM_B2 * v_ref[...] + (1.0 - ADAM_B2) * (gv * gv)
        m_hat = nm / (1.0 - ADAM_B1 ** ADAM_STEP)
        v_hat = nv / (1.0 - ADAM_B2 ** ADAM_STEP)
        d_ref[...] = -ADAM_LR * (m_hat / (jnp.sqrt(v_hat) + ADAM_EPS) + ADAM_WD * w_ref[...])
        nm_ref[...] = nm
        nv_ref[...] = nv

    spec = pl.BlockSpec((t, cols), lambda i: (i, 0))
    sds = jax.ShapeDtypeStruct((rows, cols), F32)
    d, nm, nv = pl.pallas_call(
        body, name=name, grid=(rows // t,),
        in_specs=[spec] * 4, out_specs=[spec] * 3, out_shape=[sds] * 3,
        compiler_params=_params("parallel"),
    )(r2(w), r2(g), r2(m), r2(v))
    return d.reshape(shape), nm.reshape(shape), nv.reshape(shape)


def _my_place():
    return lax.axis_index("x"), lax.axis_index("y"), lax.axis_index("c")


def _other_chips(x, y):
    return [(1 - x, y), (x, 1 - y), (1 - x, 1 - y)]


def _all_gather8(v, *, name, with_sum=False):
    m, n = v.shape

    def body(x_ref, out_ref, *rest):
        if with_sum:
            sum_ref, send_sems, recv_sems, local_sem = rest
        else:
            send_sems, recv_sems, local_sem = rest
        x, y, c = _my_place()
        me, sibling = (x, y, c), (x, y, 1 - c)
        chips = _other_chips(x, y)

        def rows(px, py, pc):
            return out_ref.at[pl.ds((4 * px + 2 * py + pc) * m, m), :]

        def copy(k, block, to, src=None):
            return pltpu.make_async_remote_copy(
                src_ref=rows(*block) if src is None else src, dst_ref=rows(*block),
                send_sem=send_sems.at[k], recv_sem=recv_sems.at[k], device_id=to, device_id_type=MESH)

        mine = pltpu.make_async_copy(x_ref, rows(*me), local_sem)
        mine.start()
        first = [copy(0, me, sibling, src=x_ref)]
        first += [copy(1 + j, me, (*chip, c), src=x_ref) for j, chip in enumerate(chips)]
        for cp in first:
            cp.start()
        passed = [copy(4 + j, (*chip, c), sibling) for j, chip in enumerate(chips)]
        for j, chip in enumerate(chips):
            copy(1 + j, (*chip, c), me).wait_recv()
            passed[j].start()
        copy(0, sibling, me).wait_recv()
        for j, chip in enumerate(chips):
            copy(4 + j, (*chip, 1 - c), me).wait_recv()
        for cp in first + passed:
            cp.wait_send()
        mine.wait()
        if with_sum:
            acc = out_ref[pl.ds(0, m), :]
            for d in range(1, N_DEV):
                acc = acc + out_ref[pl.ds(d * m, m), :]
            sum_ref[...] = acc

    vm = pl.BlockSpec(memory_space=pltpu.VMEM)
    out_shape = [jax.ShapeDtypeStruct((N_DEV * m, n), F32)]
    if with_sum:
        out_shape.append(jax.ShapeDtypeStruct((m, n), F32))
    res = pl.pallas_call(
        body, name=name, out_shape=out_shape, in_specs=[vm], out_specs=[vm] * len(out_shape),
        scratch_shapes=[pltpu.SemaphoreType.DMA((7,)), pltpu.SemaphoreType.DMA((7,)), pltpu.SemaphoreType.DMA],
        compiler_params=pltpu.CompilerParams(vmem_limit_bytes=VMEM_LIMIT),
    )(v)
    return res if with_sum else res[0]


class _Cut(NamedTuple):
    shape: tuple
    slab: int
    half: int


IN_WIDTH = SSM_WIDTH + 3 * ATTN_WIDTH + ATTN_HEADS + 2 * D_MODEL
CUTS = dict(
    ffn_w_in=_Cut((1, D_MODEL, 2 * D_FF), 2, 1),
    ffn_w_out=_Cut((1, D_FF, D_MODEL), 1, 2),
    mix_w_in=_Cut((N_CHIPS, D_MODEL, IN_WIDTH // N_CHIPS), 0, 1),
    glu_w=_Cut((1, SSM_WIDTH, 2 * D_MODEL), 2, 1),
    attn_w_out=_Cut((1, ATTN_WIDTH, D_MODEL), 2, 1),
    mix_w_out=_Cut((1, D_MODEL, D_MODEL), 1, 2),
)
LAYER_MATS = (("ffn_w_in", 0), ("ffn_w_in", 1), ("ffn_w_out", 0), ("ffn_w_out", 1), ("mix_w_in", None),
              ("glu_w", None), ("attn_w_out", None), ("mix_w_out", None))
FIRST_MATS = (0, 2)
MIXER_MATS = (4, 5, 6, 7)
LATE_MATS = (1, 3)


def _part_shape(cut, slab=False, half=False):
    s = list(cut.shape)
    if slab:
        s[cut.slab] //= N_CHIPS
    if half:
        s[cut.half] //= 2
    return tuple(s)


def _window(ref, cut, slab=None, half=None):
    idx = [slice(None)] * len(cut.shape)
    for axis, parts, which in ((cut.slab, N_CHIPS, slab), (cut.half, 2, half)):
        if which is not None:
            width = cut.shape[axis] // parts
            idx[axis] = pl.ds(pl.multiple_of(which * width, width), width)
    return ref.at[tuple(idx)]


HBM_SPEC = pl.BlockSpec(memory_space=pltpu.HBM)


class _Exchange(NamedTuple):
    operands: list
    out_shapes: list
    sems: list
    start: object
    finish: object


def _run_exchange(ex, *, name):
    ni, no = len(ex.operands), len(ex.out_shapes)

    def body(*refs):
        parts = (refs[:ni], refs[ni:ni + no], refs[ni + no:])
        ex.start(*parts)
        ex.finish(*parts)

    return pl.pallas_call(
        body, name=name, out_shape=ex.out_shapes, in_specs=[HBM_SPEC] * ni, out_specs=[HBM_SPEC] * no,
        scratch_shapes=ex.sems,
    )(*ex.operands)


def _gather_exchange(shards, cuts):
    n = len(shards)

    def setup(s_refs, f_refs, sems):
        send_sems, recv_sems = sems
        x, y, c = _my_place()
        me, sibling, mine = (x, y, c), (x, y, 1 - c), 2 * x + y
        chips = _other_chips(x, y)

        def copy(i, k, src, dst, to):
            return pltpu.make_async_remote_copy(
                src_ref=src, dst_ref=dst, send_sem=send_sems.at[7 * i + k], recv_sem=recv_sems.at[7 * i + k],
                device_id=to, device_id_type=MESH)

        def landed(i, j, half):
            return _window(f_refs[i], cuts[i], slab=2 * chips[j][0] + chips[j][1], half=half)

        def sends():
            own = [copy(i, 6, s_refs[i], _window(f_refs[i], cuts[i], slab=mine), sibling) for i in range(n)]
            return own + [copy(i, j, _window(s_refs[i], cuts[i], half=c),
                               _window(f_refs[i], cuts[i], slab=mine, half=c), (*chips[j], c))
                          for i in range(n) for j in range(3)]

        return c, me, sibling, copy, landed, sends

    def start(s_refs, f_refs, sems):
        for cp in setup(s_refs, f_refs, sems)[-1]():
            cp.start()

    def finish(s_refs, f_refs, sems):
        c, me, sibling, copy, landed, sends = setup(s_refs, f_refs, sems)
        passed = []
        for i in range(n):
            for j in range(3):
                copy(i, j, landed(i, j, c), landed(i, j, c), me).wait_recv()
                passed.append(copy(i, 3 + j, landed(i, j, c), landed(i, j, c), sibling))
                passed[-1].start()
        for i in range(n):
            for j in range(3):
                copy(i, 3 + j, landed(i, j, 1 - c), landed(i, j, 1 - c), me).wait_recv()
        for i in range(n):
            mine_i = _window(f_refs[i], cuts[i], slab=2 * me[0] + me[1])
            copy(i, 6, mine_i, mine_i, me).wait_recv()
        for cp in sends() + passed:
            cp.wait_send()

    return _Exchange(
        list(shards), [jax.ShapeDtypeStruct(cut.shape, s.dtype) for s, cut in zip(shards, cuts)],
        [pltpu.SemaphoreType.DMA((7 * n,)), pltpu.SemaphoreType.DMA((7 * n,))], start, finish)


def _swap_exchange(mats, cuts):
    n = len(mats)

    def copies(m_refs, r_refs, sems):
        send_sems, recv_sems = sems
        x, y, c = _my_place()
        return [pltpu.make_async_remote_copy(
            src_ref=_window(m_refs[i], cuts[i], half=1 - c), dst_ref=r_refs[i], send_sem=send_sems.at[i],
            recv_sem=recv_sems.at[i], device_id=(x, y, 1 - c), device_id_type=MESH) for i in range(n)]

    def start(m_refs, r_refs, sems):
        for cp in copies(m_refs, r_refs, sems):
            cp.start()

    def finish(m_refs, r_refs, sems):
        for cp in copies(m_refs, r_refs, sems):
            cp.wait()

    return _Exchange(
        list(mats), [jax.ShapeDtypeStruct(_part_shape(cut, half=True), m.dtype) for m, cut in zip(mats, cuts)],
        [pltpu.SemaphoreType.DMA((n,)), pltpu.SemaphoreType.DMA((n,))], start, finish)


def _partials_exchange(sums, cuts):
    n = len(sums)

    def copies(s_refs, p_refs, sems):
        send_sems, recv_sems = sems
        x, y, c = _my_place()
        return [pltpu.make_async_remote_copy(
            src_ref=_window(s_refs[i], cuts[i], slab=2 * chip[0] + chip[1]), dst_ref=p_refs[i].at[j],
            send_sem=send_sems.at[3 * i + j], recv_sem=recv_sems.at[3 * i + j],
            device_id=(*chip, c), device_id_type=MESH)
            for i in range(n) for j, chip in enumerate(_other_chips(x, y))]

    def start(s_refs, p_refs, sems):
        for cp in copies(s_refs, p_refs, sems):
            cp.start()

    def finish(s_refs, p_refs, sems):
        for cp in copies(s_refs, p_refs, sems):
            cp.wait()

    return _Exchange(
        list(sums), [jax.ShapeDtypeStruct((3,) + _part_shape(cut, slab=True, half=True), s.dtype)
                     for s, cut in zip(sums, cuts)],
        [pltpu.SemaphoreType.DMA((3 * n,)), pltpu.SemaphoreType.DMA((3 * n,))], start, finish)


def _share_all(dests, cuts, places, *, name):
    names = list(dests)
    nn, n = len(names), len(places)

    def body(*refs):
        o_refs = dict(zip(names, refs[nn:2 * nn]))
        send_sems, recv_sems = refs[2 * nn:]
        x, y, c = _my_place()

        def win(i, half):
            slab_cut = _Cut(_part_shape(cuts[i], slab=True), cuts[i].slab, cuts[i].half)
            return _window(o_refs[places[i][0]].at[places[i][1]], slab_cut, half=half)

        def copy(i, half):
            return pltpu.make_async_remote_copy(
                src_ref=win(i, half), dst_ref=win(i, half), send_sem=send_sems.at[i], recv_sem=recv_sems.at[i],
                device_id=(x, y, 1 - c), device_id_type=MESH)

        for i in range(n):
            copy(i, c).start()
        for i in range(n):
            copy(i, c).wait_send()
            copy(i, 1 - c).wait_recv()

    return pl.pallas_call(
        body, name=name, out_shape=[jax.ShapeDtypeStruct(dests[k].shape, F32) for k in names],
        in_specs=[HBM_SPEC] * nn, out_specs=[HBM_SPEC] * nn,
        input_output_aliases={i: i for i in range(nn)},
        scratch_shapes=[pltpu.SemaphoreType.DMA((n,)), pltpu.SemaphoreType.DMA((n,))],
    )(*[dests[k] for k in names])


def _cut_blocks(shape):
    _, R, C = shape
    tr = _tile(R, 256, 16)
    tc = _tile(C, 2048) if C % LANES == 0 else C
    return (None, tr, tc), (shape[0], R // tr, C // tc)


def _offset_map(axis, blocks):
    def index_map(b, i, j, which):
        idx = [b, i, j]
        idx[axis] = which[0] * blocks[axis] + idx[axis]
        return tuple(idx)
    return index_map


def _add_half(mat, other, cut, c_idx, *, name):
    shape = _part_shape(cut, half=True)
    block, grid = _cut_blocks(shape)

    def body(c_ref, m_ref, o_ref, f_ref, b_ref):
        s = m_ref[...] + o_ref[...]
        f_ref[...] = s
        b_ref[...] = s.astype(BF16)

    plain = pl.BlockSpec(block, lambda b, i, j, which: (b, i, j))
    grid_spec = pltpu.PrefetchScalarGridSpec(
        num_scalar_prefetch=1, grid=grid,
        in_specs=[pl.BlockSpec(block, _offset_map(cut.half, grid)), plain], out_specs=[plain, plain])
    return pl.pallas_call(
        body, name=name, grid_spec=grid_spec,
        out_shape=[jax.ShapeDtypeStruct(shape, F32), jax.ShapeDtypeStruct(shape, BF16)],
        compiler_params=_params("parallel", "parallel", "parallel"),
    )(c_idx, mat, other)


def _sum_slab(own, parts, cut, dest, place, chip_idx, c_idx, *, name):
    shape = _part_shape(cut, slab=True, half=True)
    block, grid = _cut_blocks(shape)
    assert shape[0] == 1

    def body(k_ref, c_ref, o_ref, p_ref, dest_ref, out_ref):
        acc = o_ref[...]
        for j in range(3):
            acc = acc + p_ref[j].astype(F32)
        out_ref[...] = acc

    def own_map(b, i, j, chip, core):
        idx = [b, i, j]
        idx[cut.slab] = chip[0] * grid[cut.slab] + idx[cut.slab]
        return tuple(idx)

    def dest_map(b, i, j, chip, core):
        idx = [b, i, j]
        idx[cut.half] = core[0] * grid[cut.half] + idx[cut.half]
        return tuple(place) + tuple(idx)

    grid_spec = pltpu.PrefetchScalarGridSpec(
        num_scalar_prefetch=2, grid=grid,
        in_specs=[pl.BlockSpec(block, own_map),
                  pl.BlockSpec((3,) + block[1:], lambda b, i, j, chip, core: (0, i, j)),
                  pl.BlockSpec(memory_space=pl.ANY)],
        out_specs=pl.BlockSpec((None,) * len(place) + block, dest_map))
    return pl.pallas_call(
        body, name=name, grid_spec=grid_spec, out_shape=jax.ShapeDtypeStruct(dest.shape, F32),
        input_output_aliases={4: 0},
        compiler_params=_params("parallel", "parallel", "parallel"),
    )(chip_idx, c_idx, own, parts.reshape((3,) + shape[1:]), dest)


def _pad_rows(flat, cols=8 * LANES, align=SUBLANES):
    n = flat.shape[0]
    rows = -(-n // (cols * align)) * align
    return jnp.pad(flat, (0, rows * cols - n)).reshape(rows, cols)


def _row(v):
    return v.reshape(1, -1)


def _ffn_fwd(x, mod, g_pre, g_post, w_in, w_out, tag, exchange=None):
    sh, sc, gate = _row(mod[0]), _row(mod[1]), _row(mod[2])
    h = _prenorm(x, _row(g_pre), sc, sh, name=f"prenorm_{tag}")
    gt, up, act, *exchanged = _mm_swiglu(h, w_in, name=f"swiglu_{tag}", exchange=exchange)
    y, x_out = _mm_postnorm(act, w_out, x, _row(g_post), gate, FFN_RES, name=f"ffn_out_{tag}")
    return x_out, (x, h, gt, up, act, y), exchanged


def _ffn_bwd(dxo, saved, mod, g_pre, g_post, w_in, w_out, tag, swap=None, after_swap=None):
    x, h, gt, up, act, y = saved
    sc, gate = _row(mod[1]), _row(mod[2])
    dy, dgate, dgpost, *swapped = _postnorm_bwd(dxo, y, _row(g_post), gate, FFN_RES, name=f"postnorm_bwd_{tag}",
                                                exchange=swap)
    exchange = after_swap(swapped) if after_swap is not None else None
    dgt, dup, *exchanged = _mm_swiglu_bwd(dy, w_out, gt, up, name=f"swiglu_bwd_{tag}", exchange=exchange)
    dw_out = _mm(act, dy, ta=True, name=f"dw_out_{tag}", tm=1408, tn=1024, tk=1024)
    dh = _mm(dgt, w_in, tb=True, name=f"dh_gate_{tag}", tk=D_FF)
    dh = _mm(dup, w_in, tb=True, b_k0=D_FF, bias=dh, name=f"dh_up_{tag}", tk=D_FF)
    dw_in = _mm(h, dgt, ta=True, out_n=2 * D_FF, name=f"dw_gate_{tag}", tm=1024, tn=1408, tk=1024)
    dw_in = _mm(h, dup, ta=True, out_n=2 * D_FF, out_j0=D_FF, into=dw_in, name=f"dw_up_{tag}",
                tm=1024, tn=1408, tk=1024)
    dx, dsh, dsc, dgpre = _prenorm_bwd(x, dh, dxo, _row(g_pre), sc, name=f"prenorm_bwd_{tag}")
    dmod = jnp.stack([dsh[0], dsc[0], dgate[0]])
    return dx, dmod, dgpre[0], dgpost[0], dw_in, dw_out, exchanged


def _split_mix_w_in(w):
    u0, q0, f0, g0 = 0, SSM_WIDTH, SSM_WIDTH + 3 * ATTN_WIDTH, SSM_WIDTH + 3 * ATTN_WIDTH + ATTN_HEADS
    w_f = jnp.pad(w[:, f0:g0], ((0, 0), (0, LANES - ATTN_HEADS)))
    return w[:, u0:q0], w[:, q0:f0], w_f, w[:, g0:]


def _mixer_fwd(x, mod, g_pre, g_post, w, ssm, forget_b, tag, exchange=None):
    L = x.shape[0]
    sh, sc, gate = _row(mod[0]), _row(mod[1]), _row(mod[2])
    lam_re, lam_im, bin_re, bin_im, cout_re, cout_im, dskip = ssm
    h = _prenorm(x, _row(g_pre), sc, sh, name=f"prenorm_{tag}")
    u = _mm(h, w["u"], name=f"proj_u_{tag}")
    qkv = _mm(h, w["qkv"], out_dtype=BF16, name=f"proj_qkv_{tag}")
    f = _mm(h, w["f"], name=f"proj_f_{tag}")
    gab = _mm(h, w["gab"], name=f"proj_gab_{tag}")
    u_i = _interleave(u)
    bu_re, bu_im = _mm_bd([u_i], [bin_re, bin_im], name=f"ssm_bu_{tag}")
    s_re, s_im = _ssm_scan_fwd(bu_re, bu_im, lam_re, lam_im, name=f"ssm_scan_{tag}")
    y_ssm = _deinterleave(_mm_bd([s_re, s_im], [cout_re, cout_im], bias=u_i, bscale=dskip, name=f"ssm_y_{tag}"))
    gl = _gelu_fwd(y_ssm, name=f"gelu_{tag}")
    z = _mm(gl, w["glu"], name=f"glu_{tag}")
    fb = jnp.pad(forget_b, (0, LANES - ATTN_HEADS)).reshape(1, LANES)
    cum = _fox_cum(f, fb, name=f"fox_cum_{tag}")
    cum8 = cum[:, :ATTN_HEADS]
    cum_cols = jnp.repeat(cum8, HEAD_DIM, axis=1)
    cum_rows = jnp.pad(cum8.T.reshape(ATTN_HEADS // 2, 2, L), ((0, 0), (0, SUBLANES - 2), (0, 0)))
    attn, attn32, lse, *exchanged = _fox_fwd(qkv, cum_cols, cum_rows, name=f"fox_fwd_{tag}", exchange=exchange)
    yb = _mm(attn, w["attn_out"], name=f"attn_out_{tag}")
    merged = _merge_fwd(z, yb, gab, name=f"merge_{tag}")
    y, x_out = _mm_postnorm(merged, w["out"], x, _row(g_post), gate, 1.0, name=f"mix_out_{tag}")
    saved = (x, h, u, u_i, qkv, f, gab, s_re, s_im, y_ssm, gl, z, fb, cum_cols, cum_rows, attn, attn32, lse, yb,
             merged, y)
    return x_out, saved, exchanged


def _mixer_bwd(dxo, saved, mod, g_pre, g_post, w, ssm, tag, swap=None, after_swap=None):
    (x, h, u, u_i, qkv, f, gab, s_re, s_im, y_ssm, gl, z, fb, cum_cols, cum_rows, attn, attn32, lse, yb,
     merged, y) = saved
    L = x.shape[0]
    sc, gate = _row(mod[1]), _row(mod[2])
    lam_re, lam_im, bin_re, bin_im, cout_re, cout_im, dskip = ssm
    dy, dgate, dgpost = _postnorm_bwd(dxo, y, _row(g_post), gate, 1.0, name=f"postnorm_bwd_{tag}")
    dmerged = _mm(dy, w["out"], tb=True, name=f"dmerged_{tag}")
    dw_out = _mm(merged, dy, ta=True, name=f"dw_mix_out_{tag}", tm=1024, tn=1024)
    dz, dyb, dgab, *swapped = _merge_bwd(dmerged, z, yb, gab, name=f"merge_bwd_{tag}", exchange=swap)
    exchange = after_swap(swapped) if after_swap is not None else None
    dgl = _mm(dz, w["glu"], tb=True, name=f"dgl_{tag}", tk=2048)
    dw_glu = _mm(gl, dz, ta=True, name=f"dw_glu_{tag}", tm=512, tn=2048)
    dys, dsk, dd = _gelu_bwd(dgl, y_ssm, u, dskip, name=f"gelu_bwd_{tag}")
    dys, dsk = _interleave(dys), _interleave(dsk)
    d_re, d_im = _mm_bd([dys], [cout_re, cout_im], tb=True, name=f"ssm_ds_{tag}")
    dcout_re = _mm_bd_t(s_re, dys, SSM_BLOCKS, name=f"ssm_dc_re_{tag}")
    dcout_im = _mm_bd_t(s_im, dys, SSM_BLOCKS, name=f"ssm_dc_im_{tag}")
    g_re, g_im, dlam_re, dlam_im = _ssm_scan_bwd(d_re, d_im, s_re, s_im, lam_re, lam_im, name=f"ssm_scan_bwd_{tag}")
    du = _mm_bd([g_re, g_im], [bin_re, bin_im], tb=True, bias=dsk, out_dtype=BF16, name=f"ssm_du_{tag}")
    du = _deinterleave(du)
    dbin_re = _mm_bd_t(u_i, g_re, SSM_BLOCKS, name=f"ssm_db_re_{tag}")
    dbin_im = _mm_bd_t(u_i, g_im, SSM_BLOCKS, name=f"ssm_db_im_{tag}")
    dssm = (dlam_re, dlam_im, dbin_re, dbin_im, dcout_re, dcout_im, _row(dd[0]))
    dattn = _mm(dyb, w["attn_out"], tb=True, out_dtype=BF16, name=f"dattn_{tag}")
    dw_attn = _mm(attn, dyb, ta=True, name=f"dw_attn_out_{tag}", tm=512, tn=1024)
    stats = _fox_rowstats(dattn, attn32, lse, cum_cols, name=f"fox_rowstats_{tag}")
    dk, dv, dcum_rows, dq, drow, *exchanged = _fox_bwd(qkv, dattn, stats, cum_rows, name=f"fox_bwd_{tag}",
                                                       exchange=exchange)
    drow8 = drow.reshape(L, ATTN_HEADS // 2, LANES)[:, :, :2].reshape(L, ATTN_HEADS)
    dcum = drow8 + dcum_rows[:, :2, :].reshape(ATTN_HEADS, L).T
    dcum = jnp.pad(dcum, ((0, 0), (0, LANES - ATTN_HEADS)))
    df, dfb = _fox_cum_bwd(dcum, f, fb, name=f"fox_cum_bwd_{tag}")
    dqkv = jnp.concatenate([dq.astype(BF16), dk, dv], axis=1)
    dh = _mm(dqkv, w["qkv"], tb=True, name=f"dh_qkv_{tag}", tk=1536)
    dh = _mm(du, w["u"], tb=True, bias=dh, name=f"dh_u_{tag}")
    dh = _mm(dgab, w["gab"], tb=True, bias=dh, name=f"dh_gab_{tag}", tk=2048)
    dh = _mm(df, w["f"], tb=True, bias=dh, name=f"dh_f_{tag}")
    dw_u = _mm(h, du, ta=True, name=f"dw_u_{tag}", tm=1024, tn=512)
    dw_qkv = _mm(h, dqkv, ta=True, name=f"dw_qkv_{tag}", tm=1024, tn=1536)
    dw_f = _mm(h, df, ta=True, name=f"dw_f_{tag}", tm=1024)
    dw_gab = _mm(h, dgab, ta=True, name=f"dw_gab_{tag}", tm=1024, tn=1024)
    dw_in = jnp.concatenate([dw_u, dw_qkv, dw_f[:, :ATTN_HEADS], dw_gab], axis=1)
    dx, dsh, dsc, dgpre = _prenorm_bwd(x, dh, dxo, _row(g_pre), sc, name=f"prenorm_bwd_{tag}")
    dmod = jnp.stack([dsh[0], dsc[0], dgate[0]])
    grads = dict(mix_w_in=dw_in, glu_w=dw_glu, attn_w_out=dw_attn, mix_w_out=dw_out,
                 forget_b=dfb[0, :ATTN_HEADS])
    return dx, dmod, dgpre[0], dgpost[0], grads, dssm, exchanged


SSM_NAMES = ("ssm_a_re", "ssm_a_im", "ssm_log_dt", "ssm_b_re", "ssm_b_im", "ssm_c_re", "ssm_c_im", "ssm_d")
SMALL_NAMES = ("forget_b",) + SSM_NAMES
WEIGHT_NAMES = ("mod_w", "mod_b", "norm_pre", "norm_post", "ffn_w_in", "ffn_w_out", "mix_w_in", "forget_b") \
    + SSM_NAMES + ("glu_w", "attn_w_out", "mix_w_out")


def _layer_shards(w, l):
    return [(w[n][l] if j is None else w[n][l, j]).astype(BF16).reshape(_part_shape(CUTS[n], slab=True))
            for n, j in LAYER_MATS]


def _train_step(x, c, target, w, m, v):
    xi, yi, ci = _my_place()
    chip = 2 * xi + yi
    dev = 4 * xi + 2 * yi + ci
    mod_cols = N_SUB * 3 * D_MODEL // N_CHIPS
    norm_cols = D_MODEL // N_CHIPS

    cuts = [CUTS[n] for n, _ in LAYER_MATS]
    shards = [_layer_shards(w, l) for l in range(DEPTH)]
    full = [[None] * len(LAYER_MATS) for _ in range(DEPTH)]

    def gather(keys):
        return _gather_exchange([shards[l][i] for l, i in keys], [cuts[i] for _, i in keys])

    def store(keys, mats):
        for (l, i), a in zip(keys, mats):
            full[l][i] = a

    first = lambda l: [(l, i) for i in FIRST_MATS]
    mixer = lambda l: [(l, i) for i in MIXER_MATS]
    late = lambda l: [(l, i) for i in LATE_MATS]
    store(first(0), _run_exchange(gather(first(0)), name="gather_weights_first"))
    c_all = _all_gather8(jnp.pad(c, ((0, SUBLANES - 1), (0, 0))), name="gather_c")[::SUBLANES]
    mod_b_cols = lax.dynamic_slice_in_dim(w["mod_b"], chip * mod_cols, mod_cols, axis=1)[:, None, :]
    mod_part = _mod_partial(c_all, w["mod_w"], mod_b_cols, name="mod_partial")
    small_fwd = jnp.concatenate([mod_part.reshape(-1), w["norm_pre"].reshape(-1), w["norm_post"].reshape(-1)])
    n_mod, n_norm = mod_part.size, w["norm_pre"].size
    sf_all = _all_gather8(_pad_rows(small_fwd), name="gather_mod").reshape(N_DEV, -1)
    sf_chips = sf_all[::2]
    mod_all = jnp.concatenate(
        [sf_chips[k, :n_mod].reshape(DEPTH, N_DEV, mod_cols) for k in range(N_CHIPS)], axis=2)
    mod_mine = lax.dynamic_index_in_dim(mod_all, dev, axis=1, keepdims=False).reshape(DEPTH, N_SUB, 3, D_MODEL)
    norm_pre = jnp.concatenate(
        [sf_chips[k, n_mod:n_mod + n_norm].reshape(DEPTH, N_SUB, norm_cols) for k in range(N_CHIPS)], axis=2)
    norm_post = jnp.concatenate(
        [sf_chips[k, n_mod + n_norm:n_mod + 2 * n_norm].reshape(DEPTH, N_SUB, norm_cols) for k in range(N_CHIPS)],
        axis=2)

    saved, layer_w, ssm_prep, ssm_vjp = [], [], [], []
    h = x
    for l in range(DEPTH):
        lw = dict(ffn=[(full[l][0][0], full[l][2][0]), None])
        prep, vjp = jax.vjp(_ssm_discretize, *[w[n][l] for n in SSM_NAMES])
        layer_w.append(lw)
        ssm_prep.append(prep)
        ssm_vjp.append(vjp)
        h, s0, arrived = _ffn_fwd(h, mod_mine[l, 0], norm_pre[l, 0], norm_post[l, 0], *lw["ffn"][0], tag=f"l{l}a",
                                  exchange=gather(mixer(l)))
        store(mixer(l), arrived)
        mix_in, glu, attn_out, mix_out = full[l][4:]
        w_u, w_qkv, w_f, w_gab = _split_mix_w_in(mix_in.transpose(1, 0, 2).reshape(D_MODEL, IN_WIDTH))
        lw["mix"] = dict(u=w_u, qkv=w_qkv, f=w_f, gab=w_gab, glu=glu[0], attn_out=attn_out[0], out=mix_out[0])
        coming = late(l) + (first(l + 1) if l + 1 < DEPTH else [])
        h, s1, arrived = _mixer_fwd(h, mod_mine[l, 1], norm_pre[l, 1], norm_post[l, 1], lw["mix"], prep,
                                    w["forget_b"][l], tag=f"l{l}m", exchange=gather(coming))
        store(coming, arrived)
        lw["ffn"][1] = (full[l][1][0], full[l][3][0])
        h, s2, _ = _ffn_fwd(h, mod_mine[l, 2], norm_pre[l, 2], norm_post[l, 2], *lw["ffn"][1], tag=f"l{l}b")
        saved.append((s0, s1, s2))
    dh, loss8 = _loss_head(h, target, name="loss_head")

    c_idx = ci.reshape(1).astype(jnp.int32)
    chip_idx = chip.reshape(1).astype(jnp.int32)
    places = []
    dests = {n: lax.empty(((DEPTH,) if j is None else (DEPTH, 2)) + _part_shape(CUTS[n], slab=True), F32)
             for n, j in LAYER_MATS}
    g_small = {n: [None] * DEPTH for n in SMALL_NAMES}
    dmod, dnpre, dnpost = [], [], []
    pending = []

    def swap_of(l, idx, mats):
        cs = [cuts[i] for i in idx]
        mats = [a.reshape(cut.shape) for a, cut in zip(mats, cs)]

        def added(from_sibling):
            for k, i in enumerate(idx):
                f, b = _add_half(mats[k], from_sibling[k], cs[k], c_idx, name=f"grad_add_l{l}_{i}")
                pending.append((l, i, f, b))

        return _swap_exchange(mats, cs), added

    def partials():
        return _partials_exchange([p[3] for p in pending], [cuts[p[1]] for p in pending])

    def then_partials(added):
        def after_swap(from_sibling):
            added(from_sibling)
            return partials()
        return after_swap

    def end_reduce(parts):
        for (l, i, f, _), part in zip(pending, parts):
            n, j = LAYER_MATS[i]
            place = (l,) if j is None else (l, j)
            dests[n] = _sum_slab(f, part, cuts[i], dests[n], place, chip_idx, c_idx, name=f"grad_sum_l{l}_{i}")
            places.append((n, place))
        pending.clear()

    swap, added = None, None
    for l in reversed(range(DEPTH)):
        lw = layer_w[l]
        dh, dm2, dp2, dq2, dwin2, dwout2, _ = _ffn_bwd(
            dh, saved[l][2], mod_mine[l, 2], norm_pre[l, 2], norm_post[l, 2], *lw["ffn"][1], tag=f"l{l}b",
            swap=swap, after_swap=added)
        swap, added = swap_of(l, LATE_MATS, [dwin2, dwout2])
        dh, dm1, dp1, dq1, gmix, dssm, parts = _mixer_bwd(
            dh, saved[l][1], mod_mine[l, 1], norm_pre[l, 1], norm_post[l, 1], lw["mix"], ssm_prep[l],
            tag=f"l{l}m", swap=swap, after_swap=then_partials(added))
        end_reduce(parts)
        dmix_in = gmix["mix_w_in"].reshape(D_MODEL, N_CHIPS, IN_WIDTH // N_CHIPS).transpose(1, 0, 2)
        swap, added = swap_of(l, MIXER_MATS, [dmix_in, gmix["glu_w"], gmix["attn_w_out"], gmix["mix_w_out"]])
        dh, dm0, dp0, dq0, dwin0, dwout0, parts = _ffn_bwd(
            dh, saved[l][0], mod_mine[l, 0], norm_pre[l, 0], norm_post[l, 0], *lw["ffn"][0], tag=f"l{l}a",
            swap=swap, after_swap=then_partials(added))
        end_reduce(parts)
        swap, added = swap_of(l, FIRST_MATS, [dwin0, dwout0])
        dmod.insert(0, jnp.stack([dm0, dm1, dm2]))
        dnpre.insert(0, jnp.stack([dp0, dp1, dp2]))
        dnpost.insert(0, jnp.stack([dq0, dq1, dq2]))
        g_small["forget_b"][l] = gmix["forget_b"]
        for n, g in zip(SSM_NAMES, ssm_vjp[l](dssm)):
            g_small[n][l] = g
    added(_run_exchange(swap, name="grad_swap_last"))
    end_reduce(_run_exchange(partials(), name="grad_partials_last"))
    grad_x = dh
    g_small = {n: jnp.stack(g) for n, g in g_small.items()}

    small = [loss8[0, :1], jnp.stack(dmod).reshape(-1), jnp.stack(dnpre).reshape(-1), jnp.stack(dnpost).reshape(-1)]
    small += [g_small[n].reshape(-1) for n in SMALL_NAMES]
    sizes = [int(s.size) for s in small]
    offs = np.concatenate([[0], np.cumsum(sizes)])
    sb_all, sb_sum = _all_gather8(_pad_rows(jnp.concatenate(small)), name="gather_small_grads", with_sum=True)
    sb_sum = sb_sum.reshape(-1)
    take = lambda i: sb_sum[int(offs[i]):int(offs[i + 1])]
    loss = take(0)[0]
    grads = {"mod_b": take(1).reshape(DEPTH, N_SUB * 3 * D_MODEL)}
    dnorm_pre_full = take(2).reshape(DEPTH, N_SUB, D_MODEL)
    dnorm_post_full = take(3).reshape(DEPTH, N_SUB, D_MODEL)
    grads["norm_pre"] = lax.dynamic_slice_in_dim(dnorm_pre_full, chip * norm_cols, norm_cols, axis=2)
    grads["norm_post"] = lax.dynamic_slice_in_dim(dnorm_post_full, chip * norm_cols, norm_cols, axis=2)
    for i, n in enumerate(SMALL_NAMES):
        grads[n] = take(4 + i).reshape(w[n].shape)
    dmod_all = sb_all.reshape(N_DEV, -1)[:, int(offs[1]):int(offs[2])].reshape(N_DEV, DEPTH, N_SUB * 3 * D_MODEL)
    dmod_cols = lax.dynamic_slice_in_dim(dmod_all, chip * mod_cols, mod_cols, axis=2).transpose(1, 0, 2)
    grads["mod_w"] = _mod_wgrad(c_all.T, dmod_cols, name="mod_wgrad")

    shared = _share_all(dests, [CUTS[n] for n, _ in places], places, name="grad_share")
    for n, g in zip(dests, shared):
        grads[n] = g.reshape(w[n].shape)

    delta, new_m, new_v = {}, {}, {}
    for n in WEIGHT_NAMES:
        delta[n], new_m[n], new_v[n] = _adamw(w[n], grads[n], m[n], v[n], name=f"adamw_{n}")
    outs = [loss, grad_x[None]]
    for group in (grads, delta, new_m, new_v):
        outs += [group[n] for n in WEIGHT_NAMES]
    return tuple(outs)


def kernel(x, c, mod_w, mod_b, norm_pre, norm_post, ffn_w_in, ffn_w_out, mix_w_in, forget_b, ssm_a_re, ssm_a_im, ssm_log_dt, ssm_b_re, ssm_b_im, ssm_c_re, ssm_c_im, ssm_d, glu_w, attn_w_out, mix_w_out, loss_target, m_mod_w, m_mod_b, m_norm_pre, m_norm_post, m_ffn_w_in, m_ffn_w_out, m_mix_w_in, m_forget_b, m_ssm_a_re, m_ssm_a_im, m_ssm_log_dt, m_ssm_b_re, m_ssm_b_im, m_ssm_c_re, m_ssm_c_im, m_ssm_d, m_glu_w, m_attn_w_out, m_mix_w_out, v_mod_w, v_mod_b, v_norm_pre, v_norm_post, v_ffn_w_in, v_ffn_w_out, v_mix_w_in, v_forget_b, v_ssm_a_re, v_ssm_a_im, v_ssm_log_dt, v_ssm_b_re, v_ssm_b_im, v_ssm_c_re, v_ssm_c_im, v_ssm_d, v_glu_w, v_attn_w_out, v_mix_w_out):
    w = dict(mod_w=mod_w, mod_b=mod_b, norm_pre=norm_pre, norm_post=norm_post, ffn_w_in=ffn_w_in,
             ffn_w_out=ffn_w_out, mix_w_in=mix_w_in, forget_b=forget_b, ssm_a_re=ssm_a_re, ssm_a_im=ssm_a_im,
             ssm_log_dt=ssm_log_dt, ssm_b_re=ssm_b_re, ssm_b_im=ssm_b_im, ssm_c_re=ssm_c_re, ssm_c_im=ssm_c_im,
             ssm_d=ssm_d, glu_w=glu_w, attn_w_out=attn_w_out, mix_w_out=mix_w_out)
    m = dict(mod_w=m_mod_w, mod_b=m_mod_b, norm_pre=m_norm_pre, norm_post=m_norm_post, ffn_w_in=m_ffn_w_in,
             ffn_w_out=m_ffn_w_out, mix_w_in=m_mix_w_in, forget_b=m_forget_b, ssm_a_re=m_ssm_a_re,
             ssm_a_im=m_ssm_a_im, ssm_log_dt=m_ssm_log_dt, ssm_b_re=m_ssm_b_re, ssm_b_im=m_ssm_b_im,
             ssm_c_re=m_ssm_c_re, ssm_c_im=m_ssm_c_im, ssm_d=m_ssm_d, glu_w=m_glu_w, attn_w_out=m_attn_w_out,
             mix_w_out=m_mix_w_out)
    v = dict(mod_w=v_mod_w, mod_b=v_mod_b, norm_pre=v_norm_pre, norm_post=v_norm_post, ffn_w_in=v_ffn_w_in,
             ffn_w_out=v_ffn_w_out, mix_w_in=v_mix_w_in, forget_b=v_forget_b, ssm_a_re=v_ssm_a_re,
             ssm_a_im=v_ssm_a_im, ssm_log_dt=v_ssm_log_dt, ssm_b_re=v_ssm_b_re, ssm_b_im=v_ssm_b_im,
             ssm_c_re=v_ssm_c_re, ssm_c_im=v_ssm_c_im, ssm_d=v_ssm_d, glu_w=v_glu_w, attn_w_out=v_attn_w_out,
             mix_w_out=v_mix_w_out)
    return _train_step(x[0], c, loss_target[0], w, m, v)
```

```python
import functools
import math
from typing import NamedTuple

import jax
import jax.numpy as jnp
import numpy as np
from jax import lax
from jax.experimental import pallas as pl
from jax.experimental.pallas import tpu as pltpu

F32 = jnp.float32
BF16 = jnp.bfloat16

D_MODEL = 1024
DEPTH = 2
SSM_WIDTH = 512
SSM_GROUP = 16
SSM_GROUPS = 32
SSM_STATE = 64
SSM_FLAT = SSM_GROUPS * SSM_STATE
SSM_BLOCKS = 4
ATTN_HEADS = 8
HEAD_DIM = 64
ATTN_WIDTH = 512
D_FF = 2816
FFN_RES = 0.5
N_SUB = 3
RMS_EPS = 1e-6
N_CHIPS = 4
N_DEV = 8

ADAM_LR = 0.001
ADAM_B1 = 0.9
ADAM_B2 = 0.999
ADAM_EPS = 1e-08
ADAM_WD = 0.01
ADAM_STEP = 10

LANES = 128
SUBLANES = 8
VMEM_LIMIT = 52 * 1024 * 1024
MESH = pl.DeviceIdType.MESH

NN = (((1,), (0,)), ((), ()))
NT = (((1,), (1,)), ((), ()))
TN = (((0,), (0,)), ((), ()))


def _tile(dim, target, align=LANES):
    best = None
    t = align
    while t <= min(dim, target):
        if dim % t == 0:
            best = t
        t += align
    return dim if best is None else best


def _params(*sem):
    return pltpu.CompilerParams(dimension_semantics=sem, vmem_limit_bytes=VMEM_LIMIT)


def _mm(a, b, *, name, ta=False, tb=False, out_dtype=F32, bias=None, bscale=None, b_k0=0,
        out_n=None, out_j0=0, into=None, tm=512, tn=1024, tk=1024):
    M, K = (a.shape[1], a.shape[0]) if ta else a.shape
    N = b.shape[0] if tb else b.shape[1]
    assert b_k0 + K <= (b.shape[1] if tb else b.shape[0]), (a.shape, b.shape, ta, tb)
    tm, tn, tk = _tile(M, tm), _tile(N, tn), _tile(K, tk)
    nk = K // tk
    assert b_k0 % tk == 0
    kb0 = b_k0 // tk
    dn = (((0 if ta else 1,), (1 if tb else 0,)), ((), ()))
    has_bias, has_scale = bias is not None, bscale is not None

    def body(*refs):
        a_ref, b_ref = refs[0], refs[1]
        pos = 2
        bias_ref = scale_ref = None
        if has_bias:
            bias_ref = refs[pos]
            pos += 1
        if has_scale:
            scale_ref = refs[pos]
            pos += 1
        if into is not None:
            pos += 1
        o_ref = refs[pos]
        acc_ref = refs[pos + 1] if nk > 1 else None

        def finish(r):
            if has_bias:
                extra = bias_ref[...].astype(F32)
                if has_scale:
                    extra = extra * scale_ref[...]
                r = r + extra
            o_ref[...] = r.astype(out_dtype)

        part = lax.dot_general(a_ref[...].astype(BF16), b_ref[...].astype(BF16), dn,
                               preferred_element_type=F32)
        if nk == 1:
            finish(part)
        else:
            k = pl.program_id(2)

            @pl.when(k == 0)
            def _():
                acc_ref[...] = part

            @pl.when(k > 0)
            def _():
                acc_ref[...] += part

            @pl.when(k == nk - 1)
            def _():
                finish(acc_ref[...])

    a_spec = pl.BlockSpec((tk, tm), lambda j, i, k: (k, i)) if ta else pl.BlockSpec((tm, tk), lambda j, i, k: (i, k))
    b_spec = (pl.BlockSpec((tn, tk), lambda j, i, k: (j, kb0 + k)) if tb
              else pl.BlockSpec((tk, tn), lambda j, i, k: (kb0 + k, j)))
    in_specs = [a_spec, b_spec]
    args = [a, b]
    if has_bias:
        in_specs.append(pl.BlockSpec((tm, tn), lambda j, i, k: (i, j)))
        args.append(bias)
    if has_scale:
        in_specs.append(pl.BlockSpec((1, tn), lambda j, i, k: (0, j)))
        args.append(bscale)
    aliases = {}
    if into is not None:
        in_specs.append(pl.BlockSpec(memory_space=pl.ANY))
        args.append(into)
        aliases = {len(args) - 1: 0}
    out_n = N if out_n is None else out_n
    assert out_j0 % tn == 0
    jb0 = out_j0 // tn
    return pl.pallas_call(
        body, name=name,
        grid=(N // tn, M // tm, nk),
        in_specs=in_specs,
        out_specs=pl.BlockSpec((tm, tn), lambda j, i, k: (i, jb0 + j)),
        out_shape=jax.ShapeDtypeStruct((M, out_n), out_dtype),
        scratch_shapes=[pltpu.VMEM((tm, tn), F32)] if nk > 1 else [],
        input_output_aliases=aliases,
        compiler_params=_params("parallel", "parallel", "arbitrary"),
    )(*args)


def _mm_bd(a_list, b_list, *, name, tb=False, out_dtype=F32, bias=None, bscale=None, tm=1024):
    G = b_list[0].shape[0]
    Kb, Nb = (b_list[0].shape[2], b_list[0].shape[1]) if tb else b_list[0].shape[1:]
    M = a_list[0].shape[0]
    tm = _tile(M, tm)
    na, nb = len(a_list), len(b_list)
    n_out = nb if na == 1 else 1
    dn = NT if tb else NN
    has_bias, has_scale = bias is not None, bscale is not None

    def body(*refs):
        a_refs, b_refs = refs[:na], refs[na:na + nb]
        pos = na + nb
        bias_ref = scale_ref = None
        if has_bias:
            bias_ref = refs[pos]
            pos += 1
        if has_scale:
            scale_ref = refs[pos]
            pos += 1
        o_refs = refs[pos:]
        prods = [lax.dot_general(a_refs[min(i, na - 1)][...].astype(BF16), b_refs[i][...].astype(BF16), dn,
                                 preferred_element_type=F32) for i in range(nb)]
        outs = prods if n_out == nb else [functools.reduce(jnp.add, prods)]
        for o_ref, r in zip(o_refs, outs):
            if has_bias:
                extra = bias_ref[...].astype(F32)
                r = r + (extra * scale_ref[...] if has_scale else extra)
            o_ref[...] = r.astype(out_dtype)

    a_spec = pl.BlockSpec((tm, Kb), lambda g, i: (i, g))
    b_spec = pl.BlockSpec((None,) + b_list[0].shape[1:], lambda g, i: (g, 0, 0))
    o_spec = pl.BlockSpec((tm, Nb), lambda g, i: (i, g))
    in_specs = [a_spec] * na + [b_spec] * nb
    args = list(a_list) + list(b_list)
    if has_bias:
        in_specs.append(o_spec)
        args.append(bias)
    if has_scale:
        in_specs.append(pl.BlockSpec((1, Nb), lambda g, i: (0, g)))
        args.append(bscale)
    sds = jax.ShapeDtypeStruct((M, G * Nb), out_dtype)
    res = pl.pallas_call(
        body, name=name, grid=(G, M // tm), in_specs=in_specs, out_specs=[o_spec] * n_out,
        out_shape=[sds] * n_out, compiler_params=_params("parallel", "parallel"),
    )(*args)
    return res[0] if n_out == 1 else res


def _mm_bd_t(a, b, G, *, name, tk=1024):
    K, Mb, Nb = a.shape[0], a.shape[1] // G, b.shape[1] // G
    tk = _tile(K, tk)
    nk = K // tk

    def body(a_ref, b_ref, o_ref, acc_ref):
        k = pl.program_id(1)
        part = lax.dot_general(a_ref[...].astype(BF16), b_ref[...].astype(BF16), TN, preferred_element_type=F32)

        @pl.when(k == 0)
        def _():
            acc_ref[...] = part

        @pl.when(k > 0)
        def _():
            acc_ref[...] += part

        @pl.when(k == nk - 1)
        def _():
            o_ref[...] = acc_ref[...]

    return pl.pallas_call(
        body, name=name, grid=(G, nk),
        in_specs=[pl.BlockSpec((tk, Mb), lambda g, k: (k, g)), pl.BlockSpec((tk, Nb), lambda g, k: (k, g))],
        out_specs=pl.BlockSpec((None, Mb, Nb), lambda g, k: (g, 0, 0)),
        out_shape=jax.ShapeDtypeStruct((G, Mb, Nb), F32),
        scratch_shapes=[pltpu.VMEM((Mb, Nb), F32)], compiler_params=_params("parallel", "arbitrary"),
    )(a, b)


def _sigmoid(x):
    return 0.5 * jnp.tanh(0.5 * x) + 0.5


def _mm_swiglu(h, w_in, *, name, tm=512, tn=1408, exchange=None):
    M, K = h.shape
    N = w_in.shape[1] // 2
    tm, tn = _tile(M, tm), _tile(N, tn)
    nj = N // tn

    def body(h_ref, wg_ref, wu_ref, g_ref, u_ref, a_ref):
        hv = h_ref[...]
        g = jnp.dot(hv, wg_ref[...], preferred_element_type=F32)
        u = jnp.dot(hv, wu_ref[...], preferred_element_type=F32)
        g_ref[...] = g.astype(BF16)
        u_ref[...] = u.astype(BF16)
        a_ref[...] = (g * _sigmoid(g) * u).astype(BF16)

    o_spec = pl.BlockSpec((tm, tn), lambda j, i: (i, j))
    sds = jax.ShapeDtypeStruct((M, N), BF16)
    return _call_with_exchange(
        body, exchange, name=name, grid=(nj, M // tm),
        in_specs=[pl.BlockSpec((tm, K), lambda j, i: (i, 0)), pl.BlockSpec((K, tn), lambda j, i: (0, j)),
                  pl.BlockSpec((K, tn), lambda j, i: (0, nj + j))],
        out_specs=[o_spec, o_spec, o_spec], out_shape=[sds, sds, sds], operands=(h, w_in, w_in))


def _mm_swiglu_bwd(dy, w_out, gate, up, *, name, tm=512, tn=1408, exchange=None):
    M, K = dy.shape
    N = w_out.shape[0]
    tm, tn = _tile(M, tm), _tile(N, tn)

    def body(dy_ref, w_ref, g_ref, u_ref, dg_ref, du_ref):
        dact = lax.dot_general(dy_ref[...], w_ref[...], NT, preferred_element_type=F32)
        g = g_ref[...].astype(F32)
        u = u_ref[...].astype(F32)
        sig = _sigmoid(g)
        dg_ref[...] = (dact * u * (sig * (1.0 + g * (1.0 - sig)))).astype(BF16)
        du_ref[...] = (dact * (g * sig)).astype(BF16)

    t_spec = pl.BlockSpec((tm, tn), lambda j, i: (i, j))
    sds = jax.ShapeDtypeStruct((M, N), BF16)
    return _call_with_exchange(
        body, exchange, name=name, grid=(N // tn, M // tm),
        in_specs=[pl.BlockSpec((tm, K), lambda j, i: (i, 0)), pl.BlockSpec((tn, K), lambda j, i: (j, 0)),
                  t_spec, t_spec],
        out_specs=[t_spec, t_spec], out_shape=[sds, sds], operands=(dy, w_out, gate, up))


ROW_TILE = 1024
MERGE_TILE = 512
GELU_TILE = 2048


def _colsum8(v):
    return jnp.sum(v.reshape(v.shape[0] // SUBLANES, SUBLANES, v.shape[1]), axis=0)


def _finish_colsums(step, last, refs):
    @pl.when(step == last)
    def _():
        for r in refs:
            r[...] = jnp.broadcast_to(jnp.sum(r[...], axis=0, keepdims=True), r.shape)


def _row_spec(t, d):
    return pl.BlockSpec((t, d), lambda i: (i, 0))


def _vec_spec(d, rows=1):
    return pl.BlockSpec((rows, d), lambda i: (0, 0))


def _prenorm(x, g, sc, sh, *, name):
    L, D = x.shape
    t = _tile(L, ROW_TILE, SUBLANES)

    def body(x_ref, g_ref, sc_ref, sh_ref, h_ref):
        xv = x_ref[...]
        r = lax.rsqrt(jnp.mean(xv * xv, axis=-1, keepdims=True) + RMS_EPS)
        h_ref[...] = (((xv * r) * g_ref[...]) * (1.0 + sc_ref[...]) + sh_ref[...]).astype(BF16)

    return pl.pallas_call(
        body, name=name, grid=(L // t,),
        in_specs=[_row_spec(t, D), _vec_spec(D), _vec_spec(D), _vec_spec(D)],
        out_specs=_row_spec(t, D), out_shape=jax.ShapeDtypeStruct((L, D), BF16),
        compiler_params=_params("parallel"),
    )(x, g, sc, sh)


def _mm_postnorm(a, b, x, g, gate, res_w, *, name, tm=512):
    M, K = a.shape
    D = b.shape[1]
    tm = _tile(M, tm, SUBLANES)

    def body(a_ref, b_ref, x_ref, g_ref, gate_ref, y_ref, o_ref):
        yv = jnp.dot(a_ref[...], b_ref[...], preferred_element_type=F32)
        y_ref[...] = yv
        r = lax.rsqrt(jnp.mean(yv * yv, axis=-1, keepdims=True) + RMS_EPS)
        o_ref[...] = x_ref[...] + (res_w * gate_ref[...]) * ((yv * r) * g_ref[...])

    sds = jax.ShapeDtypeStruct((M, D), F32)
    return pl.pallas_call(
        body, name=name, grid=(M // tm,),
        in_specs=[_row_spec(tm, K), pl.BlockSpec((K, D), lambda i: (0, 0)), _row_spec(tm, D), _vec_spec(D),
                  _vec_spec(D)],
        out_specs=[_row_spec(tm, D), _row_spec(tm, D)], out_shape=[sds, sds],
        compiler_params=_params("parallel"),
    )(a, b, x, g, gate)


def _postnorm_bwd(dxo, y, g, gate, res_w, *, name, exchange=None):
    L, D = y.shape
    t = _tile(L, ROW_TILE, SUBLANES)
    n = L // t

    def body(dxo_ref, y_ref, g_ref, gate_ref, dy_ref, dgate_ref, dg_ref):
        i = pl.program_id(0)

        @pl.when(i == 0)
        def _():
            dgate_ref[...] = jnp.zeros_like(dgate_ref)
            dg_ref[...] = jnp.zeros_like(dg_ref)

        yv = y_ref[...]
        dv = dxo_ref[...]
        gv = g_ref[...]
        r = lax.rsqrt(jnp.mean(yv * yv, axis=-1, keepdims=True) + RMS_EPS)
        yn = yv * r
        dgate_ref[...] += _colsum8(dv * (res_w * (yn * gv)))
        do = dv * (res_w * gate_ref[...])
        dg_ref[...] += _colsum8(do * yn)
        dyn = do * gv
        dy_ref[...] = (r * (dyn - yn * jnp.mean(dyn * yn, axis=-1, keepdims=True))).astype(BF16)
        _finish_colsums(i, n - 1, (dgate_ref, dg_ref))

    sum_sds = jax.ShapeDtypeStruct((SUBLANES, D), F32)
    return _call_with_exchange(
        body, exchange, name=name, grid=(n,), sem=("arbitrary",),
        in_specs=[_row_spec(t, D), _row_spec(t, D), _vec_spec(D), _vec_spec(D)],
        out_specs=[_row_spec(t, D), _vec_spec(D, SUBLANES), _vec_spec(D, SUBLANES)],
        out_shape=[jax.ShapeDtypeStruct((L, D), BF16), sum_sds, sum_sds], operands=(dxo, y, g, gate))


def _prenorm_bwd(x, dh, dxres, g, sc, *, name):
    L, D = x.shape
    t = _tile(L, ROW_TILE, SUBLANES)
    n = L // t

    def body(x_ref, dh_ref, dxr_ref, g_ref, sc_ref, dx_ref, dsh_ref, dsc_ref, dg_ref):
        i = pl.program_id(0)

        @pl.when(i == 0)
        def _():
            dsh_ref[...] = jnp.zeros_like(dsh_ref)
            dsc_ref[...] = jnp.zeros_like(dsc_ref)
            dg_ref[...] = jnp.zeros_like(dg_ref)

        xv = x_ref[...]
        dhv = dh_ref[...].astype(F32)
        gv = g_ref[...]
        one_sc = 1.0 + sc_ref[...]
        r = lax.rsqrt(jnp.mean(xv * xv, axis=-1, keepdims=True) + RMS_EPS)
        xn = xv * r
        tt = dhv * xn
        dsh_ref[...] += _colsum8(dhv)
        dsc_ref[...] += _colsum8(tt * gv)
        dg_ref[...] += _colsum8(tt * one_sc)
        dxn = dhv * (gv * one_sc)
        dx_ref[...] = dxr_ref[...] + r * (dxn - xn * jnp.mean(dxn * xn, axis=-1, keepdims=True))
        _finish_colsums(i, n - 1, (dsh_ref, dsc_ref, dg_ref))

    sum_sds = jax.ShapeDtypeStruct((SUBLANES, D), F32)
    sum_spec = _vec_spec(D, SUBLANES)
    return pl.pallas_call(
        body, name=name, grid=(n,),
        in_specs=[_row_spec(t, D), _row_spec(t, D), _row_spec(t, D), _vec_spec(D), _vec_spec(D)],
        out_specs=[_row_spec(t, D), sum_spec, sum_spec, sum_spec],
        out_shape=[jax.ShapeDtypeStruct((L, D), F32), sum_sds, sum_sds, sum_sds],
        compiler_params=_params("arbitrary"),
    )(x, dh, dxres, g, sc)


def _loss_head(y, target, *, name):
    L, D = y.shape
    t = _tile(L, ROW_TILE, SUBLANES)
    n = L // t

    def body(y_ref, t_ref, dy_ref, loss_ref):
        i = pl.program_id(0)

        @pl.when(i == 0)
        def _():
            loss_ref[...] = jnp.zeros_like(loss_ref)

        e = y_ref[...] - t_ref[...]
        dy_ref[...] = e * (1.0 / D)
        part = jnp.sum(jnp.mean(e * e, axis=-1, keepdims=True), axis=0, keepdims=True)
        loss_ref[...] += jnp.broadcast_to(0.5 * part, loss_ref.shape)

    return pl.pallas_call(
        body, name=name, grid=(n,),
        in_specs=[_row_spec(t, D), _row_spec(t, D)],
        out_specs=[_row_spec(t, D), pl.BlockSpec((SUBLANES, LANES), lambda i: (0, 0))],
        out_shape=[jax.ShapeDtypeStruct((L, D), F32), jax.ShapeDtypeStruct((SUBLANES, LANES), F32)],
        compiler_params=_params("arbitrary"),
    )(y, target)


GELU_C = math.sqrt(2.0 / math.pi)


def _gelu_fwd(y, *, name):
    L, W = y.shape
    t = _tile(L, GELU_TILE, SUBLANES)

    def body(y_ref, o_ref):
        v = y_ref[...]
        o_ref[...] = (0.5 * v * (1.0 + jnp.tanh(GELU_C * (v + 0.044715 * (v * v * v))))).astype(BF16)

    return pl.pallas_call(
        body, name=name, grid=(L // t,), in_specs=[_row_spec(t, W)], out_specs=_row_spec(t, W),
        out_shape=jax.ShapeDtypeStruct((L, W), BF16), compiler_params=_params("parallel"),
    )(y)


def _gelu_bwd(dgl, y, u, dskip, *, name):
    L, W = y.shape
    t = _tile(L, GELU_TILE, SUBLANES)
    n = L // t

    def body(dgl_ref, y_ref, u_ref, d_ref, dy_ref, sk_ref, dd_ref):
        i = pl.program_id(0)

        @pl.when(i == 0)
        def _():
            dd_ref[...] = jnp.zeros_like(dd_ref)

        v = y_ref[...]
        inner = GELU_C * (v + 0.044715 * (v * v * v))
        th = jnp.tanh(inner)
        dgelu = 0.5 * (1.0 + th) + 0.5 * v * (1.0 - th * th) * (GELU_C * (1.0 + 3.0 * 0.044715 * (v * v)))
        dy = dgl_ref[...] * dgelu
        dy_ref[...] = dy.astype(BF16)
        sk_ref[...] = dy * d_ref[...]
        dd_ref[...] += _colsum8(dy * u_ref[...])
        _finish_colsums(i, n - 1, (dd_ref,))

    return pl.pallas_call(
        body, name=name, grid=(n,),
        in_specs=[_row_spec(t, W), _row_spec(t, W), _row_spec(t, W), _vec_spec(W)],
        out_specs=[_row_spec(t, W), _row_spec(t, W), _vec_spec(W, SUBLANES)],
        out_shape=[jax.ShapeDtypeStruct((L, W), BF16), jax.ShapeDtypeStruct((L, W), F32),
                   jax.ShapeDtypeStruct((SUBLANES, W), F32)],
        compiler_params=_params("arbitrary"),
    )(dgl, y, u, dskip)


def _merge_fwd(z, yb, gab, *, name):
    L, D = yb.shape
    t = _tile(L, MERGE_TILE, SUBLANES)

    def body(z_ref, yb_ref, gab_ref, o_ref):
        ya = z_ref[:, :D] * _sigmoid(z_ref[:, D:])
        o_ref[...] = (_sigmoid(gab_ref[:, :D]) * ya + _sigmoid(gab_ref[:, D:]) * yb_ref[...]).astype(BF16)

    return pl.pallas_call(
        body, name=name, grid=(L // t,),
        in_specs=[_row_spec(t, 2 * D), _row_spec(t, D), _row_spec(t, 2 * D)],
        out_specs=_row_spec(t, D), out_shape=jax.ShapeDtypeStruct((L, D), BF16),
        compiler_params=_params("parallel"),
    )(z, yb, gab)


def _merge_bwd(dm, z, yb, gab, *, name, exchange=None):
    L, D = yb.shape
    t = _tile(L, MERGE_TILE, SUBLANES)

    def body(dm_ref, z_ref, yb_ref, gab_ref, dz_ref, dyb_ref, dgab_ref):
        dmv = dm_ref[...]
        zv = z_ref[:, :D]
        sz = _sigmoid(z_ref[:, D:])
        sa = _sigmoid(gab_ref[:, :D])
        sb = _sigmoid(gab_ref[:, D:])
        ybv = yb_ref[...]
        dya = dmv * sa
        dz_ref[:, :D] = (dya * sz).astype(BF16)
        dz_ref[:, D:] = (dya * zv * (sz * (1.0 - sz))).astype(BF16)
        dyb_ref[...] = (dmv * sb).astype(BF16)
        dgab_ref[:, :D] = (dmv * (zv * sz) * (sa * (1.0 - sa))).astype(BF16)
        dgab_ref[:, D:] = (dmv * ybv * (sb * (1.0 - sb))).astype(BF16)

    return _call_with_exchange(
        body, exchange, name=name, grid=(L // t,),
        in_specs=[_row_spec(t, D), _row_spec(t, 2 * D), _row_spec(t, D), _row_spec(t, 2 * D)],
        out_specs=[_row_spec(t, 2 * D), _row_spec(t, D), _row_spec(t, 2 * D)],
        out_shape=[jax.ShapeDtypeStruct((L, 2 * D), BF16), jax.ShapeDtypeStruct((L, D), BF16),
                   jax.ShapeDtypeStruct((L, 2 * D), BF16)], operands=(dm, z, yb, gab))


SCAN_W = 1024
SCAN_T = 512


def _interleave(x):
    L, W = x.shape
    seg = SCAN_T // SUBLANES
    return x.reshape(L // SCAN_T, SUBLANES, seg, W).transpose(0, 2, 1, 3).reshape(L, W)


def _deinterleave(x):
    L, W = x.shape
    seg = SCAN_T // SUBLANES
    return x.reshape(L // SCAN_T, seg, SUBLANES, W).transpose(0, 2, 1, 3).reshape(L, W)


def _power_table(a, b, pr_tab, pi_tab, n):
    def fill(k, carry):
        pr, pi = carry
        pr_tab[k] = pr
        pi_tab[k] = pi
        return a * pr - b * pi, a * pi + b * pr

    lax.fori_loop(0, n, fill, (a, b))


def _rows_to_tile(rows):
    w = rows[0].shape[1]
    sub = lax.broadcasted_iota(jnp.int32, (SUBLANES, w), 0)
    tile = jnp.broadcast_to(rows[0], (SUBLANES, w))
    for j in range(1, SUBLANES):
        tile = jnp.where(sub == j, jnp.broadcast_to(rows[j], (SUBLANES, w)), tile)
    return tile


def _ssm_scan_fwd(bu_re, bu_im, lam_re, lam_im, *, name):
    L, S = bu_re.shape
    w, t = _tile(S, SCAN_W), SCAN_T
    seg = t // SUBLANES

    def body(br_ref, bi_ref, lr_ref, li_ref, sr_ref, si_ref, pr_tab, pi_tab, cr_ref, ci_ref):
        a = jnp.broadcast_to(lr_ref[...], (SUBLANES, w))
        b = jnp.broadcast_to(li_ref[...], (SUBLANES, w))

        @pl.when(pl.program_id(1) == 0)
        def _():
            cr_ref[...] = jnp.zeros_like(cr_ref)
            ci_ref[...] = jnp.zeros_like(ci_ref)
            _power_table(a, b, pr_tab, pi_tab, seg)

        def local_scan(i, carry):
            sr, si = carry
            rows = pl.ds(pl.multiple_of(i * SUBLANES, SUBLANES), SUBLANES)
            nr = a * sr - b * si + br_ref[rows, :]
            ni = a * si + b * sr + bi_ref[rows, :]
            sr_ref[rows, :] = nr
            si_ref[rows, :] = ni
            return nr, ni

        zero = jnp.zeros((SUBLANES, w), F32)
        fr, fi = lax.fori_loop(0, seg, local_scan, (zero, zero), unroll=2)
        lsr, lsi = pr_tab[seg - 1][0:1, :], pi_tab[seg - 1][0:1, :]
        cr, ci = cr_ref[...], ci_ref[...]
        rows_r, rows_i = [], []
        for j in range(SUBLANES):
            rows_r.append(cr)
            rows_i.append(ci)
            cr, ci = fr[j:j + 1, :] + (lsr * cr - lsi * ci), fi[j:j + 1, :] + (lsr * ci + lsi * cr)
        cr_ref[...] = cr
        ci_ref[...] = ci
        in_r, in_i = _rows_to_tile(rows_r), _rows_to_tile(rows_i)

        def add_entry(i, _):
            rows = pl.ds(pl.multiple_of(i * SUBLANES, SUBLANES), SUBLANES)
            pr, pi = pr_tab[i], pi_tab[i]
            sr_ref[rows, :] += pr * in_r - pi * in_i
            si_ref[rows, :] += pr * in_i + pi * in_r
            return 0

        lax.fori_loop(0, seg, add_entry, 0, unroll=2)

    blk = pl.BlockSpec((t, w), lambda j, i: (i, j))
    vec = pl.BlockSpec((1, w), lambda j, i: (0, j))
    sds = jax.ShapeDtypeStruct((L, S), F32)
    tab = pltpu.VMEM((seg, SUBLANES, w), F32)
    return pl.pallas_call(
        body, name=name, grid=(S // w, L // t),
        in_specs=[blk, blk, vec, vec], out_specs=[blk, blk], out_shape=[sds, sds],
        scratch_shapes=[tab, tab, pltpu.VMEM((1, w), F32), pltpu.VMEM((1, w), F32)],
        compiler_params=_params("parallel", "arbitrary"),
    )(bu_re, bu_im, lam_re, lam_im)


def _ssm_scan_bwd(d_re, d_im, s_re, s_im, lam_re, lam_im, *, name):
    L, S = d_re.shape
    w, t = _tile(S, SCAN_W), SCAN_T
    nt = L // t
    seg = t // SUBLANES

    def body(dr_ref, di_ref, sr_ref, si_ref, lr_ref, li_ref, gr_ref, gi_ref, ar_ref, ai_ref,
             pr_tab, pi_tab, cgr, cgi, acc_r, acc_i):
        step = pl.program_id(1)
        a = jnp.broadcast_to(lr_ref[...], (SUBLANES, w))
        b = jnp.broadcast_to(-li_ref[...], (SUBLANES, w))

        @pl.when(step == 0)
        def _():
            for r in (cgr, cgi, acc_r, acc_i):
                r[...] = jnp.zeros_like(r)
            _power_table(a, b, pr_tab, pi_tab, seg)

        def local_scan(ii, carry):
            gr, gi = carry
            rows = pl.ds(pl.multiple_of((seg - 1 - ii) * SUBLANES, SUBLANES), SUBLANES)
            ngr = a * gr - b * gi + dr_ref[rows, :]
            ngi = a * gi + b * gr + di_ref[rows, :]
            gr_ref[rows, :] = ngr
            gi_ref[rows, :] = ngi
            return ngr, ngi

        zero = jnp.zeros((SUBLANES, w), F32)
        fr, fi = lax.fori_loop(0, seg, local_scan, (zero, zero), unroll=2)
        lsr, lsi = pr_tab[seg - 1][0:1, :], pi_tab[seg - 1][0:1, :]
        cr, ci = cgr[...], cgi[...]
        rows_r, rows_i = [None] * SUBLANES, [None] * SUBLANES
        for j in reversed(range(SUBLANES)):
            rows_r[j], rows_i[j] = cr, ci
            cr, ci = fr[j:j + 1, :] + (lsr * cr - lsi * ci), fi[j:j + 1, :] + (lsr * ci + lsi * cr)
        cgr[...] = cr
        cgi[...] = ci
        in_r, in_i = _rows_to_tile(rows_r), _rows_to_tile(rows_i)

        def add_entry(ii, carry):
            nr, ni, xr, xi = carry
            rows = pl.ds(pl.multiple_of((seg - 1 - ii) * SUBLANES, SUBLANES), SUBLANES)
            sr = sr_ref[rows, :]
            si = si_ref[rows, :]
            xr = xr + (nr * sr + ni * si)
            xi = xi + (ni * sr - nr * si)
            pr, pi = pr_tab[ii], pi_tab[ii]
            gr = gr_ref[rows, :] + (pr * in_r - pi * in_i)
            gi = gi_ref[rows, :] + (pr * in_i + pi * in_r)
            gr_ref[rows, :] = gr
            gi_ref[rows, :] = gi
            return gr, gi, xr, xi

        _, _, xr, xi = lax.fori_loop(0, seg, add_entry, (in_r, in_i, acc_r[...], acc_i[...]), unroll=2)
        acc_r[...] = xr
        acc_i[...] = xi

        @pl.when(step == nt - 1)
        def _():
            ar_ref[...] = jnp.sum(xr, axis=0, keepdims=True)
            ai_ref[...] = jnp.sum(xi, axis=0, keepdims=True)

    blk = pl.BlockSpec((t, w), lambda j, i: (nt - 1 - i, j))
    vec = pl.BlockSpec((1, w), lambda j, i: (0, j))
    sds = jax.ShapeDtypeStruct((L, S), F32)
    vsds = jax.ShapeDtypeStruct((1, S), F32)
    tab = pltpu.VMEM((seg, SUBLANES, w), F32)
    tile = pltpu.VMEM((SUBLANES, w), F32)
    return pl.pallas_call(
        body, name=name, grid=(S // w, nt),
        in_specs=[blk, blk, blk, blk, vec, vec], out_specs=[blk, blk, vec, vec],
        out_shape=[sds, sds, vsds, vsds],
        scratch_shapes=[tab, tab, pltpu.VMEM((1, w), F32), pltpu.VMEM((1, w), F32), tile, tile],
        compiler_params=_params("parallel", "arbitrary"),
    )(d_re, d_im, s_re, s_im, lam_re, lam_im)


def _ssm_discretize(a_re, a_im, log_dt, b_re, b_im, c_re, c_im, d_skip):
    G, P, N = SSM_GROUPS, SSM_STATE, SSM_GROUP
    a = jnp.minimum(a_re, -1e-4)
    dt = jnp.exp(log_dt)[:, None]
    mag = jnp.exp(a * dt)
    lr = mag * jnp.cos(a_im * dt)
    li = mag * jnp.sin(a_im * dt)
    den = a * a + a_im * a_im
    cr = ((lr - 1.0) * a + li * a_im) / den
    ci = (li * a - (lr - 1.0) * a_im) / den
    bbr = cr[..., None] * b_re - ci[..., None] * b_im
    bbi = cr[..., None] * b_im + ci[..., None] * b_re
    gl = G // SSM_BLOCKS
    eye = jnp.eye(gl, dtype=F32)[None, :, None, :, None]

    def in_map(bb):
        t = bb.transpose(0, 2, 1).reshape(SSM_BLOCKS, gl, N, P)
        return (eye * t[:, :, :, None, :]).reshape(SSM_BLOCKS, gl * N, gl * P)

    def out_map(c):
        t = c.transpose(0, 2, 1).reshape(SSM_BLOCKS, gl, P, N)
        return (eye * t[:, :, :, None, :]).reshape(SSM_BLOCKS, gl * P, gl * N)

    return (lr.reshape(1, G * P), li.reshape(1, G * P), in_map(bbr), in_map(bbi),
            out_map(c_re), out_map(-c_im), d_skip.reshape(1, SSM_WIDTH))


ATT_T = 512
ATT_WIDE = 2
CUM_T = 512


def _split3(x):
    hi = x.astype(BF16)
    r1 = x - hi.astype(F32)
    mid = r1.astype(BF16)
    lo = (r1 - mid.astype(F32)).astype(BF16)
    return hi, mid, lo


def _tri_dot(tri, x):
    hi, mid, lo = _split3(x)
    dot = lambda p: jnp.dot(tri, p, preferred_element_type=F32)
    return dot(hi) + dot(mid) + dot(lo)


def _log_sigmoid(x):
    return jnp.minimum(x, 0.0) - jnp.log(1.0 + jnp.exp(-jnp.abs(x)))


def _fox_cum(f, fb, *, name):
    L, W = f.shape
    t = _tile(L, CUM_T, SUBLANES)

    def body(f_ref, b_ref, o_ref, carry):
        @pl.when(pl.program_id(0) == 0)
        def _():
            carry[...] = jnp.zeros_like(carry)

        row = lax.broadcasted_iota(jnp.int32, (t, t), 0)
        col = lax.broadcasted_iota(jnp.int32, (t, t), 1)
        tri = jnp.where(col <= row, 1.0, 0.0).astype(BF16)
        c = _tri_dot(tri, _log_sigmoid(f_ref[...] + b_ref[...])) + carry[...]
        o_ref[...] = c
        carry[...] = c[t - 1:t, :]

    return pl.pallas_call(
        body, name=name, grid=(L // t,),
        in_specs=[_row_spec(t, W), _vec_spec(W)], out_specs=_row_spec(t, W),
        out_shape=jax.ShapeDtypeStruct((L, W), F32),
        scratch_shapes=[pltpu.VMEM((1, W), F32)], compiler_params=_params("arbitrary"),
    )(f, fb)


def _fox_cum_bwd(dcum, f, fb, *, name):
    L, W = f.shape
    t = _tile(L, CUM_T, SUBLANES)
    n = L // t

    def body(d_ref, f_ref, b_ref, o_ref, db_ref, carry):
        i = pl.program_id(0)

        @pl.when(i == 0)
        def _():
            carry[...] = jnp.zeros_like(carry)
            db_ref[...] = jnp.zeros_like(db_ref)

        row = lax.broadcasted_iota(jnp.int32, (t, t), 0)
        col = lax.broadcasted_iota(jnp.int32, (t, t), 1)
        tri = jnp.where(col >= row, 1.0, 0.0).astype(BF16)
        dlog = _tri_dot(tri, d_ref[...]) + carry[...]
        carry[...] = dlog[0:1, :]
        df = dlog * _sigmoid(-(f_ref[...] + b_ref[...]))
        o_ref[...] = df.astype(BF16)
        db_ref[...] += _colsum8(df)
        _finish_colsums(i, n - 1, (db_ref,))

    rev = pl.BlockSpec((t, W), lambda i: (n - 1 - i, 0))
    return pl.pallas_call(
        body, name=name, grid=(n,),
        in_specs=[rev, rev, _vec_spec(W)], out_specs=[rev, _vec_spec(W, SUBLANES)],
        out_shape=[jax.ShapeDtypeStruct((L, W), BF16), jax.ShapeDtypeStruct((SUBLANES, W), F32)],
        scratch_shapes=[pltpu.VMEM((1, W), F32)], compiler_params=_params("arbitrary"),
    )(dcum, f, fb)


def _head_col(blk, h):
    lane = lax.broadcasted_iota(jnp.int32, blk.shape, 1)
    return jnp.sum(jnp.where(lane == h * HEAD_DIM, blk, 0.0), axis=1, keepdims=True)


def _lo_mask(rows):
    return lax.broadcasted_iota(jnp.int32, (rows, LANES), 1) < HEAD_DIM


def _causal(t):
    row = lax.broadcasted_iota(jnp.int32, (t, t), 0)
    col = lax.broadcasted_iota(jnp.int32, (t, t), 1)
    return col <= row


def _call_with_exchange(body, exchange, *, name, grid, in_specs, out_specs, out_shape, operands, sem=None):
    sem = sem or ("parallel",) + ("arbitrary",) * (len(grid) - 1)
    if exchange is None:
        return pl.pallas_call(body, name=name, grid=grid, in_specs=in_specs, out_specs=out_specs,
                              out_shape=out_shape, compiler_params=_params(*sem))(*operands)
    n_in, n_out = len(in_specs), len(out_specs)
    ei, eo = len(exchange.operands), len(exchange.out_shapes)

    def wrapped(*refs):
        ins, ex_in = refs[:n_in], refs[n_in:n_in + ei]
        outs, ex_out = refs[n_in + ei:n_in + ei + n_out], refs[n_in + ei + n_out:n_in + ei + n_out + eo]
        sems = refs[n_in + ei + n_out + eo:]
        ids = [pl.program_id(d) for d in range(len(grid))]
        first = functools.reduce(jnp.logical_and, [i == 0 for i in ids])
        last = functools.reduce(jnp.logical_and, [i == g - 1 for i, g in zip(ids, grid)])

        @pl.when(first)
        def _():
            exchange.start(ex_in, ex_out, sems)

        body(*ins, *outs)

        @pl.when(last)
        def _():
            exchange.finish(ex_in, ex_out, sems)

    return pl.pallas_call(
        wrapped, name=name, grid=grid, in_specs=list(in_specs) + [HBM_SPEC] * ei,
        out_specs=list(out_specs) + [HBM_SPEC] * eo, out_shape=list(out_shape) + list(exchange.out_shapes),
        scratch_shapes=exchange.sems, compiler_params=_params(*(("arbitrary",) * len(grid))),
    )(*operands, *exchange.operands)


def _fox_fwd(qkv, cum_cols, cum_rows, *, name, exchange=None):
    L = qkv.shape[0]
    t = _tile(L, ATT_T)
    nq = L // t
    npair = ATTN_HEADS // 2

    def body(q_ref, k_ref, v_ref, cc_ref, cr_ref, o_ref, o32_ref, lse_ref):
        iq = pl.program_id(1)
        lo = _lo_mask(t)
        qv = q_ref[...] * 0.125
        zq = jnp.zeros_like(qv)
        qh = (jnp.where(lo, qv, zq), jnp.where(lo, zq, qv))
        ccv = cc_ref[...]
        cq = (_head_col(ccv, 0), _head_col(ccv, 1))

        def step(block, width, carry, masked):
            start = pl.multiple_of(block * width, width)
            kb = k_ref[pl.ds(start, width), :]
            vb = v_ref[pl.ds(start, width), :]
            out = []
            for h in range(2):
                m, l, acc = carry[h]
                s = lax.dot_general(qh[h], kb, NT, preferred_element_type=F32)
                s = s + (cq[h] - cr_ref[h:h + 1, pl.ds(start, width)])
                if masked:
                    s = jnp.where(_causal(t), s, -jnp.inf)
                m_new = jnp.maximum(m, jnp.max(s, axis=1, keepdims=True))
                alpha = jnp.exp(m - m_new)
                p = jnp.exp(s - m_new)
                l = alpha * l + jnp.sum(p, axis=1, keepdims=True)
                acc = alpha * acc + jnp.dot(p.astype(BF16), vb, preferred_element_type=F32)
                out.append((m_new, l, acc))
            return tuple(out)

        wide = ATT_WIDE * t
        init1 = (jnp.full((t, 1), -jnp.inf, F32), jnp.zeros((t, 1), F32), jnp.zeros((t, LANES), F32))
        carry = lax.fori_loop(0, iq // ATT_WIDE, lambda j, c: step(j, wide, c, False), (init1, init1))
        carry = lax.fori_loop((iq // ATT_WIDE) * ATT_WIDE, iq, lambda ik, c: step(ik, t, c, False), carry)
        (m0, l0, a0), (m1, l1, a1) = step(iq, t, carry, True)
        out = jnp.where(lo, a0 / l0, a1 / l1)
        o_ref[...] = out.astype(BF16)
        o32_ref[...] = out
        lse_ref[...] = jnp.where(lo, m0 + jnp.log(l0), m1 + jnp.log(l1))

    blk = lambda off: pl.BlockSpec((t, LANES), lambda hp, iq: (iq, off + hp))
    whole = lambda off: pl.BlockSpec((L, LANES), lambda hp, iq: (0, off + hp))
    return _call_with_exchange(
        body, exchange, name=name, grid=(npair, nq),
        in_specs=[blk(0), whole(npair), whole(2 * npair), blk(0),
                  pl.BlockSpec((None, SUBLANES, L), lambda hp, iq: (hp, 0, 0))],
        out_specs=[blk(0), blk(0), blk(0)],
        out_shape=[jax.ShapeDtypeStruct((L, ATTN_WIDTH), BF16), jax.ShapeDtypeStruct((L, ATTN_WIDTH), F32),
                   jax.ShapeDtypeStruct((L, ATTN_WIDTH), F32)],
        operands=(qkv, qkv, qkv, cum_cols, cum_rows))


STAT_LSE, STAT_CUM, STAT_DELTA = 0, 2, 4


def _lane_col(blk, idx):
    lane = lax.broadcasted_iota(jnp.int32, blk.shape, 1)
    return jnp.sum(jnp.where(lane == idx, blk, 0.0), axis=1, keepdims=True)


def _fox_rowstats(do, o, lse, cum_cols, *, name):
    L = do.shape[0]
    t = _tile(L, ATT_T)

    def body(do_ref, o_ref, lse_ref, cc_ref, st_ref):
        lo = _lo_mask(t)
        dd = do_ref[...].astype(F32) * o_ref[...]
        lsev, ccv = lse_ref[...], cc_ref[...]
        cols = (_head_col(lsev, 0), _head_col(lsev, 1), _head_col(ccv, 0), _head_col(ccv, 1),
                jnp.sum(jnp.where(lo, dd, 0.0), axis=1, keepdims=True),
                jnp.sum(jnp.where(lo, 0.0, dd), axis=1, keepdims=True))
        lane = lax.broadcasted_iota(jnp.int32, (t, LANES), 1)
        out = jnp.zeros((t, LANES), F32)
        for i, col in enumerate(cols):
            out = jnp.where(lane == i, col, out)
        st_ref[...] = out

    blk = pl.BlockSpec((t, LANES), lambda hp, i: (i, hp))
    return pl.pallas_call(
        body, name=name, grid=(ATTN_HEADS // 2, L // t),
        in_specs=[blk, blk, blk, blk], out_specs=blk,
        out_shape=jax.ShapeDtypeStruct((L, ATTN_WIDTH), F32),
        compiler_params=_params("parallel", "parallel"),
    )(do, o, lse, cum_cols)


def _fox_bwd(qkv, do, stats, cum_rows, *, name, exchange=None):
    L = qkv.shape[0]
    t = _tile(L, ATT_T)
    nq = L // t
    npair = ATTN_HEADS // 2

    def body(q_ref, do_ref, st_ref, qt_ref, dot_ref, k_ref, v_ref, cr_ref, dk_ref, dv_ref, dc_ref, dq_ref, drow_ref):
        ik = pl.program_id(1)

        @pl.when(ik == 0)
        def _():
            dq_ref[...] = jnp.zeros_like(dq_ref)
            drow_ref[...] = jnp.zeros_like(drow_ref)

        lo = _lo_mask(t)
        lo_rows = lax.broadcasted_iota(jnp.int32, (LANES, t), 0) < HEAD_DIM
        lane = lax.broadcasted_iota(jnp.int32, (t, LANES), 1)
        kb = k_ref[...]
        vb = v_ref[...]
        zk = jnp.zeros_like(kb)
        kh = (jnp.where(lo, kb, zk), jnp.where(lo, zk, kb))
        vh = (jnp.where(lo, vb, zk), jnp.where(lo, zk, vb))
        ck = (cr_ref[0:1, :], cr_ref[1:2, :])

        def step(block, width, carry, masked):
            dk, dv, dc0, dc1 = carry
            start = pl.multiple_of(block * width, width)
            qb = q_ref[pl.ds(start, width), :] * 0.125
            dob = do_ref[pl.ds(start, width), :]
            stb = st_ref[pl.ds(start, width), :]
            lane = lax.broadcasted_iota(jnp.int32, (width, LANES), 1)
            qtb = qt_ref[:, pl.ds(start, width)] * 0.125
            dotb = dot_ref[:, pl.ds(start, width)]
            dks, dvs, dcs, dqs, rss = [], [], [], [], []
            for h in range(2):
                s = lax.dot_general(qb, kh[h], NT, preferred_element_type=F32)
                s = s + (_lane_col(stb, STAT_CUM + h) - ck[h])
                if masked:
                    s = jnp.where(_causal(t), s, -jnp.inf)
                p = jnp.exp(s - _lane_col(stb, STAT_LSE + h))
                dp = lax.dot_general(dob, vh[h], NT, preferred_element_type=F32)
                ds = p * (dp - _lane_col(stb, STAT_DELTA + h))
                dsb = ds.astype(BF16)
                dvs.append(jnp.dot(dotb, p.astype(BF16), preferred_element_type=F32))
                dks.append(jnp.dot(qtb, dsb, preferred_element_type=F32))
                dqs.append(jnp.dot(dsb, kh[h], preferred_element_type=F32))
                dcs.append(jnp.sum(ds, axis=0, keepdims=True))
                rss.append(jnp.sum(ds, axis=1, keepdims=True))
            dq_ref[pl.ds(start, width), :] += 0.125 * (dqs[0] + dqs[1])
            drow_ref[pl.ds(start, width), :] += jnp.where(lane == 0, rss[0], jnp.where(lane == 1, rss[1], 0.0))
            return (dk + jnp.where(lo_rows, dks[0], dks[1]), dv + jnp.where(lo_rows, dvs[0], dvs[1]),
                    dc0 - dcs[0], dc1 - dcs[1])

        zero = jnp.zeros((LANES, t), F32)
        zrow = jnp.zeros((1, t), F32)
        carry = step(ik, t, (zero, zero, zrow, zrow), True)
        first_wide = (ik + ATT_WIDE) // ATT_WIDE
        carry = lax.fori_loop(ik + 1, jnp.minimum(first_wide * ATT_WIDE, nq), lambda iq, c: step(iq, t, c, False),
                              carry)
        dk, dv, dc0, dc1 = lax.fori_loop(first_wide, nq // ATT_WIDE,
                                         lambda j, c: step(j, ATT_WIDE * t, c, False), carry)
        dk_ref[...] = dk.T.astype(BF16)
        dv_ref[...] = dv.T.astype(BF16)
        dc_ref[...] = jnp.zeros_like(dc_ref)
        dc_ref[0:1, :] = dc0
        dc_ref[1:2, :] = dc1

    whole = lambda off: pl.BlockSpec((L, LANES), lambda hp, ik: (0, off + hp))
    blk = lambda off: pl.BlockSpec((t, LANES), lambda hp, ik: (ik, off + hp))
    rows = pl.BlockSpec((None, SUBLANES, t), lambda hp, ik: (hp, 0, ik))
    whole_t = pl.BlockSpec((LANES, L), lambda hp, ik: (hp, 0))
    return _call_with_exchange(
        body, exchange, name=name, grid=(npair, nq),
        in_specs=[whole(0), whole(0), whole(0), whole_t, whole_t, blk(npair), blk(2 * npair), rows],
        out_specs=[blk(0), blk(0), rows, whole(0), whole(0)],
        out_shape=[jax.ShapeDtypeStruct((L, ATTN_WIDTH), BF16), jax.ShapeDtypeStruct((L, ATTN_WIDTH), BF16),
                   jax.ShapeDtypeStruct((npair, SUBLANES, L), F32),
                   jax.ShapeDtypeStruct((L, ATTN_WIDTH), F32), jax.ShapeDtypeStruct((L, ATTN_WIDTH), F32)],
        operands=(qkv, do, stats, qkv[:, :ATTN_WIDTH].T, do.T, qkv, qkv, cum_rows))


def _mod_partial(c_all, mod_w, mod_b_cols, *, name):
    depth, K, cols = mod_w.shape
    tn = _tile(cols, 768)

    def body(c_ref, w_ref, b_ref, o_ref):
        cv = c_ref[...]
        sc = (cv * _sigmoid(cv)).astype(BF16)
        o_ref[...] = jnp.dot(sc, w_ref[...].astype(BF16), preferred_element_type=F32) + b_ref[...]

    return pl.pallas_call(
        body, name=name, grid=(depth, cols // tn),
        in_specs=[pl.BlockSpec((N_DEV, K), lambda l, j: (0, 0)),
                  pl.BlockSpec((None, K, tn), lambda l, j: (l, 0, j)),
                  pl.BlockSpec((None, 1, tn), lambda l, j: (l, 0, j))],
        out_specs=pl.BlockSpec((None, N_DEV, tn), lambda l, j: (l, 0, j)),
        out_shape=jax.ShapeDtypeStruct((depth, N_DEV, cols), F32),
        compiler_params=_params("parallel", "parallel"),
    )(c_all, mod_w, mod_b_cols)


def _mod_wgrad(c_all_t, dmod, *, name):
    depth, nb, cols = dmod.shape
    K = c_all_t.shape[0]
    tn = _tile(cols, 768)
    tk = _tile(K, 256, SUBLANES)

    def body(c_ref, d_ref, o_ref):
        cv = c_ref[...]
        sc = cv * _sigmoid(cv)
        dv = d_ref[...]
        acc = sc[:, 0:1] * dv[0:1, :]
        for b in range(1, nb):
            acc = acc + sc[:, b:b + 1] * dv[b:b + 1, :]
        o_ref[...] = acc

    return pl.pallas_call(
        body, name=name, grid=(depth, K // tk, cols // tn),
        in_specs=[pl.BlockSpec((tk, nb), lambda l, i, j: (i, 0)),
                  pl.BlockSpec((None, nb, tn), lambda l, i, j: (l, 0, j))],
        out_specs=pl.BlockSpec((None, tk, tn), lambda l, i, j: (l, i, j)),
        out_shape=jax.ShapeDtypeStruct((depth, K, cols), F32),
        compiler_params=_params("parallel", "parallel", "parallel"),
    )(c_all_t, dmod)


def _adamw(w, g, m, v, *, name):
    shape = w.shape
    cols = shape[-1]
    rows = int(np.prod(shape[:-1]))
    t = _tile(rows, 256, SUBLANES) if rows % SUBLANES == 0 else rows
    r2 = lambda a: a.reshape(rows, cols)

    def body(w_ref, g_ref, m_ref, v_ref, d_ref, nm_ref, nv_ref):
        gv = g_ref[...]
        nm = ADAM_B1 * m_ref[...] + (1.0 - ADAM_B1) * gv
        nv = ADAM_B2 * v_ref[...] + (1.0 - ADAM_B2) * (gv * gv)
        m_hat = nm / (1.0 - ADAM_B1 ** ADAM_STEP)
        v_hat = nv / (1.0 - ADAM_B2 ** ADAM_STEP)
        d_ref[...] = -ADAM_LR * (m_hat / (jnp.sqrt(v_hat) + ADAM_EPS) + ADAM_WD * w_ref[...])
        nm_ref[...] = nm
        nv_ref[...] = nv

    spec = pl.BlockSpec((t, cols), lambda i: (i, 0))
    sds = jax.ShapeDtypeStruct((rows, cols), F32)
    d, nm, nv = pl.pallas_call(
        body, name=name, grid=(rows // t,),
        in_specs=[spec] * 4, out_specs=[spec] * 3, out_shape=[sds] * 3,
        compiler_params=_params("parallel"),
    )(r2(w), r2(g), r2(m), r2(v))
    return d.reshape(shape), nm.reshape(shape), nv.reshape(shape)


def _my_place():
    return lax.axis_index("x"), lax.axis_index("y"), lax.axis_index("c")


def _other_chips(x, y):
    return [(1 - x, y), (x, 1 - y), (1 - x, 1 - y)]


def _all_gather8(v, *, name, with_sum=False):
    m, n = v.shape

    def body(x_ref, out_ref, *rest):
        if with_sum:
            sum_ref, send_sems, recv_sems, local_sem = rest
        else:
            send_sems, recv_sems, local_sem = rest
        x, y, c = _my_place()
        me, sibling = (x, y, c), (x, y, 1 - c)
        chips = _other_chips(x, y)

        def rows(px, py, pc):
            return out_ref.at[pl.ds((4 * px + 2 * py + pc) * m, m), :]

        def copy(k, block, to, src=None):
            return pltpu.make_async_remote_copy(
                src_ref=rows(*block) if src is None else src, dst_ref=rows(*block),
                send_sem=send_sems.at[k], recv_sem=recv_sems.at[k], device_id=to, device_id_type=MESH)

        mine = pltpu.make_async_copy(x_ref, rows(*me), local_sem)
        mine.start()
        first = [copy(0, me, sibling, src=x_ref)]
        first += [copy(1 + j, me, (*chip, c), src=x_ref) for j, chip in enumerate(chips)]
        for cp in first:
            cp.start()
        passed = [copy(4 + j, (*chip, c), sibling) for j, chip in enumerate(chips)]
        for j, chip in enumerate(chips):
            copy(1 + j, (*chip, c), me).wait_recv()
            passed[j].start()
        copy(0, sibling, me).wait_recv()
        for j, chip in enumerate(chips):
            copy(4 + j, (*chip, 1 - c), me).wait_recv()
        for cp in first + passed:
            cp.wait_send()
        mine.wait()
        if with_sum:
            acc = out_ref[pl.ds(0, m), :]
            for d in range(1, N_DEV):
                acc = acc + out_ref[pl.ds(d * m, m), :]
            sum_ref[...] = acc

    vm = pl.BlockSpec(memory_space=pltpu.VMEM)
    out_shape = [jax.ShapeDtypeStruct((N_DEV * m, n), F32)]
    if with_sum:
        out_shape.append(jax.ShapeDtypeStruct((m, n), F32))
    res = pl.pallas_call(
        body, name=name, out_shape=out_shape, in_specs=[vm], out_specs=[vm] * len(out_shape),
        scratch_shapes=[pltpu.SemaphoreType.DMA((7,)), pltpu.SemaphoreType.DMA((7,)), pltpu.SemaphoreType.DMA],
        compiler_params=pltpu.CompilerParams(vmem_limit_bytes=VMEM_LIMIT),
    )(v)
    return res if with_sum else res[0]


class _Cut(NamedTuple):
    shape: tuple
    slab: int
    half: int


IN_WIDTH = SSM_WIDTH + 3 * ATTN_WIDTH + ATTN_HEADS + 2 * D_MODEL
CUTS = dict(
    ffn_w_in=_Cut((1, D_MODEL, 2 * D_FF), 2, 1),
    ffn_w_out=_Cut((1, D_FF, D_MODEL), 1, 2),
    mix_w_in=_Cut((N_CHIPS, D_MODEL, IN_WIDTH // N_CHIPS), 0, 1),
    glu_w=_Cut((1, SSM_WIDTH, 2 * D_MODEL), 2, 1),
    attn_w_out=_Cut((1, ATTN_WIDTH, D_MODEL), 2, 1),
    mix_w_out=_Cut((1, D_MODEL, D_MODEL), 1, 2),
)
LAYER_MATS = (("ffn_w_in", 0), ("ffn_w_in", 1), ("ffn_w_out", 0), ("ffn_w_out", 1), ("mix_w_in", None),
              ("glu_w", None), ("attn_w_out", None), ("mix_w_out", None))
FIRST_MATS = (0, 2)
MIXER_MATS = (4, 5, 6, 7)
LATE_MATS = (1, 3)


def _part_shape(cut, slab=False, half=False):
    s = list(cut.shape)
    if slab:
        s[cut.slab] //= N_CHIPS
    if half:
        s[cut.half] //= 2
    return tuple(s)


def _window(ref, cut, slab=None, half=None):
    idx = [slice(None)] * len(cut.shape)
    for axis, parts, which in ((cut.slab, N_CHIPS, slab), (cut.half, 2, half)):
        if which is not None:
            width = cut.shape[axis] // parts
            idx[axis] = pl.ds(pl.multiple_of(which * width, width), width)
    return ref.at[tuple(idx)]


HBM_SPEC = pl.BlockSpec(memory_space=pltpu.HBM)


class _Exchange(NamedTuple):
    operands: list
    out_shapes: list
    sems: list
    start: object
    finish: object


def _run_exchange(ex, *, name):
    ni, no = len(ex.operands), len(ex.out_shapes)

    def body(*refs):
        parts = (refs[:ni], refs[ni:ni + no], refs[ni + no:])
        ex.start(*parts)
        ex.finish(*parts)

    return pl.pallas_call(
        body, name=name, out_shape=ex.out_shapes, in_specs=[HBM_SPEC] * ni, out_specs=[HBM_SPEC] * no,
        scratch_shapes=ex.sems,
    )(*ex.operands)


def _gather_exchange(shards, cuts):
    n = len(shards)

    def setup(s_refs, f_refs, sems):
        send_sems, recv_sems = sems
        x, y, c = _my_place()
        me, sibling, mine = (x, y, c), (x, y, 1 - c), 2 * x + y
        chips = _other_chips(x, y)

        def copy(i, k, src, dst, to):
            return pltpu.make_async_remote_copy(
                src_ref=src, dst_ref=dst, send_sem=send_sems.at[7 * i + k], recv_sem=recv_sems.at[7 * i + k],
                device_id=to, device_id_type=MESH)

        def landed(i, j, half):
            return _window(f_refs[i], cuts[i], slab=2 * chips[j][0] + chips[j][1], half=half)

        def sends():
            own = [copy(i, 6, s_refs[i], _window(f_refs[i], cuts[i], slab=mine), sibling) for i in range(n)]
            return own + [copy(i, j, _window(s_refs[i], cuts[i], half=c),
                               _window(f_refs[i], cuts[i], slab=mine, half=c), (*chips[j], c))
                          for i in range(n) for j in range(3)]

        return c, me, sibling, copy, landed, sends

    def start(s_refs, f_refs, sems):
        for cp in setup(s_refs, f_refs, sems)[-1]():
            cp.start()

    def finish(s_refs, f_refs, sems):
        c, me, sibling, copy, landed, sends = setup(s_refs, f_refs, sems)
        passed = []
        for i in range(n):
            for j in range(3):
                copy(i, j, landed(i, j, c), landed(i, j, c), me).wait_recv()
                passed.append(copy(i, 3 + j, landed(i, j, c), landed(i, j, c), sibling))
                passed[-1].start()
        for i in range(n):
            for j in range(3):
                copy(i, 3 + j, landed(i, j, 1 - c), landed(i, j, 1 - c), me).wait_recv()
        for i in range(n):
            mine_i = _window(f_refs[i], cuts[i], slab=2 * me[0] + me[1])
            copy(i, 6, mine_i, mine_i, me).wait_recv()
        for cp in sends() + passed:
            cp.wait_send()

    return _Exchange(
        list(shards), [jax.ShapeDtypeStruct(cut.shape, s.dtype) for s, cut in zip(shards, cuts)],
        [pltpu.SemaphoreType.DMA((7 * n,)), pltpu.SemaphoreType.DMA((7 * n,))], start, finish)


def _swap_exchange(mats, cuts):
    n = len(mats)

    def copies(m_refs, r_refs, sems):
        send_sems, recv_sems = sems
        x, y, c = _my_place()
        return [pltpu.make_async_remote_copy(
            src_ref=_window(m_refs[i], cuts[i], half=1 - c), dst_ref=r_refs[i], send_sem=send_sems.at[i],
            recv_sem=recv_sems.at[i], device_id=(x, y, 1 - c), device_id_type=MESH) for i in range(n)]

    def start(m_refs, r_refs, sems):
        for cp in copies(m_refs, r_refs, sems):
            cp.start()

    def finish(m_refs, r_refs, sems):
        for cp in copies(m_refs, r_refs, sems):
            cp.wait()

    return _Exchange(
        list(mats), [jax.ShapeDtypeStruct(_part_shape(cut, half=True), m.dtype) for m, cut in zip(mats, cuts)],
        [pltpu.SemaphoreType.DMA((n,)), pltpu.SemaphoreType.DMA((n,))], start, finish)


def _partials_exchange(sums, cuts):
    n = len(sums)

    def copies(s_refs, p_refs, sems):
        send_sems, recv_sems = sems
        x, y, c = _my_place()
        return [pltpu.make_async_remote_copy(
            src_ref=_window(s_refs[i], cuts[i], slab=2 * chip[0] + chip[1]), dst_ref=p_refs[i].at[j],
            send_sem=send_sems.at[3 * i + j], recv_sem=recv_sems.at[3 * i + j],
            device_id=(*chip, c), device_id_type=MESH)
            for i in range(n) for j, chip in enumerate(_other_chips(x, y))]

    def start(s_refs, p_refs, sems):
        for cp in copies(s_refs, p_refs, sems):
            cp.start()

    def finish(s_refs, p_refs, sems):
        for cp in copies(s_refs, p_refs, sems):
            cp.wait()

    return _Exchange(
        list(sums), [jax.ShapeDtypeStruct((3,) + _part_shape(cut, slab=True, half=True), s.dtype)
                     for s, cut in zip(sums, cuts)],
        [pltpu.SemaphoreType.DMA((3 * n,)), pltpu.SemaphoreType.DMA((3 * n,))], start, finish)


def _share_all(dests, cuts, places, *, name):
    names = list(dests)
    nn, n = len(names), len(places)

    def body(*refs):
        o_refs = dict(zip(names, refs[nn:2 * nn]))
        send_sems, recv_sems = refs[2 * nn:]
        x, y, c = _my_place()

        def win(i, half):
            slab_cut = _Cut(_part_shape(cuts[i], slab=True), cuts[i].slab, cuts[i].half)
            return _window(o_refs[places[i][0]].at[places[i][1]], slab_cut, half=half)

        def copy(i, half):
            return pltpu.make_async_remote_copy(
                src_ref=win(i, half), dst_ref=win(i, half), send_sem=send_sems.at[i], recv_sem=recv_sems.at[i],
                device_id=(x, y, 1 - c), device_id_type=MESH)

        for i in range(n):
            copy(i, c).start()
        for i in range(n):
            copy(i, c).wait_send()
            copy(i, 1 - c).wait_recv()

    return pl.pallas_call(
        body, name=name, out_shape=[jax.ShapeDtypeStruct(dests[k].shape, F32) for k in names],
        in_specs=[HBM_SPEC] * nn, out_specs=[HBM_SPEC] * nn,
        input_output_aliases={i: i for i in range(nn)},
        scratch_shapes=[pltpu.SemaphoreType.DMA((n,)), pltpu.SemaphoreType.DMA((n,))],
    )(*[dests[k] for k in names])


def _cut_blocks(shape):
    _, R, C = shape
    tr = _tile(R, 512, 16)
    tc = _tile(C, 2048) if C % LANES == 0 else C
    return (None, tr, tc), (shape[0], R // tr, C // tc)


def _offset_map(axis, blocks):
    def index_map(b, i, j, which):
        idx = [b, i, j]
        idx[axis] = which[0] * blocks[axis] + idx[axis]
        return tuple(idx)
    return index_map


def _add_half(mat, other, cut, c_idx, *, name):
    shape = _part_shape(cut, half=True)
    block, grid = _cut_blocks(shape)

    def body(c_ref, m_ref, o_ref, f_ref, b_ref):
        s = m_ref[...] + o_ref[...]
        f_ref[...] = s
        b_ref[...] = s.astype(BF16)

    plain = pl.BlockSpec(block, lambda b, i, j, which: (b, i, j))
    grid_spec = pltpu.PrefetchScalarGridSpec(
        num_scalar_prefetch=1, grid=grid,
        in_specs=[pl.BlockSpec(block, _offset_map(cut.half, grid)), plain], out_specs=[plain, plain])
    return pl.pallas_call(
        body, name=name, grid_spec=grid_spec,
        out_shape=[jax.ShapeDtypeStruct(shape, F32), jax.ShapeDtypeStruct(shape, BF16)],
        compiler_params=_params("parallel", "parallel", "parallel"),
    )(c_idx, mat, other)


def _sum_slab(own, parts, cut, dest, place, chip_idx, c_idx, *, name):
    shape = _part_shape(cut, slab=True, half=True)
    block, grid = _cut_blocks(shape)
    assert shape[0] == 1

    def body(k_ref, c_ref, o_ref, p_ref, dest_ref, out_ref):
        acc = o_ref[...]
        for j in range(3):
            acc = acc + p_ref[j].astype(F32)
        out_ref[...] = acc

    def own_map(b, i, j, chip, core):
        idx = [b, i, j]
        idx[cut.slab] = chip[0] * grid[cut.slab] + idx[cut.slab]
        return tuple(idx)

    def dest_map(b, i, j, chip, core):
        idx = [b, i, j]
        idx[cut.half] = core[0] * grid[cut.half] + idx[cut.half]
        return tuple(place) + tuple(idx)

    grid_spec = pltpu.PrefetchScalarGridSpec(
        num_scalar_prefetch=2, grid=grid,
        in_specs=[pl.BlockSpec(block, own_map),
                  pl.BlockSpec((3,) + block[1:], lambda b, i, j, chip, core: (0, i, j)),
                  pl.BlockSpec(memory_space=pl.ANY)],
        out_specs=pl.BlockSpec((None,) * len(place) + block, dest_map))
    return pl.pallas_call(
        body, name=name, grid_spec=grid_spec, out_shape=jax.ShapeDtypeStruct(dest.shape, F32),
        input_output_aliases={4: 0},
        compiler_params=_params("parallel", "parallel", "parallel"),
    )(chip_idx, c_idx, own, parts.reshape((3,) + shape[1:]), dest)


def _pad_rows(flat, cols=8 * LANES, align=SUBLANES):
    n = flat.shape[0]
    rows = -(-n // (cols * align)) * align
    return jnp.pad(flat, (0, rows * cols - n)).reshape(rows, cols)


def _row(v):
    return v.reshape(1, -1)


def _ffn_fwd(x, mod, g_pre, g_post, w_in, w_out, tag, exchange=None):
    sh, sc, gate = _row(mod[0]), _row(mod[1]), _row(mod[2])
    h = _prenorm(x, _row(g_pre), sc, sh, name=f"prenorm_{tag}")
    gt, up, act, *exchanged = _mm_swiglu(h, w_in, name=f"swiglu_{tag}", exchange=exchange)
    y, x_out = _mm_postnorm(act, w_out, x, _row(g_post), gate, FFN_RES, name=f"ffn_out_{tag}")
    return x_out, (x, h, gt, up, act, y), exchanged


def _ffn_bwd(dxo, saved, mod, g_pre, g_post, w_in, w_out, tag, swap=None, after_swap=None):
    x, h, gt, up, act, y = saved
    sc, gate = _row(mod[1]), _row(mod[2])
    dy, dgate, dgpost, *swapped = _postnorm_bwd(dxo, y, _row(g_post), gate, FFN_RES, name=f"postnorm_bwd_{tag}",
                                                exchange=swap)
    exchange = after_swap(swapped) if after_swap is not None else None
    dgt, dup, *exchanged = _mm_swiglu_bwd(dy, w_out, gt, up, name=f"swiglu_bwd_{tag}", exchange=exchange)
    dw_out = _mm(act, dy, ta=True, name=f"dw_out_{tag}", tm=1408, tn=1024, tk=1024)
    dh = _mm(dgt, w_in, tb=True, name=f"dh_gate_{tag}", tk=D_FF)
    dh = _mm(dup, w_in, tb=True, b_k0=D_FF, bias=dh, name=f"dh_up_{tag}", tk=D_FF)
    dw_in = _mm(h, dgt, ta=True, out_n=2 * D_FF, name=f"dw_gate_{tag}", tm=1024, tn=1408, tk=1024)
    dw_in = _mm(h, dup, ta=True, out_n=2 * D_FF, out_j0=D_FF, into=dw_in, name=f"dw_up_{tag}",
                tm=1024, tn=1408, tk=1024)
    dx, dsh, dsc, dgpre = _prenorm_bwd(x, dh, dxo, _row(g_pre), sc, name=f"prenorm_bwd_{tag}")
    dmod = jnp.stack([dsh[0], dsc[0], dgate[0]])
    return dx, dmod, dgpre[0], dgpost[0], dw_in, dw_out, exchanged


def _split_mix_w_in(w):
    u0, q0, f0, g0 = 0, SSM_WIDTH, SSM_WIDTH + 3 * ATTN_WIDTH, SSM_WIDTH + 3 * ATTN_WIDTH + ATTN_HEADS
    w_f = jnp.pad(w[:, f0:g0], ((0, 0), (0, LANES - ATTN_HEADS)))
    return w[:, u0:q0], w[:, q0:f0], w_f, w[:, g0:]


def _mixer_fwd(x, mod, g_pre, g_post, w, ssm, forget_b, tag, exchange=None):
    L = x.shape[0]
    sh, sc, gate = _row(mod[0]), _row(mod[1]), _row(mod[2])
    lam_re, lam_im, bin_re, bin_im, cout_re, cout_im, dskip = ssm
    h = _prenorm(x, _row(g_pre), sc, sh, name=f"prenorm_{tag}")
    u = _mm(h, w["u"], name=f"proj_u_{tag}")
    qkv = _mm(h, w["qkv"], out_dtype=BF16, name=f"proj_qkv_{tag}")
    f = _mm(h, w["f"], name=f"proj_f_{tag}")
    gab = _mm(h, w["gab"], name=f"proj_gab_{tag}")
    u_i = _interleave(u)
    bu_re, bu_im = _mm_bd([u_i], [bin_re, bin_im], name=f"ssm_bu_{tag}")
    s_re, s_im = _ssm_scan_fwd(bu_re, bu_im, lam_re, lam_im, name=f"ssm_scan_{tag}")
    y_ssm = _deinterleave(_mm_bd([s_re, s_im], [cout_re, cout_im], bias=u_i, bscale=dskip, name=f"ssm_y_{tag}"))
    gl = _gelu_fwd(y_ssm, name=f"gelu_{tag}")
    z = _mm(gl, w["glu"], name=f"glu_{tag}")
    fb = jnp.pad(forget_b, (0, LANES - ATTN_HEADS)).reshape(1, LANES)
    cum = _fox_cum(f, fb, name=f"fox_cum_{tag}")
    cum8 = cum[:, :ATTN_HEADS]
    cum_cols = jnp.repeat(cum8, HEAD_DIM, axis=1)
    cum_rows = jnp.pad(cum8.T.reshape(ATTN_HEADS // 2, 2, L), ((0, 0), (0, SUBLANES - 2), (0, 0)))
    attn, attn32, lse, *exchanged = _fox_fwd(qkv, cum_cols, cum_rows, name=f"fox_fwd_{tag}", exchange=exchange)
    yb = _mm(attn, w["attn_out"], name=f"attn_out_{tag}")
    merged = _merge_fwd(z, yb, gab, name=f"merge_{tag}")
    y, x_out = _mm_postnorm(merged, w["out"], x, _row(g_post), gate, 1.0, name=f"mix_out_{tag}")
    saved = (x, h, u, u_i, qkv, f, gab, s_re, s_im, y_ssm, gl, z, fb, cum_cols, cum_rows, attn, attn32, lse, yb,
             merged, y)
    return x_out, saved, exchanged


def _mixer_bwd(dxo, saved, mod, g_pre, g_post, w, ssm, tag, swap=None, after_swap=None):
    (x, h, u, u_i, qkv, f, gab, s_re, s_im, y_ssm, gl, z, fb, cum_cols, cum_rows, attn, attn32, lse, yb,
     merged, y) = saved
    L = x.shape[0]
    sc, gate = _row(mod[1]), _row(mod[2])
    lam_re, lam_im, bin_re, bin_im, cout_re, cout_im, dskip = ssm
    dy, dgate, dgpost = _postnorm_bwd(dxo, y, _row(g_post), gate, 1.0, name=f"postnorm_bwd_{tag}")
    dmerged = _mm(dy, w["out"], tb=True, name=f"dmerged_{tag}")
    dw_out = _mm(merged, dy, ta=True, name=f"dw_mix_out_{tag}", tm=1024, tn=1024)
    dz, dyb, dgab, *swapped = _merge_bwd(dmerged, z, yb, gab, name=f"merge_bwd_{tag}", exchange=swap)
    exchange = after_swap(swapped) if after_swap is not None else None
    dgl = _mm(dz, w["glu"], tb=True, name=f"dgl_{tag}", tk=2048)
    dw_glu = _mm(gl, dz, ta=True, name=f"dw_glu_{tag}", tm=512, tn=2048)
    dys, dsk, dd = _gelu_bwd(dgl, y_ssm, u, dskip, name=f"gelu_bwd_{tag}")
    dys, dsk = _interleave(dys), _interleave(dsk)
    d_re, d_im = _mm_bd([dys], [cout_re, cout_im], tb=True, name=f"ssm_ds_{tag}")
    dcout_re = _mm_bd_t(s_re, dys, SSM_BLOCKS, name=f"ssm_dc_re_{tag}")
    dcout_im = _mm_bd_t(s_im, dys, SSM_BLOCKS, name=f"ssm_dc_im_{tag}")
    g_re, g_im, dlam_re, dlam_im = _ssm_scan_bwd(d_re, d_im, s_re, s_im, lam_re, lam_im, name=f"ssm_scan_bwd_{tag}")
    du = _mm_bd([g_re, g_im], [bin_re, bin_im], tb=True, bias=dsk, out_dtype=BF16, name=f"ssm_du_{tag}")
    du = _deinterleave(du)
    dbin_re = _mm_bd_t(u_i, g_re, SSM_BLOCKS, name=f"ssm_db_re_{tag}")
    dbin_im = _mm_bd_t(u_i, g_im, SSM_BLOCKS, name=f"ssm_db_im_{tag}")
    dssm = (dlam_re, dlam_im, dbin_re, dbin_im, dcout_re, dcout_im, _row(dd[0]))
    dattn = _mm(dyb, w["attn_out"], tb=True, out_dtype=BF16, name=f"dattn_{tag}")
    dw_attn = _mm(attn, dyb, ta=True, name=f"dw_attn_out_{tag}", tm=512, tn=1024)
    stats = _fox_rowstats(dattn, attn32, lse, cum_cols, name=f"fox_rowstats_{tag}")
    dk, dv, dcum_rows, dq, drow, *exchanged = _fox_bwd(qkv, dattn, stats, cum_rows, name=f"fox_bwd_{tag}",
                                                       exchange=exchange)
    drow8 = drow.reshape(L, ATTN_HEADS // 2, LANES)[:, :, :2].reshape(L, ATTN_HEADS)
    dcum = drow8 + dcum_rows[:, :2, :].reshape(ATTN_HEADS, L).T
    dcum = jnp.pad(dcum, ((0, 0), (0, LANES - ATTN_HEADS)))
    df, dfb = _fox_cum_bwd(dcum, f, fb, name=f"fox_cum_bwd_{tag}")
    dqkv = jnp.concatenate([dq.astype(BF16), dk, dv], axis=1)
    dh = _mm(dqkv, w["qkv"], tb=True, name=f"dh_qkv_{tag}", tk=1536)
    dh = _mm(du, w["u"], tb=True, bias=dh, name=f"dh_u_{tag}")
    dh = _mm(dgab, w["gab"], tb=True, bias=dh, name=f"dh_gab_{tag}", tk=2048)
    dh = _mm(df, w["f"], tb=True, bias=dh, name=f"dh_f_{tag}")
    dw_u = _mm(h, du, ta=True, name=f"dw_u_{tag}", tm=1024, tn=512)
    dw_qkv = _mm(h, dqkv, ta=True, name=f"dw_qkv_{tag}", tm=1024, tn=1536)
    dw_f = _mm(h, df, ta=True, name=f"dw_f_{tag}", tm=1024)
    dw_gab = _mm(h, dgab, ta=True, name=f"dw_gab_{tag}", tm=1024, tn=1024)
    dw_in = jnp.concatenate([dw_u, dw_qkv, dw_f[:, :ATTN_HEADS], dw_gab], axis=1)
    dx, dsh, dsc, dgpre = _prenorm_bwd(x, dh, dxo, _row(g_pre), sc, name=f"prenorm_bwd_{tag}")
    dmod = jnp.stack([dsh[0], dsc[0], dgate[0]])
    grads = dict(mix_w_in=dw_in, glu_w=dw_glu, attn_w_out=dw_attn, mix_w_out=dw_out,
                 forget_b=dfb[0, :ATTN_HEADS])
    return dx, dmod, dgpre[0], dgpost[0], grads, dssm, exchanged


SSM_NAMES = ("ssm_a_re", "ssm_a_im", "ssm_log_dt", "ssm_b_re", "ssm_b_im", "ssm_c_re", "ssm_c_im", "ssm_d")
SMALL_NAMES = ("forget_b",) + SSM_NAMES
WEIGHT_NAMES = ("mod_w", "mod_b", "norm_pre", "norm_post", "ffn_w_in", "ffn_w_out", "mix_w_in", "forget_b") \
    + SSM_NAMES + ("glu_w", "attn_w_out", "mix_w_out")


def _layer_shards(w, l):
    return [(w[n][l] if j is None else w[n][l, j]).astype(BF16).reshape(_part_shape(CUTS[n], slab=True))
            for n, j in LAYER_MATS]


def _train_step(x, c, target, w, m, v):
    xi, yi, ci = _my_place()
    chip = 2 * xi + yi
    dev = 4 * xi + 2 * yi + ci
    mod_cols = N_SUB * 3 * D_MODEL // N_CHIPS
    norm_cols = D_MODEL // N_CHIPS

    cuts = [CUTS[n] for n, _ in LAYER_MATS]
    shards = [_layer_shards(w, l) for l in range(DEPTH)]
    full = [[None] * len(LAYER_MATS) for _ in range(DEPTH)]

    def gather(keys):
        return _gather_exchange([shards[l][i] for l, i in keys], [cuts[i] for _, i in keys])

    def store(keys, mats):
        for (l, i), a in zip(keys, mats):
            full[l][i] = a

    first = lambda l: [(l, i) for i in FIRST_MATS]
    mixer = lambda l: [(l, i) for i in MIXER_MATS]
    late = lambda l: [(l, i) for i in LATE_MATS]
    store(first(0), _run_exchange(gather(first(0)), name="gather_weights_first"))
    c_all = _all_gather8(jnp.pad(c, ((0, SUBLANES - 1), (0, 0))), name="gather_c")[::SUBLANES]
    mod_b_cols = lax.dynamic_slice_in_dim(w["mod_b"], chip * mod_cols, mod_cols, axis=1)[:, None, :]
    mod_part = _mod_partial(c_all, w["mod_w"], mod_b_cols, name="mod_partial")
    small_fwd = jnp.concatenate([mod_part.reshape(-1), w["norm_pre"].reshape(-1), w["norm_post"].reshape(-1)])
    n_mod, n_norm = mod_part.size, w["norm_pre"].size
    sf_all = _all_gather8(_pad_rows(small_fwd), name="gather_mod").reshape(N_DEV, -1)
    sf_chips = sf_all[::2]
    mod_all = jnp.concatenate(
        [sf_chips[k, :n_mod].reshape(DEPTH, N_DEV, mod_cols) for k in range(N_CHIPS)], axis=2)
    mod_mine = lax.dynamic_index_in_dim(mod_all, dev, axis=1, keepdims=False).reshape(DEPTH, N_SUB, 3, D_MODEL)
    norm_pre = jnp.concatenate(
        [sf_chips[k, n_mod:n_mod + n_norm].reshape(DEPTH, N_SUB, norm_cols) for k in range(N_CHIPS)], axis=2)
    norm_post = jnp.concatenate(
        [sf_chips[k, n_mod + n_norm:n_mod + 2 * n_norm].reshape(DEPTH, N_SUB, norm_cols) for k in range(N_CHIPS)],
        axis=2)

    saved, layer_w, ssm_prep, ssm_vjp = [], [], [], []
    h = x
    for l in range(DEPTH):
        lw = dict(ffn=[(full[l][0][0], full[l][2][0]), None])
        prep, vjp = jax.vjp(_ssm_discretize, *[w[n][l] for n in SSM_NAMES])
        layer_w.append(lw)
        ssm_prep.append(prep)
        ssm_vjp.append(vjp)
        h, s0, arrived = _ffn_fwd(h, mod_mine[l, 0], norm_pre[l, 0], norm_post[l, 0], *lw["ffn"][0], tag=f"l{l}a",
                                  exchange=gather(mixer(l)))
        store(mixer(l), arrived)
        mix_in, glu, attn_out, mix_out = full[l][4:]
        w_u, w_qkv, w_f, w_gab = _split_mix_w_in(mix_in.transpose(1, 0, 2).reshape(D_MODEL, IN_WIDTH))
        lw["mix"] = dict(u=w_u, qkv=w_qkv, f=w_f, gab=w_gab, glu=glu[0], attn_out=attn_out[0], out=mix_out[0])
        coming = late(l) + (first(l + 1) if l + 1 < DEPTH else [])
        h, s1, arrived = _mixer_fwd(h, mod_mine[l, 1], norm_pre[l, 1], norm_post[l, 1], lw["mix"], prep,
                                    w["forget_b"][l], tag=f"l{l}m", exchange=gather(coming))
        store(coming, arrived)
        lw["ffn"][1] = (full[l][1][0], full[l][3][0])
        h, s2, _ = _ffn_fwd(h, mod_mine[l, 2], norm_pre[l, 2], norm_post[l, 2], *lw["ffn"][1], tag=f"l{l}b")
        saved.append((s0, s1, s2))
    dh, loss8 = _loss_head(h, target, name="loss_head")

    c_idx = ci.reshape(1).astype(jnp.int32)
    chip_idx = chip.reshape(1).astype(jnp.int32)
    places = []
    dests = {n: lax.empty(((DEPTH,) if j is None else (DEPTH, 2)) + _part_shape(CUTS[n], slab=True), F32)
             for n, j in LAYER_MATS}
    g_small = {n: [None] * DEPTH for n in SMALL_NAMES}
    dmod, dnpre, dnpost = [], [], []
    pending = []

    def swap_of(l, idx, mats):
        cs = [cuts[i] for i in idx]
        mats = [a.reshape(cut.shape) for a, cut in zip(mats, cs)]

        def added(from_sibling):
            for k, i in enumerate(idx):
                f, b = _add_half(mats[k], from_sibling[k], cs[k], c_idx, name=f"grad_add_l{l}_{i}")
                pending.append((l, i, f, b))

        return _swap_exchange(mats, cs), added

    def partials():
        return _partials_exchange([p[3] for p in pending], [cuts[p[1]] for p in pending])

    def then_partials(added):
        def after_swap(from_sibling):
            added(from_sibling)
            return partials()
        return after_swap

    def end_reduce(parts):
        for (l, i, f, _), part in zip(pending, parts):
            n, j = LAYER_MATS[i]
            place = (l,) if j is None else (l, j)
            dests[n] = _sum_slab(f, part, cuts[i], dests[n], place, chip_idx, c_idx, name=f"grad_sum_l{l}_{i}")
            places.append((n, place))
        pending.clear()

    swap, added = None, None
    for l in reversed(range(DEPTH)):
        lw = layer_w[l]
        dh, dm2, dp2, dq2, dwin2, dwout2, _ = _ffn_bwd(
            dh, saved[l][2], mod_mine[l, 2], norm_pre[l, 2], norm_post[l, 2], *lw["ffn"][1], tag=f"l{l}b",
            swap=swap, after_swap=added)
        swap, added = swap_of(l, LATE_MATS, [dwin2, dwout2])
        dh, dm1, dp1, dq1, gmix, dssm, parts = _mixer_bwd(
            dh, saved[l][1], mod_mine[l, 1], norm_pre[l, 1], norm_post[l, 1], lw["mix"], ssm_prep[l],
            tag=f"l{l}m", swap=swap, after_swap=then_partials(added))
        end_reduce(parts)
        dmix_in = gmix["mix_w_in"].reshape(D_MODEL, N_CHIPS, IN_WIDTH // N_CHIPS).transpose(1, 0, 2)
        swap, added = swap_of(l, MIXER_MATS, [dmix_in, gmix["glu_w"], gmix["attn_w_out"], gmix["mix_w_out"]])
        dh, dm0, dp0, dq0, dwin0, dwout0, parts = _ffn_bwd(
            dh, saved[l][0], mod_mine[l, 0], norm_pre[l, 0], norm_post[l, 0], *lw["ffn"][0], tag=f"l{l}a",
            swap=swap, after_swap=then_partials(added))
        end_reduce(parts)
        swap, added = swap_of(l, FIRST_MATS, [dwin0, dwout0])
        dmod.insert(0, jnp.stack([dm0, dm1, dm2]))
        dnpre.insert(0, jnp.stack([dp0, dp1, dp2]))
        dnpost.insert(0, jnp.stack([dq0, dq1, dq2]))
        g_small["forget_b"][l] = gmix["forget_b"]
        for n, g in zip(SSM_NAMES, ssm_vjp[l](dssm)):
            g_small[n][l] = g
    added(_run_exchange(swap, name="grad_swap_last"))
    end_reduce(_run_exchange(partials(), name="grad_partials_last"))
    grad_x = dh
    g_small = {n: jnp.stack(g) for n, g in g_small.items()}

    small = [loss8[0, :1], jnp.stack(dmod).reshape(-1), jnp.stack(dnpre).reshape(-1), jnp.stack(dnpost).reshape(-1)]
    small += [g_small[n].reshape(-1) for n in SMALL_NAMES]
    sizes = [int(s.size) for s in small]
    offs = np.concatenate([[0], np.cumsum(sizes)])
    sb_all, sb_sum = _all_gather8(_pad_rows(jnp.concatenate(small)), name="gather_small_grads", with_sum=True)
    sb_sum = sb_sum.reshape(-1)
    take = lambda i: sb_sum[int(offs[i]):int(offs[i + 1])]
    loss = take(0)[0]
    grads = {"mod_b": take(1).reshape(DEPTH, N_SUB * 3 * D_MODEL)}
    dnorm_pre_full = take(2).reshape(DEPTH, N_SUB, D_MODEL)
    dnorm_post_full = take(3).reshape(DEPTH, N_SUB, D_MODEL)
    grads["norm_pre"] = lax.dynamic_slice_in_dim(dnorm_pre_full, chip * norm_cols, norm_cols, axis=2)
    grads["norm_post"] = lax.dynamic_slice_in_dim(dnorm_post_full, chip * norm_cols, norm_cols, axis=2)
    for i, n in enumerate(SMALL_NAMES):
        grads[n] = take(4 + i).reshape(w[n].shape)
    dmod_all = sb_all.reshape(N_DEV, -1)[:, int(offs[1]):int(offs[2])].reshape(N_DEV, DEPTH, N_SUB * 3 * D_MODEL)
    dmod_cols = lax.dynamic_slice_in_dim(dmod_all, chip * mod_cols, mod_cols, axis=2).transpose(1, 0, 2)
    grads["mod_w"] = _mod_wgrad(c_all.T, dmod_cols, name="mod_wgrad")

    shared = _share_all(dests, [CUTS[n] for n, _ in places], places, name="grad_share")
    for n, g in zip(dests, shared):
        grads[n] = g.reshape(w[n].shape)

    delta, new_m, new_v = {}, {}, {}
    for n in WEIGHT_NAMES:
        delta[n], new_m[n], new_v[n] = _adamw(w[n], grads[n], m[n], v[n], name=f"adamw_{n}")
    outs = [loss, grad_x[None]]
    for group in (grads, delta, new_m, new_v):
        outs += [group[n] for n in WEIGHT_NAMES]
    return tuple(outs)


def kernel(x, c, mod_w, mod_b, norm_pre, norm_post, ffn_w_in, ffn_w_out, mix_w_in, forget_b, ssm_a_re, ssm_a_im, ssm_log_dt, ssm_b_re, ssm_b_im, ssm_c_re, ssm_c_im, ssm_d, glu_w, attn_w_out, mix_w_out, loss_target, m_mod_w, m_mod_b, m_norm_pre, m_norm_post, m_ffn_w_in, m_ffn_w_out, m_mix_w_in, m_forget_b, m_ssm_a_re, m_ssm_a_im, m_ssm_log_dt, m_ssm_b_re, m_ssm_b_im, m_ssm_c_re, m_ssm_c_im, m_ssm_d, m_glu_w, m_attn_w_out, m_mix_w_out, v_mod_w, v_mod_b, v_norm_pre, v_norm_post, v_ffn_w_in, v_ffn_w_out, v_mix_w_in, v_forget_b, v_ssm_a_re, v_ssm_a_im, v_ssm_log_dt, v_ssm_b_re, v_ssm_b_im, v_ssm_c_re, v_ssm_c_im, v_ssm_d, v_glu_w, v_attn_w_out, v_mix_w_out):
    w = dict(mod_w=mod_w, mod_b=mod_b, norm_pre=norm_pre, norm_post=norm_post, ffn_w_in=ffn_w_in,
             ffn_w_out=ffn_w_out, mix_w_in=mix_w_in, forget_b=forget_b, ssm_a_re=ssm_a_re, ssm_a_im=ssm_a_im,
             ssm_log_dt=ssm_log_dt, ssm_b_re=ssm_b_re, ssm_b_im=ssm_b_im, ssm_c_re=ssm_c_re, ssm_c_im=ssm_c_im,
             ssm_d=ssm_d, glu_w=glu_w, attn_w_out=attn_w_out, mix_w_out=mix_w_out)
    m = dict(mod_w=m_mod_w, mod_b=m_mod_b, norm_pre=m_norm_pre, norm_post=m_norm_post, ffn_w_in=m_ffn_w_in,
             ffn_w_out=m_ffn_w_out, mix_w_in=m_mix_w_in, forget_b=m_forget_b, ssm_a_re=m_ssm_a_re,
             ssm_a_im=m_ssm_a_im, ssm_log_dt=m_ssm_log_dt, ssm_b_re=m_ssm_b_re, ssm_b_im=m_ssm_b_im,
             ssm_c_re=m_ssm_c_re, ssm_c_im=m_ssm_c_im, ssm_d=m_ssm_d, glu_w=m_glu_w, attn_w_out=m_attn_w_out,
             mix_w_out=m_mix_w_out)
    v = dict(mod_w=v_mod_w, mod_b=v_mod_b, norm_pre=v_norm_pre, norm_post=v_norm_post, ffn_w_in=v_ffn_w_in,
             ffn_w_out=v_ffn_w_out, mix_w_in=v_mix_w_in, forget_b=v_forget_b, ssm_a_re=v_ssm_a_re,
             ssm_a_im=v_ssm_a_im, ssm_log_dt=v_ssm_log_dt, ssm_b_re=v_ssm_b_re, ssm_b_im=v_ssm_b_im,
             ssm_c_re=v_ssm_c_re, ssm_c_im=v_ssm_c_im, ssm_d=v_ssm_d, glu_w=v_glu_w, attn_w_out=v_attn_w_out,
             mix_w_out=v_mix_w_out)
    return _train_step(x[0], c, loss_target[0], w, m, v)
```

```python
import functools
import math
from typing import NamedTuple

import jax
import jax.numpy as jnp
import numpy as np
from jax import lax
from jax.experimental import pallas as pl
from jax.experimental.pallas import tpu as pltpu

F32 = jnp.float32
BF16 = jnp.bfloat16

D_MODEL = 1024
DEPTH = 2
SSM_WIDTH = 512
SSM_GROUP = 16
SSM_GROUPS = 32
SSM_STATE = 64
SSM_FLAT = SSM_GROUPS * SSM_STATE
SSM_BLOCKS = 4
ATTN_HEADS = 8
HEAD_DIM = 64
ATTN_WIDTH = 512
D_FF = 2816
FFN_RES = 0.5
N_SUB = 3
RMS_EPS = 1e-6
N_CHIPS = 4
N_DEV = 8

ADAM_LR = 0.001
ADAM_B1 = 0.9
ADAM_B2 = 0.999
ADAM_EPS = 1e-08
ADAM_WD = 0.01
ADAM_STEP = 10

LANES = 128
SUBLANES = 8
VMEM_LIMIT = 52 * 1024 * 1024
MESH = pl.DeviceIdType.MESH

NN = (((1,), (0,)), ((), ()))
NT = (((1,), (1,)), ((), ()))
TN = (((0,), (0,)), ((), ()))


def _tile(dim, target, align=LANES):
    best = None
    t = align
    while t <= min(dim, target):
        if dim % t == 0:
            best = t
        t += align
    return dim if best is None else best


def _params(*sem):
    return pltpu.CompilerParams(dimension_semantics=sem, vmem_limit_bytes=VMEM_LIMIT)


def _mm(a, b, *, name, ta=False, tb=False, out_dtype=F32, bias=None, bscale=None, b_k0=0,
        out_n=None, out_j0=0, into=None, tm=512, tn=1024, tk=1024):
    M, K = (a.shape[1], a.shape[0]) if ta else a.shape
    N = b.shape[0] if tb else b.shape[1]
    assert b_k0 + K <= (b.shape[1] if tb else b.shape[0]), (a.shape, b.shape, ta, tb)
    tm, tn, tk = _tile(M, tm), _tile(N, tn), _tile(K, tk)
    nk = K // tk
    assert b_k0 % tk == 0
    kb0 = b_k0 // tk
    dn = (((0 if ta else 1,), (1 if tb else 0,)), ((), ()))
    has_bias, has_scale = bias is not None, bscale is not None

    def body(*refs):
        a_ref, b_ref = refs[0], refs[1]
        pos = 2
        bias_ref = scale_ref = None
        if has_bias:
            bias_ref = refs[pos]
            pos += 1
        if has_scale:
            scale_ref = refs[pos]
            pos += 1
        if into is not None:
            pos += 1
        o_ref = refs[pos]
        acc_ref = refs[pos + 1] if nk > 1 else None

        def finish(r):
            if has_bias:
                extra = bias_ref[...].astype(F32)
                if has_scale:
                    extra = extra * scale_ref[...]
                r = r + extra
            o_ref[...] = r.astype(out_dtype)

        part = lax.dot_general(a_ref[...].astype(BF16), b_ref[...].astype(BF16), dn,
                               preferred_element_type=F32)
        if nk == 1:
            finish(part)
        else:
            k = pl.program_id(2)

            @pl.when(k == 0)
            def _():
                acc_ref[...] = part

            @pl.when(k > 0)
            def _():
                acc_ref[...] += part

            @pl.when(k == nk - 1)
            def _():
                finish(acc_ref[...])

    a_spec = pl.BlockSpec((tk, tm), lambda j, i, k: (k, i)) if ta else pl.BlockSpec((tm, tk), lambda j, i, k: (i, k))
    b_spec = (pl.BlockSpec((tn, tk), lambda j, i, k: (j, kb0 + k)) if tb
              else pl.BlockSpec((tk, tn), lambda j, i, k: (kb0 + k, j)))
    in_specs = [a_spec, b_spec]
    args = [a, b]
    if has_bias:
        in_specs.append(pl.BlockSpec((tm, tn), lambda j, i, k: (i, j)))
        args.append(bias)
    if has_scale:
        in_specs.append(pl.BlockSpec((1, tn), lambda j, i, k: (0, j)))
        args.append(bscale)
    aliases = {}
    if into is not None:
        in_specs.append(pl.BlockSpec(memory_space=pl.ANY))
        args.append(into)
        aliases = {len(args) - 1: 0}
    out_n = N if out_n is None else out_n
    assert out_j0 % tn == 0
    jb0 = out_j0 // tn
    return pl.pallas_call(
        body, name=name,
        grid=(N // tn, M // tm, nk),
        in_specs=in_specs,
        out_specs=pl.BlockSpec((tm, tn), lambda j, i, k: (i, jb0 + j)),
        out_shape=jax.ShapeDtypeStruct((M, out_n), out_dtype),
        scratch_shapes=[pltpu.VMEM((tm, tn), F32)] if nk > 1 else [],
        input_output_aliases=aliases,
        compiler_params=_params("parallel", "parallel", "arbitrary"),
    )(*args)


def _mm_bd(a_list, b_list, *, name, tb=False, out_dtype=F32, bias=None, bscale=None, tm=1024):
    G = b_list[0].shape[0]
    Kb, Nb = (b_list[0].shape[2], b_list[0].shape[1]) if tb else b_list[0].shape[1:]
    M = a_list[0].shape[0]
    tm = _tile(M, tm)
    na, nb = len(a_list), len(b_list)
    n_out = nb if na == 1 else 1
    dn = NT if tb else NN
    has_bias, has_scale = bias is not None, bscale is not None

    def body(*refs):
        a_refs, b_refs = refs[:na], refs[na:na + nb]
        pos = na + nb
        bias_ref = scale_ref = None
        if has_bias:
            bias_ref = refs[pos]
            pos += 1
        if has_scale:
            scale_ref = refs[pos]
            pos += 1
        o_refs = refs[pos:]
        prods = [lax.dot_general(a_refs[min(i, na - 1)][...].astype(BF16), b_refs[i][...].astype(BF16), dn,
                                 preferred_element_type=F32) for i in range(nb)]
        outs = prods if n_out == nb else [functools.reduce(jnp.add, prods)]
        for o_ref, r in zip(o_refs, outs):
            if has_bias:
                extra = bias_ref[...].astype(F32)
                r = r + (extra * scale_ref[...] if has_scale else extra)
            o_ref[...] = r.astype(out_dtype)

    a_spec = pl.BlockSpec((tm, Kb), lambda g, i: (i, g))
    b_spec = pl.BlockSpec((None,) + b_list[0].shape[1:], lambda g, i: (g, 0, 0))
    o_spec = pl.BlockSpec((tm, Nb), lambda g, i: (i, g))
    in_specs = [a_spec] * na + [b_spec] * nb
    args = list(a_list) + list(b_list)
    if has_bias:
        in_specs.append(o_spec)
        args.append(bias)
    if has_scale:
        in_specs.append(pl.BlockSpec((1, Nb), lambda g, i: (0, g)))
        args.append(bscale)
    sds = jax.ShapeDtypeStruct((M, G * Nb), out_dtype)
    res = pl.pallas_call(
        body, name=name, grid=(G, M // tm), in_specs=in_specs, out_specs=[o_spec] * n_out,
        out_shape=[sds] * n_out, compiler_params=_params("parallel", "parallel"),
    )(*args)
    return res[0] if n_out == 1 else res


def _mm_bd_t(a, b, G, *, name, tk=1024):
    K, Mb, Nb = a.shape[0], a.shape[1] // G, b.shape[1] // G
    tk = _tile(K, tk)
    nk = K // tk

    def body(a_ref, b_ref, o_ref, acc_ref):
        k = pl.program_id(1)
        part = lax.dot_general(a_ref[...].astype(BF16), b_ref[...].astype(BF16), TN, preferred_element_type=F32)

        @pl.when(k == 0)
        def _():
            acc_ref[...] = part

        @pl.when(k > 0)
        def _():
            acc_ref[...] += part

        @pl.when(k == nk - 1)
        def _():
            o_ref[...] = acc_ref[...]

    return pl.pallas_call(
        body, name=name, grid=(G, nk),
        in_specs=[pl.BlockSpec((tk, Mb), lambda g, k: (k, g)), pl.BlockSpec((tk, Nb), lambda g, k: (k, g))],
        out_specs=pl.BlockSpec((None, Mb, Nb), lambda g, k: (g, 0, 0)),
        out_shape=jax.ShapeDtypeStruct((G, Mb, Nb), F32),
        scratch_shapes=[pltpu.VMEM((Mb, Nb), F32)], compiler_params=_params("parallel", "arbitrary"),
    )(a, b)


def _sigmoid(x):
    return 0.5 * jnp.tanh(0.5 * x) + 0.5


def _mm_swiglu(h, w_in, *, name, tm=512, tn=1408, exchange=None):
    M, K = h.shape
    N = w_in.shape[1] // 2
    tm, tn = _tile(M, tm), _tile(N, tn)
    nj = N // tn

    def body(h_ref, wg_ref, wu_ref, g_ref, u_ref, a_ref):
        hv = h_ref[...]
        g = jnp.dot(hv, wg_ref[...], preferred_element_type=F32)
        u = jnp.dot(hv, wu_ref[...], preferred_element_type=F32)
        g_ref[...] = g.astype(BF16)
        u_ref[...] = u.astype(BF16)
        a_ref[...] = (g * _sigmoid(g) * u).astype(BF16)

    o_spec = pl.BlockSpec((tm, tn), lambda j, i: (i, j))
    sds = jax.ShapeDtypeStruct((M, N), BF16)
    return _call_with_exchange(
        body, exchange, name=name, grid=(nj, M // tm),
        in_specs=[pl.BlockSpec((tm, K), lambda j, i: (i, 0)), pl.BlockSpec((K, tn), lambda j, i: (0, j)),
                  pl.BlockSpec((K, tn), lambda j, i: (0, nj + j))],
        out_specs=[o_spec, o_spec, o_spec], out_shape=[sds, sds, sds], operands=(h, w_in, w_in))


def _mm_swiglu_bwd(dy, w_out, gate, up, *, name, tm=512, tn=1408, exchange=None):
    M, K = dy.shape
    N = w_out.shape[0]
    tm, tn = _tile(M, tm), _tile(N, tn)

    def body(dy_ref, w_ref, g_ref, u_ref, dg_ref, du_ref):
        dact = lax.dot_general(dy_ref[...], w_ref[...], NT, preferred_element_type=F32)
        g = g_ref[...].astype(F32)
        u = u_ref[...].astype(F32)
        sig = _sigmoid(g)
        dg_ref[...] = (dact * u * (sig * (1.0 + g * (1.0 - sig)))).astype(BF16)
        du_ref[...] = (dact * (g * sig)).astype(BF16)

    t_spec = pl.BlockSpec((tm, tn), lambda j, i: (i, j))
    sds = jax.ShapeDtypeStruct((M, N), BF16)
    return _call_with_exchange(
        body, exchange, name=name, grid=(N // tn, M // tm),
        in_specs=[pl.BlockSpec((tm, K), lambda j, i: (i, 0)), pl.BlockSpec((tn, K), lambda j, i: (j, 0)),
                  t_spec, t_spec],
        out_specs=[t_spec, t_spec], out_shape=[sds, sds], operands=(dy, w_out, gate, up))


ROW_TILE = 1024
MERGE_TILE = 512
GELU_TILE = 2048


def _colsum8(v):
    return jnp.sum(v.reshape(v.shape[0] // SUBLANES, SUBLANES, v.shape[1]), axis=0)


def _finish_colsums(step, last, refs):
    @pl.when(step == last)
    def _():
        for r in refs:
            r[...] = jnp.broadcast_to(jnp.sum(r[...], axis=0, keepdims=True), r.shape)


def _row_spec(t, d):
    return pl.BlockSpec((t, d), lambda i: (i, 0))


def _vec_spec(d, rows=1):
    return pl.BlockSpec((rows, d), lambda i: (0, 0))


def _prenorm(x, g, sc, sh, *, name):
    L, D = x.shape
    t = _tile(L, ROW_TILE, SUBLANES)

    def body(x_ref, g_ref, sc_ref, sh_ref, h_ref):
        xv = x_ref[...]
        r = lax.rsqrt(jnp.mean(xv * xv, axis=-1, keepdims=True) + RMS_EPS)
        h_ref[...] = (((xv * r) * g_ref[...]) * (1.0 + sc_ref[...]) + sh_ref[...]).astype(BF16)

    return pl.pallas_call(
        body, name=name, grid=(L // t,),
        in_specs=[_row_spec(t, D), _vec_spec(D), _vec_spec(D), _vec_spec(D)],
        out_specs=_row_spec(t, D), out_shape=jax.ShapeDtypeStruct((L, D), BF16),
        compiler_params=_params("parallel"),
    )(x, g, sc, sh)


def _mm_postnorm(a, b, x, g, gate, res_w, *, name, tm=512):
    M, K = a.shape
    D = b.shape[1]
    tm = _tile(M, tm, SUBLANES)

    def body(a_ref, b_ref, x_ref, g_ref, gate_ref, y_ref, o_ref):
        yv = jnp.dot(a_ref[...], b_ref[...], preferred_element_type=F32)
        y_ref[...] = yv
        r = lax.rsqrt(jnp.mean(yv * yv, axis=-1, keepdims=True) + RMS_EPS)
        o_ref[...] = x_ref[...] + (res_w * gate_ref[...]) * ((yv * r) * g_ref[...])

    sds = jax.ShapeDtypeStruct((M, D), F32)
    return pl.pallas_call(
        body, name=name, grid=(M // tm,),
        in_specs=[_row_spec(tm, K), pl.BlockSpec((K, D), lambda i: (0, 0)), _row_spec(tm, D), _vec_spec(D),
                  _vec_spec(D)],
        out_specs=[_row_spec(tm, D), _row_spec(tm, D)], out_shape=[sds, sds],
        compiler_params=_params("parallel"),
    )(a, b, x, g, gate)


def _postnorm_bwd(dxo, y, g, gate, res_w, *, name, exchange=None):
    L, D = y.shape
    t = _tile(L, ROW_TILE, SUBLANES)
    n = L // t

    def body(dxo_ref, y_ref, g_ref, gate_ref, dy_ref, dgate_ref, dg_ref):
        i = pl.program_id(0)

        @pl.when(i == 0)
        def _():
            dgate_ref[...] = jnp.zeros_like(dgate_ref)
            dg_ref[...] = jnp.zeros_like(dg_ref)

        yv = y_ref[...]
        dv = dxo_ref[...]
        gv = g_ref[...]
        r = lax.rsqrt(jnp.mean(yv * yv, axis=-1, keepdims=True) + RMS_EPS)
        yn = yv * r
        dgate_ref[...] += _colsum8(dv * (res_w * (yn * gv)))
        do = dv * (res_w * gate_ref[...])
        dg_ref[...] += _colsum8(do * yn)
        dyn = do * gv
        dy_ref[...] = (r * (dyn - yn * jnp.mean(dyn * yn, axis=-1, keepdims=True))).astype(BF16)
        _finish_colsums(i, n - 1, (dgate_ref, dg_ref))

    sum_sds = jax.ShapeDtypeStruct((SUBLANES, D), F32)
    return _call_with_exchange(
        body, exchange, name=name, grid=(n,), sem=("arbitrary",),
        in_specs=[_row_spec(t, D), _row_spec(t, D), _vec_spec(D), _vec_spec(D)],
        out_specs=[_row_spec(t, D), _vec_spec(D, SUBLANES), _vec_spec(D, SUBLANES)],
        out_shape=[jax.ShapeDtypeStruct((L, D), BF16), sum_sds, sum_sds], operands=(dxo, y, g, gate))


def _prenorm_bwd(x, dh, dxres, g, sc, *, name):
    L, D = x.shape
    t = _tile(L, ROW_TILE, SUBLANES)
    n = L // t

    def body(x_ref, dh_ref, dxr_ref, g_ref, sc_ref, dx_ref, dsh_ref, dsc_ref, dg_ref):
        i = pl.program_id(0)

        @pl.when(i == 0)
        def _():
            dsh_ref[...] = jnp.zeros_like(dsh_ref)
            dsc_ref[...] = jnp.zeros_like(dsc_ref)
            dg_ref[...] = jnp.zeros_like(dg_ref)

        xv = x_ref[...]
        dhv = dh_ref[...].astype(F32)
        gv = g_ref[...]
        one_sc = 1.0 + sc_ref[...]
        r = lax.rsqrt(jnp.mean(xv * xv, axis=-1, keepdims=True) + RMS_EPS)
        xn = xv * r
        tt = dhv * xn
        dsh_ref[...] += _colsum8(dhv)
        dsc_ref[...] += _colsum8(tt * gv)
        dg_ref[...] += _colsum8(tt * one_sc)
        dxn = dhv * (gv * one_sc)
        dx_ref[...] = dxr_ref[...] + r * (dxn - xn * jnp.mean(dxn * xn, axis=-1, keepdims=True))
        _finish_colsums(i, n - 1, (dsh_ref, dsc_ref, dg_ref))

    sum_sds = jax.ShapeDtypeStruct((SUBLANES, D), F32)
    sum_spec = _vec_spec(D, SUBLANES)
    return pl.pallas_call(
        body, name=name, grid=(n,),
        in_specs=[_row_spec(t, D), _row_spec(t, D), _row_spec(t, D), _vec_spec(D), _vec_spec(D)],
        out_specs=[_row_spec(t, D), sum_spec, sum_spec, sum_spec],
        out_shape=[jax.ShapeDtypeStruct((L, D), F32), sum_sds, sum_sds, sum_sds],
        compiler_params=_params("arbitrary"),
    )(x, dh, dxres, g, sc)


def _loss_head(y, target, *, name):
    L, D = y.shape
    t = _tile(L, ROW_TILE, SUBLANES)
    n = L // t

    def body(y_ref, t_ref, dy_ref, loss_ref):
        i = pl.program_id(0)

        @pl.when(i == 0)
        def _():
            loss_ref[...] = jnp.zeros_like(loss_ref)

        e = y_ref[...] - t_ref[...]
        dy_ref[...] = e * (1.0 / D)
        part = jnp.sum(jnp.mean(e * e, axis=-1, keepdims=True), axis=0, keepdims=True)
        loss_ref[...] += jnp.broadcast_to(0.5 * part, loss_ref.shape)

    return pl.pallas_call(
        body, name=name, grid=(n,),
        in_specs=[_row_spec(t, D), _row_spec(t, D)],
        out_specs=[_row_spec(t, D), pl.BlockSpec((SUBLANES, LANES), lambda i: (0, 0))],
        out_shape=[jax.ShapeDtypeStruct((L, D), F32), jax.ShapeDtypeStruct((SUBLANES, LANES), F32)],
        compiler_params=_params("arbitrary"),
    )(y, target)


GELU_C = math.sqrt(2.0 / math.pi)


def _gelu_fwd(y, *, name):
    L, W = y.shape
    t = _tile(L, GELU_TILE, SUBLANES)

    def body(y_ref, o_ref):
        v = y_ref[...]
        o_ref[...] = (0.5 * v * (1.0 + jnp.tanh(GELU_C * (v + 0.044715 * (v * v * v))))).astype(BF16)

    return pl.pallas_call(
        body, name=name, grid=(L // t,), in_specs=[_row_spec(t, W)], out_specs=_row_spec(t, W),
        out_shape=jax.ShapeDtypeStruct((L, W), BF16), compiler_params=_params("parallel"),
    )(y)


def _gelu_bwd(dgl, y, u, dskip, *, name):
    L, W = y.shape
    t = _tile(L, GELU_TILE, SUBLANES)
    n = L // t

    def body(dgl_ref, y_ref, u_ref, d_ref, dy_ref, sk_ref, dd_ref):
        i = pl.program_id(0)

        @pl.when(i == 0)
        def _():
            dd_ref[...] = jnp.zeros_like(dd_ref)

        v = y_ref[...]
        inner = GELU_C * (v + 0.044715 * (v * v * v))
        th = jnp.tanh(inner)
        dgelu = 0.5 * (1.0 + th) + 0.5 * v * (1.0 - th * th) * (GELU_C * (1.0 + 3.0 * 0.044715 * (v * v)))
        dy = dgl_ref[...] * dgelu
        dy_ref[...] = dy.astype(BF16)
        sk_ref[...] = dy * d_ref[...]
        dd_ref[...] += _colsum8(dy * u_ref[...])
        _finish_colsums(i, n - 1, (dd_ref,))

    return pl.pallas_call(
        body, name=name, grid=(n,),
        in_specs=[_row_spec(t, W), _row_spec(t, W), _row_spec(t, W), _vec_spec(W)],
        out_specs=[_row_spec(t, W), _row_spec(t, W), _vec_spec(W, SUBLANES)],
        out_shape=[jax.ShapeDtypeStruct((L, W), BF16), jax.ShapeDtypeStruct((L, W), F32),
                   jax.ShapeDtypeStruct((SUBLANES, W), F32)],
        compiler_params=_params("arbitrary"),
    )(dgl, y, u, dskip)


def _merge_fwd(z, yb, gab, *, name):
    L, D = yb.shape
    t = _tile(L, MERGE_TILE, SUBLANES)

    def body(z_ref, yb_ref, gab_ref, o_ref):
        ya = z_ref[:, :D] * _sigmoid(z_ref[:, D:])
        o_ref[...] = (_sigmoid(gab_ref[:, :D]) * ya + _sigmoid(gab_ref[:, D:]) * yb_ref[...]).astype(BF16)

    return pl.pallas_call(
        body, name=name, grid=(L // t,),
        in_specs=[_row_spec(t, 2 * D), _row_spec(t, D), _row_spec(t, 2 * D)],
        out_specs=_row_spec(t, D), out_shape=jax.ShapeDtypeStruct((L, D), BF16),
        compiler_params=_params("parallel"),
    )(z, yb, gab)


def _merge_bwd(dm, z, yb, gab, *, name, exchange=None):
    L, D = yb.shape
    t = _tile(L, MERGE_TILE, SUBLANES)

    def body(dm_ref, z_ref, yb_ref, gab_ref, dz_ref, dyb_ref, dgab_ref):
        dmv = dm_ref[...]
        zv = z_ref[:, :D]
        sz = _sigmoid(z_ref[:, D:])
        sa = _sigmoid(gab_ref[:, :D])
        sb = _sigmoid(gab_ref[:, D:])
        ybv = yb_ref[...]
        dya = dmv * sa
        dz_ref[:, :D] = (dya * sz).astype(BF16)
        dz_ref[:, D:] = (dya * zv * (sz * (1.0 - sz))).astype(BF16)
        dyb_ref[...] = (dmv * sb).astype(BF16)
        dgab_ref[:, :D] = (dmv * (zv * sz) * (sa * (1.0 - sa))).astype(BF16)
        dgab_ref[:, D:] = (dmv * ybv * (sb * (1.0 - sb))).astype(BF16)

    return _call_with_exchange(
        body, exchange, name=name, grid=(L // t,),
        in_specs=[_row_spec(t, D), _row_spec(t, 2 * D), _row_spec(t, D), _row_spec(t, 2 * D)],
        out_specs=[_row_spec(t, 2 * D), _row_spec(t, D), _row_spec(t, 2 * D)],
        out_shape=[jax.ShapeDtypeStruct((L, 2 * D), BF16), jax.ShapeDtypeStruct((L, D), BF16),
                   jax.ShapeDtypeStruct((L, 2 * D), BF16)], operands=(dm, z, yb, gab))


SCAN_W = 1024
SCAN_T = 512


def _interleave(x):
    L, W = x.shape
    seg = SCAN_T // SUBLANES
    return x.reshape(L // SCAN_T, SUBLANES, seg, W).transpose(0, 2, 1, 3).reshape(L, W)


def _deinterleave(x):
    L, W = x.shape
    seg = SCAN_T // SUBLANES
    return x.reshape(L // SCAN_T, seg, SUBLANES, W).transpose(0, 2, 1, 3).reshape(L, W)


def _power_table(a, b, pr_tab, pi_tab, n):
    def fill(k, carry):
        pr, pi = carry
        pr_tab[k] = pr
        pi_tab[k] = pi
        return a * pr - b * pi, a * pi + b * pr

    lax.fori_loop(0, n, fill, (a, b))


def _rows_to_tile(rows):
    w = rows[0].shape[1]
    sub = lax.broadcasted_iota(jnp.int32, (SUBLANES, w), 0)
    tile = jnp.broadcast_to(rows[0], (SUBLANES, w))
    for j in range(1, SUBLANES):
        tile = jnp.where(sub == j, jnp.broadcast_to(rows[j], (SUBLANES, w)), tile)
    return tile


def _ssm_scan_fwd(bu_re, bu_im, lam_re, lam_im, *, name):
    L, S = bu_re.shape
    w, t = _tile(S, SCAN_W), SCAN_T
    seg = t // SUBLANES

    def body(br_ref, bi_ref, lr_ref, li_ref, sr_ref, si_ref, pr_tab, pi_tab, cr_ref, ci_ref):
        a = jnp.broadcast_to(lr_ref[...], (SUBLANES, w))
        b = jnp.broadcast_to(li_ref[...], (SUBLANES, w))

        @pl.when(pl.program_id(1) == 0)
        def _():
            cr_ref[...] = jnp.zeros_like(cr_ref)
            ci_ref[...] = jnp.zeros_like(ci_ref)
            _power_table(a, b, pr_tab, pi_tab, seg)

        def local_scan(i, carry):
            sr, si = carry
            rows = pl.ds(pl.multiple_of(i * SUBLANES, SUBLANES), SUBLANES)
            nr = a * sr - b * si + br_ref[rows, :]
            ni = a * si + b * sr + bi_ref[rows, :]
            sr_ref[rows, :] = nr
            si_ref[rows, :] = ni
            return nr, ni

        zero = jnp.zeros((SUBLANES, w), F32)
        fr, fi = lax.fori_loop(0, seg, local_scan, (zero, zero), unroll=2)
        lsr, lsi = pr_tab[seg - 1][0:1, :], pi_tab[seg - 1][0:1, :]
        cr, ci = cr_ref[...], ci_ref[...]
        rows_r, rows_i = [], []
        for j in range(SUBLANES):
            rows_r.append(cr)
            rows_i.append(ci)
            cr, ci = fr[j:j + 1, :] + (lsr * cr - lsi * ci), fi[j:j + 1, :] + (lsr * ci + lsi * cr)
        cr_ref[...] = cr
        ci_ref[...] = ci
        in_r, in_i = _rows_to_tile(rows_r), _rows_to_tile(rows_i)

        def add_entry(i, _):
            rows = pl.ds(pl.multiple_of(i * SUBLANES, SUBLANES), SUBLANES)
            pr, pi = pr_tab[i], pi_tab[i]
            sr_ref[rows, :] += pr * in_r - pi * in_i
            si_ref[rows, :] += pr * in_i + pi * in_r
            return 0

        lax.fori_loop(0, seg, add_entry, 0, unroll=2)

    blk = pl.BlockSpec((t, w), lambda j, i: (i, j))
    vec = pl.BlockSpec((1, w), lambda j, i: (0, j))
    sds = jax.ShapeDtypeStruct((L, S), F32)
    tab = pltpu.VMEM((seg, SUBLANES, w), F32)
    return pl.pallas_call(
        body, name=name, grid=(S // w, L // t),
        in_specs=[blk, blk, vec, vec], out_specs=[blk, blk], out_shape=[sds, sds],
        scratch_shapes=[tab, tab, pltpu.VMEM((1, w), F32), pltpu.VMEM((1, w), F32)],
        compiler_params=_params("parallel", "arbitrary"),
    )(bu_re, bu_im, lam_re, lam_im)


def _ssm_scan_bwd(d_re, d_im, s_re, s_im, lam_re, lam_im, *, name):
    L, S = d_re.shape
    w, t = _tile(S, SCAN_W), SCAN_T
    nt = L // t
    seg = t // SUBLANES

    def body(dr_ref, di_ref, sr_ref, si_ref, lr_ref, li_ref, gr_ref, gi_ref, ar_ref, ai_ref,
             pr_tab, pi_tab, cgr, cgi, acc_r, acc_i):
        step = pl.program_id(1)
        a = jnp.broadcast_to(lr_ref[...], (SUBLANES, w))
        b = jnp.broadcast_to(-li_ref[...], (SUBLANES, w))

        @pl.when(step == 0)
        def _():
            for r in (cgr, cgi, acc_r, acc_i):
                r[...] = jnp.zeros_like(r)
            _power_table(a, b, pr_tab, pi_tab, seg)

        def local_scan(ii, carry):
            gr, gi = carry
            rows = pl.ds(pl.multiple_of((seg - 1 - ii) * SUBLANES, SUBLANES), SUBLANES)
            ngr = a * gr - b * gi + dr_ref[rows, :]
            ngi = a * gi + b * gr + di_ref[rows, :]
            gr_ref[rows, :] = ngr
            gi_ref[rows, :] = ngi
            return ngr, ngi

        zero = jnp.zeros((SUBLANES, w), F32)
        fr, fi = lax.fori_loop(0, seg, local_scan, (zero, zero), unroll=2)
        lsr, lsi = pr_tab[seg - 1][0:1, :], pi_tab[seg - 1][0:1, :]
        cr, ci = cgr[...], cgi[...]
        rows_r, rows_i = [None] * SUBLANES, [None] * SUBLANES
        for j in reversed(range(SUBLANES)):
            rows_r[j], rows_i[j] = cr, ci
            cr, ci = fr[j:j + 1, :] + (lsr * cr - lsi * ci), fi[j:j + 1, :] + (lsr * ci + lsi * cr)
        cgr[...] = cr
        cgi[...] = ci
        in_r, in_i = _rows_to_tile(rows_r), _rows_to_tile(rows_i)

        def add_entry(ii, carry):
            nr, ni, xr, xi = carry
            rows = pl.ds(pl.multiple_of((seg - 1 - ii) * SUBLANES, SUBLANES), SUBLANES)
            sr = sr_ref[rows, :]
            si = si_ref[rows, :]
            xr = xr + (nr * sr + ni * si)
            xi = xi + (ni * sr - nr * si)
            pr, pi = pr_tab[ii], pi_tab[ii]
            gr = gr_ref[rows, :] + (pr * in_r - pi * in_i)
            gi = gi_ref[rows, :] + (pr * in_i + pi * in_r)
            gr_ref[rows, :] = gr
            gi_ref[rows, :] = gi
            return gr, gi, xr, xi

        _, _, xr, xi = lax.fori_loop(0, seg, add_entry, (in_r, in_i, acc_r[...], acc_i[...]), unroll=2)
        acc_r[...] = xr
        acc_i[...] = xi

        @pl.when(step == nt - 1)
        def _():
            ar_ref[...] = jnp.sum(xr, axis=0, keepdims=True)
            ai_ref[...] = jnp.sum(xi, axis=0, keepdims=True)

    blk = pl.BlockSpec((t, w), lambda j, i: (nt - 1 - i, j))
    vec = pl.BlockSpec((1, w), lambda j, i: (0, j))
    sds = jax.ShapeDtypeStruct((L, S), F32)
    vsds = jax.ShapeDtypeStruct((1, S), F32)
    tab = pltpu.VMEM((seg, SUBLANES, w), F32)
    tile = pltpu.VMEM((SUBLANES, w), F32)
    return pl.pallas_call(
        body, name=name, grid=(S // w, nt),
        in_specs=[blk, blk, blk, blk, vec, vec], out_specs=[blk, blk, vec, vec],
        out_shape=[sds, sds, vsds, vsds],
        scratch_shapes=[tab, tab, pltpu.VMEM((1, w), F32), pltpu.VMEM((1, w), F32), tile, tile],
        compiler_params=_params("parallel", "arbitrary"),
    )(d_re, d_im, s_re, s_im, lam_re, lam_im)


def _ssm_discretize(a_re, a_im, log_dt, b_re, b_im, c_re, c_im, d_skip):
    G, P, N = SSM_GROUPS, SSM_STATE, SSM_GROUP
    a = jnp.minimum(a_re, -1e-4)
    dt = jnp.exp(log_dt)[:, None]
    mag = jnp.exp(a * dt)
    lr = mag * jnp.cos(a_im * dt)
    li = mag * jnp.sin(a_im * dt)
    den = a * a + a_im * a_im
    cr = ((lr - 1.0) * a + li * a_im) / den
    ci = (li * a - (lr - 1.0) * a_im) / den
    bbr = cr[..., None] * b_re - ci[..., None] * b_im
    bbi = cr[..., None] * b_im + ci[..., None] * b_re
    gl = G // SSM_BLOCKS
    eye = jnp.eye(gl, dtype=F32)[None, :, None, :, None]

    def in_map(bb):
        t = bb.transpose(0, 2, 1).reshape(SSM_BLOCKS, gl, N, P)
        return (eye * t[:, :, :, None, :]).reshape(SSM_BLOCKS, gl * N, gl * P)

    def out_map(c):
        t = c.transpose(0, 2, 1).reshape(SSM_BLOCKS, gl, P, N)
        return (eye * t[:, :, :, None, :]).reshape(SSM_BLOCKS, gl * P, gl * N)

    return (lr.reshape(1, G * P), li.reshape(1, G * P), in_map(bbr), in_map(bbi),
            out_map(c_re), out_map(-c_im), d_skip.reshape(1, SSM_WIDTH))


ATT_T = 512
ATT_WIDE = 2
CUM_T = 512


def _split3(x):
    hi = x.astype(BF16)
    r1 = x - hi.astype(F32)
    mid = r1.astype(BF16)
    lo = (r1 - mid.astype(F32)).astype(BF16)
    return hi, mid, lo


def _tri_dot(tri, x):
    hi, mid, lo = _split3(x)
    dot = lambda p: jnp.dot(tri, p, preferred_element_type=F32)
    return dot(hi) + dot(mid) + dot(lo)


def _log_sigmoid(x):
    return jnp.minimum(x, 0.0) - jnp.log(1.0 + jnp.exp(-jnp.abs(x)))


def _fox_cum(f, fb, *, name):
    L, W = f.shape
    t = _tile(L, CUM_T, SUBLANES)

    def body(f_ref, b_ref, o_ref, carry):
        @pl.when(pl.program_id(0) == 0)
        def _():
            carry[...] = jnp.zeros_like(carry)

        row = lax.broadcasted_iota(jnp.int32, (t, t), 0)
        col = lax.broadcasted_iota(jnp.int32, (t, t), 1)
        tri = jnp.where(col <= row, 1.0, 0.0).astype(BF16)
        c = _tri_dot(tri, _log_sigmoid(f_ref[...] + b_ref[...])) + carry[...]
        o_ref[...] = c
        carry[...] = c[t - 1:t, :]

    return pl.pallas_call(
        body, name=name, grid=(L // t,),
        in_specs=[_row_spec(t, W), _vec_spec(W)], out_specs=_row_spec(t, W),
        out_shape=jax.ShapeDtypeStruct((L, W), F32),
        scratch_shapes=[pltpu.VMEM((1, W), F32)], compiler_params=_params("arbitrary"),
    )(f, fb)


def _fox_cum_bwd(dcum, f, fb, *, name):
    L, W = f.shape
    t = _tile(L, CUM_T, SUBLANES)
    n = L // t

    def body(d_ref, f_ref, b_ref, o_ref, db_ref, carry):
        i = pl.program_id(0)

        @pl.when(i == 0)
        def _():
            carry[...] = jnp.zeros_like(carry)
            db_ref[...] = jnp.zeros_like(db_ref)

        row = lax.broadcasted_iota(jnp.int32, (t, t), 0)
        col = lax.broadcasted_iota(jnp.int32, (t, t), 1)
        tri = jnp.where(col >= row, 1.0, 0.0).astype(BF16)
        dlog = _tri_dot(tri, d_ref[...]) + carry[...]
        carry[...] = dlog[0:1, :]
        df = dlog * _sigmoid(-(f_ref[...] + b_ref[...]))
        o_ref[...] = df.astype(BF16)
        db_ref[...] += _colsum8(df)
        _finish_colsums(i, n - 1, (db_ref,))

    rev = pl.BlockSpec((t, W), lambda i: (n - 1 - i, 0))
    return pl.pallas_call(
        body, name=name, grid=(n,),
        in_specs=[rev, rev, _vec_spec(W)], out_specs=[rev, _vec_spec(W, SUBLANES)],
        out_shape=[jax.ShapeDtypeStruct((L, W), BF16), jax.ShapeDtypeStruct((SUBLANES, W), F32)],
        scratch_shapes=[pltpu.VMEM((1, W), F32)], compiler_params=_params("arbitrary"),
    )(dcum, f, fb)


def _head_col(blk, h):
    lane = lax.broadcasted_iota(jnp.int32, blk.shape, 1)
    return jnp.sum(jnp.where(lane == h * HEAD_DIM, blk, 0.0), axis=1, keepdims=True)


def _lo_mask(rows):
    return lax.broadcasted_iota(jnp.int32, (rows, LANES), 1) < HEAD_DIM


def _causal(t):
    row = lax.broadcasted_iota(jnp.int32, (t, t), 0)
    col = lax.broadcasted_iota(jnp.int32, (t, t), 1)
    return col <= row


def _call_with_exchange(body, exchange, *, name, grid, in_specs, out_specs, out_shape, operands, sem=None):
    sem = sem or ("parallel",) + ("arbitrary",) * (len(grid) - 1)
    if exchange is None:
        return pl.pallas_call(body, name=name, grid=grid, in_specs=in_specs, out_specs=out_specs,
                              out_shape=out_shape, compiler_params=_params(*sem))(*operands)
    n_in, n_out = len(in_specs), len(out_specs)
    ei, eo = len(exchange.operands), len(exchange.out_shapes)

    def wrapped(*refs):
        ins, ex_in = refs[:n_in], refs[n_in:n_in + ei]
        outs, ex_out = refs[n_in + ei:n_in + ei + n_out], refs[n_in + ei + n_out:n_in + ei + n_out + eo]
        sems = refs[n_in + ei + n_out + eo:]
        ids = [pl.program_id(d) for d in range(len(grid))]
        first = functools.reduce(jnp.logical_and, [i == 0 for i in ids])
        last = functools.reduce(jnp.logical_and, [i == g - 1 for i, g in zip(ids, grid)])

        @pl.when(first)
        def _():
            exchange.start(ex_in, ex_out, sems)

        body(*ins, *outs)

        @pl.when(last)
        def _():
            exchange.finish(ex_in, ex_out, sems)

    return pl.pallas_call(
        wrapped, name=name, grid=grid, in_specs=list(in_specs) + [HBM_SPEC] * ei,
        out_specs=list(out_specs) + [HBM_SPEC] * eo, out_shape=list(out_shape) + list(exchange.out_shapes),
        scratch_shapes=exchange.sems, compiler_params=_params(*(("arbitrary",) * len(grid))),
    )(*operands, *exchange.operands)


def _fox_fwd(qkv, cum_cols, cum_rows, *, name, exchange=None):
    L = qkv.shape[0]
    t = _tile(L, ATT_T)
    nq = L // t
    npair = ATTN_HEADS // 2

    def body(q_ref, k_ref, v_ref, cc_ref, cr_ref, o_ref, o32_ref, lse_ref):
        iq = pl.program_id(1)
        lo = _lo_mask(t)
        qv = q_ref[...] * 0.125
        zq = jnp.zeros_like(qv)
        qh = (jnp.where(lo, qv, zq), jnp.where(lo, zq, qv))
        ccv = cc_ref[...]
        cq = (_head_col(ccv, 0), _head_col(ccv, 1))

        def step(block, width, carry, masked):
            start = pl.multiple_of(block * width, width)
            kb = k_ref[pl.ds(start, width), :]
            vb = v_ref[pl.ds(start, width), :]
            out = []
            for h in range(2):
                m, l, acc = carry[h]
                s = lax.dot_general(qh[h], kb, NT, preferred_element_type=F32)
                s = s + (cq[h] - cr_ref[h:h + 1, pl.ds(start, width)])
                if masked:
                    s = jnp.where(_causal(t), s, -jnp.inf)
                m_new = jnp.maximum(m, jnp.max(s, axis=1, keepdims=True))
                alpha = jnp.exp(m - m_new)
                p = jnp.exp(s - m_new)
                l = alpha * l + jnp.sum(p, axis=1, keepdims=True)
                acc = alpha * acc + jnp.dot(p.astype(BF16), vb, preferred_element_type=F32)
                out.append((m_new, l, acc))
            return tuple(out)

        init1 = (jnp.full((t, 1), -jnp.inf, F32), jnp.zeros((t, 1), F32), jnp.zeros((t, LANES), F32))
        n4, rest = iq // 4, iq % 4
        carry = lax.fori_loop(0, n4, lambda j, c: step(j, 4 * t, c, False), (init1, init1))
        carry = lax.fori_loop(2 * n4, 2 * n4 + rest // 2, lambda j, c: step(j, 2 * t, c, False), carry)
        carry = lax.fori_loop(iq - rest % 2, iq, lambda ik, c: step(ik, t, c, False), carry)
        (m0, l0, a0), (m1, l1, a1) = step(iq, t, carry, True)
        out = jnp.where(lo, a0 / l0, a1 / l1)
        o_ref[...] = out.astype(BF16)
        o32_ref[...] = out
        lse_ref[...] = jnp.where(lo, m0 + jnp.log(l0), m1 + jnp.log(l1))

    blk = lambda off: pl.BlockSpec((t, LANES), lambda hp, iq: (iq, off + hp))
    whole = lambda off: pl.BlockSpec((L, LANES), lambda hp, iq: (0, off + hp))
    return _call_with_exchange(
        body, exchange, name=name, grid=(npair, nq),
        in_specs=[blk(0), whole(npair), whole(2 * npair), blk(0),
                  pl.BlockSpec((None, SUBLANES, L), lambda hp, iq: (hp, 0, 0))],
        out_specs=[blk(0), blk(0), blk(0)],
        out_shape=[jax.ShapeDtypeStruct((L, ATTN_WIDTH), BF16), jax.ShapeDtypeStruct((L, ATTN_WIDTH), F32),
                   jax.ShapeDtypeStruct((L, ATTN_WIDTH), F32)],
        operands=(qkv, qkv, qkv, cum_cols, cum_rows))


STAT_LSE, STAT_CUM, STAT_DELTA = 0, 2, 4


def _lane_col(blk, idx):
    lane = lax.broadcasted_iota(jnp.int32, blk.shape, 1)
    return jnp.sum(jnp.where(lane == idx, blk, 0.0), axis=1, keepdims=True)


def _fox_rowstats(do, o, lse, cum_cols, *, name):
    L = do.shape[0]
    t = _tile(L, ATT_T)

    def body(do_ref, o_ref, lse_ref, cc_ref, st_ref):
        lo = _lo_mask(t)
        dd = do_ref[...].astype(F32) * o_ref[...]
        lsev, ccv = lse_ref[...], cc_ref[...]
        cols = (_head_col(lsev, 0), _head_col(lsev, 1), _head_col(ccv, 0), _head_col(ccv, 1),
                jnp.sum(jnp.where(lo, dd, 0.0), axis=1, keepdims=True),
                jnp.sum(jnp.where(lo, 0.0, dd), axis=1, keepdims=True))
        lane = lax.broadcasted_iota(jnp.int32, (t, LANES), 1)
        out = jnp.zeros((t, LANES), F32)
        for i, col in enumerate(cols):
            out = jnp.where(lane == i, col, out)
        st_ref[...] = out

    blk = pl.BlockSpec((t, LANES), lambda hp, i: (i, hp))
    return pl.pallas_call(
        body, name=name, grid=(ATTN_HEADS // 2, L // t),
        in_specs=[blk, blk, blk, blk], out_specs=blk,
        out_shape=jax.ShapeDtypeStruct((L, ATTN_WIDTH), F32),
        compiler_params=_params("parallel", "parallel"),
    )(do, o, lse, cum_cols)


def _fox_bwd(qkv, do, stats, cum_rows, *, name, exchange=None):
    L = qkv.shape[0]
    t = _tile(L, ATT_T)
    nq = L // t
    npair = ATTN_HEADS // 2

    def body(q_ref, do_ref, st_ref, qt_ref, dot_ref, k_ref, v_ref, cr_ref, dk_ref, dv_ref, dc_ref, dq_ref, drow_ref):
        ik = pl.program_id(1)

        @pl.when(ik == 0)
        def _():
            dq_ref[...] = jnp.zeros_like(dq_ref)
            drow_ref[...] = jnp.zeros_like(drow_ref)

        lo = _lo_mask(t)
        lo_rows = lax.broadcasted_iota(jnp.int32, (LANES, t), 0) < HEAD_DIM
        lane = lax.broadcasted_iota(jnp.int32, (t, LANES), 1)
        kb = k_ref[...]
        vb = v_ref[...]
        zk = jnp.zeros_like(kb)
        kh = (jnp.where(lo, kb, zk), jnp.where(lo, zk, kb))
        vh = (jnp.where(lo, vb, zk), jnp.where(lo, zk, vb))
        ck = (cr_ref[0:1, :], cr_ref[1:2, :])

        def step(block, width, carry, masked):
            dk, dv, dc0, dc1 = carry
            start = pl.multiple_of(block * width, width)
            qb = q_ref[pl.ds(start, width), :] * 0.125
            dob = do_ref[pl.ds(start, width), :]
            stb = st_ref[pl.ds(start, width), :]
            lane = lax.broadcasted_iota(jnp.int32, (width, LANES), 1)
            qtb = qt_ref[:, pl.ds(start, width)] * 0.125
            dotb = dot_ref[:, pl.ds(start, width)]
            dks, dvs, dcs, dqs, rss = [], [], [], [], []
            for h in range(2):
                s = lax.dot_general(qb, kh[h], NT, preferred_element_type=F32)
                s = s + (_lane_col(stb, STAT_CUM + h) - ck[h])
                if masked:
                    s = jnp.where(_causal(t), s, -jnp.inf)
                p = jnp.exp(s - _lane_col(stb, STAT_LSE + h))
                dp = lax.dot_general(dob, vh[h], NT, preferred_element_type=F32)
                ds = p * (dp - _lane_col(stb, STAT_DELTA + h))
                dsb = ds.astype(BF16)
                dvs.append(jnp.dot(dotb, p.astype(BF16), preferred_element_type=F32))
                dks.append(jnp.dot(qtb, dsb, preferred_element_type=F32))
                dqs.append(jnp.dot(dsb, kh[h], preferred_element_type=F32))
                dcs.append(jnp.sum(ds, axis=0, keepdims=True))
                rss.append(jnp.sum(ds, axis=1, keepdims=True))
            dq_ref[pl.ds(start, width), :] += 0.125 * (dqs[0] + dqs[1])
            drow_ref[pl.ds(start, width), :] += jnp.where(lane == 0, rss[0], jnp.where(lane == 1, rss[1], 0.0))
            return (dk + jnp.where(lo_rows, dks[0], dks[1]), dv + jnp.where(lo_rows, dvs[0], dvs[1]),
                    dc0 - dcs[0], dc1 - dcs[1])

        zero = jnp.zeros((LANES, t), F32)
        zrow = jnp.zeros((1, t), F32)
        carry = step(ik, t, (zero, zero, zrow, zrow), True)
        first_wide = (ik + ATT_WIDE) // ATT_WIDE
        carry = lax.fori_loop(ik + 1, jnp.minimum(first_wide * ATT_WIDE, nq), lambda iq, c: step(iq, t, c, False),
                              carry)
        dk, dv, dc0, dc1 = lax.fori_loop(first_wide, nq // ATT_WIDE,
                                         lambda j, c: step(j, ATT_WIDE * t, c, False), carry)
        dk_ref[...] = dk.T.astype(BF16)
        dv_ref[...] = dv.T.astype(BF16)
        dc_ref[...] = jnp.zeros_like(dc_ref)
        dc_ref[0:1, :] = dc0
        dc_ref[1:2, :] = dc1

    whole = lambda off: pl.BlockSpec((L, LANES), lambda hp, ik: (0, off + hp))
    blk = lambda off: pl.BlockSpec((t, LANES), lambda hp, ik: (ik, off + hp))
    rows = pl.BlockSpec((None, SUBLANES, t), lambda hp, ik: (hp, 0, ik))
    whole_t = pl.BlockSpec((LANES, L), lambda hp, ik: (hp, 0))
    return _call_with_exchange(
        body, exchange, name=name, grid=(npair, nq),
        in_specs=[whole(0), whole(0), whole(0), whole_t, whole_t, blk(npair), blk(2 * npair), rows],
        out_specs=[blk(0), blk(0), rows, whole(0), whole(0)],
        out_shape=[jax.ShapeDtypeStruct((L, ATTN_WIDTH), BF16), jax.ShapeDtypeStruct((L, ATTN_WIDTH), BF16),
                   jax.ShapeDtypeStruct((npair, SUBLANES, L), F32),
                   jax.ShapeDtypeStruct((L, ATTN_WIDTH), F32), jax.ShapeDtypeStruct((L, ATTN_WIDTH), F32)],
        operands=(qkv, do, stats, qkv[:, :ATTN_WIDTH].T, do.T, qkv, qkv, cum_rows))


def _mod_partial(c_all, mod_w, mod_b_cols, *, name):
    depth, K, cols = mod_w.shape
    tn = _tile(cols, 768)

    def body(c_ref, w_ref, b_ref, o_ref):
        cv = c_ref[...]
        sc = (cv * _sigmoid(cv)).astype(BF16)
        o_ref[...] = jnp.dot(sc, w_ref[...].astype(BF16), preferred_element_type=F32) + b_ref[...]

    return pl.pallas_call(
        body, name=name, grid=(depth, cols // tn),
        in_specs=[pl.BlockSpec((N_DEV, K), lambda l, j: (0, 0)),
                  pl.BlockSpec((None, K, tn), lambda l, j: (l, 0, j)),
                  pl.BlockSpec((None, 1, tn), lambda l, j: (l, 0, j))],
        out_specs=pl.BlockSpec((None, N_DEV, tn), lambda l, j: (l, 0, j)),
        out_shape=jax.ShapeDtypeStruct((depth, N_DEV, cols), F32),
        compiler_params=_params("parallel", "parallel"),
    )(c_all, mod_w, mod_b_cols)


def _mod_wgrad(c_all_t, dmod, *, name):
    depth, nb, cols = dmod.shape
    K = c_all_t.shape[0]
    tn = _tile(cols, 768)
    tk = _tile(K, 256, SUBLANES)

    def body(c_ref, d_ref, o_ref):
        cv = c_ref[...]
        sc = cv * _sigmoid(cv)
        dv = d_ref[...]
        acc = sc[:, 0:1] * dv[0:1, :]
        for b in range(1, nb):
            acc = acc + sc[:, b:b + 1] * dv[b:b + 1, :]
        o_ref[...] = acc

    return pl.pallas_call(
        body, name=name, grid=(depth, K // tk, cols // tn),
        in_specs=[pl.BlockSpec((tk, nb), lambda l, i, j: (i, 0)),
                  pl.BlockSpec((None, nb, tn), lambda l, i, j: (l, 0, j))],
        out_specs=pl.BlockSpec((None, tk, tn), lambda l, i, j: (l, i, j)),
        out_shape=jax.ShapeDtypeStruct((depth, K, cols), F32),
        compiler_params=_params("parallel", "parallel", "parallel"),
    )(c_all_t, dmod)


def _adamw(w, g, m, v, *, name):
    shape = w.shape
    cols = shape[-1]
    rows = int(np.prod(shape[:-1]))
    t = _tile(rows, 256, SUBLANES) if rows % SUBLANES == 0 else rows
    r2 = lambda a: a.reshape(rows, cols)

    def body(w_ref, g_ref, m_ref, v_ref, d_ref, nm_ref, nv_ref):
        gv = g_ref[...]
        nm = ADAM_B1 * m_ref[...] + (1.0 - ADAM_B1) * gv
        nv = ADAM_B2 * v_ref[...] + (1.0 - ADAM_B2) * (gv * gv)
        m_hat = nm / (1.0 - ADAM_B1 ** ADAM_STEP)
        v_hat = nv / (1.0 - ADAM_B2 ** ADAM_STEP)
        d_ref[...] = -ADAM_LR * (m_hat / (jnp.sqrt(v_hat) + ADAM_EPS) + ADAM_WD * w_ref[...])
        nm_ref[...] = nm
        nv_ref[...] = nv

    spec = pl.BlockSpec((t, cols), lambda i: (i, 0))
    sds = jax.ShapeDtypeStruct((rows, cols), F32)
    d, nm, nv = pl.pallas_call(
        body, name=name, grid=(rows // t,),
        in_specs=[spec] * 4, out_specs=[spec] * 3, out_shape=[sds] * 3,
        compiler_params=_params("parallel"),
    )(r2(w), r2(g), r2(m), r2(v))
    return d.reshape(shape), nm.reshape(shape), nv.reshape(shape)


def _my_place():
    return lax.axis_index("x"), lax.axis_index("y"), lax.axis_index("c")


def _other_chips(x, y):
    return [(1 - x, y), (x, 1 - y), (1 - x, 1 - y)]


def _all_gather8(v, *, name, with_sum=False):
    m, n = v.shape

    def body(x_ref, out_ref, *rest):
        if with_sum:
            sum_ref, send_sems, recv_sems, local_sem = rest
        else:
            send_sems, recv_sems, local_sem = rest
        x, y, c = _my_place()
        me, sibling = (x, y, c), (x, y, 1 - c)
        chips = _other_chips(x, y)

        def rows(px, py, pc):
            return out_ref.at[pl.ds((4 * px + 2 * py + pc) * m, m), :]

        def copy(k, block, to, src=None):
            return pltpu.make_async_remote_copy(
                src_ref=rows(*block) if src is None else src, dst_ref=rows(*block),
                send_sem=send_sems.at[k], recv_sem=recv_sems.at[k], device_id=to, device_id_type=MESH)

        mine = pltpu.make_async_copy(x_ref, rows(*me), local_sem)
        mine.start()
        first = [copy(0, me, sibling, src=x_ref)]
        first += [copy(1 + j, me, (*chip, c), src=x_ref) for j, chip in enumerate(chips)]
        for cp in first:
            cp.start()
        passed = [copy(4 + j, (*chip, c), sibling) for j, chip in enumerate(chips)]
        for j, chip in enumerate(chips):
            copy(1 + j, (*chip, c), me).wait_recv()
            passed[j].start()
        copy(0, sibling, me).wait_recv()
        for j, chip in enumerate(chips):
            copy(4 + j, (*chip, 1 - c), me).wait_recv()
        for cp in first + passed:
            cp.wait_send()
        mine.wait()
        if with_sum:
            acc = out_ref[pl.ds(0, m), :]
            for d in range(1, N_DEV):
                acc = acc + out_ref[pl.ds(d * m, m), :]
            sum_ref[...] = acc

    vm = pl.BlockSpec(memory_space=pltpu.VMEM)
    out_shape = [jax.ShapeDtypeStruct((N_DEV * m, n), F32)]
    if with_sum:
        out_shape.append(jax.ShapeDtypeStruct((m, n), F32))
    res = pl.pallas_call(
        body, name=name, out_shape=out_shape, in_specs=[vm], out_specs=[vm] * len(out_shape),
        scratch_shapes=[pltpu.SemaphoreType.DMA((7,)), pltpu.SemaphoreType.DMA((7,)), pltpu.SemaphoreType.DMA],
        compiler_params=pltpu.CompilerParams(vmem_limit_bytes=VMEM_LIMIT),
    )(v)
    return res if with_sum else res[0]


class _Cut(NamedTuple):
    shape: tuple
    slab: int
    half: int


IN_WIDTH = SSM_WIDTH + 3 * ATTN_WIDTH + ATTN_HEADS + 2 * D_MODEL
CUTS = dict(
    ffn_w_in=_Cut((1, D_MODEL, 2 * D_FF), 2, 1),
    ffn_w_out=_Cut((1, D_FF, D_MODEL), 1, 2),
    mix_w_in=_Cut((N_CHIPS, D_MODEL, IN_WIDTH // N_CHIPS), 0, 1),
    glu_w=_Cut((1, SSM_WIDTH, 2 * D_MODEL), 2, 1),
    attn_w_out=_Cut((1, ATTN_WIDTH, D_MODEL), 2, 1),
    mix_w_out=_Cut((1, D_MODEL, D_MODEL), 1, 2),
)
LAYER_MATS = (("ffn_w_in", 0), ("ffn_w_in", 1), ("ffn_w_out", 0), ("ffn_w_out", 1), ("mix_w_in", None),
              ("glu_w", None), ("attn_w_out", None), ("mix_w_out", None))
FIRST_MATS = (0, 2)
MIXER_MATS = (4, 5, 6, 7)
LATE_MATS = (1, 3)


def _part_shape(cut, slab=False, half=False):
    s = list(cut.shape)
    if slab:
        s[cut.slab] //= N_CHIPS
    if half:
        s[cut.half] //= 2
    return tuple(s)


def _window(ref, cut, slab=None, half=None):
    idx = [slice(None)] * len(cut.shape)
    for axis, parts, which in ((cut.slab, N_CHIPS, slab), (cut.half, 2, half)):
        if which is not None:
            width = cut.shape[axis] // parts
            idx[axis] = pl.ds(pl.multiple_of(which * width, width), width)
    return ref.at[tuple(idx)]


HBM_SPEC = pl.BlockSpec(memory_space=pltpu.HBM)


class _Exchange(NamedTuple):
    operands: list
    out_shapes: list
    sems: list
    start: object
    finish: object


def _run_exchange(ex, *, name):
    ni, no = len(ex.operands), len(ex.out_shapes)

    def body(*refs):
        parts = (refs[:ni], refs[ni:ni + no], refs[ni + no:])
        ex.start(*parts)
        ex.finish(*parts)

    return pl.pallas_call(
        body, name=name, out_shape=ex.out_shapes, in_specs=[HBM_SPEC] * ni, out_specs=[HBM_SPEC] * no,
        scratch_shapes=ex.sems,
    )(*ex.operands)


def _gather_exchange(shards, cuts):
    n = len(shards)

    def setup(s_refs, f_refs, sems):
        send_sems, recv_sems = sems
        x, y, c = _my_place()
        me, sibling, mine = (x, y, c), (x, y, 1 - c), 2 * x + y
        chips = _other_chips(x, y)

        def copy(i, k, src, dst, to):
            return pltpu.make_async_remote_copy(
                src_ref=src, dst_ref=dst, send_sem=send_sems.at[7 * i + k], recv_sem=recv_sems.at[7 * i + k],
                device_id=to, device_id_type=MESH)

        def landed(i, j, half):
            return _window(f_refs[i], cuts[i], slab=2 * chips[j][0] + chips[j][1], half=half)

        def sends():
            own = [copy(i, 6, s_refs[i], _window(f_refs[i], cuts[i], slab=mine), sibling) for i in range(n)]
            return own + [copy(i, j, _window(s_refs[i], cuts[i], half=c),
                               _window(f_refs[i], cuts[i], slab=mine, half=c), (*chips[j], c))
                          for i in range(n) for j in range(3)]

        return c, me, sibling, copy, landed, sends

    def start(s_refs, f_refs, sems):
        for cp in setup(s_refs, f_refs, sems)[-1]():
            cp.start()

    def finish(s_refs, f_refs, sems):
        c, me, sibling, copy, landed, sends = setup(s_refs, f_refs, sems)
        passed = []
        for i in range(n):
            for j in range(3):
                copy(i, j, landed(i, j, c), landed(i, j, c), me).wait_recv()
                passed.append(copy(i, 3 + j, landed(i, j, c), landed(i, j, c), sibling))
                passed[-1].start()
        for i in range(n):
            for j in range(3):
                copy(i, 3 + j, landed(i, j, 1 - c), landed(i, j, 1 - c), me).wait_recv()
        for i in range(n):
            mine_i = _window(f_refs[i], cuts[i], slab=2 * me[0] + me[1])
            copy(i, 6, mine_i, mine_i, me).wait_recv()
        for cp in sends() + passed:
            cp.wait_send()

    return _Exchange(
        list(shards), [jax.ShapeDtypeStruct(cut.shape, s.dtype) for s, cut in zip(shards, cuts)],
        [pltpu.SemaphoreType.DMA((7 * n,)), pltpu.SemaphoreType.DMA((7 * n,))], start, finish)


def _swap_exchange(mats, cuts):
    n = len(mats)

    def copies(m_refs, r_refs, sems):
        send_sems, recv_sems = sems
        x, y, c = _my_place()
        return [pltpu.make_async_remote_copy(
            src_ref=_window(m_refs[i], cuts[i], half=1 - c), dst_ref=r_refs[i], send_sem=send_sems.at[i],
            recv_sem=recv_sems.at[i], device_id=(x, y, 1 - c), device_id_type=MESH) for i in range(n)]

    def start(m_refs, r_refs, sems):
        for cp in copies(m_refs, r_refs, sems):
            cp.start()

    def finish(m_refs, r_refs, sems):
        for cp in copies(m_refs, r_refs, sems):
            cp.wait()

    return _Exchange(
        list(mats), [jax.ShapeDtypeStruct(_part_shape(cut, half=True), m.dtype) for m, cut in zip(mats, cuts)],
        [pltpu.SemaphoreType.DMA((n,)), pltpu.SemaphoreType.DMA((n,))], start, finish)


def _partials_exchange(sums, cuts):
    n = len(sums)

    def copies(s_refs, p_refs, sems):
        send_sems, recv_sems = sems
        x, y, c = _my_place()
        return [pltpu.make_async_remote_copy(
            src_ref=_window(s_refs[i], cuts[i], slab=2 * chip[0] + chip[1]), dst_ref=p_refs[i].at[j],
            send_sem=send_sems.at[3 * i + j], recv_sem=recv_sems.at[3 * i + j],
            device_id=(*chip, c), device_id_type=MESH)
            for i in range(n) for j, chip in enumerate(_other_chips(x, y))]

    def start(s_refs, p_refs, sems):
        for cp in copies(s_refs, p_refs, sems):
            cp.start()

    def finish(s_refs, p_refs, sems):
        for cp in copies(s_refs, p_refs, sems):
            cp.wait()

    return _Exchange(
        list(sums), [jax.ShapeDtypeStruct((3,) + _part_shape(cut, slab=True, half=True), s.dtype)
                     for s, cut in zip(sums, cuts)],
        [pltpu.SemaphoreType.DMA((3 * n,)), pltpu.SemaphoreType.DMA((3 * n,))], start, finish)


def _share_all(dests, cuts, places, *, name):
    names = list(dests)
    nn, n = len(names), len(places)

    def body(*refs):
        o_refs = dict(zip(names, refs[nn:2 * nn]))
        send_sems, recv_sems = refs[2 * nn:]
        x, y, c = _my_place()

        def win(i, half):
            slab_cut = _Cut(_part_shape(cuts[i], slab=True), cuts[i].slab, cuts[i].half)
            return _window(o_refs[places[i][0]].at[places[i][1]], slab_cut, half=half)

        def copy(i, half):
            return pltpu.make_async_remote_copy(
                src_ref=win(i, half), dst_ref=win(i, half), send_sem=send_sems.at[i], recv_sem=recv_sems.at[i],
                device_id=(x, y, 1 - c), device_id_type=MESH)

        for i in range(n):
            copy(i, c).start()
        for i in range(n):
            copy(i, c).wait_send()
            copy(i, 1 - c).wait_recv()

    return pl.pallas_call(
        body, name=name, out_shape=[jax.ShapeDtypeStruct(dests[k].shape, F32) for k in names],
        in_specs=[HBM_SPEC] * nn, out_specs=[HBM_SPEC] * nn,
        input_output_aliases={i: i for i in range(nn)},
        scratch_shapes=[pltpu.SemaphoreType.DMA((n,)), pltpu.SemaphoreType.DMA((n,))],
    )(*[dests[k] for k in names])


def _cut_blocks(shape):
    _, R, C = shape
    tr = _tile(R, 512, 16)
    tc = _tile(C, 2048) if C % LANES == 0 else C
    return (None, tr, tc), (shape[0], R // tr, C // tc)


def _offset_map(axis, blocks):
    def index_map(b, i, j, which):
        idx = [b, i, j]
        idx[axis] = which[0] * blocks[axis] + idx[axis]
        return tuple(idx)
    return index_map


def _add_half(mat, other, cut, c_idx, *, name):
    shape = _part_shape(cut, half=True)
    block, grid = _cut_blocks(shape)

    def body(c_ref, m_ref, o_ref, f_ref, b_ref):
        s = m_ref[...] + o_ref[...]
        f_ref[...] = s
        b_ref[...] = s.astype(BF16)

    plain = pl.BlockSpec(block, lambda b, i, j, which: (b, i, j))
    grid_spec = pltpu.PrefetchScalarGridSpec(
        num_scalar_prefetch=1, grid=grid,
        in_specs=[pl.BlockSpec(block, _offset_map(cut.half, grid)), plain], out_specs=[plain, plain])
    return pl.pallas_call(
        body, name=name, grid_spec=grid_spec,
        out_shape=[jax.ShapeDtypeStruct(shape, F32), jax.ShapeDtypeStruct(shape, BF16)],
        compiler_params=_params("parallel", "parallel", "parallel"),
    )(c_idx, mat, other)


def _sum_slab(own, parts, cut, dest, place, chip_idx, c_idx, *, name):
    shape = _part_shape(cut, slab=True, half=True)
    block, grid = _cut_blocks(shape)
    assert shape[0] == 1

    def body(k_ref, c_ref, o_ref, p_ref, dest_ref, out_ref):
        acc = o_ref[...]
        for j in range(3):
            acc = acc + p_ref[j].astype(F32)
        out_ref[...] = acc

    def own_map(b, i, j, chip, core):
        idx = [b, i, j]
        idx[cut.slab] = chip[0] * grid[cut.slab] + idx[cut.slab]
        return tuple(idx)

    def dest_map(b, i, j, chip, core):
        idx = [b, i, j]
        idx[cut.half] = core[0] * grid[cut.half] + idx[cut.half]
        return tuple(place) + tuple(idx)

    grid_spec = pltpu.PrefetchScalarGridSpec(
        num_scalar_prefetch=2, grid=grid,
        in_specs=[pl.BlockSpec(block, own_map),
                  pl.BlockSpec((3,) + block[1:], lambda b, i, j, chip, core: (0, i, j)),
                  pl.BlockSpec(memory_space=pl.ANY)],
        out_specs=pl.BlockSpec((None,) * len(place) + block, dest_map))
    return pl.pallas_call(
        body, name=name, grid_spec=grid_spec, out_shape=jax.ShapeDtypeStruct(dest.shape, F32),
        input_output_aliases={4: 0},
        compiler_params=_params("parallel", "parallel", "parallel"),
    )(chip_idx, c_idx, own, parts.reshape((3,) + shape[1:]), dest)


def _pad_rows(flat, cols=8 * LANES, align=SUBLANES):
    n = flat.shape[0]
    rows = -(-n // (cols * align)) * align
    return jnp.pad(flat, (0, rows * cols - n)).reshape(rows, cols)


def _row(v):
    return v.reshape(1, -1)


def _ffn_fwd(x, mod, g_pre, g_post, w_in, w_out, tag, exchange=None):
    sh, sc, gate = _row(mod[0]), _row(mod[1]), _row(mod[2])
    h = _prenorm(x, _row(g_pre), sc, sh, name=f"prenorm_{tag}")
    gt, up, act, *exchanged = _mm_swiglu(h, w_in, name=f"swiglu_{tag}", exchange=exchange)
    y, x_out = _mm_postnorm(act, w_out, x, _row(g_post), gate, FFN_RES, name=f"ffn_out_{tag}")
    return x_out, (x, h, gt, up, act, y), exchanged


def _ffn_bwd(dxo, saved, mod, g_pre, g_post, w_in, w_out, tag, swap=None, after_swap=None):
    x, h, gt, up, act, y = saved
    sc, gate = _row(mod[1]), _row(mod[2])
    dy, dgate, dgpost, *swapped = _postnorm_bwd(dxo, y, _row(g_post), gate, FFN_RES, name=f"postnorm_bwd_{tag}",
                                                exchange=swap)
    exchange = after_swap(swapped) if after_swap is not None else None
    dgt, dup, *exchanged = _mm_swiglu_bwd(dy, w_out, gt, up, name=f"swiglu_bwd_{tag}", exchange=exchange)
    dw_out = _mm(act, dy, ta=True, name=f"dw_out_{tag}", tm=1408, tn=1024, tk=2048)
    dh = _mm(dgt, w_in, tb=True, name=f"dh_gate_{tag}", tm=1024, tk=D_FF)
    dh = _mm(dup, w_in, tb=True, b_k0=D_FF, bias=dh, name=f"dh_up_{tag}", tm=1024, tk=D_FF)
    dw_in = _mm(h, dgt, ta=True, out_n=2 * D_FF, name=f"dw_gate_{tag}", tm=1024, tn=1408, tk=2048)
    dw_in = _mm(h, dup, ta=True, out_n=2 * D_FF, out_j0=D_FF, into=dw_in, name=f"dw_up_{tag}",
                tm=1024, tn=1408, tk=2048)
    dx, dsh, dsc, dgpre = _prenorm_bwd(x, dh, dxo, _row(g_pre), sc, name=f"prenorm_bwd_{tag}")
    dmod = jnp.stack([dsh[0], dsc[0], dgate[0]])
    return dx, dmod, dgpre[0], dgpost[0], dw_in, dw_out, exchanged


def _split_mix_w_in(w):
    u0, q0, f0, g0 = 0, SSM_WIDTH, SSM_WIDTH + 3 * ATTN_WIDTH, SSM_WIDTH + 3 * ATTN_WIDTH + ATTN_HEADS
    w_f = jnp.pad(w[:, f0:g0], ((0, 0), (0, LANES - ATTN_HEADS)))
    return w[:, u0:q0], w[:, q0:f0], w_f, w[:, g0:]


def _mixer_fwd(x, mod, g_pre, g_post, w, ssm, forget_b, tag, exchange=None):
    L = x.shape[0]
    sh, sc, gate = _row(mod[0]), _row(mod[1]), _row(mod[2])
    lam_re, lam_im, bin_re, bin_im, cout_re, cout_im, dskip = ssm
    h = _prenorm(x, _row(g_pre), sc, sh, name=f"prenorm_{tag}")
    u = _mm(h, w["u"], name=f"proj_u_{tag}")
    qkv = _mm(h, w["qkv"], out_dtype=BF16, name=f"proj_qkv_{tag}")
    f = _mm(h, w["f"], name=f"proj_f_{tag}")
    gab = _mm(h, w["gab"], name=f"proj_gab_{tag}")
    u_i = _interleave(u)
    bu_re, bu_im = _mm_bd([u_i], [bin_re, bin_im], name=f"ssm_bu_{tag}")
    s_re, s_im = _ssm_scan_fwd(bu_re, bu_im, lam_re, lam_im, name=f"ssm_scan_{tag}")
    y_ssm = _deinterleave(_mm_bd([s_re, s_im], [cout_re, cout_im], bias=u_i, bscale=dskip, name=f"ssm_y_{tag}"))
    gl = _gelu_fwd(y_ssm, name=f"gelu_{tag}")
    z = _mm(gl, w["glu"], name=f"glu_{tag}")
    fb = jnp.pad(forget_b, (0, LANES - ATTN_HEADS)).reshape(1, LANES)
    cum = _fox_cum(f, fb, name=f"fox_cum_{tag}")
    cum8 = cum[:, :ATTN_HEADS]
    cum_cols = jnp.repeat(cum8, HEAD_DIM, axis=1)
    cum_rows = jnp.pad(cum8.T.reshape(ATTN_HEADS // 2, 2, L), ((0, 0), (0, SUBLANES - 2), (0, 0)))
    attn, attn32, lse, *exchanged = _fox_fwd(qkv, cum_cols, cum_rows, name=f"fox_fwd_{tag}", exchange=exchange)
    yb = _mm(attn, w["attn_out"], name=f"attn_out_{tag}")
    merged = _merge_fwd(z, yb, gab, name=f"merge_{tag}")
    y, x_out = _mm_postnorm(merged, w["out"], x, _row(g_post), gate, 1.0, name=f"mix_out_{tag}")
    saved = (x, h, u, u_i, qkv, f, gab, s_re, s_im, y_ssm, gl, z, fb, cum_cols, cum_rows, attn, attn32, lse, yb,
             merged, y)
    return x_out, saved, exchanged


def _mixer_bwd(dxo, saved, mod, g_pre, g_post, w, ssm, tag, swap=None, after_swap=None):
    (x, h, u, u_i, qkv, f, gab, s_re, s_im, y_ssm, gl, z, fb, cum_cols, cum_rows, attn, attn32, lse, yb,
     merged, y) = saved
    L = x.shape[0]
    sc, gate = _row(mod[1]), _row(mod[2])
    lam_re, lam_im, bin_re, bin_im, cout_re, cout_im, dskip = ssm
    dy, dgate, dgpost = _postnorm_bwd(dxo, y, _row(g_post), gate, 1.0, name=f"postnorm_bwd_{tag}")
    dmerged = _mm(dy, w["out"], tb=True, name=f"dmerged_{tag}")
    dw_out = _mm(merged, dy, ta=True, name=f"dw_mix_out_{tag}", tm=1024, tn=1024)
    dz, dyb, dgab, *swapped = _merge_bwd(dmerged, z, yb, gab, name=f"merge_bwd_{tag}", exchange=swap)
    exchange = after_swap(swapped) if after_swap is not None else None
    dgl = _mm(dz, w["glu"], tb=True, name=f"dgl_{tag}", tk=2048)
    dw_glu = _mm(gl, dz, ta=True, name=f"dw_glu_{tag}", tm=512, tn=2048)
    dys, dsk, dd = _gelu_bwd(dgl, y_ssm, u, dskip, name=f"gelu_bwd_{tag}")
    dys, dsk = _interleave(dys), _interleave(dsk)
    d_re, d_im = _mm_bd([dys], [cout_re, cout_im], tb=True, name=f"ssm_ds_{tag}")
    dcout_re = _mm_bd_t(s_re, dys, SSM_BLOCKS, name=f"ssm_dc_re_{tag}")
    dcout_im = _mm_bd_t(s_im, dys, SSM_BLOCKS, name=f"ssm_dc_im_{tag}")
    g_re, g_im, dlam_re, dlam_im = _ssm_scan_bwd(d_re, d_im, s_re, s_im, lam_re, lam_im, name=f"ssm_scan_bwd_{tag}")
    du = _mm_bd([g_re, g_im], [bin_re, bin_im], tb=True, bias=dsk, out_dtype=BF16, name=f"ssm_du_{tag}")
    du = _deinterleave(du)
    dbin_re = _mm_bd_t(u_i, g_re, SSM_BLOCKS, name=f"ssm_db_re_{tag}")
    dbin_im = _mm_bd_t(u_i, g_im, SSM_BLOCKS, name=f"ssm_db_im_{tag}")
    dssm = (dlam_re, dlam_im, dbin_re, dbin_im, dcout_re, dcout_im, _row(dd[0]))
    dattn = _mm(dyb, w["attn_out"], tb=True, out_dtype=BF16, name=f"dattn_{tag}")
    dw_attn = _mm(attn, dyb, ta=True, name=f"dw_attn_out_{tag}", tm=512, tn=1024)
    stats = _fox_rowstats(dattn, attn32, lse, cum_cols, name=f"fox_rowstats_{tag}")
    dk, dv, dcum_rows, dq, drow, *exchanged = _fox_bwd(qkv, dattn, stats, cum_rows, name=f"fox_bwd_{tag}",
                                                       exchange=exchange)
    drow8 = drow.reshape(L, ATTN_HEADS // 2, LANES)[:, :, :2].reshape(L, ATTN_HEADS)
    dcum = drow8 + dcum_rows[:, :2, :].reshape(ATTN_HEADS, L).T
    dcum = jnp.pad(dcum, ((0, 0), (0, LANES - ATTN_HEADS)))
    df, dfb = _fox_cum_bwd(dcum, f, fb, name=f"fox_cum_bwd_{tag}")
    dqkv = jnp.concatenate([dq.astype(BF16), dk, dv], axis=1)
    dh = _mm(dqkv, w["qkv"], tb=True, name=f"dh_qkv_{tag}", tk=1536)
    dh = _mm(du, w["u"], tb=True, bias=dh, name=f"dh_u_{tag}")
    dh = _mm(dgab, w["gab"], tb=True, bias=dh, name=f"dh_gab_{tag}", tk=2048)
    dh = _mm(df, w["f"], tb=True, bias=dh, name=f"dh_f_{tag}")
    dw_u = _mm(h, du, ta=True, name=f"dw_u_{tag}", tm=1024, tn=512)
    dw_qkv = _mm(h, dqkv, ta=True, name=f"dw_qkv_{tag}", tm=1024, tn=1536)
    dw_f = _mm(h, df, ta=True, name=f"dw_f_{tag}", tm=1024)
    dw_gab = _mm(h, dgab, ta=True, name=f"dw_gab_{tag}", tm=1024, tn=1024)
    dw_in = jnp.concatenate([dw_u, dw_qkv, dw_f[:, :ATTN_HEADS], dw_gab], axis=1)
    dx, dsh, dsc, dgpre = _prenorm_bwd(x, dh, dxo, _row(g_pre), sc, name=f"prenorm_bwd_{tag}")
    dmod = jnp.stack([dsh[0], dsc[0], dgate[0]])
    grads = dict(mix_w_in=dw_in, glu_w=dw_glu, attn_w_out=dw_attn, mix_w_out=dw_out,
                 forget_b=dfb[0, :ATTN_HEADS])
    return dx, dmod, dgpre[0], dgpost[0], grads, dssm, exchanged


SSM_NAMES = ("ssm_a_re", "ssm_a_im", "ssm_log_dt", "ssm_b_re", "ssm_b_im", "ssm_c_re", "ssm_c_im", "ssm_d")
SMALL_NAMES = ("forget_b",) + SSM_NAMES
WEIGHT_NAMES = ("mod_w", "mod_b", "norm_pre", "norm_post", "ffn_w_in", "ffn_w_out", "mix_w_in", "forget_b") \
    + SSM_NAMES + ("glu_w", "attn_w_out", "mix_w_out")


def _layer_shards(w, l):
    return [(w[n][l] if j is None else w[n][l, j]).astype(BF16).reshape(_part_shape(CUTS[n], slab=True))
            for n, j in LAYER_MATS]


def _train_step(x, c, target, w, m, v):
    xi, yi, ci = _my_place()
    chip = 2 * xi + yi
    dev = 4 * xi + 2 * yi + ci
    mod_cols = N_SUB * 3 * D_MODEL // N_CHIPS
    norm_cols = D_MODEL // N_CHIPS

    cuts = [CUTS[n] for n, _ in LAYER_MATS]
    shards = [_layer_shards(w, l) for l in range(DEPTH)]
    full = [[None] * len(LAYER_MATS) for _ in range(DEPTH)]

    def gather(keys):
        return _gather_exchange([shards[l][i] for l, i in keys], [cuts[i] for _, i in keys])

    def store(keys, mats):
        for (l, i), a in zip(keys, mats):
            full[l][i] = a

    first = lambda l: [(l, i) for i in FIRST_MATS]
    mixer = lambda l: [(l, i) for i in MIXER_MATS]
    late = lambda l: [(l, i) for i in LATE_MATS]
    store(first(0), _run_exchange(gather(first(0)), name="gather_weights_first"))
    c_all = _all_gather8(jnp.pad(c, ((0, SUBLANES - 1), (0, 0))), name="gather_c")[::SUBLANES]
    mod_b_cols = lax.dynamic_slice_in_dim(w["mod_b"], chip * mod_cols, mod_cols, axis=1)[:, None, :]
    mod_part = _mod_partial(c_all, w["mod_w"], mod_b_cols, name="mod_partial")
    small_fwd = jnp.concatenate([mod_part.reshape(-1), w["norm_pre"].reshape(-1), w["norm_post"].reshape(-1)])
    n_mod, n_norm = mod_part.size, w["norm_pre"].size
    sf_all = _all_gather8(_pad_rows(small_fwd), name="gather_mod").reshape(N_DEV, -1)
    sf_chips = sf_all[::2]
    mod_all = jnp.concatenate(
        [sf_chips[k, :n_mod].reshape(DEPTH, N_DEV, mod_cols) for k in range(N_CHIPS)], axis=2)
    mod_mine = lax.dynamic_index_in_dim(mod_all, dev, axis=1, keepdims=False).reshape(DEPTH, N_SUB, 3, D_MODEL)
    norm_pre = jnp.concatenate(
        [sf_chips[k, n_mod:n_mod + n_norm].reshape(DEPTH, N_SUB, norm_cols) for k in range(N_CHIPS)], axis=2)
    norm_post = jnp.concatenate(
        [sf_chips[k, n_mod + n_norm:n_mod + 2 * n_norm].reshape(DEPTH, N_SUB, norm_cols) for k in range(N_CHIPS)],
        axis=2)

    saved, layer_w, ssm_prep, ssm_vjp = [], [], [], []
    h = x
    for l in range(DEPTH):
        lw = dict(ffn=[(full[l][0][0], full[l][2][0]), None])
        prep, vjp = jax.vjp(_ssm_discretize, *[w[n][l] for n in SSM_NAMES])
        layer_w.append(lw)
        ssm_prep.append(prep)
        ssm_vjp.append(vjp)
        h, s0, arrived = _ffn_fwd(h, mod_mine[l, 0], norm_pre[l, 0], norm_post[l, 0], *lw["ffn"][0], tag=f"l{l}a",
                                  exchange=gather(mixer(l)))
        store(mixer(l), arrived)
        mix_in, glu, attn_out, mix_out = full[l][4:]
        w_u, w_qkv, w_f, w_gab = _split_mix_w_in(mix_in.transpose(1, 0, 2).reshape(D_MODEL, IN_WIDTH))
        lw["mix"] = dict(u=w_u, qkv=w_qkv, f=w_f, gab=w_gab, glu=glu[0], attn_out=attn_out[0], out=mix_out[0])
        coming = late(l) + (first(l + 1) if l + 1 < DEPTH else [])
        h, s1, arrived = _mixer_fwd(h, mod_mine[l, 1], norm_pre[l, 1], norm_post[l, 1], lw["mix"], prep,
                                    w["forget_b"][l], tag=f"l{l}m", exchange=gather(coming))
        store(coming, arrived)
        lw["ffn"][1] = (full[l][1][0], full[l][3][0])
        h, s2, _ = _ffn_fwd(h, mod_mine[l, 2], norm_pre[l, 2], norm_post[l, 2], *lw["ffn"][1], tag=f"l{l}b")
        saved.append((s0, s1, s2))
    dh, loss8 = _loss_head(h, target, name="loss_head")

    c_idx = ci.reshape(1).astype(jnp.int32)
    chip_idx = chip.reshape(1).astype(jnp.int32)
    places = []
    dests = {n: lax.empty(((DEPTH,) if j is None else (DEPTH, 2)) + _part_shape(CUTS[n], slab=True), F32)
             for n, j in LAYER_MATS}
    g_small = {n: [None] * DEPTH for n in SMALL_NAMES}
    dmod, dnpre, dnpost = [], [], []
    pending = []

    def swap_of(l, idx, mats):
        cs = [cuts[i] for i in idx]
        mats = [a.reshape(cut.shape) for a, cut in zip(mats, cs)]

        def added(from_sibling):
            for k, i in enumerate(idx):
                f, b = _add_half(mats[k], from_sibling[k], cs[k], c_idx, name=f"grad_add_l{l}_{i}")
                pending.append((l, i, f, b))

        return _swap_exchange(mats, cs), added

    def partials():
        return _partials_exchange([p[3] for p in pending], [cuts[p[1]] for p in pending])

    def then_partials(added):
        def after_swap(from_sibling):
            added(from_sibling)
            return partials()
        return after_swap

    def end_reduce(parts):
        for (l, i, f, _), part in zip(pending, parts):
            n, j = LAYER_MATS[i]
            place = (l,) if j is None else (l, j)
            dests[n] = _sum_slab(f, part, cuts[i], dests[n], place, chip_idx, c_idx, name=f"grad_sum_l{l}_{i}")
            places.append((n, place))
        pending.clear()

    swap, added = None, None
    for l in reversed(range(DEPTH)):
        lw = layer_w[l]
        dh, dm2, dp2, dq2, dwin2, dwout2, _ = _ffn_bwd(
            dh, saved[l][2], mod_mine[l, 2], norm_pre[l, 2], norm_post[l, 2], *lw["ffn"][1], tag=f"l{l}b",
            swap=swap, after_swap=added)
        swap, added = swap_of(l, LATE_MATS, [dwin2, dwout2])
        dh, dm1, dp1, dq1, gmix, dssm, parts = _mixer_bwd(
            dh, saved[l][1], mod_mine[l, 1], norm_pre[l, 1], norm_post[l, 1], lw["mix"], ssm_prep[l],
            tag=f"l{l}m", swap=swap, after_swap=then_partials(added))
        end_reduce(parts)
        dmix_in = gmix["mix_w_in"].reshape(D_MODEL, N_CHIPS, IN_WIDTH // N_CHIPS).transpose(1, 0, 2)
        swap, added = swap_of(l, MIXER_MATS, [dmix_in, gmix["glu_w"], gmix["attn_w_out"], gmix["mix_w_out"]])
        dh, dm0, dp0, dq0, dwin0, dwout0, parts = _ffn_bwd(
            dh, saved[l][0], mod_mine[l, 0], norm_pre[l, 0], norm_post[l, 0], *lw["ffn"][0], tag=f"l{l}a",
            swap=swap, after_swap=then_partials(added))
        end_reduce(parts)
        swap, added = swap_of(l, FIRST_MATS, [dwin0, dwout0])
        dmod.insert(0, jnp.stack([dm0, dm1, dm2]))
        dnpre.insert(0, jnp.stack([dp0, dp1, dp2]))
        dnpost.insert(0, jnp.stack([dq0, dq1, dq2]))
        g_small["forget_b"][l] = gmix["forget_b"]
        for n, g in zip(SSM_NAMES, ssm_vjp[l](dssm)):
            g_small[n][l] = g
    added(_run_exchange(swap, name="grad_swap_last"))
    end_reduce(_run_exchange(partials(), name="grad_partials_last"))
    grad_x = dh
    g_small = {n: jnp.stack(g) for n, g in g_small.items()}

    small = [loss8[0, :1], jnp.stack(dmod).reshape(-1), jnp.stack(dnpre).reshape(-1), jnp.stack(dnpost).reshape(-1)]
    small += [g_small[n].reshape(-1) for n in SMALL_NAMES]
    sizes = [int(s.size) for s in small]
    offs = np.concatenate([[0], np.cumsum(sizes)])
    sb_all, sb_sum = _all_gather8(_pad_rows(jnp.concatenate(small)), name="gather_small_grads", with_sum=True)
    sb_sum = sb_sum.reshape(-1)
    take = lambda i: sb_sum[int(offs[i]):int(offs[i + 1])]
    loss = take(0)[0]
    grads = {"mod_b": take(1).reshape(DEPTH, N_SUB * 3 * D_MODEL)}
    dnorm_pre_full = take(2).reshape(DEPTH, N_SUB, D_MODEL)
    dnorm_post_full = take(3).reshape(DEPTH, N_SUB, D_MODEL)
    grads["norm_pre"] = lax.dynamic_slice_in_dim(dnorm_pre_full, chip * norm_cols, norm_cols, axis=2)
    grads["norm_post"] = lax.dynamic_slice_in_dim(dnorm_post_full, chip * norm_cols, norm_cols, axis=2)
    for i, n in enumerate(SMALL_NAMES):
        grads[n] = take(4 + i).reshape(w[n].shape)
    dmod_all = sb_all.reshape(N_DEV, -1)[:, int(offs[1]):int(offs[2])].reshape(N_DEV, DEPTH, N_SUB * 3 * D_MODEL)
    dmod_cols = lax.dynamic_slice_in_dim(dmod_all, chip * mod_cols, mod_cols, axis=2).transpose(1, 0, 2)
    grads["mod_w"] = _mod_wgrad(c_all.T, dmod_cols, name="mod_wgrad")

    shared = _share_all(dests, [CUTS[n] for n, _ in places], places, name="grad_share")
    for n, g in zip(dests, shared):
        grads[n] = g.reshape(w[n].shape)

    delta, new_m, new_v = {}, {}, {}
    for n in WEIGHT_NAMES:
        delta[n], new_m[n], new_v[n] = _adamw(w[n], grads[n], m[n], v[n], name=f"adamw_{n}")
    outs = [loss, grad_x[None]]
    for group in (grads, delta, new_m, new_v):
        outs += [group[n] for n in WEIGHT_NAMES]
    return tuple(outs)


def kernel(x, c, mod_w, mod_b, norm_pre, norm_post, ffn_w_in, ffn_w_out, mix_w_in, forget_b, ssm_a_re, ssm_a_im, ssm_log_dt, ssm_b_re, ssm_b_im, ssm_c_re, ssm_c_im, ssm_d, glu_w, attn_w_out, mix_w_out, loss_target, m_mod_w, m_mod_b, m_norm_pre, m_norm_post, m_ffn_w_in, m_ffn_w_out, m_mix_w_in, m_forget_b, m_ssm_a_re, m_ssm_a_im, m_ssm_log_dt, m_ssm_b_re, m_ssm_b_im, m_ssm_c_re, m_ssm_c_im, m_ssm_d, m_glu_w, m_attn_w_out, m_mix_w_out, v_mod_w, v_mod_b, v_norm_pre, v_norm_post, v_ffn_w_in, v_ffn_w_out, v_mix_w_in, v_forget_b, v_ssm_a_re, v_ssm_a_im, v_ssm_log_dt, v_ssm_b_re, v_ssm_b_im, v_ssm_c_re, v_ssm_c_im, v_ssm_d, v_glu_w, v_attn_w_out, v_mix_w_out):
    w = dict(mod_w=mod_w, mod_b=mod_b, norm_pre=norm_pre, norm_post=norm_post, ffn_w_in=ffn_w_in,
             ffn_w_out=ffn_w_out, mix_w_in=mix_w_in, forget_b=forget_b, ssm_a_re=ssm_a_re, ssm_a_im=ssm_a_im,
             ssm_log_dt=ssm_log_dt, ssm_b_re=ssm_b_re, ssm_b_im=ssm_b_im, ssm_c_re=ssm_c_re, ssm_c_im=ssm_c_im,
             ssm_d=ssm_d, glu_w=glu_w, attn_w_out=attn_w_out, mix_w_out=mix_w_out)
    m = dict(mod_w=m_mod_w, mod_b=m_mod_b, norm_pre=m_norm_pre, norm_post=m_norm_post, ffn_w_in=m_ffn_w_in,
             ffn_w_out=m_ffn_w_out, mix_w_in=m_mix_w_in, forget_b=m_forget_b, ssm_a_re=m_ssm_a_re,
             ssm_a_im=m_ssm_a_im, ssm_log_dt=m_ssm_log_dt, ssm_b_re=m_ssm_b_re, ssm_b_im=m_ssm_b_im,
             ssm_c_re=m_ssm_c_re, ssm_c_im=m_ssm_c_im, ssm_d=m_ssm_d, glu_w=m_glu_w, attn_w_out=m_attn_w_out,
             mix_w_out=m_mix_w_out)
    v = dict(mod_w=v_mod_w, mod_b=v_mod_b, norm_pre=v_norm_pre, norm_post=v_norm_post, ffn_w_in=v_ffn_w_in,
             ffn_w_out=v_ffn_w_out, mix_w_in=v_mix_w_in, forget_b=v_forget_b, ssm_a_re=v_ssm_a_re,
             ssm_a_im=v_ssm_a_im, ssm_log_dt=v_ssm_log_dt, ssm_b_re=v_ssm_b_re, ssm_b_im=v_ssm_b_im,
             ssm_c_re=v_ssm_c_re, ssm_c_im=v_ssm_c_im, ssm_d=v_ssm_d, glu_w=v_glu_w, attn_w_out=v_attn_w_out,
             mix_w_out=v_mix_w_out)
    return _train_step(x[0], c, loss_target[0], w, m, v)
```

```python
import functools
import math
from typing import NamedTuple

import jax
import jax.numpy as jnp
import numpy as np
from jax import lax
from jax.experimental import pallas as pl
from jax.experimental.pallas import tpu as pltpu

F32 = jnp.float32
BF16 = jnp.bfloat16

D_MODEL = 1024
DEPTH = 2
SSM_WIDTH = 512
SSM_GROUP = 16
SSM_GROUPS = 32
SSM_STATE = 64
SSM_FLAT = SSM_GROUPS * SSM_STATE
SSM_BLOCKS = 4
ATTN_HEADS = 8
HEAD_DIM = 64
ATTN_WIDTH = 512
D_FF = 2816
FFN_RES = 0.5
N_SUB = 3
RMS_EPS = 1e-6
N_CHIPS = 4
N_DEV = 8

ADAM_LR = 0.001
ADAM_B1 = 0.9
ADAM_B2 = 0.999
ADAM_EPS = 1e-08
ADAM_WD = 0.01
ADAM_STEP = 10

LANES = 128
SUBLANES = 8
VMEM_LIMIT = 52 * 1024 * 1024
MESH = pl.DeviceIdType.MESH

NN = (((1,), (0,)), ((), ()))
NT = (((1,), (1,)), ((), ()))
TN = (((0,), (0,)), ((), ()))


def _tile(dim, target, align=LANES):
    best = None
    t = align
    while t <= min(dim, target):
        if dim % t == 0:
            best = t
        t += align
    return dim if best is None else best


def _params(*sem):
    return pltpu.CompilerParams(dimension_semantics=sem, vmem_limit_bytes=VMEM_LIMIT)


def _mm(a, b, *, name, ta=False, tb=False, out_dtype=F32, bias=None, bscale=None, b_k0=0,
        out_n=None, out_j0=0, into=None, tm=1024, tn=1024, tk=2048):
    M, K = (a.shape[1], a.shape[0]) if ta else a.shape
    N = b.shape[0] if tb else b.shape[1]
    assert b_k0 + K <= (b.shape[1] if tb else b.shape[0]), (a.shape, b.shape, ta, tb)
    tm, tn, tk = _tile(M, tm), _tile(N, tn), _tile(K, tk)
    nk = K // tk
    assert b_k0 % tk == 0
    kb0 = b_k0 // tk
    dn = (((0 if ta else 1,), (1 if tb else 0,)), ((), ()))
    has_bias, has_scale = bias is not None, bscale is not None

    def body(*refs):
        a_ref, b_ref = refs[0], refs[1]
        pos = 2
        bias_ref = scale_ref = None
        if has_bias:
            bias_ref = refs[pos]
            pos += 1
        if has_scale:
            scale_ref = refs[pos]
            pos += 1
        if into is not None:
            pos += 1
        o_ref = refs[pos]
        acc_ref = refs[pos + 1] if nk > 1 else None

        def finish(r):
            if has_bias:
                extra = bias_ref[...].astype(F32)
                if has_scale:
                    extra = extra * scale_ref[...]
                r = r + extra
            o_ref[...] = r.astype(out_dtype)

        part = lax.dot_general(a_ref[...].astype(BF16), b_ref[...].astype(BF16), dn,
                               preferred_element_type=F32)
        if nk == 1:
            finish(part)
        else:
            k = pl.program_id(2)

            @pl.when(k == 0)
            def _():
                acc_ref[...] = part

            @pl.when(k > 0)
            def _():
                acc_ref[...] += part

            @pl.when(k == nk - 1)
            def _():
                finish(acc_ref[...])

    a_spec = pl.BlockSpec((tk, tm), lambda j, i, k: (k, i)) if ta else pl.BlockSpec((tm, tk), lambda j, i, k: (i, k))
    b_spec = (pl.BlockSpec((tn, tk), lambda j, i, k: (j, kb0 + k)) if tb
              else pl.BlockSpec((tk, tn), lambda j, i, k: (kb0 + k, j)))
    in_specs = [a_spec, b_spec]
    args = [a, b]
    if has_bias:
        in_specs.append(pl.BlockSpec((tm, tn), lambda j, i, k: (i, j)))
        args.append(bias)
    if has_scale:
        in_specs.append(pl.BlockSpec((1, tn), lambda j, i, k: (0, j)))
        args.append(bscale)
    aliases = {}
    if into is not None:
        in_specs.append(pl.BlockSpec(memory_space=pl.ANY))
        args.append(into)
        aliases = {len(args) - 1: 0}
    out_n = N if out_n is None else out_n
    assert out_j0 % tn == 0
    jb0 = out_j0 // tn
    return pl.pallas_call(
        body, name=name,
        grid=(N // tn, M // tm, nk),
        in_specs=in_specs,
        out_specs=pl.BlockSpec((tm, tn), lambda j, i, k: (i, jb0 + j)),
        out_shape=jax.ShapeDtypeStruct((M, out_n), out_dtype),
        scratch_shapes=[pltpu.VMEM((tm, tn), F32)] if nk > 1 else [],
        input_output_aliases=aliases,
        compiler_params=_params("parallel", "parallel", "arbitrary"),
    )(*args)


def _mm_bd(a_list, b_list, *, name, tb=False, out_dtype=F32, bias=None, bscale=None, tm=1024):
    G = b_list[0].shape[0]
    Kb, Nb = (b_list[0].shape[2], b_list[0].shape[1]) if tb else b_list[0].shape[1:]
    M = a_list[0].shape[0]
    tm = _tile(M, tm)
    na, nb = len(a_list), len(b_list)
    n_out = nb if na == 1 else 1
    dn = NT if tb else NN
    has_bias, has_scale = bias is not None, bscale is not None

    def body(*refs):
        a_refs, b_refs = refs[:na], refs[na:na + nb]
        pos = na + nb
        bias_ref = scale_ref = None
        if has_bias:
            bias_ref = refs[pos]
            pos += 1
        if has_scale:
            scale_ref = refs[pos]
            pos += 1
        o_refs = refs[pos:]
        prods = [lax.dot_general(a_refs[min(i, na - 1)][...].astype(BF16), b_refs[i][...].astype(BF16), dn,
                                 preferred_element_type=F32) for i in range(nb)]
        outs = prods if n_out == nb else [functools.reduce(jnp.add, prods)]
        for o_ref, r in zip(o_refs, outs):
            if has_bias:
                extra = bias_ref[...].astype(F32)
                r = r + (extra * scale_ref[...] if has_scale else extra)
            o_ref[...] = r.astype(out_dtype)

    a_spec = pl.BlockSpec((tm, Kb), lambda g, i: (i, g))
    b_spec = pl.BlockSpec((None,) + b_list[0].shape[1:], lambda g, i: (g, 0, 0))
    o_spec = pl.BlockSpec((tm, Nb), lambda g, i: (i, g))
    in_specs = [a_spec] * na + [b_spec] * nb
    args = list(a_list) + list(b_list)
    if has_bias:
        in_specs.append(o_spec)
        args.append(bias)
    if has_scale:
        in_specs.append(pl.BlockSpec((1, Nb), lambda g, i: (0, g)))
        args.append(bscale)
    sds = jax.ShapeDtypeStruct((M, G * Nb), out_dtype)
    res = pl.pallas_call(
        body, name=name, grid=(G, M // tm), in_specs=in_specs, out_specs=[o_spec] * n_out,
        out_shape=[sds] * n_out, compiler_params=_params("parallel", "parallel"),
    )(*args)
    return res[0] if n_out == 1 else res


def _mm_bd_t(a, b, G, *, name, tk=1024):
    K, Mb, Nb = a.shape[0], a.shape[1] // G, b.shape[1] // G
    tk = _tile(K, tk)
    nk = K // tk

    def body(a_ref, b_ref, o_ref, acc_ref):
        k = pl.program_id(1)
        part = lax.dot_general(a_ref[...].astype(BF16), b_ref[...].astype(BF16), TN, preferred_element_type=F32)

        @pl.when(k == 0)
        def _():
            acc_ref[...] = part

        @pl.when(k > 0)
        def _():
            acc_ref[...] += part

        @pl.when(k == nk - 1)
        def _():
            o_ref[...] = acc_ref[...]

    return pl.pallas_call(
        body, name=name, grid=(G, nk),
        in_specs=[pl.BlockSpec((tk, Mb), lambda g, k: (k, g)), pl.BlockSpec((tk, Nb), lambda g, k: (k, g))],
        out_specs=pl.BlockSpec((None, Mb, Nb), lambda g, k: (g, 0, 0)),
        out_shape=jax.ShapeDtypeStruct((G, Mb, Nb), F32),
        scratch_shapes=[pltpu.VMEM((Mb, Nb), F32)], compiler_params=_params("parallel", "arbitrary"),
    )(a, b)


def _sigmoid(x):
    return 0.5 * jnp.tanh(0.5 * x) + 0.5


def _mm_swiglu(h, w_in, *, name, tm=512, tn=1408, exchange=None):
    M, K = h.shape
    N = w_in.shape[1] // 2
    tm, tn = _tile(M, tm), _tile(N, tn)
    nj = N // tn

    def body(h_ref, wg_ref, wu_ref, g_ref, u_ref, a_ref):
        hv = h_ref[...]
        g = jnp.dot(hv, wg_ref[...], preferred_element_type=F32)
        u = jnp.dot(hv, wu_ref[...], preferred_element_type=F32)
        g_ref[...] = g.astype(BF16)
        u_ref[...] = u.astype(BF16)
        a_ref[...] = (g * _sigmoid(g) * u).astype(BF16)

    o_spec = pl.BlockSpec((tm, tn), lambda j, i: (i, j))
    sds = jax.ShapeDtypeStruct((M, N), BF16)
    return _call_with_exchange(
        body, exchange, name=name, grid=(nj, M // tm),
        in_specs=[pl.BlockSpec((tm, K), lambda j, i: (i, 0)), pl.BlockSpec((K, tn), lambda j, i: (0, j)),
                  pl.BlockSpec((K, tn), lambda j, i: (0, nj + j))],
        out_specs=[o_spec, o_spec, o_spec], out_shape=[sds, sds, sds], operands=(h, w_in, w_in))


def _mm_swiglu_bwd(dy, w_out, gate, up, *, name, tm=512, tn=1408, exchange=None):
    M, K = dy.shape
    N = w_out.shape[0]
    tm, tn = _tile(M, tm), _tile(N, tn)

    def body(dy_ref, w_ref, g_ref, u_ref, dg_ref, du_ref):
        dact = lax.dot_general(dy_ref[...], w_ref[...], NT, preferred_element_type=F32)
        g = g_ref[...].astype(F32)
        u = u_ref[...].astype(F32)
        sig = _sigmoid(g)
        dg_ref[...] = (dact * u * (sig * (1.0 + g * (1.0 - sig)))).astype(BF16)
        du_ref[...] = (dact * (g * sig)).astype(BF16)

    t_spec = pl.BlockSpec((tm, tn), lambda j, i: (i, j))
    sds = jax.ShapeDtypeStruct((M, N), BF16)
    return _call_with_exchange(
        body, exchange, name=name, grid=(N // tn, M // tm),
        in_specs=[pl.BlockSpec((tm, K), lambda j, i: (i, 0)), pl.BlockSpec((tn, K), lambda j, i: (j, 0)),
                  t_spec, t_spec],
        out_specs=[t_spec, t_spec], out_shape=[sds, sds], operands=(dy, w_out, gate, up))


ROW_TILE = 1024
MERGE_TILE = 512
GELU_TILE = 2048


def _colsum8(v):
    return jnp.sum(v.reshape(v.shape[0] // SUBLANES, SUBLANES, v.shape[1]), axis=0)


def _finish_colsums(step, last, refs):
    @pl.when(step == last)
    def _():
        for r in refs:
            r[...] = jnp.broadcast_to(jnp.sum(r[...], axis=0, keepdims=True), r.shape)


def _row_spec(t, d):
    return pl.BlockSpec((t, d), lambda i: (i, 0))


def _vec_spec(d, rows=1):
    return pl.BlockSpec((rows, d), lambda i: (0, 0))


def _prenorm(x, g, sc, sh, *, name):
    L, D = x.shape
    t = _tile(L, ROW_TILE, SUBLANES)

    def body(x_ref, g_ref, sc_ref, sh_ref, h_ref):
        xv = x_ref[...]
        r = lax.rsqrt(jnp.mean(xv * xv, axis=-1, keepdims=True) + RMS_EPS)
        h_ref[...] = (((xv * r) * g_ref[...]) * (1.0 + sc_ref[...]) + sh_ref[...]).astype(BF16)

    return pl.pallas_call(
        body, name=name, grid=(L // t,),
        in_specs=[_row_spec(t, D), _vec_spec(D), _vec_spec(D), _vec_spec(D)],
        out_specs=_row_spec(t, D), out_shape=jax.ShapeDtypeStruct((L, D), BF16),
        compiler_params=_params("parallel"),
    )(x, g, sc, sh)


def _mm_postnorm(a, b, x, g, gate, res_w, *, name, tm=512):
    M, K = a.shape
    D = b.shape[1]
    tm = _tile(M, tm, SUBLANES)

    def body(a_ref, b_ref, x_ref, g_ref, gate_ref, y_ref, o_ref):
        yv = jnp.dot(a_ref[...], b_ref[...], preferred_element_type=F32)
        y_ref[...] = yv
        r = lax.rsqrt(jnp.mean(yv * yv, axis=-1, keepdims=True) + RMS_EPS)
        o_ref[...] = x_ref[...] + (res_w * gate_ref[...]) * ((yv * r) * g_ref[...])

    sds = jax.ShapeDtypeStruct((M, D), F32)
    return pl.pallas_call(
        body, name=name, grid=(M // tm,),
        in_specs=[_row_spec(tm, K), pl.BlockSpec((K, D), lambda i: (0, 0)), _row_spec(tm, D), _vec_spec(D),
                  _vec_spec(D)],
        out_specs=[_row_spec(tm, D), _row_spec(tm, D)], out_shape=[sds, sds],
        compiler_params=_params("parallel"),
    )(a, b, x, g, gate)


def _postnorm_bwd(dxo, y, g, gate, res_w, *, name, exchange=None):
    L, D = y.shape
    t = _tile(L, ROW_TILE, SUBLANES)
    n = L // t

    def body(dxo_ref, y_ref, g_ref, gate_ref, dy_ref, dgate_ref, dg_ref):
        i = pl.program_id(0)

        @pl.when(i == 0)
        def _():
            dgate_ref[...] = jnp.zeros_like(dgate_ref)
            dg_ref[...] = jnp.zeros_like(dg_ref)

        yv = y_ref[...]
        dv = dxo_ref[...]
        gv = g_ref[...]
        r = lax.rsqrt(jnp.mean(yv * yv, axis=-1, keepdims=True) + RMS_EPS)
        yn = yv * r
        dgate_ref[...] += _colsum8(dv * (res_w * (yn * gv)))
        do = dv * (res_w * gate_ref[...])
        dg_ref[...] += _colsum8(do * yn)
        dyn = do * gv
        dy_ref[...] = (r * (dyn - yn * jnp.mean(dyn * yn, axis=-1, keepdims=True))).astype(BF16)
        _finish_colsums(i, n - 1, (dgate_ref, dg_ref))

    sum_sds = jax.ShapeDtypeStruct((SUBLANES, D), F32)
    return _call_with_exchange(
        body, exchange, name=name, grid=(n,), sem=("arbitrary",),
        in_specs=[_row_spec(t, D), _row_spec(t, D), _vec_spec(D), _vec_spec(D)],
        out_specs=[_row_spec(t, D), _vec_spec(D, SUBLANES), _vec_spec(D, SUBLANES)],
        out_shape=[jax.ShapeDtypeStruct((L, D), BF16), sum_sds, sum_sds], operands=(dxo, y, g, gate))


def _prenorm_bwd(x, dh, dxres, g, sc, *, name):
    L, D = x.shape
    t = _tile(L, ROW_TILE, SUBLANES)
    n = L // t

    def body(x_ref, dh_ref, dxr_ref, g_ref, sc_ref, dx_ref, dsh_ref, dsc_ref, dg_ref):
        i = pl.program_id(0)

        @pl.when(i == 0)
        def _():
            dsh_ref[...] = jnp.zeros_like(dsh_ref)
            dsc_ref[...] = jnp.zeros_like(dsc_ref)
            dg_ref[...] = jnp.zeros_like(dg_ref)

        xv = x_ref[...]
        dhv = dh_ref[...].astype(F32)
        gv = g_ref[...]
        one_sc = 1.0 + sc_ref[...]
        r = lax.rsqrt(jnp.mean(xv * xv, axis=-1, keepdims=True) + RMS_EPS)
        xn = xv * r
        tt = dhv * xn
        dsh_ref[...] += _colsum8(dhv)
        dsc_ref[...] += _colsum8(tt * gv)
        dg_ref[...] += _colsum8(tt * one_sc)
        dxn = dhv * (gv * one_sc)
        dx_ref[...] = dxr_ref[...] + r * (dxn - xn * jnp.mean(dxn * xn, axis=-1, keepdims=True))
        _finish_colsums(i, n - 1, (dsh_ref, dsc_ref, dg_ref))

    sum_sds = jax.ShapeDtypeStruct((SUBLANES, D), F32)
    sum_spec = _vec_spec(D, SUBLANES)
    return pl.pallas_call(
        body, name=name, grid=(n,),
        in_specs=[_row_spec(t, D), _row_spec(t, D), _row_spec(t, D), _vec_spec(D), _vec_spec(D)],
        out_specs=[_row_spec(t, D), sum_spec, sum_spec, sum_spec],
        out_shape=[jax.ShapeDtypeStruct((L, D), F32), sum_sds, sum_sds, sum_sds],
        compiler_params=_params("arbitrary"),
    )(x, dh, dxres, g, sc)


def _loss_head(y, target, *, name):
    L, D = y.shape
    t = _tile(L, ROW_TILE, SUBLANES)
    n = L // t

    def body(y_ref, t_ref, dy_ref, loss_ref):
        i = pl.program_id(0)

        @pl.when(i == 0)
        def _():
            loss_ref[...] = jnp.zeros_like(loss_ref)

        e = y_ref[...] - t_ref[...]
        dy_ref[...] = e * (1.0 / D)
        part = jnp.sum(jnp.mean(e * e, axis=-1, keepdims=True), axis=0, keepdims=True)
        loss_ref[...] += jnp.broadcast_to(0.5 * part, loss_ref.shape)

    return pl.pallas_call(
        body, name=name, grid=(n,),
        in_specs=[_row_spec(t, D), _row_spec(t, D)],
        out_specs=[_row_spec(t, D), pl.BlockSpec((SUBLANES, LANES), lambda i: (0, 0))],
        out_shape=[jax.ShapeDtypeStruct((L, D), F32), jax.ShapeDtypeStruct((SUBLANES, LANES), F32)],
        compiler_params=_params("arbitrary"),
    )(y, target)


GELU_C = math.sqrt(2.0 / math.pi)


def _gelu_fwd(y, *, name):
    L, W = y.shape
    t = _tile(L, GELU_TILE, SUBLANES)

    def body(y_ref, o_ref):
        v = y_ref[...]
        o_ref[...] = (0.5 * v * (1.0 + jnp.tanh(GELU_C * (v + 0.044715 * (v * v * v))))).astype(BF16)

    return pl.pallas_call(
        body, name=name, grid=(L // t,), in_specs=[_row_spec(t, W)], out_specs=_row_spec(t, W),
        out_shape=jax.ShapeDtypeStruct((L, W), BF16), compiler_params=_params("parallel"),
    )(y)


def _gelu_bwd(dgl, y, u, dskip, *, name):
    L, W = y.shape
    t = _tile(L, GELU_TILE, SUBLANES)
    n = L // t

    def body(dgl_ref, y_ref, u_ref, d_ref, dy_ref, sk_ref, dd_ref):
        i = pl.program_id(0)

        @pl.when(i == 0)
        def _():
            dd_ref[...] = jnp.zeros_like(dd_ref)

        v = y_ref[...]
        inner = GELU_C * (v + 0.044715 * (v * v * v))
        th = jnp.tanh(inner)
        dgelu = 0.5 * (1.0 + th) + 0.5 * v * (1.0 - th * th) * (GELU_C * (1.0 + 3.0 * 0.044715 * (v * v)))
        dy = dgl_ref[...] * dgelu
        dy_ref[...] = dy.astype(BF16)
        sk_ref[...] = dy * d_ref[...]
        dd_ref[...] += _colsum8(dy * u_ref[...])
        _finish_colsums(i, n - 1, (dd_ref,))

    return pl.pallas_call(
        body, name=name, grid=(n,),
        in_specs=[_row_spec(t, W), _row_spec(t, W), _row_spec(t, W), _vec_spec(W)],
        out_specs=[_row_spec(t, W), _row_spec(t, W), _vec_spec(W, SUBLANES)],
        out_shape=[jax.ShapeDtypeStruct((L, W), BF16), jax.ShapeDtypeStruct((L, W), F32),
                   jax.ShapeDtypeStruct((SUBLANES, W), F32)],
        compiler_params=_params("arbitrary"),
    )(dgl, y, u, dskip)


def _merge_fwd(z, yb, gab, *, name):
    L, D = yb.shape
    t = _tile(L, MERGE_TILE, SUBLANES)

    def body(z_ref, yb_ref, gab_ref, o_ref):
        ya = z_ref[:, :D] * _sigmoid(z_ref[:, D:])
        o_ref[...] = (_sigmoid(gab_ref[:, :D]) * ya + _sigmoid(gab_ref[:, D:]) * yb_ref[...]).astype(BF16)

    return pl.pallas_call(
        body, name=name, grid=(L // t,),
        in_specs=[_row_spec(t, 2 * D), _row_spec(t, D), _row_spec(t, 2 * D)],
        out_specs=_row_spec(t, D), out_shape=jax.ShapeDtypeStruct((L, D), BF16),
        compiler_params=_params("parallel"),
    )(z, yb, gab)


def _merge_bwd(dm, z, yb, gab, *, name, exchange=None):
    L, D = yb.shape
    t = _tile(L, MERGE_TILE, SUBLANES)

    def body(dm_ref, z_ref, yb_ref, gab_ref, dz_ref, dyb_ref, dgab_ref):
        dmv = dm_ref[...]
        zv = z_ref[:, :D]
        sz = _sigmoid(z_ref[:, D:])
        sa = _sigmoid(gab_ref[:, :D])
        sb = _sigmoid(gab_ref[:, D:])
        ybv = yb_ref[...]
        dya = dmv * sa
        dz_ref[:, :D] = (dya * sz).astype(BF16)
        dz_ref[:, D:] = (dya * zv * (sz * (1.0 - sz))).astype(BF16)
        dyb_ref[...] = (dmv * sb).astype(BF16)
        dgab_ref[:, :D] = (dmv * (zv * sz) * (sa * (1.0 - sa))).astype(BF16)
        dgab_ref[:, D:] = (dmv * ybv * (sb * (1.0 - sb))).astype(BF16)

    return _call_with_exchange(
        body, exchange, name=name, grid=(L // t,),
        in_specs=[_row_spec(t, D), _row_spec(t, 2 * D), _row_spec(t, D), _row_spec(t, 2 * D)],
        out_specs=[_row_spec(t, 2 * D), _row_spec(t, D), _row_spec(t, 2 * D)],
        out_shape=[jax.ShapeDtypeStruct((L, 2 * D), BF16), jax.ShapeDtypeStruct((L, D), BF16),
                   jax.ShapeDtypeStruct((L, 2 * D), BF16)], operands=(dm, z, yb, gab))


SCAN_W = 1024
SCAN_T = 512


def _interleave(x):
    L, W = x.shape
    seg = SCAN_T // SUBLANES
    return x.reshape(L // SCAN_T, SUBLANES, seg, W).transpose(0, 2, 1, 3).reshape(L, W)


def _deinterleave(x):
    L, W = x.shape
    seg = SCAN_T // SUBLANES
    return x.reshape(L // SCAN_T, seg, SUBLANES, W).transpose(0, 2, 1, 3).reshape(L, W)


def _power_table(a, b, pr_tab, pi_tab, n):
    def fill(k, carry):
        pr, pi = carry
        pr_tab[k] = pr
        pi_tab[k] = pi
        return a * pr - b * pi, a * pi + b * pr

    lax.fori_loop(0, n, fill, (a, b))


def _rows_to_tile(rows):
    w = rows[0].shape[1]
    sub = lax.broadcasted_iota(jnp.int32, (SUBLANES, w), 0)
    tile = jnp.broadcast_to(rows[0], (SUBLANES, w))
    for j in range(1, SUBLANES):
        tile = jnp.where(sub == j, jnp.broadcast_to(rows[j], (SUBLANES, w)), tile)
    return tile


def _ssm_scan_fwd(bu_re, bu_im, lam_re, lam_im, *, name):
    L, S = bu_re.shape
    w, t = _tile(S, SCAN_W), SCAN_T
    seg = t // SUBLANES

    def body(br_ref, bi_ref, lr_ref, li_ref, sr_ref, si_ref, pr_tab, pi_tab, cr_ref, ci_ref):
        a = jnp.broadcast_to(lr_ref[...], (SUBLANES, w))
        b = jnp.broadcast_to(li_ref[...], (SUBLANES, w))

        @pl.when(pl.program_id(1) == 0)
        def _():
            cr_ref[...] = jnp.zeros_like(cr_ref)
            ci_ref[...] = jnp.zeros_like(ci_ref)
            _power_table(a, b, pr_tab, pi_tab, seg)

        def local_scan(i, carry):
            sr, si = carry
            rows = pl.ds(pl.multiple_of(i * SUBLANES, SUBLANES), SUBLANES)
            nr = a * sr - b * si + br_ref[rows, :]
            ni = a * si + b * sr + bi_ref[rows, :]
            sr_ref[rows, :] = nr
            si_ref[rows, :] = ni
            return nr, ni

        zero = jnp.zeros((SUBLANES, w), F32)
        fr, fi = lax.fori_loop(0, seg, local_scan, (zero, zero), unroll=2)
        lsr, lsi = pr_tab[seg - 1][0:1, :], pi_tab[seg - 1][0:1, :]
        cr, ci = cr_ref[...], ci_ref[...]
        rows_r, rows_i = [], []
        for j in range(SUBLANES):
            rows_r.append(cr)
            rows_i.append(ci)
            cr, ci = fr[j:j + 1, :] + (lsr * cr - lsi * ci), fi[j:j + 1, :] + (lsr * ci + lsi * cr)
        cr_ref[...] = cr
        ci_ref[...] = ci
        in_r, in_i = _rows_to_tile(rows_r), _rows_to_tile(rows_i)

        def add_entry(i, _):
            rows = pl.ds(pl.multiple_of(i * SUBLANES, SUBLANES), SUBLANES)
            pr, pi = pr_tab[i], pi_tab[i]
            sr_ref[rows, :] += pr * in_r - pi * in_i
            si_ref[rows, :] += pr * in_i + pi * in_r
            return 0

        lax.fori_loop(0, seg, add_entry, 0, unroll=2)

    blk = pl.BlockSpec((t, w), lambda j, i: (i, j))
    vec = pl.BlockSpec((1, w), lambda j, i: (0, j))
    sds = jax.ShapeDtypeStruct((L, S), F32)
    tab = pltpu.VMEM((seg, SUBLANES, w), F32)
    return pl.pallas_call(
        body, name=name, grid=(S // w, L // t),
        in_specs=[blk, blk, vec, vec], out_specs=[blk, blk], out_shape=[sds, sds],
        scratch_shapes=[tab, tab, pltpu.VMEM((1, w), F32), pltpu.VMEM((1, w), F32)],
        compiler_params=_params("parallel", "arbitrary"),
    )(bu_re, bu_im, lam_re, lam_im)


def _ssm_scan_bwd(d_re, d_im, s_re, s_im, lam_re, lam_im, *, name):
    L, S = d_re.shape
    w, t = _tile(S, SCAN_W), SCAN_T
    nt = L // t
    seg = t // SUBLANES

    def body(dr_ref, di_ref, sr_ref, si_ref, lr_ref, li_ref, gr_ref, gi_ref, ar_ref, ai_ref,
             pr_tab, pi_tab, cgr, cgi, acc_r, acc_i):
        step = pl.program_id(1)
        a = jnp.broadcast_to(lr_ref[...], (SUBLANES, w))
        b = jnp.broadcast_to(-li_ref[...], (SUBLANES, w))

        @pl.when(step == 0)
        def _():
            for r in (cgr, cgi, acc_r, acc_i):
                r[...] = jnp.zeros_like(r)
            _power_table(a, b, pr_tab, pi_tab, seg)

        def local_scan(ii, carry):
            gr, gi = carry
            rows = pl.ds(pl.multiple_of((seg - 1 - ii) * SUBLANES, SUBLANES), SUBLANES)
            ngr = a * gr - b * gi + dr_ref[rows, :]
            ngi = a * gi + b * gr + di_ref[rows, :]
            gr_ref[rows, :] = ngr
            gi_ref[rows, :] = ngi
            return ngr, ngi

        zero = jnp.zeros((SUBLANES, w), F32)
        fr, fi = lax.fori_loop(0, seg, local_scan, (zero, zero), unroll=2)
        lsr, lsi = pr_tab[seg - 1][0:1, :], pi_tab[seg - 1][0:1, :]
        cr, ci = cgr[...], cgi[...]
        rows_r, rows_i = [None] * SUBLANES, [None] * SUBLANES
        for j in reversed(range(SUBLANES)):
            rows_r[j], rows_i[j] = cr, ci
            cr, ci = fr[j:j + 1, :] + (lsr * cr - lsi * ci), fi[j:j + 1, :] + (lsr * ci + lsi * cr)
        cgr[...] = cr
        cgi[...] = ci
        in_r, in_i = _rows_to_tile(rows_r), _rows_to_tile(rows_i)

        def add_entry(ii, carry):
            nr, ni, xr, xi = carry
            rows = pl.ds(pl.multiple_of((seg - 1 - ii) * SUBLANES, SUBLANES), SUBLANES)
            sr = sr_ref[rows, :]
            si = si_ref[rows, :]
            xr = xr + (nr * sr + ni * si)
            xi = xi + (ni * sr - nr * si)
            pr, pi = pr_tab[ii], pi_tab[ii]
            gr = gr_ref[rows, :] + (pr * in_r - pi * in_i)
            gi = gi_ref[rows, :] + (pr * in_i + pi * in_r)
            gr_ref[rows, :] = gr
            gi_ref[rows, :] = gi
            return gr, gi, xr, xi

        _, _, xr, xi = lax.fori_loop(0, seg, add_entry, (in_r, in_i, acc_r[...], acc_i[...]), unroll=2)
        acc_r[...] = xr
        acc_i[...] = xi

        @pl.when(step == nt - 1)
        def _():
            ar_ref[...] = jnp.sum(xr, axis=0, keepdims=True)
            ai_ref[...] = jnp.sum(xi, axis=0, keepdims=True)

    blk = pl.BlockSpec((t, w), lambda j, i: (nt - 1 - i, j))
    vec = pl.BlockSpec((1, w), lambda j, i: (0, j))
    sds = jax.ShapeDtypeStruct((L, S), F32)
    vsds = jax.ShapeDtypeStruct((1, S), F32)
    tab = pltpu.VMEM((seg, SUBLANES, w), F32)
    tile = pltpu.VMEM((SUBLANES, w), F32)
    return pl.pallas_call(
        body, name=name, grid=(S // w, nt),
        in_specs=[blk, blk, blk, blk, vec, vec], out_specs=[blk, blk, vec, vec],
        out_shape=[sds, sds, vsds, vsds],
        scratch_shapes=[tab, tab, pltpu.VMEM((1, w), F32), pltpu.VMEM((1, w), F32), tile, tile],
        compiler_params=_params("parallel", "arbitrary"),
    )(d_re, d_im, s_re, s_im, lam_re, lam_im)


def _ssm_discretize(a_re, a_im, log_dt, b_re, b_im, c_re, c_im, d_skip):
    G, P, N = SSM_GROUPS, SSM_STATE, SSM_GROUP
    a = jnp.minimum(a_re, -1e-4)
    dt = jnp.exp(log_dt)[:, None]
    mag = jnp.exp(a * dt)
    lr = mag * jnp.cos(a_im * dt)
    li = mag * jnp.sin(a_im * dt)
    den = a * a + a_im * a_im
    cr = ((lr - 1.0) * a + li * a_im) / den
    ci = (li * a - (lr - 1.0) * a_im) / den
    bbr = cr[..., None] * b_re - ci[..., None] * b_im
    bbi = cr[..., None] * b_im + ci[..., None] * b_re
    gl = G // SSM_BLOCKS
    eye = jnp.eye(gl, dtype=F32)[None, :, None, :, None]

    def in_map(bb):
        t = bb.transpose(0, 2, 1).reshape(SSM_BLOCKS, gl, N, P)
        return (eye * t[:, :, :, None, :]).reshape(SSM_BLOCKS, gl * N, gl * P)

    def out_map(c):
        t = c.transpose(0, 2, 1).reshape(SSM_BLOCKS, gl, P, N)
        return (eye * t[:, :, :, None, :]).reshape(SSM_BLOCKS, gl * P, gl * N)

    return (lr.reshape(1, G * P), li.reshape(1, G * P), in_map(bbr), in_map(bbi),
            out_map(c_re), out_map(-c_im), d_skip.reshape(1, SSM_WIDTH))


ATT_T = 512
ATT_WIDE = 2
CUM_T = 512


def _split3(x):
    hi = x.astype(BF16)
    r1 = x - hi.astype(F32)
    mid = r1.astype(BF16)
    lo = (r1 - mid.astype(F32)).astype(BF16)
    return hi, mid, lo


def _tri_dot(tri, x):
    hi, mid, lo = _split3(x)
    dot = lambda p: jnp.dot(tri, p, preferred_element_type=F32)
    return dot(hi) + dot(mid) + dot(lo)


def _log_sigmoid(x):
    return jnp.minimum(x, 0.0) - jnp.log(1.0 + jnp.exp(-jnp.abs(x)))


def _fox_cum(f, fb, *, name):
    L, W = f.shape
    t = _tile(L, CUM_T, SUBLANES)

    def body(f_ref, b_ref, o_ref, carry):
        @pl.when(pl.program_id(0) == 0)
        def _():
            carry[...] = jnp.zeros_like(carry)

        row = lax.broadcasted_iota(jnp.int32, (t, t), 0)
        col = lax.broadcasted_iota(jnp.int32, (t, t), 1)
        tri = jnp.where(col <= row, 1.0, 0.0).astype(BF16)
        c = _tri_dot(tri, _log_sigmoid(f_ref[...] + b_ref[...])) + carry[...]
        o_ref[...] = c
        carry[...] = c[t - 1:t, :]

    return pl.pallas_call(
        body, name=name, grid=(L // t,),
        in_specs=[_row_spec(t, W), _vec_spec(W)], out_specs=_row_spec(t, W),
        out_shape=jax.ShapeDtypeStruct((L, W), F32),
        scratch_shapes=[pltpu.VMEM((1, W), F32)], compiler_params=_params("arbitrary"),
    )(f, fb)


def _fox_cum_bwd(dcum, f, fb, *, name):
    L, W = f.shape
    t = _tile(L, CUM_T, SUBLANES)
    n = L // t

    def body(d_ref, f_ref, b_ref, o_ref, db_ref, carry):
        i = pl.program_id(0)

        @pl.when(i == 0)
        def _():
            carry[...] = jnp.zeros_like(carry)
            db_ref[...] = jnp.zeros_like(db_ref)

        row = lax.broadcasted_iota(jnp.int32, (t, t), 0)
        col = lax.broadcasted_iota(jnp.int32, (t, t), 1)
        tri = jnp.where(col >= row, 1.0, 0.0).astype(BF16)
        dlog = _tri_dot(tri, d_ref[...]) + carry[...]
        carry[...] = dlog[0:1, :]
        df = dlog * _sigmoid(-(f_ref[...] + b_ref[...]))
        o_ref[...] = df.astype(BF16)
        db_ref[...] += _colsum8(df)
        _finish_colsums(i, n - 1, (db_ref,))

    rev = pl.BlockSpec((t, W), lambda i: (n - 1 - i, 0))
    return pl.pallas_call(
        body, name=name, grid=(n,),
        in_specs=[rev, rev, _vec_spec(W)], out_specs=[rev, _vec_spec(W, SUBLANES)],
        out_shape=[jax.ShapeDtypeStruct((L, W), BF16), jax.ShapeDtypeStruct((SUBLANES, W), F32)],
        scratch_shapes=[pltpu.VMEM((1, W), F32)], compiler_params=_params("arbitrary"),
    )(dcum, f, fb)


def _head_col(blk, h):
    lane = lax.broadcasted_iota(jnp.int32, blk.shape, 1)
    return jnp.sum(jnp.where(lane == h * HEAD_DIM, blk, 0.0), axis=1, keepdims=True)


def _lo_mask(rows):
    return lax.broadcasted_iota(jnp.int32, (rows, LANES), 1) < HEAD_DIM


def _causal(t):
    row = lax.broadcasted_iota(jnp.int32, (t, t), 0)
    col = lax.broadcasted_iota(jnp.int32, (t, t), 1)
    return col <= row


def _call_with_exchange(body, exchange, *, name, grid, in_specs, out_specs, out_shape, operands, sem=None):
    sem = sem or ("parallel",) + ("arbitrary",) * (len(grid) - 1)
    if exchange is None:
        return pl.pallas_call(body, name=name, grid=grid, in_specs=in_specs, out_specs=out_specs,
                              out_shape=out_shape, compiler_params=_params(*sem))(*operands)
    n_in, n_out = len(in_specs), len(out_specs)
    ei, eo = len(exchange.operands), len(exchange.out_shapes)

    def wrapped(*refs):
        ins, ex_in = refs[:n_in], refs[n_in:n_in + ei]
        outs, ex_out = refs[n_in + ei:n_in + ei + n_out], refs[n_in + ei + n_out:n_in + ei + n_out + eo]
        sems = refs[n_in + ei + n_out + eo:]
        ids = [pl.program_id(d) for d in range(len(grid))]
        first = functools.reduce(jnp.logical_and, [i == 0 for i in ids])
        last = functools.reduce(jnp.logical_and, [i == g - 1 for i, g in zip(ids, grid)])

        @pl.when(first)
        def _():
            exchange.start(ex_in, ex_out, sems)

        body(*ins, *outs)

        @pl.when(last)
        def _():
            exchange.finish(ex_in, ex_out, sems)

    return pl.pallas_call(
        wrapped, name=name, grid=grid, in_specs=list(in_specs) + [HBM_SPEC] * ei,
        out_specs=list(out_specs) + [HBM_SPEC] * eo, out_shape=list(out_shape) + list(exchange.out_shapes),
        scratch_shapes=exchange.sems, compiler_params=_params(*(("arbitrary",) * len(grid))),
    )(*operands, *exchange.operands)


def _fox_fwd(qkv, cum_cols, cum_rows, *, name, exchange=None):
    L = qkv.shape[0]
    t = _tile(L, ATT_T)
    nq = L // t
    npair = ATTN_HEADS // 2

    def body(q_ref, k_ref, v_ref, cc_ref, cr_ref, o_ref, o32_ref, lse_ref):
        iq = pl.program_id(1)
        lo = _lo_mask(t)
        qv = q_ref[...] * 0.125
        zq = jnp.zeros_like(qv)
        qh = (jnp.where(lo, qv, zq), jnp.where(lo, zq, qv))
        ccv = cc_ref[...]
        cq = (_head_col(ccv, 0), _head_col(ccv, 1))

        def step(block, width, carry, masked):
            start = pl.multiple_of(block * width, width)
            kb = k_ref[pl.ds(start, width), :]
            vb = v_ref[pl.ds(start, width), :]
            out = []
            for h in range(2):
                m, l, acc = carry[h]
                s = lax.dot_general(qh[h], kb, NT, preferred_element_type=F32)
                s = s + (cq[h] - cr_ref[h:h + 1, pl.ds(start, width)])
                if masked:
                    s = jnp.where(_causal(t), s, -jnp.inf)
                m_new = jnp.maximum(m, jnp.max(s, axis=1, keepdims=True))
                alpha = jnp.exp(m - m_new)
                p = jnp.exp(s - m_new)
                l = alpha * l + jnp.sum(p, axis=1, keepdims=True)
                acc = alpha * acc + jnp.dot(p.astype(BF16), vb, preferred_element_type=F32)
                out.append((m_new, l, acc))
            return tuple(out)

        init1 = (jnp.full((t, 1), -jnp.inf, F32), jnp.zeros((t, 1), F32), jnp.zeros((t, LANES), F32))
        n4, rest = iq // 4, iq % 4
        carry = lax.fori_loop(0, n4, lambda j, c: step(j, 4 * t, c, False), (init1, init1))
        carry = lax.fori_loop(2 * n4, 2 * n4 + rest // 2, lambda j, c: step(j, 2 * t, c, False), carry)
        carry = lax.fori_loop(iq - rest % 2, iq, lambda ik, c: step(ik, t, c, False), carry)
        (m0, l0, a0), (m1, l1, a1) = step(iq, t, carry, True)
        out = jnp.where(lo, a0 / l0, a1 / l1)
        o_ref[...] = out.astype(BF16)
        o32_ref[...] = out
        lse_ref[...] = jnp.where(lo, m0 + jnp.log(l0), m1 + jnp.log(l1))

    blk = lambda off: pl.BlockSpec((t, LANES), lambda hp, iq: (iq, off + hp))
    whole = lambda off: pl.BlockSpec((L, LANES), lambda hp, iq: (0, off + hp))
    return _call_with_exchange(
        body, exchange, name=name, grid=(npair, nq),
        in_specs=[blk(0), whole(npair), whole(2 * npair), blk(0),
                  pl.BlockSpec((None, SUBLANES, L), lambda hp, iq: (hp, 0, 0))],
        out_specs=[blk(0), blk(0), blk(0)],
        out_shape=[jax.ShapeDtypeStruct((L, ATTN_WIDTH), BF16), jax.ShapeDtypeStruct((L, ATTN_WIDTH), F32),
                   jax.ShapeDtypeStruct((L, ATTN_WIDTH), F32)],
        operands=(qkv, qkv, qkv, cum_cols, cum_rows))


STAT_LSE, STAT_CUM, STAT_DELTA = 0, 2, 4


def _lane_col(blk, idx):
    lane = lax.broadcasted_iota(jnp.int32, blk.shape, 1)
    return jnp.sum(jnp.where(lane == idx, blk, 0.0), axis=1, keepdims=True)


def _fox_rowstats(do, o, lse, cum_cols, *, name):
    L = do.shape[0]
    t = _tile(L, ATT_T)

    def body(do_ref, o_ref, lse_ref, cc_ref, st_ref):
        lo = _lo_mask(t)
        dd = do_ref[...].astype(F32) * o_ref[...]
        lsev, ccv = lse_ref[...], cc_ref[...]
        cols = (_head_col(lsev, 0), _head_col(lsev, 1), _head_col(ccv, 0), _head_col(ccv, 1),
                jnp.sum(jnp.where(lo, dd, 0.0), axis=1, keepdims=True),
                jnp.sum(jnp.where(lo, 0.0, dd), axis=1, keepdims=True))
        lane = lax.broadcasted_iota(jnp.int32, (t, LANES), 1)
        out = jnp.zeros((t, LANES), F32)
        for i, col in enumerate(cols):
            out = jnp.where(lane == i, col, out)
        st_ref[...] = out

    blk = pl.BlockSpec((t, LANES), lambda hp, i: (i, hp))
    return pl.pallas_call(
        body, name=name, grid=(ATTN_HEADS // 2, L // t),
        in_specs=[blk, blk, blk, blk], out_specs=blk,
        out_shape=jax.ShapeDtypeStruct((L, ATTN_WIDTH), F32),
        compiler_params=_params("parallel", "parallel"),
    )(do, o, lse, cum_cols)


def _fox_bwd(qkv, do, stats, cum_rows, *, name, exchange=None):
    L = qkv.shape[0]
    t = _tile(L, ATT_T)
    nq = L // t
    npair = ATTN_HEADS // 2

    def body(q_ref, do_ref, st_ref, qt_ref, dot_ref, k_ref, v_ref, cr_ref, dk_ref, dv_ref, dc_ref, dq_ref, drow_ref):
        ik = pl.program_id(1)

        @pl.when(ik == 0)
        def _():
            dq_ref[...] = jnp.zeros_like(dq_ref)
            drow_ref[...] = jnp.zeros_like(drow_ref)

        lo = _lo_mask(t)
        lo_rows = lax.broadcasted_iota(jnp.int32, (LANES, t), 0) < HEAD_DIM
        lane = lax.broadcasted_iota(jnp.int32, (t, LANES), 1)
        kb = k_ref[...]
        vb = v_ref[...]
        zk = jnp.zeros_like(kb)
        kh = (jnp.where(lo, kb, zk), jnp.where(lo, zk, kb))
        vh = (jnp.where(lo, vb, zk), jnp.where(lo, zk, vb))
        ck = (cr_ref[0:1, :], cr_ref[1:2, :])

        def step(block, width, carry, masked):
            dk, dv, dc0, dc1 = carry
            start = pl.multiple_of(block * width, width)
            qb = q_ref[pl.ds(start, width), :] * 0.125
            dob = do_ref[pl.ds(start, width), :]
            stb = st_ref[pl.ds(start, width), :]
            lane = lax.broadcasted_iota(jnp.int32, (width, LANES), 1)
            qtb = qt_ref[:, pl.ds(start, width)] * 0.125
            dotb = dot_ref[:, pl.ds(start, width)]
            dks, dvs, dcs, dqs, rss = [], [], [], [], []
            for h in range(2):
                s = lax.dot_general(qb, kh[h], NT, preferred_element_type=F32)
                s = s + (_lane_col(stb, STAT_CUM + h) - ck[h])
                if masked:
                    s = jnp.where(_causal(t), s, -jnp.inf)
                p = jnp.exp(s - _lane_col(stb, STAT_LSE + h))
                dp = lax.dot_general(dob, vh[h], NT, preferred_element_type=F32)
                ds = p * (dp - _lane_col(stb, STAT_DELTA + h))
                dsb = ds.astype(BF16)
                dvs.append(jnp.dot(dotb, p.astype(BF16), preferred_element_type=F32))
                dks.append(jnp.dot(qtb, dsb, preferred_element_type=F32))
                dqs.append(jnp.dot(dsb, kh[h], preferred_element_type=F32))
                dcs.append(jnp.sum(ds, axis=0, keepdims=True))
                rss.append(jnp.sum(ds, axis=1, keepdims=True))
            dq_ref[pl.ds(start, width), :] += 0.125 * (dqs[0] + dqs[1])
            drow_ref[pl.ds(start, width), :] += jnp.where(lane == 0, rss[0], jnp.where(lane == 1, rss[1], 0.0))
            return (dk + jnp.where(lo_rows, dks[0], dks[1]), dv + jnp.where(lo_rows, dvs[0], dvs[1]),
                    dc0 - dcs[0], dc1 - dcs[1])

        zero = jnp.zeros((LANES, t), F32)
        zrow = jnp.zeros((1, t), F32)
        carry = step(ik, t, (zero, zero, zrow, zrow), True)
        first_wide = (ik + ATT_WIDE) // ATT_WIDE
        carry = lax.fori_loop(ik + 1, jnp.minimum(first_wide * ATT_WIDE, nq), lambda iq, c: step(iq, t, c, False),
                              carry)
        dk, dv, dc0, dc1 = lax.fori_loop(first_wide, nq // ATT_WIDE,
                                         lambda j, c: step(j, ATT_WIDE * t, c, False), carry)
        dk_ref[...] = dk.T.astype(BF16)
        dv_ref[...] = dv.T.astype(BF16)
        dc_ref[...] = jnp.zeros_like(dc_ref)
        dc_ref[0:1, :] = dc0
        dc_ref[1:2, :] = dc1

    whole = lambda off: pl.BlockSpec((L, LANES), lambda hp, ik: (0, off + hp))
    blk = lambda off: pl.BlockSpec((t, LANES), lambda hp, ik: (ik, off + hp))
    rows = pl.BlockSpec((None, SUBLANES, t), lambda hp, ik: (hp, 0, ik))
    whole_t = pl.BlockSpec((LANES, L), lambda hp, ik: (hp, 0))
    return _call_with_exchange(
        body, exchange, name=name, grid=(npair, nq),
        in_specs=[whole(0), whole(0), whole(0), whole_t, whole_t, blk(npair), blk(2 * npair), rows],
        out_specs=[blk(0), blk(0), rows, whole(0), whole(0)],
        out_shape=[jax.ShapeDtypeStruct((L, ATTN_WIDTH), BF16), jax.ShapeDtypeStruct((L, ATTN_WIDTH), BF16),
                   jax.ShapeDtypeStruct((npair, SUBLANES, L), F32),
                   jax.ShapeDtypeStruct((L, ATTN_WIDTH), F32), jax.ShapeDtypeStruct((L, ATTN_WIDTH), F32)],
        operands=(qkv, do, stats, qkv[:, :ATTN_WIDTH].T, do.T, qkv, qkv, cum_rows))


def _mod_partial(c_all, mod_w, mod_b_cols, *, name):
    depth, K, cols = mod_w.shape
    tn = _tile(cols, 768)

    def body(c_ref, w_ref, b_ref, o_ref):
        cv = c_ref[...]
        sc = (cv * _sigmoid(cv)).astype(BF16)
        o_ref[...] = jnp.dot(sc, w_ref[...].astype(BF16), preferred_element_type=F32) + b_ref[...]

    return pl.pallas_call(
        body, name=name, grid=(depth, cols // tn),
        in_specs=[pl.BlockSpec((N_DEV, K), lambda l, j: (0, 0)),
                  pl.BlockSpec((None, K, tn), lambda l, j: (l, 0, j)),
                  pl.BlockSpec((None, 1, tn), lambda l, j: (l, 0, j))],
        out_specs=pl.BlockSpec((None, N_DEV, tn), lambda l, j: (l, 0, j)),
        out_shape=jax.ShapeDtypeStruct((depth, N_DEV, cols), F32),
        compiler_params=_params("parallel", "parallel"),
    )(c_all, mod_w, mod_b_cols)


def _mod_wgrad(c_all_t, dmod, *, name):
    depth, nb, cols = dmod.shape
    K = c_all_t.shape[0]
    tn = _tile(cols, 768)
    tk = _tile(K, 256, SUBLANES)

    def body(c_ref, d_ref, o_ref):
        cv = c_ref[...]
        sc = cv * _sigmoid(cv)
        dv = d_ref[...]
        acc = sc[:, 0:1] * dv[0:1, :]
        for b in range(1, nb):
            acc = acc + sc[:, b:b + 1] * dv[b:b + 1, :]
        o_ref[...] = acc

    return pl.pallas_call(
        body, name=name, grid=(depth, K // tk, cols // tn),
        in_specs=[pl.BlockSpec((tk, nb), lambda l, i, j: (i, 0)),
                  pl.BlockSpec((None, nb, tn), lambda l, i, j: (l, 0, j))],
        out_specs=pl.BlockSpec((None, tk, tn), lambda l, i, j: (l, i, j)),
        out_shape=jax.ShapeDtypeStruct((depth, K, cols), F32),
        compiler_params=_params("parallel", "parallel", "parallel"),
    )(c_all_t, dmod)


def _adamw(w, g, m, v, *, name):
    shape = w.shape
    cols = shape[-1]
    rows = int(np.prod(shape[:-1]))
    t = _tile(rows, 256, SUBLANES) if rows % SUBLANES == 0 else rows
    r2 = lambda a: a.reshape(rows, cols)

    def body(w_ref, g_ref, m_ref, v_ref, d_ref, nm_ref, nv_ref):
        gv = g_ref[...]
        nm = ADAM_B1 * m_ref[...] + (1.0 - ADAM_B1) * gv
        nv = ADAM_B2 * v_ref[...] + (1.0 - ADAM_B2) * (gv * gv)
        m_hat = nm / (1.0 - ADAM_B1 ** ADAM_STEP)
        v_hat = nv / (1.0 - ADAM_B2 ** ADAM_STEP)
        d_ref[...] = -ADAM_LR * (m_hat / (jnp.sqrt(v_hat) + ADAM_EPS) + ADAM_WD * w_ref[...])
        nm_ref[...] = nm
        nv_ref[...] = nv

    spec = pl.BlockSpec((t, cols), lambda i: (i, 0))
    sds = jax.ShapeDtypeStruct((rows, cols), F32)
    d, nm, nv = pl.pallas_call(
        body, name=name, grid=(rows // t,),
        in_specs=[spec] * 4, out_specs=[spec] * 3, out_shape=[sds] * 3,
        compiler_params=_params("parallel"),
    )(r2(w), r2(g), r2(m), r2(v))
    return d.reshape(shape), nm.reshape(shape), nv.reshape(shape)


def _my_place():
    return lax.axis_index("x"), lax.axis_index("y"), lax.axis_index("c")


def _other_chips(x, y):
    return [(1 - x, y), (x, 1 - y), (1 - x, 1 - y)]


def _all_gather8(v, *, name, with_sum=False):
    m, n = v.shape

    def body(x_ref, out_ref, *rest):
        if with_sum:
            sum_ref, send_sems, recv_sems, local_sem = rest
        else:
            send_sems, recv_sems, local_sem = rest
        x, y, c = _my_place()
        me, sibling = (x, y, c), (x, y, 1 - c)
        chips = _other_chips(x, y)

        def rows(px, py, pc):
            return out_ref.at[pl.ds((4 * px + 2 * py + pc) * m, m), :]

        def copy(k, block, to, src=None):
            return pltpu.make_async_remote_copy(
                src_ref=rows(*block) if src is None else src, dst_ref=rows(*block),
                send_sem=send_sems.at[k], recv_sem=recv_sems.at[k], device_id=to, device_id_type=MESH)

        mine = pltpu.make_async_copy(x_ref, rows(*me), local_sem)
        mine.start()
        first = [copy(0, me, sibling, src=x_ref)]
        first += [copy(1 + j, me, (*chip, c), src=x_ref) for j, chip in enumerate(chips)]
        for cp in first:
            cp.start()
        passed = [copy(4 + j, (*chip, c), sibling) for j, chip in enumerate(chips)]
        for j, chip in enumerate(chips):
            copy(1 + j, (*chip, c), me).wait_recv()
            passed[j].start()
        copy(0, sibling, me).wait_recv()
        for j, chip in enumerate(chips):
            copy(4 + j, (*chip, 1 - c), me).wait_recv()
        for cp in first + passed:
            cp.wait_send()
        mine.wait()
        if with_sum:
            acc = out_ref[pl.ds(0, m), :]
            for d in range(1, N_DEV):
                acc = acc + out_ref[pl.ds(d * m, m), :]
            sum_ref[...] = acc

    vm = pl.BlockSpec(memory_space=pltpu.VMEM)
    out_shape = [jax.ShapeDtypeStruct((N_DEV * m, n), F32)]
    if with_sum:
        out_shape.append(jax.ShapeDtypeStruct((m, n), F32))
    res = pl.pallas_call(
        body, name=name, out_shape=out_shape, in_specs=[vm], out_specs=[vm] * len(out_shape),
        scratch_shapes=[pltpu.SemaphoreType.DMA((7,)), pltpu.SemaphoreType.DMA((7,)), pltpu.SemaphoreType.DMA],
        compiler_params=pltpu.CompilerParams(vmem_limit_bytes=VMEM_LIMIT),
    )(v)
    return res if with_sum else res[0]


class _Cut(NamedTuple):
    shape: tuple
    slab: int
    half: int


IN_WIDTH = SSM_WIDTH + 3 * ATTN_WIDTH + ATTN_HEADS + 2 * D_MODEL
CUTS = dict(
    ffn_w_in=_Cut((1, D_MODEL, 2 * D_FF), 2, 1),
    ffn_w_out=_Cut((1, D_FF, D_MODEL), 1, 2),
    mix_w_in=_Cut((N_CHIPS, D_MODEL, IN_WIDTH // N_CHIPS), 0, 1),
    glu_w=_Cut((1, SSM_WIDTH, 2 * D_MODEL), 2, 1),
    attn_w_out=_Cut((1, ATTN_WIDTH, D_MODEL), 2, 1),
    mix_w_out=_Cut((1, D_MODEL, D_MODEL), 1, 2),
)
LAYER_MATS = (("ffn_w_in", 0), ("ffn_w_in", 1), ("ffn_w_out", 0), ("ffn_w_out", 1), ("mix_w_in", None),
              ("glu_w", None), ("attn_w_out", None), ("mix_w_out", None))
FIRST_MATS = (0, 2)
MIXER_MATS = (4, 5, 6, 7)
LATE_MATS = (1, 3)


def _part_shape(cut, slab=False, half=False):
    s = list(cut.shape)
    if slab:
        s[cut.slab] //= N_CHIPS
    if half:
        s[cut.half] //= 2
    return tuple(s)


def _window(ref, cut, slab=None, half=None):
    idx = [slice(None)] * len(cut.shape)
    for axis, parts, which in ((cut.slab, N_CHIPS, slab), (cut.half, 2, half)):
        if which is not None:
            width = cut.shape[axis] // parts
            idx[axis] = pl.ds(pl.multiple_of(which * width, width), width)
    return ref.at[tuple(idx)]


HBM_SPEC = pl.BlockSpec(memory_space=pltpu.HBM)


class _Exchange(NamedTuple):
    operands: list
    out_shapes: list
    sems: list
    start: object
    finish: object


def _run_exchange(ex, *, name):
    ni, no = len(ex.operands), len(ex.out_shapes)

    def body(*refs):
        parts = (refs[:ni], refs[ni:ni + no], refs[ni + no:])
        ex.start(*parts)
        ex.finish(*parts)

    return pl.pallas_call(
        body, name=name, out_shape=ex.out_shapes, in_specs=[HBM_SPEC] * ni, out_specs=[HBM_SPEC] * no,
        scratch_shapes=ex.sems,
    )(*ex.operands)


def _gather_exchange(shards, cuts):
    n = len(shards)

    def setup(s_refs, f_refs, sems):
        send_sems, recv_sems = sems
        x, y, c = _my_place()
        me, sibling, mine = (x, y, c), (x, y, 1 - c), 2 * x + y
        chips = _other_chips(x, y)

        def copy(i, k, src, dst, to):
            return pltpu.make_async_remote_copy(
                src_ref=src, dst_ref=dst, send_sem=send_sems.at[7 * i + k], recv_sem=recv_sems.at[7 * i + k],
                device_id=to, device_id_type=MESH)

        def landed(i, j, half):
            return _window(f_refs[i], cuts[i], slab=2 * chips[j][0] + chips[j][1], half=half)

        def sends():
            own = [copy(i, 6, s_refs[i], _window(f_refs[i], cuts[i], slab=mine), sibling) for i in range(n)]
            return own + [copy(i, j, _window(s_refs[i], cuts[i], half=c),
                               _window(f_refs[i], cuts[i], slab=mine, half=c), (*chips[j], c))
                          for i in range(n) for j in range(3)]

        return c, me, sibling, copy, landed, sends

    def start(s_refs, f_refs, sems):
        for cp in setup(s_refs, f_refs, sems)[-1]():
            cp.start()

    def finish(s_refs, f_refs, sems):
        c, me, sibling, copy, landed, sends = setup(s_refs, f_refs, sems)
        passed = []
        for i in range(n):
            for j in range(3):
                copy(i, j, landed(i, j, c), landed(i, j, c), me).wait_recv()
                passed.append(copy(i, 3 + j, landed(i, j, c), landed(i, j, c), sibling))
                passed[-1].start()
        for i in range(n):
            for j in range(3):
                copy(i, 3 + j, landed(i, j, 1 - c), landed(i, j, 1 - c), me).wait_recv()
        for i in range(n):
            mine_i = _window(f_refs[i], cuts[i], slab=2 * me[0] + me[1])
            copy(i, 6, mine_i, mine_i, me).wait_recv()
        for cp in sends() + passed:
            cp.wait_send()

    return _Exchange(
        list(shards), [jax.ShapeDtypeStruct(cut.shape, s.dtype) for s, cut in zip(shards, cuts)],
        [pltpu.SemaphoreType.DMA((7 * n,)), pltpu.SemaphoreType.DMA((7 * n,))], start, finish)


def _swap_exchange(mats, cuts):
    n = len(mats)

    def copies(m_refs, r_refs, sems):
        send_sems, recv_sems = sems
        x, y, c = _my_place()
        return [pltpu.make_async_remote_copy(
            src_ref=_window(m_refs[i], cuts[i], half=1 - c), dst_ref=r_refs[i], send_sem=send_sems.at[i],
            recv_sem=recv_sems.at[i], device_id=(x, y, 1 - c), device_id_type=MESH) for i in range(n)]

    def start(m_refs, r_refs, sems):
        for cp in copies(m_refs, r_refs, sems):
            cp.start()

    def finish(m_refs, r_refs, sems):
        for cp in copies(m_refs, r_refs, sems):
            cp.wait()

    return _Exchange(
        list(mats), [jax.ShapeDtypeStruct(_part_shape(cut, half=True), m.dtype) for m, cut in zip(mats, cuts)],
        [pltpu.SemaphoreType.DMA((n,)), pltpu.SemaphoreType.DMA((n,))], start, finish)


def _partials_exchange(sums, cuts):
    n = len(sums)

    def copies(s_refs, p_refs, sems):
        send_sems, recv_sems = sems
        x, y, c = _my_place()
        return [pltpu.make_async_remote_copy(
            src_ref=_window(s_refs[i], cuts[i], slab=2 * chip[0] + chip[1]), dst_ref=p_refs[i].at[j],
            send_sem=send_sems.at[3 * i + j], recv_sem=recv_sems.at[3 * i + j],
            device_id=(*chip, c), device_id_type=MESH)
            for i in range(n) for j, chip in enumerate(_other_chips(x, y))]

    def start(s_refs, p_refs, sems):
        for cp in copies(s_refs, p_refs, sems):
            cp.start()

    def finish(s_refs, p_refs, sems):
        for cp in copies(s_refs, p_refs, sems):
            cp.wait()

    return _Exchange(
        list(sums), [jax.ShapeDtypeStruct((3,) + _part_shape(cut, slab=True, half=True), s.dtype)
                     for s, cut in zip(sums, cuts)],
        [pltpu.SemaphoreType.DMA((3 * n,)), pltpu.SemaphoreType.DMA((3 * n,))], start, finish)


def _share_all(dests, cuts, places, *, name):
    names = list(dests)
    nn, n = len(names), len(places)

    def body(*refs):
        o_refs = dict(zip(names, refs[nn:2 * nn]))
        send_sems, recv_sems = refs[2 * nn:]
        x, y, c = _my_place()

        def win(i, half):
            slab_cut = _Cut(_part_shape(cuts[i], slab=True), cuts[i].slab, cuts[i].half)
            return _window(o_refs[places[i][0]].at[places[i][1]], slab_cut, half=half)

        def copy(i, half):
            return pltpu.make_async_remote_copy(
                src_ref=win(i, half), dst_ref=win(i, half), send_sem=send_sems.at[i], recv_sem=recv_sems.at[i],
                device_id=(x, y, 1 - c), device_id_type=MESH)

        for i in range(n):
            copy(i, c).start()
        for i in range(n):
            copy(i, c).wait_send()
            copy(i, 1 - c).wait_recv()

    return pl.pallas_call(
        body, name=name, out_shape=[jax.ShapeDtypeStruct(dests[k].shape, F32) for k in names],
        in_specs=[HBM_SPEC] * nn, out_specs=[HBM_SPEC] * nn,
        input_output_aliases={i: i for i in range(nn)},
        scratch_shapes=[pltpu.SemaphoreType.DMA((n,)), pltpu.SemaphoreType.DMA((n,))],
    )(*[dests[k] for k in names])


def _cut_blocks(shape):
    _, R, C = shape
    tr = _tile(R, 512, 16)
    tc = _tile(C, 2048) if C % LANES == 0 else C
    return (None, tr, tc), (shape[0], R // tr, C // tc)


def _offset_map(axis, blocks):
    def index_map(b, i, j, which):
        idx = [b, i, j]
        idx[axis] = which[0] * blocks[axis] + idx[axis]
        return tuple(idx)
    return index_map


def _add_half(mat, other, cut, c_idx, *, name):
    shape = _part_shape(cut, half=True)
    block, grid = _cut_blocks(shape)

    def body(c_ref, m_ref, o_ref, f_ref, b_ref):
        s = m_ref[...] + o_ref[...]
        f_ref[...] = s
        b_ref[...] = s.astype(BF16)

    plain = pl.BlockSpec(block, lambda b, i, j, which: (b, i, j))
    grid_spec = pltpu.PrefetchScalarGridSpec(
        num_scalar_prefetch=1, grid=grid,
        in_specs=[pl.BlockSpec(block, _offset_map(cut.half, grid)), plain], out_specs=[plain, plain])
    return pl.pallas_call(
        body, name=name, grid_spec=grid_spec,
        out_shape=[jax.ShapeDtypeStruct(shape, F32), jax.ShapeDtypeStruct(shape, BF16)],
        compiler_params=_params("parallel", "parallel", "parallel"),
    )(c_idx, mat, other)


def _sum_slab(own, parts, cut, dest, place, chip_idx, c_idx, *, name):
    shape = _part_shape(cut, slab=True, half=True)
    block, grid = _cut_blocks(shape)
    assert shape[0] == 1

    def body(k_ref, c_ref, o_ref, p_ref, dest_ref, out_ref):
        acc = o_ref[...]
        for j in range(3):
            acc = acc + p_ref[j].astype(F32)
        out_ref[...] = acc

    def own_map(b, i, j, chip, core):
        idx = [b, i, j]
        idx[cut.slab] = chip[0] * grid[cut.slab] + idx[cut.slab]
        return tuple(idx)

    def dest_map(b, i, j, chip, core):
        idx = [b, i, j]
        idx[cut.half] = core[0] * grid[cut.half] + idx[cut.half]
        return tuple(place) + tuple(idx)

    grid_spec = pltpu.PrefetchScalarGridSpec(
        num_scalar_prefetch=2, grid=grid,
        in_specs=[pl.BlockSpec(block, own_map),
                  pl.BlockSpec((3,) + block[1:], lambda b, i, j, chip, core: (0, i, j)),
                  pl.BlockSpec(memory_space=pl.ANY)],
        out_specs=pl.BlockSpec((None,) * len(place) + block, dest_map))
    return pl.pallas_call(
        body, name=name, grid_spec=grid_spec, out_shape=jax.ShapeDtypeStruct(dest.shape, F32),
        input_output_aliases={4: 0},
        compiler_params=_params("parallel", "parallel", "parallel"),
    )(chip_idx, c_idx, own, parts.reshape((3,) + shape[1:]), dest)


def _pad_rows(flat, cols=8 * LANES, align=SUBLANES):
    n = flat.shape[0]
    rows = -(-n // (cols * align)) * align
    return jnp.pad(flat, (0, rows * cols - n)).reshape(rows, cols)


def _row(v):
    return v.reshape(1, -1)


def _ffn_fwd(x, mod, g_pre, g_post, w_in, w_out, tag, exchange=None):
    sh, sc, gate = _row(mod[0]), _row(mod[1]), _row(mod[2])
    h = _prenorm(x, _row(g_pre), sc, sh, name=f"prenorm_{tag}")
    gt, up, act, *exchanged = _mm_swiglu(h, w_in, name=f"swiglu_{tag}", exchange=exchange)
    y, x_out = _mm_postnorm(act, w_out, x, _row(g_post), gate, FFN_RES, name=f"ffn_out_{tag}")
    return x_out, (x, h, gt, up, act, y), exchanged


def _ffn_bwd(dxo, saved, mod, g_pre, g_post, w_in, w_out, tag, swap=None, after_swap=None):
    x, h, gt, up, act, y = saved
    sc, gate = _row(mod[1]), _row(mod[2])
    dy, dgate, dgpost, *swapped = _postnorm_bwd(dxo, y, _row(g_post), gate, FFN_RES, name=f"postnorm_bwd_{tag}",
                                                exchange=swap)
    exchange = after_swap(swapped) if after_swap is not None else None
    dgt, dup, *exchanged = _mm_swiglu_bwd(dy, w_out, gt, up, name=f"swiglu_bwd_{tag}", exchange=exchange)
    dw_out = _mm(act, dy, ta=True, name=f"dw_out_{tag}", tm=1408, tn=1024, tk=2048)
    dh = _mm(dgt, w_in, tb=True, name=f"dh_gate_{tag}", tm=1024, tk=D_FF)
    dh = _mm(dup, w_in, tb=True, b_k0=D_FF, bias=dh, name=f"dh_up_{tag}", tm=1024, tk=D_FF)
    dw_in = _mm(h, dgt, ta=True, out_n=2 * D_FF, name=f"dw_gate_{tag}", tm=1024, tn=1408, tk=2048)
    dw_in = _mm(h, dup, ta=True, out_n=2 * D_FF, out_j0=D_FF, into=dw_in, name=f"dw_up_{tag}",
                tm=1024, tn=1408, tk=2048)
    dx, dsh, dsc, dgpre = _prenorm_bwd(x, dh, dxo, _row(g_pre), sc, name=f"prenorm_bwd_{tag}")
    dmod = jnp.stack([dsh[0], dsc[0], dgate[0]])
    return dx, dmod, dgpre[0], dgpost[0], dw_in, dw_out, exchanged


def _split_mix_w_in(w):
    u0, q0, f0, g0 = 0, SSM_WIDTH, SSM_WIDTH + 3 * ATTN_WIDTH, SSM_WIDTH + 3 * ATTN_WIDTH + ATTN_HEADS
    w_f = jnp.pad(w[:, f0:g0], ((0, 0), (0, LANES - ATTN_HEADS)))
    return w[:, u0:q0], w[:, q0:f0], w_f, w[:, g0:]


def _mixer_fwd(x, mod, g_pre, g_post, w, ssm, forget_b, tag, exchange=None):
    L = x.shape[0]
    sh, sc, gate = _row(mod[0]), _row(mod[1]), _row(mod[2])
    lam_re, lam_im, bin_re, bin_im, cout_re, cout_im, dskip = ssm
    h = _prenorm(x, _row(g_pre), sc, sh, name=f"prenorm_{tag}")
    u = _mm(h, w["u"], name=f"proj_u_{tag}")
    qkv = _mm(h, w["qkv"], out_dtype=BF16, name=f"proj_qkv_{tag}")
    f = _mm(h, w["f"], name=f"proj_f_{tag}")
    gab = _mm(h, w["gab"], name=f"proj_gab_{tag}")
    u_i = _interleave(u)
    bu_re, bu_im = _mm_bd([u_i], [bin_re, bin_im], name=f"ssm_bu_{tag}")
    s_re, s_im = _ssm_scan_fwd(bu_re, bu_im, lam_re, lam_im, name=f"ssm_scan_{tag}")
    y_ssm = _deinterleave(_mm_bd([s_re, s_im], [cout_re, cout_im], bias=u_i, bscale=dskip, name=f"ssm_y_{tag}"))
    gl = _gelu_fwd(y_ssm, name=f"gelu_{tag}")
    z = _mm(gl, w["glu"], name=f"glu_{tag}")
    fb = jnp.pad(forget_b, (0, LANES - ATTN_HEADS)).reshape(1, LANES)
    cum = _fox_cum(f, fb, name=f"fox_cum_{tag}")
    cum8 = cum[:, :ATTN_HEADS]
    cum_cols = jnp.repeat(cum8, HEAD_DIM, axis=1)
    cum_rows = jnp.pad(cum8.T.reshape(ATTN_HEADS // 2, 2, L), ((0, 0), (0, SUBLANES - 2), (0, 0)))
    attn, attn32, lse, *exchanged = _fox_fwd(qkv, cum_cols, cum_rows, name=f"fox_fwd_{tag}", exchange=exchange)
    yb = _mm(attn, w["attn_out"], name=f"attn_out_{tag}")
    merged = _merge_fwd(z, yb, gab, name=f"merge_{tag}")
    y, x_out = _mm_postnorm(merged, w["out"], x, _row(g_post), gate, 1.0, name=f"mix_out_{tag}")
    saved = (x, h, u, u_i, qkv, f, gab, s_re, s_im, y_ssm, gl, z, fb, cum_cols, cum_rows, attn, attn32, lse, yb,
             merged, y)
    return x_out, saved, exchanged


def _mixer_bwd(dxo, saved, mod, g_pre, g_post, w, ssm, tag, swap=None, after_swap=None):
    (x, h, u, u_i, qkv, f, gab, s_re, s_im, y_ssm, gl, z, fb, cum_cols, cum_rows, attn, attn32, lse, yb,
     merged, y) = saved
    L = x.shape[0]
    sc, gate = _row(mod[1]), _row(mod[2])
    lam_re, lam_im, bin_re, bin_im, cout_re, cout_im, dskip = ssm
    dy, dgate, dgpost = _postnorm_bwd(dxo, y, _row(g_post), gate, 1.0, name=f"postnorm_bwd_{tag}")
    dmerged = _mm(dy, w["out"], tb=True, name=f"dmerged_{tag}")
    dw_out = _mm(merged, dy, ta=True, name=f"dw_mix_out_{tag}", tm=1024, tn=1024)
    dz, dyb, dgab, *swapped = _merge_bwd(dmerged, z, yb, gab, name=f"merge_bwd_{tag}", exchange=swap)
    exchange = after_swap(swapped) if after_swap is not None else None
    dgl = _mm(dz, w["glu"], tb=True, name=f"dgl_{tag}", tk=2048)
    dw_glu = _mm(gl, dz, ta=True, name=f"dw_glu_{tag}", tm=512, tn=2048)
    dys, dsk, dd = _gelu_bwd(dgl, y_ssm, u, dskip, name=f"gelu_bwd_{tag}")
    dys, dsk = _interleave(dys), _interleave(dsk)
    d_re, d_im = _mm_bd([dys], [cout_re, cout_im], tb=True, name=f"ssm_ds_{tag}")
    dcout_re = _mm_bd_t(s_re, dys, SSM_BLOCKS, name=f"ssm_dc_re_{tag}")
    dcout_im = _mm_bd_t(s_im, dys, SSM_BLOCKS, name=f"ssm_dc_im_{tag}")
    g_re, g_im, dlam_re, dlam_im = _ssm_scan_bwd(d_re, d_im, s_re, s_im, lam_re, lam_im, name=f"ssm_scan_bwd_{tag}")
    du = _mm_bd([g_re, g_im], [bin_re, bin_im], tb=True, bias=dsk, out_dtype=BF16, name=f"ssm_du_{tag}")
    du = _deinterleave(du)
    dbin_re = _mm_bd_t(u_i, g_re, SSM_BLOCKS, name=f"ssm_db_re_{tag}")
    dbin_im = _mm_bd_t(u_i, g_im, SSM_BLOCKS, name=f"ssm_db_im_{tag}")
    dssm = (dlam_re, dlam_im, dbin_re, dbin_im, dcout_re, dcout_im, _row(dd[0]))
    dattn = _mm(dyb, w["attn_out"], tb=True, out_dtype=BF16, name=f"dattn_{tag}")
    dw_attn = _mm(attn, dyb, ta=True, name=f"dw_attn_out_{tag}", tm=512, tn=1024)
    stats = _fox_rowstats(dattn, attn32, lse, cum_cols, name=f"fox_rowstats_{tag}")
    dk, dv, dcum_rows, dq, drow, *exchanged = _fox_bwd(qkv, dattn, stats, cum_rows, name=f"fox_bwd_{tag}",
                                                       exchange=exchange)
    drow8 = drow.reshape(L, ATTN_HEADS // 2, LANES)[:, :, :2].reshape(L, ATTN_HEADS)
    dcum = drow8 + dcum_rows[:, :2, :].reshape(ATTN_HEADS, L).T
    dcum = jnp.pad(dcum, ((0, 0), (0, LANES - ATTN_HEADS)))
    df, dfb = _fox_cum_bwd(dcum, f, fb, name=f"fox_cum_bwd_{tag}")
    dqkv = jnp.concatenate([dq.astype(BF16), dk, dv], axis=1)
    dh = _mm(dqkv, w["qkv"], tb=True, name=f"dh_qkv_{tag}", tk=1536)
    dh = _mm(du, w["u"], tb=True, bias=dh, name=f"dh_u_{tag}")
    dh = _mm(dgab, w["gab"], tb=True, bias=dh, name=f"dh_gab_{tag}", tk=2048)
    dh = _mm(df, w["f"], tb=True, bias=dh, name=f"dh_f_{tag}")
    dw_u = _mm(h, du, ta=True, name=f"dw_u_{tag}", tm=1024, tn=512)
    dw_qkv = _mm(h, dqkv, ta=True, name=f"dw_qkv_{tag}", tm=1024, tn=1536)
    dw_f = _mm(h, df, ta=True, name=f"dw_f_{tag}", tm=1024)
    dw_gab = _mm(h, dgab, ta=True, name=f"dw_gab_{tag}", tm=1024, tn=1024)
    dw_in = jnp.concatenate([dw_u, dw_qkv, dw_f[:, :ATTN_HEADS], dw_gab], axis=1)
    dx, dsh, dsc, dgpre = _prenorm_bwd(x, dh, dxo, _row(g_pre), sc, name=f"prenorm_bwd_{tag}")
    dmod = jnp.stack([dsh[0], dsc[0], dgate[0]])
    grads = dict(mix_w_in=dw_in, glu_w=dw_glu, attn_w_out=dw_attn, mix_w_out=dw_out,
                 forget_b=dfb[0, :ATTN_HEADS])
    return dx, dmod, dgpre[0], dgpost[0], grads, dssm, exchanged


SSM_NAMES = ("ssm_a_re", "ssm_a_im", "ssm_log_dt", "ssm_b_re", "ssm_b_im", "ssm_c_re", "ssm_c_im", "ssm_d")
SMALL_NAMES = ("forget_b",) + SSM_NAMES
WEIGHT_NAMES = ("mod_w", "mod_b", "norm_pre", "norm_post", "ffn_w_in", "ffn_w_out", "mix_w_in", "forget_b") \
    + SSM_NAMES + ("glu_w", "attn_w_out", "mix_w_out")


def _layer_shards(w, l):
    return [(w[n][l] if j is None else w[n][l, j]).astype(BF16).reshape(_part_shape(CUTS[n], slab=True))
            for n, j in LAYER_MATS]


def _train_step(x, c, target, w, m, v):
    xi, yi, ci = _my_place()
    chip = 2 * xi + yi
    dev = 4 * xi + 2 * yi + ci
    mod_cols = N_SUB * 3 * D_MODEL // N_CHIPS
    norm_cols = D_MODEL // N_CHIPS

    cuts = [CUTS[n] for n, _ in LAYER_MATS]
    shards = [_layer_shards(w, l) for l in range(DEPTH)]
    full = [[None] * len(LAYER_MATS) for _ in range(DEPTH)]

    def gather(keys):
        return _gather_exchange([shards[l][i] for l, i in keys], [cuts[i] for _, i in keys])

    def store(keys, mats):
        for (l, i), a in zip(keys, mats):
            full[l][i] = a

    first = lambda l: [(l, i) for i in FIRST_MATS]
    mixer = lambda l: [(l, i) for i in MIXER_MATS]
    late = lambda l: [(l, i) for i in LATE_MATS]
    store(first(0), _run_exchange(gather(first(0)), name="gather_weights_first"))
    c_all = _all_gather8(jnp.pad(c, ((0, SUBLANES - 1), (0, 0))), name="gather_c")[::SUBLANES]
    mod_b_cols = lax.dynamic_slice_in_dim(w["mod_b"], chip * mod_cols, mod_cols, axis=1)[:, None, :]
    mod_part = _mod_partial(c_all, w["mod_w"], mod_b_cols, name="mod_partial")
    small_fwd = jnp.concatenate([mod_part.reshape(-1), w["norm_pre"].reshape(-1), w["norm_post"].reshape(-1)])
    n_mod, n_norm = mod_part.size, w["norm_pre"].size
    sf_all = _all_gather8(_pad_rows(small_fwd), name="gather_mod").reshape(N_DEV, -1)
    sf_chips = sf_all[::2]
    mod_all = jnp.concatenate(
        [sf_chips[k, :n_mod].reshape(DEPTH, N_DEV, mod_cols) for k in range(N_CHIPS)], axis=2)
    mod_mine = lax.dynamic_index_in_dim(mod_all, dev, axis=1, keepdims=False).reshape(DEPTH, N_SUB, 3, D_MODEL)
    norm_pre = jnp.concatenate(
        [sf_chips[k, n_mod:n_mod + n_norm].reshape(DEPTH, N_SUB, norm_cols) for k in range(N_CHIPS)], axis=2)
    norm_post = jnp.concatenate(
        [sf_chips[k, n_mod + n_norm:n_mod + 2 * n_norm].reshape(DEPTH, N_SUB, norm_cols) for k in range(N_CHIPS)],
        axis=2)

    saved, layer_w, ssm_prep, ssm_vjp = [], [], [], []
    h = x
    for l in range(DEPTH):
        lw = dict(ffn=[(full[l][0][0], full[l][2][0]), None])
        prep, vjp = jax.vjp(_ssm_discretize, *[w[n][l] for n in SSM_NAMES])
        layer_w.append(lw)
        ssm_prep.append(prep)
        ssm_vjp.append(vjp)
        h, s0, arrived = _ffn_fwd(h, mod_mine[l, 0], norm_pre[l, 0], norm_post[l, 0], *lw["ffn"][0], tag=f"l{l}a",
                                  exchange=gather(mixer(l)))
        store(mixer(l), arrived)
        mix_in, glu, attn_out, mix_out = full[l][4:]
        w_u, w_qkv, w_f, w_gab = _split_mix_w_in(mix_in.transpose(1, 0, 2).reshape(D_MODEL, IN_WIDTH))
        lw["mix"] = dict(u=w_u, qkv=w_qkv, f=w_f, gab=w_gab, glu=glu[0], attn_out=attn_out[0], out=mix_out[0])
        coming = late(l) + (first(l + 1) if l + 1 < DEPTH else [])
        h, s1, arrived = _mixer_fwd(h, mod_mine[l, 1], norm_pre[l, 1], norm_post[l, 1], lw["mix"], prep,
                                    w["forget_b"][l], tag=f"l{l}m", exchange=gather(coming))
        store(coming, arrived)
        lw["ffn"][1] = (full[l][1][0], full[l][3][0])
        h, s2, _ = _ffn_fwd(h, mod_mine[l, 2], norm_pre[l, 2], norm_post[l, 2], *lw["ffn"][1], tag=f"l{l}b")
        saved.append((s0, s1, s2))
    dh, loss8 = _loss_head(h, target, name="loss_head")

    c_idx = ci.reshape(1).astype(jnp.int32)
    chip_idx = chip.reshape(1).astype(jnp.int32)
    places = []
    dests = {n: lax.empty(((DEPTH,) if j is None else (DEPTH, 2)) + _part_shape(CUTS[n], slab=True), F32)
             for n, j in LAYER_MATS}
    g_small = {n: [None] * DEPTH for n in SMALL_NAMES}
    dmod, dnpre, dnpost = [], [], []
    pending = []

    def swap_of(l, idx, mats):
        cs = [cuts[i] for i in idx]
        mats = [a.reshape(cut.shape) for a, cut in zip(mats, cs)]

        def added(from_sibling):
            for k, i in enumerate(idx):
                f, b = _add_half(mats[k], from_sibling[k], cs[k], c_idx, name=f"grad_add_l{l}_{i}")
                pending.append((l, i, f, b))

        return _swap_exchange(mats, cs), added

    def partials():
        return _partials_exchange([p[3] for p in pending], [cuts[p[1]] for p in pending])

    def then_partials(added):
        def after_swap(from_sibling):
            added(from_sibling)
            return partials()
        return after_swap

    def end_reduce(parts):
        for (l, i, f, _), part in zip(pending, parts):
            n, j = LAYER_MATS[i]
            place = (l,) if j is None else (l, j)
            dests[n] = _sum_slab(f, part, cuts[i], dests[n], place, chip_idx, c_idx, name=f"grad_sum_l{l}_{i}")
            places.append((n, place))
        pending.clear()

    swap, added = None, None
    for l in reversed(range(DEPTH)):
        lw = layer_w[l]
        dh, dm2, dp2, dq2, dwin2, dwout2, _ = _ffn_bwd(
            dh, saved[l][2], mod_mine[l, 2], norm_pre[l, 2], norm_post[l, 2], *lw["ffn"][1], tag=f"l{l}b",
            swap=swap, after_swap=added)
        swap, added = swap_of(l, LATE_MATS, [dwin2, dwout2])
        dh, dm1, dp1, dq1, gmix, dssm, parts = _mixer_bwd(
            dh, saved[l][1], mod_mine[l, 1], norm_pre[l, 1], norm_post[l, 1], lw["mix"], ssm_prep[l],
            tag=f"l{l}m", swap=swap, after_swap=then_partials(added))
        end_reduce(parts)
        dmix_in = gmix["mix_w_in"].reshape(D_MODEL, N_CHIPS, IN_WIDTH // N_CHIPS).transpose(1, 0, 2)
        swap, added = swap_of(l, MIXER_MATS, [dmix_in, gmix["glu_w"], gmix["attn_w_out"], gmix["mix_w_out"]])
        dh, dm0, dp0, dq0, dwin0, dwout0, parts = _ffn_bwd(
            dh, saved[l][0], mod_mine[l, 0], norm_pre[l, 0], norm_post[l, 0], *lw["ffn"][0], tag=f"l{l}a",
            swap=swap, after_swap=then_partials(added))
        end_reduce(parts)
        swap, added = swap_of(l, FIRST_MATS, [dwin0, dwout0])
        dmod.insert(0, jnp.stack([dm0, dm1, dm2]))
        dnpre.insert(0, jnp.stack([dp0, dp1, dp2]))
        dnpost.insert(0, jnp.stack([dq0, dq1, dq2]))
        g_small["forget_b"][l] = gmix["forget_b"]
        for n, g in zip(SSM_NAMES, ssm_vjp[l](dssm)):
            g_small[n][l] = g
    added(_run_exchange(swap, name="grad_swap_last"))
    end_reduce(_run_exchange(partials(), name="grad_partials_last"))
    grad_x = dh
    g_small = {n: jnp.stack(g) for n, g in g_small.items()}

    small = [loss8[0, :1], jnp.stack(dmod).reshape(-1), jnp.stack(dnpre).reshape(-1), jnp.stack(dnpost).reshape(-1)]
    small += [g_small[n].reshape(-1) for n in SMALL_NAMES]
    sizes = [int(s.size) for s in small]
    offs = np.concatenate([[0], np.cumsum(sizes)])
    sb_all, sb_sum = _all_gather8(_pad_rows(jnp.concatenate(small)), name="gather_small_grads", with_sum=True)
    sb_sum = sb_sum.reshape(-1)
    take = lambda i: sb_sum[int(offs[i]):int(offs[i + 1])]
    loss = take(0)[0]
    grads = {"mod_b": take(1).reshape(DEPTH, N_SUB * 3 * D_MODEL)}
    dnorm_pre_full = take(2).reshape(DEPTH, N_SUB, D_MODEL)
    dnorm_post_full = take(3).reshape(DEPTH, N_SUB, D_MODEL)
    grads["norm_pre"] = lax.dynamic_slice_in_dim(dnorm_pre_full, chip * norm_cols, norm_cols, axis=2)
    grads["norm_post"] = lax.dynamic_slice_in_dim(dnorm_post_full, chip * norm_cols, norm_cols, axis=2)
    for i, n in enumerate(SMALL_NAMES):
        grads[n] = take(4 + i).reshape(w[n].shape)
    dmod_all = sb_all.reshape(N_DEV, -1)[:, int(offs[1]):int(offs[2])].reshape(N_DEV, DEPTH, N_SUB * 3 * D_MODEL)
    dmod_cols = lax.dynamic_slice_in_dim(dmod_all, chip * mod_cols, mod_cols, axis=2).transpose(1, 0, 2)
    grads["mod_w"] = _mod_wgrad(c_all.T, dmod_cols, name="mod_wgrad")

    shared = _share_all(dests, [CUTS[n] for n, _ in places], places, name="grad_share")
    for n, g in zip(dests, shared):
        grads[n] = g.reshape(w[n].shape)

    delta, new_m, new_v = {}, {}, {}
    for n in WEIGHT_NAMES:
        delta[n], new_m[n], new_v[n] = _adamw(w[n], grads[n], m[n], v[n], name=f"adamw_{n}")
    outs = [loss, grad_x[None]]
    for group in (grads, delta, new_m, new_v):
        outs += [group[n] for n in WEIGHT_NAMES]
    return tuple(outs)


def kernel(x, c, mod_w, mod_b, norm_pre, norm_post, ffn_w_in, ffn_w_out, mix_w_in, forget_b, ssm_a_re, ssm_a_im, ssm_log_dt, ssm_b_re, ssm_b_im, ssm_c_re, ssm_c_im, ssm_d, glu_w, attn_w_out, mix_w_out, loss_target, m_mod_w, m_mod_b, m_norm_pre, m_norm_post, m_ffn_w_in, m_ffn_w_out, m_mix_w_in, m_forget_b, m_ssm_a_re, m_ssm_a_im, m_ssm_log_dt, m_ssm_b_re, m_ssm_b_im, m_ssm_c_re, m_ssm_c_im, m_ssm_d, m_glu_w, m_attn_w_out, m_mix_w_out, v_mod_w, v_mod_b, v_norm_pre, v_norm_post, v_ffn_w_in, v_ffn_w_out, v_mix_w_in, v_forget_b, v_ssm_a_re, v_ssm_a_im, v_ssm_log_dt, v_ssm_b_re, v_ssm_b_im, v_ssm_c_re, v_ssm_c_im, v_ssm_d, v_glu_w, v_attn_w_out, v_mix_w_out):
    w = dict(mod_w=mod_w, mod_b=mod_b, norm_pre=norm_pre, norm_post=norm_post, ffn_w_in=ffn_w_in,
             ffn_w_out=ffn_w_out, mix_w_in=mix_w_in, forget_b=forget_b, ssm_a_re=ssm_a_re, ssm_a_im=ssm_a_im,
             ssm_log_dt=ssm_log_dt, ssm_b_re=ssm_b_re, ssm_b_im=ssm_b_im, ssm_c_re=ssm_c_re, ssm_c_im=ssm_c_im,
             ssm_d=ssm_d, glu_w=glu_w, attn_w_out=attn_w_out, mix_w_out=mix_w_out)
    m = dict(mod_w=m_mod_w, mod_b=m_mod_b, norm_pre=m_norm_pre, norm_post=m_norm_post, ffn_w_in=m_ffn_w_in,
             ffn_w_out=m_ffn_w_out, mix_w_in=m_mix_w_in, forget_b=m_forget_b, ssm_a_re=m_ssm_a_re,
             ssm_a_im=m_ssm_a_im, ssm_log_dt=m_ssm_log_dt, ssm_b_re=m_ssm_b_re, ssm_b_im=m_ssm_b_im,
             ssm_c_re=m_ssm_c_re, ssm_c_im=m_ssm_c_im, ssm_d=m_ssm_d, glu_w=m_glu_w, attn_w_out=m_attn_w_out,
             mix_w_out=m_mix_w_out)
    v = dict(mod_w=v_mod_w, mod_b=v_mod_b, norm_pre=v_norm_pre, norm_post=v_norm_post, ffn_w_in=v_ffn_w_in,
             ffn_w_out=v_ffn_w_out, mix_w_in=v_mix_w_in, forget_b=v_forget_b, ssm_a_re=v_ssm_a_re,
             ssm_a_im=v_ssm_a_im, ssm_log_dt=v_ssm_log_dt, ssm_b_re=v_ssm_b_re, ssm_b_im=v_ssm_b_im,
             ssm_c_re=v_ssm_c_re, ssm_c_im=v_ssm_c_im, ssm_d=v_ssm_d, glu_w=v_glu_w, attn_w_out=v_attn_w_out,
             mix_w_out=v_mix_w_out)
    return _train_step(x[0], c, loss_target[0], w, m, v)
```

```python
import functools
import math
from typing import NamedTuple

import jax
import jax.numpy as jnp
import numpy as np
from jax import lax
from jax.experimental import pallas as pl
from jax.experimental.pallas import tpu as pltpu

F32 = jnp.float32
BF16 = jnp.bfloat16

D_MODEL = 1024
DEPTH = 2
SSM_WIDTH = 512
SSM_GROUP = 16
SSM_GROUPS = 32
SSM_STATE = 64
SSM_FLAT = SSM_GROUPS * SSM_STATE
SSM_BLOCKS = 4
ATTN_HEADS = 8
HEAD_DIM = 64
ATTN_WIDTH = 512
D_FF = 2816
FFN_RES = 0.5
N_SUB = 3
RMS_EPS = 1e-6
N_CHIPS = 4
N_DEV = 8

ADAM_LR = 0.001
ADAM_B1 = 0.9
ADAM_B2 = 0.999
ADAM_EPS = 1e-08
ADAM_WD = 0.01
ADAM_STEP = 10

LANES = 128
SUBLANES = 8
VMEM_LIMIT = 52 * 1024 * 1024
MESH = pl.DeviceIdType.MESH

NN = (((1,), (0,)), ((), ()))
NT = (((1,), (1,)), ((), ()))
TN = (((0,), (0,)), ((), ()))


def _tile(dim, target, align=LANES):
    best = None
    t = align
    while t <= min(dim, target):
        if dim % t == 0:
            best = t
        t += align
    return dim if best is None else best


def _params(*sem):
    return pltpu.CompilerParams(dimension_semantics=sem, vmem_limit_bytes=VMEM_LIMIT)


def _mm(a, b, *, name, ta=False, tb=False, out_dtype=F32, bias=None, bscale=None, b_k0=0,
        out_n=None, out_j0=0, into=None, tm=1024, tn=1024, tk=2048):
    M, K = (a.shape[1], a.shape[0]) if ta else a.shape
    N = b.shape[0] if tb else b.shape[1]
    assert b_k0 + K <= (b.shape[1] if tb else b.shape[0]), (a.shape, b.shape, ta, tb)
    tm, tn, tk = _tile(M, tm), _tile(N, tn), _tile(K, tk)
    nk = K // tk
    assert b_k0 % tk == 0
    kb0 = b_k0 // tk
    dn = (((0 if ta else 1,), (1 if tb else 0,)), ((), ()))
    has_bias, has_scale = bias is not None, bscale is not None

    def body(*refs):
        a_ref, b_ref = refs[0], refs[1]
        pos = 2
        bias_ref = scale_ref = None
        if has_bias:
            bias_ref = refs[pos]
            pos += 1
        if has_scale:
            scale_ref = refs[pos]
            pos += 1
        if into is not None:
            pos += 1
        o_ref = refs[pos]
        acc_ref = refs[pos + 1] if nk > 1 else None

        def finish(r):
            if has_bias:
                extra = bias_ref[...].astype(F32)
                if has_scale:
                    extra = extra * scale_ref[...]
                r = r + extra
            o_ref[...] = r.astype(out_dtype)

        part = lax.dot_general(a_ref[...].astype(BF16), b_ref[...].astype(BF16), dn,
                               preferred_element_type=F32)
        if nk == 1:
            finish(part)
        else:
            k = pl.program_id(2)

            @pl.when(k == 0)
            def _():
                acc_ref[...] = part

            @pl.when(k > 0)
            def _():
                acc_ref[...] += part

            @pl.when(k == nk - 1)
            def _():
                finish(acc_ref[...])

    a_spec = pl.BlockSpec((tk, tm), lambda j, i, k: (k, i)) if ta else pl.BlockSpec((tm, tk), lambda j, i, k: (i, k))
    b_spec = (pl.BlockSpec((tn, tk), lambda j, i, k: (j, kb0 + k)) if tb
              else pl.BlockSpec((tk, tn), lambda j, i, k: (kb0 + k, j)))
    in_specs = [a_spec, b_spec]
    args = [a, b]
    if has_bias:
        in_specs.append(pl.BlockSpec((tm, tn), lambda j, i, k: (i, j)))
        args.append(bias)
    if has_scale:
        in_specs.append(pl.BlockSpec((1, tn), lambda j, i, k: (0, j)))
        args.append(bscale)
    aliases = {}
    if into is not None:
        in_specs.append(pl.BlockSpec(memory_space=pl.ANY))
        args.append(into)
        aliases = {len(args) - 1: 0}
    out_n = N if out_n is None else out_n
    assert out_j0 % tn == 0
    jb0 = out_j0 // tn
    return pl.pallas_call(
        body, name=name,
        grid=(N // tn, M // tm, nk),
        in_specs=in_specs,
        out_specs=pl.BlockSpec((tm, tn), lambda j, i, k: (i, jb0 + j)),
        out_shape=jax.ShapeDtypeStruct((M, out_n), out_dtype),
        scratch_shapes=[pltpu.VMEM((tm, tn), F32)] if nk > 1 else [],
        input_output_aliases=aliases,
        compiler_params=_params("parallel", "parallel", "arbitrary"),
    )(*args)


def _mm_bd(a_list, b_list, *, name, tb=False, out_dtype=F32, bias=None, bscale=None, tm=1024):
    G = b_list[0].shape[0]
    Kb, Nb = (b_list[0].shape[2], b_list[0].shape[1]) if tb else b_list[0].shape[1:]
    M = a_list[0].shape[0]
    tm = _tile(M, tm)
    na, nb = len(a_list), len(b_list)
    n_out = nb if na == 1 else 1
    dn = NT if tb else NN
    has_bias, has_scale = bias is not None, bscale is not None

    def body(*refs):
        a_refs, b_refs = refs[:na], refs[na:na + nb]
        pos = na + nb
        bias_ref = scale_ref = None
        if has_bias:
            bias_ref = refs[pos]
            pos += 1
        if has_scale:
            scale_ref = refs[pos]
            pos += 1
        o_refs = refs[pos:]
        prods = [lax.dot_general(a_refs[min(i, na - 1)][...].astype(BF16), b_refs[i][...].astype(BF16), dn,
                                 preferred_element_type=F32) for i in range(nb)]
        outs = prods if n_out == nb else [functools.reduce(jnp.add, prods)]
        for o_ref, r in zip(o_refs, outs):
            if has_bias:
                extra = bias_ref[...].astype(F32)
                r = r + (extra * scale_ref[...] if has_scale else extra)
            o_ref[...] = r.astype(out_dtype)

    a_spec = pl.BlockSpec((tm, Kb), lambda g, i: (i, g))
    b_spec = pl.BlockSpec((None,) + b_list[0].shape[1:], lambda g, i: (g, 0, 0))
    o_spec = pl.BlockSpec((tm, Nb), lambda g, i: (i, g))
    in_specs = [a_spec] * na + [b_spec] * nb
    args = list(a_list) + list(b_list)
    if has_bias:
        in_specs.append(o_spec)
        args.append(bias)
    if has_scale:
        in_specs.append(pl.BlockSpec((1, Nb), lambda g, i: (0, g)))
        args.append(bscale)
    sds = jax.ShapeDtypeStruct((M, G * Nb), out_dtype)
    res = pl.pallas_call(
        body, name=name, grid=(G, M // tm), in_specs=in_specs, out_specs=[o_spec] * n_out,
        out_shape=[sds] * n_out, compiler_params=_params("parallel", "parallel"),
    )(*args)
    return res[0] if n_out == 1 else res


def _mm_bd_t(a, b, G, *, name, tk=1024):
    K, Mb, Nb = a.shape[0], a.shape[1] // G, b.shape[1] // G
    tk = _tile(K, tk)
    nk = K // tk

    def body(a_ref, b_ref, o_ref, acc_ref):
        k = pl.program_id(1)
        part = lax.dot_general(a_ref[...].astype(BF16), b_ref[...].astype(BF16), TN, preferred_element_type=F32)

        @pl.when(k == 0)
        def _():
            acc_ref[...] = part

        @pl.when(k > 0)
        def _():
            acc_ref[...] += part

        @pl.when(k == nk - 1)
        def _():
            o_ref[...] = acc_ref[...]

    return pl.pallas_call(
        body, name=name, grid=(G, nk),
        in_specs=[pl.BlockSpec((tk, Mb), lambda g, k: (k, g)), pl.BlockSpec((tk, Nb), lambda g, k: (k, g))],
        out_specs=pl.BlockSpec((None, Mb, Nb), lambda g, k: (g, 0, 0)),
        out_shape=jax.ShapeDtypeStruct((G, Mb, Nb), F32),
        scratch_shapes=[pltpu.VMEM((Mb, Nb), F32)], compiler_params=_params("parallel", "arbitrary"),
    )(a, b)


def _sigmoid(x):
    return 0.5 * jnp.tanh(0.5 * x) + 0.5


def _mm_swiglu(h, w_in, *, name, tm=512, tn=1408, exchange=None):
    M, K = h.shape
    N = w_in.shape[1] // 2
    tm, tn = _tile(M, tm), _tile(N, tn)
    nj = N // tn

    def body(h_ref, wg_ref, wu_ref, g_ref, u_ref, a_ref):
        hv = h_ref[...]
        g = jnp.dot(hv, wg_ref[...], preferred_element_type=F32)
        u = jnp.dot(hv, wu_ref[...], preferred_element_type=F32)
        g_ref[...] = g.astype(BF16)
        u_ref[...] = u.astype(BF16)
        a_ref[...] = (g * _sigmoid(g) * u).astype(BF16)

    o_spec = pl.BlockSpec((tm, tn), lambda j, i: (i, j))
    sds = jax.ShapeDtypeStruct((M, N), BF16)
    return _call_with_exchange(
        body, exchange, name=name, grid=(nj, M // tm),
        in_specs=[pl.BlockSpec((tm, K), lambda j, i: (i, 0)), pl.BlockSpec((K, tn), lambda j, i: (0, j)),
                  pl.BlockSpec((K, tn), lambda j, i: (0, nj + j))],
        out_specs=[o_spec, o_spec, o_spec], out_shape=[sds, sds, sds], operands=(h, w_in, w_in))


def _mm_swiglu_bwd(dy, w_out, gate, up, *, name, tm=512, tn=1408, exchange=None):
    M, K = dy.shape
    N = w_out.shape[0]
    tm, tn = _tile(M, tm), _tile(N, tn)

    def body(dy_ref, w_ref, g_ref, u_ref, dg_ref, du_ref):
        dact = lax.dot_general(dy_ref[...], w_ref[...], NT, preferred_element_type=F32)
        g = g_ref[...].astype(F32)
        u = u_ref[...].astype(F32)
        sig = _sigmoid(g)
        dg_ref[...] = (dact * u * (sig * (1.0 + g * (1.0 - sig)))).astype(BF16)
        du_ref[...] = (dact * (g * sig)).astype(BF16)

    t_spec = pl.BlockSpec((tm, tn), lambda j, i: (i, j))
    sds = jax.ShapeDtypeStruct((M, N), BF16)
    return _call_with_exchange(
        body, exchange, name=name, grid=(N // tn, M // tm),
        in_specs=[pl.BlockSpec((tm, K), lambda j, i: (i, 0)), pl.BlockSpec((tn, K), lambda j, i: (j, 0)),
                  t_spec, t_spec],
        out_specs=[t_spec, t_spec], out_shape=[sds, sds], operands=(dy, w_out, gate, up))


ROW_TILE = 1024
MERGE_TILE = 512
GELU_TILE = 2048


def _colsum8(v):
    return jnp.sum(v.reshape(v.shape[0] // SUBLANES, SUBLANES, v.shape[1]), axis=0)


def _finish_colsums(step, last, refs):
    @pl.when(step == last)
    def _():
        for r in refs:
            r[...] = jnp.broadcast_to(jnp.sum(r[...], axis=0, keepdims=True), r.shape)


def _row_spec(t, d):
    return pl.BlockSpec((t, d), lambda i: (i, 0))


def _vec_spec(d, rows=1):
    return pl.BlockSpec((rows, d), lambda i: (0, 0))


def _prenorm(x, g, sc, sh, *, name):
    L, D = x.shape
    t = _tile(L, ROW_TILE, SUBLANES)

    def body(x_ref, g_ref, sc_ref, sh_ref, h_ref):
        xv = x_ref[...]
        r = lax.rsqrt(jnp.mean(xv * xv, axis=-1, keepdims=True) + RMS_EPS)
        h_ref[...] = (((xv * r) * g_ref[...]) * (1.0 + sc_ref[...]) + sh_ref[...]).astype(BF16)

    return pl.pallas_call(
        body, name=name, grid=(L // t,),
        in_specs=[_row_spec(t, D), _vec_spec(D), _vec_spec(D), _vec_spec(D)],
        out_specs=_row_spec(t, D), out_shape=jax.ShapeDtypeStruct((L, D), BF16),
        compiler_params=_params("parallel"),
    )(x, g, sc, sh)


def _mm_postnorm(a, b, x, g, gate, res_w, *, name, tm=1024):
    M, K = a.shape
    D = b.shape[1]
    tm = _tile(M, tm, SUBLANES)

    def body(a_ref, b_ref, x_ref, g_ref, gate_ref, y_ref, o_ref):
        yv = jnp.dot(a_ref[...], b_ref[...], preferred_element_type=F32)
        y_ref[...] = yv
        r = lax.rsqrt(jnp.mean(yv * yv, axis=-1, keepdims=True) + RMS_EPS)
        o_ref[...] = x_ref[...] + (res_w * gate_ref[...]) * ((yv * r) * g_ref[...])

    sds = jax.ShapeDtypeStruct((M, D), F32)
    return pl.pallas_call(
        body, name=name, grid=(M // tm,),
        in_specs=[_row_spec(tm, K), pl.BlockSpec((K, D), lambda i: (0, 0)), _row_spec(tm, D), _vec_spec(D),
                  _vec_spec(D)],
        out_specs=[_row_spec(tm, D), _row_spec(tm, D)], out_shape=[sds, sds],
        compiler_params=_params("parallel"),
    )(a, b, x, g, gate)


def _postnorm_bwd(dxo, y, g, gate, res_w, *, name, exchange=None):
    L, D = y.shape
    t = _tile(L, ROW_TILE, SUBLANES)
    n = L // t

    def body(dxo_ref, y_ref, g_ref, gate_ref, dy_ref, dgate_ref, dg_ref):
        i = pl.program_id(0)

        @pl.when(i == 0)
        def _():
            dgate_ref[...] = jnp.zeros_like(dgate_ref)
            dg_ref[...] = jnp.zeros_like(dg_ref)

        yv = y_ref[...]
        dv = dxo_ref[...]
        gv = g_ref[...]
        r = lax.rsqrt(jnp.mean(yv * yv, axis=-1, keepdims=True) + RMS_EPS)
        yn = yv * r
        dgate_ref[...] += _colsum8(dv * (res_w * (yn * gv)))
        do = dv * (res_w * gate_ref[...])
        dg_ref[...] += _colsum8(do * yn)
        dyn = do * gv
        dy_ref[...] = (r * (dyn - yn * jnp.mean(dyn * yn, axis=-1, keepdims=True))).astype(BF16)
        _finish_colsums(i, n - 1, (dgate_ref, dg_ref))

    sum_sds = jax.ShapeDtypeStruct((SUBLANES, D), F32)
    return _call_with_exchange(
        body, exchange, name=name, grid=(n,), sem=("arbitrary",),
        in_specs=[_row_spec(t, D), _row_spec(t, D), _vec_spec(D), _vec_spec(D)],
        out_specs=[_row_spec(t, D), _vec_spec(D, SUBLANES), _vec_spec(D, SUBLANES)],
        out_shape=[jax.ShapeDtypeStruct((L, D), BF16), sum_sds, sum_sds], operands=(dxo, y, g, gate))


def _prenorm_bwd(x, dh, dxres, g, sc, *, name):
    L, D = x.shape
    t = _tile(L, ROW_TILE, SUBLANES)
    n = L // t

    def body(x_ref, dh_ref, dxr_ref, g_ref, sc_ref, dx_ref, dsh_ref, dsc_ref, dg_ref):
        i = pl.program_id(0)

        @pl.when(i == 0)
        def _():
            dsh_ref[...] = jnp.zeros_like(dsh_ref)
            dsc_ref[...] = jnp.zeros_like(dsc_ref)
            dg_ref[...] = jnp.zeros_like(dg_ref)

        xv = x_ref[...]
        dhv = dh_ref[...].astype(F32)
        gv = g_ref[...]
        one_sc = 1.0 + sc_ref[...]
        r = lax.rsqrt(jnp.mean(xv * xv, axis=-1, keepdims=True) + RMS_EPS)
        xn = xv * r
        tt = dhv * xn
        dsh_ref[...] += _colsum8(dhv)
        dsc_ref[...] += _colsum8(tt * gv)
        dg_ref[...] += _colsum8(tt * one_sc)
        dxn = dhv * (gv * one_sc)
        dx_ref[...] = dxr_ref[...] + r * (dxn - xn * jnp.mean(dxn * xn, axis=-1, keepdims=True))
        _finish_colsums(i, n - 1, (dsh_ref, dsc_ref, dg_ref))

    sum_sds = jax.ShapeDtypeStruct((SUBLANES, D), F32)
    sum_spec = _vec_spec(D, SUBLANES)
    return pl.pallas_call(
        body, name=name, grid=(n,),
        in_specs=[_row_spec(t, D), _row_spec(t, D), _row_spec(t, D), _vec_spec(D), _vec_spec(D)],
        out_specs=[_row_spec(t, D), sum_spec, sum_spec, sum_spec],
        out_shape=[jax.ShapeDtypeStruct((L, D), F32), sum_sds, sum_sds, sum_sds],
        compiler_params=_params("arbitrary"),
    )(x, dh, dxres, g, sc)


def _loss_head(y, target, *, name):
    L, D = y.shape
    t = _tile(L, ROW_TILE, SUBLANES)
    n = L // t

    def body(y_ref, t_ref, dy_ref, loss_ref):
        i = pl.program_id(0)

        @pl.when(i == 0)
        def _():
            loss_ref[...] = jnp.zeros_like(loss_ref)

        e = y_ref[...] - t_ref[...]
        dy_ref[...] = e * (1.0 / D)
        part = jnp.sum(jnp.mean(e * e, axis=-1, keepdims=True), axis=0, keepdims=True)
        loss_ref[...] += jnp.broadcast_to(0.5 * part, loss_ref.shape)

    return pl.pallas_call(
        body, name=name, grid=(n,),
        in_specs=[_row_spec(t, D), _row_spec(t, D)],
        out_specs=[_row_spec(t, D), pl.BlockSpec((SUBLANES, LANES), lambda i: (0, 0))],
        out_shape=[jax.ShapeDtypeStruct((L, D), F32), jax.ShapeDtypeStruct((SUBLANES, LANES), F32)],
        compiler_params=_params("arbitrary"),
    )(y, target)


GELU_C = math.sqrt(2.0 / math.pi)


def _gelu_fwd(y, *, name):
    L, W = y.shape
    t = _tile(L, GELU_TILE, SUBLANES)

    def body(y_ref, o_ref):
        v = y_ref[...]
        o_ref[...] = (0.5 * v * (1.0 + jnp.tanh(GELU_C * (v + 0.044715 * (v * v * v))))).astype(BF16)

    return pl.pallas_call(
        body, name=name, grid=(L // t,), in_specs=[_row_spec(t, W)], out_specs=_row_spec(t, W),
        out_shape=jax.ShapeDtypeStruct((L, W), BF16), compiler_params=_params("parallel"),
    )(y)


def _gelu_bwd(dgl, y, u, dskip, *, name):
    L, W = y.shape
    t = _tile(L, GELU_TILE, SUBLANES)
    n = L // t

    def body(dgl_ref, y_ref, u_ref, d_ref, dy_ref, sk_ref, dd_ref):
        i = pl.program_id(0)

        @pl.when(i == 0)
        def _():
            dd_ref[...] = jnp.zeros_like(dd_ref)

        v = y_ref[...]
        inner = GELU_C * (v + 0.044715 * (v * v * v))
        th = jnp.tanh(inner)
        dgelu = 0.5 * (1.0 + th) + 0.5 * v * (1.0 - th * th) * (GELU_C * (1.0 + 3.0 * 0.044715 * (v * v)))
        dy = dgl_ref[...] * dgelu
        dy_ref[...] = dy.astype(BF16)
        sk_ref[...] = dy * d_ref[...]
        dd_ref[...] += _colsum8(dy * u_ref[...])
        _finish_colsums(i, n - 1, (dd_ref,))

    return pl.pallas_call(
        body, name=name, grid=(n,),
        in_specs=[_row_spec(t, W), _row_spec(t, W), _row_spec(t, W), _vec_spec(W)],
        out_specs=[_row_spec(t, W), _row_spec(t, W), _vec_spec(W, SUBLANES)],
        out_shape=[jax.ShapeDtypeStruct((L, W), BF16), jax.ShapeDtypeStruct((L, W), F32),
                   jax.ShapeDtypeStruct((SUBLANES, W), F32)],
        compiler_params=_params("arbitrary"),
    )(dgl, y, u, dskip)


def _merge_fwd(z, yb, gab, *, name):
    L, D = yb.shape
    t = _tile(L, MERGE_TILE, SUBLANES)

    def body(z_ref, yb_ref, gab_ref, o_ref):
        ya = z_ref[:, :D] * _sigmoid(z_ref[:, D:])
        o_ref[...] = (_sigmoid(gab_ref[:, :D]) * ya + _sigmoid(gab_ref[:, D:]) * yb_ref[...]).astype(BF16)

    return pl.pallas_call(
        body, name=name, grid=(L // t,),
        in_specs=[_row_spec(t, 2 * D), _row_spec(t, D), _row_spec(t, 2 * D)],
        out_specs=_row_spec(t, D), out_shape=jax.ShapeDtypeStruct((L, D), BF16),
        compiler_params=_params("parallel"),
    )(z, yb, gab)


def _merge_bwd(dm, z, yb, gab, *, name, exchange=None):
    L, D = yb.shape
    t = _tile(L, MERGE_TILE, SUBLANES)

    def body(dm_ref, z_ref, yb_ref, gab_ref, dz_ref, dyb_ref, dgab_ref):
        dmv = dm_ref[...]
        zv = z_ref[:, :D]
        sz = _sigmoid(z_ref[:, D:])
        sa = _sigmoid(gab_ref[:, :D])
        sb = _sigmoid(gab_ref[:, D:])
        ybv = yb_ref[...]
        dya = dmv * sa
        dz_ref[:, :D] = (dya * sz).astype(BF16)
        dz_ref[:, D:] = (dya * zv * (sz * (1.0 - sz))).astype(BF16)
        dyb_ref[...] = (dmv * sb).astype(BF16)
        dgab_ref[:, :D] = (dmv * (zv * sz) * (sa * (1.0 - sa))).astype(BF16)
        dgab_ref[:, D:] = (dmv * ybv * (sb * (1.0 - sb))).astype(BF16)

    return _call_with_exchange(
        body, exchange, name=name, grid=(L // t,),
        in_specs=[_row_spec(t, D), _row_spec(t, 2 * D), _row_spec(t, D), _row_spec(t, 2 * D)],
        out_specs=[_row_spec(t, 2 * D), _row_spec(t, D), _row_spec(t, 2 * D)],
        out_shape=[jax.ShapeDtypeStruct((L, 2 * D), BF16), jax.ShapeDtypeStruct((L, D), BF16),
                   jax.ShapeDtypeStruct((L, 2 * D), BF16)], operands=(dm, z, yb, gab))


SCAN_W = 1024
SCAN_T = 512


def _interleave(x):
    L, W = x.shape
    seg = SCAN_T // SUBLANES
    return x.reshape(L // SCAN_T, SUBLANES, seg, W).transpose(0, 2, 1, 3).reshape(L, W)


def _deinterleave(x):
    L, W = x.shape
    seg = SCAN_T // SUBLANES
    return x.reshape(L // SCAN_T, seg, SUBLANES, W).transpose(0, 2, 1, 3).reshape(L, W)


def _power_table(a, b, pr_tab, pi_tab, n):
    def fill(k, carry):
        pr, pi = carry
        pr_tab[k] = pr
        pi_tab[k] = pi
        return a * pr - b * pi, a * pi + b * pr

    lax.fori_loop(0, n, fill, (a, b))


def _rows_to_tile(rows):
    w = rows[0].shape[1]
    sub = lax.broadcasted_iota(jnp.int32, (SUBLANES, w), 0)
    tile = jnp.broadcast_to(rows[0], (SUBLANES, w))
    for j in range(1, SUBLANES):
        tile = jnp.where(sub == j, jnp.broadcast_to(rows[j], (SUBLANES, w)), tile)
    return tile


def _ssm_scan_fwd(bu_re, bu_im, lam_re, lam_im, *, name):
    L, S = bu_re.shape
    w, t = _tile(S, SCAN_W), SCAN_T
    seg = t // SUBLANES

    def body(br_ref, bi_ref, lr_ref, li_ref, sr_ref, si_ref, pr_tab, pi_tab, cr_ref, ci_ref):
        a = jnp.broadcast_to(lr_ref[...], (SUBLANES, w))
        b = jnp.broadcast_to(li_ref[...], (SUBLANES, w))

        @pl.when(pl.program_id(1) == 0)
        def _():
            cr_ref[...] = jnp.zeros_like(cr_ref)
            ci_ref[...] = jnp.zeros_like(ci_ref)
            _power_table(a, b, pr_tab, pi_tab, seg)

        def local_scan(i, carry):
            sr, si = carry
            rows = pl.ds(pl.multiple_of(i * SUBLANES, SUBLANES), SUBLANES)
            nr = a * sr - b * si + br_ref[rows, :]
            ni = a * si + b * sr + bi_ref[rows, :]
            sr_ref[rows, :] = nr
            si_ref[rows, :] = ni
            return nr, ni

        zero = jnp.zeros((SUBLANES, w), F32)
        fr, fi = lax.fori_loop(0, seg, local_scan, (zero, zero), unroll=2)
        lsr, lsi = pr_tab[seg - 1][0:1, :], pi_tab[seg - 1][0:1, :]
        cr, ci = cr_ref[...], ci_ref[...]
        rows_r, rows_i = [], []
        for j in range(SUBLANES):
            rows_r.append(cr)
            rows_i.append(ci)
            cr, ci = fr[j:j + 1, :] + (lsr * cr - lsi * ci), fi[j:j + 1, :] + (lsr * ci + lsi * cr)
        cr_ref[...] = cr
        ci_ref[...] = ci
        in_r, in_i = _rows_to_tile(rows_r), _rows_to_tile(rows_i)

        def add_entry(i, _):
            rows = pl.ds(pl.multiple_of(i * SUBLANES, SUBLANES), SUBLANES)
            pr, pi = pr_tab[i], pi_tab[i]
            sr_ref[rows, :] += pr * in_r - pi * in_i
            si_ref[rows, :] += pr * in_i + pi * in_r
            return 0

        lax.fori_loop(0, seg, add_entry, 0, unroll=2)

    blk = pl.BlockSpec((t, w), lambda j, i: (i, j))
    vec = pl.BlockSpec((1, w), lambda j, i: (0, j))
    sds = jax.ShapeDtypeStruct((L, S), F32)
    tab = pltpu.VMEM((seg, SUBLANES, w), F32)
    return pl.pallas_call(
        body, name=name, grid=(S // w, L // t),
        in_specs=[blk, blk, vec, vec], out_specs=[blk, blk], out_shape=[sds, sds],
        scratch_shapes=[tab, tab, pltpu.VMEM((1, w), F32), pltpu.VMEM((1, w), F32)],
        compiler_params=_params("parallel", "arbitrary"),
    )(bu_re, bu_im, lam_re, lam_im)


def _ssm_scan_bwd(d_re, d_im, s_re, s_im, lam_re, lam_im, *, name):
    L, S = d_re.shape
    w, t = _tile(S, SCAN_W), SCAN_T
    nt = L // t
    seg = t // SUBLANES

    def body(dr_ref, di_ref, sr_ref, si_ref, lr_ref, li_ref, gr_ref, gi_ref, ar_ref, ai_ref,
             pr_tab, pi_tab, cgr, cgi, acc_r, acc_i):
        step = pl.program_id(1)
        a = jnp.broadcast_to(lr_ref[...], (SUBLANES, w))
        b = jnp.broadcast_to(-li_ref[...], (SUBLANES, w))

        @pl.when(step == 0)
        def _():
            for r in (cgr, cgi, acc_r, acc_i):
                r[...] = jnp.zeros_like(r)
            _power_table(a, b, pr_tab, pi_tab, seg)

        def local_scan(ii, carry):
            gr, gi = carry
            rows = pl.ds(pl.multiple_of((seg - 1 - ii) * SUBLANES, SUBLANES), SUBLANES)
            ngr = a * gr - b * gi + dr_ref[rows, :]
            ngi = a * gi + b * gr + di_ref[rows, :]
            gr_ref[rows, :] = ngr
            gi_ref[rows, :] = ngi
            return ngr, ngi

        zero = jnp.zeros((SUBLANES, w), F32)
        fr, fi = lax.fori_loop(0, seg, local_scan, (zero, zero), unroll=2)
        lsr, lsi = pr_tab[seg - 1][0:1, :], pi_tab[seg - 1][0:1, :]
        cr, ci = cgr[...], cgi[...]
        rows_r, rows_i = [None] * SUBLANES, [None] * SUBLANES
        for j in reversed(range(SUBLANES)):
            rows_r[j], rows_i[j] = cr, ci
            cr, ci = fr[j:j + 1, :] + (lsr * cr - lsi * ci), fi[j:j + 1, :] + (lsr * ci + lsi * cr)
        cgr[...] = cr
        cgi[...] = ci
        in_r, in_i = _rows_to_tile(rows_r), _rows_to_tile(rows_i)

        def add_entry(ii, carry):
            nr, ni, xr, xi = carry
            rows = pl.ds(pl.multiple_of((seg - 1 - ii) * SUBLANES, SUBLANES), SUBLANES)
            sr = sr_ref[rows, :]
            si = si_ref[rows, :]
            xr = xr + (nr * sr + ni * si)
            xi = xi + (ni * sr - nr * si)
            pr, pi = pr_tab[ii], pi_tab[ii]
            gr = gr_ref[rows, :] + (pr * in_r - pi * in_i)
            gi = gi_ref[rows, :] + (pr * in_i + pi * in_r)
            gr_ref[rows, :] = gr
            gi_ref[rows, :] = gi
            return gr, gi, xr, xi

        _, _, xr, xi = lax.fori_loop(0, seg, add_entry, (in_r, in_i, acc_r[...], acc_i[...]), unroll=2)
        acc_r[...] = xr
        acc_i[...] = xi

        @pl.when(step == nt - 1)
        def _():
            ar_ref[...] = jnp.sum(xr, axis=0, keepdims=True)
            ai_ref[...] = jnp.sum(xi, axis=0, keepdims=True)

    blk = pl.BlockSpec((t, w), lambda j, i: (nt - 1 - i, j))
    vec = pl.BlockSpec((1, w), lambda j, i: (0, j))
    sds = jax.ShapeDtypeStruct((L, S), F32)
    vsds = jax.ShapeDtypeStruct((1, S), F32)
    tab = pltpu.VMEM((seg, SUBLANES, w), F32)
    tile = pltpu.VMEM((SUBLANES, w), F32)
    return pl.pallas_call(
        body, name=name, grid=(S // w, nt),
        in_specs=[blk, blk, blk, blk, vec, vec], out_specs=[blk, blk, vec, vec],
        out_shape=[sds, sds, vsds, vsds],
        scratch_shapes=[tab, tab, pltpu.VMEM((1, w), F32), pltpu.VMEM((1, w), F32), tile, tile],
        compiler_params=_params("parallel", "arbitrary"),
    )(d_re, d_im, s_re, s_im, lam_re, lam_im)


def _ssm_discretize(a_re, a_im, log_dt, b_re, b_im, c_re, c_im, d_skip):
    G, P, N = SSM_GROUPS, SSM_STATE, SSM_GROUP
    a = jnp.minimum(a_re, -1e-4)
    dt = jnp.exp(log_dt)[:, None]
    mag = jnp.exp(a * dt)
    lr = mag * jnp.cos(a_im * dt)
    li = mag * jnp.sin(a_im * dt)
    den = a * a + a_im * a_im
    cr = ((lr - 1.0) * a + li * a_im) / den
    ci = (li * a - (lr - 1.0) * a_im) / den
    bbr = cr[..., None] * b_re - ci[..., None] * b_im
    bbi = cr[..., None] * b_im + ci[..., None] * b_re
    gl = G // SSM_BLOCKS
    eye = jnp.eye(gl, dtype=F32)[None, :, None, :, None]

    def in_map(bb):
        t = bb.transpose(0, 2, 1).reshape(SSM_BLOCKS, gl, N, P)
        return (eye * t[:, :, :, None, :]).reshape(SSM_BLOCKS, gl * N, gl * P)

    def out_map(c):
        t = c.transpose(0, 2, 1).reshape(SSM_BLOCKS, gl, P, N)
        return (eye * t[:, :, :, None, :]).reshape(SSM_BLOCKS, gl * P, gl * N)

    return (lr.reshape(1, G * P), li.reshape(1, G * P), in_map(bbr), in_map(bbi),
            out_map(c_re), out_map(-c_im), d_skip.reshape(1, SSM_WIDTH))


ATT_T = 512
ATT_WIDE = 2
CUM_T = 512


def _split3(x):
    hi = x.astype(BF16)
    r1 = x - hi.astype(F32)
    mid = r1.astype(BF16)
    lo = (r1 - mid.astype(F32)).astype(BF16)
    return hi, mid, lo


def _tri_dot(tri, x):
    hi, mid, lo = _split3(x)
    dot = lambda p: jnp.dot(tri, p, preferred_element_type=F32)
    return dot(hi) + dot(mid) + dot(lo)


def _log_sigmoid(x):
    return jnp.minimum(x, 0.0) - jnp.log(1.0 + jnp.exp(-jnp.abs(x)))


def _fox_cum(f, fb, *, name):
    L, W = f.shape
    t = _tile(L, CUM_T, SUBLANES)

    def body(f_ref, b_ref, o_ref, carry):
        @pl.when(pl.program_id(0) == 0)
        def _():
            carry[...] = jnp.zeros_like(carry)

        row = lax.broadcasted_iota(jnp.int32, (t, t), 0)
        col = lax.broadcasted_iota(jnp.int32, (t, t), 1)
        tri = jnp.where(col <= row, 1.0, 0.0).astype(BF16)
        c = _tri_dot(tri, _log_sigmoid(f_ref[...] + b_ref[...])) + carry[...]
        o_ref[...] = c
        carry[...] = c[t - 1:t, :]

    return pl.pallas_call(
        body, name=name, grid=(L // t,),
        in_specs=[_row_spec(t, W), _vec_spec(W)], out_specs=_row_spec(t, W),
        out_shape=jax.ShapeDtypeStruct((L, W), F32),
        scratch_shapes=[pltpu.VMEM((1, W), F32)], compiler_params=_params("arbitrary"),
    )(f, fb)


def _fox_cum_bwd(dcum, f, fb, *, name):
    L, W = f.shape
    t = _tile(L, CUM_T, SUBLANES)
    n = L // t

    def body(d_ref, f_ref, b_ref, o_ref, db_ref, carry):
        i = pl.program_id(0)

        @pl.when(i == 0)
        def _():
            carry[...] = jnp.zeros_like(carry)
            db_ref[...] = jnp.zeros_like(db_ref)

        row = lax.broadcasted_iota(jnp.int32, (t, t), 0)
        col = lax.broadcasted_iota(jnp.int32, (t, t), 1)
        tri = jnp.where(col >= row, 1.0, 0.0).astype(BF16)
        dlog = _tri_dot(tri, d_ref[...]) + carry[...]
        carry[...] = dlog[0:1, :]
        df = dlog * _sigmoid(-(f_ref[...] + b_ref[...]))
        o_ref[...] = df.astype(BF16)
        db_ref[...] += _colsum8(df)
        _finish_colsums(i, n - 1, (db_ref,))

    rev = pl.BlockSpec((t, W), lambda i: (n - 1 - i, 0))
    return pl.pallas_call(
        body, name=name, grid=(n,),
        in_specs=[rev, rev, _vec_spec(W)], out_specs=[rev, _vec_spec(W, SUBLANES)],
        out_shape=[jax.ShapeDtypeStruct((L, W), BF16), jax.ShapeDtypeStruct((SUBLANES, W), F32)],
        scratch_shapes=[pltpu.VMEM((1, W), F32)], compiler_params=_params("arbitrary"),
    )(dcum, f, fb)


def _head_col(blk, h):
    lane = lax.broadcasted_iota(jnp.int32, blk.shape, 1)
    return jnp.sum(jnp.where(lane == h * HEAD_DIM, blk, 0.0), axis=1, keepdims=True)


def _lo_mask(rows):
    return lax.broadcasted_iota(jnp.int32, (rows, LANES), 1) < HEAD_DIM


def _causal(t):
    row = lax.broadcasted_iota(jnp.int32, (t, t), 0)
    col = lax.broadcasted_iota(jnp.int32, (t, t), 1)
    return col <= row


def _call_with_exchange(body, exchange, *, name, grid, in_specs, out_specs, out_shape, operands, sem=None):
    sem = sem or ("parallel",) + ("arbitrary",) * (len(grid) - 1)
    if exchange is None:
        return pl.pallas_call(body, name=name, grid=grid, in_specs=in_specs, out_specs=out_specs,
                              out_shape=out_shape, compiler_params=_params(*sem))(*operands)
    n_in, n_out = len(in_specs), len(out_specs)
    ei, eo = len(exchange.operands), len(exchange.out_shapes)

    def wrapped(*refs):
        ins, ex_in = refs[:n_in], refs[n_in:n_in + ei]
        outs, ex_out = refs[n_in + ei:n_in + ei + n_out], refs[n_in + ei + n_out:n_in + ei + n_out + eo]
        sems = refs[n_in + ei + n_out + eo:]
        ids = [pl.program_id(d) for d in range(len(grid))]
        first = functools.reduce(jnp.logical_and, [i == 0 for i in ids])
        last = functools.reduce(jnp.logical_and, [i == g - 1 for i, g in zip(ids, grid)])

        @pl.when(first)
        def _():
            exchange.start(ex_in, ex_out, sems)

        body(*ins, *outs)

        @pl.when(last)
        def _():
            exchange.finish(ex_in, ex_out, sems)

    return pl.pallas_call(
        wrapped, name=name, grid=grid, in_specs=list(in_specs) + [HBM_SPEC] * ei,
        out_specs=list(out_specs) + [HBM_SPEC] * eo, out_shape=list(out_shape) + list(exchange.out_shapes),
        scratch_shapes=exchange.sems, compiler_params=_params(*(("arbitrary",) * len(grid))),
    )(*operands, *exchange.operands)


def _fox_fwd(qkv, cum_cols, cum_rows, *, name, exchange=None):
    L = qkv.shape[0]
    t = _tile(L, ATT_T)
    nq = L // t
    npair = ATTN_HEADS // 2

    def body(q_ref, k_ref, v_ref, cc_ref, cr_ref, o_ref, o32_ref, lse_ref):
        iq = pl.program_id(1)
        lo = _lo_mask(t)
        qv = q_ref[...] * 0.125
        zq = jnp.zeros_like(qv)
        qh = (jnp.where(lo, qv, zq), jnp.where(lo, zq, qv))
        ccv = cc_ref[...]
        cq = (_head_col(ccv, 0), _head_col(ccv, 1))

        def step(block, width, carry, masked):
            start = pl.multiple_of(block * width, width)
            kb = k_ref[pl.ds(start, width), :]
            vb = v_ref[pl.ds(start, width), :]
            out = []
            for h in range(2):
                m, l, acc = carry[h]
                s = lax.dot_general(qh[h], kb, NT, preferred_element_type=F32)
                s = s + (cq[h] - cr_ref[h:h + 1, pl.ds(start, width)])
                if masked:
                    s = jnp.where(_causal(t), s, -jnp.inf)
                m_new = jnp.maximum(m, jnp.max(s, axis=1, keepdims=True))
                alpha = jnp.exp(m - m_new)
                p = jnp.exp(s - m_new)
                l = alpha * l + jnp.sum(p, axis=1, keepdims=True)
                acc = alpha * acc + jnp.dot(p.astype(BF16), vb, preferred_element_type=F32)
                out.append((m_new, l, acc))
            return tuple(out)

        init1 = (jnp.full((t, 1), -jnp.inf, F32), jnp.zeros((t, 1), F32), jnp.zeros((t, LANES), F32))
        n4, rest = iq // 4, iq % 4
        carry = lax.fori_loop(0, n4, lambda j, c: step(j, 4 * t, c, False), (init1, init1))
        carry = lax.fori_loop(2 * n4, 2 * n4 + rest // 2, lambda j, c: step(j, 2 * t, c, False), carry)
        carry = lax.fori_loop(iq - rest % 2, iq, lambda ik, c: step(ik, t, c, False), carry)
        (m0, l0, a0), (m1, l1, a1) = step(iq, t, carry, True)
        out = jnp.where(lo, a0 / l0, a1 / l1)
        o_ref[...] = out.astype(BF16)
        o32_ref[...] = out
        lse_ref[...] = jnp.where(lo, m0 + jnp.log(l0), m1 + jnp.log(l1))

    blk = lambda off: pl.BlockSpec((t, LANES), lambda hp, iq: (iq, off + hp))
    whole = lambda off: pl.BlockSpec((L, LANES), lambda hp, iq: (0, off + hp))
    return _call_with_exchange(
        body, exchange, name=name, grid=(npair, nq),
        in_specs=[blk(0), whole(npair), whole(2 * npair), blk(0),
                  pl.BlockSpec((None, SUBLANES, L), lambda hp, iq: (hp, 0, 0))],
        out_specs=[blk(0), blk(0), blk(0)],
        out_shape=[jax.ShapeDtypeStruct((L, ATTN_WIDTH), BF16), jax.ShapeDtypeStruct((L, ATTN_WIDTH), F32),
                   jax.ShapeDtypeStruct((L, ATTN_WIDTH), F32)],
        operands=(qkv, qkv, qkv, cum_cols, cum_rows))


STAT_LSE, STAT_CUM, STAT_DELTA = 0, 2, 4


def _lane_col(blk, idx):
    lane = lax.broadcasted_iota(jnp.int32, blk.shape, 1)
    return jnp.sum(jnp.where(lane == idx, blk, 0.0), axis=1, keepdims=True)


def _fox_rowstats(do, o, lse, cum_cols, *, name):
    L = do.shape[0]
    t = _tile(L, ATT_T)

    def body(do_ref, o_ref, lse_ref, cc_ref, st_ref):
        lo = _lo_mask(t)
        dd = do_ref[...].astype(F32) * o_ref[...]
        lsev, ccv = lse_ref[...], cc_ref[...]
        cols = (_head_col(lsev, 0), _head_col(lsev, 1), _head_col(ccv, 0), _head_col(ccv, 1),
                jnp.sum(jnp.where(lo, dd, 0.0), axis=1, keepdims=True),
                jnp.sum(jnp.where(lo, 0.0, dd), axis=1, keepdims=True))
        lane = lax.broadcasted_iota(jnp.int32, (t, LANES), 1)
        out = jnp.zeros((t, LANES), F32)
        for i, col in enumerate(cols):
            out = jnp.where(lane == i, col, out)
        st_ref[...] = out

    blk = pl.BlockSpec((t, LANES), lambda hp, i: (i, hp))
    return pl.pallas_call(
        body, name=name, grid=(ATTN_HEADS // 2, L // t),
        in_specs=[blk, blk, blk, blk], out_specs=blk,
        out_shape=jax.ShapeDtypeStruct((L, ATTN_WIDTH), F32),
        compiler_params=_params("parallel", "parallel"),
    )(do, o, lse, cum_cols)


def _fox_bwd(qkv, do, stats, cum_rows, *, name, exchange=None):
    L = qkv.shape[0]
    t = _tile(L, ATT_T)
    nq = L // t
    npair = ATTN_HEADS // 2

    def body(q_ref, do_ref, st_ref, qt_ref, dot_ref, k_ref, v_ref, cr_ref, dk_ref, dv_ref, dc_ref, dq_ref, drow_ref):
        ik = pl.program_id(1)

        @pl.when(ik == 0)
        def _():
            dq_ref[...] = jnp.zeros_like(dq_ref)
            drow_ref[...] = jnp.zeros_like(drow_ref)

        lo = _lo_mask(t)
        lo_rows = lax.broadcasted_iota(jnp.int32, (LANES, t), 0) < HEAD_DIM
        lane = lax.broadcasted_iota(jnp.int32, (t, LANES), 1)
        kb = k_ref[...]
        vb = v_ref[...]
        zk = jnp.zeros_like(kb)
        kh = (jnp.where(lo, kb, zk), jnp.where(lo, zk, kb))
        vh = (jnp.where(lo, vb, zk), jnp.where(lo, zk, vb))
        ck = (cr_ref[0:1, :], cr_ref[1:2, :])

        def step(block, width, carry, masked):
            dk, dv, dc0, dc1 = carry
            start = pl.multiple_of(block * width, width)
            qb = q_ref[pl.ds(start, width), :] * 0.125
            dob = do_ref[pl.ds(start, width), :]
            stb = st_ref[pl.ds(start, width), :]
            lane = lax.broadcasted_iota(jnp.int32, (width, LANES), 1)
            qtb = qt_ref[:, pl.ds(start, width)] * 0.125
            dotb = dot_ref[:, pl.ds(start, width)]
            dks, dvs, dcs, dqs, rss = [], [], [], [], []
            for h in range(2):
                s = lax.dot_general(qb, kh[h], NT, preferred_element_type=F32)
                s = s + (_lane_col(stb, STAT_CUM + h) - ck[h])
                if masked:
                    s = jnp.where(_causal(t), s, -jnp.inf)
                p = jnp.exp(s - _lane_col(stb, STAT_LSE + h))
                dp = lax.dot_general(dob, vh[h], NT, preferred_element_type=F32)
                ds = p * (dp - _lane_col(stb, STAT_DELTA + h))
                dsb = ds.astype(BF16)
                dvs.append(jnp.dot(dotb, p.astype(BF16), preferred_element_type=F32))
                dks.append(jnp.dot(qtb, dsb, preferred_element_type=F32))
                dqs.append(jnp.dot(dsb, kh[h], preferred_element_type=F32))
                dcs.append(jnp.sum(ds, axis=0, keepdims=True))
                rss.append(jnp.sum(ds, axis=1, keepdims=True))
            dq_ref[pl.ds(start, width), :] += 0.125 * (dqs[0] + dqs[1])
            drow_ref[pl.ds(start, width), :] += jnp.where(lane == 0, rss[0], jnp.where(lane == 1, rss[1], 0.0))
            return (dk + jnp.where(lo_rows, dks[0], dks[1]), dv + jnp.where(lo_rows, dvs[0], dvs[1]),
                    dc0 - dcs[0], dc1 - dcs[1])

        zero = jnp.zeros((LANES, t), F32)
        zrow = jnp.zeros((1, t), F32)
        carry = step(ik, t, (zero, zero, zrow, zrow), True)
        first_wide = (ik + ATT_WIDE) // ATT_WIDE
        carry = lax.fori_loop(ik + 1, jnp.minimum(first_wide * ATT_WIDE, nq), lambda iq, c: step(iq, t, c, False),
                              carry)
        dk, dv, dc0, dc1 = lax.fori_loop(first_wide, nq // ATT_WIDE,
                                         lambda j, c: step(j, ATT_WIDE * t, c, False), carry)
        dk_ref[...] = dk.T.astype(BF16)
        dv_ref[...] = dv.T.astype(BF16)
        dc_ref[...] = jnp.zeros_like(dc_ref)
        dc_ref[0:1, :] = dc0
        dc_ref[1:2, :] = dc1

    whole = lambda off: pl.BlockSpec((L, LANES), lambda hp, ik: (0, off + hp))
    blk = lambda off: pl.BlockSpec((t, LANES), lambda hp, ik: (ik, off + hp))
    rows = pl.BlockSpec((None, SUBLANES, t), lambda hp, ik: (hp, 0, ik))
    whole_t = pl.BlockSpec((LANES, L), lambda hp, ik: (hp, 0))
    return _call_with_exchange(
        body, exchange, name=name, grid=(npair, nq),
        in_specs=[whole(0), whole(0), whole(0), whole_t, whole_t, blk(npair), blk(2 * npair), rows],
        out_specs=[blk(0), blk(0), rows, whole(0), whole(0)],
        out_shape=[jax.ShapeDtypeStruct((L, ATTN_WIDTH), BF16), jax.ShapeDtypeStruct((L, ATTN_WIDTH), BF16),
                   jax.ShapeDtypeStruct((npair, SUBLANES, L), F32),
                   jax.ShapeDtypeStruct((L, ATTN_WIDTH), F32), jax.ShapeDtypeStruct((L, ATTN_WIDTH), F32)],
        operands=(qkv, do, stats, qkv[:, :ATTN_WIDTH].T, do.T, qkv, qkv, cum_rows))


def _mod_partial(c_all, mod_w, mod_b_cols, *, name):
    depth, K, cols = mod_w.shape
    tn = _tile(cols, 768)

    def body(c_ref, w_ref, b_ref, o_ref):
        cv = c_ref[...]
        sc = (cv * _sigmoid(cv)).astype(BF16)
        o_ref[...] = jnp.dot(sc, w_ref[...].astype(BF16), preferred_element_type=F32) + b_ref[...]

    return pl.pallas_call(
        body, name=name, grid=(depth, cols // tn),
        in_specs=[pl.BlockSpec((N_DEV, K), lambda l, j: (0, 0)),
                  pl.BlockSpec((None, K, tn), lambda l, j: (l, 0, j)),
                  pl.BlockSpec((None, 1, tn), lambda l, j: (l, 0, j))],
        out_specs=pl.BlockSpec((None, N_DEV, tn), lambda l, j: (l, 0, j)),
        out_shape=jax.ShapeDtypeStruct((depth, N_DEV, cols), F32),
        compiler_params=_params("parallel", "parallel"),
    )(c_all, mod_w, mod_b_cols)


def _mod_wgrad(c_all_t, dmod, *, name):
    depth, nb, cols = dmod.shape
    K = c_all_t.shape[0]
    tn = _tile(cols, 768)
    tk = _tile(K, 256, SUBLANES)

    def body(c_ref, d_ref, o_ref):
        cv = c_ref[...]
        sc = cv * _sigmoid(cv)
        dv = d_ref[...]
        acc = sc[:, 0:1] * dv[0:1, :]
        for b in range(1, nb):
            acc = acc + sc[:, b:b + 1] * dv[b:b + 1, :]
        o_ref[...] = acc

    return pl.pallas_call(
        body, name=name, grid=(depth, K // tk, cols // tn),
        in_specs=[pl.BlockSpec((tk, nb), lambda l, i, j: (i, 0)),
                  pl.BlockSpec((None, nb, tn), lambda l, i, j: (l, 0, j))],
        out_specs=pl.BlockSpec((None, tk, tn), lambda l, i, j: (l, i, j)),
        out_shape=jax.ShapeDtypeStruct((depth, K, cols), F32),
        compiler_params=_params("parallel", "parallel", "parallel"),
    )(c_all_t, dmod)


def _adamw(w, g, m, v, *, name):
    shape = w.shape
    cols = shape[-1]
    rows = int(np.prod(shape[:-1]))
    t = _tile(rows, 256, SUBLANES) if rows % SUBLANES == 0 else rows
    r2 = lambda a: a.reshape(rows, cols)

    def body(w_ref, g_ref, m_ref, v_ref, d_ref, nm_ref, nv_ref):
        gv = g_ref[...]
        nm = ADAM_B1 * m_ref[...] + (1.0 - ADAM_B1) * gv
        nv = ADAM_B2 * v_ref[...] + (1.0 - ADAM_B2) * (gv * gv)
        m_hat = nm / (1.0 - ADAM_B1 ** ADAM_STEP)
        v_hat = nv / (1.0 - ADAM_B2 ** ADAM_STEP)
        d_ref[...] = -ADAM_LR * (m_hat / (jnp.sqrt(v_hat) + ADAM_EPS) + ADAM_WD * w_ref[...])
        nm_ref[...] = nm
        nv_ref[...] = nv

    spec = pl.BlockSpec((t, cols), lambda i: (i, 0))
    sds = jax.ShapeDtypeStruct((rows, cols), F32)
    d, nm, nv = pl.pallas_call(
        body, name=name, grid=(rows // t,),
        in_specs=[spec] * 4, out_specs=[spec] * 3, out_shape=[sds] * 3,
        compiler_params=_params("parallel"),
    )(r2(w), r2(g), r2(m), r2(v))
    return d.reshape(shape), nm.reshape(shape), nv.reshape(shape)


def _my_place():
    return lax.axis_index("x"), lax.axis_index("y"), lax.axis_index("c")


def _other_chips(x, y):
    return [(1 - x, y), (x, 1 - y), (1 - x, 1 - y)]


def _all_gather8(v, *, name, with_sum=False):
    m, n = v.shape

    def body(x_ref, out_ref, *rest):
        if with_sum:
            sum_ref, send_sems, recv_sems, local_sem = rest
        else:
            send_sems, recv_sems, local_sem = rest
        x, y, c = _my_place()
        me, sibling = (x, y, c), (x, y, 1 - c)
        chips = _other_chips(x, y)

        def rows(px, py, pc):
            return out_ref.at[pl.ds((4 * px + 2 * py + pc) * m, m), :]

        def copy(k, block, to, src=None):
            return pltpu.make_async_remote_copy(
                src_ref=rows(*block) if src is None else src, dst_ref=rows(*block),
                send_sem=send_sems.at[k], recv_sem=recv_sems.at[k], device_id=to, device_id_type=MESH)

        mine = pltpu.make_async_copy(x_ref, rows(*me), local_sem)
        mine.start()
        first = [copy(0, me, sibling, src=x_ref)]
        first += [copy(1 + j, me, (*chip, c), src=x_ref) for j, chip in enumerate(chips)]
        for cp in first:
            cp.start()
        passed = [copy(4 + j, (*chip, c), sibling) for j, chip in enumerate(chips)]
        for j, chip in enumerate(chips):
            copy(1 + j, (*chip, c), me).wait_recv()
            passed[j].start()
        copy(0, sibling, me).wait_recv()
        for j, chip in enumerate(chips):
            copy(4 + j, (*chip, 1 - c), me).wait_recv()
        for cp in first + passed:
            cp.wait_send()
        mine.wait()
        if with_sum:
            acc = out_ref[pl.ds(0, m), :]
            for d in range(1, N_DEV):
                acc = acc + out_ref[pl.ds(d * m, m), :]
            sum_ref[...] = acc

    vm = pl.BlockSpec(memory_space=pltpu.VMEM)
    out_shape = [jax.ShapeDtypeStruct((N_DEV * m, n), F32)]
    if with_sum:
        out_shape.append(jax.ShapeDtypeStruct((m, n), F32))
    res = pl.pallas_call(
        body, name=name, out_shape=out_shape, in_specs=[vm], out_specs=[vm] * len(out_shape),
        scratch_shapes=[pltpu.SemaphoreType.DMA((7,)), pltpu.SemaphoreType.DMA((7,)), pltpu.SemaphoreType.DMA],
        compiler_params=pltpu.CompilerParams(vmem_limit_bytes=VMEM_LIMIT),
    )(v)
    return res if with_sum else res[0]


class _Cut(NamedTuple):
    shape: tuple
    slab: int
    half: int


IN_WIDTH = SSM_WIDTH + 3 * ATTN_WIDTH + ATTN_HEADS + 2 * D_MODEL
CUTS = dict(
    ffn_w_in=_Cut((1, D_MODEL, 2 * D_FF), 2, 1),
    ffn_w_out=_Cut((1, D_FF, D_MODEL), 1, 2),
    mix_w_in=_Cut((N_CHIPS, D_MODEL, IN_WIDTH // N_CHIPS), 0, 1),
    glu_w=_Cut((1, SSM_WIDTH, 2 * D_MODEL), 2, 1),
    attn_w_out=_Cut((1, ATTN_WIDTH, D_MODEL), 2, 1),
    mix_w_out=_Cut((1, D_MODEL, D_MODEL), 1, 2),
)
LAYER_MATS = (("ffn_w_in", 0), ("ffn_w_in", 1), ("ffn_w_out", 0), ("ffn_w_out", 1), ("mix_w_in", None),
              ("glu_w", None), ("attn_w_out", None), ("mix_w_out", None))
FIRST_MATS = (0, 2)
MIXER_MATS = (4, 5, 6, 7)
LATE_MATS = (1, 3)


def _part_shape(cut, slab=False, half=False):
    s = list(cut.shape)
    if slab:
        s[cut.slab] //= N_CHIPS
    if half:
        s[cut.half] //= 2
    return tuple(s)


def _window(ref, cut, slab=None, half=None):
    idx = [slice(None)] * len(cut.shape)
    for axis, parts, which in ((cut.slab, N_CHIPS, slab), (cut.half, 2, half)):
        if which is not None:
            width = cut.shape[axis] // parts
            idx[axis] = pl.ds(pl.multiple_of(which * width, width), width)
    return ref.at[tuple(idx)]


HBM_SPEC = pl.BlockSpec(memory_space=pltpu.HBM)


class _Exchange(NamedTuple):
    operands: list
    out_shapes: list
    sems: list
    start: object
    finish: object


def _run_exchange(ex, *, name):
    ni, no = len(ex.operands), len(ex.out_shapes)

    def body(*refs):
        parts = (refs[:ni], refs[ni:ni + no], refs[ni + no:])
        ex.start(*parts)
        ex.finish(*parts)

    return pl.pallas_call(
        body, name=name, out_shape=ex.out_shapes, in_specs=[HBM_SPEC] * ni, out_specs=[HBM_SPEC] * no,
        scratch_shapes=ex.sems,
    )(*ex.operands)


def _gather_exchange(shards, cuts):
    n = len(shards)

    def setup(s_refs, f_refs, sems):
        send_sems, recv_sems = sems
        x, y, c = _my_place()
        me, sibling, mine = (x, y, c), (x, y, 1 - c), 2 * x + y
        chips = _other_chips(x, y)

        def copy(i, k, src, dst, to):
            return pltpu.make_async_remote_copy(
                src_ref=src, dst_ref=dst, send_sem=send_sems.at[7 * i + k], recv_sem=recv_sems.at[7 * i + k],
                device_id=to, device_id_type=MESH)

        def landed(i, j, half):
            return _window(f_refs[i], cuts[i], slab=2 * chips[j][0] + chips[j][1], half=half)

        def sends():
            own = [copy(i, 6, s_refs[i], _window(f_refs[i], cuts[i], slab=mine), sibling) for i in range(n)]
            return own + [copy(i, j, _window(s_refs[i], cuts[i], half=c),
                               _window(f_refs[i], cuts[i], slab=mine, half=c), (*chips[j], c))
                          for i in range(n) for j in range(3)]

        return c, me, sibling, copy, landed, sends

    def start(s_refs, f_refs, sems):
        for cp in setup(s_refs, f_refs, sems)[-1]():
            cp.start()

    def finish(s_refs, f_refs, sems):
        c, me, sibling, copy, landed, sends = setup(s_refs, f_refs, sems)
        passed = []
        for i in range(n):
            for j in range(3):
                copy(i, j, landed(i, j, c), landed(i, j, c), me).wait_recv()
                passed.append(copy(i, 3 + j, landed(i, j, c), landed(i, j, c), sibling))
                passed[-1].start()
        for i in range(n):
            for j in range(3):
                copy(i, 3 + j, landed(i, j, 1 - c), landed(i, j, 1 - c), me).wait_recv()
        for i in range(n):
            mine_i = _window(f_refs[i], cuts[i], slab=2 * me[0] + me[1])
            copy(i, 6, mine_i, mine_i, me).wait_recv()
        for cp in sends() + passed:
            cp.wait_send()

    return _Exchange(
        list(shards), [jax.ShapeDtypeStruct(cut.shape, s.dtype) for s, cut in zip(shards, cuts)],
        [pltpu.SemaphoreType.DMA((7 * n,)), pltpu.SemaphoreType.DMA((7 * n,))], start, finish)


def _swap_exchange(mats, cuts):
    n = len(mats)

    def copies(m_refs, r_refs, sems):
        send_sems, recv_sems = sems
        x, y, c = _my_place()
        return [pltpu.make_async_remote_copy(
            src_ref=_window(m_refs[i], cuts[i], half=1 - c), dst_ref=r_refs[i], send_sem=send_sems.at[i],
            recv_sem=recv_sems.at[i], device_id=(x, y, 1 - c), device_id_type=MESH) for i in range(n)]

    def start(m_refs, r_refs, sems):
        for cp in copies(m_refs, r_refs, sems):
            cp.start()

    def finish(m_refs, r_refs, sems):
        for cp in copies(m_refs, r_refs, sems):
            cp.wait()

    return _Exchange(
        list(mats), [jax.ShapeDtypeStruct(_part_shape(cut, half=True), m.dtype) for m, cut in zip(mats, cuts)],
        [pltpu.SemaphoreType.DMA((n,)), pltpu.SemaphoreType.DMA((n,))], start, finish)


def _partials_exchange(sums, cuts):
    n = len(sums)

    def copies(s_refs, p_refs, sems):
        send_sems, recv_sems = sems
        x, y, c = _my_place()
        return [pltpu.make_async_remote_copy(
            src_ref=_window(s_refs[i], cuts[i], slab=2 * chip[0] + chip[1]), dst_ref=p_refs[i].at[j],
            send_sem=send_sems.at[3 * i + j], recv_sem=recv_sems.at[3 * i + j],
            device_id=(*chip, c), device_id_type=MESH)
            for i in range(n) for j, chip in enumerate(_other_chips(x, y))]

    def start(s_refs, p_refs, sems):
        for cp in copies(s_refs, p_refs, sems):
            cp.start()

    def finish(s_refs, p_refs, sems):
        for cp in copies(s_refs, p_refs, sems):
            cp.wait()

    return _Exchange(
        list(sums), [jax.ShapeDtypeStruct((3,) + _part_shape(cut, slab=True, half=True), s.dtype)
                     for s, cut in zip(sums, cuts)],
        [pltpu.SemaphoreType.DMA((3 * n,)), pltpu.SemaphoreType.DMA((3 * n,))], start, finish)


def _share_all(dests, cuts, places, *, name):
    names = list(dests)
    nn, n = len(names), len(places)

    def body(*refs):
        o_refs = dict(zip(names, refs[nn:2 * nn]))
        send_sems, recv_sems = refs[2 * nn:]
        x, y, c = _my_place()

        def win(i, half):
            slab_cut = _Cut(_part_shape(cuts[i], slab=True), cuts[i].slab, cuts[i].half)
            return _window(o_refs[places[i][0]].at[places[i][1]], slab_cut, half=half)

        def copy(i, half):
            return pltpu.make_async_remote_copy(
                src_ref=win(i, half), dst_ref=win(i, half), send_sem=send_sems.at[i], recv_sem=recv_sems.at[i],
                device_id=(x, y, 1 - c), device_id_type=MESH)

        for i in range(n):
            copy(i, c).start()
        for i in range(n):
            copy(i, c).wait_send()
            copy(i, 1 - c).wait_recv()

    return pl.pallas_call(
        body, name=name, out_shape=[jax.ShapeDtypeStruct(dests[k].shape, F32) for k in names],
        in_specs=[HBM_SPEC] * nn, out_specs=[HBM_SPEC] * nn,
        input_output_aliases={i: i for i in range(nn)},
        scratch_shapes=[pltpu.SemaphoreType.DMA((n,)), pltpu.SemaphoreType.DMA((n,))],
    )(*[dests[k] for k in names])


def _cut_blocks(shape):
    _, R, C = shape
    tr = _tile(R, 512, 16)
    tc = _tile(C, 2048) if C % LANES == 0 else C
    return (None, tr, tc), (shape[0], R // tr, C // tc)


def _offset_map(axis, blocks):
    def index_map(b, i, j, which):
        idx = [b, i, j]
        idx[axis] = which[0] * blocks[axis] + idx[axis]
        return tuple(idx)
    return index_map


def _add_half(mat, other, cut, c_idx, *, name):
    shape = _part_shape(cut, half=True)
    block, grid = _cut_blocks(shape)

    def body(c_ref, m_ref, o_ref, f_ref, b_ref):
        s = m_ref[...] + o_ref[...]
        f_ref[...] = s
        b_ref[...] = s.astype(BF16)

    plain = pl.BlockSpec(block, lambda b, i, j, which: (b, i, j))
    grid_spec = pltpu.PrefetchScalarGridSpec(
        num_scalar_prefetch=1, grid=grid,
        in_specs=[pl.BlockSpec(block, _offset_map(cut.half, grid)), plain], out_specs=[plain, plain])
    return pl.pallas_call(
        body, name=name, grid_spec=grid_spec,
        out_shape=[jax.ShapeDtypeStruct(shape, F32), jax.ShapeDtypeStruct(shape, BF16)],
        compiler_params=_params("parallel", "parallel", "parallel"),
    )(c_idx, mat, other)


def _sum_slab(own, parts, cut, dest, place, chip_idx, c_idx, *, name):
    shape = _part_shape(cut, slab=True, half=True)
    block, grid = _cut_blocks(shape)
    assert shape[0] == 1

    def body(k_ref, c_ref, o_ref, p_ref, dest_ref, out_ref):
        acc = o_ref[...]
        for j in range(3):
            acc = acc + p_ref[j].astype(F32)
        out_ref[...] = acc

    def own_map(b, i, j, chip, core):
        idx = [b, i, j]
        idx[cut.slab] = chip[0] * grid[cut.slab] + idx[cut.slab]
        return tuple(idx)

    def dest_map(b, i, j, chip, core):
        idx = [b, i, j]
        idx[cut.half] = core[0] * grid[cut.half] + idx[cut.half]
        return tuple(place) + tuple(idx)

    grid_spec = pltpu.PrefetchScalarGridSpec(
        num_scalar_prefetch=2, grid=grid,
        in_specs=[pl.BlockSpec(block, own_map),
                  pl.BlockSpec((3,) + block[1:], lambda b, i, j, chip, core: (0, i, j)),
                  pl.BlockSpec(memory_space=pl.ANY)],
        out_specs=pl.BlockSpec((None,) * len(place) + block, dest_map))
    return pl.pallas_call(
        body, name=name, grid_spec=grid_spec, out_shape=jax.ShapeDtypeStruct(dest.shape, F32),
        input_output_aliases={4: 0},
        compiler_params=_params("parallel", "parallel", "parallel"),
    )(chip_idx, c_idx, own, parts.reshape((3,) + shape[1:]), dest)


def _pad_rows(flat, cols=8 * LANES, align=SUBLANES):
    n = flat.shape[0]
    rows = -(-n // (cols * align)) * align
    return jnp.pad(flat, (0, rows * cols - n)).reshape(rows, cols)


def _row(v):
    return v.reshape(1, -1)


def _ffn_fwd(x, mod, g_pre, g_post, w_in, w_out, tag, exchange=None):
    sh, sc, gate = _row(mod[0]), _row(mod[1]), _row(mod[2])
    h = _prenorm(x, _row(g_pre), sc, sh, name=f"prenorm_{tag}")
    gt, up, act, *exchanged = _mm_swiglu(h, w_in, name=f"swiglu_{tag}", exchange=exchange)
    y, x_out = _mm_postnorm(act, w_out, x, _row(g_post), gate, FFN_RES, name=f"ffn_out_{tag}")
    return x_out, (x, h, gt, up, act, y), exchanged


def _ffn_bwd(dxo, saved, mod, g_pre, g_post, w_in, w_out, tag, swap=None, after_swap=None):
    x, h, gt, up, act, y = saved
    sc, gate = _row(mod[1]), _row(mod[2])
    dy, dgate, dgpost, *swapped = _postnorm_bwd(dxo, y, _row(g_post), gate, FFN_RES, name=f"postnorm_bwd_{tag}",
                                                exchange=swap)
    exchange = after_swap(swapped) if after_swap is not None else None
    dgt, dup, *exchanged = _mm_swiglu_bwd(dy, w_out, gt, up, name=f"swiglu_bwd_{tag}", exchange=exchange)
    dw_out = _mm(act, dy, ta=True, name=f"dw_out_{tag}", tm=1408, tn=1024, tk=2048)
    dh = _mm(dgt, w_in, tb=True, name=f"dh_gate_{tag}", tm=1024, tk=D_FF)
    dh = _mm(dup, w_in, tb=True, b_k0=D_FF, bias=dh, name=f"dh_up_{tag}", tm=1024, tk=D_FF)
    dw_in = _mm(h, dgt, ta=True, out_n=2 * D_FF, name=f"dw_gate_{tag}", tm=1024, tn=1408, tk=2048)
    dw_in = _mm(h, dup, ta=True, out_n=2 * D_FF, out_j0=D_FF, into=dw_in, name=f"dw_up_{tag}",
                tm=1024, tn=1408, tk=2048)
    dx, dsh, dsc, dgpre = _prenorm_bwd(x, dh, dxo, _row(g_pre), sc, name=f"prenorm_bwd_{tag}")
    dmod = jnp.stack([dsh[0], dsc[0], dgate[0]])
    return dx, dmod, dgpre[0], dgpost[0], dw_in, dw_out, exchanged


def _split_mix_w_in(w):
    u0, q0, f0, g0 = 0, SSM_WIDTH, SSM_WIDTH + 3 * ATTN_WIDTH, SSM_WIDTH + 3 * ATTN_WIDTH + ATTN_HEADS
    w_f = jnp.pad(w[:, f0:g0], ((0, 0), (0, LANES - ATTN_HEADS)))
    return w[:, u0:q0], w[:, q0:f0], w_f, w[:, g0:]


def _mixer_fwd(x, mod, g_pre, g_post, w, ssm, forget_b, tag, exchange=None):
    L = x.shape[0]
    sh, sc, gate = _row(mod[0]), _row(mod[1]), _row(mod[2])
    lam_re, lam_im, bin_re, bin_im, cout_re, cout_im, dskip = ssm
    h = _prenorm(x, _row(g_pre), sc, sh, name=f"prenorm_{tag}")
    u = _mm(h, w["u"], name=f"proj_u_{tag}")
    qkv = _mm(h, w["qkv"], out_dtype=BF16, name=f"proj_qkv_{tag}")
    f = _mm(h, w["f"], name=f"proj_f_{tag}")
    gab = _mm(h, w["gab"], name=f"proj_gab_{tag}")
    u_i = _interleave(u)
    bu_re, bu_im = _mm_bd([u_i], [bin_re, bin_im], name=f"ssm_bu_{tag}")
    s_re, s_im = _ssm_scan_fwd(bu_re, bu_im, lam_re, lam_im, name=f"ssm_scan_{tag}")
    y_ssm = _deinterleave(_mm_bd([s_re, s_im], [cout_re, cout_im], bias=u_i, bscale=dskip, name=f"ssm_y_{tag}"))
    gl = _gelu_fwd(y_ssm, name=f"gelu_{tag}")
    z = _mm(gl, w["glu"], name=f"glu_{tag}")
    fb = jnp.pad(forget_b, (0, LANES - ATTN_HEADS)).reshape(1, LANES)
    cum = _fox_cum(f, fb, name=f"fox_cum_{tag}")
    cum8 = cum[:, :ATTN_HEADS]
    cum_cols = jnp.repeat(cum8, HEAD_DIM, axis=1)
    cum_rows = jnp.pad(cum8.T.reshape(ATTN_HEADS // 2, 2, L), ((0, 0), (0, SUBLANES - 2), (0, 0)))
    attn, attn32, lse, *exchanged = _fox_fwd(qkv, cum_cols, cum_rows, name=f"fox_fwd_{tag}", exchange=exchange)
    yb = _mm(attn, w["attn_out"], name=f"attn_out_{tag}")
    merged = _merge_fwd(z, yb, gab, name=f"merge_{tag}")
    y, x_out = _mm_postnorm(merged, w["out"], x, _row(g_post), gate, 1.0, name=f"mix_out_{tag}")
    saved = (x, h, u, u_i, qkv, f, gab, s_re, s_im, y_ssm, gl, z, fb, cum_cols, cum_rows, attn, attn32, lse, yb,
             merged, y)
    return x_out, saved, exchanged


def _mixer_bwd(dxo, saved, mod, g_pre, g_post, w, ssm, tag, swap=None, after_swap=None):
    (x, h, u, u_i, qkv, f, gab, s_re, s_im, y_ssm, gl, z, fb, cum_cols, cum_rows, attn, attn32, lse, yb,
     merged, y) = saved
    L = x.shape[0]
    sc, gate = _row(mod[1]), _row(mod[2])
    lam_re, lam_im, bin_re, bin_im, cout_re, cout_im, dskip = ssm
    dy, dgate, dgpost = _postnorm_bwd(dxo, y, _row(g_post), gate, 1.0, name=f"postnorm_bwd_{tag}")
    dmerged = _mm(dy, w["out"], tb=True, name=f"dmerged_{tag}")
    dw_out = _mm(merged, dy, ta=True, name=f"dw_mix_out_{tag}", tm=1024, tn=1024)
    dz, dyb, dgab, *swapped = _merge_bwd(dmerged, z, yb, gab, name=f"merge_bwd_{tag}", exchange=swap)
    exchange = after_swap(swapped) if after_swap is not None else None
    dgl = _mm(dz, w["glu"], tb=True, name=f"dgl_{tag}", tk=2048)
    dw_glu = _mm(gl, dz, ta=True, name=f"dw_glu_{tag}", tm=512, tn=2048)
    dys, dsk, dd = _gelu_bwd(dgl, y_ssm, u, dskip, name=f"gelu_bwd_{tag}")
    dys, dsk = _interleave(dys), _interleave(dsk)
    d_re, d_im = _mm_bd([dys], [cout_re, cout_im], tb=True, name=f"ssm_ds_{tag}")
    dcout_re = _mm_bd_t(s_re, dys, SSM_BLOCKS, name=f"ssm_dc_re_{tag}")
    dcout_im = _mm_bd_t(s_im, dys, SSM_BLOCKS, name=f"ssm_dc_im_{tag}")
    g_re, g_im, dlam_re, dlam_im = _ssm_scan_bwd(d_re, d_im, s_re, s_im, lam_re, lam_im, name=f"ssm_scan_bwd_{tag}")
    du = _mm_bd([g_re, g_im], [bin_re, bin_im], tb=True, bias=dsk, out_dtype=BF16, name=f"ssm_du_{tag}")
    du = _deinterleave(du)
    dbin_re = _mm_bd_t(u_i, g_re, SSM_BLOCKS, name=f"ssm_db_re_{tag}")
    dbin_im = _mm_bd_t(u_i, g_im, SSM_BLOCKS, name=f"ssm_db_im_{tag}")
    dssm = (dlam_re, dlam_im, dbin_re, dbin_im, dcout_re, dcout_im, _row(dd[0]))
    dattn = _mm(dyb, w["attn_out"], tb=True, out_dtype=BF16, name=f"dattn_{tag}")
    dw_attn = _mm(attn, dyb, ta=True, name=f"dw_attn_out_{tag}", tm=512, tn=1024)
    stats = _fox_rowstats(dattn, attn32, lse, cum_cols, name=f"fox_rowstats_{tag}")
    dk, dv, dcum_rows, dq, drow, *exchanged = _fox_bwd(qkv, dattn, stats, cum_rows, name=f"fox_bwd_{tag}",
                                                       exchange=exchange)
    drow8 = drow.reshape(L, ATTN_HEADS // 2, LANES)[:, :, :2].reshape(L, ATTN_HEADS)
    dcum = drow8 + dcum_rows[:, :2, :].reshape(ATTN_HEADS, L).T
    dcum = jnp.pad(dcum, ((0, 0), (0, LANES - ATTN_HEADS)))
    df, dfb = _fox_cum_bwd(dcum, f, fb, name=f"fox_cum_bwd_{tag}")
    dqkv = jnp.concatenate([dq.astype(BF16), dk, dv], axis=1)
    dh = _mm(dqkv, w["qkv"], tb=True, name=f"dh_qkv_{tag}", tk=1536)
    dh = _mm(du, w["u"], tb=True, bias=dh, name=f"dh_u_{tag}")
    dh = _mm(dgab, w["gab"], tb=True, bias=dh, name=f"dh_gab_{tag}", tk=2048)
    dh = _mm(df, w["f"], tb=True, bias=dh, name=f"dh_f_{tag}")
    dw_u = _mm(h, du, ta=True, name=f"dw_u_{tag}", tm=1024, tn=512)
    dw_qkv = _mm(h, dqkv, ta=True, name=f"dw_qkv_{tag}", tm=1024, tn=1536)
    dw_f = _mm(h, df, ta=True, name=f"dw_f_{tag}", tm=1024)
    dw_gab = _mm(h, dgab, ta=True, name=f"dw_gab_{tag}", tm=1024, tn=1024)
    dw_in = jnp.concatenate([dw_u, dw_qkv, dw_f[:, :ATTN_HEADS], dw_gab], axis=1)
    dx, dsh, dsc, dgpre = _prenorm_bwd(x, dh, dxo, _row(g_pre), sc, name=f"prenorm_bwd_{tag}")
    dmod = jnp.stack([dsh[0], dsc[0], dgate[0]])
    grads = dict(mix_w_in=dw_in, glu_w=dw_glu, attn_w_out=dw_attn, mix_w_out=dw_out,
                 forget_b=dfb[0, :ATTN_HEADS])
    return dx, dmod, dgpre[0], dgpost[0], grads, dssm, exchanged


SSM_NAMES = ("ssm_a_re", "ssm_a_im", "ssm_log_dt", "ssm_b_re", "ssm_b_im", "ssm_c_re", "ssm_c_im", "ssm_d")
SMALL_NAMES = ("forget_b",) + SSM_NAMES
WEIGHT_NAMES = ("mod_w", "mod_b", "norm_pre", "norm_post", "ffn_w_in", "ffn_w_out", "mix_w_in", "forget_b") \
    + SSM_NAMES + ("glu_w", "attn_w_out", "mix_w_out")


def _layer_shards(w, l):
    return [(w[n][l] if j is None else w[n][l, j]).astype(BF16).reshape(_part_shape(CUTS[n], slab=True))
            for n, j in LAYER_MATS]


def _train_step(x, c, target, w, m, v):
    xi, yi, ci = _my_place()
    chip = 2 * xi + yi
    dev = 4 * xi + 2 * yi + ci
    mod_cols = N_SUB * 3 * D_MODEL // N_CHIPS
    norm_cols = D_MODEL // N_CHIPS

    cuts = [CUTS[n] for n, _ in LAYER_MATS]
    shards = [_layer_shards(w, l) for l in range(DEPTH)]
    full = [[None] * len(LAYER_MATS) for _ in range(DEPTH)]

    def gather(keys):
        return _gather_exchange([shards[l][i] for l, i in keys], [cuts[i] for _, i in keys])

    def store(keys, mats):
        for (l, i), a in zip(keys, mats):
            full[l][i] = a

    first = lambda l: [(l, i) for i in FIRST_MATS]
    mixer = lambda l: [(l, i) for i in MIXER_MATS]
    late = lambda l: [(l, i) for i in LATE_MATS]
    store(first(0), _run_exchange(gather(first(0)), name="gather_weights_first"))
    c_all = _all_gather8(jnp.pad(c, ((0, SUBLANES - 1), (0, 0))), name="gather_c")[::SUBLANES]
    mod_b_cols = lax.dynamic_slice_in_dim(w["mod_b"], chip * mod_cols, mod_cols, axis=1)[:, None, :]
    mod_part = _mod_partial(c_all, w["mod_w"], mod_b_cols, name="mod_partial")
    small_fwd = jnp.concatenate([mod_part.reshape(-1), w["norm_pre"].reshape(-1), w["norm_post"].reshape(-1)])
    n_mod, n_norm = mod_part.size, w["norm_pre"].size
    sf_all = _all_gather8(_pad_rows(small_fwd), name="gather_mod").reshape(N_DEV, -1)
    sf_chips = sf_all[::2]
    mod_all = jnp.concatenate(
        [sf_chips[k, :n_mod].reshape(DEPTH, N_DEV, mod_cols) for k in range(N_CHIPS)], axis=2)
    mod_mine = lax.dynamic_index_in_dim(mod_all, dev, axis=1, keepdims=False).reshape(DEPTH, N_SUB, 3, D_MODEL)
    norm_pre = jnp.concatenate(
        [sf_chips[k, n_mod:n_mod + n_norm].reshape(DEPTH, N_SUB, norm_cols) for k in range(N_CHIPS)], axis=2)
    norm_post = jnp.concatenate(
        [sf_chips[k, n_mod + n_norm:n_mod + 2 * n_norm].reshape(DEPTH, N_SUB, norm_cols) for k in range(N_CHIPS)],
        axis=2)

    saved, layer_w, ssm_prep, ssm_vjp = [], [], [], []
    h = x
    for l in range(DEPTH):
        lw = dict(ffn=[(full[l][0][0], full[l][2][0]), None])
        prep, vjp = jax.vjp(_ssm_discretize, *[w[n][l] for n in SSM_NAMES])
        layer_w.append(lw)
        ssm_prep.append(prep)
        ssm_vjp.append(vjp)
        h, s0, arrived = _ffn_fwd(h, mod_mine[l, 0], norm_pre[l, 0], norm_post[l, 0], *lw["ffn"][0], tag=f"l{l}a",
                                  exchange=gather(mixer(l)))
        store(mixer(l), arrived)
        mix_in, glu, attn_out, mix_out = full[l][4:]
        w_u, w_qkv, w_f, w_gab = _split_mix_w_in(mix_in.transpose(1, 0, 2).reshape(D_MODEL, IN_WIDTH))
        lw["mix"] = dict(u=w_u, qkv=w_qkv, f=w_f, gab=w_gab, glu=glu[0], attn_out=attn_out[0], out=mix_out[0])
        coming = late(l) + (first(l + 1) if l + 1 < DEPTH else [])
        h, s1, arrived = _mixer_fwd(h, mod_mine[l, 1], norm_pre[l, 1], norm_post[l, 1], lw["mix"], prep,
                                    w["forget_b"][l], tag=f"l{l}m", exchange=gather(coming))
        store(coming, arrived)
        lw["ffn"][1] = (full[l][1][0], full[l][3][0])
        h, s2, _ = _ffn_fwd(h, mod_mine[l, 2], norm_pre[l, 2], norm_post[l, 2], *lw["ffn"][1], tag=f"l{l}b")
        saved.append((s0, s1, s2))
    dh, loss8 = _loss_head(h, target, name="loss_head")

    c_idx = ci.reshape(1).astype(jnp.int32)
    chip_idx = chip.reshape(1).astype(jnp.int32)
    places = []
    dests = {n: lax.empty(((DEPTH,) if j is None else (DEPTH, 2)) + _part_shape(CUTS[n], slab=True), F32)
             for n, j in LAYER_MATS}
    g_small = {n: [None] * DEPTH for n in SMALL_NAMES}
    dmod, dnpre, dnpost = [], [], []
    pending = []

    def swap_of(l, idx, mats):
        cs = [cuts[i] for i in idx]
        mats = [a.reshape(cut.shape) for a, cut in zip(mats, cs)]

        def added(from_sibling):
            for k, i in enumerate(idx):
                f, b = _add_half(mats[k], from_sibling[k], cs[k], c_idx, name=f"grad_add_l{l}_{i}")
                pending.append((l, i, f, b))

        return _swap_exchange(mats, cs), added

    def partials():
        return _partials_exchange([p[3] for p in pending], [cuts[p[1]] for p in pending])

    def then_partials(added):
        def after_swap(from_sibling):
            added(from_sibling)
            return partials()
        return after_swap

    def end_reduce(parts):
        for (l, i, f, _), part in zip(pending, parts):
            n, j = LAYER_MATS[i]
            place = (l,) if j is None else (l, j)
            dests[n] = _sum_slab(f, part, cuts[i], dests[n], place, chip_idx, c_idx, name=f"grad_sum_l{l}_{i}")
            places.append((n, place))
        pending.clear()

    swap, added = None, None
    for l in reversed(range(DEPTH)):
        lw = layer_w[l]
        dh, dm2, dp2, dq2, dwin2, dwout2, _ = _ffn_bwd(
            dh, saved[l][2], mod_mine[l, 2], norm_pre[l, 2], norm_post[l, 2], *lw["ffn"][1], tag=f"l{l}b",
            swap=swap, after_swap=added)
        swap, added = swap_of(l, LATE_MATS, [dwin2, dwout2])
        dh, dm1, dp1, dq1, gmix, dssm, parts = _mixer_bwd(
            dh, saved[l][1], mod_mine[l, 1], norm_pre[l, 1], norm_post[l, 1], lw["mix"], ssm_prep[l],
            tag=f"l{l}m", swap=swap, after_swap=then_partials(added))
        end_reduce(parts)
        dmix_in = gmix["mix_w_in"].reshape(D_MODEL, N_CHIPS, IN_WIDTH // N_CHIPS).transpose(1, 0, 2)
        swap, added = swap_of(l, MIXER_MATS, [dmix_in, gmix["glu_w"], gmix["attn_w_out"], gmix["mix_w_out"]])
        dh, dm0, dp0, dq0, dwin0, dwout0, parts = _ffn_bwd(
            dh, saved[l][0], mod_mine[l, 0], norm_pre[l, 0], norm_post[l, 0], *lw["ffn"][0], tag=f"l{l}a",
            swap=swap, after_swap=then_partials(added))
        end_reduce(parts)
        swap, added = swap_of(l, FIRST_MATS, [dwin0, dwout0])
        dmod.insert(0, jnp.stack([dm0, dm1, dm2]))
        dnpre.insert(0, jnp.stack([dp0, dp1, dp2]))
        dnpost.insert(0, jnp.stack([dq0, dq1, dq2]))
        g_small["forget_b"][l] = gmix["forget_b"]
        for n, g in zip(SSM_NAMES, ssm_vjp[l](dssm)):
            g_small[n][l] = g
    added(_run_exchange(swap, name="grad_swap_last"))
    end_reduce(_run_exchange(partials(), name="grad_partials_last"))
    grad_x = dh
    g_small = {n: jnp.stack(g) for n, g in g_small.items()}

    small = [loss8[0, :1], jnp.stack(dmod).reshape(-1), jnp.stack(dnpre).reshape(-1), jnp.stack(dnpost).reshape(-1)]
    small += [g_small[n].reshape(-1) for n in SMALL_NAMES]
    sizes = [int(s.size) for s in small]
    offs = np.concatenate([[0], np.cumsum(sizes)])
    sb_all, sb_sum = _all_gather8(_pad_rows(jnp.concatenate(small)), name="gather_small_grads", with_sum=True)
    sb_sum = sb_sum.reshape(-1)
    take = lambda i: sb_sum[int(offs[i]):int(offs[i + 1])]
    loss = take(0)[0]
    grads = {"mod_b": take(1).reshape(DEPTH, N_SUB * 3 * D_MODEL)}
    dnorm_pre_full = take(2).reshape(DEPTH, N_SUB, D_MODEL)
    dnorm_post_full = take(3).reshape(DEPTH, N_SUB, D_MODEL)
    grads["norm_pre"] = lax.dynamic_slice_in_dim(dnorm_pre_full, chip * norm_cols, norm_cols, axis=2)
    grads["norm_post"] = lax.dynamic_slice_in_dim(dnorm_post_full, chip * norm_cols, norm_cols, axis=2)
    for i, n in enumerate(SMALL_NAMES):
        grads[n] = take(4 + i).reshape(w[n].shape)
    dmod_all = sb_all.reshape(N_DEV, -1)[:, int(offs[1]):int(offs[2])].reshape(N_DEV, DEPTH, N_SUB * 3 * D_MODEL)
    dmod_cols = lax.dynamic_slice_in_dim(dmod_all, chip * mod_cols, mod_cols, axis=2).transpose(1, 0, 2)
    grads["mod_w"] = _mod_wgrad(c_all.T, dmod_cols, name="mod_wgrad")

    shared = _share_all(dests, [CUTS[n] for n, _ in places], places, name="grad_share")
    for n, g in zip(dests, shared):
        grads[n] = g.reshape(w[n].shape)

    delta, new_m, new_v = {}, {}, {}
    for n in WEIGHT_NAMES:
        delta[n], new_m[n], new_v[n] = _adamw(w[n], grads[n], m[n], v[n], name=f"adamw_{n}")
    outs = [loss, grad_x[None]]
    for group in (grads, delta, new_m, new_v):
        outs += [group[n] for n in WEIGHT_NAMES]
    return tuple(outs)


def kernel(x, c, mod_w, mod_b, norm_pre, norm_post, ffn_w_in, ffn_w_out, mix_w_in, forget_b, ssm_a_re, ssm_a_im, ssm_log_dt, ssm_b_re, ssm_b_im, ssm_c_re, ssm_c_im, ssm_d, glu_w, attn_w_out, mix_w_out, loss_target, m_mod_w, m_mod_b, m_norm_pre, m_norm_post, m_ffn_w_in, m_ffn_w_out, m_mix_w_in, m_forget_b, m_ssm_a_re, m_ssm_a_im, m_ssm_log_dt, m_ssm_b_re, m_ssm_b_im, m_ssm_c_re, m_ssm_c_im, m_ssm_d, m_glu_w, m_attn_w_out, m_mix_w_out, v_mod_w, v_mod_b, v_norm_pre, v_norm_post, v_ffn_w_in, v_ffn_w_out, v_mix_w_in, v_forget_b, v_ssm_a_re, v_ssm_a_im, v_ssm_log_dt, v_ssm_b_re, v_ssm_b_im, v_ssm_c_re, v_ssm_c_im, v_ssm_d, v_glu_w, v_attn_w_out, v_mix_w_out):
    w = dict(mod_w=mod_w, mod_b=mod_b, norm_pre=norm_pre, norm_post=norm_post, ffn_w_in=ffn_w_in,
             ffn_w_out=ffn_w_out, mix_w_in=mix_w_in, forget_b=forget_b, ssm_a_re=ssm_a_re, ssm_a_im=ssm_a_im,
             ssm_log_dt=ssm_log_dt, ssm_b_re=ssm_b_re, ssm_b_im=ssm_b_im, ssm_c_re=ssm_c_re, ssm_c_im=ssm_c_im,
             ssm_d=ssm_d, glu_w=glu_w, attn_w_out=attn_w_out, mix_w_out=mix_w_out)
    m = dict(mod_w=m_mod_w, mod_b=m_mod_b, norm_pre=m_norm_pre, norm_post=m_norm_post, ffn_w_in=m_ffn_w_in,
             ffn_w_out=m_ffn_w_out, mix_w_in=m_mix_w_in, forget_b=m_forget_b, ssm_a_re=m_ssm_a_re,
             ssm_a_im=m_ssm_a_im, ssm_log_dt=m_ssm_log_dt, ssm_b_re=m_ssm_b_re, ssm_b_im=m_ssm_b_im,
             ssm_c_re=m_ssm_c_re, ssm_c_im=m_ssm_c_im, ssm_d=m_ssm_d, glu_w=m_glu_w, attn_w_out=m_attn_w_out,
             mix_w_out=m_mix_w_out)
    v = dict(mod_w=v_mod_w, mod_b=v_mod_b, norm_pre=v_norm_pre, norm_post=v_norm_post, ffn_w_in=v_ffn_w_in,
             ffn_w_out=v_ffn_w_out, mix_w_in=v_mix_w_in, forget_b=v_forget_b, ssm_a_re=v_ssm_a_re,
             ssm_a_im=v_ssm_a_im, ssm_log_dt=v_ssm_log_dt, ssm_b_re=v_ssm_b_re, ssm_b_im=v_ssm_b_im,
             ssm_c_re=v_ssm_c_re, ssm_c_im=v_ssm_c_im, ssm_d=v_ssm_d, glu_w=v_glu_w, attn_w_out=v_attn_w_out,
             mix_w_out=v_mix_w_out)
    return _train_step(x[0], c, loss_target[0], w, m, v)
```
